```python
import math
import jax, jax.numpy as jnp
from jax import lax
import numpy as np

D_MODEL = 1024
BATCH = 2
SEQ = 8192
DEPTH = 1

CHUNK = 64
D_MIX = D_MODEL
SB_HEADS = 8
SB_HEAD_DIM = 64
SB_WIDTH = SB_HEADS * SB_HEAD_DIM
LRU_WIDTH = D_MIX - SB_WIDTH
LRU_BLOCKS = 8
LRU_BLOCK_W = LRU_WIDTH // LRU_BLOCKS
CONV_W = 4
RG_C = 8.0
Q_BLOCK = 128
N_GROUPS = 4
EXPERTS_PER_GROUP = 8
N_EXPERTS = N_GROUPS * EXPERTS_PER_GROUP
TOP_K_FINE = 2
D_EXPERT = 512
EPS = 1e-6
IN_COLS = 3 * SB_WIDTH + 2 * LRU_WIDTH

kernel_name = "hymba_stickbreak_rglru_hiermoe"


def _rms(x, g):
    xf = x.astype(jnp.float32)
    y = xf * lax.rsqrt(jnp.mean(xf * xf, axis=-1, keepdims=True) + EPS)
    return (y * g.astype(jnp.float32)).astype(x.dtype)


def _stick_breaking(q, k, v):
    b, h, s, d = q.shape
    nb = s // Q_BLOCK
    scale = 1.0 / math.sqrt(d)
    qb = q.reshape(b, h, nb, Q_BLOCK, d).transpose(2, 0, 1, 3, 4)
    key_pos = jnp.arange(s)

    def one_block(args):
        q_blk, blk = args
        t_pos = blk * Q_BLOCK + jnp.arange(Q_BLOCK)
        mask = key_pos[None, :] < t_pos[:, None]
        z = jnp.einsum('bhqd,bhkd->bhqk', q_blk, k).astype(jnp.float32) * scale
        log_nb_raw = jax.nn.log_sigmoid(-z)
        log_b = z + log_nb_raw
        log_nb = jnp.where(mask, log_nb_raw, 0.0)
        excl = lax.cumsum(log_nb, axis=3, reverse=True) - log_nb
        a = jnp.where(mask, jnp.exp(log_b + excl), 0.0)
        return jnp.einsum('bhqk,bhkd->bhqd', a.astype(v.dtype), v)

    out = lax.map(one_block, (qb, jnp.arange(nb)))
    return out.transpose(1, 0, 3, 2, 4).reshape(b, s, h * d)


def _causal_dwconv(x, w, bias):
    c = x.shape[-1]
    y = lax.conv_general_dilated(
        x, w.reshape(CONV_W, 1, c).astype(x.dtype), window_strides=(1,),
        padding=[(CONV_W - 1, 0)], dimension_numbers=('NWC', 'WIO', 'NWC'),
        feature_group_count=c)
    return y + bias


def _rglru(xc, w_rgate, b_rgate, w_igate, b_igate, lam):
    b, s, c = xc.shape
    xb = xc.reshape(b, s, LRU_BLOCKS, LRU_BLOCK_W)
    r = jax.nn.sigmoid(jnp.einsum('bsnc,ncd->bsnd', xb, w_rgate).reshape(b, s, c) + b_rgate)
    i = jax.nn.sigmoid(jnp.einsum('bsnc,ncd->bsnd', xb, w_igate).reshape(b, s, c) + b_igate)
    log_a = RG_C * r.astype(jnp.float32) * jax.nn.log_sigmoid(lam.astype(jnp.float32))
    a = jnp.exp(log_a)
    mult = jnp.sqrt(-jnp.expm1(2.0 * log_a))
    u = mult * (i * xc).astype(jnp.float32)

    def combine(left, right):
        a1, b1 = left
        a2, b2 = right
        return a1 * a2, a2 * b1 + b2

    _, hseq = lax.associative_scan(combine, (a, u), axis=1)
    return hseq.astype(xc.dtype)


def _hier_moe(u, w_group, b_group, w_fine, b_fine, w_e_gate, w_e_up, w_e_down):
    b, s, d = u.shape
    n = b * s
    xt = u.reshape(n, d)
    g_logits = (xt @ w_group + b_group).astype(jnp.float32)
    g_probs = jax.nn.softmax(g_logits, axis=-1)
    g_idx = jnp.argmax(g_probs, axis=-1)
    g_p = jnp.take_along_axis(g_probs, g_idx[:, None], axis=-1)
    f_logits = (xt @ w_fine + b_fine).astype(jnp.float32).reshape(n, N_GROUPS, EXPERTS_PER_GROUP)
    f_logits = jnp.take_along_axis(f_logits, g_idx[:, None, None], axis=1)[:, 0]
    f_probs = jax.nn.softmax(f_logits, axis=-1)
    top_p, top_i = lax.top_k(f_probs, TOP_K_FINE)
    top_p = top_p / jnp.sum(top_p, axis=-1, keepdims=True)
    weights = (g_p * top_p).reshape(-1)
    expert_ids = (g_idx[:, None] * EXPERTS_PER_GROUP + top_i).reshape(-1).astype(jnp.int32)
    order = jnp.argsort(expert_ids)
    tok = order // TOP_K_FINE
    xs = xt[tok]
    sizes = jnp.bincount(expert_ids, length=N_EXPERTS).astype(jnp.int32)
    hg = lax.ragged_dot(xs, w_e_gate, sizes)
    hu = lax.ragged_dot(xs, w_e_up, sizes)
    ys = lax.ragged_dot(jax.nn.silu(hg) * hu, w_e_down, sizes)
    ys = ys * weights[order][:, None].astype(ys.dtype)
    out = jnp.zeros((n, d), ys.dtype).at[tok].add(ys)
    return out.reshape(b, s, d).astype(u.dtype)


def setup_inputs(seed: int = 0) -> dict:
    key = jax.random.key(seed)
    ks = jax.random.split(key, 24)
    f32 = jnp.float32
    nrm = lambda k, shp, fan: jax.random.normal(k, shp, f32) * (fan ** -0.5)
    u_dec = jax.random.uniform(ks[10], (LRU_WIDTH,), f32, 0.9, 0.999)
    a0 = u_dec ** (1.0 / RG_C)
    lam = jnp.log(a0) - jnp.log1p(-a0)
    return {
        "x": jax.random.normal(ks[0], (BATCH, SEQ, D_MODEL), f32),
        "norm1_g": 1.0 + 0.05 * jax.random.normal(ks[1], (D_MODEL,), f32),
        "w_in": nrm(ks[2], (D_MODEL, IN_COLS), D_MODEL),
        "conv_w": nrm(ks[3], (CONV_W, LRU_WIDTH), CONV_W),
        "conv_b": 0.02 * jax.random.normal(ks[4], (LRU_WIDTH,), f32),
        "w_rgate": nrm(ks[5], (LRU_BLOCKS, LRU_BLOCK_W, LRU_BLOCK_W), LRU_BLOCK_W),
        "b_rgate": 0.02 * jax.random.normal(ks[6], (LRU_WIDTH,), f32),
        "w_igate": nrm(ks[7], (LRU_BLOCKS, LRU_BLOCK_W, LRU_BLOCK_W), LRU_BLOCK_W),
        "b_igate": 0.02 * jax.random.normal(ks[8], (LRU_WIDTH,), f32),
        "lam": lam,
        "sb_norm_g": 1.0 + 0.05 * jax.random.normal(ks[9], (SB_WIDTH,), f32),
        "lru_norm_g": 1.0 + 0.05 * jax.random.normal(ks[11], (LRU_WIDTH,), f32),
        "w_out": nrm(ks[12], (D_MIX, D_MODEL), D_MIX),
        "norm2_g": 1.0 + 0.05 * jax.random.normal(ks[13], (D_MODEL,), f32),
        "w_group": nrm(ks[14], (D_MODEL, N_GROUPS), D_MODEL),
        "b_group": 0.01 * jax.random.normal(ks[15], (N_GROUPS,), f32),
        "w_fine": nrm(ks[16], (D_MODEL, N_EXPERTS), D_MODEL),
        "b_fine": 0.01 * jax.random.normal(ks[17], (N_EXPERTS,), f32),
        "w_e_gate": nrm(ks[18], (N_EXPERTS, D_MODEL, D_EXPERT), D_MODEL),
        "w_e_up": nrm(ks[19], (N_EXPERTS, D_MODEL, D_EXPERT), D_MODEL),
        "w_e_down": nrm(ks[20], (N_EXPERTS, D_EXPERT, D_MODEL), D_EXPERT),
        "final_g": 1.0 + 0.05 * jax.random.normal(ks[21], (D_MODEL,), f32),
    }


def reference(x, norm1_g, w_in, conv_w, conv_b, w_rgate, b_rgate, w_igate, b_igate, lam,
              sb_norm_g, lru_norm_g, w_out, norm2_g, w_group, b_group, w_fine, b_fine,
              w_e_gate, w_e_up, w_e_down, final_g):
    b, s, _ = x.shape
    h = x
    for _layer in range(DEPTH):
        u = _rms(h, norm1_g)
        proj = u @ w_in
        q, k, v, x_lru, g_lru = jnp.split(
            proj, [SB_WIDTH, 2 * SB_WIDTH, 3 * SB_WIDTH, 3 * SB_WIDTH + LRU_WIDTH], axis=-1)
        to_heads = lambda t: t.reshape(b, s, SB_HEADS, SB_HEAD_DIM).transpose(0, 2, 1, 3)
        out_sb = _stick_breaking(to_heads(q), to_heads(k), to_heads(v))
        xc = _causal_dwconv(x_lru, conv_w, conv_b)
        out_lru = _rglru(xc, w_rgate, b_rgate, w_igate, b_igate, lam) * jax.nn.gelu(g_lru, approximate=True)
        mixed = jnp.concatenate([_rms(out_sb, sb_norm_g), _rms(out_lru, lru_norm_g)], axis=-1)
        h = h + mixed @ w_out
        h = h + _hier_moe(_rms(h, norm2_g), w_group, b_group, w_fine, b_fine,
                          w_e_gate, w_e_up, w_e_down)
    return _rms(h, final_g)
```

```python
import functools
import math

import jax
import jax.numpy as jnp
from jax import lax
from jax.experimental import pallas as pl
from jax.experimental.pallas import tpu as pltpu

F32 = jnp.float32
BF16 = jnp.bfloat16

EPS = 1e-6
HEAD_DIM = 64
HEADS_PER_BLOCK = 2
LANES = 128
SUBLANES = 8
CONV_W = 4
RG_C = 8.0
N_GROUPS = 4
EXPERTS_PER_GROUP = 8
N_EXPERTS = N_GROUPS * EXPERTS_PER_GROUP
ROUTER_ROWS = SUBLANES * (1 + N_GROUPS)
NEG_BIG = -1e30
ATTN_STOP = -104.0

VMEM_LIMIT = 56 * 1024 * 1024


def _cparams(sem):
    return pltpu.CompilerParams(dimension_semantics=sem, vmem_limit_bytes=VMEM_LIMIT)


def _rms_f32(x, g):
    return x * lax.rsqrt(jnp.mean(x * x, axis=-1, keepdims=True) + EPS) * g


def _in_proj_kernel(x_ref, g_ref, w_ref, q_ref, k_ref, v_ref, xl_ref, gl_ref, *, width, q_scale):
    u = _rms_f32(x_ref[...], g_ref[...]).astype(BF16)
    for c, o_ref in enumerate((q_ref, k_ref, v_ref, xl_ref, gl_ref)):
        p = jnp.dot(u, w_ref[:, c * width:(c + 1) * width], preferred_element_type=F32)
        if c == 0:
            p = p * q_scale
        o_ref[...] = p.astype(o_ref.dtype)


def _in_proj(x2, g, w_bf, width, tm):
    n, d = x2.shape
    row = lambda i: (i, 0)
    out_bf = jax.ShapeDtypeStruct((n, width), BF16)
    out_f = jax.ShapeDtypeStruct((n, width), F32)
    return pl.pallas_call(
        functools.partial(_in_proj_kernel, width=width, q_scale=1.0 / math.sqrt(HEAD_DIM)),
        grid=(n // tm,),
        in_specs=[pl.BlockSpec((tm, d), row),
                  pl.BlockSpec((1, d), lambda i: (0, 0)),
                  pl.BlockSpec(w_bf.shape, lambda i: (0, 0))],
        out_specs=[pl.BlockSpec((tm, width), row)] * 5,
        out_shape=[out_bf, out_bf, out_bf, out_f, out_f],
        compiler_params=_cparams(("arbitrary",)),
        name="in_proj",
    )(x2, g, w_bf)


def _attn_kernel(q_ref, k_ref, v_ref, o_ref, acc_ref, carry_ref, *, blk):
    i = pl.program_id(2)
    lane = lax.broadcasted_iota(jnp.int32, (1, LANES), 1)
    r_id = lax.broadcasted_iota(jnp.int32, (blk, blk), 0)
    c_id = lax.broadcasted_iota(jnp.int32, (blk, blk), 1)
    later = (r_id > c_id).astype(BF16)
    causal = c_id < r_id
    q = q_ref[...]

    def block(qh, j, masked):
        start = pl.multiple_of(j * blk, blk)
        kj = k_ref[pl.ds(start, blk), :]
        vj = v_ref[pl.ds(start, blk), :]
        z = lax.dot_general(qh, kj, (((1,), (1,)), ((), ())), preferred_element_type=F32)
        softplus = jnp.maximum(z, 0.0) + jnp.log1p(jnp.exp(-jnp.abs(z)))
        log_nb = -softplus
        log_b = z - softplus
        if masked:
            log_nb = jnp.where(causal, log_nb, 0.0)
        hi = log_nb.astype(BF16)
        lo = (log_nb - hi.astype(F32)).astype(BF16)
        excl = (jnp.dot(hi, later, preferred_element_type=F32)
                + jnp.dot(lo, later, preferred_element_type=F32))
        carry = carry_ref[...]
        a = jnp.exp(log_b + excl + carry)
        if masked:
            a = jnp.where(causal, a, 0.0)
        acc_ref[...] += jnp.dot(a.astype(BF16), vj, preferred_element_type=F32)
        carry = carry + jnp.sum(log_nb, axis=-1, keepdims=True)
        carry_ref[...] = carry
        return jnp.max(carry)

    outs = []
    for h in range(HEADS_PER_BLOCK):
        in_head = (lane >= h * HEAD_DIM) & (lane < (h + 1) * HEAD_DIM)
        qh = jnp.where(in_head, q, jnp.zeros_like(q))
        acc_ref[...] = jnp.zeros_like(acc_ref)
        carry_ref[...] = jnp.zeros_like(carry_ref)
        cmax = block(qh, i, True)

        def cond(state):
            j, cmax = state
            return (j >= 0) & (cmax >= ATTN_STOP)

        def body(state, qh=qh):
            j, _ = state
            return j - 1, block(qh, j, False)

        lax.while_loop(cond, body, (i - 1, cmax))
        outs.append(acc_ref[...])
    o_ref[...] = jnp.where(lane < HEAD_DIM, outs[0], outs[1])


def _attention(q, k, v, batch, seq, blk):
    n, width = q.shape
    nq = seq // blk
    nhp = width // LANES
    qmap = lambda b, hp, i: (b * nq + i, hp)
    kvmap = lambda b, hp, i: (b, hp)
    return pl.pallas_call(
        functools.partial(_attn_kernel, blk=blk),
        grid=(batch, nhp, nq),
        in_specs=[pl.BlockSpec((blk, LANES), qmap),
                  pl.BlockSpec((seq, LANES), kvmap),
                  pl.BlockSpec((seq, LANES), kvmap)],
        out_specs=pl.BlockSpec((blk, LANES), qmap),
        out_shape=jax.ShapeDtypeStruct((n, width), F32),
        scratch_shapes=[pltpu.VMEM((blk, LANES), F32), pltpu.VMEM((blk, 1), F32)],
        compiler_params=_cparams(("arbitrary", "arbitrary", "arbitrary")),
        name="attn",
    )(q, k, v)


def _gelu_tanh(x):
    return 0.5 * x * (1.0 + jnp.tanh(math.sqrt(2.0 / math.pi) * (x + 0.044715 * (x * x * x))))


def _lru_kernel(xl_ref, gl_ref, cw_ref, cb_ref, wr_ref, br_ref, wi_ref, bi_ref, lam_ref, o_ref,
                xbuf, a_buf, u_buf, hp_buf, h_ref, *, ts, gate_w):
    t = pl.program_id(1)
    width = xl_ref.shape[1]

    @pl.when(t == 0)
    def _():
        xbuf[0:SUBLANES, :] = jnp.zeros((SUBLANES, width), F32)
        h_ref[...] = jnp.zeros_like(h_ref)

    xbuf[SUBLANES:SUBLANES + ts, :] = xl_ref[...]
    xc = cb_ref[...]
    for w in range(CONV_W):
        xc = xc + xbuf[pl.ds(SUBLANES - (CONV_W - 1) + w, ts), :] * cw_ref[w:w + 1, :]
    xbuf[0:SUBLANES, :] = xbuf[ts:ts + SUBLANES, :]

    xcb = xc.astype(BF16)
    r_parts, i_parts = [], []
    for c in range(width // gate_w):
        xs = xcb[:, c * gate_w:(c + 1) * gate_w]
        r_parts.append(jnp.dot(xs, wr_ref[c], preferred_element_type=F32))
        i_parts.append(jnp.dot(xs, wi_ref[c], preferred_element_type=F32))
    r = jax.nn.sigmoid(jnp.concatenate(r_parts, axis=-1) + br_ref[...])
    ig = jax.nn.sigmoid(jnp.concatenate(i_parts, axis=-1) + bi_ref[...])
    lam = lam_ref[...]
    log_sig_lam = -(jnp.maximum(-lam, 0.0) + jnp.log1p(jnp.exp(-jnp.abs(lam))))
    log_a = RG_C * r * log_sig_lam
    a = jnp.exp(log_a)
    th = jnp.tanh(log_a)
    u = jnp.sqrt(-2.0 * th / (1.0 - th)) * (ig * xc)

    sub = lax.broadcasted_iota(jnp.int32, (ts, width), 0) % SUBLANES
    for d in (1, 2, 4):
        keep = sub >= d
        a_prev = jnp.where(keep, pltpu.roll(a, d, axis=0), 1.0)
        u_prev = jnp.where(keep, pltpu.roll(u, d, axis=0), 0.0)
        u = a * u_prev + u
        a = a * a_prev
    a_buf[...] = a
    u_buf[...] = u

    def group(g, h):
        base = pl.multiple_of(g * SUBLANES, SUBLANES)
        hp_buf[pl.ds(base, SUBLANES), :] = jnp.broadcast_to(h, (SUBLANES, width))
        last = base + SUBLANES - 1
        return a_buf[pl.ds(last, 1), :] * h + u_buf[pl.ds(last, 1), :]

    h_ref[...] = lax.fori_loop(0, ts // SUBLANES, group, h_ref[...])
    hseq = u_buf[...] + a_buf[...] * hp_buf[...]
    o_ref[...] = hseq * _gelu_tanh(gl_ref[...])


def _lru(xl, gl, conv_w, conv_b, wr_bd, br, wi_bd, bi, lam, batch, seq, ts):
    n, width = xl.shape
    nt = seq // ts
    gate_w = wr_bd.shape[-1]
    row = lambda b, t: (b * nt + t, 0)
    const2 = lambda b, t: (0, 0)
    const3 = lambda b, t: (0, 0, 0)
    vec = pl.BlockSpec((1, width), const2)
    return pl.pallas_call(
        functools.partial(_lru_kernel, ts=ts, gate_w=gate_w),
        grid=(batch, nt),
        in_specs=[pl.BlockSpec((ts, width), row), pl.BlockSpec((ts, width), row),
                  pl.BlockSpec((CONV_W, width), const2), vec,
                  pl.BlockSpec(wr_bd.shape, const3), vec,
                  pl.BlockSpec(wi_bd.shape, const3), vec, vec],
        out_specs=pl.BlockSpec((ts, width), row),
        out_shape=jax.ShapeDtypeStruct((n, width), F32),
        scratch_shapes=[pltpu.VMEM((ts + SUBLANES, width), F32),
                        pltpu.VMEM((ts, width), F32), pltpu.VMEM((ts, width), F32),
                        pltpu.VMEM((ts, width), F32), pltpu.VMEM((1, width), F32)],
        compiler_params=_cparams(("arbitrary", "arbitrary")),
        name="lru",
    )(xl, gl, conv_w, conv_b, wr_bd, br, wi_bd, bi, lam)


def _mix_route_kernel(sb_ref, lru_ref, x_ref, sbg_ref, lrug_ref, wo_ref, n2g_ref,
                      wrh_ref, wrl_ref, rb_ref,
                      h_ref, u2_ref, re_ref, rwt_ref, cnt_ref,
                      before_ref, count_ref, *, tm):
    step = pl.program_id(0)
    half = sb_ref.shape[1]

    @pl.when(step == 0)
    def _():
        r_id = lax.broadcasted_iota(jnp.int32, (tm, tm), 0)
        c_id = lax.broadcasted_iota(jnp.int32, (tm, tm), 1)
        before_ref[...] = (r_id < c_id).astype(BF16)
        count_ref[...] = jnp.zeros_like(count_ref)

    m_sb = _rms_f32(sb_ref[...], sbg_ref[...]).astype(BF16)
    m_lru = _rms_f32(lru_ref[...], lrug_ref[...]).astype(BF16)
    h = (x_ref[...]
         + jnp.dot(m_sb, wo_ref[0:half, :], preferred_element_type=F32)
         + jnp.dot(m_lru, wo_ref[half:2 * half, :], preferred_element_type=F32))
    h_ref[...] = h
    u2 = _rms_f32(h, n2g_ref[...])
    u2_ref[...] = u2

    u_hi = u2.astype(BF16)
    u_lo = (u2 - u_hi.astype(F32)).astype(BF16)
    logits = (jnp.dot(u_hi, wrh_ref[...], preferred_element_type=F32)
              + jnp.dot(u_lo, wrh_ref[...], preferred_element_type=F32)
              + jnp.dot(u_hi, wrl_ref[...], preferred_element_type=F32))
    lt = logits.T + rb_ref[...]

    sub = lax.broadcasted_iota(jnp.int32, (SUBLANES, tm), 0)

    def top1(x):
        m = jnp.max(x, axis=0, keepdims=True)
        idx = jnp.min(jnp.where(x == m, sub, SUBLANES), axis=0, keepdims=True)
        return m, idx

    grp = lt[0:SUBLANES, :]
    g_max, g_idx = top1(grp)
    g_p = 1.0 / jnp.sum(jnp.exp(grp - g_max), axis=0, keepdims=True)
    fine = lt[SUBLANES:2 * SUBLANES, :]
    for g in range(1, N_GROUPS):
        fine = jnp.where(g_idx == g, lt[(g + 1) * SUBLANES:(g + 2) * SUBLANES, :], fine)
    m1, i1 = top1(fine)
    m2, i2 = top1(jnp.where(sub == i1, -jnp.inf, fine))
    e2 = jnp.exp(m2 - m1)
    p1 = 1.0 / (1.0 + e2)
    w1 = g_p * p1
    w2 = g_p * (e2 * p1)
    x1 = g_idx * EXPERTS_PER_GROUP + i1
    x2 = g_idx * EXPERTS_PER_GROUP + i2

    eid = lax.broadcasted_iota(jnp.int32, (N_EXPERTS, tm), 0)
    oh1 = jnp.where(eid == x1, 1.0, 0.0)
    oh2 = jnp.where(eid == x2, 1.0, 0.0)
    pre1 = jnp.dot(oh1.astype(BF16), before_ref[...], preferred_element_type=F32)
    pre2 = jnp.dot(oh2.astype(BF16), before_ref[...], preferred_element_type=F32)
    cnt1 = jnp.sum(oh1, axis=1, keepdims=True)
    cnt2 = jnp.sum(oh2, axis=1, keepdims=True)
    base = count_ref[...]
    rank1 = jnp.sum(oh1 * (pre1 + base), axis=0, keepdims=True)
    rank2 = jnp.sum(oh2 * (pre2 + (base + cnt1)), axis=0, keepdims=True)
    total = base + cnt1 + cnt2
    count_ref[...] = total
    cnt_ref[...] = jnp.broadcast_to(total, cnt_ref.shape).astype(jnp.int32)

    zrow = jnp.zeros((SUBLANES - 4, tm), jnp.int32)
    re_ref[...] = jnp.concatenate(
        [x1, x2, rank1.astype(jnp.int32), rank2.astype(jnp.int32), zrow], axis=0)
    wt = jnp.concatenate([w1, w2, jnp.zeros((LANES - 2, tm), F32)], axis=0)
    rwt_ref[...] = wt.T


def _mix_route(sb, lru, x2, sbg, lrug, wo_bf, n2g, wr_hi, wr_lo, rbias, tm):
    n, d = x2.shape
    half = sb.shape[1]
    row = lambda i: (i, 0)
    const = lambda i: (0, 0)
    return pl.pallas_call(
        functools.partial(_mix_route_kernel, tm=tm),
        grid=(n // tm,),
        in_specs=[pl.BlockSpec((tm, half), row), pl.BlockSpec((tm, half), row),
                  pl.BlockSpec((tm, d), row),
                  pl.BlockSpec((1, half), const), pl.BlockSpec((1, half), const),
                  pl.BlockSpec(wo_bf.shape, const), pl.BlockSpec((1, d), const),
                  pl.BlockSpec(wr_hi.shape, const), pl.BlockSpec(wr_lo.shape, const),
                  pl.BlockSpec(rbias.shape, const)],
        out_specs=[pl.BlockSpec((tm, d), row), pl.BlockSpec((tm, d), row),
                   pl.BlockSpec((SUBLANES, tm), lambda i: (0, i)),
                   pl.BlockSpec((tm, LANES), row),
                   pl.BlockSpec((N_EXPERTS, LANES), const)],
        out_shape=[jax.ShapeDtypeStruct((n, d), F32), jax.ShapeDtypeStruct((n, d), F32),
                   jax.ShapeDtypeStruct((SUBLANES, n), jnp.int32),
                   jax.ShapeDtypeStruct((n, LANES), F32),
                   jax.ShapeDtypeStruct((N_EXPERTS, LANES), jnp.int32)],
        scratch_shapes=[pltpu.VMEM((tm, tm), BF16), pltpu.VMEM((N_EXPERTS, 1), F32)],
        compiler_params=_cparams(("arbitrary",)),
        name="mix_route",
    )(sb, lru, x2, sbg, lrug, wo_bf, n2g, wr_hi, wr_lo, rbias)


def _dispatch_kernel(off_ref, re_ref, u2_ref, xs_in_ref, xs_ref, sem, *, td):
    del xs_in_ref

    def row_copy(t, slot):
        pos = off_ref[re_ref[slot, t]] + re_ref[2 + slot, t]
        return pltpu.make_async_copy(u2_ref.at[pl.ds(t, 1)], xs_ref.at[pl.ds(pos, 1)], sem)

    def issue(t, c):
        row_copy(t, 0).start()
        row_copy(t, 1).start()
        return c

    lax.fori_loop(0, td, issue, 0)
    for _ in range(2):
        pltpu.make_async_copy(u2_ref, xs_ref.at[pl.ds(0, td)], sem).wait()


def _dispatch(off, re, u2, xs_zero, td):
    n, d = u2.shape
    return pl.pallas_call(
        functools.partial(_dispatch_kernel, td=td),
        grid_spec=pltpu.PrefetchScalarGridSpec(
            num_scalar_prefetch=1,
            grid=(n // td,),
            in_specs=[pl.BlockSpec((SUBLANES, td), lambda i, off: (0, i), memory_space=pltpu.SMEM),
                      pl.BlockSpec((td, d), lambda i, off: (i, 0)),
                      pl.BlockSpec(memory_space=pl.ANY)],
            out_specs=pl.BlockSpec(memory_space=pl.ANY),
            scratch_shapes=[pltpu.SemaphoreType.DMA(())]),
        out_shape=jax.ShapeDtypeStruct(xs_zero.shape, xs_zero.dtype),
        input_output_aliases={3: 0},
        compiler_params=_cparams(("arbitrary",)),
        name="dispatch",
    )(off, re, u2, xs_zero)


def _experts_kernel(te_ref, nused_ref, xs_ref, wg_ref, wu_ref, wd_ref, ys_ref,
                    wg_bf, wu_bf, wd_bf):
    t = pl.program_id(0)
    prev = te_ref[jnp.maximum(t - 1, 0)]
    used = t < nused_ref[0]

    @pl.when(used & ((t == 0) | (te_ref[t] != prev)))
    def _():
        wg_bf[...] = wg_ref[0].astype(BF16)
        wu_bf[...] = wu_ref[0].astype(BF16)
        wd_bf[...] = wd_ref[0].astype(BF16)

    @pl.when(used)
    def _():
        x = xs_ref[...].astype(BF16)
        hg = jnp.dot(x, wg_bf[...], preferred_element_type=F32)
        hu = jnp.dot(x, wu_bf[...], preferred_element_type=F32)
        act = (hg * jax.nn.sigmoid(hg) * hu).astype(BF16)
        ys_ref[...] = jnp.dot(act, wd_bf[...], preferred_element_type=F32)

    @pl.when(jnp.logical_not(used))
    def _():
        ys_ref[...] = jnp.zeros_like(ys_ref)


def _experts(tile_expert, n_used, xs, wg, wu, wd, tme):
    p, d = xs.shape
    de = wg.shape[-1]
    nt = p // tme

    def xmap(t, te, nu):
        return (jnp.minimum(t, nu[0] - 1), 0)

    wmap = lambda t, te, nu: (te[t], 0, 0)
    return pl.pallas_call(
        _experts_kernel,
        grid_spec=pltpu.PrefetchScalarGridSpec(
            num_scalar_prefetch=2,
            grid=(nt,),
            in_specs=[pl.BlockSpec((tme, d), xmap),
                      pl.BlockSpec((1, d, de), wmap), pl.BlockSpec((1, d, de), wmap),
                      pl.BlockSpec((1, de, d), wmap)],
            out_specs=pl.BlockSpec((tme, d), lambda t, te, nu: (t, 0)),
            scratch_shapes=[pltpu.VMEM((d, de), BF16), pltpu.VMEM((d, de), BF16),
                            pltpu.VMEM((de, d), BF16)]),
        out_shape=jax.ShapeDtypeStruct((p, d), F32),
        compiler_params=_cparams(("arbitrary",)),
        name="experts",
    )(tile_expert, n_used, xs, wg, wu, wd)


def _combine_kernel(off_ref, re_ref, rwt_ref, h_ref, fg_ref, ys_ref, y_ref, g1, g2, sem, *, tc):
    def row_copy(t, slot, dst):
        pos = off_ref[re_ref[slot, t]] + re_ref[2 + slot, t]
        return pltpu.make_async_copy(ys_ref.at[pl.ds(pos, 1)], dst.at[pl.ds(t, 1)], sem)

    def issue(t, c):
        row_copy(t, 0, g1).start()
        row_copy(t, 1, g2).start()
        return c

    lax.fori_loop(0, tc, issue, 0)
    for dst in (g1, g2):
        pltpu.make_async_copy(ys_ref.at[pl.ds(0, tc)], dst, sem).wait()
    w = rwt_ref[...]
    out = h_ref[...] + (w[:, 0:1] * g1[...] + w[:, 1:2] * g2[...])
    y_ref[...] = _rms_f32(out, fg_ref[...])


def _combine(off, re, rwt, h, final_g, ys, tc):
    n, d = h.shape
    return pl.pallas_call(
        functools.partial(_combine_kernel, tc=tc),
        grid_spec=pltpu.PrefetchScalarGridSpec(
            num_scalar_prefetch=1,
            grid=(n // tc,),
            in_specs=[pl.BlockSpec((SUBLANES, tc), lambda i, off: (0, i), memory_space=pltpu.SMEM),
                      pl.BlockSpec((tc, LANES), lambda i, off: (i, 0)),
                      pl.BlockSpec((tc, d), lambda i, off: (i, 0)),
                      pl.BlockSpec((1, d), lambda i, off: (0, 0)),
                      pl.BlockSpec(memory_space=pl.ANY)],
            out_specs=pl.BlockSpec((tc, d), lambda i, off: (i, 0)),
            scratch_shapes=[pltpu.VMEM((tc, d), F32), pltpu.VMEM((tc, d), F32),
                            pltpu.SemaphoreType.DMA(())]),
        out_shape=jax.ShapeDtypeStruct((n, d), F32),
        compiler_params=_cparams(("arbitrary",)),
        name="combine",
    )(off, re, rwt, h, final_g, ys)


def _block_diag(w, per):
    nb, c, _ = w.shape
    eye = jnp.eye(per, dtype=w.dtype)
    wg = w.reshape(nb // per, per, c, c)
    return jnp.einsum("gpij,pq->gpiqj", wg, eye).reshape(nb // per, per * c, per * c)


def _router_tables(w_group, b_group, w_fine, b_fine):
    d = w_group.shape[0]
    w = jnp.zeros((d, LANES), F32)
    w = w.at[:, 0:N_GROUPS].set(w_group).at[:, SUBLANES:SUBLANES + N_EXPERTS].set(w_fine)
    b = jnp.full((LANES,), NEG_BIG, F32)
    b = b.at[0:N_GROUPS].set(b_group).at[SUBLANES:SUBLANES + N_EXPERTS].set(b_fine)
    w_hi = w.astype(BF16)
    w_lo = (w - w_hi.astype(F32)).astype(BF16)
    return w_hi, w_lo, b.reshape(LANES, 1)


def kernel(x, norm1_g, w_in, conv_w, conv_b, w_rgate, b_rgate, w_igate, b_igate, lam, sb_norm_g,
           lru_norm_g, w_out, norm2_g, w_group, b_group, w_fine, b_fine, w_e_gate, w_e_up,
           w_e_down, final_g):
    batch, seq, d = x.shape
    n = batch * seq
    width = w_in.shape[1] // 5
    tm = min(512, seq)
    blk = min(256, seq)
    tme = 256
    gate_per = 256 // w_rgate.shape[1]

    x2 = x.reshape(n, d)
    vec = lambda a: a.reshape(1, -1)

    q, k, v, xl, gl = _in_proj(x2, vec(norm1_g), w_in.astype(BF16), width, tm)
    out_sb = _attention(q, k, v, batch, seq, blk)
    out_lru = _lru(xl, gl, conv_w, vec(conv_b),
                   _block_diag(w_rgate, gate_per).astype(BF16), vec(b_rgate),
                   _block_diag(w_igate, gate_per).astype(BF16), vec(b_igate),
                   vec(lam), batch, seq, min(256, seq))

    wr_hi, wr_lo, rbias = _router_tables(w_group, b_group, w_fine, b_fine)
    h, u2, re, rwt, cnt = _mix_route(out_sb, out_lru, x2, vec(sb_norm_g), vec(lru_norm_g),
                                     w_out.astype(BF16), vec(norm2_g), wr_hi, wr_lo, rbias, tm)

    counts = cnt[:, 0]
    padded = ((counts + tme - 1) // tme) * tme
    ends = jnp.cumsum(padded)
    off = (ends - padded).astype(jnp.int32)
    p_rows = 2 * n + N_EXPERTS * tme
    tile_start = jnp.arange(p_rows // tme, dtype=jnp.int32) * tme
    tile_expert = jnp.minimum(jnp.sum(tile_start[:, None] >= ends[None, :], axis=1),
                              N_EXPERTS - 1).astype(jnp.int32)
    n_used = (ends[-1:] // tme).astype(jnp.int32)

    xs = _dispatch(off, re, u2, jnp.zeros((p_rows, d), F32), tm)
    ys = _experts(tile_expert, n_used, xs, w_e_gate, w_e_up, w_e_down, tme)
    y = _combine(off, re, rwt, h, vec(final_g), ys, tm)
    return y.reshape(batch, seq, d)
```

```python
import functools
import math

import jax
import jax.numpy as jnp
from jax import lax
from jax.experimental import pallas as pl
from jax.experimental.pallas import tpu as pltpu

F32 = jnp.float32
BF16 = jnp.bfloat16

EPS = 1e-6
HEAD_DIM = 64
HEADS_PER_BLOCK = 2
LANES = 128
SUBLANES = 8
CONV_W = 4
RG_C = 8.0
N_GROUPS = 4
EXPERTS_PER_GROUP = 8
N_EXPERTS = N_GROUPS * EXPERTS_PER_GROUP
NEG_BIG = -1e30
ATTN_STOP = -104.0

VMEM_LIMIT = 56 * 1024 * 1024


def _cparams(sem):
    return pltpu.CompilerParams(dimension_semantics=sem, vmem_limit_bytes=VMEM_LIMIT)


def _rms_f32(x, g):
    return x * lax.rsqrt(jnp.mean(x * x, axis=-1, keepdims=True) + EPS) * g


def _in_proj_kernel(x_ref, g_ref, w_ref, q_ref, k_ref, v_ref, xl_ref, gl_ref, *, width, q_scale):
    u = _rms_f32(x_ref[...], g_ref[...]).astype(BF16)
    for c, o_ref in enumerate((q_ref, k_ref, v_ref, xl_ref, gl_ref)):
        p = jnp.dot(u, w_ref[:, c * width:(c + 1) * width], preferred_element_type=F32)
        if c == 0:
            p = p * q_scale
        o_ref[...] = p.astype(o_ref.dtype)


def _in_proj(x2, g, w_bf, width, tm):
    n, d = x2.shape
    row = lambda i: (i, 0)
    out_bf = jax.ShapeDtypeStruct((n, width), BF16)
    out_f = jax.ShapeDtypeStruct((n, width), F32)
    return pl.pallas_call(
        functools.partial(_in_proj_kernel, width=width, q_scale=1.0 / math.sqrt(HEAD_DIM)),
        grid=(n // tm,),
        in_specs=[pl.BlockSpec((tm, d), row),
                  pl.BlockSpec((1, d), lambda i: (0, 0)),
                  pl.BlockSpec(w_bf.shape, lambda i: (0, 0))],
        out_specs=[pl.BlockSpec((tm, width), row)] * 5,
        out_shape=[out_bf, out_bf, out_bf, out_f, out_f],
        compiler_params=_cparams(("arbitrary",)),
        name="in_proj",
    )(x2, g, w_bf)


def _attn_kernel(q_ref, k_ref, v_ref, o_ref, acc_ref, carry_ref, *, blk):
    i = pl.program_id(2)
    lane = lax.broadcasted_iota(jnp.int32, (1, LANES), 1)
    r_id = lax.broadcasted_iota(jnp.int32, (blk, blk), 0)
    c_id = lax.broadcasted_iota(jnp.int32, (blk, blk), 1)
    later = (r_id > c_id).astype(BF16)
    causal = c_id < r_id
    q = q_ref[...]

    def block(qh, j, masked):
        start = pl.multiple_of(j * blk, blk)
        kj = k_ref[pl.ds(start, blk), :]
        vj = v_ref[pl.ds(start, blk), :]
        z = lax.dot_general(qh, kj, (((1,), (1,)), ((), ())), preferred_element_type=F32)
        softplus = jnp.maximum(z, 0.0) + jnp.log1p(jnp.exp(-jnp.abs(z)))
        log_nb = -softplus
        log_b = z - softplus
        if masked:
            log_nb = jnp.where(causal, log_nb, 0.0)
        hi = log_nb.astype(BF16)
        lo = (log_nb - hi.astype(F32)).astype(BF16)
        excl = (jnp.dot(hi, later, preferred_element_type=F32)
                + jnp.dot(lo, later, preferred_element_type=F32))
        carry = carry_ref[...]
        a = jnp.exp(log_b + excl + carry)
        if masked:
            a = jnp.where(causal, a, 0.0)
        acc_ref[...] += jnp.dot(a.astype(BF16), vj, preferred_element_type=F32)
        carry = carry + jnp.sum(log_nb, axis=-1, keepdims=True)
        carry_ref[...] = carry
        return jnp.max(carry)

    outs = []
    for h in range(HEADS_PER_BLOCK):
        in_head = (lane >= h * HEAD_DIM) & (lane < (h + 1) * HEAD_DIM)
        qh = jnp.where(in_head, q, jnp.zeros_like(q))
        acc_ref[...] = jnp.zeros_like(acc_ref)
        carry_ref[...] = jnp.zeros_like(carry_ref)
        cmax = block(qh, i, True)

        def cond(state):
            j, cmax = state
            return (j >= 0) & (cmax >= ATTN_STOP)

        def body(state, qh=qh):
            j, _ = state
            return j - 1, block(qh, j, False)

        lax.while_loop(cond, body, (i - 1, cmax))
        outs.append(acc_ref[...])
    o_ref[...] = jnp.where(lane < HEAD_DIM, outs[0], outs[1])


def _attention(q, k, v, batch, seq, blk):
    n, width = q.shape
    nq = seq // blk
    nhp = width // LANES
    qmap = lambda b, hp, i: (b * nq + i, hp)
    kvmap = lambda b, hp, i: (b, hp)
    return pl.pallas_call(
        functools.partial(_attn_kernel, blk=blk),
        grid=(batch, nhp, nq),
        in_specs=[pl.BlockSpec((blk, LANES), qmap),
                  pl.BlockSpec((seq, LANES), kvmap),
                  pl.BlockSpec((seq, LANES), kvmap)],
        out_specs=pl.BlockSpec((blk, LANES), qmap),
        out_shape=jax.ShapeDtypeStruct((n, width), F32),
        scratch_shapes=[pltpu.VMEM((blk, LANES), F32), pltpu.VMEM((blk, 1), F32)],
        compiler_params=_cparams(("arbitrary", "arbitrary", "arbitrary")),
        name="attn",
    )(q, k, v)


def _gelu_tanh(x):
    return 0.5 * x * (1.0 + jnp.tanh(math.sqrt(2.0 / math.pi) * (x + 0.044715 * (x * x * x))))


def _lru_kernel(xl_ref, gl_ref, cw_ref, cb_ref, wr_ref, br_ref, wi_ref, bi_ref, lam_ref, o_ref,
                xbuf, a_buf, u_buf, hp_buf, h_ref, *, ts, gate_w):
    t = pl.program_id(1)
    width = xl_ref.shape[1]

    @pl.when(t == 0)
    def _():
        xbuf[0:SUBLANES, :] = jnp.zeros((SUBLANES, width), F32)
        h_ref[...] = jnp.zeros_like(h_ref)

    xbuf[SUBLANES:SUBLANES + ts, :] = xl_ref[...]
    xc = cb_ref[...]
    for w in range(CONV_W):
        xc = xc + xbuf[pl.ds(SUBLANES - (CONV_W - 1) + w, ts), :] * cw_ref[w:w + 1, :]
    xbuf[0:SUBLANES, :] = xbuf[ts:ts + SUBLANES, :]

    xcb = xc.astype(BF16)
    r_parts, i_parts = [], []
    for c in range(width // gate_w):
        xs = xcb[:, c * gate_w:(c + 1) * gate_w]
        r_parts.append(jnp.dot(xs, wr_ref[c], preferred_element_type=F32))
        i_parts.append(jnp.dot(xs, wi_ref[c], preferred_element_type=F32))
    r = jax.nn.sigmoid(jnp.concatenate(r_parts, axis=-1) + br_ref[...])
    ig = jax.nn.sigmoid(jnp.concatenate(i_parts, axis=-1) + bi_ref[...])
    lam = lam_ref[...]
    log_sig_lam = -(jnp.maximum(-lam, 0.0) + jnp.log1p(jnp.exp(-jnp.abs(lam))))
    log_a = RG_C * r * log_sig_lam
    a = jnp.exp(log_a)
    th = jnp.tanh(log_a)
    u = jnp.sqrt(-2.0 * th / (1.0 - th)) * (ig * xc)

    sub = lax.broadcasted_iota(jnp.int32, (ts, width), 0) % SUBLANES
    for d in (1, 2, 4):
        keep = sub >= d
        a_prev = jnp.where(keep, pltpu.roll(a, d, axis=0), 1.0)
        u_prev = jnp.where(keep, pltpu.roll(u, d, axis=0), 0.0)
        u = a * u_prev + u
        a = a * a_prev
    a_buf[...] = a
    u_buf[...] = u

    def group(g, h):
        base = pl.multiple_of(g * SUBLANES, SUBLANES)
        hp_buf[pl.ds(base, SUBLANES), :] = jnp.broadcast_to(h, (SUBLANES, width))
        last = base + SUBLANES - 1
        return a_buf[pl.ds(last, 1), :] * h + u_buf[pl.ds(last, 1), :]

    h_ref[...] = lax.fori_loop(0, ts // SUBLANES, group, h_ref[...])
    hseq = u_buf[...] + a_buf[...] * hp_buf[...]
    o_ref[...] = hseq * _gelu_tanh(gl_ref[...])


def _lru(xl, gl, conv_w, conv_b, wr_bd, br, wi_bd, bi, lam, batch, seq, ts):
    n, width = xl.shape
    nt = seq // ts
    gate_w = wr_bd.shape[-1]
    row = lambda b, t: (b * nt + t, 0)
    const2 = lambda b, t: (0, 0)
    const3 = lambda b, t: (0, 0, 0)
    vec = pl.BlockSpec((1, width), const2)
    return pl.pallas_call(
        functools.partial(_lru_kernel, ts=ts, gate_w=gate_w),
        grid=(batch, nt),
        in_specs=[pl.BlockSpec((ts, width), row), pl.BlockSpec((ts, width), row),
                  pl.BlockSpec((CONV_W, width), const2), vec,
                  pl.BlockSpec(wr_bd.shape, const3), vec,
                  pl.BlockSpec(wi_bd.shape, const3), vec, vec],
        out_specs=pl.BlockSpec((ts, width), row),
        out_shape=jax.ShapeDtypeStruct((n, width), F32),
        scratch_shapes=[pltpu.VMEM((ts + SUBLANES, width), F32),
                        pltpu.VMEM((ts, width), F32), pltpu.VMEM((ts, width), F32),
                        pltpu.VMEM((ts, width), F32), pltpu.VMEM((1, width), F32)],
        compiler_params=_cparams(("arbitrary", "arbitrary")),
        name="lru",
    )(xl, gl, conv_w, conv_b, wr_bd, br, wi_bd, bi, lam)


def _mix_route_kernel(sb_ref, lru_ref, x_ref, sbg_ref, lrug_ref, wo_ref, n2g_ref,
                      wrh_ref, wrl_ref, rb_ref,
                      h_ref, u2_ref, rr_ref, rwt_ref, tc_ref,
                      before_ref, *, tm):
    step = pl.program_id(0)
    half = sb_ref.shape[1]

    @pl.when(step == 0)
    def _():
        r_id = lax.broadcasted_iota(jnp.int32, (tm, tm), 0)
        c_id = lax.broadcasted_iota(jnp.int32, (tm, tm), 1)
        before_ref[...] = (r_id < c_id).astype(BF16)
        tc_ref[...] = jnp.zeros_like(tc_ref)

    m_sb = _rms_f32(sb_ref[...], sbg_ref[...]).astype(BF16)
    m_lru = _rms_f32(lru_ref[...], lrug_ref[...]).astype(BF16)
    h = (x_ref[...]
         + jnp.dot(m_sb, wo_ref[0:half, :], preferred_element_type=F32)
         + jnp.dot(m_lru, wo_ref[half:2 * half, :], preferred_element_type=F32))
    h_ref[...] = h
    u2 = _rms_f32(h, n2g_ref[...])
    u2_ref[...] = u2.astype(BF16)

    u_hi = u2.astype(BF16)
    u_lo = (u2 - u_hi.astype(F32)).astype(BF16)
    logits = (jnp.dot(u_hi, wrh_ref[...], preferred_element_type=F32)
              + jnp.dot(u_lo, wrh_ref[...], preferred_element_type=F32)
              + jnp.dot(u_hi, wrl_ref[...], preferred_element_type=F32))
    lt = logits.T + rb_ref[...]

    sub = lax.broadcasted_iota(jnp.int32, (SUBLANES, tm), 0)

    def top1(x):
        m = jnp.max(x, axis=0, keepdims=True)
        idx = jnp.min(jnp.where(x == m, sub, SUBLANES), axis=0, keepdims=True)
        return m, idx

    grp = lt[0:SUBLANES, :]
    g_max, g_idx = top1(grp)
    g_p = 1.0 / jnp.sum(jnp.exp(grp - g_max), axis=0, keepdims=True)
    fine = lt[SUBLANES:2 * SUBLANES, :]
    for g in range(1, N_GROUPS):
        fine = jnp.where(g_idx == g, lt[(g + 1) * SUBLANES:(g + 2) * SUBLANES, :], fine)
    m1, i1 = top1(fine)
    m2, i2 = top1(jnp.where(sub == i1, -jnp.inf, fine))
    e2 = jnp.exp(m2 - m1)
    p1 = 1.0 / (1.0 + e2)
    w1 = g_p * p1
    w2 = g_p * (e2 * p1)
    x1 = g_idx * EXPERTS_PER_GROUP + i1
    x2 = g_idx * EXPERTS_PER_GROUP + i2

    eid = lax.broadcasted_iota(jnp.int32, (N_EXPERTS, tm), 0)
    oh1 = jnp.where(eid == x1, 1.0, 0.0)
    oh2 = jnp.where(eid == x2, 1.0, 0.0)
    pre1 = jnp.dot(oh1.astype(BF16), before_ref[...], preferred_element_type=F32)
    pre2 = jnp.dot(oh2.astype(BF16), before_ref[...], preferred_element_type=F32)
    cnt1 = jnp.sum(oh1, axis=1, keepdims=True)
    cnt2 = jnp.sum(oh2, axis=1, keepdims=True)
    seg8 = jnp.floor((cnt1 + cnt2 + (SUBLANES - 1.0)) * (1.0 / SUBLANES))
    e_r = lax.broadcasted_iota(jnp.int32, (N_EXPERTS, N_EXPERTS), 0)
    e_c = lax.broadcasted_iota(jnp.int32, (N_EXPERTS, N_EXPERTS), 1)
    lower = jnp.where(e_c < e_r, 1.0, 0.0).astype(BF16)
    seg8_b = jnp.broadcast_to(seg8, (N_EXPERTS, LANES)).astype(BF16)
    seg_off = SUBLANES * jnp.dot(lower, seg8_b, preferred_element_type=F32)[:, 0:1]
    pos1 = jnp.sum(oh1 * (pre1 + seg_off), axis=0, keepdims=True)
    pos2 = jnp.sum(oh2 * (pre2 + (seg_off + cnt1)), axis=0, keepdims=True)

    lane = lax.broadcasted_iota(jnp.int32, tc_ref.shape, 1)
    seg_rows = jnp.broadcast_to(seg8 * SUBLANES, tc_ref.shape).astype(jnp.int32)
    tc_ref[...] = jnp.where(lane == step, seg_rows, tc_ref[...])

    zrow = jnp.zeros((SUBLANES - 4, tm), jnp.int32)
    rr_ref[...] = jnp.concatenate(
        [pos1.astype(jnp.int32), pos2.astype(jnp.int32), x1, x2, zrow], axis=0)
    wt = jnp.concatenate([w1, w2, pos1, pos2, jnp.zeros((LANES - 4, tm), F32)], axis=0)
    rwt_ref[...] = wt.T


def _mix_route(sb, lru, x2, sbg, lrug, wo_bf, n2g, wr_hi, wr_lo, rbias, tm):
    n, d = x2.shape
    half = sb.shape[1]
    row = lambda i: (i, 0)
    const = lambda i: (0, 0)
    return pl.pallas_call(
        functools.partial(_mix_route_kernel, tm=tm),
        grid=(n // tm,),
        in_specs=[pl.BlockSpec((tm, half), row), pl.BlockSpec((tm, half), row),
                  pl.BlockSpec((tm, d), row),
                  pl.BlockSpec((1, half), const), pl.BlockSpec((1, half), const),
                  pl.BlockSpec(wo_bf.shape, const), pl.BlockSpec((1, d), const),
                  pl.BlockSpec(wr_hi.shape, const), pl.BlockSpec(wr_lo.shape, const),
                  pl.BlockSpec(rbias.shape, const)],
        out_specs=[pl.BlockSpec((tm, d), row), pl.BlockSpec((tm, d), row),
                   pl.BlockSpec((SUBLANES, tm), lambda i: (0, i)),
                   pl.BlockSpec((tm, LANES), row),
                   pl.BlockSpec((N_EXPERTS, LANES), const)],
        out_shape=[jax.ShapeDtypeStruct((n, d), F32), jax.ShapeDtypeStruct((n, d), BF16),
                   jax.ShapeDtypeStruct((SUBLANES, n), jnp.int32),
                   jax.ShapeDtypeStruct((n, LANES), F32),
                   jax.ShapeDtypeStruct((N_EXPERTS, LANES), jnp.int32)],
        scratch_shapes=[pltpu.VMEM((tm, tm), BF16)],
        compiler_params=_cparams(("arbitrary",)),
        name="mix_route",
    )(sb, lru, x2, sbg, lrug, wo_bf, n2g, wr_hi, wr_lo, rbias)


HIGH_HALF = -65536


def _pack_halves(x):
    half = x.shape[1] // 2
    lo = lax.shift_right_logical(lax.bitcast_convert_type(x[:, :half], jnp.int32), 16)
    hi = lax.bitcast_convert_type(x[:, half:], jnp.int32) & HIGH_HALF
    return hi | lo


def _unpack_halves(p):
    lo = lax.bitcast_convert_type(lax.shift_left(p, 16), F32)
    hi = lax.bitcast_convert_type(p & HIGH_HALF, F32)
    return lo.astype(BF16), hi.astype(BF16)


def _segment_copies(tile, c8_ref, loff_ref, goff_ref, make):
    for e in range(N_EXPERTS):
        idx = tile * N_EXPERTS + e
        rows = pl.multiple_of(c8_ref[idx], SUBLANES)

        @pl.when(rows > 0)
        def _(idx=idx, rows=rows):
            lo = pl.multiple_of(loff_ref[idx], SUBLANES)
            go = pl.multiple_of(goff_ref[idx], SUBLANES)
            make(pl.ds(lo, rows), pl.ds(go, rows)).start()


def _dispatch_kernel(c8_ref, loff_ref, goff_ref, tot_ref, zs_ref, zl_ref, nused_ref,
                     rr_ref, u2_ref, xs_ref, lbuf, zbuf, sem, zsem, *, td, lrows):
    i = pl.program_id(0)
    slot = i % 2

    r_id = lax.broadcasted_iota(jnp.int32, (lrows, td), 0)
    perm = jnp.where(r_id == rr_ref[0:1, :], 1.0, jnp.where(r_id == rr_ref[1:2, :], 1.0, 0.0))
    sorted_rows = jnp.dot(perm.astype(BF16), u2_ref[...], preferred_element_type=F32)
    lbuf[slot] = _pack_halves(sorted_rows)

    @pl.when(i == 0)
    def _():
        zbuf[...] = jnp.zeros_like(zbuf)
        for e in range(N_EXPERTS):
            rows = pl.multiple_of(zl_ref[e], SUBLANES)

            @pl.when(rows > 0)
            def _(e=e, rows=rows):
                start = pl.multiple_of(zs_ref[e], SUBLANES)
                pltpu.make_async_copy(zbuf.at[pl.ds(0, rows)], xs_ref.at[pl.ds(start, rows)],
                                      zsem).start()
        tme = zbuf.shape[0]
        n_row_tiles = xs_ref.shape[0] // tme

        def tail_copy(t):
            start = pl.multiple_of(t * tme, tme)
            return pltpu.make_async_copy(zbuf, xs_ref.at[pl.ds(start, tme)], zsem)

        def tail_start(t, c):
            tail_copy(t).start()
            return c

        def tail_wait(t, c):
            tail_copy(t).wait()
            return c

        lax.fori_loop(nused_ref[0], n_row_tiles, tail_start, 0)
        for e in range(N_EXPERTS):
            rows = pl.multiple_of(zl_ref[e], SUBLANES)

            @pl.when(rows > 0)
            def _(rows=rows):
                pltpu.make_async_copy(zbuf.at[pl.ds(0, rows)], xs_ref.at[pl.ds(0, rows)],
                                      zsem).wait()
        lax.fori_loop(nused_ref[0], n_row_tiles, tail_wait, 0)

    def wait_tile(tile, s):
        rows = pl.multiple_of(tot_ref[tile], SUBLANES)
        pltpu.make_async_copy(lbuf.at[s, pl.ds(0, rows)], xs_ref.at[pl.ds(0, rows)], sem).wait()

    @pl.when(i > 0)
    def _():
        wait_tile(i - 1, 1 - slot)

    _segment_copies(i, c8_ref, loff_ref, goff_ref,
                    lambda loc, glob: pltpu.make_async_copy(lbuf.at[slot, loc], xs_ref.at[glob], sem))

    @pl.when(i == pl.num_programs(0) - 1)
    def _():
        wait_tile(i, slot)


def _dispatch(c8, loff, goff, tot, zstart, zlen, n_used, rr, u2, p_rows, td, lrows, tme):
    n, d = u2.shape
    pmap = lambda i, *_: (0, i)
    return pl.pallas_call(
        functools.partial(_dispatch_kernel, td=td, lrows=lrows),
        grid_spec=pltpu.PrefetchScalarGridSpec(
            num_scalar_prefetch=7,
            grid=(n // td,),
            in_specs=[pl.BlockSpec((SUBLANES, td), pmap),
                      pl.BlockSpec((td, d), lambda i, *_: (i, 0))],
            out_specs=pl.BlockSpec(memory_space=pl.ANY),
            scratch_shapes=[pltpu.VMEM((2, lrows, d // 2), jnp.int32),
                            pltpu.VMEM((tme, d // 2), jnp.int32),
                            pltpu.SemaphoreType.DMA(()), pltpu.SemaphoreType.DMA(())]),
        out_shape=jax.ShapeDtypeStruct((p_rows, d // 2), jnp.int32),
        compiler_params=_cparams(("arbitrary",)),
        name="dispatch",
    )(c8, loff, goff, tot, zstart, zlen, n_used, rr, u2)


def _experts_kernel(te_ref, nused_ref, xs_ref, wg_ref, wu_ref, wd_ref, ys_ref,
                    wg_bf, wu_bf, wd_bf):
    t = pl.program_id(0)
    prev = te_ref[jnp.maximum(t - 1, 0)]
    used = t < nused_ref[0]

    @pl.when(used & ((t == 0) | (te_ref[t] != prev)))
    def _():
        wg_bf[...] = wg_ref[0].astype(BF16)
        wu_bf[...] = wu_ref[0].astype(BF16)
        wd_bf[...] = wd_ref[0].astype(BF16)

    @pl.when(used)
    def _():
        x_lo, x_hi = _unpack_halves(xs_ref[...])
        half = x_lo.shape[1]
        hg = (jnp.dot(x_lo, wg_bf[0:half, :], preferred_element_type=F32)
              + jnp.dot(x_hi, wg_bf[half:2 * half, :], preferred_element_type=F32))
        hu = (jnp.dot(x_lo, wu_bf[0:half, :], preferred_element_type=F32)
              + jnp.dot(x_hi, wu_bf[half:2 * half, :], preferred_element_type=F32))
        act = (hg * jax.nn.sigmoid(hg) * hu).astype(BF16)
        y = jnp.dot(act, wd_bf[...], preferred_element_type=F32)
        ys_ref[...] = _pack_halves(y.astype(BF16).astype(F32))

    @pl.when(jnp.logical_not(used))
    def _():
        ys_ref[...] = jnp.zeros_like(ys_ref)


def _experts(tile_expert, n_used, xs, wg, wu, wd, tme):
    p = xs.shape[0]
    d = wg.shape[1]
    de = wg.shape[-1]
    nt = p // tme

    def xmap(t, te, nu):
        return (jnp.minimum(t, nu[0] - 1), 0)

    wmap = lambda t, te, nu: (te[t], 0, 0)
    return pl.pallas_call(
        _experts_kernel,
        grid_spec=pltpu.PrefetchScalarGridSpec(
            num_scalar_prefetch=2,
            grid=(nt,),
            in_specs=[pl.BlockSpec((tme, d // 2), xmap),
                      pl.BlockSpec((1, d, de), wmap), pl.BlockSpec((1, d, de), wmap),
                      pl.BlockSpec((1, de, d), wmap)],
            out_specs=pl.BlockSpec((tme, d // 2), lambda t, te, nu: (t, 0)),
            scratch_shapes=[pltpu.VMEM((d, de), BF16), pltpu.VMEM((d, de), BF16),
                            pltpu.VMEM((de, d), BF16)]),
        out_shape=jax.ShapeDtypeStruct((p, d // 2), jnp.int32),
        compiler_params=_cparams(("arbitrary",)),
        name="experts",
    )(tile_expert, n_used, xs, wg, wu, wd)


def _combine_kernel(c8_ref, loff_ref, goff_ref, tot_ref,
                    rwt_ref, h_ref, fg_ref, ys_ref, y_ref, ybuf, sems, *, tc, lrows):
    i = pl.program_id(0)
    slot = i % 2

    def gather_tile(tile, s):
        _segment_copies(tile, c8_ref, loff_ref, goff_ref,
                        lambda loc, glob: pltpu.make_async_copy(ys_ref.at[glob], ybuf.at[s, loc],
                                                                sems.at[s]))

    @pl.when(i == 0)
    def _():
        ybuf[...] = jnp.zeros_like(ybuf)
        gather_tile(0, 0)

    @pl.when(i + 1 < pl.num_programs(0))
    def _():
        gather_tile(i + 1, 1 - slot)

    rows = pl.multiple_of(tot_ref[i], SUBLANES)
    pltpu.make_async_copy(ys_ref.at[pl.ds(0, rows)], ybuf.at[slot, pl.ds(0, rows)],
                          sems.at[slot]).wait()

    w = rwt_ref[...]
    c_id = lax.broadcasted_iota(jnp.int32, (tc, lrows), 1)
    pos1 = w[:, 2:3].astype(jnp.int32)
    pos2 = w[:, 3:4].astype(jnp.int32)
    wmat = (jnp.where(c_id == pos1, w[:, 0:1], 0.0)
            + jnp.where(c_id == pos2, w[:, 1:2], 0.0)).astype(BF16)
    y_lo, y_hi = _unpack_halves(ybuf[slot])
    moe = jnp.concatenate([jnp.dot(wmat, y_lo, preferred_element_type=F32),
                           jnp.dot(wmat, y_hi, preferred_element_type=F32)], axis=-1)
    y_ref[...] = _rms_f32(h_ref[...] + moe, fg_ref[...])


def _combine(c8, loff, goff, tot, rwt, h, final_g, ys, tc, lrows):
    n, d = h.shape
    return pl.pallas_call(
        functools.partial(_combine_kernel, tc=tc, lrows=lrows),
        grid_spec=pltpu.PrefetchScalarGridSpec(
            num_scalar_prefetch=4,
            grid=(n // tc,),
            in_specs=[pl.BlockSpec((tc, LANES), lambda i, *_: (i, 0)),
                      pl.BlockSpec((tc, d), lambda i, *_: (i, 0)),
                      pl.BlockSpec((1, d), lambda i, *_: (0, 0)),
                      pl.BlockSpec(memory_space=pl.ANY)],
            out_specs=pl.BlockSpec((tc, d), lambda i, *_: (i, 0)),
            scratch_shapes=[pltpu.VMEM((2, lrows, d // 2), jnp.int32),
                            pltpu.SemaphoreType.DMA((2,))]),
        out_shape=jax.ShapeDtypeStruct((n, d), F32),
        compiler_params=_cparams(("arbitrary",)),
        name="combine",
    )(c8, loff, goff, tot, rwt, h, final_g, ys)


def _block_diag(w, per):
    nb, c, _ = w.shape
    eye = jnp.eye(per, dtype=w.dtype)
    wg = w.reshape(nb // per, per, c, c)
    return jnp.einsum("gpij,pq->gpiqj", wg, eye).reshape(nb // per, per * c, per * c)


def _router_tables(w_group, b_group, w_fine, b_fine):
    d = w_group.shape[0]
    w = jnp.zeros((d, LANES), F32)
    w = w.at[:, 0:N_GROUPS].set(w_group).at[:, SUBLANES:SUBLANES + N_EXPERTS].set(w_fine)
    b = jnp.full((LANES,), NEG_BIG, F32)
    b = b.at[0:N_GROUPS].set(b_group).at[SUBLANES:SUBLANES + N_EXPERTS].set(b_fine)
    w_hi = w.astype(BF16)
    w_lo = (w - w_hi.astype(F32)).astype(BF16)
    return w_hi, w_lo, b.reshape(LANES, 1)


def kernel(x, norm1_g, w_in, conv_w, conv_b, w_rgate, b_rgate, w_igate, b_igate, lam, sb_norm_g,
           lru_norm_g, w_out, norm2_g, w_group, b_group, w_fine, b_fine, w_e_gate, w_e_up,
           w_e_down, final_g):
    batch, seq, d = x.shape
    n = batch * seq
    width = w_in.shape[1] // 5
    tm = min(512, seq)
    blk = min(256, seq)
    tme = 256
    gate_per = 256 // w_rgate.shape[1]

    x2 = x.reshape(n, d)
    vec = lambda a: a.reshape(1, -1)

    q, k, v, xl, gl = _in_proj(x2, vec(norm1_g), w_in.astype(BF16), width, tm)
    out_sb = _attention(q, k, v, batch, seq, blk)
    out_lru = _lru(xl, gl, conv_w, vec(conv_b),
                   _block_diag(w_rgate, gate_per).astype(BF16), vec(b_rgate),
                   _block_diag(w_igate, gate_per).astype(BF16), vec(b_igate),
                   vec(lam), batch, seq, min(256, seq))

    wr_hi, wr_lo, rbias = _router_tables(w_group, b_group, w_fine, b_fine)
    h, u2, rr, rwt, tcnt = _mix_route(out_sb, out_lru, x2, vec(sb_norm_g), vec(lru_norm_g),
                                      w_out.astype(BF16), vec(norm2_g), wr_hi, wr_lo, rbias, tm)

    n_tiles = n // tm
    assert n_tiles <= LANES, "one lane of the per-tile count table per token tile"
    c8 = tcnt[:, :n_tiles].T
    seg_total = jnp.sum(c8, axis=0)
    region = ((seg_total + tme - 1) // tme) * tme
    ends = jnp.cumsum(region)
    goff = (ends - region)[None, :] + jnp.cumsum(c8, axis=0) - c8
    loff = jnp.cumsum(c8, axis=1) - c8
    tot = jnp.sum(c8, axis=1)
    lrows = 2 * tm + N_EXPERTS * SUBLANES
    p_rows = -(-(2 * n + n_tiles * N_EXPERTS * (SUBLANES - 1) + N_EXPERTS * (tme - 1)) // tme) * tme
    tile_start = jnp.arange(p_rows // tme, dtype=jnp.int32) * tme
    tile_expert = jnp.minimum(jnp.sum(tile_start[:, None] >= ends[None, :], axis=1),
                              N_EXPERTS - 1).astype(jnp.int32)
    n_used = (ends[-1:] // tme).astype(jnp.int32)
    i32 = lambda a: a.reshape(-1).astype(jnp.int32)
    c8, loff, goff, tot = i32(c8), i32(loff), i32(goff), i32(tot)

    xs = _dispatch(c8, loff, goff, tot, i32(ends - region + seg_total), i32(region - seg_total),
                   n_used, rr, u2, p_rows, tm, lrows, tme)
    ys = _experts(tile_expert, n_used, xs, w_e_gate, w_e_up, w_e_down, tme)
    y = _combine(c8, loff, goff, tot, rwt, h, vec(final_g), ys, tm, lrows)
    return y.reshape(batch, seq, d)
```

```python
import functools
import math

import jax
import jax.numpy as jnp
from jax import lax
from jax.experimental import pallas as pl
from jax.experimental.pallas import tpu as pltpu

F32 = jnp.float32
BF16 = jnp.bfloat16

EPS = 1e-6
HEAD_DIM = 64
HEADS_PER_BLOCK = 2
LANES = 128
SUBLANES = 8
CONV_W = 4
RG_C = 8.0
N_GROUPS = 4
EXPERTS_PER_GROUP = 8
N_EXPERTS = N_GROUPS * EXPERTS_PER_GROUP
NEG_BIG = -1e30
ATTN_STOP = 104.0
ATTN_WINDOW_BLOCKS = 3

VMEM_LIMIT = 56 * 1024 * 1024


def _cparams(sem):
    return pltpu.CompilerParams(dimension_semantics=sem, vmem_limit_bytes=VMEM_LIMIT)


def _rms_f32(x, g):
    return x * lax.rsqrt(jnp.mean(x * x, axis=-1, keepdims=True) + EPS) * g


def _in_proj_kernel(x_ref, g_ref, w_ref, q_ref, k_ref, v_ref, xl_ref, gl_ref, *, width, q_scale):
    u = _rms_f32(x_ref[...], g_ref[...]).astype(BF16)
    for c, o_ref in enumerate((q_ref, k_ref, v_ref, xl_ref, gl_ref)):
        p = jnp.dot(u, w_ref[:, c * width:(c + 1) * width], preferred_element_type=F32)
        if c == 0:
            p = p * q_scale
        o_ref[...] = p.astype(o_ref.dtype)


def _in_proj(x2, g, w_bf, width, tm):
    n, d = x2.shape
    row = lambda i: (i, 0)
    out_bf = jax.ShapeDtypeStruct((n, width), BF16)
    out_f = jax.ShapeDtypeStruct((n, width), F32)
    return pl.pallas_call(
        functools.partial(_in_proj_kernel, width=width, q_scale=1.0 / math.sqrt(HEAD_DIM)),
        grid=(n // tm,),
        in_specs=[pl.BlockSpec((tm, d), row),
                  pl.BlockSpec((1, d), lambda i: (0, 0)),
                  pl.BlockSpec(w_bf.shape, lambda i: (0, 0))],
        out_specs=[pl.BlockSpec((tm, width), row)] * 5,
        out_shape=[out_bf, out_bf, out_bf, out_f, out_f],
        compiler_params=_cparams(("arbitrary",)),
        name="in_proj",
    )(x2, g, w_bf)


def _attn_kernel(q_ref, k_ref, v_ref, o_ref, tri_ref, z_ref, arg_ref, ctot_ref, acc_ref, carry_ref,
                 *, tq, nwin):
    seq = q_ref.shape[0]
    win = nwin * tq
    lane = lax.broadcasted_iota(jnp.int32, (1, LANES), 1)
    r_id = lax.broadcasted_iota(jnp.int32, (tq, tq), 0)
    c_id = lax.broadcasted_iota(jnp.int32, (tq, tq), 1)
    causal = c_id < r_id
    rel = c_id - r_id

    k_r = lax.broadcasted_iota(jnp.int32, (2 * tq, 2 * tq), 0) % tq
    k_c = lax.broadcasted_iota(jnp.int32, (2 * tq, 2 * tq), 1)
    tri_ref[...] = jnp.where(k_c >= tq, 1.0, jnp.where(k_r > k_c, 1.0, 0.0)).astype(BF16)

    def scores(qh, keys):
        z = lax.dot_general(qh, keys, (((1,), (1,)), ((), ())), preferred_element_type=F32)
        softplus = jnp.maximum(z, 0.0) + jnp.log(1.0 + jnp.exp(-jnp.abs(z)))
        return softplus, z - softplus

    def suffix(nlog_nb):
        hi = nlog_nb.astype(BF16)
        lo = (nlog_nb - hi.astype(F32)).astype(BF16)
        r = jnp.dot(jnp.concatenate([hi, lo], axis=1), tri_ref[...], preferred_element_type=F32)
        return r[:, :tq], r[:, tq:]

    def window(qh, wstart, masks):
        nlog_nb, log_b = scores(qh, k_ref[pl.ds(wstart, win), :])
        carry = None
        parts = [None] * nwin
        for b in reversed(range(nwin)):
            nl = nlog_nb[:, b * tq:(b + 1) * tq]
            if masks[b] is not None:
                nl = jnp.where(masks[b], nl, 0.0)
            excl, tot = suffix(nl)
            arg = log_b[:, b * tq:(b + 1) * tq] - excl
            a = jnp.exp(arg if carry is None else arg - carry)
            if masks[b] is not None:
                a = jnp.where(masks[b], a, 0.0)
            parts[b] = a.astype(BF16)
            carry = tot if carry is None else carry + tot
        out = jnp.dot(jnp.concatenate(parts, axis=1), v_ref[pl.ds(wstart, win), :],
                      preferred_element_type=F32)
        return out, carry

    def head_queries(i):
        q = q_ref[pl.ds(i * tq, tq), :]
        return [jnp.where((lane >= h * HEAD_DIM) & (lane < (h + 1) * HEAD_DIM), q, jnp.zeros_like(q))
                for h in range(HEADS_PER_BLOCK)]

    def store(i, outs):
        o_ref[pl.ds(i * tq, tq), :] = jnp.where(lane < HEAD_DIM, outs[0], outs[1])

    for i in range(nwin - 1):
        masks = [(rel + b * tq) < i * tq for b in range(nwin)]
        store(i, [window(qh, 0, masks)[0] for qh in head_queries(i)])

    def window_start(i):
        return pl.multiple_of((i - (nwin - 1)) * tq, tq)

    def stage_scores(i):
        keys = k_ref[pl.ds(window_start(i), win), :]
        for h, qh in enumerate(head_queries(i)):
            z_ref[i % 2, h] = lax.dot_general(qh, keys, (((1,), (1,)), ((), ())),
                                              preferred_element_type=F32)

    def stage_exponents(i):
        carries = []
        for h in range(HEADS_PER_BLOCK):
            z = z_ref[i % 2, h]
            softplus = jnp.maximum(z, 0.0) + jnp.log(1.0 + jnp.exp(-jnp.abs(z)))
            carry = None
            for b in reversed(range(nwin)):
                diag = b == nwin - 1
                cols = slice(b * tq, (b + 1) * tq)
                nl = softplus[:, cols]
                if diag:
                    nl = jnp.where(causal, nl, 0.0)
                excl, tot = suffix(nl)
                arg = z[:, cols] - softplus[:, cols] - excl
                if carry is not None:
                    arg = arg - carry
                if diag:
                    arg = jnp.where(causal, arg, NEG_BIG)
                arg_ref[i % 2, h, :, cols] = arg
                carry = tot if carry is None else carry + tot
            ctot_ref[i % 2, h] = carry
            carries.append(carry)
        return jnp.min(jnp.minimum(carries[0], carries[1]))

    def stage_output(i):
        vals = v_ref[pl.ds(window_start(i), win), :]
        for h in range(HEADS_PER_BLOCK):
            a = jnp.exp(arg_ref[i % 2, h]).astype(BF16)
            acc_ref[h] = jnp.dot(a, vals, preferred_element_type=F32)
            carry_ref[h] = ctot_ref[i % 2, h]

    def finish(i, cmin):
        def cond(state):
            j, cmin = state
            return (j >= 0) & (cmin <= ATTN_STOP)

        def older(state):
            j, _ = state
            start = pl.multiple_of(j * tq, tq)
            keys = k_ref[pl.ds(start, tq), :]
            vals = v_ref[pl.ds(start, tq), :]
            cs = []
            for h, qh in enumerate(head_queries(i)):
                nlog_nb, log_b = scores(qh, keys)
                excl, tot = suffix(nlog_nb)
                carry = carry_ref[h]
                a = jnp.exp(log_b - excl - carry)
                acc_ref[h] += jnp.dot(a.astype(BF16), vals, preferred_element_type=F32)
                carry_ref[h] = carry + tot
                cs.append(carry + tot)
            return j - 1, jnp.min(jnp.minimum(cs[0], cs[1]))

        lax.while_loop(cond, older, (i - nwin, cmin))
        store(i, [acc_ref[0], acc_ref[1]])

    first = nwin - 1
    last = seq // tq - 1
    stage_scores(first)
    stage_scores(first + 1)
    cmin = stage_exponents(first)

    def steady(t, cmin):
        stage_output(t - 2)
        cmin_next = stage_exponents(t - 1)
        stage_scores(t)
        finish(t - 2, cmin)
        return cmin_next

    cmin = lax.fori_loop(first + 2, last + 1, steady, cmin)
    stage_output(last - 1)
    cmin_last = stage_exponents(last)
    finish(last - 1, cmin)
    stage_output(last)
    finish(last, cmin_last)


def _attention(q, k, v, batch, seq, tq, nwin):
    n, width = q.shape
    assert seq >= (nwin + 1) * tq and tq == LANES
    blk = pl.BlockSpec((seq, LANES), lambda b, hp: (b, hp))
    stage_buf = pltpu.VMEM((2, HEADS_PER_BLOCK, tq, nwin * tq), F32)
    return pl.pallas_call(
        functools.partial(_attn_kernel, tq=tq, nwin=nwin),
        grid=(batch, width // LANES),
        in_specs=[blk, blk, blk],
        out_specs=blk,
        out_shape=jax.ShapeDtypeStruct((n, width), F32),
        scratch_shapes=[pltpu.VMEM((2 * tq, 2 * tq), BF16),
                        stage_buf, stage_buf,
                        pltpu.VMEM((2, HEADS_PER_BLOCK, tq, tq), F32),
                        pltpu.VMEM((HEADS_PER_BLOCK, tq, LANES), F32),
                        pltpu.VMEM((HEADS_PER_BLOCK, tq, tq), F32)],
        compiler_params=_cparams(("arbitrary", "arbitrary")),
        name="attn",
    )(q, k, v)


def _gelu_tanh(x):
    return 0.5 * x * (1.0 + jnp.tanh(math.sqrt(2.0 / math.pi) * (x + 0.044715 * (x * x * x))))


def _lru_kernel(xl_ref, gl_ref, cw_ref, cb_ref, wr_ref, br_ref, wi_ref, bi_ref, lam_ref, o_ref,
                xbuf, a_buf, u_buf, hp_buf, h_ref, *, ts, gate_w):
    t = pl.program_id(1)
    width = xl_ref.shape[1]

    @pl.when(t == 0)
    def _():
        xbuf[0:SUBLANES, :] = jnp.zeros((SUBLANES, width), F32)
        h_ref[...] = jnp.zeros_like(h_ref)

    xbuf[SUBLANES:SUBLANES + ts, :] = xl_ref[...]
    xc = cb_ref[...]
    for w in range(CONV_W):
        xc = xc + xbuf[pl.ds(SUBLANES - (CONV_W - 1) + w, ts), :] * cw_ref[w:w + 1, :]
    xbuf[0:SUBLANES, :] = xbuf[ts:ts + SUBLANES, :]

    xcb = xc.astype(BF16)
    r_parts, i_parts = [], []
    for c in range(width // gate_w):
        xs = xcb[:, c * gate_w:(c + 1) * gate_w]
        r_parts.append(jnp.dot(xs, wr_ref[c], preferred_element_type=F32))
        i_parts.append(jnp.dot(xs, wi_ref[c], preferred_element_type=F32))
    r = jax.nn.sigmoid(jnp.concatenate(r_parts, axis=-1) + br_ref[...])
    ig = jax.nn.sigmoid(jnp.concatenate(i_parts, axis=-1) + bi_ref[...])
    lam = lam_ref[...]
    log_sig_lam = -(jnp.maximum(-lam, 0.0) + jnp.log1p(jnp.exp(-jnp.abs(lam))))
    log_a = RG_C * r * log_sig_lam
    a = jnp.exp(log_a)
    th = jnp.tanh(log_a)
    u = jnp.sqrt(-2.0 * th / (1.0 - th)) * (ig * xc)

    sub = lax.broadcasted_iota(jnp.int32, (ts, width), 0) % SUBLANES
    for d in (1, 2, 4):
        keep = sub >= d
        a_prev = jnp.where(keep, pltpu.roll(a, d, axis=0), 1.0)
        u_prev = jnp.where(keep, pltpu.roll(u, d, axis=0), 0.0)
        u = a * u_prev + u
        a = a * a_prev
    a_buf[...] = a
    u_buf[...] = u

    def group(g, h):
        base = pl.multiple_of(g * SUBLANES, SUBLANES)
        hp_buf[pl.ds(base, SUBLANES), :] = jnp.broadcast_to(h, (SUBLANES, width))
        last = base + SUBLANES - 1
        return a_buf[pl.ds(last, 1), :] * h + u_buf[pl.ds(last, 1), :]

    h_ref[...] = lax.fori_loop(0, ts // SUBLANES, group, h_ref[...])
    hseq = u_buf[...] + a_buf[...] * hp_buf[...]
    o_ref[...] = hseq * _gelu_tanh(gl_ref[...])


def _lru(xl, gl, conv_w, conv_b, wr_bd, br, wi_bd, bi, lam, batch, seq, ts):
    n, width = xl.shape
    nt = seq // ts
    gate_w = wr_bd.shape[-1]
    row = lambda b, t: (b * nt + t, 0)
    const2 = lambda b, t: (0, 0)
    const3 = lambda b, t: (0, 0, 0)
    vec = pl.BlockSpec((1, width), const2)
    return pl.pallas_call(
        functools.partial(_lru_kernel, ts=ts, gate_w=gate_w),
        grid=(batch, nt),
        in_specs=[pl.BlockSpec((ts, width), row), pl.BlockSpec((ts, width), row),
                  pl.BlockSpec((CONV_W, width), const2), vec,
                  pl.BlockSpec(wr_bd.shape, const3), vec,
                  pl.BlockSpec(wi_bd.shape, const3), vec, vec],
        out_specs=pl.BlockSpec((ts, width), row),
        out_shape=jax.ShapeDtypeStruct((n, width), F32),
        scratch_shapes=[pltpu.VMEM((ts + SUBLANES, width), F32),
                        pltpu.VMEM((ts, width), F32), pltpu.VMEM((ts, width), F32),
                        pltpu.VMEM((ts, width), F32), pltpu.VMEM((1, width), F32)],
        compiler_params=_cparams(("arbitrary", "arbitrary")),
        name="lru",
    )(xl, gl, conv_w, conv_b, wr_bd, br, wi_bd, bi, lam)


def _mix_route_kernel(sb_ref, lru_ref, x_ref, sbg_ref, lrug_ref, wo_ref, n2g_ref,
                      wrh_ref, wrl_ref, rb_ref,
                      h_ref, u2_ref, rr_ref, rwt_ref, tc_ref,
                      before_ref, *, tm):
    step = pl.program_id(0)
    half = sb_ref.shape[1]

    @pl.when(step == 0)
    def _():
        r_id = lax.broadcasted_iota(jnp.int32, (tm, tm), 0)
        c_id = lax.broadcasted_iota(jnp.int32, (tm, tm), 1)
        before_ref[...] = (r_id < c_id).astype(BF16)
        tc_ref[...] = jnp.zeros_like(tc_ref)

    m_sb = _rms_f32(sb_ref[...], sbg_ref[...]).astype(BF16)
    m_lru = _rms_f32(lru_ref[...], lrug_ref[...]).astype(BF16)
    h = (x_ref[...]
         + jnp.dot(m_sb, wo_ref[0:half, :], preferred_element_type=F32)
         + jnp.dot(m_lru, wo_ref[half:2 * half, :], preferred_element_type=F32))
    h_ref[...] = h
    u2 = _rms_f32(h, n2g_ref[...])
    u2_ref[...] = u2.astype(BF16)

    u_hi = u2.astype(BF16)
    u_lo = (u2 - u_hi.astype(F32)).astype(BF16)
    logits = (jnp.dot(u_hi, wrh_ref[...], preferred_element_type=F32)
              + jnp.dot(u_lo, wrh_ref[...], preferred_element_type=F32)
              + jnp.dot(u_hi, wrl_ref[...], preferred_element_type=F32))
    lt = logits.T + rb_ref[...]

    sub = lax.broadcasted_iota(jnp.int32, (SUBLANES, tm), 0)

    def top1(x):
        m = jnp.max(x, axis=0, keepdims=True)
        idx = jnp.min(jnp.where(x == m, sub, SUBLANES), axis=0, keepdims=True)
        return m, idx

    grp = lt[0:SUBLANES, :]
    g_max, g_idx = top1(grp)
    g_p = 1.0 / jnp.sum(jnp.exp(grp - g_max), axis=0, keepdims=True)
    fine = lt[SUBLANES:2 * SUBLANES, :]
    for g in range(1, N_GROUPS):
        fine = jnp.where(g_idx == g, lt[(g + 1) * SUBLANES:(g + 2) * SUBLANES, :], fine)
    m1, i1 = top1(fine)
    m2, i2 = top1(jnp.where(sub == i1, -jnp.inf, fine))
    e2 = jnp.exp(m2 - m1)
    p1 = 1.0 / (1.0 + e2)
    w1 = g_p * p1
    w2 = g_p * (e2 * p1)
    x1 = g_idx * EXPERTS_PER_GROUP + i1
    x2 = g_idx * EXPERTS_PER_GROUP + i2

    eid = lax.broadcasted_iota(jnp.int32, (N_EXPERTS, tm), 0)
    oh1 = jnp.where(eid == x1, 1.0, 0.0)
    oh2 = jnp.where(eid == x2, 1.0, 0.0)
    pre1 = jnp.dot(oh1.astype(BF16), before_ref[...], preferred_element_type=F32)
    pre2 = jnp.dot(oh2.astype(BF16), before_ref[...], preferred_element_type=F32)
    cnt1 = jnp.sum(oh1, axis=1, keepdims=True)
    cnt2 = jnp.sum(oh2, axis=1, keepdims=True)
    seg8 = jnp.floor((cnt1 + cnt2 + (SUBLANES - 1.0)) * (1.0 / SUBLANES))
    e_r = lax.broadcasted_iota(jnp.int32, (N_EXPERTS, N_EXPERTS), 0)
    e_c = lax.broadcasted_iota(jnp.int32, (N_EXPERTS, N_EXPERTS), 1)
    lower = jnp.where(e_c < e_r, 1.0, 0.0).astype(BF16)
    seg8_b = jnp.broadcast_to(seg8, (N_EXPERTS, LANES)).astype(BF16)
    seg_off = SUBLANES * jnp.dot(lower, seg8_b, preferred_element_type=F32)[:, 0:1]
    pos1 = jnp.sum(oh1 * (pre1 + seg_off), axis=0, keepdims=True)
    pos2 = jnp.sum(oh2 * (pre2 + (seg_off + cnt1)), axis=0, keepdims=True)

    lane = lax.broadcasted_iota(jnp.int32, tc_ref.shape, 1)
    seg_rows = jnp.broadcast_to(seg8 * SUBLANES, tc_ref.shape).astype(jnp.int32)
    tc_ref[...] = jnp.where(lane == step, seg_rows, tc_ref[...])

    zrow = jnp.zeros((SUBLANES - 4, tm), jnp.int32)
    rr_ref[...] = jnp.concatenate(
        [pos1.astype(jnp.int32), pos2.astype(jnp.int32), x1, x2, zrow], axis=0)
    wt = jnp.concatenate([w1, w2, pos1, pos2, jnp.zeros((LANES - 4, tm), F32)], axis=0)
    rwt_ref[...] = wt.T


def _mix_route(sb, lru, x2, sbg, lrug, wo_bf, n2g, wr_hi, wr_lo, rbias, tm):
    n, d = x2.shape
    half = sb.shape[1]
    row = lambda i: (i, 0)
    const = lambda i: (0, 0)
    return pl.pallas_call(
        functools.partial(_mix_route_kernel, tm=tm),
        grid=(n // tm,),
        in_specs=[pl.BlockSpec((tm, half), row), pl.BlockSpec((tm, half), row),
                  pl.BlockSpec((tm, d), row),
                  pl.BlockSpec((1, half), const), pl.BlockSpec((1, half), const),
                  pl.BlockSpec(wo_bf.shape, const), pl.BlockSpec((1, d), const),
                  pl.BlockSpec(wr_hi.shape, const), pl.BlockSpec(wr_lo.shape, const),
                  pl.BlockSpec(rbias.shape, const)],
        out_specs=[pl.BlockSpec((tm, d), row), pl.BlockSpec((tm, d), row),
                   pl.BlockSpec((SUBLANES, tm), lambda i: (0, i)),
                   pl.BlockSpec((tm, LANES), row),
                   pl.BlockSpec((N_EXPERTS, LANES), const)],
        out_shape=[jax.ShapeDtypeStruct((n, d), F32), jax.ShapeDtypeStruct((n, d), BF16),
                   jax.ShapeDtypeStruct((SUBLANES, n), jnp.int32),
                   jax.ShapeDtypeStruct((n, LANES), F32),
                   jax.ShapeDtypeStruct((N_EXPERTS, LANES), jnp.int32)],
        scratch_shapes=[pltpu.VMEM((tm, tm), BF16)],
        compiler_params=_cparams(("arbitrary",)),
        name="mix_route",
    )(sb, lru, x2, sbg, lrug, wo_bf, n2g, wr_hi, wr_lo, rbias)


HIGH_HALF = -65536


def _pack_halves(x):
    half = x.shape[1] // 2
    lo = lax.shift_right_logical(lax.bitcast_convert_type(x[:, :half], jnp.int32), 16)
    hi = lax.bitcast_convert_type(x[:, half:], jnp.int32) & HIGH_HALF
    return hi | lo


def _unpack_halves(p):
    lo = lax.bitcast_convert_type(lax.shift_left(p, 16), F32)
    hi = lax.bitcast_convert_type(p & HIGH_HALF, F32)
    return lo.astype(BF16), hi.astype(BF16)


def _segment_copies(tile, c8_ref, loff_ref, goff_ref, make):
    for e in range(N_EXPERTS):
        idx = tile * N_EXPERTS + e
        rows = pl.multiple_of(c8_ref[idx], SUBLANES)

        @pl.when(rows > 0)
        def _(idx=idx, rows=rows):
            lo = pl.multiple_of(loff_ref[idx], SUBLANES)
            go = pl.multiple_of(goff_ref[idx], SUBLANES)
            make(pl.ds(lo, rows), pl.ds(go, rows)).start()


def _dispatch_kernel(c8_ref, loff_ref, goff_ref, tot_ref, zs_ref, zl_ref, nused_ref,
                     rr_ref, u2_ref, xs_ref, lbuf, zbuf, sem, zsem, *, td, lrows):
    i = pl.program_id(0)
    slot = i % 2

    r_id = lax.broadcasted_iota(jnp.int32, (lrows, td), 0)
    perm = jnp.where(r_id == rr_ref[0:1, :], 1.0, jnp.where(r_id == rr_ref[1:2, :], 1.0, 0.0))
    sorted_rows = jnp.dot(perm.astype(BF16), u2_ref[...], preferred_element_type=F32)
    lbuf[slot] = _pack_halves(sorted_rows)

    @pl.when(i == 0)
    def _():
        zbuf[...] = jnp.zeros_like(zbuf)
        for e in range(N_EXPERTS):
            rows = pl.multiple_of(zl_ref[e], SUBLANES)

            @pl.when(rows > 0)
            def _(e=e, rows=rows):
                start = pl.multiple_of(zs_ref[e], SUBLANES)
                pltpu.make_async_copy(zbuf.at[pl.ds(0, rows)], xs_ref.at[pl.ds(start, rows)],
                                      zsem).start()
        tme = zbuf.shape[0]
        n_row_tiles = xs_ref.shape[0] // tme

        def tail_copy(t):
            start = pl.multiple_of(t * tme, tme)
            return pltpu.make_async_copy(zbuf, xs_ref.at[pl.ds(start, tme)], zsem)

        def tail_start(t, c):
            tail_copy(t).start()
            return c

        def tail_wait(t, c):
            tail_copy(t).wait()
            return c

        lax.fori_loop(nused_ref[0], n_row_tiles, tail_start, 0)
        for e in range(N_EXPERTS):
            rows = pl.multiple_of(zl_ref[e], SUBLANES)

            @pl.when(rows > 0)
            def _(rows=rows):
                pltpu.make_async_copy(zbuf.at[pl.ds(0, rows)], xs_ref.at[pl.ds(0, rows)],
                                      zsem).wait()
        lax.fori_loop(nused_ref[0], n_row_tiles, tail_wait, 0)

    def wait_tile(tile, s):
        rows = pl.multiple_of(tot_ref[tile], SUBLANES)
        pltpu.make_async_copy(lbuf.at[s, pl.ds(0, rows)], xs_ref.at[pl.ds(0, rows)], sem).wait()

    @pl.when(i > 0)
    def _():
        wait_tile(i - 1, 1 - slot)

    _segment_copies(i, c8_ref, loff_ref, goff_ref,
                    lambda loc, glob: pltpu.make_async_copy(lbuf.at[slot, loc], xs_ref.at[glob], sem))

    @pl.when(i == pl.num_programs(0) - 1)
    def _():
        wait_tile(i, slot)


def _dispatch(c8, loff, goff, tot, zstart, zlen, n_used, rr, u2, p_rows, td, lrows, tme):
    n, d = u2.shape
    pmap = lambda i, *_: (0, i)
    return pl.pallas_call(
        functools.partial(_dispatch_kernel, td=td, lrows=lrows),
        grid_spec=pltpu.PrefetchScalarGridSpec(
            num_scalar_prefetch=7,
            grid=(n // td,),
            in_specs=[pl.BlockSpec((SUBLANES, td), pmap),
                      pl.BlockSpec((td, d), lambda i, *_: (i, 0))],
            out_specs=pl.BlockSpec(memory_space=pl.ANY),
            scratch_shapes=[pltpu.VMEM((2, lrows, d // 2), jnp.int32),
                            pltpu.VMEM((tme, d // 2), jnp.int32),
                            pltpu.SemaphoreType.DMA(()), pltpu.SemaphoreType.DMA(())]),
        out_shape=jax.ShapeDtypeStruct((p_rows, d // 2), jnp.int32),
        compiler_params=_cparams(("arbitrary",)),
        name="dispatch",
    )(c8, loff, goff, tot, zstart, zlen, n_used, rr, u2)


def _experts_kernel(te_ref, nused_ref, xs_ref, wg_ref, wu_ref, wd_ref, ys_ref,
                    wg_bf, wu_bf, wd_bf):
    t = pl.program_id(0)
    prev = te_ref[jnp.maximum(t - 1, 0)]
    used = t < nused_ref[0]

    @pl.when(used & ((t == 0) | (te_ref[t] != prev)))
    def _():
        wg_bf[...] = wg_ref[0].astype(BF16)
        wu_bf[...] = wu_ref[0].astype(BF16)
        wd_bf[...] = wd_ref[0].astype(BF16)

    @pl.when(used)
    def _():
        x_lo, x_hi = _unpack_halves(xs_ref[...])
        half = x_lo.shape[1]
        hg = (jnp.dot(x_lo, wg_bf[0:half, :], preferred_element_type=F32)
              + jnp.dot(x_hi, wg_bf[half:2 * half, :], preferred_element_type=F32))
        hu = (jnp.dot(x_lo, wu_bf[0:half, :], preferred_element_type=F32)
              + jnp.dot(x_hi, wu_bf[half:2 * half, :], preferred_element_type=F32))
        act = (hg * jax.nn.sigmoid(hg) * hu).astype(BF16)
        y = jnp.dot(act, wd_bf[...], preferred_element_type=F32)
        ys_ref[...] = _pack_halves(y.astype(BF16).astype(F32))

    @pl.when(jnp.logical_not(used))
    def _():
        ys_ref[...] = jnp.zeros_like(ys_ref)


def _experts(tile_expert, n_used, xs, wg, wu, wd, tme):
    p = xs.shape[0]
    d = wg.shape[1]
    de = wg.shape[-1]
    nt = p // tme

    def xmap(t, te, nu):
        return (jnp.minimum(t, nu[0] - 1), 0)

    wmap = lambda t, te, nu: (te[t], 0, 0)
    return pl.pallas_call(
        _experts_kernel,
        grid_spec=pltpu.PrefetchScalarGridSpec(
            num_scalar_prefetch=2,
            grid=(nt,),
            in_specs=[pl.BlockSpec((tme, d // 2), xmap),
                      pl.BlockSpec((1, d, de), wmap), pl.BlockSpec((1, d, de), wmap),
                      pl.BlockSpec((1, de, d), wmap)],
            out_specs=pl.BlockSpec((tme, d // 2), lambda t, te, nu: (t, 0)),
            scratch_shapes=[pltpu.VMEM((d, de), BF16), pltpu.VMEM((d, de), BF16),
                            pltpu.VMEM((de, d), BF16)]),
        out_shape=jax.ShapeDtypeStruct((p, d // 2), jnp.int32),
        compiler_params=_cparams(("arbitrary",)),
        name="experts",
    )(tile_expert, n_used, xs, wg, wu, wd)


def _combine_kernel(c8_ref, loff_ref, goff_ref, tot_ref,
                    rwt_ref, h_ref, fg_ref, ys_ref, y_ref, ybuf, sems, *, tc, lrows):
    i = pl.program_id(0)
    slot = i % 2

    def gather_tile(tile, s):
        _segment_copies(tile, c8_ref, loff_ref, goff_ref,
                        lambda loc, glob: pltpu.make_async_copy(ys_ref.at[glob], ybuf.at[s, loc],
                                                                sems.at[s]))

    @pl.when(i == 0)
    def _():
        ybuf[...] = jnp.zeros_like(ybuf)
        gather_tile(0, 0)

    @pl.when(i + 1 < pl.num_programs(0))
    def _():
        gather_tile(i + 1, 1 - slot)

    rows = pl.multiple_of(tot_ref[i], SUBLANES)
    pltpu.make_async_copy(ys_ref.at[pl.ds(0, rows)], ybuf.at[slot, pl.ds(0, rows)],
                          sems.at[slot]).wait()

    w = rwt_ref[...]
    c_id = lax.broadcasted_iota(jnp.int32, (tc, lrows), 1)
    pos1 = w[:, 2:3].astype(jnp.int32)
    pos2 = w[:, 3:4].astype(jnp.int32)
    wmat = (jnp.where(c_id == pos1, w[:, 0:1], 0.0)
            + jnp.where(c_id == pos2, w[:, 1:2], 0.0)).astype(BF16)
    y_lo, y_hi = _unpack_halves(ybuf[slot])
    moe = jnp.concatenate([jnp.dot(wmat, y_lo, preferred_element_type=F32),
                           jnp.dot(wmat, y_hi, preferred_element_type=F32)], axis=-1)
    y_ref[...] = _rms_f32(h_ref[...] + moe, fg_ref[...])


def _combine(c8, loff, goff, tot, rwt, h, final_g, ys, tc, lrows):
    n, d = h.shape
    return pl.pallas_call(
        functools.partial(_combine_kernel, tc=tc, lrows=lrows),
        grid_spec=pltpu.PrefetchScalarGridSpec(
            num_scalar_prefetch=4,
            grid=(n // tc,),
            in_specs=[pl.BlockSpec((tc, LANES), lambda i, *_: (i, 0)),
                      pl.BlockSpec((tc, d), lambda i, *_: (i, 0)),
                      pl.BlockSpec((1, d), lambda i, *_: (0, 0)),
                      pl.BlockSpec(memory_space=pl.ANY)],
            out_specs=pl.BlockSpec((tc, d), lambda i, *_: (i, 0)),
            scratch_shapes=[pltpu.VMEM((2, lrows, d // 2), jnp.int32),
                            pltpu.SemaphoreType.DMA((2,))]),
        out_shape=jax.ShapeDtypeStruct((n, d), F32),
        compiler_params=_cparams(("arbitrary",)),
        name="combine",
    )(c8, loff, goff, tot, rwt, h, final_g, ys)


def _block_diag(w, per):
    nb, c, _ = w.shape
    eye = jnp.eye(per, dtype=w.dtype)
    wg = w.reshape(nb // per, per, c, c)
    return jnp.einsum("gpij,pq->gpiqj", wg, eye).reshape(nb // per, per * c, per * c)


def _router_tables(w_group, b_group, w_fine, b_fine):
    d = w_group.shape[0]
    w = jnp.zeros((d, LANES), F32)
    w = w.at[:, 0:N_GROUPS].set(w_group).at[:, SUBLANES:SUBLANES + N_EXPERTS].set(w_fine)
    b = jnp.full((LANES,), NEG_BIG, F32)
    b = b.at[0:N_GROUPS].set(b_group).at[SUBLANES:SUBLANES + N_EXPERTS].set(b_fine)
    w_hi = w.astype(BF16)
    w_lo = (w - w_hi.astype(F32)).astype(BF16)
    return w_hi, w_lo, b.reshape(LANES, 1)


def kernel(x, norm1_g, w_in, conv_w, conv_b, w_rgate, b_rgate, w_igate, b_igate, lam, sb_norm_g,
           lru_norm_g, w_out, norm2_g, w_group, b_group, w_fine, b_fine, w_e_gate, w_e_up,
           w_e_down, final_g):
    batch, seq, d = x.shape
    n = batch * seq
    width = w_in.shape[1] // 5
    tm = min(512, seq)
    blk = min(256, seq)
    tme = 256
    gate_per = 256 // w_rgate.shape[1]

    x2 = x.reshape(n, d)
    vec = lambda a: a.reshape(1, -1)

    q, k, v, xl, gl = _in_proj(x2, vec(norm1_g), w_in.astype(BF16), width, tm)
    out_sb = _attention(q, k, v, batch, seq, LANES, ATTN_WINDOW_BLOCKS)
    out_lru = _lru(xl, gl, conv_w, vec(conv_b),
                   _block_diag(w_rgate, gate_per).astype(BF16), vec(b_rgate),
                   _block_diag(w_igate, gate_per).astype(BF16), vec(b_igate),
                   vec(lam), batch, seq, min(256, seq))

    wr_hi, wr_lo, rbias = _router_tables(w_group, b_group, w_fine, b_fine)
    h, u2, rr, rwt, tcnt = _mix_route(out_sb, out_lru, x2, vec(sb_norm_g), vec(lru_norm_g),
                                      w_out.astype(BF16), vec(norm2_g), wr_hi, wr_lo, rbias, tm)

    n_tiles = n // tm
    assert n_tiles <= LANES, "one lane of the per-tile count table per token tile"
    c8 = tcnt[:, :n_tiles].T
    seg_total = jnp.sum(c8, axis=0)
    region = ((seg_total + tme - 1) // tme) * tme
    ends = jnp.cumsum(region)
    goff = (ends - region)[None, :] + jnp.cumsum(c8, axis=0) - c8
    loff = jnp.cumsum(c8, axis=1) - c8
    tot = jnp.sum(c8, axis=1)
    lrows = 2 * tm + N_EXPERTS * SUBLANES
    p_rows = -(-(2 * n + n_tiles * N_EXPERTS * (SUBLANES - 1) + N_EXPERTS * (tme - 1)) // tme) * tme
    tile_start = jnp.arange(p_rows // tme, dtype=jnp.int32) * tme
    tile_expert = jnp.minimum(jnp.sum(tile_start[:, None] >= ends[None, :], axis=1),
                              N_EXPERTS - 1).astype(jnp.int32)
    n_used = (ends[-1:] // tme).astype(jnp.int32)
    i32 = lambda a: a.reshape(-1).astype(jnp.int32)
    c8, loff, goff, tot = i32(c8), i32(loff), i32(goff), i32(tot)

    xs = _dispatch(c8, loff, goff, tot, i32(ends - region + seg_total), i32(region - seg_total),
                   n_used, rr, u2, p_rows, tm, lrows, tme)
    ys = _experts(tile_expert, n_used, xs, w_e_gate, w_e_up, w_e_down, tme)
    y = _combine(c8, loff, goff, tot, rwt, h, vec(final_g), ys, tm, lrows)
    return y.reshape(batch, seq, d)
```

```python
import functools
import math

import jax
import jax.numpy as jnp
from jax import lax
from jax.experimental import pallas as pl
from jax.experimental.pallas import tpu as pltpu

F32 = jnp.float32
BF16 = jnp.bfloat16

EPS = 1e-6
HEAD_DIM = 64
HEADS_PER_BLOCK = 2
LANES = 128
SUBLANES = 8
CONV_W = 4
RG_C = 8.0
N_GROUPS = 4
EXPERTS_PER_GROUP = 8
N_EXPERTS = N_GROUPS * EXPERTS_PER_GROUP
NEG_BIG = -1e30
ATTN_STOP = 104.0
ATTN_WINDOW_BLOCKS = 3

VMEM_LIMIT = 56 * 1024 * 1024


def _cparams(sem):
    return pltpu.CompilerParams(dimension_semantics=sem, vmem_limit_bytes=VMEM_LIMIT)


def _rms_f32(x, g):
    return x * lax.rsqrt(jnp.mean(x * x, axis=-1, keepdims=True) + EPS) * g


def _in_proj_kernel(x_ref, g_ref, w_ref, q_ref, k_ref, v_ref, xl_ref, gl_ref, *, width, q_scale):
    u = _rms_f32(x_ref[...], g_ref[...]).astype(BF16)
    for c, o_ref in enumerate((q_ref, k_ref, v_ref, xl_ref, gl_ref)):
        p = jnp.dot(u, w_ref[:, c * width:(c + 1) * width], preferred_element_type=F32)
        if c == 0:
            p = p * q_scale
        o_ref[...] = p.astype(o_ref.dtype)


def _in_proj(x2, g, w_bf, width, tm):
    n, d = x2.shape
    row = lambda i: (i, 0)
    out_bf = jax.ShapeDtypeStruct((n, width), BF16)
    out_f = jax.ShapeDtypeStruct((n, width), F32)
    return pl.pallas_call(
        functools.partial(_in_proj_kernel, width=width, q_scale=1.0 / math.sqrt(HEAD_DIM)),
        grid=(n // tm,),
        in_specs=[pl.BlockSpec((tm, d), row),
                  pl.BlockSpec((1, d), lambda i: (0, 0)),
                  pl.BlockSpec(w_bf.shape, lambda i: (0, 0))],
        out_specs=[pl.BlockSpec((tm, width), row)] * 5,
        out_shape=[out_bf, out_bf, out_bf, out_f, out_f],
        compiler_params=_cparams(("arbitrary",)),
        name="in_proj",
    )(x2, g, w_bf)


def _attn_kernel(q_ref, k_ref, v_ref, o_ref, tri_ref, z_ref, arg_ref, ctot_ref, acc_ref, carry_ref,
                 *, tq, nwin):
    seq = q_ref.shape[0]
    win = nwin * tq
    lane = lax.broadcasted_iota(jnp.int32, (1, LANES), 1)
    r_id = lax.broadcasted_iota(jnp.int32, (tq, tq), 0)
    c_id = lax.broadcasted_iota(jnp.int32, (tq, tq), 1)
    causal = c_id < r_id
    rel = c_id - r_id

    k_r = lax.broadcasted_iota(jnp.int32, (2 * tq, 2 * tq), 0) % tq
    k_c = lax.broadcasted_iota(jnp.int32, (2 * tq, 2 * tq), 1)
    tri_ref[...] = jnp.where(k_c >= tq, 1.0, jnp.where(k_r > k_c, 1.0, 0.0)).astype(BF16)

    def scores(qh, keys):
        z = lax.dot_general(qh, keys, (((1,), (1,)), ((), ())), preferred_element_type=F32)
        softplus = jnp.maximum(z, 0.0) + jnp.log(1.0 + jnp.exp(-jnp.abs(z)))
        return softplus, z - softplus

    def suffix(nlog_nb):
        hi = nlog_nb.astype(BF16)
        lo = (nlog_nb - hi.astype(F32)).astype(BF16)
        r = jnp.dot(jnp.concatenate([hi, lo], axis=1), tri_ref[...], preferred_element_type=F32)
        return r[:, :tq], r[:, tq:]

    def window(qh, wstart, masks):
        nlog_nb, log_b = scores(qh, k_ref[pl.ds(wstart, win), :])
        carry = None
        parts = [None] * nwin
        for b in reversed(range(nwin)):
            nl = nlog_nb[:, b * tq:(b + 1) * tq]
            if masks[b] is not None:
                nl = jnp.where(masks[b], nl, 0.0)
            excl, tot = suffix(nl)
            arg = log_b[:, b * tq:(b + 1) * tq] - excl
            a = jnp.exp(arg if carry is None else arg - carry)
            if masks[b] is not None:
                a = jnp.where(masks[b], a, 0.0)
            parts[b] = a.astype(BF16)
            carry = tot if carry is None else carry + tot
        out = jnp.dot(jnp.concatenate(parts, axis=1), v_ref[pl.ds(wstart, win), :],
                      preferred_element_type=F32)
        return out, carry

    def head_queries(i):
        q = q_ref[pl.ds(i * tq, tq), :]
        return [jnp.where((lane >= h * HEAD_DIM) & (lane < (h + 1) * HEAD_DIM), q, jnp.zeros_like(q))
                for h in range(HEADS_PER_BLOCK)]

    def store(i, outs):
        o_ref[pl.ds(i * tq, tq), :] = jnp.where(lane < HEAD_DIM, outs[0], outs[1])

    for i in range(nwin - 1):
        masks = [(rel + b * tq) < i * tq for b in range(nwin)]
        store(i, [window(qh, 0, masks)[0] for qh in head_queries(i)])

    def window_start(i):
        return pl.multiple_of((i - (nwin - 1)) * tq, tq)

    def stage_scores(i):
        keys = k_ref[pl.ds(window_start(i), win), :]
        for h, qh in enumerate(head_queries(i)):
            z_ref[i % 2, h] = lax.dot_general(qh, keys, (((1,), (1,)), ((), ())),
                                              preferred_element_type=F32)

    def stage_exponents(i):
        carries = []
        for h in range(HEADS_PER_BLOCK):
            z = z_ref[i % 2, h]
            softplus = jnp.maximum(z, 0.0) + jnp.log(1.0 + jnp.exp(-jnp.abs(z)))
            carry = None
            for b in reversed(range(nwin)):
                diag = b == nwin - 1
                cols = slice(b * tq, (b + 1) * tq)
                nl = softplus[:, cols]
                if diag:
                    nl = jnp.where(causal, nl, 0.0)
                excl, tot = suffix(nl)
                arg = z[:, cols] - softplus[:, cols] - excl
                if carry is not None:
                    arg = arg - carry
                if diag:
                    arg = jnp.where(causal, arg, NEG_BIG)
                arg_ref[i % 2, h, :, cols] = arg
                carry = tot if carry is None else carry + tot
            ctot_ref[i % 2, h] = carry
            carries.append(carry)
        return jnp.min(jnp.minimum(carries[0], carries[1]))

    def stage_output(i):
        vals = v_ref[pl.ds(window_start(i), win), :]
        for h in range(HEADS_PER_BLOCK):
            a = jnp.exp(arg_ref[i % 2, h]).astype(BF16)
            acc_ref[h] = jnp.dot(a, vals, preferred_element_type=F32)
            carry_ref[h] = ctot_ref[i % 2, h]

    def finish(i, cmin):
        def cond(state):
            j, cmin = state
            return (j >= 0) & (cmin <= ATTN_STOP)

        def older(state):
            j, _ = state
            start = pl.multiple_of(j * tq, tq)
            keys = k_ref[pl.ds(start, tq), :]
            vals = v_ref[pl.ds(start, tq), :]
            cs = []
            for h, qh in enumerate(head_queries(i)):
                nlog_nb, log_b = scores(qh, keys)
                excl, tot = suffix(nlog_nb)
                carry = carry_ref[h]
                a = jnp.exp(log_b - excl - carry)
                acc_ref[h] += jnp.dot(a.astype(BF16), vals, preferred_element_type=F32)
                carry_ref[h] = carry + tot
                cs.append(carry + tot)
            return j - 1, jnp.min(jnp.minimum(cs[0], cs[1]))

        lax.while_loop(cond, older, (i - nwin, cmin))
        store(i, [acc_ref[0], acc_ref[1]])

    first = nwin - 1
    last = seq // tq - 1
    stage_scores(first)
    stage_scores(first + 1)
    cmin = stage_exponents(first)

    def steady(t, cmin):
        stage_output(t - 2)
        cmin_next = stage_exponents(t - 1)
        stage_scores(t)
        finish(t - 2, cmin)
        return cmin_next

    cmin = lax.fori_loop(first + 2, last + 1, steady, cmin)
    stage_output(last - 1)
    cmin_last = stage_exponents(last)
    finish(last - 1, cmin)
    stage_output(last)
    finish(last, cmin_last)


def _attention(q, k, v, batch, seq, tq, nwin):
    n, width = q.shape
    assert seq >= (nwin + 1) * tq and tq == LANES
    blk = pl.BlockSpec((seq, LANES), lambda b, hp: (b, hp))
    stage_buf = pltpu.VMEM((2, HEADS_PER_BLOCK, tq, nwin * tq), F32)
    return pl.pallas_call(
        functools.partial(_attn_kernel, tq=tq, nwin=nwin),
        grid=(batch, width // LANES),
        in_specs=[blk, blk, blk],
        out_specs=blk,
        out_shape=jax.ShapeDtypeStruct((n, width), F32),
        scratch_shapes=[pltpu.VMEM((2 * tq, 2 * tq), BF16),
                        stage_buf, stage_buf,
                        pltpu.VMEM((2, HEADS_PER_BLOCK, tq, tq), F32),
                        pltpu.VMEM((HEADS_PER_BLOCK, tq, LANES), F32),
                        pltpu.VMEM((HEADS_PER_BLOCK, tq, tq), F32)],
        compiler_params=_cparams(("arbitrary", "arbitrary")),
        name="attn",
    )(q, k, v)


def _gelu_tanh(x):
    return 0.5 * x * (1.0 + jnp.tanh(math.sqrt(2.0 / math.pi) * (x + 0.044715 * (x * x * x))))


def _lru_kernel(xl_ref, gl_ref, cw_ref, cb_ref, wr_ref, br_ref, wi_ref, bi_ref, lam_ref, o_ref,
                xbuf, a_buf, u_buf, hp_buf, h_ref, *, ts, gate_w):
    t = pl.program_id(1)
    width = xl_ref.shape[1]

    @pl.when(t == 0)
    def _():
        xbuf[0:SUBLANES, :] = jnp.zeros((SUBLANES, width), F32)
        h_ref[...] = jnp.zeros_like(h_ref)

    xbuf[SUBLANES:SUBLANES + ts, :] = xl_ref[...]
    xc = cb_ref[...]
    for w in range(CONV_W):
        xc = xc + xbuf[pl.ds(SUBLANES - (CONV_W - 1) + w, ts), :] * cw_ref[w:w + 1, :]
    xbuf[0:SUBLANES, :] = xbuf[ts:ts + SUBLANES, :]

    xcb = xc.astype(BF16)
    r_parts, i_parts = [], []
    for c in range(width // gate_w):
        xs = xcb[:, c * gate_w:(c + 1) * gate_w]
        r_parts.append(jnp.dot(xs, wr_ref[c], preferred_element_type=F32))
        i_parts.append(jnp.dot(xs, wi_ref[c], preferred_element_type=F32))
    r = jax.nn.sigmoid(jnp.concatenate(r_parts, axis=-1) + br_ref[...])
    ig = jax.nn.sigmoid(jnp.concatenate(i_parts, axis=-1) + bi_ref[...])
    lam = lam_ref[...]
    log_sig_lam = -(jnp.maximum(-lam, 0.0) + jnp.log1p(jnp.exp(-jnp.abs(lam))))
    log_a = RG_C * r * log_sig_lam
    a = jnp.exp(log_a)
    th = jnp.tanh(log_a)
    u = jnp.sqrt(-2.0 * th / (1.0 - th)) * (ig * xc)

    sub = lax.broadcasted_iota(jnp.int32, (ts, width), 0) % SUBLANES
    for d in (1, 2, 4):
        keep = sub >= d
        a_prev = jnp.where(keep, pltpu.roll(a, d, axis=0), 1.0)
        u_prev = jnp.where(keep, pltpu.roll(u, d, axis=0), 0.0)
        u = a * u_prev + u
        a = a * a_prev
    a_buf[...] = a
    u_buf[...] = u

    def group(g, h):
        base = pl.multiple_of(g * SUBLANES, SUBLANES)
        hp_buf[pl.ds(base, SUBLANES), :] = jnp.broadcast_to(h, (SUBLANES, width))
        last = base + SUBLANES - 1
        return a_buf[pl.ds(last, 1), :] * h + u_buf[pl.ds(last, 1), :]

    h_ref[...] = lax.fori_loop(0, ts // SUBLANES, group, h_ref[...])
    hseq = u_buf[...] + a_buf[...] * hp_buf[...]
    o_ref[...] = hseq * _gelu_tanh(gl_ref[...])


def _lru(xl, gl, conv_w, conv_b, wr_bd, br, wi_bd, bi, lam, batch, seq, ts):
    n, width = xl.shape
    nt = seq // ts
    gate_w = wr_bd.shape[-1]
    row = lambda b, t: (b * nt + t, 0)
    const2 = lambda b, t: (0, 0)
    const3 = lambda b, t: (0, 0, 0)
    vec = pl.BlockSpec((1, width), const2)
    return pl.pallas_call(
        functools.partial(_lru_kernel, ts=ts, gate_w=gate_w),
        grid=(batch, nt),
        in_specs=[pl.BlockSpec((ts, width), row), pl.BlockSpec((ts, width), row),
                  pl.BlockSpec((CONV_W, width), const2), vec,
                  pl.BlockSpec(wr_bd.shape, const3), vec,
                  pl.BlockSpec(wi_bd.shape, const3), vec, vec],
        out_specs=pl.BlockSpec((ts, width), row),
        out_shape=jax.ShapeDtypeStruct((n, width), F32),
        scratch_shapes=[pltpu.VMEM((ts + SUBLANES, width), F32),
                        pltpu.VMEM((ts, width), F32), pltpu.VMEM((ts, width), F32),
                        pltpu.VMEM((ts, width), F32), pltpu.VMEM((1, width), F32)],
        compiler_params=_cparams(("arbitrary", "arbitrary")),
        name="lru",
    )(xl, gl, conv_w, conv_b, wr_bd, br, wi_bd, bi, lam)


def _mix_route_kernel(sb_ref, lru_ref, x_ref, sbg_ref, lrug_ref, wo_ref, n2g_ref,
                      wrh_ref, wrl_ref, rb_ref,
                      h_ref, u2_ref, rr_ref, rwt_ref, tc_ref,
                      before_ref, *, tm):
    step = pl.program_id(0)
    half = sb_ref.shape[1]

    @pl.when(step == 0)
    def _():
        r_id = lax.broadcasted_iota(jnp.int32, (tm, tm), 0)
        c_id = lax.broadcasted_iota(jnp.int32, (tm, tm), 1)
        before_ref[...] = (r_id < c_id).astype(BF16)
        tc_ref[...] = jnp.zeros_like(tc_ref)

    m_sb = _rms_f32(sb_ref[...], sbg_ref[...]).astype(BF16)
    m_lru = _rms_f32(lru_ref[...], lrug_ref[...]).astype(BF16)
    h = (x_ref[...]
         + jnp.dot(m_sb, wo_ref[0:half, :], preferred_element_type=F32)
         + jnp.dot(m_lru, wo_ref[half:2 * half, :], preferred_element_type=F32))
    h_ref[...] = h
    u2 = _rms_f32(h, n2g_ref[...])
    u2_ref[...] = u2.astype(BF16)

    u_hi = u2.astype(BF16)
    u_lo = (u2 - u_hi.astype(F32)).astype(BF16)
    logits = (jnp.dot(u_hi, wrh_ref[...], preferred_element_type=F32)
              + jnp.dot(u_lo, wrh_ref[...], preferred_element_type=F32)
              + jnp.dot(u_hi, wrl_ref[...], preferred_element_type=F32))
    lt = logits.T + rb_ref[...]

    sub = lax.broadcasted_iota(jnp.int32, (SUBLANES, tm), 0)

    def top1(x):
        m = jnp.max(x, axis=0, keepdims=True)
        idx = jnp.min(jnp.where(x == m, sub, SUBLANES), axis=0, keepdims=True)
        return m, idx

    grp = lt[0:SUBLANES, :]
    g_max, g_idx = top1(grp)
    g_p = 1.0 / jnp.sum(jnp.exp(grp - g_max), axis=0, keepdims=True)
    fine = lt[SUBLANES:2 * SUBLANES, :]
    for g in range(1, N_GROUPS):
        fine = jnp.where(g_idx == g, lt[(g + 1) * SUBLANES:(g + 2) * SUBLANES, :], fine)
    m1, i1 = top1(fine)
    m2, i2 = top1(jnp.where(sub == i1, -jnp.inf, fine))
    e2 = jnp.exp(m2 - m1)
    p1 = 1.0 / (1.0 + e2)
    w1 = g_p * p1
    w2 = g_p * (e2 * p1)
    x1 = g_idx * EXPERTS_PER_GROUP + i1
    x2 = g_idx * EXPERTS_PER_GROUP + i2

    eid = lax.broadcasted_iota(jnp.int32, (N_EXPERTS, tm), 0)
    oh1 = jnp.where(eid == x1, 1.0, 0.0)
    oh2 = jnp.where(eid == x2, 1.0, 0.0)
    pre1 = jnp.dot(oh1.astype(BF16), before_ref[...], preferred_element_type=F32)
    pre2 = jnp.dot(oh2.astype(BF16), before_ref[...], preferred_element_type=F32)
    cnt1 = jnp.sum(oh1, axis=1, keepdims=True)
    cnt2 = jnp.sum(oh2, axis=1, keepdims=True)
    seg8 = jnp.floor((cnt1 + cnt2 + (SUBLANES - 1.0)) * (1.0 / SUBLANES))
    e_r = lax.broadcasted_iota(jnp.int32, (N_EXPERTS, N_EXPERTS), 0)
    e_c = lax.broadcasted_iota(jnp.int32, (N_EXPERTS, N_EXPERTS), 1)
    lower = jnp.where(e_c < e_r, 1.0, 0.0).astype(BF16)
    seg8_b = jnp.broadcast_to(seg8, (N_EXPERTS, LANES)).astype(BF16)
    seg_off = SUBLANES * jnp.dot(lower, seg8_b, preferred_element_type=F32)[:, 0:1]
    pos1 = jnp.sum(oh1 * (pre1 + seg_off), axis=0, keepdims=True)
    pos2 = jnp.sum(oh2 * (pre2 + (seg_off + cnt1)), axis=0, keepdims=True)

    lane = lax.broadcasted_iota(jnp.int32, tc_ref.shape, 1)
    seg_rows = jnp.broadcast_to(seg8 * SUBLANES, tc_ref.shape).astype(jnp.int32)
    tc_ref[...] = jnp.where(lane == step, seg_rows, tc_ref[...])

    zrow = jnp.zeros((SUBLANES - 4, tm), jnp.int32)
    rr_ref[...] = jnp.concatenate(
        [pos1.astype(jnp.int32), pos2.astype(jnp.int32), x1, x2, zrow], axis=0)
    wt = jnp.concatenate([w1, w2, pos1, pos2, jnp.zeros((LANES - 4, tm), F32)], axis=0)
    rwt_ref[...] = wt.T


def _mix_route(sb, lru, x2, sbg, lrug, wo_bf, n2g, wr_hi, wr_lo, rbias, tm):
    n, d = x2.shape
    half = sb.shape[1]
    row = lambda i: (i, 0)
    const = lambda i: (0, 0)
    return pl.pallas_call(
        functools.partial(_mix_route_kernel, tm=tm),
        grid=(n // tm,),
        in_specs=[pl.BlockSpec((tm, half), row), pl.BlockSpec((tm, half), row),
                  pl.BlockSpec((tm, d), row),
                  pl.BlockSpec((1, half), const), pl.BlockSpec((1, half), const),
                  pl.BlockSpec(wo_bf.shape, const), pl.BlockSpec((1, d), const),
                  pl.BlockSpec(wr_hi.shape, const), pl.BlockSpec(wr_lo.shape, const),
                  pl.BlockSpec(rbias.shape, const)],
        out_specs=[pl.BlockSpec((tm, d), row), pl.BlockSpec((tm, d), row),
                   pl.BlockSpec((SUBLANES, tm), lambda i: (0, i)),
                   pl.BlockSpec((tm, LANES), row),
                   pl.BlockSpec((N_EXPERTS, LANES), const)],
        out_shape=[jax.ShapeDtypeStruct((n, d), F32), jax.ShapeDtypeStruct((n, d), BF16),
                   jax.ShapeDtypeStruct((SUBLANES, n), jnp.int32),
                   jax.ShapeDtypeStruct((n, LANES), F32),
                   jax.ShapeDtypeStruct((N_EXPERTS, LANES), jnp.int32)],
        scratch_shapes=[pltpu.VMEM((tm, tm), BF16)],
        compiler_params=_cparams(("arbitrary",)),
        name="mix_route",
    )(sb, lru, x2, sbg, lrug, wo_bf, n2g, wr_hi, wr_lo, rbias)


HIGH_HALF = -65536


def _pack_halves(x):
    half = x.shape[1] // 2
    lo = lax.shift_right_logical(lax.bitcast_convert_type(x[:, :half], jnp.int32), 16)
    hi = lax.bitcast_convert_type(x[:, half:], jnp.int32) & HIGH_HALF
    return hi | lo


def _unpack_halves(p):
    lo = lax.bitcast_convert_type(lax.shift_left(p, 16), F32)
    hi = lax.bitcast_convert_type(p & HIGH_HALF, F32)
    return lo.astype(BF16), hi.astype(BF16)


def _segment_copies(tile, c8_ref, loff_ref, goff_ref, make):
    for e in range(N_EXPERTS):
        idx = tile * N_EXPERTS + e
        rows = pl.multiple_of(c8_ref[idx], SUBLANES)

        @pl.when(rows > 0)
        def _(idx=idx, rows=rows):
            lo = pl.multiple_of(loff_ref[idx], SUBLANES)
            go = pl.multiple_of(goff_ref[idx], SUBLANES)
            make(pl.ds(lo, rows), pl.ds(go, rows)).start()


def _dispatch_kernel(c8_ref, loff_ref, goff_ref, tot_ref, used_ref,
                     rr_ref, u2_ref, xs_ref, lbuf, zbuf, sem, zsem, *, td, lrows):
    i = pl.program_id(0)
    slot = i % 2

    r_id = lax.broadcasted_iota(jnp.int32, (lrows, td), 0)
    perm = jnp.where(r_id == rr_ref[0:1, :], 1.0, jnp.where(r_id == rr_ref[1:2, :], 1.0, 0.0))
    sorted_rows = jnp.dot(perm.astype(BF16), u2_ref[...], preferred_element_type=F32)
    lbuf[slot] = _pack_halves(sorted_rows)

    @pl.when(i == 0)
    def _():
        zbuf[...] = jnp.zeros_like(zbuf)
        chunk = zbuf.shape[0]
        used = used_ref[0]
        spare = xs_ref.shape[0] - used
        n_fill = (spare + chunk - 1) // chunk

        def fill_copy(k):
            rows = pl.multiple_of(jnp.minimum(chunk, spare - k * chunk), SUBLANES)
            start = pl.multiple_of(used + k * chunk, SUBLANES)
            return pltpu.make_async_copy(zbuf.at[pl.ds(0, rows)], xs_ref.at[pl.ds(start, rows)],
                                         zsem)

        def fill_start(k, c):
            fill_copy(k).start()
            return c

        def fill_wait(k, c):
            fill_copy(k).wait()
            return c

        lax.fori_loop(0, n_fill, fill_start, 0)
        lax.fori_loop(0, n_fill, fill_wait, 0)

    def wait_tile(tile, s):
        rows = pl.multiple_of(tot_ref[tile], SUBLANES)
        pltpu.make_async_copy(lbuf.at[s, pl.ds(0, rows)], xs_ref.at[pl.ds(0, rows)], sem).wait()

    @pl.when(i > 0)
    def _():
        wait_tile(i - 1, 1 - slot)

    _segment_copies(i, c8_ref, loff_ref, goff_ref,
                    lambda loc, glob: pltpu.make_async_copy(lbuf.at[slot, loc], xs_ref.at[glob], sem))

    @pl.when(i == pl.num_programs(0) - 1)
    def _():
        wait_tile(i, slot)


def _dispatch(c8, loff, goff, tot, used, rr, u2, p_rows, td, lrows, tme):
    n, d = u2.shape
    pmap = lambda i, *_: (0, i)
    return pl.pallas_call(
        functools.partial(_dispatch_kernel, td=td, lrows=lrows),
        grid_spec=pltpu.PrefetchScalarGridSpec(
            num_scalar_prefetch=5,
            grid=(n // td,),
            in_specs=[pl.BlockSpec((SUBLANES, td), pmap),
                      pl.BlockSpec((td, d), lambda i, *_: (i, 0))],
            out_specs=pl.BlockSpec(memory_space=pl.ANY),
            scratch_shapes=[pltpu.VMEM((2, lrows, d // 2), jnp.int32),
                            pltpu.VMEM((tme, d // 2), jnp.int32),
                            pltpu.SemaphoreType.DMA(()), pltpu.SemaphoreType.DMA(())]),
        out_shape=jax.ShapeDtypeStruct((p_rows, d // 2), jnp.int32),
        compiler_params=_cparams(("arbitrary",)),
        name="dispatch",
    )(c8, loff, goff, tot, used, rr, u2)


def _experts_kernel(eoff_ref, erows_ref, xs_ref, wg_ref, wu_ref, wd_ref, ys_ref,
                    wg_bf, wu_bf, wd_bf, xbuf, ybuf, sem_in, sem_out, state, *, tme):
    e = pl.program_id(0)
    n_experts = pl.num_programs(0)
    rows = erows_ref[e]
    off = eoff_ref[e]
    n_tiles = (rows + tme - 1) // tme

    def tile_rows(total, k):
        return pl.multiple_of(jnp.minimum(tme, total - k * tme), SUBLANES)

    def in_copy(start, r, slot):
        start = pl.multiple_of(start, SUBLANES)
        return pltpu.make_async_copy(xs_ref.at[pl.ds(start, r)], xbuf.at[slot, pl.ds(0, r)],
                                     sem_in.at[slot])

    def out_copy(start, r, slot):
        start = pl.multiple_of(start, SUBLANES)
        return pltpu.make_async_copy(ybuf.at[slot, pl.ds(0, r)], ys_ref.at[pl.ds(start, r)],
                                     sem_out.at[slot])

    @pl.when(e == 0)
    def _():
        for s in range(4):
            state[s] = 0
        xbuf[...] = jnp.zeros_like(xbuf)

    @pl.when(rows > 0)
    def _():
        wg_bf[...] = wg_ref[0].astype(BF16)
        wu_bf[...] = wu_ref[0].astype(BF16)
        wd_bf[...] = wd_ref[0].astype(BF16)
        done = state[0]

        @pl.when(state[1] == 0)
        def _():
            in_copy(off, tile_rows(rows, 0), done % 2).start()

        def tile(k, c):
            slot = (done + k) % 2
            r = tile_rows(rows, k)

            @pl.when(k + 1 < n_tiles)
            def _():
                in_copy(off + (k + 1) * tme, tile_rows(rows, k + 1), 1 - slot).start()

            in_copy(off, r, slot).wait()

            @pl.when(done + k >= 2)
            def _():
                out_copy(0, pl.multiple_of(state[2 + slot], SUBLANES), slot).wait()

            x_lo, x_hi = _unpack_halves(xbuf[slot])
            half = x_lo.shape[1]
            hg = (jnp.dot(x_lo, wg_bf[0:half, :], preferred_element_type=F32)
                  + jnp.dot(x_hi, wg_bf[half:2 * half, :], preferred_element_type=F32))
            hu = (jnp.dot(x_lo, wu_bf[0:half, :], preferred_element_type=F32)
                  + jnp.dot(x_hi, wu_bf[half:2 * half, :], preferred_element_type=F32))
            act = (hg * jax.nn.sigmoid(hg) * hu).astype(BF16)
            y = jnp.dot(act, wd_bf[...], preferred_element_type=F32)
            ybuf[slot] = _pack_halves(y.astype(BF16).astype(F32))
            out_copy(off + k * tme, r, slot).start()
            state[2 + slot] = r
            return c

        lax.fori_loop(0, n_tiles, tile, 0)
        state[0] = done + n_tiles

    nxt = jnp.minimum(e + 1, n_experts - 1)
    prefetch = (rows > 0) & (e + 1 < n_experts) & (erows_ref[nxt] > 0)
    state[1] = prefetch.astype(jnp.int32)

    @pl.when(prefetch)
    def _():
        in_copy(eoff_ref[nxt], tile_rows(erows_ref[nxt], 0), state[0] % 2).start()

    @pl.when(e == n_experts - 1)
    def _():
        total = eoff_ref[e] + rows
        for slot in range(2):
            @pl.when(state[0] > slot)
            def _(slot=slot):
                out_copy(0, pl.multiple_of(state[2 + slot], SUBLANES), slot).wait()
        ybuf[0] = jnp.zeros(ybuf.shape[1:], ybuf.dtype)
        spare = ys_ref.shape[0] - total
        n_fill = (spare + tme - 1) // tme

        def fill_copy(k):
            return out_copy(total + k * tme, tile_rows(spare, k), 0)

        def fill_start(k, c):
            fill_copy(k).start()
            return c

        def fill_wait(k, c):
            fill_copy(k).wait()
            return c

        lax.fori_loop(0, n_fill, fill_start, 0)
        lax.fori_loop(0, n_fill, fill_wait, 0)


def _experts(eoff, erows, xs, wg, wu, wd, tme):
    p = xs.shape[0]
    n_experts, d, de = wg.shape
    wmap = lambda e, *_: (e, 0, 0)
    return pl.pallas_call(
        functools.partial(_experts_kernel, tme=tme),
        grid_spec=pltpu.PrefetchScalarGridSpec(
            num_scalar_prefetch=2,
            grid=(n_experts,),
            in_specs=[pl.BlockSpec(memory_space=pl.ANY),
                      pl.BlockSpec((1, d, de), wmap), pl.BlockSpec((1, d, de), wmap),
                      pl.BlockSpec((1, de, d), wmap)],
            out_specs=pl.BlockSpec(memory_space=pl.ANY),
            scratch_shapes=[pltpu.VMEM((d, de), BF16), pltpu.VMEM((d, de), BF16),
                            pltpu.VMEM((de, d), BF16),
                            pltpu.VMEM((2, tme, d // 2), jnp.int32),
                            pltpu.VMEM((2, tme, d // 2), jnp.int32),
                            pltpu.SemaphoreType.DMA((2,)), pltpu.SemaphoreType.DMA((2,)),
                            pltpu.SMEM((4,), jnp.int32)]),
        out_shape=jax.ShapeDtypeStruct((p, d // 2), jnp.int32),
        compiler_params=_cparams(("arbitrary",)),
        name="experts",
    )(eoff, erows, xs, wg, wu, wd)


def _combine_kernel(c8_ref, loff_ref, goff_ref, tot_ref,
                    rwt_ref, h_ref, fg_ref, ys_ref, y_ref, ybuf, sems, *, tc, lrows):
    i = pl.program_id(0)
    slot = i % 2

    def gather_tile(tile, s):
        _segment_copies(tile, c8_ref, loff_ref, goff_ref,
                        lambda loc, glob: pltpu.make_async_copy(ys_ref.at[glob], ybuf.at[s, loc],
                                                                sems.at[s]))

    @pl.when(i == 0)
    def _():
        ybuf[...] = jnp.zeros_like(ybuf)
        gather_tile(0, 0)

    @pl.when(i + 1 < pl.num_programs(0))
    def _():
        gather_tile(i + 1, 1 - slot)

    rows = pl.multiple_of(tot_ref[i], SUBLANES)
    pltpu.make_async_copy(ys_ref.at[pl.ds(0, rows)], ybuf.at[slot, pl.ds(0, rows)],
                          sems.at[slot]).wait()

    w = rwt_ref[...]
    c_id = lax.broadcasted_iota(jnp.int32, (tc, lrows), 1)
    pos1 = w[:, 2:3].astype(jnp.int32)
    pos2 = w[:, 3:4].astype(jnp.int32)
    wmat = (jnp.where(c_id == pos1, w[:, 0:1], 0.0)
            + jnp.where(c_id == pos2, w[:, 1:2], 0.0)).astype(BF16)
    y_lo, y_hi = _unpack_halves(ybuf[slot])
    moe = jnp.concatenate([jnp.dot(wmat, y_lo, preferred_element_type=F32),
                           jnp.dot(wmat, y_hi, preferred_element_type=F32)], axis=-1)
    y_ref[...] = _rms_f32(h_ref[...] + moe, fg_ref[...])


def _combine(c8, loff, goff, tot, rwt, h, final_g, ys, tc, lrows):
    n, d = h.shape
    return pl.pallas_call(
        functools.partial(_combine_kernel, tc=tc, lrows=lrows),
        grid_spec=pltpu.PrefetchScalarGridSpec(
            num_scalar_prefetch=4,
            grid=(n // tc,),
            in_specs=[pl.BlockSpec((tc, LANES), lambda i, *_: (i, 0)),
                      pl.BlockSpec((tc, d), lambda i, *_: (i, 0)),
                      pl.BlockSpec((1, d), lambda i, *_: (0, 0)),
                      pl.BlockSpec(memory_space=pl.ANY)],
            out_specs=pl.BlockSpec((tc, d), lambda i, *_: (i, 0)),
            scratch_shapes=[pltpu.VMEM((2, lrows, d // 2), jnp.int32),
                            pltpu.SemaphoreType.DMA((2,))]),
        out_shape=jax.ShapeDtypeStruct((n, d), F32),
        compiler_params=_cparams(("arbitrary",)),
        name="combine",
    )(c8, loff, goff, tot, rwt, h, final_g, ys)


def _block_diag(w, per):
    nb, c, _ = w.shape
    eye = jnp.eye(per, dtype=w.dtype)
    wg = w.reshape(nb // per, per, c, c)
    return jnp.einsum("gpij,pq->gpiqj", wg, eye).reshape(nb // per, per * c, per * c)


def _router_tables(w_group, b_group, w_fine, b_fine):
    d = w_group.shape[0]
    w = jnp.zeros((d, LANES), F32)
    w = w.at[:, 0:N_GROUPS].set(w_group).at[:, SUBLANES:SUBLANES + N_EXPERTS].set(w_fine)
    b = jnp.full((LANES,), NEG_BIG, F32)
    b = b.at[0:N_GROUPS].set(b_group).at[SUBLANES:SUBLANES + N_EXPERTS].set(b_fine)
    w_hi = w.astype(BF16)
    w_lo = (w - w_hi.astype(F32)).astype(BF16)
    return w_hi, w_lo, b.reshape(LANES, 1)


def kernel(x, norm1_g, w_in, conv_w, conv_b, w_rgate, b_rgate, w_igate, b_igate, lam, sb_norm_g,
           lru_norm_g, w_out, norm2_g, w_group, b_group, w_fine, b_fine, w_e_gate, w_e_up,
           w_e_down, final_g):
    batch, seq, d = x.shape
    n = batch * seq
    width = w_in.shape[1] // 5
    tm = min(512, seq)
    tme = 256
    gate_per = 256 // w_rgate.shape[1]

    x2 = x.reshape(n, d)
    vec = lambda a: a.reshape(1, -1)

    q, k, v, xl, gl = _in_proj(x2, vec(norm1_g), w_in.astype(BF16), width, tm)
    out_sb = _attention(q, k, v, batch, seq, LANES, ATTN_WINDOW_BLOCKS)
    out_lru = _lru(xl, gl, conv_w, vec(conv_b),
                   _block_diag(w_rgate, gate_per).astype(BF16), vec(b_rgate),
                   _block_diag(w_igate, gate_per).astype(BF16), vec(b_igate),
                   vec(lam), batch, seq, min(256, seq))

    wr_hi, wr_lo, rbias = _router_tables(w_group, b_group, w_fine, b_fine)
    h, u2, rr, rwt, tcnt = _mix_route(out_sb, out_lru, x2, vec(sb_norm_g), vec(lru_norm_g),
                                      w_out.astype(BF16), vec(norm2_g), wr_hi, wr_lo, rbias, tm)

    n_tiles = n // tm
    assert n_tiles <= LANES, "one lane of the per-tile count table per token tile"
    c8 = tcnt[:, :n_tiles].T
    erows = jnp.sum(c8, axis=0)
    eoff = jnp.cumsum(erows) - erows
    goff = eoff[None, :] + jnp.cumsum(c8, axis=0) - c8
    loff = jnp.cumsum(c8, axis=1) - c8
    tot = jnp.sum(c8, axis=1)
    lrows = 2 * tm + N_EXPERTS * SUBLANES
    p_rows = 2 * n + n_tiles * N_EXPERTS * (SUBLANES - 1)
    p_rows = -(-p_rows // SUBLANES) * SUBLANES
    i32 = lambda a: a.reshape(-1).astype(jnp.int32)
    c8, loff, goff, tot, eoff, erows = (i32(a) for a in (c8, loff, goff, tot, eoff, erows))

    xs = _dispatch(c8, loff, goff, tot, jnp.sum(erows, keepdims=True), rr, u2, p_rows, tm, lrows, tme)
    ys = _experts(eoff, erows, xs, w_e_gate, w_e_up, w_e_down, tme)
    y = _combine(c8, loff, goff, tot, rwt, h, vec(final_g), ys, tm, lrows)
    return y.reshape(batch, seq, d)
```

```python
import functools
import math

import jax
import jax.numpy as jnp
from jax import lax
from jax.experimental import pallas as pl
from jax.experimental.pallas import tpu as pltpu

F32 = jnp.float32
BF16 = jnp.bfloat16

EPS = 1e-6
HEAD_DIM = 64
HEADS_PER_BLOCK = 2
LANES = 128
SUBLANES = 8
CONV_W = 4
RG_C = 8.0
N_GROUPS = 4
EXPERTS_PER_GROUP = 8
N_EXPERTS = N_GROUPS * EXPERTS_PER_GROUP
NEG_BIG = -1e30
ATTN_STOP = 104.0
ATTN_WINDOW_BLOCKS = 3

VMEM_LIMIT = 56 * 1024 * 1024


def _cparams(sem):
    return pltpu.CompilerParams(dimension_semantics=sem, vmem_limit_bytes=VMEM_LIMIT)


def _rms_f32(x, g):
    return x * lax.rsqrt(jnp.mean(x * x, axis=-1, keepdims=True) + EPS) * g


def _in_proj_kernel(x_ref, g_ref, w_ref, q_ref, k_ref, v_ref, xl_ref, gl_ref, *, width, q_scale):
    u = _rms_f32(x_ref[...], g_ref[...]).astype(BF16)
    for c, o_ref in enumerate((q_ref, k_ref, v_ref, xl_ref, gl_ref)):
        p = jnp.dot(u, w_ref[:, c * width:(c + 1) * width], preferred_element_type=F32)
        if c == 0:
            p = p * q_scale
        o_ref[...] = p.astype(o_ref.dtype)


def _in_proj(x2, g, w_bf, width, tm):
    n, d = x2.shape
    row = lambda i: (i, 0)
    out_bf = jax.ShapeDtypeStruct((n, width), BF16)
    out_f = jax.ShapeDtypeStruct((n, width), F32)
    return pl.pallas_call(
        functools.partial(_in_proj_kernel, width=width, q_scale=1.0 / math.sqrt(HEAD_DIM)),
        grid=(n // tm,),
        in_specs=[pl.BlockSpec((tm, d), row),
                  pl.BlockSpec((1, d), lambda i: (0, 0)),
                  pl.BlockSpec(w_bf.shape, lambda i: (0, 0))],
        out_specs=[pl.BlockSpec((tm, width), row)] * 5,
        out_shape=[out_bf, out_bf, out_bf, out_f, out_f],
        compiler_params=_cparams(("arbitrary",)),
        name="in_proj",
    )(x2, g, w_bf)


def _attn_kernel(q_ref, k_ref, v_ref, o_ref, tri_ref, z_ref, arg_ref, ctot_ref, acc_ref, carry_ref,
                 *, tq, nwin):
    seq = q_ref.shape[0]
    win = nwin * tq
    lane = lax.broadcasted_iota(jnp.int32, (1, LANES), 1)
    r_id = lax.broadcasted_iota(jnp.int32, (tq, tq), 0)
    c_id = lax.broadcasted_iota(jnp.int32, (tq, tq), 1)
    causal = c_id < r_id
    rel = c_id - r_id

    k_r = lax.broadcasted_iota(jnp.int32, (2 * tq, 2 * tq), 0) % tq
    k_c = lax.broadcasted_iota(jnp.int32, (2 * tq, 2 * tq), 1)
    tri_ref[...] = jnp.where(k_c >= tq, 1.0, jnp.where(k_r > k_c, 1.0, 0.0)).astype(BF16)

    def scores(qh, keys):
        z = lax.dot_general(qh, keys, (((1,), (1,)), ((), ())), preferred_element_type=F32)
        softplus = jnp.maximum(z, 0.0) + jnp.log(1.0 + jnp.exp(-jnp.abs(z)))
        return softplus, z - softplus

    def suffix(nlog_nb):
        hi = nlog_nb.astype(BF16)
        lo = (nlog_nb - hi.astype(F32)).astype(BF16)
        r = jnp.dot(jnp.concatenate([hi, lo], axis=1), tri_ref[...], preferred_element_type=F32)
        return r[:, :tq], r[:, tq:]

    def window(qh, wstart, masks):
        nlog_nb, log_b = scores(qh, k_ref[pl.ds(wstart, win), :])
        carry = None
        parts = [None] * nwin
        for b in reversed(range(nwin)):
            nl = nlog_nb[:, b * tq:(b + 1) * tq]
            if masks[b] is not None:
                nl = jnp.where(masks[b], nl, 0.0)
            excl, tot = suffix(nl)
            arg = log_b[:, b * tq:(b + 1) * tq] - excl
            a = jnp.exp(arg if carry is None else arg - carry)
            if masks[b] is not None:
                a = jnp.where(masks[b], a, 0.0)
            parts[b] = a.astype(BF16)
            carry = tot if carry is None else carry + tot
        out = jnp.dot(jnp.concatenate(parts, axis=1), v_ref[pl.ds(wstart, win), :],
                      preferred_element_type=F32)
        return out, carry

    def head_queries(i):
        q = q_ref[pl.ds(i * tq, tq), :]
        return [jnp.where((lane >= h * HEAD_DIM) & (lane < (h + 1) * HEAD_DIM), q, jnp.zeros_like(q))
                for h in range(HEADS_PER_BLOCK)]

    def store(i, outs):
        o_ref[pl.ds(i * tq, tq), :] = jnp.where(lane < HEAD_DIM, outs[0], outs[1])

    for i in range(nwin - 1):
        masks = [(rel + b * tq) < i * tq for b in range(nwin)]
        store(i, [window(qh, 0, masks)[0] for qh in head_queries(i)])

    def window_start(i):
        return pl.multiple_of((i - (nwin - 1)) * tq, tq)

    def stage_scores(i):
        keys = k_ref[pl.ds(window_start(i), win), :]
        z = lax.dot_general(jnp.concatenate(head_queries(i), axis=0), keys,
                            (((1,), (1,)), ((), ())), preferred_element_type=F32)
        for h in range(HEADS_PER_BLOCK):
            z_ref[i % 2, h] = z[h * tq:(h + 1) * tq]

    def stage_exponents(i):
        log_b, split = {}, []
        for h in range(HEADS_PER_BLOCK):
            z = z_ref[i % 2, h]
            softplus = jnp.maximum(z, 0.0) + jnp.log(1.0 + jnp.exp(-jnp.abs(z)))
            for b in range(nwin):
                cols = slice(b * tq, (b + 1) * tq)
                nl = softplus[:, cols]
                if b == nwin - 1:
                    nl = jnp.where(causal, nl, 0.0)
                log_b[h, b] = z[:, cols] - softplus[:, cols]
                hi = nl.astype(BF16)
                lo = (nl - hi.astype(F32)).astype(BF16)
                split.append(jnp.concatenate([hi, lo], axis=1))
        sums = jnp.dot(jnp.concatenate(split, axis=0), tri_ref[...], preferred_element_type=F32)
        carries = []
        for h in range(HEADS_PER_BLOCK):
            carry = None
            for b in reversed(range(nwin)):
                r = sums[(h * nwin + b) * tq:(h * nwin + b + 1) * tq]
                arg = log_b[h, b] - r[:, :tq]
                if carry is not None:
                    arg = arg - carry
                if b == nwin - 1:
                    arg = jnp.where(causal, arg, NEG_BIG)
                arg_ref[i % 2, h, :, b * tq:(b + 1) * tq] = arg
                carry = r[:, tq:] if carry is None else carry + r[:, tq:]
            ctot_ref[i % 2, h] = carry
            carries.append(carry)
        return jnp.min(jnp.minimum(carries[0], carries[1]))

    def stage_output(i):
        vals = v_ref[pl.ds(window_start(i), win), :]
        a = jnp.concatenate([jnp.exp(arg_ref[i % 2, h]).astype(BF16)
                             for h in range(HEADS_PER_BLOCK)], axis=0)
        out = jnp.dot(a, vals, preferred_element_type=F32)
        for h in range(HEADS_PER_BLOCK):
            acc_ref[h] = out[h * tq:(h + 1) * tq]
            carry_ref[h] = ctot_ref[i % 2, h]

    def finish(i, cmin):
        def cond(state):
            j, cmin = state
            return (j >= 0) & (cmin <= ATTN_STOP)

        def older(state):
            j, _ = state
            start = pl.multiple_of(j * tq, tq)
            keys = k_ref[pl.ds(start, tq), :]
            vals = v_ref[pl.ds(start, tq), :]
            cs = []
            for h, qh in enumerate(head_queries(i)):
                nlog_nb, log_b = scores(qh, keys)
                excl, tot = suffix(nlog_nb)
                carry = carry_ref[h]
                a = jnp.exp(log_b - excl - carry)
                acc_ref[h] += jnp.dot(a.astype(BF16), vals, preferred_element_type=F32)
                carry_ref[h] = carry + tot
                cs.append(carry + tot)
            return j - 1, jnp.min(jnp.minimum(cs[0], cs[1]))

        lax.while_loop(cond, older, (i - nwin, cmin))
        store(i, [acc_ref[0], acc_ref[1]])

    first = nwin - 1
    last = seq // tq - 1
    stage_scores(first)
    stage_scores(first + 1)
    cmin = stage_exponents(first)

    def steady(t, cmin):
        stage_output(t - 2)
        cmin_next = stage_exponents(t - 1)
        stage_scores(t)
        finish(t - 2, cmin)
        return cmin_next

    cmin = lax.fori_loop(first + 2, last + 1, steady, cmin)
    stage_output(last - 1)
    cmin_last = stage_exponents(last)
    finish(last - 1, cmin)
    stage_output(last)
    finish(last, cmin_last)


def _attention(q, k, v, batch, seq, tq, nwin):
    n, width = q.shape
    assert seq >= (nwin + 1) * tq and tq == LANES
    blk = pl.BlockSpec((seq, LANES), lambda b, hp: (b, hp))
    stage_buf = pltpu.VMEM((2, HEADS_PER_BLOCK, tq, nwin * tq), F32)
    return pl.pallas_call(
        functools.partial(_attn_kernel, tq=tq, nwin=nwin),
        grid=(batch, width // LANES),
        in_specs=[blk, blk, blk],
        out_specs=blk,
        out_shape=jax.ShapeDtypeStruct((n, width), F32),
        scratch_shapes=[pltpu.VMEM((2 * tq, 2 * tq), BF16),
                        stage_buf, stage_buf,
                        pltpu.VMEM((2, HEADS_PER_BLOCK, tq, tq), F32),
                        pltpu.VMEM((HEADS_PER_BLOCK, tq, LANES), F32),
                        pltpu.VMEM((HEADS_PER_BLOCK, tq, tq), F32)],
        compiler_params=_cparams(("arbitrary", "arbitrary")),
        name="attn",
    )(q, k, v)


def _gelu_tanh(x):
    return 0.5 * x * (1.0 + jnp.tanh(math.sqrt(2.0 / math.pi) * (x + 0.044715 * (x * x * x))))


def _lru_kernel(xl_ref, gl_ref, cw_ref, cb_ref, wr_ref, br_ref, wi_ref, bi_ref, lam_ref, o_ref,
                xbuf, a_buf, u_buf, hp_buf, h_ref, *, ts, gate_w):
    t = pl.program_id(1)
    width = xl_ref.shape[1]

    @pl.when(t == 0)
    def _():
        xbuf[0:SUBLANES, :] = jnp.zeros((SUBLANES, width), F32)
        h_ref[...] = jnp.zeros_like(h_ref)

    xbuf[SUBLANES:SUBLANES + ts, :] = xl_ref[...]
    xc = cb_ref[...]
    for w in range(CONV_W):
        xc = xc + xbuf[pl.ds(SUBLANES - (CONV_W - 1) + w, ts), :] * cw_ref[w:w + 1, :]
    xbuf[0:SUBLANES, :] = xbuf[ts:ts + SUBLANES, :]

    xcb = xc.astype(BF16)
    r_parts, i_parts = [], []
    for c in range(width // gate_w):
        xs = xcb[:, c * gate_w:(c + 1) * gate_w]
        r_parts.append(jnp.dot(xs, wr_ref[c], preferred_element_type=F32))
        i_parts.append(jnp.dot(xs, wi_ref[c], preferred_element_type=F32))
    r = jax.nn.sigmoid(jnp.concatenate(r_parts, axis=-1) + br_ref[...])
    ig = jax.nn.sigmoid(jnp.concatenate(i_parts, axis=-1) + bi_ref[...])
    lam = lam_ref[...]
    log_sig_lam = -(jnp.maximum(-lam, 0.0) + jnp.log1p(jnp.exp(-jnp.abs(lam))))
    log_a = RG_C * r * log_sig_lam
    a = jnp.exp(log_a)
    th = jnp.tanh(log_a)
    u = jnp.sqrt(-2.0 * th / (1.0 - th)) * (ig * xc)

    sub = lax.broadcasted_iota(jnp.int32, (ts, width), 0) % SUBLANES
    for d in (1, 2, 4):
        keep = sub >= d
        a_prev = jnp.where(keep, pltpu.roll(a, d, axis=0), 1.0)
        u_prev = jnp.where(keep, pltpu.roll(u, d, axis=0), 0.0)
        u = a * u_prev + u
        a = a * a_prev
    a_buf[...] = a
    u_buf[...] = u

    def group(g, h):
        base = pl.multiple_of(g * SUBLANES, SUBLANES)
        hp_buf[pl.ds(base, SUBLANES), :] = jnp.broadcast_to(h, (SUBLANES, width))
        last = base + SUBLANES - 1
        return a_buf[pl.ds(last, 1), :] * h + u_buf[pl.ds(last, 1), :]

    h_ref[...] = lax.fori_loop(0, ts // SUBLANES, group, h_ref[...])
    hseq = u_buf[...] + a_buf[...] * hp_buf[...]
    o_ref[...] = hseq * _gelu_tanh(gl_ref[...])


def _lru(xl, gl, conv_w, conv_b, wr_bd, br, wi_bd, bi, lam, batch, seq, ts):
    n, width = xl.shape
    nt = seq // ts
    gate_w = wr_bd.shape[-1]
    row = lambda b, t: (b * nt + t, 0)
    const2 = lambda b, t: (0, 0)
    const3 = lambda b, t: (0, 0, 0)
    vec = pl.BlockSpec((1, width), const2)
    return pl.pallas_call(
        functools.partial(_lru_kernel, ts=ts, gate_w=gate_w),
        grid=(batch, nt),
        in_specs=[pl.BlockSpec((ts, width), row), pl.BlockSpec((ts, width), row),
                  pl.BlockSpec((CONV_W, width), const2), vec,
                  pl.BlockSpec(wr_bd.shape, const3), vec,
                  pl.BlockSpec(wi_bd.shape, const3), vec, vec],
        out_specs=pl.BlockSpec((ts, width), row),
        out_shape=jax.ShapeDtypeStruct((n, width), F32),
        scratch_shapes=[pltpu.VMEM((ts + SUBLANES, width), F32),
                        pltpu.VMEM((ts, width), F32), pltpu.VMEM((ts, width), F32),
                        pltpu.VMEM((ts, width), F32), pltpu.VMEM((1, width), F32)],
        compiler_params=_cparams(("arbitrary", "arbitrary")),
        name="lru",
    )(xl, gl, conv_w, conv_b, wr_bd, br, wi_bd, bi, lam)


def _mix_route_kernel(sb_ref, lru_ref, x_ref, sbg_ref, lrug_ref, wo_ref, n2g_ref,
                      wrh_ref, wrl_ref, rb_ref,
                      h_ref, u2_ref, rr_ref, rwt_ref, tc_ref,
                      before_ref, *, tm):
    step = pl.program_id(0)
    half = sb_ref.shape[1]

    @pl.when(step == 0)
    def _():
        r_id = lax.broadcasted_iota(jnp.int32, (tm, tm), 0)
        c_id = lax.broadcasted_iota(jnp.int32, (tm, tm), 1)
        before_ref[...] = (r_id < c_id).astype(BF16)
        tc_ref[...] = jnp.zeros_like(tc_ref)

    m_sb = _rms_f32(sb_ref[...], sbg_ref[...]).astype(BF16)
    m_lru = _rms_f32(lru_ref[...], lrug_ref[...]).astype(BF16)
    h = (x_ref[...]
         + jnp.dot(m_sb, wo_ref[0:half, :], preferred_element_type=F32)
         + jnp.dot(m_lru, wo_ref[half:2 * half, :], preferred_element_type=F32))
    h_ref[...] = h
    u2 = _rms_f32(h, n2g_ref[...])
    u2_ref[...] = u2.astype(BF16)

    u_hi = u2.astype(BF16)
    u_lo = (u2 - u_hi.astype(F32)).astype(BF16)
    logits = (jnp.dot(u_hi, wrh_ref[...], preferred_element_type=F32)
              + jnp.dot(u_lo, wrh_ref[...], preferred_element_type=F32)
              + jnp.dot(u_hi, wrl_ref[...], preferred_element_type=F32))
    lt = logits.T + rb_ref[...]

    sub = lax.broadcasted_iota(jnp.int32, (SUBLANES, tm), 0)

    def top1(x):
        m = jnp.max(x, axis=0, keepdims=True)
        idx = jnp.min(jnp.where(x == m, sub, SUBLANES), axis=0, keepdims=True)
        return m, idx

    grp = lt[0:SUBLANES, :]
    g_max, g_idx = top1(grp)
    g_p = 1.0 / jnp.sum(jnp.exp(grp - g_max), axis=0, keepdims=True)
    fine = lt[SUBLANES:2 * SUBLANES, :]
    for g in range(1, N_GROUPS):
        fine = jnp.where(g_idx == g, lt[(g + 1) * SUBLANES:(g + 2) * SUBLANES, :], fine)
    m1, i1 = top1(fine)
    m2, i2 = top1(jnp.where(sub == i1, -jnp.inf, fine))
    e2 = jnp.exp(m2 - m1)
    p1 = 1.0 / (1.0 + e2)
    w1 = g_p * p1
    w2 = g_p * (e2 * p1)
    x1 = g_idx * EXPERTS_PER_GROUP + i1
    x2 = g_idx * EXPERTS_PER_GROUP + i2

    eid = lax.broadcasted_iota(jnp.int32, (N_EXPERTS, tm), 0)
    oh1 = jnp.where(eid == x1, 1.0, 0.0)
    oh2 = jnp.where(eid == x2, 1.0, 0.0)
    pre1 = jnp.dot(oh1.astype(BF16), before_ref[...], preferred_element_type=F32)
    pre2 = jnp.dot(oh2.astype(BF16), before_ref[...], preferred_element_type=F32)
    cnt1 = jnp.sum(oh1, axis=1, keepdims=True)
    cnt2 = jnp.sum(oh2, axis=1, keepdims=True)
    seg8 = jnp.floor((cnt1 + cnt2 + (SUBLANES - 1.0)) * (1.0 / SUBLANES))
    e_r = lax.broadcasted_iota(jnp.int32, (N_EXPERTS, N_EXPERTS), 0)
    e_c = lax.broadcasted_iota(jnp.int32, (N_EXPERTS, N_EXPERTS), 1)
    lower = jnp.where(e_c < e_r, 1.0, 0.0).astype(BF16)
    seg8_b = jnp.broadcast_to(seg8, (N_EXPERTS, LANES)).astype(BF16)
    seg_off = SUBLANES * jnp.dot(lower, seg8_b, preferred_element_type=F32)[:, 0:1]
    pos1 = jnp.sum(oh1 * (pre1 + seg_off), axis=0, keepdims=True)
    pos2 = jnp.sum(oh2 * (pre2 + (seg_off + cnt1)), axis=0, keepdims=True)

    lane = lax.broadcasted_iota(jnp.int32, tc_ref.shape, 1)
    seg_rows = jnp.broadcast_to(seg8 * SUBLANES, tc_ref.shape).astype(jnp.int32)
    tc_ref[...] = jnp.where(lane == step, seg_rows, tc_ref[...])

    zrow = jnp.zeros((SUBLANES - 4, tm), jnp.int32)
    rr_ref[...] = jnp.concatenate(
        [pos1.astype(jnp.int32), pos2.astype(jnp.int32), x1, x2, zrow], axis=0)
    wt = jnp.concatenate([w1, w2, pos1, pos2, jnp.zeros((LANES - 4, tm), F32)], axis=0)
    rwt_ref[...] = wt.T


def _mix_route(sb, lru, x2, sbg, lrug, wo_bf, n2g, wr_hi, wr_lo, rbias, tm):
    n, d = x2.shape
    half = sb.shape[1]
    row = lambda i: (i, 0)
    const = lambda i: (0, 0)
    return pl.pallas_call(
        functools.partial(_mix_route_kernel, tm=tm),
        grid=(n // tm,),
        in_specs=[pl.BlockSpec((tm, half), row), pl.BlockSpec((tm, half), row),
                  pl.BlockSpec((tm, d), row),
                  pl.BlockSpec((1, half), const), pl.BlockSpec((1, half), const),
                  pl.BlockSpec(wo_bf.shape, const), pl.BlockSpec((1, d), const),
                  pl.BlockSpec(wr_hi.shape, const), pl.BlockSpec(wr_lo.shape, const),
                  pl.BlockSpec(rbias.shape, const)],
        out_specs=[pl.BlockSpec((tm, d), row), pl.BlockSpec((tm, d), row),
                   pl.BlockSpec((SUBLANES, tm), lambda i: (0, i)),
                   pl.BlockSpec((tm, LANES), row),
                   pl.BlockSpec((N_EXPERTS, LANES), const)],
        out_shape=[jax.ShapeDtypeStruct((n, d), F32), jax.ShapeDtypeStruct((n, d), BF16),
                   jax.ShapeDtypeStruct((SUBLANES, n), jnp.int32),
                   jax.ShapeDtypeStruct((n, LANES), F32),
                   jax.ShapeDtypeStruct((N_EXPERTS, LANES), jnp.int32)],
        scratch_shapes=[pltpu.VMEM((tm, tm), BF16)],
        compiler_params=_cparams(("arbitrary",)),
        name="mix_route",
    )(sb, lru, x2, sbg, lrug, wo_bf, n2g, wr_hi, wr_lo, rbias)


HIGH_HALF = -65536


def _pack_halves(x):
    half = x.shape[1] // 2
    lo = lax.shift_right_logical(lax.bitcast_convert_type(x[:, :half], jnp.int32), 16)
    hi = lax.bitcast_convert_type(x[:, half:], jnp.int32) & HIGH_HALF
    return hi | lo


def _unpack_halves(p):
    lo = lax.bitcast_convert_type(lax.shift_left(p, 16), F32)
    hi = lax.bitcast_convert_type(p & HIGH_HALF, F32)
    return lo.astype(BF16), hi.astype(BF16)


def _segment_copies(tile, c8_ref, loff_ref, goff_ref, make):
    for e in range(N_EXPERTS):
        idx = tile * N_EXPERTS + e
        rows = pl.multiple_of(c8_ref[idx], SUBLANES)

        @pl.when(rows > 0)
        def _(idx=idx, rows=rows):
            lo = pl.multiple_of(loff_ref[idx], SUBLANES)
            go = pl.multiple_of(goff_ref[idx], SUBLANES)
            make(pl.ds(lo, rows), pl.ds(go, rows)).start()


def _dispatch_kernel(c8_ref, loff_ref, goff_ref, tot_ref, used_ref,
                     rr_ref, u2_ref, xs_ref, lbuf, zbuf, sem, zsem, *, td, lrows):
    i = pl.program_id(0)
    slot = i % 2

    r_id = lax.broadcasted_iota(jnp.int32, (lrows, td), 0)
    perm = jnp.where(r_id == rr_ref[0:1, :], 1.0, jnp.where(r_id == rr_ref[1:2, :], 1.0, 0.0))
    sorted_rows = jnp.dot(perm.astype(BF16), u2_ref[...], preferred_element_type=F32)
    lbuf[slot] = _pack_halves(sorted_rows)

    @pl.when(i == 0)
    def _():
        zbuf[...] = jnp.zeros_like(zbuf)
        chunk = zbuf.shape[0]
        used = used_ref[0]
        spare = xs_ref.shape[0] - used
        n_fill = (spare + chunk - 1) // chunk

        def fill_copy(k):
            rows = pl.multiple_of(jnp.minimum(chunk, spare - k * chunk), SUBLANES)
            start = pl.multiple_of(used + k * chunk, SUBLANES)
            return pltpu.make_async_copy(zbuf.at[pl.ds(0, rows)], xs_ref.at[pl.ds(start, rows)],
                                         zsem)

        def fill_start(k, c):
            fill_copy(k).start()
            return c

        def fill_wait(k, c):
            fill_copy(k).wait()
            return c

        lax.fori_loop(0, n_fill, fill_start, 0)
        lax.fori_loop(0, n_fill, fill_wait, 0)

    def wait_tile(tile, s):
        rows = pl.multiple_of(tot_ref[tile], SUBLANES)
        pltpu.make_async_copy(lbuf.at[s, pl.ds(0, rows)], xs_ref.at[pl.ds(0, rows)], sem).wait()

    @pl.when(i > 0)
    def _():
        wait_tile(i - 1, 1 - slot)

    _segment_copies(i, c8_ref, loff_ref, goff_ref,
                    lambda loc, glob: pltpu.make_async_copy(lbuf.at[slot, loc], xs_ref.at[glob], sem))

    @pl.when(i == pl.num_programs(0) - 1)
    def _():
        wait_tile(i, slot)


def _dispatch(c8, loff, goff, tot, used, rr, u2, p_rows, td, lrows, tme):
    n, d = u2.shape
    pmap = lambda i, *_: (0, i)
    return pl.pallas_call(
        functools.partial(_dispatch_kernel, td=td, lrows=lrows),
        grid_spec=pltpu.PrefetchScalarGridSpec(
            num_scalar_prefetch=5,
            grid=(n // td,),
            in_specs=[pl.BlockSpec((SUBLANES, td), pmap),
                      pl.BlockSpec((td, d), lambda i, *_: (i, 0))],
            out_specs=pl.BlockSpec(memory_space=pl.ANY),
            scratch_shapes=[pltpu.VMEM((2, lrows, d // 2), jnp.int32),
                            pltpu.VMEM((tme, d // 2), jnp.int32),
                            pltpu.SemaphoreType.DMA(()), pltpu.SemaphoreType.DMA(())]),
        out_shape=jax.ShapeDtypeStruct((p_rows, d // 2), jnp.int32),
        compiler_params=_cparams(("arbitrary",)),
        name="dispatch",
    )(c8, loff, goff, tot, used, rr, u2)


def _experts_kernel(eoff_ref, erows_ref, xs_ref, wg_ref, wu_ref, wd_ref, ys_ref,
                    wg_bf, wu_bf, wd_bf, xbuf, ybuf, sem_in, sem_out, state, *, tme):
    e = pl.program_id(0)
    n_experts = pl.num_programs(0)
    rows = erows_ref[e]
    off = eoff_ref[e]
    n_tiles = (rows + tme - 1) // tme

    def tile_rows(total, k):
        return pl.multiple_of(jnp.minimum(tme, total - k * tme), SUBLANES)

    def in_copy(start, r, slot):
        start = pl.multiple_of(start, SUBLANES)
        return pltpu.make_async_copy(xs_ref.at[pl.ds(start, r)], xbuf.at[slot, pl.ds(0, r)],
                                     sem_in.at[slot])

    def out_copy(start, r, slot):
        start = pl.multiple_of(start, SUBLANES)
        return pltpu.make_async_copy(ybuf.at[slot, pl.ds(0, r)], ys_ref.at[pl.ds(start, r)],
                                     sem_out.at[slot])

    @pl.when(e == 0)
    def _():
        for s in range(4):
            state[s] = 0
        xbuf[...] = jnp.zeros_like(xbuf)

    @pl.when(rows > 0)
    def _():
        wg_bf[...] = wg_ref[0].astype(BF16)
        wu_bf[...] = wu_ref[0].astype(BF16)
        wd_bf[...] = wd_ref[0].astype(BF16)
        done = state[0]

        @pl.when(state[1] == 0)
        def _():
            in_copy(off, tile_rows(rows, 0), done % 2).start()

        def tile(k, c):
            slot = (done + k) % 2
            r = tile_rows(rows, k)

            @pl.when(k + 1 < n_tiles)
            def _():
                in_copy(off + (k + 1) * tme, tile_rows(rows, k + 1), 1 - slot).start()

            in_copy(off, r, slot).wait()

            @pl.when(done + k >= 2)
            def _():
                out_copy(0, pl.multiple_of(state[2 + slot], SUBLANES), slot).wait()

            def mlp(n):
                x_lo, x_hi = _unpack_halves(xbuf[slot, 0:n])
                half = x_lo.shape[1]
                hg = (jnp.dot(x_lo, wg_bf[0:half, :], preferred_element_type=F32)
                      + jnp.dot(x_hi, wg_bf[half:2 * half, :], preferred_element_type=F32))
                hu = (jnp.dot(x_lo, wu_bf[0:half, :], preferred_element_type=F32)
                      + jnp.dot(x_hi, wu_bf[half:2 * half, :], preferred_element_type=F32))
                act = (hg * jax.nn.sigmoid(hg) * hu).astype(BF16)
                y = jnp.dot(act, wd_bf[...], preferred_element_type=F32)
                ybuf[slot, 0:n] = _pack_halves(y.astype(BF16).astype(F32))

            @pl.when(r > tme // 2)
            def _():
                mlp(tme)

            @pl.when(r <= tme // 2)
            def _():
                mlp(tme // 2)

            out_copy(off + k * tme, r, slot).start()
            state[2 + slot] = r
            return c

        lax.fori_loop(0, n_tiles, tile, 0)
        state[0] = done + n_tiles

    nxt = jnp.minimum(e + 1, n_experts - 1)
    prefetch = (rows > 0) & (e + 1 < n_experts) & (erows_ref[nxt] > 0)
    state[1] = prefetch.astype(jnp.int32)

    @pl.when(prefetch)
    def _():
        in_copy(eoff_ref[nxt], tile_rows(erows_ref[nxt], 0), state[0] % 2).start()

    @pl.when(e == n_experts - 1)
    def _():
        total = eoff_ref[e] + rows
        for slot in range(2):
            @pl.when(state[0] > slot)
            def _(slot=slot):
                out_copy(0, pl.multiple_of(state[2 + slot], SUBLANES), slot).wait()
        ybuf[0] = jnp.zeros(ybuf.shape[1:], ybuf.dtype)
        spare = ys_ref.shape[0] - total
        n_fill = (spare + tme - 1) // tme

        def fill_copy(k):
            return out_copy(total + k * tme, tile_rows(spare, k), 0)

        def fill_start(k, c):
            fill_copy(k).start()
            return c

        def fill_wait(k, c):
            fill_copy(k).wait()
            return c

        lax.fori_loop(0, n_fill, fill_start, 0)
        lax.fori_loop(0, n_fill, fill_wait, 0)


def _experts(eoff, erows, xs, wg, wu, wd, tme):
    p = xs.shape[0]
    n_experts, d, de = wg.shape
    wmap = lambda e, *_: (e, 0, 0)
    return pl.pallas_call(
        functools.partial(_experts_kernel, tme=tme),
        grid_spec=pltpu.PrefetchScalarGridSpec(
            num_scalar_prefetch=2,
            grid=(n_experts,),
            in_specs=[pl.BlockSpec(memory_space=pl.ANY),
                      pl.BlockSpec((1, d, de), wmap), pl.BlockSpec((1, d, de), wmap),
                      pl.BlockSpec((1, de, d), wmap)],
            out_specs=pl.BlockSpec(memory_space=pl.ANY),
            scratch_shapes=[pltpu.VMEM((d, de), BF16), pltpu.VMEM((d, de), BF16),
                            pltpu.VMEM((de, d), BF16),
                            pltpu.VMEM((2, tme, d // 2), jnp.int32),
                            pltpu.VMEM((2, tme, d // 2), jnp.int32),
                            pltpu.SemaphoreType.DMA((2,)), pltpu.SemaphoreType.DMA((2,)),
                            pltpu.SMEM((4,), jnp.int32)]),
        out_shape=jax.ShapeDtypeStruct((p, d // 2), jnp.int32),
        compiler_params=_cparams(("arbitrary",)),
        name="experts",
    )(eoff, erows, xs, wg, wu, wd)


def _combine_kernel(c8_ref, loff_ref, goff_ref, tot_ref,
                    rwt_ref, h_ref, fg_ref, ys_ref, y_ref, ybuf, sems, *, tc, lrows):
    i = pl.program_id(0)
    slot = i % 2

    def gather_tile(tile, s):
        _segment_copies(tile, c8_ref, loff_ref, goff_ref,
                        lambda loc, glob: pltpu.make_async_copy(ys_ref.at[glob], ybuf.at[s, loc],
                                                                sems.at[s]))

    @pl.when(i == 0)
    def _():
        ybuf[...] = jnp.zeros_like(ybuf)
        gather_tile(0, 0)

    @pl.when(i + 1 < pl.num_programs(0))
    def _():
        gather_tile(i + 1, 1 - slot)

    rows = pl.multiple_of(tot_ref[i], SUBLANES)
    pltpu.make_async_copy(ys_ref.at[pl.ds(0, rows)], ybuf.at[slot, pl.ds(0, rows)],
                          sems.at[slot]).wait()

    w = rwt_ref[...]
    c_id = lax.broadcasted_iota(jnp.int32, (tc, lrows), 1)
    pos1 = w[:, 2:3].astype(jnp.int32)
    pos2 = w[:, 3:4].astype(jnp.int32)
    wmat = (jnp.where(c_id == pos1, w[:, 0:1], 0.0)
            + jnp.where(c_id == pos2, w[:, 1:2], 0.0)).astype(BF16)
    y_lo, y_hi = _unpack_halves(ybuf[slot])
    moe = jnp.concatenate([jnp.dot(wmat, y_lo, preferred_element_type=F32),
                           jnp.dot(wmat, y_hi, preferred_element_type=F32)], axis=-1)
    y_ref[...] = _rms_f32(h_ref[...] + moe, fg_ref[...])


def _combine(c8, loff, goff, tot, rwt, h, final_g, ys, tc, lrows):
    n, d = h.shape
    return pl.pallas_call(
        functools.partial(_combine_kernel, tc=tc, lrows=lrows),
        grid_spec=pltpu.PrefetchScalarGridSpec(
            num_scalar_prefetch=4,
            grid=(n // tc,),
            in_specs=[pl.BlockSpec((tc, LANES), lambda i, *_: (i, 0)),
                      pl.BlockSpec((tc, d), lambda i, *_: (i, 0)),
                      pl.BlockSpec((1, d), lambda i, *_: (0, 0)),
                      pl.BlockSpec(memory_space=pl.ANY)],
            out_specs=pl.BlockSpec((tc, d), lambda i, *_: (i, 0)),
            scratch_shapes=[pltpu.VMEM((2, lrows, d // 2), jnp.int32),
                            pltpu.SemaphoreType.DMA((2,))]),
        out_shape=jax.ShapeDtypeStruct((n, d), F32),
        compiler_params=_cparams(("arbitrary",)),
        name="combine",
    )(c8, loff, goff, tot, rwt, h, final_g, ys)


def _block_diag(w, per):
    nb, c, _ = w.shape
    eye = jnp.eye(per, dtype=w.dtype)
    wg = w.reshape(nb // per, per, c, c)
    return jnp.einsum("gpij,pq->gpiqj", wg, eye).reshape(nb // per, per * c, per * c)


def _router_tables(w_group, b_group, w_fine, b_fine):
    d = w_group.shape[0]
    w = jnp.zeros((d, LANES), F32)
    w = w.at[:, 0:N_GROUPS].set(w_group).at[:, SUBLANES:SUBLANES + N_EXPERTS].set(w_fine)
    b = jnp.full((LANES,), NEG_BIG, F32)
    b = b.at[0:N_GROUPS].set(b_group).at[SUBLANES:SUBLANES + N_EXPERTS].set(b_fine)
    w_hi = w.astype(BF16)
    w_lo = (w - w_hi.astype(F32)).astype(BF16)
    return w_hi, w_lo, b.reshape(LANES, 1)


def kernel(x, norm1_g, w_in, conv_w, conv_b, w_rgate, b_rgate, w_igate, b_igate, lam, sb_norm_g,
           lru_norm_g, w_out, norm2_g, w_group, b_group, w_fine, b_fine, w_e_gate, w_e_up,
           w_e_down, final_g):
    batch, seq, d = x.shape
    n = batch * seq
    width = w_in.shape[1] // 5
    tm = min(512, seq)
    tme = 512
    gate_per = 256 // w_rgate.shape[1]

    x2 = x.reshape(n, d)
    vec = lambda a: a.reshape(1, -1)

    q, k, v, xl, gl = _in_proj(x2, vec(norm1_g), w_in.astype(BF16), width, min(2 * tm, seq))
    out_sb = _attention(q, k, v, batch, seq, LANES, ATTN_WINDOW_BLOCKS)
    out_lru = _lru(xl, gl, conv_w, vec(conv_b),
                   _block_diag(w_rgate, gate_per).astype(BF16), vec(b_rgate),
                   _block_diag(w_igate, gate_per).astype(BF16), vec(b_igate),
                   vec(lam), batch, seq, min(256, seq))

    wr_hi, wr_lo, rbias = _router_tables(w_group, b_group, w_fine, b_fine)
    h, u2, rr, rwt, tcnt = _mix_route(out_sb, out_lru, x2, vec(sb_norm_g), vec(lru_norm_g),
                                      w_out.astype(BF16), vec(norm2_g), wr_hi, wr_lo, rbias, tm)

    n_tiles = n // tm
    assert n_tiles <= LANES, "one lane of the per-tile count table per token tile"
    c8 = tcnt[:, :n_tiles].T
    erows = jnp.sum(c8, axis=0)
    eoff = jnp.cumsum(erows) - erows
    goff = eoff[None, :] + jnp.cumsum(c8, axis=0) - c8
    loff = jnp.cumsum(c8, axis=1) - c8
    tot = jnp.sum(c8, axis=1)
    lrows = 2 * tm + N_EXPERTS * SUBLANES
    p_rows = 2 * n + n_tiles * N_EXPERTS * (SUBLANES - 1)
    p_rows = -(-p_rows // SUBLANES) * SUBLANES
    i32 = lambda a: a.reshape(-1).astype(jnp.int32)
    c8, loff, goff, tot, eoff, erows = (i32(a) for a in (c8, loff, goff, tot, eoff, erows))

    xs = _dispatch(c8, loff, goff, tot, jnp.sum(erows, keepdims=True), rr, u2, p_rows, tm, lrows, tme)
    ys = _experts(eoff, erows, xs, w_e_gate, w_e_up, w_e_down, tme)
    y = _combine(c8, loff, goff, tot, rwt, h, vec(final_g), ys, tm, lrows)
    return y.reshape(batch, seq, d)
```

```python
import functools
import math

import jax
import jax.numpy as jnp
from jax import lax
from jax.experimental import pallas as pl
from jax.experimental.pallas import tpu as pltpu

F32 = jnp.float32
BF16 = jnp.bfloat16

EPS = 1e-6
HEAD_DIM = 64
HEADS_PER_BLOCK = 2
LANES = 128
SUBLANES = 8
CONV_W = 4
RG_C = 8.0
N_GROUPS = 4
EXPERTS_PER_GROUP = 8
N_EXPERTS = N_GROUPS * EXPERTS_PER_GROUP
NEG_BIG = -1e30
LOG2_E = math.log2(math.e)
ATTN_STOP = 104.0 * LOG2_E
ATTN_WINDOW_BLOCKS = 3
ATTN_UNROLL = 4
HIGH_HALF = -65536

VMEM_LIMIT = 56 * 1024 * 1024


def _cparams(sem):
    return pltpu.CompilerParams(dimension_semantics=sem, vmem_limit_bytes=VMEM_LIMIT)


def _rms_f32(x, g):
    return x * lax.rsqrt(jnp.mean(x * x, axis=-1, keepdims=True) + EPS) * g


def _in_proj_kernel(x_ref, g_ref, w_ref, q_ref, k_ref, v_ref, xl_ref, gl_ref, *, width, q_scale):
    u = _rms_f32(x_ref[...], g_ref[...]).astype(BF16)
    for c, o_ref in enumerate((q_ref, k_ref, v_ref, xl_ref, gl_ref)):
        p = jnp.dot(u, w_ref[:, c * width:(c + 1) * width], preferred_element_type=F32)
        if c == 0:
            p = p * q_scale
        o_ref[...] = p.astype(o_ref.dtype)


def _in_proj(x2, g, w_bf, width, tm):
    n, d = x2.shape
    row = lambda i: (i, 0)
    out_bf = jax.ShapeDtypeStruct((n, width), BF16)
    out_f = jax.ShapeDtypeStruct((n, width), F32)
    return pl.pallas_call(
        functools.partial(_in_proj_kernel, width=width, q_scale=1.0 / math.sqrt(HEAD_DIM)),
        grid=(n // tm,),
        in_specs=[pl.BlockSpec((tm, d), row),
                  pl.BlockSpec((1, d), lambda i: (0, 0)),
                  pl.BlockSpec(w_bf.shape, lambda i: (0, 0))],
        out_specs=[pl.BlockSpec((tm, width), row)] * 5,
        out_shape=[out_bf, out_bf, out_bf, out_f, out_f],
        compiler_params=_cparams(("arbitrary",)),
        name="in_proj",
    )(x2, g, w_bf)


def _attn_kernel(q_ref, k_ref, v_ref, o_ref, tri_ref, z_ref, arg_ref, ctot_ref, acc_ref, carry_ref,
                 *, tq, nwin):
    seq = q_ref.shape[0]
    win = nwin * tq
    lane = lax.broadcasted_iota(jnp.int32, (1, LANES), 1)
    r_id = lax.broadcasted_iota(jnp.int32, (tq, tq), 0)
    c_id = lax.broadcasted_iota(jnp.int32, (tq, tq), 1)
    causal = c_id < r_id
    rel = c_id - r_id

    k_r = lax.broadcasted_iota(jnp.int32, (2 * tq, 2 * tq), 0) % tq
    k_c = lax.broadcasted_iota(jnp.int32, (2 * tq, 2 * tq), 1)
    tri_ref[...] = jnp.where(k_c >= tq, 1.0, jnp.where(k_r > k_c, 1.0, 0.0)).astype(BF16)

    def softplus2(z):
        return jnp.maximum(z, 0.0) + jnp.log2(1.0 + jnp.exp2(-jnp.abs(z)))

    def scores(qh, keys):
        z = LOG2_E * lax.dot_general(qh, keys, (((1,), (1,)), ((), ())),
                                     preferred_element_type=F32)
        nlog_nb = softplus2(z)
        return nlog_nb, z - nlog_nb

    def suffix(nlog_nb):
        hi = lax.bitcast_convert_type(lax.bitcast_convert_type(nlog_nb, jnp.int32) & HIGH_HALF, F32)
        lo = (nlog_nb - hi).astype(BF16)
        hi = hi.astype(BF16)
        r = jnp.dot(jnp.concatenate([hi, lo], axis=1), tri_ref[...], preferred_element_type=F32)
        return r[:, :tq], r[:, tq:]

    def window(qh, wstart, masks):
        nlog_nb, log_b = scores(qh, k_ref[pl.ds(wstart, win), :])
        carry = None
        parts = [None] * nwin
        for b in reversed(range(nwin)):
            nl = nlog_nb[:, b * tq:(b + 1) * tq]
            if masks[b] is not None:
                nl = jnp.where(masks[b], nl, 0.0)
            excl, tot = suffix(nl)
            arg = log_b[:, b * tq:(b + 1) * tq] - excl
            a = jnp.exp2(arg if carry is None else arg - carry)
            if masks[b] is not None:
                a = jnp.where(masks[b], a, 0.0)
            parts[b] = a.astype(BF16)
            carry = tot if carry is None else carry + tot
        out = jnp.dot(jnp.concatenate(parts, axis=1), v_ref[pl.ds(wstart, win), :],
                      preferred_element_type=F32)
        return out, carry

    def head_queries(i):
        q = q_ref[pl.ds(i * tq, tq), :]
        return [jnp.where((lane >= h * HEAD_DIM) & (lane < (h + 1) * HEAD_DIM), q, jnp.zeros_like(q))
                for h in range(HEADS_PER_BLOCK)]

    def store(i, outs):
        o_ref[pl.ds(i * tq, tq), :] = jnp.where(lane < HEAD_DIM, outs[0], outs[1])

    for i in range(nwin - 1):
        masks = [(rel + b * tq) < i * tq for b in range(nwin)]
        store(i, [window(qh, 0, masks)[0] for qh in head_queries(i)])

    def window_start(i):
        return pl.multiple_of((i - (nwin - 1)) * tq, tq)

    def stage_scores(i, p):
        keys = k_ref[pl.ds(window_start(i), win), :]
        for h, qh in enumerate(head_queries(i)):
            z_ref[p, h] = LOG2_E * lax.dot_general(qh, keys, (((1,), (1,)), ((), ())),
                                                   preferred_element_type=F32)

    def stage_exponents(p):
        carries = []
        for h in range(HEADS_PER_BLOCK):
            z = z_ref[p, h]
            softplus = softplus2(z)
            carry = None
            for b in reversed(range(nwin)):
                diag = b == nwin - 1
                cols = slice(b * tq, (b + 1) * tq)
                nl = softplus[:, cols]
                if diag:
                    nl = jnp.where(causal, nl, 0.0)
                excl, tot = suffix(nl)
                arg = z[:, cols] - softplus[:, cols] - excl
                if carry is not None:
                    arg = arg - carry
                if diag:
                    arg = jnp.where(causal, arg, NEG_BIG)
                arg_ref[p, h, :, cols] = arg
                carry = tot if carry is None else carry + tot
            ctot_ref[p, h] = carry
            carries.append(carry)
        return jnp.min(jnp.minimum(carries[0], carries[1]))

    def stage_output(i, p, s):
        vals = v_ref[pl.ds(window_start(i), win), :]
        for h in range(HEADS_PER_BLOCK):
            a = jnp.exp2(arg_ref[p, h]).astype(BF16)
            acc_ref[s, h] = jnp.dot(a, vals, preferred_element_type=F32)
            carry_ref[s, h] = ctot_ref[p, h]

    def finish(i, s, cmin):
        def cond(state):
            j, cmin = state
            return (j >= 0) & (cmin <= ATTN_STOP)

        def older(state):
            j, _ = state
            start = pl.multiple_of(j * tq, tq)
            keys = k_ref[pl.ds(start, tq), :]
            vals = v_ref[pl.ds(start, tq), :]
            cs = []
            for h, qh in enumerate(head_queries(i)):
                nlog_nb, log_b = scores(qh, keys)
                excl, tot = suffix(nlog_nb)
                carry = carry_ref[s, h]
                a = jnp.exp2(log_b - excl - carry)
                acc_ref[s, h] += jnp.dot(a.astype(BF16), vals, preferred_element_type=F32)
                carry_ref[s, h] = carry + tot
                cs.append(carry + tot)
            return j - 1, jnp.min(jnp.minimum(cs[0], cs[1]))

        lax.while_loop(cond, older, (i - nwin, cmin))
        store(i, [acc_ref[s, 0], acc_ref[s, 1]])

    first = nwin - 1
    n_blocks = seq // tq
    unroll = acc_ref.shape[0]
    slot = lambda i: (first + i) % 2
    stage_scores(first, slot(0))
    stage_scores(first + 1, slot(1))
    cmin = stage_exponents(slot(0))

    def steady(m, cmin):
        t = first + 2 + unroll * m
        cmins = [cmin]
        for u in range(unroll):
            stage_output(t - 2 + u, slot(u), u)
            cmins.append(stage_exponents(slot(u + 1)))
            stage_scores(t + u, slot(u))
        for u in range(unroll):
            finish(t - 2 + u, u, cmins[u])
        return cmins[unroll]

    cmin = lax.fori_loop(0, (n_blocks - first - 2) // unroll, steady, cmin)
    stage_output(n_blocks - 2, slot(0), 0)
    cmin_last = stage_exponents(slot(1))
    finish(n_blocks - 2, 0, cmin)
    stage_output(n_blocks - 1, slot(1), 1)
    finish(n_blocks - 1, 1, cmin_last)


def _attention(q, k, v, batch, seq, tq, nwin, unroll):
    n, width = q.shape
    assert tq == LANES and seq >= (nwin + 1) * tq
    assert unroll % 2 == 0 and (seq // tq - (nwin + 1)) % unroll == 0
    blk = pl.BlockSpec((seq, LANES), lambda b, hp: (b, hp))
    stage_buf = pltpu.VMEM((2, HEADS_PER_BLOCK, tq, nwin * tq), F32)
    carry_buf = pltpu.VMEM((2, HEADS_PER_BLOCK, tq, tq), F32)
    row_buf = pltpu.VMEM((unroll, HEADS_PER_BLOCK, tq, LANES), F32)
    return pl.pallas_call(
        functools.partial(_attn_kernel, tq=tq, nwin=nwin),
        grid=(batch, width // LANES),
        in_specs=[blk, blk, blk],
        out_specs=blk,
        out_shape=jax.ShapeDtypeStruct((n, width), F32),
        scratch_shapes=[pltpu.VMEM((2 * tq, 2 * tq), BF16),
                        stage_buf, stage_buf, carry_buf, row_buf, row_buf],
        compiler_params=_cparams(("arbitrary", "arbitrary")),
        name="attn",
    )(q, k, v)


def _gelu_tanh(x):
    return 0.5 * x * (1.0 + jnp.tanh(math.sqrt(2.0 / math.pi) * (x + 0.044715 * (x * x * x))))


def _lru_kernel(xl_ref, gl_ref, cw_ref, cb_ref, wr_ref, br_ref, wi_ref, bi_ref, lam_ref, o_ref,
                xbuf, a_buf, u_buf, hp_buf, h_ref, *, ts, gate_w):
    t = pl.program_id(1)
    width = xl_ref.shape[1]

    @pl.when(t == 0)
    def _():
        xbuf[0:SUBLANES, :] = jnp.zeros((SUBLANES, width), F32)
        h_ref[...] = jnp.zeros_like(h_ref)

    xbuf[SUBLANES:SUBLANES + ts, :] = xl_ref[...]
    xc = cb_ref[...]
    for w in range(CONV_W):
        xc = xc + xbuf[pl.ds(SUBLANES - (CONV_W - 1) + w, ts), :] * cw_ref[w:w + 1, :]
    xbuf[0:SUBLANES, :] = xbuf[ts:ts + SUBLANES, :]

    xcb = xc.astype(BF16)
    r_parts, i_parts = [], []
    for c in range(width // gate_w):
        xs = xcb[:, c * gate_w:(c + 1) * gate_w]
        r_parts.append(jnp.dot(xs, wr_ref[c], preferred_element_type=F32))
        i_parts.append(jnp.dot(xs, wi_ref[c], preferred_element_type=F32))
    r = jax.nn.sigmoid(jnp.concatenate(r_parts, axis=-1) + br_ref[...])
    ig = jax.nn.sigmoid(jnp.concatenate(i_parts, axis=-1) + bi_ref[...])
    lam = lam_ref[...]
    log_sig_lam = -(jnp.maximum(-lam, 0.0) + jnp.log1p(jnp.exp(-jnp.abs(lam))))
    log_a = RG_C * r * log_sig_lam
    a = jnp.exp(log_a)
    th = jnp.tanh(log_a)
    u = jnp.sqrt(-2.0 * th / (1.0 - th)) * (ig * xc)

    sub = lax.broadcasted_iota(jnp.int32, (ts, width), 0) % SUBLANES
    for d in (1, 2, 4):
        keep = sub >= d
        a_prev = jnp.where(keep, pltpu.roll(a, d, axis=0), 1.0)
        u_prev = jnp.where(keep, pltpu.roll(u, d, axis=0), 0.0)
        u = a * u_prev + u
        a = a * a_prev
    a_buf[...] = a
    u_buf[...] = u

    def group(g, h):
        base = pl.multiple_of(g * SUBLANES, SUBLANES)
        hp_buf[pl.ds(base, SUBLANES), :] = jnp.broadcast_to(h, (SUBLANES, width))
        last = base + SUBLANES - 1
        return a_buf[pl.ds(last, 1), :] * h + u_buf[pl.ds(last, 1), :]

    h_ref[...] = lax.fori_loop(0, ts // SUBLANES, group, h_ref[...])
    hseq = u_buf[...] + a_buf[...] * hp_buf[...]
    o_ref[...] = hseq * _gelu_tanh(gl_ref[...])


def _lru(xl, gl, conv_w, conv_b, wr_bd, br, wi_bd, bi, lam, batch, seq, ts):
    n, width = xl.shape
    nt = seq // ts
    gate_w = wr_bd.shape[-1]
    row = lambda b, t: (b * nt + t, 0)
    const2 = lambda b, t: (0, 0)
    const3 = lambda b, t: (0, 0, 0)
    vec = pl.BlockSpec((1, width), const2)
    return pl.pallas_call(
        functools.partial(_lru_kernel, ts=ts, gate_w=gate_w),
        grid=(batch, nt),
        in_specs=[pl.BlockSpec((ts, width), row), pl.BlockSpec((ts, width), row),
                  pl.BlockSpec((CONV_W, width), const2), vec,
                  pl.BlockSpec(wr_bd.shape, const3), vec,
                  pl.BlockSpec(wi_bd.shape, const3), vec, vec],
        out_specs=pl.BlockSpec((ts, width), row),
        out_shape=jax.ShapeDtypeStruct((n, width), F32),
        scratch_shapes=[pltpu.VMEM((ts + SUBLANES, width), F32),
                        pltpu.VMEM((ts, width), F32), pltpu.VMEM((ts, width), F32),
                        pltpu.VMEM((ts, width), F32), pltpu.VMEM((1, width), F32)],
        compiler_params=_cparams(("arbitrary", "arbitrary")),
        name="lru",
    )(xl, gl, conv_w, conv_b, wr_bd, br, wi_bd, bi, lam)


def _mix_route_kernel(sb_ref, lru_ref, x_ref, sbg_ref, lrug_ref, wo_ref, n2g_ref,
                      wrh_ref, wrl_ref, rb_ref,
                      h_ref, u2_ref, rr_ref, rwt_ref, tc_ref,
                      before_ref, *, tm):
    step = pl.program_id(0)
    half = sb_ref.shape[1]

    @pl.when(step == 0)
    def _():
        r_id = lax.broadcasted_iota(jnp.int32, (tm, tm), 0)
        c_id = lax.broadcasted_iota(jnp.int32, (tm, tm), 1)
        before_ref[...] = (r_id < c_id).astype(BF16)
        tc_ref[...] = jnp.zeros_like(tc_ref)

    m_sb = _rms_f32(sb_ref[...], sbg_ref[...]).astype(BF16)
    m_lru = _rms_f32(lru_ref[...], lrug_ref[...]).astype(BF16)
    h = (x_ref[...]
         + jnp.dot(m_sb, wo_ref[0:half, :], preferred_element_type=F32)
         + jnp.dot(m_lru, wo_ref[half:2 * half, :], preferred_element_type=F32))
    h_ref[...] = h
    u2 = _rms_f32(h, n2g_ref[...])
    u2_ref[...] = u2.astype(BF16)

    u_hi = u2.astype(BF16)
    u_lo = (u2 - u_hi.astype(F32)).astype(BF16)
    logits = (jnp.dot(u_hi, wrh_ref[...], preferred_element_type=F32)
              + jnp.dot(u_lo, wrh_ref[...], preferred_element_type=F32)
              + jnp.dot(u_hi, wrl_ref[...], preferred_element_type=F32))
    lt = logits.T + rb_ref[...]

    sub = lax.broadcasted_iota(jnp.int32, (SUBLANES, tm), 0)

    def top1(x):
        m = jnp.max(x, axis=0, keepdims=True)
        idx = jnp.min(jnp.where(x == m, sub, SUBLANES), axis=0, keepdims=True)
        return m, idx

    grp = lt[0:SUBLANES, :]
    g_max, g_idx = top1(grp)
    g_p = 1.0 / jnp.sum(jnp.exp(grp - g_max), axis=0, keepdims=True)
    fine = lt[SUBLANES:2 * SUBLANES, :]
    for g in range(1, N_GROUPS):
        fine = jnp.where(g_idx == g, lt[(g + 1) * SUBLANES:(g + 2) * SUBLANES, :], fine)
    m1, i1 = top1(fine)
    m2, i2 = top1(jnp.where(sub == i1, -jnp.inf, fine))
    e2 = jnp.exp(m2 - m1)
    p1 = 1.0 / (1.0 + e2)
    w1 = g_p * p1
    w2 = g_p * (e2 * p1)
    x1 = g_idx * EXPERTS_PER_GROUP + i1
    x2 = g_idx * EXPERTS_PER_GROUP + i2

    eid = lax.broadcasted_iota(jnp.int32, (N_EXPERTS, tm), 0)
    oh1 = jnp.where(eid == x1, 1.0, 0.0)
    oh2 = jnp.where(eid == x2, 1.0, 0.0)
    pre1 = jnp.dot(oh1.astype(BF16), before_ref[...], preferred_element_type=F32)
    pre2 = jnp.dot(oh2.astype(BF16), before_ref[...], preferred_element_type=F32)
    cnt1 = jnp.sum(oh1, axis=1, keepdims=True)
    cnt2 = jnp.sum(oh2, axis=1, keepdims=True)
    seg8 = jnp.floor((cnt1 + cnt2 + (SUBLANES - 1.0)) * (1.0 / SUBLANES))
    e_r = lax.broadcasted_iota(jnp.int32, (N_EXPERTS, N_EXPERTS), 0)
    e_c = lax.broadcasted_iota(jnp.int32, (N_EXPERTS, N_EXPERTS), 1)
    lower = jnp.where(e_c < e_r, 1.0, 0.0).astype(BF16)
    seg8_b = jnp.broadcast_to(seg8, (N_EXPERTS, LANES)).astype(BF16)
    seg_off = SUBLANES * jnp.dot(lower, seg8_b, preferred_element_type=F32)[:, 0:1]
    pos1 = jnp.sum(oh1 * (pre1 + seg_off), axis=0, keepdims=True)
    pos2 = jnp.sum(oh2 * (pre2 + (seg_off + cnt1)), axis=0, keepdims=True)

    lane = lax.broadcasted_iota(jnp.int32, tc_ref.shape, 1)
    seg_rows = jnp.broadcast_to(seg8 * SUBLANES, tc_ref.shape).astype(jnp.int32)
    tc_ref[...] = jnp.where(lane == step, seg_rows, tc_ref[...])

    zrow = jnp.zeros((SUBLANES - 4, tm), jnp.int32)
    rr_ref[...] = jnp.concatenate(
        [pos1.astype(jnp.int32), pos2.astype(jnp.int32), x1, x2, zrow], axis=0)
    wt = jnp.concatenate([w1, w2, pos1, pos2, jnp.zeros((LANES - 4, tm), F32)], axis=0)
    rwt_ref[...] = wt.T


def _mix_route(sb, lru, x2, sbg, lrug, wo_bf, n2g, wr_hi, wr_lo, rbias, tm):
    n, d = x2.shape
    half = sb.shape[1]
    row = lambda i: (i, 0)
    const = lambda i: (0, 0)
    return pl.pallas_call(
        functools.partial(_mix_route_kernel, tm=tm),
        grid=(n // tm,),
        in_specs=[pl.BlockSpec((tm, half), row), pl.BlockSpec((tm, half), row),
                  pl.BlockSpec((tm, d), row),
                  pl.BlockSpec((1, half), const), pl.BlockSpec((1, half), const),
                  pl.BlockSpec(wo_bf.shape, const), pl.BlockSpec((1, d), const),
                  pl.BlockSpec(wr_hi.shape, const), pl.BlockSpec(wr_lo.shape, const),
                  pl.BlockSpec(rbias.shape, const)],
        out_specs=[pl.BlockSpec((tm, d), row), pl.BlockSpec((tm, d), row),
                   pl.BlockSpec((SUBLANES, tm), lambda i: (0, i)),
                   pl.BlockSpec((tm, LANES), row),
                   pl.BlockSpec((N_EXPERTS, LANES), const)],
        out_shape=[jax.ShapeDtypeStruct((n, d), F32), jax.ShapeDtypeStruct((n, d), BF16),
                   jax.ShapeDtypeStruct((SUBLANES, n), jnp.int32),
                   jax.ShapeDtypeStruct((n, LANES), F32),
                   jax.ShapeDtypeStruct((N_EXPERTS, LANES), jnp.int32)],
        scratch_shapes=[pltpu.VMEM((tm, tm), BF16)],
        compiler_params=_cparams(("arbitrary",)),
        name="mix_route",
    )(sb, lru, x2, sbg, lrug, wo_bf, n2g, wr_hi, wr_lo, rbias)


def _pack_halves(x):
    half = x.shape[1] // 2
    lo = lax.shift_right_logical(lax.bitcast_convert_type(x[:, :half], jnp.int32), 16)
    hi = lax.bitcast_convert_type(x[:, half:], jnp.int32) & HIGH_HALF
    return hi | lo


def _unpack_halves(p):
    lo = lax.bitcast_convert_type(lax.shift_left(p, 16), F32)
    hi = lax.bitcast_convert_type(p & HIGH_HALF, F32)
    return lo.astype(BF16), hi.astype(BF16)


def _segment_copies(tile, c8_ref, loff_ref, goff_ref, make):
    for e in range(N_EXPERTS):
        idx = tile * N_EXPERTS + e
        rows = pl.multiple_of(c8_ref[idx], SUBLANES)

        @pl.when(rows > 0)
        def _(idx=idx, rows=rows):
            lo = pl.multiple_of(loff_ref[idx], SUBLANES)
            go = pl.multiple_of(goff_ref[idx], SUBLANES)
            make(pl.ds(lo, rows), pl.ds(go, rows)).start()


def _dispatch_kernel(c8_ref, loff_ref, goff_ref, tot_ref, used_ref,
                     rr_ref, u2_ref, xs_ref, lbuf, zbuf, sem, zsem, *, td, lrows):
    i = pl.program_id(0)
    slot = i % 2

    r_id = lax.broadcasted_iota(jnp.int32, (lrows, td), 0)
    perm = jnp.where(r_id == rr_ref[0:1, :], 1.0, jnp.where(r_id == rr_ref[1:2, :], 1.0, 0.0))
    sorted_rows = jnp.dot(perm.astype(BF16), u2_ref[...], preferred_element_type=F32)
    lbuf[slot] = _pack_halves(sorted_rows)

    @pl.when(i == 0)
    def _():
        zbuf[...] = jnp.zeros_like(zbuf)
        chunk = zbuf.shape[0]
        used = used_ref[0]
        spare = xs_ref.shape[0] - used
        n_fill = (spare + chunk - 1) // chunk

        def fill_copy(k):
            rows = pl.multiple_of(jnp.minimum(chunk, spare - k * chunk), SUBLANES)
            start = pl.multiple_of(used + k * chunk, SUBLANES)
            return pltpu.make_async_copy(zbuf.at[pl.ds(0, rows)], xs_ref.at[pl.ds(start, rows)],
                                         zsem)

        def fill_start(k, c):
            fill_copy(k).start()
            return c

        def fill_wait(k, c):
            fill_copy(k).wait()
            return c

        lax.fori_loop(0, n_fill, fill_start, 0)
        lax.fori_loop(0, n_fill, fill_wait, 0)

    def wait_tile(tile, s):
        rows = pl.multiple_of(tot_ref[tile], SUBLANES)
        pltpu.make_async_copy(lbuf.at[s, pl.ds(0, rows)], xs_ref.at[pl.ds(0, rows)], sem).wait()

    @pl.when(i > 0)
    def _():
        wait_tile(i - 1, 1 - slot)

    _segment_copies(i, c8_ref, loff_ref, goff_ref,
                    lambda loc, glob: pltpu.make_async_copy(lbuf.at[slot, loc], xs_ref.at[glob], sem))

    @pl.when(i == pl.num_programs(0) - 1)
    def _():
        wait_tile(i, slot)


def _dispatch(c8, loff, goff, tot, used, rr, u2, p_rows, td, lrows, tme):
    n, d = u2.shape
    pmap = lambda i, *_: (0, i)
    return pl.pallas_call(
        functools.partial(_dispatch_kernel, td=td, lrows=lrows),
        grid_spec=pltpu.PrefetchScalarGridSpec(
            num_scalar_prefetch=5,
            grid=(n // td,),
            in_specs=[pl.BlockSpec((SUBLANES, td), pmap),
                      pl.BlockSpec((td, d), lambda i, *_: (i, 0))],
            out_specs=pl.BlockSpec(memory_space=pl.ANY),
            scratch_shapes=[pltpu.VMEM((2, lrows, d // 2), jnp.int32),
                            pltpu.VMEM((tme, d // 2), jnp.int32),
                            pltpu.SemaphoreType.DMA(()), pltpu.SemaphoreType.DMA(())]),
        out_shape=jax.ShapeDtypeStruct((p_rows, d // 2), jnp.int32),
        compiler_params=_cparams(("arbitrary",)),
        name="dispatch",
    )(c8, loff, goff, tot, used, rr, u2)


def _experts_kernel(eoff_ref, erows_ref, xs_ref, wg_ref, wu_ref, wd_ref, ys_ref,
                    wg_bf, wu_bf, wd_bf, xbuf, ybuf, sem_in, sem_out, state, *, tme):
    e = pl.program_id(0)
    n_experts = pl.num_programs(0)
    rows = erows_ref[e]
    off = eoff_ref[e]
    n_tiles = (rows + tme - 1) // tme

    def tile_rows(total, k):
        return pl.multiple_of(jnp.minimum(tme, total - k * tme), SUBLANES)

    def in_copy(start, r, slot):
        start = pl.multiple_of(start, SUBLANES)
        return pltpu.make_async_copy(xs_ref.at[pl.ds(start, r)], xbuf.at[slot, pl.ds(0, r)],
                                     sem_in.at[slot])

    def out_copy(start, r, slot):
        start = pl.multiple_of(start, SUBLANES)
        return pltpu.make_async_copy(ybuf.at[slot, pl.ds(0, r)], ys_ref.at[pl.ds(start, r)],
                                     sem_out.at[slot])

    @pl.when(e == 0)
    def _():
        for s in range(4):
            state[s] = 0
        xbuf[...] = jnp.zeros_like(xbuf)

    @pl.when(rows > 0)
    def _():
        wg_bf[...] = wg_ref[0].astype(BF16)
        wu_bf[...] = wu_ref[0].astype(BF16)
        wd_bf[...] = wd_ref[0].astype(BF16)
        done = state[0]

        @pl.when(state[1] == 0)
        def _():
            in_copy(off, tile_rows(rows, 0), done % 2).start()

        def tile(k, c):
            slot = (done + k) % 2
            r = tile_rows(rows, k)

            @pl.when(k + 1 < n_tiles)
            def _():
                in_copy(off + (k + 1) * tme, tile_rows(rows, k + 1), 1 - slot).start()

            in_copy(off, r, slot).wait()

            @pl.when(done + k >= 2)
            def _():
                out_copy(0, pl.multiple_of(state[2 + slot], SUBLANES), slot).wait()

            def mlp(n):
                x_lo, x_hi = _unpack_halves(xbuf[slot, 0:n])
                half = x_lo.shape[1]
                hg = (jnp.dot(x_lo, wg_bf[0:half, :], preferred_element_type=F32)
                      + jnp.dot(x_hi, wg_bf[half:2 * half, :], preferred_element_type=F32))
                hu = (jnp.dot(x_lo, wu_bf[0:half, :], preferred_element_type=F32)
                      + jnp.dot(x_hi, wu_bf[half:2 * half, :], preferred_element_type=F32))
                act = (hg * jax.nn.sigmoid(hg) * hu).astype(BF16)
                y = jnp.dot(act, wd_bf[...], preferred_element_type=F32)
                ybuf[slot, 0:n] = _pack_halves(y.astype(BF16).astype(F32))

            @pl.when(r > tme // 2)
            def _():
                mlp(tme)

            @pl.when(r <= tme // 2)
            def _():
                mlp(tme // 2)

            out_copy(off + k * tme, r, slot).start()
            state[2 + slot] = r
            return c

        lax.fori_loop(0, n_tiles, tile, 0)
        state[0] = done + n_tiles

    nxt = jnp.minimum(e + 1, n_experts - 1)
    prefetch = (rows > 0) & (e + 1 < n_experts) & (erows_ref[nxt] > 0)
    state[1] = prefetch.astype(jnp.int32)

    @pl.when(prefetch)
    def _():
        in_copy(eoff_ref[nxt], tile_rows(erows_ref[nxt], 0), state[0] % 2).start()

    @pl.when(e == n_experts - 1)
    def _():
        total = eoff_ref[e] + rows
        for slot in range(2):
            @pl.when(state[0] > slot)
            def _(slot=slot):
                out_copy(0, pl.multiple_of(state[2 + slot], SUBLANES), slot).wait()
        ybuf[0] = jnp.zeros(ybuf.shape[1:], ybuf.dtype)
        spare = ys_ref.shape[0] - total
        n_fill = (spare + tme - 1) // tme

        def fill_copy(k):
            return out_copy(total + k * tme, tile_rows(spare, k), 0)

        def fill_start(k, c):
            fill_copy(k).start()
            return c

        def fill_wait(k, c):
            fill_copy(k).wait()
            return c

        lax.fori_loop(0, n_fill, fill_start, 0)
        lax.fori_loop(0, n_fill, fill_wait, 0)


def _experts(eoff, erows, xs, wg, wu, wd, tme):
    p = xs.shape[0]
    n_experts, d, de = wg.shape
    wmap = lambda e, *_: (e, 0, 0)
    return pl.pallas_call(
        functools.partial(_experts_kernel, tme=tme),
        grid_spec=pltpu.PrefetchScalarGridSpec(
            num_scalar_prefetch=2,
            grid=(n_experts,),
            in_specs=[pl.BlockSpec(memory_space=pl.ANY),
                      pl.BlockSpec((1, d, de), wmap), pl.BlockSpec((1, d, de), wmap),
                      pl.BlockSpec((1, de, d), wmap)],
            out_specs=pl.BlockSpec(memory_space=pl.ANY),
            scratch_shapes=[pltpu.VMEM((d, de), BF16), pltpu.VMEM((d, de), BF16),
                            pltpu.VMEM((de, d), BF16),
                            pltpu.VMEM((2, tme, d // 2), jnp.int32),
                            pltpu.VMEM((2, tme, d // 2), jnp.int32),
                            pltpu.SemaphoreType.DMA((2,)), pltpu.SemaphoreType.DMA((2,)),
                            pltpu.SMEM((4,), jnp.int32)]),
        out_shape=jax.ShapeDtypeStruct((p, d // 2), jnp.int32),
        compiler_params=_cparams(("arbitrary",)),
        name="experts",
    )(eoff, erows, xs, wg, wu, wd)


def _combine_kernel(c8_ref, loff_ref, goff_ref, tot_ref,
                    rwt_ref, h_ref, fg_ref, ys_ref, y_ref, ybuf, sems, *, tc, lrows):
    i = pl.program_id(0)
    slot = i % 2

    def gather_tile(tile, s):
        _segment_copies(tile, c8_ref, loff_ref, goff_ref,
                        lambda loc, glob: pltpu.make_async_copy(ys_ref.at[glob], ybuf.at[s, loc],
                                                                sems.at[s]))

    @pl.when(i == 0)
    def _():
        ybuf[...] = jnp.zeros_like(ybuf)
        gather_tile(0, 0)

    @pl.when(i + 1 < pl.num_programs(0))
    def _():
        gather_tile(i + 1, 1 - slot)

    rows = pl.multiple_of(tot_ref[i], SUBLANES)
    pltpu.make_async_copy(ys_ref.at[pl.ds(0, rows)], ybuf.at[slot, pl.ds(0, rows)],
                          sems.at[slot]).wait()

    w = rwt_ref[...]
    c_id = lax.broadcasted_iota(jnp.int32, (tc, lrows), 1)
    pos1 = w[:, 2:3].astype(jnp.int32)
    pos2 = w[:, 3:4].astype(jnp.int32)
    wmat = (jnp.where(c_id == pos1, w[:, 0:1], 0.0)
            + jnp.where(c_id == pos2, w[:, 1:2], 0.0)).astype(BF16)
    y_lo, y_hi = _unpack_halves(ybuf[slot])
    moe = jnp.concatenate([jnp.dot(wmat, y_lo, preferred_element_type=F32),
                           jnp.dot(wmat, y_hi, preferred_element_type=F32)], axis=-1)
    y_ref[...] = _rms_f32(h_ref[...] + moe, fg_ref[...])


def _combine(c8, loff, goff, tot, rwt, h, final_g, ys, tc, lrows):
    n, d = h.shape
    return pl.pallas_call(
        functools.partial(_combine_kernel, tc=tc, lrows=lrows),
        grid_spec=pltpu.PrefetchScalarGridSpec(
            num_scalar_prefetch=4,
            grid=(n // tc,),
            in_specs=[pl.BlockSpec((tc, LANES), lambda i, *_: (i, 0)),
                      pl.BlockSpec((tc, d), lambda i, *_: (i, 0)),
                      pl.BlockSpec((1, d), lambda i, *_: (0, 0)),
                      pl.BlockSpec(memory_space=pl.ANY)],
            out_specs=pl.BlockSpec((tc, d), lambda i, *_: (i, 0)),
            scratch_shapes=[pltpu.VMEM((2, lrows, d // 2), jnp.int32),
                            pltpu.SemaphoreType.DMA((2,))]),
        out_shape=jax.ShapeDtypeStruct((n, d), F32),
        compiler_params=_cparams(("arbitrary",)),
        name="combine",
    )(c8, loff, goff, tot, rwt, h, final_g, ys)


def _block_diag(w, per):
    nb, c, _ = w.shape
    eye = jnp.eye(per, dtype=w.dtype)
    wg = w.reshape(nb // per, per, c, c)
    return jnp.einsum("gpij,pq->gpiqj", wg, eye).reshape(nb // per, per * c, per * c)


def _router_tables(w_group, b_group, w_fine, b_fine):
    d = w_group.shape[0]
    w = jnp.zeros((d, LANES), F32)
    w = w.at[:, 0:N_GROUPS].set(w_group).at[:, SUBLANES:SUBLANES + N_EXPERTS].set(w_fine)
    b = jnp.full((LANES,), NEG_BIG, F32)
    b = b.at[0:N_GROUPS].set(b_group).at[SUBLANES:SUBLANES + N_EXPERTS].set(b_fine)
    w_hi = w.astype(BF16)
    w_lo = (w - w_hi.astype(F32)).astype(BF16)
    return w_hi, w_lo, b.reshape(LANES, 1)


def kernel(x, norm1_g, w_in, conv_w, conv_b, w_rgate, b_rgate, w_igate, b_igate, lam, sb_norm_g,
           lru_norm_g, w_out, norm2_g, w_group, b_group, w_fine, b_fine, w_e_gate, w_e_up,
           w_e_down, final_g):
    batch, seq, d = x.shape
    n = batch * seq
    width = w_in.shape[1] // 5
    tm = min(512, seq)
    tme = 512
    gate_per = 256 // w_rgate.shape[1]

    x2 = x.reshape(n, d)
    vec = lambda a: a.reshape(1, -1)

    q, k, v, xl, gl = _in_proj(x2, vec(norm1_g), w_in.astype(BF16), width, min(2 * tm, seq))
    n_pipe = seq // LANES - (ATTN_WINDOW_BLOCKS + 1)
    unroll = ATTN_UNROLL if n_pipe % ATTN_UNROLL == 0 else 2
    out_sb = _attention(q, k, v, batch, seq, LANES, ATTN_WINDOW_BLOCKS, unroll)
    out_lru = _lru(xl, gl, conv_w, vec(conv_b),
                   _block_diag(w_rgate, gate_per).astype(BF16), vec(b_rgate),
                   _block_diag(w_igate, gate_per).astype(BF16), vec(b_igate),
                   vec(lam), batch, seq, min(256, seq))

    wr_hi, wr_lo, rbias = _router_tables(w_group, b_group, w_fine, b_fine)
    h, u2, rr, rwt, tcnt = _mix_route(out_sb, out_lru, x2, vec(sb_norm_g), vec(lru_norm_g),
                                      w_out.astype(BF16), vec(norm2_g), wr_hi, wr_lo, rbias, tm)

    n_tiles = n // tm
    assert n_tiles <= LANES, "one lane of the per-tile count table per token tile"
    c8 = tcnt[:, :n_tiles].T
    erows = jnp.sum(c8, axis=0)
    eoff = jnp.cumsum(erows) - erows
    goff = eoff[None, :] + jnp.cumsum(c8, axis=0) - c8
    loff = jnp.cumsum(c8, axis=1) - c8
    tot = jnp.sum(c8, axis=1)
    lrows = 2 * tm + N_EXPERTS * SUBLANES
    p_rows = 2 * n + n_tiles * N_EXPERTS * (SUBLANES - 1)
    p_rows = -(-p_rows // SUBLANES) * SUBLANES
    i32 = lambda a: a.reshape(-1).astype(jnp.int32)
    c8, loff, goff, tot, eoff, erows = (i32(a) for a in (c8, loff, goff, tot, eoff, erows))

    xs = _dispatch(c8, loff, goff, tot, jnp.sum(erows, keepdims=True), rr, u2, p_rows, tm, lrows, tme)
    ys = _experts(eoff, erows, xs, w_e_gate, w_e_up, w_e_down, tme)
    y = _combine(c8, loff, goff, tot, rwt, h, vec(final_g), ys, tm, lrows)
    return y.reshape(batch, seq, d)
```

```python
import functools
import math

import jax
import jax.numpy as jnp
from jax import lax
from jax.experimental import pallas as pl
from jax.experimental.pallas import tpu as pltpu

F32 = jnp.float32
BF16 = jnp.bfloat16

EPS = 1e-6
HEAD_DIM = 64
HEADS_PER_BLOCK = 2
LANES = 128
SUBLANES = 8
CONV_W = 4
RG_C = 8.0
N_GROUPS = 4
EXPERTS_PER_GROUP = 8
N_EXPERTS = N_GROUPS * EXPERTS_PER_GROUP
ROUTER_ROWS = 48
NEG_BIG = -1e30
LOG2_E = math.log2(math.e)
ATTN_STOP = 104.0 * LOG2_E
ATTN_WINDOW_BLOCKS = 3
ATTN_UNROLL = 4
HIGH_HALF = -65536

VMEM_LIMIT = 56 * 1024 * 1024


def _cparams(sem):
    return pltpu.CompilerParams(dimension_semantics=sem, vmem_limit_bytes=VMEM_LIMIT)


def _rms_f32(x, g):
    return x * lax.rsqrt(jnp.mean(x * x, axis=-1, keepdims=True) + EPS) * g


def _in_proj_kernel(x_ref, g_ref, w_ref, q_ref, k_ref, v_ref, xl_ref, gl_ref, *, width, q_scale):
    u = _rms_f32(x_ref[...], g_ref[...]).astype(BF16)
    for c, o_ref in enumerate((q_ref, k_ref, v_ref, xl_ref, gl_ref)):
        p = jnp.dot(u, w_ref[:, c * width:(c + 1) * width], preferred_element_type=F32)
        if c == 0:
            p = p * q_scale
        o_ref[...] = p.astype(o_ref.dtype)


def _in_proj(x2, g, w_bf, width, tm):
    n, d = x2.shape
    row = lambda i: (i, 0)
    out_bf = jax.ShapeDtypeStruct((n, width), BF16)
    out_f = jax.ShapeDtypeStruct((n, width), F32)
    return pl.pallas_call(
        functools.partial(_in_proj_kernel, width=width, q_scale=1.0 / math.sqrt(HEAD_DIM)),
        grid=(n // tm,),
        in_specs=[pl.BlockSpec((tm, d), row),
                  pl.BlockSpec((1, d), lambda i: (0, 0)),
                  pl.BlockSpec(w_bf.shape, lambda i: (0, 0))],
        out_specs=[pl.BlockSpec((tm, width), row)] * 5,
        out_shape=[out_bf, out_bf, out_bf, out_f, out_f],
        compiler_params=_cparams(("arbitrary",)),
        name="in_proj",
    )(x2, g, w_bf)


def _attn_kernel(q_ref, k_ref, v_ref, o_ref, tri_ref, z_ref, arg_ref, ctot_ref, acc_ref, carry_ref,
                 *, tq, nwin):
    seq = q_ref.shape[0]
    win = nwin * tq
    lane = lax.broadcasted_iota(jnp.int32, (1, LANES), 1)
    r_id = lax.broadcasted_iota(jnp.int32, (tq, tq), 0)
    c_id = lax.broadcasted_iota(jnp.int32, (tq, tq), 1)
    causal = c_id < r_id
    rel = c_id - r_id

    k_r = lax.broadcasted_iota(jnp.int32, (2 * tq, 2 * tq), 0) % tq
    k_c = lax.broadcasted_iota(jnp.int32, (2 * tq, 2 * tq), 1)
    tri_ref[...] = jnp.where(k_c >= tq, 1.0, jnp.where(k_r > k_c, 1.0, 0.0)).astype(BF16)

    def softplus2(z):
        return jnp.maximum(z, 0.0) + jnp.log2(1.0 + jnp.exp2(-jnp.abs(z)))

    def scores(qh, keys):
        z = LOG2_E * lax.dot_general(qh, keys, (((1,), (1,)), ((), ())),
                                     preferred_element_type=F32)
        nlog_nb = softplus2(z)
        return nlog_nb, z - nlog_nb

    def suffix(nlog_nb):
        hi = lax.bitcast_convert_type(lax.bitcast_convert_type(nlog_nb, jnp.int32) & HIGH_HALF, F32)
        lo = (nlog_nb - hi).astype(BF16)
        hi = hi.astype(BF16)
        r = jnp.dot(jnp.concatenate([hi, lo], axis=1), tri_ref[...], preferred_element_type=F32)
        return r[:, :tq], r[:, tq:]

    def window(qh, wstart, masks):
        nlog_nb, log_b = scores(qh, k_ref[pl.ds(wstart, win), :])
        carry = None
        parts = [None] * nwin
        for b in reversed(range(nwin)):
            nl = nlog_nb[:, b * tq:(b + 1) * tq]
            if masks[b] is not None:
                nl = jnp.where(masks[b], nl, 0.0)
            excl, tot = suffix(nl)
            arg = log_b[:, b * tq:(b + 1) * tq] - excl
            a = jnp.exp2(arg if carry is None else arg - carry)
            if masks[b] is not None:
                a = jnp.where(masks[b], a, 0.0)
            parts[b] = a.astype(BF16)
            carry = tot if carry is None else carry + tot
        out = jnp.dot(jnp.concatenate(parts, axis=1), v_ref[pl.ds(wstart, win), :],
                      preferred_element_type=F32)
        return out, carry

    def head_queries(i):
        q = q_ref[pl.ds(i * tq, tq), :]
        return [jnp.where((lane >= h * HEAD_DIM) & (lane < (h + 1) * HEAD_DIM), q, jnp.zeros_like(q))
                for h in range(HEADS_PER_BLOCK)]

    def store(i, outs):
        o_ref[pl.ds(i * tq, tq), :] = jnp.where(lane < HEAD_DIM, outs[0], outs[1])

    for i in range(nwin - 1):
        masks = [(rel + b * tq) < i * tq for b in range(nwin)]
        store(i, [window(qh, 0, masks)[0] for qh in head_queries(i)])

    def window_start(i):
        return pl.multiple_of((i - (nwin - 1)) * tq, tq)

    def stage_scores(i, p):
        keys = k_ref[pl.ds(window_start(i), win), :]
        for h, qh in enumerate(head_queries(i)):
            z_ref[p, h] = LOG2_E * lax.dot_general(qh, keys, (((1,), (1,)), ((), ())),
                                                   preferred_element_type=F32)

    def stage_exponents(p):
        carries = []
        for h in range(HEADS_PER_BLOCK):
            z = z_ref[p, h]
            softplus = softplus2(z)
            carry = None
            for b in reversed(range(nwin)):
                diag = b == nwin - 1
                cols = slice(b * tq, (b + 1) * tq)
                nl = softplus[:, cols]
                if diag:
                    nl = jnp.where(causal, nl, 0.0)
                excl, tot = suffix(nl)
                arg = z[:, cols] - softplus[:, cols] - excl
                if carry is not None:
                    arg = arg - carry
                if diag:
                    arg = jnp.where(causal, arg, NEG_BIG)
                arg_ref[p, h, :, cols] = arg
                carry = tot if carry is None else carry + tot
            ctot_ref[p, h] = carry
            carries.append(carry)
        return jnp.min(jnp.minimum(carries[0], carries[1]))

    def stage_output(i, p, s):
        vals = v_ref[pl.ds(window_start(i), win), :]
        for h in range(HEADS_PER_BLOCK):
            a = jnp.exp2(arg_ref[p, h]).astype(BF16)
            acc_ref[s, h] = jnp.dot(a, vals, preferred_element_type=F32)
            carry_ref[s, h] = ctot_ref[p, h]

    def finish(i, s, cmin):
        def cond(state):
            j, cmin = state
            return (j >= 0) & (cmin <= ATTN_STOP)

        def older(state):
            j, _ = state
            start = pl.multiple_of(j * tq, tq)
            keys = k_ref[pl.ds(start, tq), :]
            vals = v_ref[pl.ds(start, tq), :]
            cs = []
            for h, qh in enumerate(head_queries(i)):
                nlog_nb, log_b = scores(qh, keys)
                excl, tot = suffix(nlog_nb)
                carry = carry_ref[s, h]
                a = jnp.exp2(log_b - excl - carry)
                acc_ref[s, h] += jnp.dot(a.astype(BF16), vals, preferred_element_type=F32)
                carry_ref[s, h] = carry + tot
                cs.append(carry + tot)
            return j - 1, jnp.min(jnp.minimum(cs[0], cs[1]))

        lax.while_loop(cond, older, (i - nwin, cmin))
        store(i, [acc_ref[s, 0], acc_ref[s, 1]])

    first = nwin - 1
    n_blocks = seq // tq
    unroll = acc_ref.shape[0]
    slot = lambda i: (first + i) % 2
    stage_scores(first, slot(0))
    stage_scores(first + 1, slot(1))
    cmin = stage_exponents(slot(0))

    def steady(m, cmin):
        t = first + 2 + unroll * m
        cmins = [cmin]
        for u in range(unroll):
            stage_output(t - 2 + u, slot(u), u)
            cmins.append(stage_exponents(slot(u + 1)))
            stage_scores(t + u, slot(u))
        for u in range(unroll):
            finish(t - 2 + u, u, cmins[u])
        return cmins[unroll]

    cmin = lax.fori_loop(0, (n_blocks - first - 2) // unroll, steady, cmin)
    stage_output(n_blocks - 2, slot(0), 0)
    cmin_last = stage_exponents(slot(1))
    finish(n_blocks - 2, 0, cmin)
    stage_output(n_blocks - 1, slot(1), 1)
    finish(n_blocks - 1, 1, cmin_last)


def _attention(q, k, v, batch, seq, tq, nwin, unroll):
    n, width = q.shape
    assert tq == LANES and seq >= (nwin + 1) * tq
    assert unroll % 2 == 0 and (seq // tq - (nwin + 1)) % unroll == 0
    blk = pl.BlockSpec((seq, LANES), lambda b, hp: (b, hp))
    stage_buf = pltpu.VMEM((2, HEADS_PER_BLOCK, tq, nwin * tq), F32)
    carry_buf = pltpu.VMEM((2, HEADS_PER_BLOCK, tq, tq), F32)
    row_buf = pltpu.VMEM((unroll, HEADS_PER_BLOCK, tq, LANES), F32)
    return pl.pallas_call(
        functools.partial(_attn_kernel, tq=tq, nwin=nwin),
        grid=(batch, width // LANES),
        in_specs=[blk, blk, blk],
        out_specs=blk,
        out_shape=jax.ShapeDtypeStruct((n, width), F32),
        scratch_shapes=[pltpu.VMEM((2 * tq, 2 * tq), BF16),
                        stage_buf, stage_buf, carry_buf, row_buf, row_buf],
        compiler_params=_cparams(("arbitrary", "arbitrary")),
        name="attn",
    )(q, k, v)


def _gelu_tanh(x):
    return 0.5 * x * (1.0 + jnp.tanh(math.sqrt(2.0 / math.pi) * (x + 0.044715 * (x * x * x))))


def _lru_kernel(*refs, ts, n_slab):
    xl_refs, gl_refs = refs[0:n_slab], refs[n_slab:2 * n_slab]
    cw_ref, cb_ref, wr_ref, br_ref, wi_ref, bi_ref, lam_ref = refs[2 * n_slab:2 * n_slab + 7]
    o_refs = refs[2 * n_slab + 7:3 * n_slab + 7]
    tail_ref, a7_ref, u7_ref, hp_ref, pa_ref, pu_ref, h_ref = refs[3 * n_slab + 7:]
    t = pl.program_id(1)
    groups = ts // SUBLANES

    @pl.when(t == 0)
    def _():
        tail_ref[...] = jnp.zeros_like(tail_ref)
        h_ref[...] = jnp.zeros_like(h_ref)

    first_group = lax.broadcasted_iota(jnp.int32, (groups, LANES), 0) == 0
    for c in range(n_slab):
        lanes = slice(c * LANES, (c + 1) * LANES)
        x = [xl_refs[c][pl.ds(s, groups, stride=SUBLANES), :] for s in range(SUBLANES)]
        shifted = {}
        for s in range(SUBLANES - (CONV_W - 1), SUBLANES):
            shifted[s] = jnp.where(first_group, tail_ref[c, s:s + 1, :], pltpu.roll(x[s], 1, axis=0))
            tail_ref[c, s:s + 1, :] = x[s][groups - 1:groups, :]
        conv = []
        for s in range(SUBLANES):
            y = cb_ref[:, lanes]
            for w in range(CONV_W):
                j = s - (CONV_W - 1) + w
                y = y + (x[j] if j >= 0 else shifted[j + SUBLANES]) * cw_ref[w:w + 1, lanes]
            conv.append(y)
        xc = jnp.concatenate(conv, axis=0)

        xcb = xc.astype(BF16)
        r = jax.nn.sigmoid(jnp.dot(xcb, wr_ref[c], preferred_element_type=F32) + br_ref[:, lanes])
        ig = jax.nn.sigmoid(jnp.dot(xcb, wi_ref[c], preferred_element_type=F32) + bi_ref[:, lanes])
        lam = lam_ref[:, lanes]
        log_sig_lam = -(jnp.maximum(-lam, 0.0) + jnp.log1p(jnp.exp(-jnp.abs(lam))))
        log_a = RG_C * r * log_sig_lam
        a = jnp.exp(log_a)
        th = jnp.tanh(log_a)
        u = jnp.sqrt(-2.0 * th / (1.0 - th)) * (ig * xc)

        a_run = u_run = None
        for s in range(SUBLANES):
            rows = slice(s * groups, (s + 1) * groups)
            if s == 0:
                a_run, u_run = a[rows], u[rows]
            else:
                u_run = a[rows] * u_run + u[rows]
                a_run = a_run * a[rows]
            pa_ref[c, rows, :] = a_run
            pu_ref[c, rows, :] = u_run
        a7_ref[c] = a_run
        u7_ref[c] = u_run

    def group(g, hs):
        nxt = []
        for c in range(n_slab):
            hp_ref[c, pl.ds(g, 1), :] = hs[c]
            nxt.append(a7_ref[c, pl.ds(g, 1), :] * hs[c] + u7_ref[c, pl.ds(g, 1), :])
        return tuple(nxt)

    hs = lax.fori_loop(0, groups, group, tuple(h_ref[c] for c in range(n_slab)), unroll=8)
    for c in range(n_slab):
        h_ref[c] = hs[c]

    for c in range(n_slab):
        h_in = hp_ref[c]
        for s in range(SUBLANES):
            rows = slice(s * groups, (s + 1) * groups)
            hseq = pu_ref[c, rows, :] + pa_ref[c, rows, :] * h_in
            gate = _gelu_tanh(gl_refs[c][pl.ds(s, groups, stride=SUBLANES), :])
            o_refs[c][pl.ds(s, groups, stride=SUBLANES), :] = hseq * gate


def _lru(xl, gl, conv_w, conv_b, wr_bd, br, wi_bd, bi, lam, batch, seq, ts):
    n, width = xl.shape
    nt = seq // ts
    n_slab = width // LANES
    assert wr_bd.shape == (n_slab, LANES, LANES) and ts % (SUBLANES * SUBLANES) == 0
    groups = ts // SUBLANES
    slab = [pl.BlockSpec((ts, LANES), functools.partial(lambda b, t, c: (b * nt + t, c), c=c))
            for c in range(n_slab)]
    const2 = lambda b, t: (0, 0)
    const3 = lambda b, t: (0, 0, 0)
    vec = pl.BlockSpec((1, width), const2)
    per_group = pltpu.VMEM((n_slab, groups, LANES), F32)
    per_step = pltpu.VMEM((n_slab, ts, LANES), F32)
    return pl.pallas_call(
        functools.partial(_lru_kernel, ts=ts, n_slab=n_slab),
        grid=(batch, nt),
        in_specs=slab + slab + [pl.BlockSpec((CONV_W, width), const2), vec,
                                pl.BlockSpec(wr_bd.shape, const3), vec,
                                pl.BlockSpec(wi_bd.shape, const3), vec, vec],
        out_specs=[pl.BlockSpec((ts, LANES), lambda b, t: (b * nt + t, 0))] * n_slab,
        out_shape=[jax.ShapeDtypeStruct((n, LANES), F32)] * n_slab,
        scratch_shapes=[pltpu.VMEM((n_slab, SUBLANES, LANES), F32),
                        per_group, per_group, per_group, per_step, per_step,
                        pltpu.VMEM((n_slab, 1, LANES), F32)],
        compiler_params=_cparams(("arbitrary", "arbitrary")),
        name="lru",
    )(*([xl] * n_slab), *([gl] * n_slab), conv_w, conv_b, wr_bd, br, wi_bd, bi, lam)


def _mix_route_kernel(sb_ref, *refs, tm, n_slab):
    lru_refs = refs[:n_slab]
    (x_ref, sbg_ref, lrug_ref, wo_ref, n2g_ref, wrs_ref, rb_ref,
     h_ref, u2_ref, rr_ref, rwt_ref, tc_ref, before_ref, u2s_ref) = refs[n_slab:]
    _mix_route_body(sb_ref, lru_refs, x_ref, sbg_ref, lrug_ref, wo_ref, n2g_ref, wrs_ref, rb_ref,
                    h_ref, u2_ref, rr_ref, rwt_ref, tc_ref, before_ref, u2s_ref, tm)


def _mix_route_body(sb_ref, lru_refs, x_ref, sbg_ref, lrug_ref, wo_ref, n2g_ref, wrs_ref, rb_ref,
                    h_ref, u2_ref, rr_ref, rwt_ref, tc_ref, before_ref, u2s_ref, tm):
    step = pl.program_id(0)
    half = sb_ref.shape[1]

    @pl.when(step == 0)
    def _():
        r_id = lax.broadcasted_iota(jnp.int32, (tm, tm), 0)
        c_id = lax.broadcasted_iota(jnp.int32, (tm, tm), 1)
        before_ref[...] = (r_id < c_id).astype(BF16)
        tc_ref[...] = jnp.zeros_like(tc_ref)
        u2s_ref[...] = jnp.zeros_like(u2s_ref)

    u2 = u2s_ref[...]
    u_hi = u2.astype(BF16)
    u_lo = (u2 - u_hi.astype(F32)).astype(BF16)
    nt_dims = (((1,), (1,)), ((), ()))
    n_rows = rb_ref.shape[0]
    both = lax.dot_general(wrs_ref[...], u_hi, nt_dims, preferred_element_type=F32)
    lt = (both[0:n_rows] + both[n_rows:2 * n_rows]
          + lax.dot_general(wrs_ref[0:n_rows, :], u_lo, nt_dims, preferred_element_type=F32)
          + rb_ref[...])

    sub = lax.broadcasted_iota(jnp.int32, (SUBLANES, tm), 0)

    def top1(x):
        m = jnp.max(x, axis=0, keepdims=True)
        idx = jnp.min(jnp.where(x == m, sub, SUBLANES), axis=0, keepdims=True)
        return m, idx

    grp = lt[0:SUBLANES, :]
    g_max, g_idx = top1(grp)
    g_p = 1.0 / jnp.sum(jnp.exp(grp - g_max), axis=0, keepdims=True)
    fine = lt[SUBLANES:2 * SUBLANES, :]
    for g in range(1, N_GROUPS):
        fine = jnp.where(g_idx == g, lt[(g + 1) * SUBLANES:(g + 2) * SUBLANES, :], fine)
    m1, i1 = top1(fine)
    m2, i2 = top1(jnp.where(sub == i1, -jnp.inf, fine))
    e2 = jnp.exp(m2 - m1)
    p1 = 1.0 / (1.0 + e2)
    w1 = g_p * p1
    w2 = g_p * (e2 * p1)
    x1 = g_idx * EXPERTS_PER_GROUP + i1
    x2 = g_idx * EXPERTS_PER_GROUP + i2

    eid = lax.broadcasted_iota(jnp.int32, (N_EXPERTS, tm), 0)
    oh1 = jnp.where(eid == x1, 1.0, 0.0)
    oh2 = jnp.where(eid == x2, 1.0, 0.0)
    pre1 = jnp.dot(oh1.astype(BF16), before_ref[...], preferred_element_type=F32)
    pre2 = jnp.dot(oh2.astype(BF16), before_ref[...], preferred_element_type=F32)
    cnt1 = jnp.sum(oh1, axis=1, keepdims=True)
    cnt2 = jnp.sum(oh2, axis=1, keepdims=True)
    seg8 = jnp.floor((cnt1 + cnt2 + (SUBLANES - 1.0)) * (1.0 / SUBLANES))
    e_r = lax.broadcasted_iota(jnp.int32, (N_EXPERTS, N_EXPERTS), 0)
    e_c = lax.broadcasted_iota(jnp.int32, (N_EXPERTS, N_EXPERTS), 1)
    lower = jnp.where(e_c < e_r, 1.0, 0.0).astype(BF16)
    seg8_b = jnp.broadcast_to(seg8, (N_EXPERTS, LANES)).astype(BF16)
    seg_off = SUBLANES * jnp.dot(lower, seg8_b, preferred_element_type=F32)[:, 0:1]
    pos1 = jnp.sum(oh1 * (pre1 + seg_off), axis=0, keepdims=True)
    pos2 = jnp.sum(oh2 * (pre2 + (seg_off + cnt1)), axis=0, keepdims=True)

    lane = lax.broadcasted_iota(jnp.int32, tc_ref.shape, 1)
    seg_rows = jnp.broadcast_to(seg8 * SUBLANES, tc_ref.shape).astype(jnp.int32)
    tc_ref[...] = jnp.where(lane == step - 1, seg_rows, tc_ref[...])

    zrow = jnp.zeros((SUBLANES - 4, tm), jnp.int32)
    rr_ref[...] = jnp.concatenate(
        [pos1.astype(jnp.int32), pos2.astype(jnp.int32), x1, x2, zrow], axis=0)
    wt = jnp.concatenate([w1, w2, pos1, pos2, jnp.zeros((LANES - 4, tm), F32)], axis=0)
    rwt_ref[...] = wt.T

    m_sb = _rms_f32(sb_ref[...], sbg_ref[...]).astype(BF16)
    lru = jnp.concatenate([r[...] for r in lru_refs], axis=-1)
    m_lru = _rms_f32(lru, lrug_ref[...]).astype(BF16)
    h = (x_ref[...]
         + jnp.dot(m_sb, wo_ref[0:half, :], preferred_element_type=F32)
         + jnp.dot(m_lru, wo_ref[half:2 * half, :], preferred_element_type=F32))
    h_ref[...] = h
    u2_next = _rms_f32(h, n2g_ref[...])
    u2_ref[...] = u2_next.astype(BF16)
    u2s_ref[...] = u2_next


def _mix_route(sb, lru, x2, sbg, lrug, wo_bf, n2g, wr_stack, rbias, tm):
    n, d = x2.shape
    half = sb.shape[1]
    n_tiles = n // tm
    row = lambda i: (jnp.minimum(i, n_tiles - 1), 0)
    routed = lambda i: (jnp.maximum(i - 1, 0), 0)
    const = lambda i: (0, 0)
    return pl.pallas_call(
        functools.partial(_mix_route_kernel, tm=tm, n_slab=len(lru)),
        grid=(n_tiles + 1,),
        in_specs=[pl.BlockSpec((tm, half), row)] + [pl.BlockSpec((tm, LANES), row)] * len(lru)
                 + [pl.BlockSpec((tm, d), row),
                  pl.BlockSpec((1, half), const), pl.BlockSpec((1, half), const),
                  pl.BlockSpec(wo_bf.shape, const), pl.BlockSpec((1, d), const),
                  pl.BlockSpec(wr_stack.shape, const), pl.BlockSpec(rbias.shape, const)],
        out_specs=[pl.BlockSpec((tm, d), row), pl.BlockSpec((tm, d), row),
                   pl.BlockSpec((SUBLANES, tm), lambda i: (0, jnp.maximum(i - 1, 0))),
                   pl.BlockSpec((tm, LANES), routed),
                   pl.BlockSpec((N_EXPERTS, LANES), const)],
        out_shape=[jax.ShapeDtypeStruct((n, d), F32), jax.ShapeDtypeStruct((n, d), BF16),
                   jax.ShapeDtypeStruct((SUBLANES, n), jnp.int32),
                   jax.ShapeDtypeStruct((n, LANES), F32),
                   jax.ShapeDtypeStruct((N_EXPERTS, LANES), jnp.int32)],
        scratch_shapes=[pltpu.VMEM((tm, tm), BF16), pltpu.VMEM((tm, d), F32)],
        compiler_params=_cparams(("arbitrary",)),
        name="mix_route",
    )(sb, *lru, x2, sbg, lrug, wo_bf, n2g, wr_stack, rbias)


def _pack_halves(x):
    half = x.shape[1] // 2
    lo = lax.shift_right_logical(lax.bitcast_convert_type(x[:, :half], jnp.int32), 16)
    hi = lax.bitcast_convert_type(x[:, half:], jnp.int32) & HIGH_HALF
    return hi | lo


def _unpack_halves(p):
    lo = lax.bitcast_convert_type(lax.shift_left(p, 16), F32)
    hi = lax.bitcast_convert_type(p & HIGH_HALF, F32)
    return lo.astype(BF16), hi.astype(BF16)


def _segment_copies(tile, c8_ref, loff_ref, goff_ref, make):
    for e in range(N_EXPERTS):
        idx = tile * N_EXPERTS + e
        rows = pl.multiple_of(c8_ref[idx], SUBLANES)

        @pl.when(rows > 0)
        def _(idx=idx, rows=rows):
            lo = pl.multiple_of(loff_ref[idx], SUBLANES)
            go = pl.multiple_of(goff_ref[idx], SUBLANES)
            make(pl.ds(lo, rows), pl.ds(go, rows)).start()


def _dispatch_kernel(c8_ref, loff_ref, goff_ref, tot_ref, used_ref,
                     rr_ref, u2_ref, xs_ref, lbuf, zbuf, sem, zsem, *, td, lrows):
    i = pl.program_id(0)
    slot = i % 2

    r_id = lax.broadcasted_iota(jnp.int32, (lrows, td), 0)
    perm = jnp.where(r_id == rr_ref[0:1, :], 1.0, jnp.where(r_id == rr_ref[1:2, :], 1.0, 0.0))
    sorted_rows = jnp.dot(perm.astype(BF16), u2_ref[...], preferred_element_type=F32)
    lbuf[slot] = _pack_halves(sorted_rows)

    @pl.when(i == 0)
    def _():
        zbuf[...] = jnp.zeros_like(zbuf)
        chunk = zbuf.shape[0]
        used = used_ref[0]
        spare = xs_ref.shape[0] - used
        n_fill = (spare + chunk - 1) // chunk

        def fill_copy(k):
            rows = pl.multiple_of(jnp.minimum(chunk, spare - k * chunk), SUBLANES)
            start = pl.multiple_of(used + k * chunk, SUBLANES)
            return pltpu.make_async_copy(zbuf.at[pl.ds(0, rows)], xs_ref.at[pl.ds(start, rows)],
                                         zsem)

        def fill_start(k, c):
            fill_copy(k).start()
            return c

        def fill_wait(k, c):
            fill_copy(k).wait()
            return c

        lax.fori_loop(0, n_fill, fill_start, 0)
        lax.fori_loop(0, n_fill, fill_wait, 0)

    def wait_tile(tile, s):
        rows = pl.multiple_of(tot_ref[tile], SUBLANES)
        pltpu.make_async_copy(lbuf.at[s, pl.ds(0, rows)], xs_ref.at[pl.ds(0, rows)], sem).wait()

    @pl.when(i > 0)
    def _():
        wait_tile(i - 1, 1 - slot)

    _segment_copies(i, c8_ref, loff_ref, goff_ref,
                    lambda loc, glob: pltpu.make_async_copy(lbuf.at[slot, loc], xs_ref.at[glob], sem))

    @pl.when(i == pl.num_programs(0) - 1)
    def _():
        wait_tile(i, slot)


def _dispatch(c8, loff, goff, tot, used, rr, u2, p_rows, td, lrows, tme):
    n, d = u2.shape
    pmap = lambda i, *_: (0, i)
    return pl.pallas_call(
        functools.partial(_dispatch_kernel, td=td, lrows=lrows),
        grid_spec=pltpu.PrefetchScalarGridSpec(
            num_scalar_prefetch=5,
            grid=(n // td,),
            in_specs=[pl.BlockSpec((SUBLANES, td), pmap),
                      pl.BlockSpec((td, d), lambda i, *_: (i, 0))],
            out_specs=pl.BlockSpec(memory_space=pl.ANY),
            scratch_shapes=[pltpu.VMEM((2, lrows, d // 2), jnp.int32),
                            pltpu.VMEM((tme, d // 2), jnp.int32),
                            pltpu.SemaphoreType.DMA(()), pltpu.SemaphoreType.DMA(())]),
        out_shape=jax.ShapeDtypeStruct((p_rows, d // 2), jnp.int32),
        compiler_params=_cparams(("arbitrary",)),
        name="dispatch",
    )(c8, loff, goff, tot, used, rr, u2)


def _experts_kernel(eoff_ref, erows_ref, xs_ref, wg_ref, wu_ref, wd_ref, ys_ref,
                    wg_bf, wu_bf, wd_bf, xbuf, ybuf, sem_in, sem_out, state, *, tme):
    e = pl.program_id(0)
    n_experts = pl.num_programs(0)
    rows = erows_ref[e]
    off = eoff_ref[e]
    n_tiles = (rows + tme - 1) // tme

    def tile_rows(total, k):
        return pl.multiple_of(jnp.minimum(tme, total - k * tme), SUBLANES)

    def in_copy(start, r, slot):
        start = pl.multiple_of(start, SUBLANES)
        return pltpu.make_async_copy(xs_ref.at[pl.ds(start, r)], xbuf.at[slot, pl.ds(0, r)],
                                     sem_in.at[slot])

    def out_copy(start, r, slot):
        start = pl.multiple_of(start, SUBLANES)
        return pltpu.make_async_copy(ybuf.at[slot, pl.ds(0, r)], ys_ref.at[pl.ds(start, r)],
                                     sem_out.at[slot])

    @pl.when(e == 0)
    def _():
        for s in range(4):
            state[s] = 0
        xbuf[...] = jnp.zeros_like(xbuf)

    @pl.when(rows > 0)
    def _():
        wg_bf[...] = wg_ref[0].astype(BF16)
        wu_bf[...] = wu_ref[0].astype(BF16)
        wd_bf[...] = wd_ref[0].astype(BF16)
        done = state[0]

        @pl.when(state[1] == 0)
        def _():
            in_copy(off, tile_rows(rows, 0), done % 2).start()

        def tile(k, c):
            slot = (done + k) % 2
            r = tile_rows(rows, k)

            @pl.when(k + 1 < n_tiles)
            def _():
                in_copy(off + (k + 1) * tme, tile_rows(rows, k + 1), 1 - slot).start()

            in_copy(off, r, slot).wait()

            @pl.when(done + k >= 2)
            def _():
                out_copy(0, pl.multiple_of(state[2 + slot], SUBLANES), slot).wait()

            def mlp(n):
                x_lo, x_hi = _unpack_halves(xbuf[slot, 0:n])
                half = x_lo.shape[1]
                hg = (jnp.dot(x_lo, wg_bf[0:half, :], preferred_element_type=F32)
                      + jnp.dot(x_hi, wg_bf[half:2 * half, :], preferred_element_type=F32))
                hu = (jnp.dot(x_lo, wu_bf[0:half, :], preferred_element_type=F32)
                      + jnp.dot(x_hi, wu_bf[half:2 * half, :], preferred_element_type=F32))
                act = (hg * jax.nn.sigmoid(hg) * hu).astype(BF16)
                y = jnp.dot(act, wd_bf[...], preferred_element_type=F32)
                ybuf[slot, 0:n] = _pack_halves(y.astype(BF16).astype(F32))

            @pl.when(r > tme // 2)
            def _():
                mlp(tme)

            @pl.when(r <= tme // 2)
            def _():
                mlp(tme // 2)

            out_copy(off + k * tme, r, slot).start()
            state[2 + slot] = r
            return c

        lax.fori_loop(0, n_tiles, tile, 0)
        state[0] = done + n_tiles

    nxt = jnp.minimum(e + 1, n_experts - 1)
    prefetch = (rows > 0) & (e + 1 < n_experts) & (erows_ref[nxt] > 0)
    state[1] = prefetch.astype(jnp.int32)

    @pl.when(prefetch)
    def _():
        in_copy(eoff_ref[nxt], tile_rows(erows_ref[nxt], 0), state[0] % 2).start()

    @pl.when(e == n_experts - 1)
    def _():
        total = eoff_ref[e] + rows
        for slot in range(2):
            @pl.when(state[0] > slot)
            def _(slot=slot):
                out_copy(0, pl.multiple_of(state[2 + slot], SUBLANES), slot).wait()
        ybuf[0] = jnp.zeros(ybuf.shape[1:], ybuf.dtype)
        spare = ys_ref.shape[0] - total
        n_fill = (spare + tme - 1) // tme

        def fill_copy(k):
            return out_copy(total + k * tme, tile_rows(spare, k), 0)

        def fill_start(k, c):
            fill_copy(k).start()
            return c

        def fill_wait(k, c):
            fill_copy(k).wait()
            return c

        lax.fori_loop(0, n_fill, fill_start, 0)
        lax.fori_loop(0, n_fill, fill_wait, 0)


def _experts(eoff, erows, xs, wg, wu, wd, tme):
    p = xs.shape[0]
    n_experts, d, de = wg.shape
    wmap = lambda e, *_: (e, 0, 0)
    return pl.pallas_call(
        functools.partial(_experts_kernel, tme=tme),
        grid_spec=pltpu.PrefetchScalarGridSpec(
            num_scalar_prefetch=2,
            grid=(n_experts,),
            in_specs=[pl.BlockSpec(memory_space=pl.ANY),
                      pl.BlockSpec((1, d, de), wmap), pl.BlockSpec((1, d, de), wmap),
                      pl.BlockSpec((1, de, d), wmap)],
            out_specs=pl.BlockSpec(memory_space=pl.ANY),
            scratch_shapes=[pltpu.VMEM((d, de), BF16), pltpu.VMEM((d, de), BF16),
                            pltpu.VMEM((de, d), BF16),
                            pltpu.VMEM((2, tme, d // 2), jnp.int32),
                            pltpu.VMEM((2, tme, d // 2), jnp.int32),
                            pltpu.SemaphoreType.DMA((2,)), pltpu.SemaphoreType.DMA((2,)),
                            pltpu.SMEM((4,), jnp.int32)]),
        out_shape=jax.ShapeDtypeStruct((p, d // 2), jnp.int32),
        compiler_params=_cparams(("arbitrary",)),
        name="experts",
    )(eoff, erows, xs, wg, wu, wd)


def _combine_kernel(c8_ref, loff_ref, goff_ref, tot_ref,
                    rwt_ref, h_ref, fg_ref, ys_ref, y_ref, ybuf, sems, *, tc, lrows):
    i = pl.program_id(0)
    slot = i % 2

    def gather_tile(tile, s):
        _segment_copies(tile, c8_ref, loff_ref, goff_ref,
                        lambda loc, glob: pltpu.make_async_copy(ys_ref.at[glob], ybuf.at[s, loc],
                                                                sems.at[s]))

    @pl.when(i == 0)
    def _():
        ybuf[...] = jnp.zeros_like(ybuf)
        gather_tile(0, 0)

    @pl.when(i + 1 < pl.num_programs(0))
    def _():
        gather_tile(i + 1, 1 - slot)

    rows = pl.multiple_of(tot_ref[i], SUBLANES)
    pltpu.make_async_copy(ys_ref.at[pl.ds(0, rows)], ybuf.at[slot, pl.ds(0, rows)],
                          sems.at[slot]).wait()

    w = rwt_ref[...]
    c_id = lax.broadcasted_iota(jnp.int32, (tc, lrows), 1)
    pos1 = w[:, 2:3].astype(jnp.int32)
    pos2 = w[:, 3:4].astype(jnp.int32)
    wmat = (jnp.where(c_id == pos1, w[:, 0:1], 0.0)
            + jnp.where(c_id == pos2, w[:, 1:2], 0.0)).astype(BF16)
    y_lo, y_hi = _unpack_halves(ybuf[slot])
    moe = jnp.concatenate([jnp.dot(wmat, y_lo, preferred_element_type=F32),
                           jnp.dot(wmat, y_hi, preferred_element_type=F32)], axis=-1)
    y_ref[...] = _rms_f32(h_ref[...] + moe, fg_ref[...])


def _combine(c8, loff, goff, tot, rwt, h, final_g, ys, tc, lrows):
    n, d = h.shape
    return pl.pallas_call(
        functools.partial(_combine_kernel, tc=tc, lrows=lrows),
        grid_spec=pltpu.PrefetchScalarGridSpec(
            num_scalar_prefetch=4,
            grid=(n // tc,),
            in_specs=[pl.BlockSpec((tc, LANES), lambda i, *_: (i, 0)),
                      pl.BlockSpec((tc, d), lambda i, *_: (i, 0)),
                      pl.BlockSpec((1, d), lambda i, *_: (0, 0)),
                      pl.BlockSpec(memory_space=pl.ANY)],
            out_specs=pl.BlockSpec((tc, d), lambda i, *_: (i, 0)),
            scratch_shapes=[pltpu.VMEM((2, lrows, d // 2), jnp.int32),
                            pltpu.SemaphoreType.DMA((2,))]),
        out_shape=jax.ShapeDtypeStruct((n, d), F32),
        compiler_params=_cparams(("arbitrary",)),
        name="combine",
    )(c8, loff, goff, tot, rwt, h, final_g, ys)


def _block_diag(w, per):
    nb, c, _ = w.shape
    eye = jnp.eye(per, dtype=w.dtype)
    wg = w.reshape(nb // per, per, c, c)
    return jnp.einsum("gpij,pq->gpiqj", wg, eye).reshape(nb // per, per * c, per * c)


def _router_tables(w_group, b_group, w_fine, b_fine):
    d = w_group.shape[0]
    w = jnp.zeros((ROUTER_ROWS, d), F32)
    w = w.at[0:N_GROUPS].set(w_group.T).at[SUBLANES:SUBLANES + N_EXPERTS].set(w_fine.T)
    b = jnp.full((ROUTER_ROWS,), NEG_BIG, F32)
    b = b.at[0:N_GROUPS].set(b_group).at[SUBLANES:SUBLANES + N_EXPERTS].set(b_fine)
    w_hi = w.astype(BF16)
    w_lo = (w - w_hi.astype(F32)).astype(BF16)
    return jnp.concatenate([w_hi, w_lo], axis=0), b.reshape(ROUTER_ROWS, 1)


def kernel(x, norm1_g, w_in, conv_w, conv_b, w_rgate, b_rgate, w_igate, b_igate, lam, sb_norm_g,
           lru_norm_g, w_out, norm2_g, w_group, b_group, w_fine, b_fine, w_e_gate, w_e_up,
           w_e_down, final_g):
    batch, seq, d = x.shape
    n = batch * seq
    width = w_in.shape[1] // 5
    tm = min(512, seq)
    tme = 512
    gate_per = LANES // w_rgate.shape[1]

    x2 = x.reshape(n, d)
    vec = lambda a: a.reshape(1, -1)

    q, k, v, xl, gl = _in_proj(x2, vec(norm1_g), w_in.astype(BF16), width, min(2 * tm, seq))
    n_pipe = seq // LANES - (ATTN_WINDOW_BLOCKS + 1)
    unroll = ATTN_UNROLL if n_pipe % ATTN_UNROLL == 0 else 2
    out_sb = _attention(q, k, v, batch, seq, LANES, ATTN_WINDOW_BLOCKS, unroll)
    out_lru = _lru(xl, gl, conv_w, vec(conv_b),
                   _block_diag(w_rgate, gate_per).astype(BF16), vec(b_rgate),
                   _block_diag(w_igate, gate_per).astype(BF16), vec(b_igate),
                   vec(lam), batch, seq, min(1024, seq))

    wr_stack, rbias = _router_tables(w_group, b_group, w_fine, b_fine)
    h, u2, rr, rwt, tcnt = _mix_route(out_sb, out_lru, x2, vec(sb_norm_g), vec(lru_norm_g),
                                      w_out.astype(BF16), vec(norm2_g), wr_stack, rbias, tm)

    n_tiles = n // tm
    assert n_tiles <= LANES, "one lane of the per-tile count table per token tile"
    c8 = tcnt[:, :n_tiles].T
    erows = jnp.sum(c8, axis=0)
    eoff = jnp.cumsum(erows) - erows
    goff = eoff[None, :] + jnp.cumsum(c8, axis=0) - c8
    loff = jnp.cumsum(c8, axis=1) - c8
    tot = jnp.sum(c8, axis=1)
    lrows = 2 * tm + N_EXPERTS * SUBLANES
    p_rows = 2 * n + n_tiles * N_EXPERTS * (SUBLANES - 1)
    p_rows = -(-p_rows // SUBLANES) * SUBLANES
    i32 = lambda a: a.reshape(-1).astype(jnp.int32)
    c8, loff, goff, tot, eoff, erows = (i32(a) for a in (c8, loff, goff, tot, eoff, erows))

    xs = _dispatch(c8, loff, goff, tot, jnp.sum(erows, keepdims=True), rr, u2, p_rows, tm, lrows, tme)
    ys = _experts(eoff, erows, xs, w_e_gate, w_e_up, w_e_down, tme)
    y = _combine(c8, loff, goff, tot, rwt, h, vec(final_g), ys, tm, lrows)
    return y.reshape(batch, seq, d)
```

```python
import functools
import math

import jax
import jax.numpy as jnp
from jax import lax
from jax.experimental import pallas as pl
from jax.experimental.pallas import tpu as pltpu

F32 = jnp.float32
BF16 = jnp.bfloat16

EPS = 1e-6
HEAD_DIM = 64
HEADS_PER_BLOCK = 2
LANES = 128
SUBLANES = 8
CONV_W = 4
RG_C = 8.0
N_GROUPS = 4
EXPERTS_PER_GROUP = 8
N_EXPERTS = N_GROUPS * EXPERTS_PER_GROUP
ROUTER_ROWS = 48
NEG_BIG = -1e30
LOG2_E = math.log2(math.e)
ATTN_STOP = 104.0 * LOG2_E
ATTN_WINDOW_BLOCKS = 3
ATTN_UNROLL = 4
HIGH_HALF = -65536
PERM_CHUNKS = 4

VMEM_LIMIT = 56 * 1024 * 1024


def _cparams(sem):
    return pltpu.CompilerParams(dimension_semantics=sem, vmem_limit_bytes=VMEM_LIMIT)


def _rms_f32(x, g):
    return x * lax.rsqrt(jnp.mean(x * x, axis=-1, keepdims=True) + EPS) * g


def _in_proj_kernel(x_ref, g_ref, w_ref, q_ref, k_ref, v_ref, xl_ref, gl_ref, w_bf, *,
                    width, q_scale):
    @pl.when(pl.program_id(0) == 0)
    def _():
        for c in range(w_ref.shape[1] // width):
            cols = slice(c * width, (c + 1) * width)
            w_bf[:, cols] = w_ref[:, cols].astype(BF16)

    u = _rms_f32(x_ref[...], g_ref[...]).astype(BF16)
    for c, o_ref in enumerate((q_ref, k_ref, v_ref, xl_ref, gl_ref)):
        p = jnp.dot(u, w_bf[:, c * width:(c + 1) * width], preferred_element_type=F32)
        if c == 0:
            p = p * q_scale
        o_ref[...] = p.astype(o_ref.dtype)


def _in_proj(x2, g, w, width, tm):
    n, d = x2.shape
    row = lambda i: (i, 0)
    out_bf = jax.ShapeDtypeStruct((n, width), BF16)
    out_f = jax.ShapeDtypeStruct((n, width), F32)
    return pl.pallas_call(
        functools.partial(_in_proj_kernel, width=width, q_scale=1.0 / math.sqrt(HEAD_DIM)),
        grid=(n // tm,),
        in_specs=[pl.BlockSpec((tm, d), row),
                  pl.BlockSpec((1, d), lambda i: (0, 0)),
                  pl.BlockSpec(w.shape, lambda i: (0, 0), pipeline_mode=pl.Buffered(1))],
        out_specs=[pl.BlockSpec((tm, width), row)] * 5,
        out_shape=[out_bf, out_bf, out_bf, out_f, out_f],
        scratch_shapes=[pltpu.VMEM(w.shape, BF16)],
        compiler_params=_cparams(("arbitrary",)),
        name="in_proj",
    )(x2, g, w)


def _attn_kernel(q_ref, k_ref, v_ref, o_ref, tri_ref, z_ref, arg_ref, ctot_ref, acc_ref, carry_ref,
                 *, tq, nwin):
    seq = q_ref.shape[0]
    win = nwin * tq
    lane = lax.broadcasted_iota(jnp.int32, (1, LANES), 1)
    r_id = lax.broadcasted_iota(jnp.int32, (tq, tq), 0)
    c_id = lax.broadcasted_iota(jnp.int32, (tq, tq), 1)
    causal = c_id < r_id
    rel = c_id - r_id

    k_r = lax.broadcasted_iota(jnp.int32, (tq, 2 * tq), 0)
    k_c = lax.broadcasted_iota(jnp.int32, (tq, 2 * tq), 1)
    tri_ref[...] = jnp.where(k_c >= tq, 1.0, jnp.where(k_r > k_c, 1.0, 0.0)).astype(BF16)

    def softplus2(z):
        return jnp.maximum(z, 0.0) + jnp.log2(1.0 + jnp.exp2(-jnp.abs(z)))

    def scores(qh, keys):
        z = LOG2_E * lax.dot_general(qh, keys, (((1,), (1,)), ((), ())),
                                     preferred_element_type=F32)
        nlog_nb = softplus2(z)
        return nlog_nb, z - nlog_nb

    def suffix(nlog_nb):
        r = jnp.dot(nlog_nb.astype(BF16), tri_ref[...], preferred_element_type=F32)
        return r[:, :tq], r[:, tq:]

    def window(qh, wstart, masks):
        nlog_nb, log_b = scores(qh, k_ref[pl.ds(wstart, win), :])
        carry = None
        parts = [None] * nwin
        for b in reversed(range(nwin)):
            nl = nlog_nb[:, b * tq:(b + 1) * tq]
            if masks[b] is not None:
                nl = jnp.where(masks[b], nl, 0.0)
            excl, tot = suffix(nl)
            arg = log_b[:, b * tq:(b + 1) * tq] - excl
            a = jnp.exp2(arg if carry is None else arg - carry)
            if masks[b] is not None:
                a = jnp.where(masks[b], a, 0.0)
            parts[b] = a.astype(BF16)
            carry = tot if carry is None else carry + tot
        out = jnp.dot(jnp.concatenate(parts, axis=1), v_ref[pl.ds(wstart, win), :],
                      preferred_element_type=F32)
        return out, carry

    def head_queries(i):
        q = q_ref[pl.ds(i * tq, tq), :]
        return [jnp.where((lane >= h * HEAD_DIM) & (lane < (h + 1) * HEAD_DIM), q, jnp.zeros_like(q))
                for h in range(HEADS_PER_BLOCK)]

    def store(i, outs):
        o_ref[pl.ds(i * tq, tq), :] = jnp.where(lane < HEAD_DIM, outs[0], outs[1])

    for i in range(nwin - 1):
        masks = [(rel + b * tq) < i * tq for b in range(nwin)]
        store(i, [window(qh, 0, masks)[0] for qh in head_queries(i)])

    def window_start(i):
        return pl.multiple_of((i - (nwin - 1)) * tq, tq)

    def stage_scores(i, p):
        keys = k_ref[pl.ds(window_start(i), win), :]
        for h, qh in enumerate(head_queries(i)):
            z_ref[p, h] = LOG2_E * lax.dot_general(qh, keys, (((1,), (1,)), ((), ())),
                                                   preferred_element_type=F32)

    def stage_exponents(p):
        carries = []
        for h in range(HEADS_PER_BLOCK):
            z = z_ref[p, h]
            softplus = softplus2(z)
            carry = None
            for b in reversed(range(nwin)):
                diag = b == nwin - 1
                cols = slice(b * tq, (b + 1) * tq)
                nl = softplus[:, cols]
                if diag:
                    nl = jnp.where(causal, nl, 0.0)
                excl, tot = suffix(nl)
                arg = z[:, cols] - softplus[:, cols] - excl
                if carry is not None:
                    arg = arg - carry
                if diag:
                    arg = jnp.where(causal, arg, NEG_BIG)
                arg_ref[p, h, :, cols] = arg
                carry = tot if carry is None else carry + tot
            ctot_ref[p, h] = carry
            carries.append(carry)
        return jnp.min(jnp.minimum(carries[0], carries[1]))

    def stage_output(i, p, s):
        vals = v_ref[pl.ds(window_start(i), win), :]
        for h in range(HEADS_PER_BLOCK):
            a = jnp.exp2(arg_ref[p, h]).astype(BF16)
            acc_ref[s, h] = jnp.dot(a, vals, preferred_element_type=F32)
            carry_ref[s, h] = ctot_ref[p, h]

    def finish(i, s, cmin):
        def cond(state):
            j, cmin = state
            return (j >= 0) & (cmin <= ATTN_STOP)

        def older(state):
            j, _ = state
            start = pl.multiple_of(j * tq, tq)
            keys = k_ref[pl.ds(start, tq), :]
            vals = v_ref[pl.ds(start, tq), :]
            cs = []
            for h, qh in enumerate(head_queries(i)):
                nlog_nb, log_b = scores(qh, keys)
                excl, tot = suffix(nlog_nb)
                carry = carry_ref[s, h]
                a = jnp.exp2(log_b - excl - carry)
                acc_ref[s, h] += jnp.dot(a.astype(BF16), vals, preferred_element_type=F32)
                carry_ref[s, h] = carry + tot
                cs.append(carry + tot)
            return j - 1, jnp.min(jnp.minimum(cs[0], cs[1]))

        lax.while_loop(cond, older, (i - nwin, cmin))
        store(i, [acc_ref[s, 0], acc_ref[s, 1]])

    first = nwin - 1
    n_blocks = seq // tq
    unroll = acc_ref.shape[0]
    slot = lambda i: (first + i) % 2
    stage_scores(first, slot(0))
    stage_scores(first + 1, slot(1))
    cmin = stage_exponents(slot(0))

    def steady(m, cmin):
        t = first + 2 + unroll * m
        cmins = [cmin]
        for u in range(unroll):
            stage_output(t - 2 + u, slot(u), u)
            cmins.append(stage_exponents(slot(u + 1)))
            stage_scores(t + u, slot(u))
        for u in range(unroll):
            finish(t - 2 + u, u, cmins[u])
        return cmins[unroll]

    cmin = lax.fori_loop(0, (n_blocks - first - 2) // unroll, steady, cmin)
    stage_output(n_blocks - 2, slot(0), 0)
    cmin_last = stage_exponents(slot(1))
    finish(n_blocks - 2, 0, cmin)
    stage_output(n_blocks - 1, slot(1), 1)
    finish(n_blocks - 1, 1, cmin_last)


def _attention(q, k, v, batch, seq, tq, nwin, unroll):
    n, width = q.shape
    assert tq == LANES and seq >= (nwin + 1) * tq
    assert unroll % 2 == 0 and (seq // tq - (nwin + 1)) % unroll == 0
    blk = pl.BlockSpec((seq, LANES), lambda b, hp: (b, hp))
    stage_buf = pltpu.VMEM((2, HEADS_PER_BLOCK, tq, nwin * tq), F32)
    carry_buf = pltpu.VMEM((2, HEADS_PER_BLOCK, tq, tq), F32)
    row_buf = pltpu.VMEM((unroll, HEADS_PER_BLOCK, tq, LANES), F32)
    return pl.pallas_call(
        functools.partial(_attn_kernel, tq=tq, nwin=nwin),
        grid=(batch, width // LANES),
        in_specs=[blk, blk, blk],
        out_specs=blk,
        out_shape=jax.ShapeDtypeStruct((n, width), F32),
        scratch_shapes=[pltpu.VMEM((tq, 2 * tq), BF16),
                        stage_buf, stage_buf, carry_buf, row_buf, row_buf],
        compiler_params=_cparams(("arbitrary", "arbitrary")),
        name="attn",
    )(q, k, v)


def _gelu_tanh(x):
    return 0.5 * x * (1.0 + jnp.tanh(math.sqrt(2.0 / math.pi) * (x + 0.044715 * (x * x * x))))


def _lru_kernel(*refs, ts, n_slab):
    xl_refs, gl_refs = refs[0:n_slab], refs[n_slab:2 * n_slab]
    cw_ref, cb_ref, wr_ref, br_ref, wi_ref, bi_ref, lam_ref = refs[2 * n_slab:2 * n_slab + 7]
    o_refs = refs[2 * n_slab + 7:3 * n_slab + 7]
    tail_ref, a7_ref, u7_ref, hp_ref, pa_ref, pu_ref, h_ref = refs[3 * n_slab + 7:]
    t = pl.program_id(1)
    groups = ts // SUBLANES

    @pl.when(t == 0)
    def _():
        tail_ref[...] = jnp.zeros_like(tail_ref)
        h_ref[...] = jnp.zeros_like(h_ref)

    first_group = lax.broadcasted_iota(jnp.int32, (groups, LANES), 0) == 0
    for c in range(n_slab):
        lanes = slice(c * LANES, (c + 1) * LANES)
        x = [xl_refs[c][pl.ds(s, groups, stride=SUBLANES), :] for s in range(SUBLANES)]
        shifted = {}
        for s in range(SUBLANES - (CONV_W - 1), SUBLANES):
            shifted[s] = jnp.where(first_group, tail_ref[c, s:s + 1, :], pltpu.roll(x[s], 1, axis=0))
            tail_ref[c, s:s + 1, :] = x[s][groups - 1:groups, :]
        conv = []
        for s in range(SUBLANES):
            y = cb_ref[:, lanes]
            for w in range(CONV_W):
                j = s - (CONV_W - 1) + w
                y = y + (x[j] if j >= 0 else shifted[j + SUBLANES]) * cw_ref[w:w + 1, lanes]
            conv.append(y)
        xc = jnp.concatenate(conv, axis=0)

        xcb = xc.astype(BF16)
        r = jax.nn.sigmoid(jnp.dot(xcb, wr_ref[c], preferred_element_type=F32) + br_ref[:, lanes])
        ig = jax.nn.sigmoid(jnp.dot(xcb, wi_ref[c], preferred_element_type=F32) + bi_ref[:, lanes])
        lam = lam_ref[:, lanes]
        log_sig_lam = -(jnp.maximum(-lam, 0.0) + jnp.log1p(jnp.exp(-jnp.abs(lam))))
        log_a = RG_C * r * log_sig_lam
        a = jnp.exp(log_a)
        th = jnp.tanh(log_a)
        u = jnp.sqrt(-2.0 * th / (1.0 - th)) * (ig * xc)

        a_run = u_run = None
        for s in range(SUBLANES):
            rows = slice(s * groups, (s + 1) * groups)
            if s == 0:
                a_run, u_run = a[rows], u[rows]
            else:
                u_run = a[rows] * u_run + u[rows]
                a_run = a_run * a[rows]
            pa_ref[c, rows, :] = a_run
            pu_ref[c, rows, :] = u_run
        a7_ref[c] = a_run
        u7_ref[c] = u_run

    def group(g, hs):
        nxt = []
        for c in range(n_slab):
            hp_ref[c, pl.ds(g, 1), :] = hs[c]
            nxt.append(a7_ref[c, pl.ds(g, 1), :] * hs[c] + u7_ref[c, pl.ds(g, 1), :])
        return tuple(nxt)

    hs = lax.fori_loop(0, groups, group, tuple(h_ref[c] for c in range(n_slab)), unroll=8)
    for c in range(n_slab):
        h_ref[c] = hs[c]

    for c in range(n_slab):
        h_in = hp_ref[c]
        for s in range(SUBLANES):
            rows = slice(s * groups, (s + 1) * groups)
            hseq = pu_ref[c, rows, :] + pa_ref[c, rows, :] * h_in
            gate = _gelu_tanh(gl_refs[c][pl.ds(s, groups, stride=SUBLANES), :])
            o_refs[c][pl.ds(s, groups, stride=SUBLANES), :] = hseq * gate


def _lru(xl, gl, conv_w, conv_b, wr_bd, br, wi_bd, bi, lam, batch, seq, ts):
    n, width = xl.shape
    nt = seq // ts
    n_slab = width // LANES
    assert wr_bd.shape == (n_slab, LANES, LANES) and ts % (SUBLANES * SUBLANES) == 0
    groups = ts // SUBLANES
    slab = [pl.BlockSpec((ts, LANES), functools.partial(lambda b, t, c: (b * nt + t, c), c=c))
            for c in range(n_slab)]
    const2 = lambda b, t: (0, 0)
    const3 = lambda b, t: (0, 0, 0)
    vec = pl.BlockSpec((1, width), const2)
    per_group = pltpu.VMEM((n_slab, groups, LANES), F32)
    per_step = pltpu.VMEM((n_slab, ts, LANES), F32)
    return pl.pallas_call(
        functools.partial(_lru_kernel, ts=ts, n_slab=n_slab),
        grid=(batch, nt),
        in_specs=slab + slab + [pl.BlockSpec((CONV_W, width), const2), vec,
                                pl.BlockSpec(wr_bd.shape, const3), vec,
                                pl.BlockSpec(wi_bd.shape, const3), vec, vec],
        out_specs=[pl.BlockSpec((ts, LANES), lambda b, t: (b * nt + t, 0))] * n_slab,
        out_shape=[jax.ShapeDtypeStruct((n, LANES), F32)] * n_slab,
        scratch_shapes=[pltpu.VMEM((n_slab, SUBLANES, LANES), F32),
                        per_group, per_group, per_group, per_step, per_step,
                        pltpu.VMEM((n_slab, 1, LANES), F32)],
        compiler_params=_cparams(("arbitrary", "arbitrary")),
        name="lru",
    )(*([xl] * n_slab), *([gl] * n_slab), conv_w, conv_b, wr_bd, br, wi_bd, bi, lam)


def _mix_route_kernel(sb_ref, *refs, tm, n_slab):
    lru_refs = refs[:n_slab]
    (x_ref, sbg_ref, lrug_ref, wo_ref, n2g_ref, wrs_ref, rb_ref,
     h_ref, u2_ref, rr_ref, rwt_ref, tc_ref, before_ref, u2s_ref, wo_bf) = refs[n_slab:]

    @pl.when(pl.program_id(0) == 0)
    def _():
        wo_bf[...] = wo_ref[...].astype(BF16)

    _mix_route_body(sb_ref, lru_refs, x_ref, sbg_ref, lrug_ref, wo_bf, n2g_ref, wrs_ref, rb_ref,
                    h_ref, u2_ref, rr_ref, rwt_ref, tc_ref, before_ref, u2s_ref, tm)


def _mix_route_body(sb_ref, lru_refs, x_ref, sbg_ref, lrug_ref, wo_ref, n2g_ref, wrs_ref, rb_ref,
                    h_ref, u2_ref, rr_ref, rwt_ref, tc_ref, before_ref, u2s_ref, tm):
    step = pl.program_id(0)
    half = sb_ref.shape[1]

    @pl.when(step == 0)
    def _():
        r_id = lax.broadcasted_iota(jnp.int32, (tm, tm), 0)
        c_id = lax.broadcasted_iota(jnp.int32, (tm, tm), 1)
        before_ref[...] = (r_id < c_id).astype(BF16)
        tc_ref[...] = jnp.zeros_like(tc_ref)
        u2s_ref[...] = jnp.zeros_like(u2s_ref)

    u2 = u2s_ref[...]
    u_hi = u2.astype(BF16)
    u_lo = (u2 - u_hi.astype(F32)).astype(BF16)
    nt_dims = (((1,), (1,)), ((), ()))
    n_rows = rb_ref.shape[0]
    both = lax.dot_general(wrs_ref[...], u_hi, nt_dims, preferred_element_type=F32)
    lt = (both[0:n_rows] + both[n_rows:2 * n_rows]
          + lax.dot_general(wrs_ref[0:n_rows, :], u_lo, nt_dims, preferred_element_type=F32)
          + rb_ref[...])

    sub = lax.broadcasted_iota(jnp.int32, (SUBLANES, tm), 0)

    def top1(x):
        m = jnp.max(x, axis=0, keepdims=True)
        idx = jnp.min(jnp.where(x == m, sub, SUBLANES), axis=0, keepdims=True)
        return m, idx

    grp = lt[0:SUBLANES, :]
    g_max, g_idx = top1(grp)
    g_p = 1.0 / jnp.sum(jnp.exp(grp - g_max), axis=0, keepdims=True)
    fine = lt[SUBLANES:2 * SUBLANES, :]
    for g in range(1, N_GROUPS):
        fine = jnp.where(g_idx == g, lt[(g + 1) * SUBLANES:(g + 2) * SUBLANES, :], fine)
    m1, i1 = top1(fine)
    m2, i2 = top1(jnp.where(sub == i1, -jnp.inf, fine))
    e2 = jnp.exp(m2 - m1)
    p1 = 1.0 / (1.0 + e2)
    w1 = g_p * p1
    w2 = g_p * (e2 * p1)
    x1 = g_idx * EXPERTS_PER_GROUP + i1
    x2 = g_idx * EXPERTS_PER_GROUP + i2

    eid = lax.broadcasted_iota(jnp.int32, (N_EXPERTS, tm), 0)
    oh1 = jnp.where(eid == x1, 1.0, 0.0)
    oh2 = jnp.where(eid == x2, 1.0, 0.0)
    pre1 = jnp.dot(oh1.astype(BF16), before_ref[...], preferred_element_type=F32)
    pre2 = jnp.dot(oh2.astype(BF16), before_ref[...], preferred_element_type=F32)
    cnt1 = jnp.sum(oh1, axis=1, keepdims=True)
    cnt2 = jnp.sum(oh2, axis=1, keepdims=True)
    seg8 = jnp.floor((cnt1 + cnt2 + (SUBLANES - 1.0)) * (1.0 / SUBLANES))
    e_r = lax.broadcasted_iota(jnp.int32, (N_EXPERTS, N_EXPERTS), 0)
    e_c = lax.broadcasted_iota(jnp.int32, (N_EXPERTS, N_EXPERTS), 1)
    lower = jnp.where(e_c < e_r, 1.0, 0.0).astype(BF16)
    seg8_b = jnp.broadcast_to(seg8, (N_EXPERTS, LANES)).astype(BF16)
    seg_off = SUBLANES * jnp.dot(lower, seg8_b, preferred_element_type=F32)[:, 0:1]
    pos1 = jnp.sum(oh1 * (pre1 + seg_off), axis=0, keepdims=True)
    pos2 = jnp.sum(oh2 * (pre2 + (seg_off + cnt1)), axis=0, keepdims=True)

    lane = lax.broadcasted_iota(jnp.int32, tc_ref.shape, 1)
    seg_rows = jnp.broadcast_to(seg8 * SUBLANES, tc_ref.shape).astype(jnp.int32)
    tc_ref[...] = jnp.where(lane == step - 1, seg_rows, tc_ref[...])

    zrow = jnp.zeros((SUBLANES - 4, tm), jnp.int32)
    rr_ref[...] = jnp.concatenate(
        [pos1.astype(jnp.int32), pos2.astype(jnp.int32), x1, x2, zrow], axis=0)
    wt = jnp.concatenate([w1, w2, pos1, pos2, jnp.zeros((LANES - 4, tm), F32)], axis=0)
    rwt_ref[...] = wt.T

    m_sb = _rms_f32(sb_ref[...], sbg_ref[...]).astype(BF16)
    lru = jnp.concatenate([r[...] for r in lru_refs], axis=-1)
    m_lru = _rms_f32(lru, lrug_ref[...]).astype(BF16)
    h = (x_ref[...]
         + jnp.dot(m_sb, wo_ref[0:half, :], preferred_element_type=F32)
         + jnp.dot(m_lru, wo_ref[half:2 * half, :], preferred_element_type=F32))
    h_ref[...] = h
    u2_next = _rms_f32(h, n2g_ref[...])
    u2_ref[...] = u2_next.astype(BF16)
    u2s_ref[...] = u2_next


def _mix_route(sb, lru, x2, sbg, lrug, w_out, n2g, wr_stack, rbias, tm):
    n, d = x2.shape
    half = sb.shape[1]
    n_tiles = n // tm
    row = lambda i: (jnp.minimum(i, n_tiles - 1), 0)
    routed = lambda i: (jnp.maximum(i - 1, 0), 0)
    const = lambda i: (0, 0)
    return pl.pallas_call(
        functools.partial(_mix_route_kernel, tm=tm, n_slab=len(lru)),
        grid=(n_tiles + 1,),
        in_specs=[pl.BlockSpec((tm, half), row)] + [pl.BlockSpec((tm, LANES), row)] * len(lru)
                 + [pl.BlockSpec((tm, d), row),
                  pl.BlockSpec((1, half), const), pl.BlockSpec((1, half), const),
                  pl.BlockSpec(w_out.shape, const, pipeline_mode=pl.Buffered(1)),
                  pl.BlockSpec((1, d), const),
                  pl.BlockSpec(wr_stack.shape, const), pl.BlockSpec(rbias.shape, const)],
        out_specs=[pl.BlockSpec((tm, d), row), pl.BlockSpec((tm, d), row),
                   pl.BlockSpec((SUBLANES, tm), lambda i: (0, jnp.maximum(i - 1, 0))),
                   pl.BlockSpec((tm, LANES), routed),
                   pl.BlockSpec((N_EXPERTS, LANES), const)],
        out_shape=[jax.ShapeDtypeStruct((n, d), F32), jax.ShapeDtypeStruct((n, d), BF16),
                   jax.ShapeDtypeStruct((SUBLANES, n), jnp.int32),
                   jax.ShapeDtypeStruct((n, LANES), F32),
                   jax.ShapeDtypeStruct((N_EXPERTS, LANES), jnp.int32)],
        scratch_shapes=[pltpu.VMEM((tm, tm), BF16), pltpu.VMEM((tm, d), F32),
                        pltpu.VMEM(w_out.shape, BF16)],
        compiler_params=_cparams(("arbitrary",)),
        name="mix_route",
    )(sb, *lru, x2, sbg, lrug, w_out, n2g, wr_stack, rbias)


def _pack_halves(x):
    half = x.shape[1] // 2
    lo = lax.shift_right_logical(lax.bitcast_convert_type(x[:, :half], jnp.int32), 16)
    hi = lax.bitcast_convert_type(x[:, half:], jnp.int32) & HIGH_HALF
    return hi | lo


def _unpack_halves(p):
    lo = lax.bitcast_convert_type(lax.shift_left(p, 16), F32)
    hi = lax.bitcast_convert_type(p & HIGH_HALF, F32)
    return lo.astype(BF16), hi.astype(BF16)


def _segment_copies(tile, c8_ref, loff_ref, goff_ref, make):
    for e in range(N_EXPERTS):
        idx = tile * N_EXPERTS + e
        rows = pl.multiple_of(c8_ref[idx], SUBLANES)

        @pl.when(rows > 0)
        def _(idx=idx, rows=rows):
            lo = pl.multiple_of(loff_ref[idx], SUBLANES)
            go = pl.multiple_of(goff_ref[idx], SUBLANES)
            make(pl.ds(lo, rows), pl.ds(go, rows)).start()


def _dispatch_kernel(c8_ref, loff_ref, goff_ref, tot_ref, used_ref,
                     rr_ref, u2_ref, xs_ref, lbuf, zbuf, sem, zsem, *, td, lrows):
    i = pl.program_id(0)
    slot = i % 2

    chunk = lrows // PERM_CHUNKS
    for c in range(PERM_CHUNKS):
        r_id = lax.broadcasted_iota(jnp.int32, (chunk, td), 0) + c * chunk
        perm = jnp.where(r_id == rr_ref[0:1, :], 1.0, jnp.where(r_id == rr_ref[1:2, :], 1.0, 0.0))
        sorted_rows = jnp.dot(perm.astype(BF16), u2_ref[...], preferred_element_type=F32)
        lbuf[slot, c * chunk:(c + 1) * chunk] = _pack_halves(sorted_rows)

    @pl.when(i == 0)
    def _():
        zbuf[...] = jnp.zeros_like(zbuf)
        chunk = zbuf.shape[0]
        used = used_ref[0]
        spare = xs_ref.shape[0] - used
        n_fill = (spare + chunk - 1) // chunk

        def fill_copy(k):
            rows = pl.multiple_of(jnp.minimum(chunk, spare - k * chunk), SUBLANES)
            start = pl.multiple_of(used + k * chunk, SUBLANES)
            return pltpu.make_async_copy(zbuf.at[pl.ds(0, rows)], xs_ref.at[pl.ds(start, rows)],
                                         zsem)

        def fill_start(k, c):
            fill_copy(k).start()
            return c

        def fill_wait(k, c):
            fill_copy(k).wait()
            return c

        lax.fori_loop(0, n_fill, fill_start, 0)
        lax.fori_loop(0, n_fill, fill_wait, 0)

    def wait_tile(tile, s):
        rows = pl.multiple_of(tot_ref[tile], SUBLANES)
        pltpu.make_async_copy(lbuf.at[s, pl.ds(0, rows)], xs_ref.at[pl.ds(0, rows)], sem).wait()

    @pl.when(i > 0)
    def _():
        wait_tile(i - 1, 1 - slot)

    _segment_copies(i, c8_ref, loff_ref, goff_ref,
                    lambda loc, glob: pltpu.make_async_copy(lbuf.at[slot, loc], xs_ref.at[glob], sem))

    @pl.when(i == pl.num_programs(0) - 1)
    def _():
        wait_tile(i, slot)


def _dispatch(c8, loff, goff, tot, used, rr, u2, p_rows, td, lrows, tme):
    n, d = u2.shape
    pmap = lambda i, *_: (0, i)
    return pl.pallas_call(
        functools.partial(_dispatch_kernel, td=td, lrows=lrows),
        grid_spec=pltpu.PrefetchScalarGridSpec(
            num_scalar_prefetch=5,
            grid=(n // td,),
            in_specs=[pl.BlockSpec((SUBLANES, td), pmap),
                      pl.BlockSpec((td, d), lambda i, *_: (i, 0))],
            out_specs=pl.BlockSpec(memory_space=pl.ANY),
            scratch_shapes=[pltpu.VMEM((2, lrows, d // 2), jnp.int32),
                            pltpu.VMEM((tme, d // 2), jnp.int32),
                            pltpu.SemaphoreType.DMA(()), pltpu.SemaphoreType.DMA(())]),
        out_shape=jax.ShapeDtypeStruct((p_rows, d // 2), jnp.int32),
        compiler_params=_cparams(("arbitrary",)),
        name="dispatch",
    )(c8, loff, goff, tot, used, rr, u2)


def _experts_kernel(eoff_ref, erows_ref, xs_ref, wg_ref, wu_ref, wd_ref, ys_ref,
                    wg_bf, wu_bf, wd_bf, xbuf, ybuf, sem_in, sem_out, state, *, tme):
    e = pl.program_id(0)
    n_experts = pl.num_programs(0)
    rows = erows_ref[e]
    off = eoff_ref[e]
    n_tiles = (rows + tme - 1) // tme

    def tile_rows(total, k):
        return pl.multiple_of(jnp.minimum(tme, total - k * tme), SUBLANES)

    def in_copy(start, r, slot):
        start = pl.multiple_of(start, SUBLANES)
        return pltpu.make_async_copy(xs_ref.at[pl.ds(start, r)], xbuf.at[slot, pl.ds(0, r)],
                                     sem_in.at[slot])

    def out_copy(start, r, slot):
        start = pl.multiple_of(start, SUBLANES)
        return pltpu.make_async_copy(ybuf.at[slot, pl.ds(0, r)], ys_ref.at[pl.ds(start, r)],
                                     sem_out.at[slot])

    @pl.when(e == 0)
    def _():
        for s in range(4):
            state[s] = 0
        xbuf[...] = jnp.zeros_like(xbuf)

    @pl.when(rows > 0)
    def _():
        wg_bf[...] = wg_ref[0].astype(BF16)
        wu_bf[...] = wu_ref[0].astype(BF16)
        wd_bf[...] = wd_ref[0].astype(BF16)
        done = state[0]

        @pl.when(state[1] == 0)
        def _():
            in_copy(off, tile_rows(rows, 0), done % 2).start()

        def tile(k, c):
            slot = (done + k) % 2
            r = tile_rows(rows, k)

            @pl.when(k + 1 < n_tiles)
            def _():
                in_copy(off + (k + 1) * tme, tile_rows(rows, k + 1), 1 - slot).start()

            in_copy(off, r, slot).wait()

            @pl.when(done + k >= 2)
            def _():
                out_copy(0, pl.multiple_of(state[2 + slot], SUBLANES), slot).wait()

            def mlp(n):
                x_lo, x_hi = _unpack_halves(xbuf[slot, 0:n])
                half = x_lo.shape[1]
                hg = (jnp.dot(x_lo, wg_bf[0:half, :], preferred_element_type=F32)
                      + jnp.dot(x_hi, wg_bf[half:2 * half, :], preferred_element_type=F32))
                hu = (jnp.dot(x_lo, wu_bf[0:half, :], preferred_element_type=F32)
                      + jnp.dot(x_hi, wu_bf[half:2 * half, :], preferred_element_type=F32))
                act = (hg * jax.nn.sigmoid(hg) * hu).astype(BF16)
                y = jnp.dot(act, wd_bf[...], preferred_element_type=F32)
                ybuf[slot, 0:n] = _pack_halves(y.astype(BF16).astype(F32))

            @pl.when(r > tme // 2)
            def _():
                mlp(tme)

            @pl.when(r <= tme // 2)
            def _():
                mlp(tme // 2)

            out_copy(off + k * tme, r, slot).start()
            state[2 + slot] = r
            return c

        lax.fori_loop(0, n_tiles, tile, 0)
        state[0] = done + n_tiles

    nxt = jnp.minimum(e + 1, n_experts - 1)
    prefetch = (rows > 0) & (e + 1 < n_experts) & (erows_ref[nxt] > 0)
    state[1] = prefetch.astype(jnp.int32)

    @pl.when(prefetch)
    def _():
        in_copy(eoff_ref[nxt], tile_rows(erows_ref[nxt], 0), state[0] % 2).start()

    @pl.when(e == n_experts - 1)
    def _():
        total = eoff_ref[e] + rows
        for slot in range(2):
            @pl.when(state[0] > slot)
            def _(slot=slot):
                out_copy(0, pl.multiple_of(state[2 + slot], SUBLANES), slot).wait()
        ybuf[0] = jnp.zeros(ybuf.shape[1:], ybuf.dtype)
        spare = ys_ref.shape[0] - total
        n_fill = (spare + tme - 1) // tme

        def fill_copy(k):
            return out_copy(total + k * tme, tile_rows(spare, k), 0)

        def fill_start(k, c):
            fill_copy(k).start()
            return c

        def fill_wait(k, c):
            fill_copy(k).wait()
            return c

        lax.fori_loop(0, n_fill, fill_start, 0)
        lax.fori_loop(0, n_fill, fill_wait, 0)


def _experts(eoff, erows, xs, wg, wu, wd, tme):
    p = xs.shape[0]
    n_experts, d, de = wg.shape
    wmap = lambda e, *_: (e, 0, 0)
    return pl.pallas_call(
        functools.partial(_experts_kernel, tme=tme),
        grid_spec=pltpu.PrefetchScalarGridSpec(
            num_scalar_prefetch=2,
            grid=(n_experts,),
            in_specs=[pl.BlockSpec(memory_space=pl.ANY),
                      pl.BlockSpec((1, d, de), wmap), pl.BlockSpec((1, d, de), wmap),
                      pl.BlockSpec((1, de, d), wmap)],
            out_specs=pl.BlockSpec(memory_space=pl.ANY),
            scratch_shapes=[pltpu.VMEM((d, de), BF16), pltpu.VMEM((d, de), BF16),
                            pltpu.VMEM((de, d), BF16),
                            pltpu.VMEM((2, tme, d // 2), jnp.int32),
                            pltpu.VMEM((2, tme, d // 2), jnp.int32),
                            pltpu.SemaphoreType.DMA((2,)), pltpu.SemaphoreType.DMA((2,)),
                            pltpu.SMEM((4,), jnp.int32)]),
        out_shape=jax.ShapeDtypeStruct((p, d // 2), jnp.int32),
        compiler_params=_cparams(("arbitrary",)),
        name="experts",
    )(eoff, erows, xs, wg, wu, wd)


def _combine_kernel(c8_ref, loff_ref, goff_ref, tot_ref,
                    rwt_ref, h_ref, fg_ref, ys_ref, y_ref, ybuf, sems, *, tc, lrows):
    i = pl.program_id(0)
    slot = i % 2

    def gather_tile(tile, s):
        _segment_copies(tile, c8_ref, loff_ref, goff_ref,
                        lambda loc, glob: pltpu.make_async_copy(ys_ref.at[glob], ybuf.at[s, loc],
                                                                sems.at[s]))

    @pl.when(i == 0)
    def _():
        ybuf[...] = jnp.zeros_like(ybuf)
        gather_tile(0, 0)

    @pl.when(i + 1 < pl.num_programs(0))
    def _():
        gather_tile(i + 1, 1 - slot)

    rows = pl.multiple_of(tot_ref[i], SUBLANES)
    pltpu.make_async_copy(ys_ref.at[pl.ds(0, rows)], ybuf.at[slot, pl.ds(0, rows)],
                          sems.at[slot]).wait()

    w = rwt_ref[...]
    c_id = lax.broadcasted_iota(jnp.int32, (tc, lrows), 1)
    pos1 = w[:, 2:3].astype(jnp.int32)
    pos2 = w[:, 3:4].astype(jnp.int32)
    wmat = (jnp.where(c_id == pos1, w[:, 0:1], 0.0)
            + jnp.where(c_id == pos2, w[:, 1:2], 0.0)).astype(BF16)
    y_lo, y_hi = _unpack_halves(ybuf[slot])
    moe = jnp.concatenate([jnp.dot(wmat, y_lo, preferred_element_type=F32),
                           jnp.dot(wmat, y_hi, preferred_element_type=F32)], axis=-1)
    y_ref[...] = _rms_f32(h_ref[...] + moe, fg_ref[...])


def _combine(c8, loff, goff, tot, rwt, h, final_g, ys, tc, lrows):
    n, d = h.shape
    return pl.pallas_call(
        functools.partial(_combine_kernel, tc=tc, lrows=lrows),
        grid_spec=pltpu.PrefetchScalarGridSpec(
            num_scalar_prefetch=4,
            grid=(n // tc,),
            in_specs=[pl.BlockSpec((tc, LANES), lambda i, *_: (i, 0)),
                      pl.BlockSpec((tc, d), lambda i, *_: (i, 0)),
                      pl.BlockSpec((1, d), lambda i, *_: (0, 0)),
                      pl.BlockSpec(memory_space=pl.ANY)],
            out_specs=pl.BlockSpec((tc, d), lambda i, *_: (i, 0)),
            scratch_shapes=[pltpu.VMEM((2, lrows, d // 2), jnp.int32),
                            pltpu.SemaphoreType.DMA((2,))]),
        out_shape=jax.ShapeDtypeStruct((n, d), F32),
        compiler_params=_cparams(("arbitrary",)),
        name="combine",
    )(c8, loff, goff, tot, rwt, h, final_g, ys)


def _block_diag(w, per):
    nb, c, _ = w.shape
    eye = jnp.eye(per, dtype=w.dtype)
    wg = w.reshape(nb // per, per, c, c)
    return jnp.einsum("gpij,pq->gpiqj", wg, eye).reshape(nb // per, per * c, per * c)


def _router_tables(w_group, b_group, w_fine, b_fine):
    d = w_group.shape[0]
    w = jnp.zeros((ROUTER_ROWS, d), F32)
    w = w.at[0:N_GROUPS].set(w_group.T).at[SUBLANES:SUBLANES + N_EXPERTS].set(w_fine.T)
    b = jnp.full((ROUTER_ROWS,), NEG_BIG, F32)
    b = b.at[0:N_GROUPS].set(b_group).at[SUBLANES:SUBLANES + N_EXPERTS].set(b_fine)
    w_hi = w.astype(BF16)
    w_lo = (w - w_hi.astype(F32)).astype(BF16)
    return jnp.concatenate([w_hi, w_lo], axis=0), b.reshape(ROUTER_ROWS, 1)


def kernel(x, norm1_g, w_in, conv_w, conv_b, w_rgate, b_rgate, w_igate, b_igate, lam, sb_norm_g,
           lru_norm_g, w_out, norm2_g, w_group, b_group, w_fine, b_fine, w_e_gate, w_e_up,
           w_e_down, final_g):
    batch, seq, d = x.shape
    n = batch * seq
    width = w_in.shape[1] // 5
    tm = min(512, seq)
    tme = 512
    gate_per = LANES // w_rgate.shape[1]

    x2 = x.reshape(n, d)
    vec = lambda a: a.reshape(1, -1)

    q, k, v, xl, gl = _in_proj(x2, vec(norm1_g), w_in, width, min(2 * tm, seq))
    n_pipe = seq // LANES - (ATTN_WINDOW_BLOCKS + 1)
    unroll = ATTN_UNROLL if n_pipe % ATTN_UNROLL == 0 else 2
    out_sb = _attention(q, k, v, batch, seq, LANES, ATTN_WINDOW_BLOCKS, unroll)
    out_lru = _lru(xl, gl, conv_w, vec(conv_b),
                   _block_diag(w_rgate, gate_per).astype(BF16), vec(b_rgate),
                   _block_diag(w_igate, gate_per).astype(BF16), vec(b_igate),
                   vec(lam), batch, seq, min(1024, seq))

    wr_stack, rbias = _router_tables(w_group, b_group, w_fine, b_fine)
    h, u2, rr, rwt, tcnt = _mix_route(out_sb, out_lru, x2, vec(sb_norm_g), vec(lru_norm_g),
                                      w_out, vec(norm2_g), wr_stack, rbias, tm)

    n_tiles = n // tm
    assert n_tiles <= LANES, "one lane of the per-tile count table per token tile"
    c8 = tcnt[:, :n_tiles].T
    erows = jnp.sum(c8, axis=0)
    eoff = jnp.cumsum(erows) - erows
    goff = eoff[None, :] + jnp.cumsum(c8, axis=0) - c8
    loff = jnp.cumsum(c8, axis=1) - c8
    tot = jnp.sum(c8, axis=1)
    lrows = 2 * tm + N_EXPERTS * SUBLANES
    p_rows = 2 * n + n_tiles * N_EXPERTS * (SUBLANES - 1)
    p_rows = -(-p_rows // SUBLANES) * SUBLANES
    i32 = lambda a: a.reshape(-1).astype(jnp.int32)
    c8, loff, goff, tot, eoff, erows = (i32(a) for a in (c8, loff, goff, tot, eoff, erows))

    xs = _dispatch(c8, loff, goff, tot, jnp.sum(erows, keepdims=True), rr, u2, p_rows, tm, lrows, tme)
    ys = _experts(eoff, erows, xs, w_e_gate, w_e_up, w_e_down, tme)
    y = _combine(c8, loff, goff, tot, rwt, h, vec(final_g), ys, tm, lrows)
    return y.reshape(batch, seq, d)
```

```python
import functools
import math

import jax
import jax.numpy as jnp
from jax import lax
from jax.experimental import pallas as pl
from jax.experimental.pallas import tpu as pltpu

F32 = jnp.float32
BF16 = jnp.bfloat16

EPS = 1e-6
HEAD_DIM = 64
HEADS_PER_BLOCK = 2
LANES = 128
SUBLANES = 8
CONV_W = 4
RG_C = 8.0
N_GROUPS = 4
EXPERTS_PER_GROUP = 8
N_EXPERTS = N_GROUPS * EXPERTS_PER_GROUP
ROUTER_ROWS = 48
NEG_BIG = -1e30
LOG2_E = math.log2(math.e)
ATTN_STOP = 104.0 * LOG2_E
ATTN_WINDOW_BLOCKS = 3
ATTN_UNROLL = 4
HIGH_HALF = -65536

VMEM_LIMIT = 56 * 1024 * 1024


def _cparams(sem):
    return pltpu.CompilerParams(dimension_semantics=sem, vmem_limit_bytes=VMEM_LIMIT)


def _rms_f32(x, g):
    return x * lax.rsqrt(jnp.mean(x * x, axis=-1, keepdims=True) + EPS) * g


def _in_proj_kernel(x_ref, g_ref, w_ref, q_ref, k_ref, v_ref, xl_ref, gl_ref, w_bf, *,
                    width, q_scale):
    @pl.when(pl.program_id(0) == 0)
    def _():
        for c in range(w_ref.shape[1] // width):
            cols = slice(c * width, (c + 1) * width)
            w_bf[:, cols] = w_ref[:, cols].astype(BF16)

    u = _rms_f32(x_ref[...], g_ref[...]).astype(BF16)
    for c, o_ref in enumerate((q_ref, k_ref, v_ref, xl_ref, gl_ref)):
        p = jnp.dot(u, w_bf[:, c * width:(c + 1) * width], preferred_element_type=F32)
        if c == 0:
            p = p * q_scale
        o_ref[...] = p.astype(o_ref.dtype)


def _in_proj(x2, g, w, width, tm):
    n, d = x2.shape
    row = lambda i: (i, 0)
    out_bf = jax.ShapeDtypeStruct((n, width), BF16)
    out_f = jax.ShapeDtypeStruct((n, width), F32)
    return pl.pallas_call(
        functools.partial(_in_proj_kernel, width=width, q_scale=1.0 / math.sqrt(HEAD_DIM)),
        grid=(n // tm,),
        in_specs=[pl.BlockSpec((tm, d), row),
                  pl.BlockSpec((1, d), lambda i: (0, 0)),
                  pl.BlockSpec(w.shape, lambda i: (0, 0), pipeline_mode=pl.Buffered(1))],
        out_specs=[pl.BlockSpec((tm, width), row)] * 5,
        out_shape=[out_bf, out_bf, out_bf, out_f, out_f],
        scratch_shapes=[pltpu.VMEM(w.shape, BF16)],
        compiler_params=_cparams(("arbitrary",)),
        name="in_proj",
    )(x2, g, w)


def _attn_kernel(q_ref, k_ref, v_ref, o_ref, tri_ref, z_ref, arg_ref, ctot_ref, acc_ref, carry_ref,
                 *, tq, nwin):
    seq = q_ref.shape[0]
    win = nwin * tq
    lane = lax.broadcasted_iota(jnp.int32, (1, LANES), 1)
    r_id = lax.broadcasted_iota(jnp.int32, (tq, tq), 0)
    c_id = lax.broadcasted_iota(jnp.int32, (tq, tq), 1)
    causal = c_id < r_id
    rel = c_id - r_id

    k_r = lax.broadcasted_iota(jnp.int32, (tq, 2 * tq), 0)
    k_c = lax.broadcasted_iota(jnp.int32, (tq, 2 * tq), 1)
    tri_ref[...] = jnp.where(k_c >= tq, 1.0, jnp.where(k_r > k_c, 1.0, 0.0)).astype(BF16)

    def softplus2(z):
        return jnp.maximum(z, 0.0) + jnp.log2(1.0 + jnp.exp2(-jnp.abs(z)))

    def scores(qh, keys):
        z = LOG2_E * lax.dot_general(qh, keys, (((1,), (1,)), ((), ())),
                                     preferred_element_type=F32)
        nlog_nb = softplus2(z)
        return nlog_nb, z - nlog_nb

    def suffix(nlog_nb):
        r = jnp.dot(nlog_nb.astype(BF16), tri_ref[...], preferred_element_type=F32)
        return r[:, :tq], r[:, tq:]

    def window(qh, wstart, masks):
        nlog_nb, log_b = scores(qh, k_ref[pl.ds(wstart, win), :])
        carry = None
        parts = [None] * nwin
        for b in reversed(range(nwin)):
            nl = nlog_nb[:, b * tq:(b + 1) * tq]
            if masks[b] is not None:
                nl = jnp.where(masks[b], nl, 0.0)
            excl, tot = suffix(nl)
            arg = log_b[:, b * tq:(b + 1) * tq] - excl
            a = jnp.exp2(arg if carry is None else arg - carry)
            if masks[b] is not None:
                a = jnp.where(masks[b], a, 0.0)
            parts[b] = a.astype(BF16)
            carry = tot if carry is None else carry + tot
        out = jnp.dot(jnp.concatenate(parts, axis=1), v_ref[pl.ds(wstart, win), :],
                      preferred_element_type=F32)
        return out, carry

    def head_queries(i):
        q = q_ref[pl.ds(i * tq, tq), :]
        return [jnp.where((lane >= h * HEAD_DIM) & (lane < (h + 1) * HEAD_DIM), q, jnp.zeros_like(q))
                for h in range(HEADS_PER_BLOCK)]

    def store(i, outs):
        o_ref[pl.ds(i * tq, tq), :] = jnp.where(lane < HEAD_DIM, outs[0], outs[1])

    for i in range(nwin - 1):
        masks = [(rel + b * tq) < i * tq for b in range(nwin)]
        store(i, [window(qh, 0, masks)[0] for qh in head_queries(i)])

    def window_start(i):
        return pl.multiple_of((i - (nwin - 1)) * tq, tq)

    def stage_scores(i, p):
        keys = k_ref[pl.ds(window_start(i), win), :]
        for h, qh in enumerate(head_queries(i)):
            z_ref[p, h] = LOG2_E * lax.dot_general(qh, keys, (((1,), (1,)), ((), ())),
                                                   preferred_element_type=F32)

    def stage_exponents(p):
        carries = []
        for h in range(HEADS_PER_BLOCK):
            z = z_ref[p, h]
            softplus = softplus2(z)
            carry = None
            for b in reversed(range(nwin)):
                diag = b == nwin - 1
                cols = slice(b * tq, (b + 1) * tq)
                nl = softplus[:, cols]
                if diag:
                    nl = jnp.where(causal, nl, 0.0)
                excl, tot = suffix(nl)
                arg = z[:, cols] - softplus[:, cols] - excl
                if carry is not None:
                    arg = arg - carry
                if diag:
                    arg = jnp.where(causal, arg, NEG_BIG)
                arg_ref[p, h, :, cols] = arg
                carry = tot if carry is None else carry + tot
            ctot_ref[p, h] = carry
            carries.append(carry)
        return jnp.min(jnp.minimum(carries[0], carries[1]))

    def stage_output(i, p, s):
        vals = v_ref[pl.ds(window_start(i), win), :]
        for h in range(HEADS_PER_BLOCK):
            a = jnp.exp2(arg_ref[p, h]).astype(BF16)
            acc_ref[s, h] = jnp.dot(a, vals, preferred_element_type=F32)
            carry_ref[s, h] = ctot_ref[p, h]

    def finish(i, s, cmin):
        def cond(state):
            j, cmin = state
            return (j >= 0) & (cmin <= ATTN_STOP)

        def older(state):
            j, _ = state
            start = pl.multiple_of(j * tq, tq)
            keys = k_ref[pl.ds(start, tq), :]
            vals = v_ref[pl.ds(start, tq), :]
            cs = []
            for h, qh in enumerate(head_queries(i)):
                nlog_nb, log_b = scores(qh, keys)
                excl, tot = suffix(nlog_nb)
                carry = carry_ref[s, h]
                a = jnp.exp2(log_b - excl - carry)
                acc_ref[s, h] += jnp.dot(a.astype(BF16), vals, preferred_element_type=F32)
                carry_ref[s, h] = carry + tot
                cs.append(carry + tot)
            return j - 1, jnp.min(jnp.minimum(cs[0], cs[1]))

        lax.while_loop(cond, older, (i - nwin, cmin))
        store(i, [acc_ref[s, 0], acc_ref[s, 1]])

    first = nwin - 1
    n_blocks = seq // tq
    unroll = acc_ref.shape[0]
    slot = lambda i: (first + i) % 2
    stage_scores(first, slot(0))
    stage_scores(first + 1, slot(1))
    cmin = stage_exponents(slot(0))

    def steady(m, cmin):
        t = first + 2 + unroll * m
        cmins = [cmin]
        for u in range(unroll):
            stage_output(t - 2 + u, slot(u), u)
            cmins.append(stage_exponents(slot(u + 1)))
            stage_scores(t + u, slot(u))
        for u in range(unroll):
            finish(t - 2 + u, u, cmins[u])
        return cmins[unroll]

    cmin = lax.fori_loop(0, (n_blocks - first - 2) // unroll, steady, cmin)
    stage_output(n_blocks - 2, slot(0), 0)
    cmin_last = stage_exponents(slot(1))
    finish(n_blocks - 2, 0, cmin)
    stage_output(n_blocks - 1, slot(1), 1)
    finish(n_blocks - 1, 1, cmin_last)


def _attention(q, k, v, batch, seq, tq, nwin, unroll):
    n, width = q.shape
    assert tq == LANES and seq >= (nwin + 1) * tq
    assert unroll % 2 == 0 and (seq // tq - (nwin + 1)) % unroll == 0
    blk = pl.BlockSpec((seq, LANES), lambda b, hp: (b, hp))
    stage_buf = pltpu.VMEM((2, HEADS_PER_BLOCK, tq, nwin * tq), F32)
    carry_buf = pltpu.VMEM((2, HEADS_PER_BLOCK, tq, tq), F32)
    row_buf = pltpu.VMEM((unroll, HEADS_PER_BLOCK, tq, LANES), F32)
    return pl.pallas_call(
        functools.partial(_attn_kernel, tq=tq, nwin=nwin),
        grid=(batch, width // LANES),
        in_specs=[blk, blk, blk],
        out_specs=blk,
        out_shape=jax.ShapeDtypeStruct((n, width), F32),
        scratch_shapes=[pltpu.VMEM((tq, 2 * tq), BF16),
                        stage_buf, stage_buf, carry_buf, row_buf, row_buf],
        compiler_params=_cparams(("arbitrary", "arbitrary")),
        name="attn",
    )(q, k, v)


def _gelu_tanh(x):
    return 0.5 * x * (1.0 + jnp.tanh(math.sqrt(2.0 / math.pi) * (x + 0.044715 * (x * x * x))))


def _lru_kernel(*refs, ts, n_slab):
    xl_refs, gl_refs = refs[0:n_slab], refs[n_slab:2 * n_slab]
    cw_ref, cb_ref, wr_ref, br_ref, wi_ref, bi_ref, lam_ref = refs[2 * n_slab:2 * n_slab + 7]
    o_refs = refs[2 * n_slab + 7:3 * n_slab + 7]
    tail_ref, a7_ref, u7_ref, hp_ref, pa_ref, pu_ref, h_ref = refs[3 * n_slab + 7:]
    t = pl.program_id(1)
    groups = ts // SUBLANES

    @pl.when(t == 0)
    def _():
        tail_ref[...] = jnp.zeros_like(tail_ref)
        h_ref[...] = jnp.zeros_like(h_ref)

    first_group = lax.broadcasted_iota(jnp.int32, (groups, LANES), 0) == 0
    for c in range(n_slab):
        lanes = slice(c * LANES, (c + 1) * LANES)
        x = [xl_refs[c][pl.ds(s, groups, stride=SUBLANES), :] for s in range(SUBLANES)]
        shifted = {}
        for s in range(SUBLANES - (CONV_W - 1), SUBLANES):
            shifted[s] = jnp.where(first_group, tail_ref[c, s:s + 1, :], pltpu.roll(x[s], 1, axis=0))
            tail_ref[c, s:s + 1, :] = x[s][groups - 1:groups, :]
        conv = []
        for s in range(SUBLANES):
            y = cb_ref[:, lanes]
            for w in range(CONV_W):
                j = s - (CONV_W - 1) + w
                y = y + (x[j] if j >= 0 else shifted[j + SUBLANES]) * cw_ref[w:w + 1, lanes]
            conv.append(y)
        xc = jnp.concatenate(conv, axis=0)

        xcb = xc.astype(BF16)
        r = jax.nn.sigmoid(jnp.dot(xcb, wr_ref[c], preferred_element_type=F32) + br_ref[:, lanes])
        ig = jax.nn.sigmoid(jnp.dot(xcb, wi_ref[c], preferred_element_type=F32) + bi_ref[:, lanes])
        lam = lam_ref[:, lanes]
        log_sig_lam = -(jnp.maximum(-lam, 0.0) + jnp.log1p(jnp.exp(-jnp.abs(lam))))
        log_a = RG_C * r * log_sig_lam
        a = jnp.exp(log_a)
        th = jnp.tanh(log_a)
        u = jnp.sqrt(-2.0 * th / (1.0 - th)) * (ig * xc)

        a_run = u_run = None
        for s in range(SUBLANES):
            rows = slice(s * groups, (s + 1) * groups)
            if s == 0:
                a_run, u_run = a[rows], u[rows]
            else:
                u_run = a[rows] * u_run + u[rows]
                a_run = a_run * a[rows]
            pa_ref[c, rows, :] = a_run
            pu_ref[c, rows, :] = u_run
        a7_ref[c] = a_run
        u7_ref[c] = u_run

    def group(g, hs):
        nxt = []
        for c in range(n_slab):
            hp_ref[c, pl.ds(g, 1), :] = hs[c]
            nxt.append(a7_ref[c, pl.ds(g, 1), :] * hs[c] + u7_ref[c, pl.ds(g, 1), :])
        return tuple(nxt)

    hs = lax.fori_loop(0, groups, group, tuple(h_ref[c] for c in range(n_slab)), unroll=8)
    for c in range(n_slab):
        h_ref[c] = hs[c]

    for c in range(n_slab):
        h_in = hp_ref[c]
        for s in range(SUBLANES):
            rows = slice(s * groups, (s + 1) * groups)
            hseq = pu_ref[c, rows, :] + pa_ref[c, rows, :] * h_in
            gate = _gelu_tanh(gl_refs[c][pl.ds(s, groups, stride=SUBLANES), :])
            o_refs[c][pl.ds(s, groups, stride=SUBLANES), :] = hseq * gate


def _lru(xl, gl, conv_w, conv_b, wr_bd, br, wi_bd, bi, lam, batch, seq, ts):
    n, width = xl.shape
    nt = seq // ts
    n_slab = width // LANES
    assert wr_bd.shape == (n_slab, LANES, LANES) and ts % (SUBLANES * SUBLANES) == 0
    groups = ts // SUBLANES
    slab = [pl.BlockSpec((ts, LANES), functools.partial(lambda b, t, c: (b * nt + t, c), c=c))
            for c in range(n_slab)]
    const2 = lambda b, t: (0, 0)
    const3 = lambda b, t: (0, 0, 0)
    vec = pl.BlockSpec((1, width), const2)
    per_group = pltpu.VMEM((n_slab, groups, LANES), F32)
    per_step = pltpu.VMEM((n_slab, ts, LANES), F32)
    return pl.pallas_call(
        functools.partial(_lru_kernel, ts=ts, n_slab=n_slab),
        grid=(batch, nt),
        in_specs=slab + slab + [pl.BlockSpec((CONV_W, width), const2), vec,
                                pl.BlockSpec(wr_bd.shape, const3), vec,
                                pl.BlockSpec(wi_bd.shape, const3), vec, vec],
        out_specs=[pl.BlockSpec((ts, LANES), lambda b, t: (b * nt + t, 0))] * n_slab,
        out_shape=[jax.ShapeDtypeStruct((n, LANES), F32)] * n_slab,
        scratch_shapes=[pltpu.VMEM((n_slab, SUBLANES, LANES), F32),
                        per_group, per_group, per_group, per_step, per_step,
                        pltpu.VMEM((n_slab, 1, LANES), F32)],
        compiler_params=_cparams(("arbitrary", "arbitrary")),
        name="lru",
    )(*([xl] * n_slab), *([gl] * n_slab), conv_w, conv_b, wr_bd, br, wi_bd, bi, lam)


def _mix_route_kernel(sb_ref, *refs, tm, n_slab):
    lru_refs = refs[:n_slab]
    (x_ref, sbg_ref, lrug_ref, wo_ref, n2g_ref, wrs_ref, rb_ref,
     h_ref, u2_ref, rr_ref, rwt_ref, tc_ref, before_ref, u2s_ref, wo_bf) = refs[n_slab:]

    @pl.when(pl.program_id(0) == 0)
    def _():
        wo_bf[...] = wo_ref[...].astype(BF16)

    _mix_route_body(sb_ref, lru_refs, x_ref, sbg_ref, lrug_ref, wo_bf, n2g_ref, wrs_ref, rb_ref,
                    h_ref, u2_ref, rr_ref, rwt_ref, tc_ref, before_ref, u2s_ref, tm)


def _mix_route_body(sb_ref, lru_refs, x_ref, sbg_ref, lrug_ref, wo_ref, n2g_ref, wrs_ref, rb_ref,
                    h_ref, u2_ref, rr_ref, rwt_ref, tc_ref, before_ref, u2s_ref, tm):
    step = pl.program_id(0)
    half = sb_ref.shape[1]

    @pl.when(step == 0)
    def _():
        r_id = lax.broadcasted_iota(jnp.int32, (tm, tm), 0)
        c_id = lax.broadcasted_iota(jnp.int32, (tm, tm), 1)
        before_ref[...] = (r_id < c_id).astype(BF16)
        tc_ref[...] = jnp.zeros_like(tc_ref)
        u2s_ref[...] = jnp.zeros_like(u2s_ref)

    u2 = u2s_ref[...]
    u_hi = u2.astype(BF16)
    u_lo = (u2 - u_hi.astype(F32)).astype(BF16)
    nt_dims = (((1,), (1,)), ((), ()))
    n_rows = rb_ref.shape[0]
    both = lax.dot_general(wrs_ref[...], u_hi, nt_dims, preferred_element_type=F32)
    lt = (both[0:n_rows] + both[n_rows:2 * n_rows]
          + lax.dot_general(wrs_ref[0:n_rows, :], u_lo, nt_dims, preferred_element_type=F32)
          + rb_ref[...])

    sub = lax.broadcasted_iota(jnp.int32, (SUBLANES, tm), 0)

    def top1(x):
        m = jnp.max(x, axis=0, keepdims=True)
        idx = jnp.min(jnp.where(x == m, sub, SUBLANES), axis=0, keepdims=True)
        return m, idx

    grp = lt[0:SUBLANES, :]
    g_max, g_idx = top1(grp)
    g_p = 1.0 / jnp.sum(jnp.exp(grp - g_max), axis=0, keepdims=True)
    fine = lt[SUBLANES:2 * SUBLANES, :]
    for g in range(1, N_GROUPS):
        fine = jnp.where(g_idx == g, lt[(g + 1) * SUBLANES:(g + 2) * SUBLANES, :], fine)
    m1, i1 = top1(fine)
    m2, i2 = top1(jnp.where(sub == i1, -jnp.inf, fine))
    e2 = jnp.exp(m2 - m1)
    p1 = 1.0 / (1.0 + e2)
    w1 = g_p * p1
    w2 = g_p * (e2 * p1)
    x1 = g_idx * EXPERTS_PER_GROUP + i1
    x2 = g_idx * EXPERTS_PER_GROUP + i2

    eid = lax.broadcasted_iota(jnp.int32, (N_EXPERTS, tm), 0)
    oh1 = jnp.where(eid == x1, 1.0, 0.0)
    oh2 = jnp.where(eid == x2, 1.0, 0.0)
    pre1 = jnp.dot(oh1.astype(BF16), before_ref[...], preferred_element_type=F32)
    pre2 = jnp.dot(oh2.astype(BF16), before_ref[...], preferred_element_type=F32)
    cnt1 = jnp.sum(oh1, axis=1, keepdims=True)
    cnt2 = jnp.sum(oh2, axis=1, keepdims=True)
    seg8 = jnp.floor((cnt1 + cnt2 + (SUBLANES - 1.0)) * (1.0 / SUBLANES))
    e_r = lax.broadcasted_iota(jnp.int32, (N_EXPERTS, N_EXPERTS), 0)
    e_c = lax.broadcasted_iota(jnp.int32, (N_EXPERTS, N_EXPERTS), 1)
    lower = jnp.where(e_c < e_r, 1.0, 0.0).astype(BF16)
    seg8_b = jnp.broadcast_to(seg8, (N_EXPERTS, LANES)).astype(BF16)
    seg_off = SUBLANES * jnp.dot(lower, seg8_b, preferred_element_type=F32)[:, 0:1]
    pos1 = jnp.sum(oh1 * (pre1 + seg_off), axis=0, keepdims=True)
    pos2 = jnp.sum(oh2 * (pre2 + (seg_off + cnt1)), axis=0, keepdims=True)

    lane = lax.broadcasted_iota(jnp.int32, tc_ref.shape, 1)
    seg_rows = jnp.broadcast_to(seg8 * SUBLANES, tc_ref.shape).astype(jnp.int32)
    tc_ref[...] = jnp.where(lane == step - 1, seg_rows, tc_ref[...])

    zrow = jnp.zeros((SUBLANES - 4, tm), jnp.int32)
    rr_ref[...] = jnp.concatenate(
        [pos1.astype(jnp.int32), pos2.astype(jnp.int32), x1, x2, zrow], axis=0)
    wt = jnp.concatenate([w1, w2, pos1, pos2, jnp.zeros((LANES - 4, tm), F32)], axis=0)
    rwt_ref[...] = wt.T

    m_sb = _rms_f32(sb_ref[...], sbg_ref[...]).astype(BF16)
    lru = jnp.concatenate([r[...] for r in lru_refs], axis=-1)
    m_lru = _rms_f32(lru, lrug_ref[...]).astype(BF16)
    h = (x_ref[...]
         + jnp.dot(m_sb, wo_ref[0:half, :], preferred_element_type=F32)
         + jnp.dot(m_lru, wo_ref[half:2 * half, :], preferred_element_type=F32))
    h_ref[...] = h
    u2_next = _rms_f32(h, n2g_ref[...])
    u2_ref[...] = u2_next.astype(BF16)
    u2s_ref[...] = u2_next


def _mix_route(sb, lru, x2, sbg, lrug, w_out, n2g, wr_stack, rbias, tm):
    n, d = x2.shape
    half = sb.shape[1]
    n_tiles = n // tm
    row = lambda i: (jnp.minimum(i, n_tiles - 1), 0)
    routed = lambda i: (jnp.maximum(i - 1, 0), 0)
    const = lambda i: (0, 0)
    return pl.pallas_call(
        functools.partial(_mix_route_kernel, tm=tm, n_slab=len(lru)),
        grid=(n_tiles + 1,),
        in_specs=[pl.BlockSpec((tm, half), row)] + [pl.BlockSpec((tm, LANES), row)] * len(lru)
                 + [pl.BlockSpec((tm, d), row),
                  pl.BlockSpec((1, half), const), pl.BlockSpec((1, half), const),
                  pl.BlockSpec(w_out.shape, const, pipeline_mode=pl.Buffered(1)),
                  pl.BlockSpec((1, d), const),
                  pl.BlockSpec(wr_stack.shape, const), pl.BlockSpec(rbias.shape, const)],
        out_specs=[pl.BlockSpec((tm, d), row), pl.BlockSpec((tm, d), row),
                   pl.BlockSpec((SUBLANES, tm), lambda i: (0, jnp.maximum(i - 1, 0))),
                   pl.BlockSpec((tm, LANES), routed),
                   pl.BlockSpec((N_EXPERTS, LANES), const)],
        out_shape=[jax.ShapeDtypeStruct((n, d), F32), jax.ShapeDtypeStruct((n, d), BF16),
                   jax.ShapeDtypeStruct((SUBLANES, n), jnp.int32),
                   jax.ShapeDtypeStruct((n, LANES), F32),
                   jax.ShapeDtypeStruct((N_EXPERTS, LANES), jnp.int32)],
        scratch_shapes=[pltpu.VMEM((tm, tm), BF16), pltpu.VMEM((tm, d), F32),
                        pltpu.VMEM(w_out.shape, BF16)],
        compiler_params=_cparams(("arbitrary",)),
        name="mix_route",
    )(sb, *lru, x2, sbg, lrug, w_out, n2g, wr_stack, rbias)


def _pack_halves(x):
    half = x.shape[1] // 2
    lo = lax.shift_right_logical(lax.bitcast_convert_type(x[:, :half], jnp.int32), 16)
    hi = lax.bitcast_convert_type(x[:, half:], jnp.int32) & HIGH_HALF
    return hi | lo


def _unpack_halves(p):
    lo = lax.bitcast_convert_type(lax.shift_left(p, 16), F32)
    hi = lax.bitcast_convert_type(p & HIGH_HALF, F32)
    return lo.astype(BF16), hi.astype(BF16)


def _segment_copies(tile, c8_ref, loff_ref, goff_ref, make):
    for e in range(N_EXPERTS):
        idx = tile * N_EXPERTS + e
        rows = pl.multiple_of(c8_ref[idx], SUBLANES)

        @pl.when(rows > 0)
        def _(idx=idx, rows=rows):
            lo = pl.multiple_of(loff_ref[idx], SUBLANES)
            go = pl.multiple_of(goff_ref[idx], SUBLANES)
            make(pl.ds(lo, rows), pl.ds(go, rows)).start()


def _dispatch_kernel(c8_ref, loff_ref, goff_ref, tot_ref, used_ref,
                     rr_ref, u2_ref, xs_ref, lbuf, zbuf, sem, zsem, *, td, lrows):
    i = pl.program_id(0)
    slot = i % 2

    r_id = lax.broadcasted_iota(jnp.int32, (lrows, td), 0)
    perm = jnp.where(r_id == rr_ref[0:1, :], 1.0, jnp.where(r_id == rr_ref[1:2, :], 1.0, 0.0))
    sorted_rows = jnp.dot(perm.astype(BF16), u2_ref[...], preferred_element_type=F32)
    lbuf[slot] = _pack_halves(sorted_rows)

    @pl.when(i == 0)
    def _():
        zbuf[...] = jnp.zeros_like(zbuf)
        chunk = zbuf.shape[0]
        used = used_ref[0]
        spare = xs_ref.shape[0] - used
        n_fill = (spare + chunk - 1) // chunk

        def fill_copy(k):
            rows = pl.multiple_of(jnp.minimum(chunk, spare - k * chunk), SUBLANES)
            start = pl.multiple_of(used + k * chunk, SUBLANES)
            return pltpu.make_async_copy(zbuf.at[pl.ds(0, rows)], xs_ref.at[pl.ds(start, rows)],
                                         zsem)

        def fill_start(k, c):
            fill_copy(k).start()
            return c

        def fill_wait(k, c):
            fill_copy(k).wait()
            return c

        lax.fori_loop(0, n_fill, fill_start, 0)
        lax.fori_loop(0, n_fill, fill_wait, 0)

    def wait_tile(tile, s):
        rows = pl.multiple_of(tot_ref[tile], SUBLANES)
        pltpu.make_async_copy(lbuf.at[s, pl.ds(0, rows)], xs_ref.at[pl.ds(0, rows)], sem).wait()

    @pl.when(i > 0)
    def _():
        wait_tile(i - 1, 1 - slot)

    _segment_copies(i, c8_ref, loff_ref, goff_ref,
                    lambda loc, glob: pltpu.make_async_copy(lbuf.at[slot, loc], xs_ref.at[glob], sem))

    @pl.when(i == pl.num_programs(0) - 1)
    def _():
        wait_tile(i, slot)


def _dispatch(c8, loff, goff, tot, used, rr, u2, p_rows, td, lrows, tme):
    n, d = u2.shape
    pmap = lambda i, *_: (0, i)
    return pl.pallas_call(
        functools.partial(_dispatch_kernel, td=td, lrows=lrows),
        grid_spec=pltpu.PrefetchScalarGridSpec(
            num_scalar_prefetch=5,
            grid=(n // td,),
            in_specs=[pl.BlockSpec((SUBLANES, td), pmap),
                      pl.BlockSpec((td, d), lambda i, *_: (i, 0))],
            out_specs=pl.BlockSpec(memory_space=pl.ANY),
            scratch_shapes=[pltpu.VMEM((2, lrows, d // 2), jnp.int32),
                            pltpu.VMEM((tme, d // 2), jnp.int32),
                            pltpu.SemaphoreType.DMA(()), pltpu.SemaphoreType.DMA(())]),
        out_shape=jax.ShapeDtypeStruct((p_rows, d // 2), jnp.int32),
        compiler_params=_cparams(("arbitrary",)),
        name="dispatch",
    )(c8, loff, goff, tot, used, rr, u2)


def _experts_kernel(eoff_ref, erows_ref, xs_ref, wg_ref, wu_ref, wd_ref, ys_ref,
                    wg_bf, wu_bf, wd_bf, xbuf, ybuf, sem_in, sem_out, state, *, tme):
    e = pl.program_id(0)
    n_experts = pl.num_programs(0)
    rows = erows_ref[e]
    off = eoff_ref[e]
    n_tiles = (rows + tme - 1) // tme
    nxt = jnp.minimum(e + 1, n_experts - 1)
    prefetch = (rows > 0) & (e + 1 < n_experts) & (erows_ref[nxt] > 0)

    def tile_rows(total, k):
        return pl.multiple_of(jnp.minimum(tme, total - k * tme), SUBLANES)

    def in_copy(start, r, slot):
        start = pl.multiple_of(start, SUBLANES)
        return pltpu.make_async_copy(xs_ref.at[pl.ds(start, r)], xbuf.at[slot, pl.ds(0, r)],
                                     sem_in.at[slot])

    def out_copy(start, r, slot):
        start = pl.multiple_of(start, SUBLANES)
        return pltpu.make_async_copy(ybuf.at[slot, pl.ds(0, r)], ys_ref.at[pl.ds(start, r)],
                                     sem_out.at[slot])

    @pl.when(e == 0)
    def _():
        for s in range(4):
            state[s] = 0
        xbuf[...] = jnp.zeros_like(xbuf)

    @pl.when(rows > 0)
    def _():
        wg_bf[...] = wg_ref[0].astype(BF16)
        wu_bf[...] = wu_ref[0].astype(BF16)
        wd_bf[...] = wd_ref[0].astype(BF16)
        done = state[0]

        @pl.when(state[1] == 0)
        def _():
            in_copy(off, tile_rows(rows, 0), done % 2).start()

        def tile(k, c):
            slot = (done + k) % 2
            r = tile_rows(rows, k)

            @pl.when(k + 1 < n_tiles)
            def _():
                in_copy(off + (k + 1) * tme, tile_rows(rows, k + 1), 1 - slot).start()

            @pl.when((k + 1 == n_tiles) & prefetch)
            def _():
                in_copy(eoff_ref[nxt], tile_rows(erows_ref[nxt], 0), 1 - slot).start()

            in_copy(off, r, slot).wait()

            @pl.when(done + k >= 2)
            def _():
                out_copy(0, pl.multiple_of(state[2 + slot], SUBLANES), slot).wait()

            def mlp(n):
                x = jnp.concatenate(_unpack_halves(xbuf[slot, 0:n]), axis=1)
                hg = jnp.dot(x, wg_bf[...], preferred_element_type=F32)
                hu = jnp.dot(x, wu_bf[...], preferred_element_type=F32)
                act = (hg * jax.nn.sigmoid(hg) * hu).astype(BF16)
                y = jnp.dot(act, wd_bf[...], preferred_element_type=F32)
                ybuf[slot, 0:n] = _pack_halves(y.astype(BF16).astype(F32))

            @pl.when(r > tme // 2)
            def _():
                mlp(tme)

            @pl.when(r <= tme // 2)
            def _():
                mlp(tme // 2)

            out_copy(off + k * tme, r, slot).start()
            state[2 + slot] = r
            return c

        lax.fori_loop(0, n_tiles, tile, 0)
        state[0] = done + n_tiles

    state[1] = prefetch.astype(jnp.int32)

    @pl.when(e == n_experts - 1)
    def _():
        total = eoff_ref[e] + rows
        for slot in range(2):
            @pl.when(state[0] > slot)
            def _(slot=slot):
                out_copy(0, pl.multiple_of(state[2 + slot], SUBLANES), slot).wait()
        ybuf[0] = jnp.zeros(ybuf.shape[1:], ybuf.dtype)
        spare = ys_ref.shape[0] - total
        n_fill = (spare + tme - 1) // tme

        def fill_copy(k):
            return out_copy(total + k * tme, tile_rows(spare, k), 0)

        def fill_start(k, c):
            fill_copy(k).start()
            return c

        def fill_wait(k, c):
            fill_copy(k).wait()
            return c

        lax.fori_loop(0, n_fill, fill_start, 0)
        lax.fori_loop(0, n_fill, fill_wait, 0)


def _experts(eoff, erows, xs, wg, wu, wd, tme):
    p = xs.shape[0]
    n_experts, d, de = wg.shape
    wmap = lambda e, *_: (e, 0, 0)
    return pl.pallas_call(
        functools.partial(_experts_kernel, tme=tme),
        grid_spec=pltpu.PrefetchScalarGridSpec(
            num_scalar_prefetch=2,
            grid=(n_experts,),
            in_specs=[pl.BlockSpec(memory_space=pl.ANY),
                      pl.BlockSpec((1, d, de), wmap), pl.BlockSpec((1, d, de), wmap),
                      pl.BlockSpec((1, de, d), wmap)],
            out_specs=pl.BlockSpec(memory_space=pl.ANY),
            scratch_shapes=[pltpu.VMEM((d, de), BF16), pltpu.VMEM((d, de), BF16),
                            pltpu.VMEM((de, d), BF16),
                            pltpu.VMEM((2, tme, d // 2), jnp.int32),
                            pltpu.VMEM((2, tme, d // 2), jnp.int32),
                            pltpu.SemaphoreType.DMA((2,)), pltpu.SemaphoreType.DMA((2,)),
                            pltpu.SMEM((4,), jnp.int32)]),
        out_shape=jax.ShapeDtypeStruct((p, d // 2), jnp.int32),
        compiler_params=_cparams(("arbitrary",)),
        name="experts",
    )(eoff, erows, xs, wg, wu, wd)


def _combine_kernel(c8_ref, loff_ref, goff_ref, tot_ref,
                    rwt_ref, h_ref, fg_ref, ys_ref, y_ref, ybuf, sems, *, tc, lrows):
    i = pl.program_id(0)
    slot = i % 2

    def gather_tile(tile, s):
        _segment_copies(tile, c8_ref, loff_ref, goff_ref,
                        lambda loc, glob: pltpu.make_async_copy(ys_ref.at[glob], ybuf.at[s, loc],
                                                                sems.at[s]))

    @pl.when(i == 0)
    def _():
        ybuf[...] = jnp.zeros_like(ybuf)
        gather_tile(0, 0)

    @pl.when(i + 1 < pl.num_programs(0))
    def _():
        gather_tile(i + 1, 1 - slot)

    rows = pl.multiple_of(tot_ref[i], SUBLANES)
    pltpu.make_async_copy(ys_ref.at[pl.ds(0, rows)], ybuf.at[slot, pl.ds(0, rows)],
                          sems.at[slot]).wait()

    w = rwt_ref[...]
    c_id = lax.broadcasted_iota(jnp.int32, (tc, lrows), 1)
    pos1 = w[:, 2:3].astype(jnp.int32)
    pos2 = w[:, 3:4].astype(jnp.int32)
    wmat = (jnp.where(c_id == pos1, w[:, 0:1], 0.0)
            + jnp.where(c_id == pos2, w[:, 1:2], 0.0)).astype(BF16)
    y_lo, y_hi = _unpack_halves(ybuf[slot])
    moe = jnp.concatenate([jnp.dot(wmat, y_lo, preferred_element_type=F32),
                           jnp.dot(wmat, y_hi, preferred_element_type=F32)], axis=-1)
    y_ref[...] = _rms_f32(h_ref[...] + moe, fg_ref[...])


def _combine(c8, loff, goff, tot, rwt, h, final_g, ys, tc, lrows):
    n, d = h.shape
    return pl.pallas_call(
        functools.partial(_combine_kernel, tc=tc, lrows=lrows),
        grid_spec=pltpu.PrefetchScalarGridSpec(
            num_scalar_prefetch=4,
            grid=(n // tc,),
            in_specs=[pl.BlockSpec((tc, LANES), lambda i, *_: (i, 0)),
                      pl.BlockSpec((tc, d), lambda i, *_: (i, 0)),
                      pl.BlockSpec((1, d), lambda i, *_: (0, 0)),
                      pl.BlockSpec(memory_space=pl.ANY)],
            out_specs=pl.BlockSpec((tc, d), lambda i, *_: (i, 0)),
            scratch_shapes=[pltpu.VMEM((2, lrows, d // 2), jnp.int32),
                            pltpu.SemaphoreType.DMA((2,))]),
        out_shape=jax.ShapeDtypeStruct((n, d), F32),
        compiler_params=_cparams(("arbitrary",)),
        name="combine",
    )(c8, loff, goff, tot, rwt, h, final_g, ys)


def _block_diag(w, per):
    nb, c, _ = w.shape
    eye = jnp.eye(per, dtype=w.dtype)
    wg = w.reshape(nb // per, per, c, c)
    return jnp.einsum("gpij,pq->gpiqj", wg, eye).reshape(nb // per, per * c, per * c)


def _router_tables(w_group, b_group, w_fine, b_fine):
    d = w_group.shape[0]
    w = jnp.zeros((ROUTER_ROWS, d), F32)
    w = w.at[0:N_GROUPS].set(w_group.T).at[SUBLANES:SUBLANES + N_EXPERTS].set(w_fine.T)
    b = jnp.full((ROUTER_ROWS,), NEG_BIG, F32)
    b = b.at[0:N_GROUPS].set(b_group).at[SUBLANES:SUBLANES + N_EXPERTS].set(b_fine)
    w_hi = w.astype(BF16)
    w_lo = (w - w_hi.astype(F32)).astype(BF16)
    return jnp.concatenate([w_hi, w_lo], axis=0), b.reshape(ROUTER_ROWS, 1)


def kernel(x, norm1_g, w_in, conv_w, conv_b, w_rgate, b_rgate, w_igate, b_igate, lam, sb_norm_g,
           lru_norm_g, w_out, norm2_g, w_group, b_group, w_fine, b_fine, w_e_gate, w_e_up,
           w_e_down, final_g):
    batch, seq, d = x.shape
    n = batch * seq
    width = w_in.shape[1] // 5
    tm = min(512, seq)
    tme = 512
    gate_per = LANES // w_rgate.shape[1]

    x2 = x.reshape(n, d)
    vec = lambda a: a.reshape(1, -1)

    q, k, v, xl, gl = _in_proj(x2, vec(norm1_g), w_in, width, min(2 * tm, seq))
    n_pipe = seq // LANES - (ATTN_WINDOW_BLOCKS + 1)
    unroll = ATTN_UNROLL if n_pipe % ATTN_UNROLL == 0 else 2
    out_sb = _attention(q, k, v, batch, seq, LANES, ATTN_WINDOW_BLOCKS, unroll)
    out_lru = _lru(xl, gl, conv_w, vec(conv_b),
                   _block_diag(w_rgate, gate_per).astype(BF16), vec(b_rgate),
                   _block_diag(w_igate, gate_per).astype(BF16), vec(b_igate),
                   vec(lam), batch, seq, min(1024, seq))

    wr_stack, rbias = _router_tables(w_group, b_group, w_fine, b_fine)
    h, u2, rr, rwt, tcnt = _mix_route(out_sb, out_lru, x2, vec(sb_norm_g), vec(lru_norm_g),
                                      w_out, vec(norm2_g), wr_stack, rbias, tm)

    n_tiles = n // tm
    assert n_tiles <= LANES, "one lane of the per-tile count table per token tile"
    c8 = tcnt[:, :n_tiles].T
    erows = jnp.sum(c8, axis=0)
    eoff = jnp.cumsum(erows) - erows
    goff = eoff[None, :] + jnp.cumsum(c8, axis=0) - c8
    loff = jnp.cumsum(c8, axis=1) - c8
    tot = jnp.sum(c8, axis=1)
    lrows = 2 * tm + N_EXPERTS * SUBLANES
    p_rows = 2 * n + n_tiles * N_EXPERTS * (SUBLANES - 1)
    p_rows = -(-p_rows // SUBLANES) * SUBLANES
    i32 = lambda a: a.reshape(-1).astype(jnp.int32)
    c8, loff, goff, tot, eoff, erows = (i32(a) for a in (c8, loff, goff, tot, eoff, erows))

    xs = _dispatch(c8, loff, goff, tot, jnp.sum(erows, keepdims=True), rr, u2, p_rows, tm, lrows, tme)
    ys = _experts(eoff, erows, xs, w_e_gate, w_e_up, w_e_down, tme)
    y = _combine(c8, loff, goff, tot, rwt, h, vec(final_g), ys, tm, lrows)
    return y.reshape(batch, seq, d)
```

```python
import functools
import math

import jax
import jax.numpy as jnp
from jax import lax
from jax.experimental import pallas as pl
from jax.experimental.pallas import tpu as pltpu

F32 = jnp.float32
BF16 = jnp.bfloat16

EPS = 1e-6
HEAD_DIM = 64
HEADS_PER_BLOCK = 2
LANES = 128
SUBLANES = 8
CONV_W = 4
RG_C = 8.0
N_GROUPS = 4
EXPERTS_PER_GROUP = 8
N_EXPERTS = N_GROUPS * EXPERTS_PER_GROUP
ROUTER_ROWS = 48
NEG_BIG = -1e30
LOG2_E = math.log2(math.e)
ATTN_STOP = 104.0 * LOG2_E
ATTN_WINDOW_BLOCKS = 3
ATTN_UNROLL = 12
HIGH_HALF = -65536

VMEM_LIMIT = 56 * 1024 * 1024


def _cparams(sem):
    return pltpu.CompilerParams(dimension_semantics=sem, vmem_limit_bytes=VMEM_LIMIT)


def _rms_f32(x, g):
    return x * lax.rsqrt(jnp.mean(x * x, axis=-1, keepdims=True) + EPS) * g


def _in_proj_kernel(x_ref, g_ref, w_ref, q_ref, k_ref, v_ref, xl_ref, gl_ref, w_bf, *,
                    width, q_scale):
    @pl.when(pl.program_id(0) == 0)
    def _():
        for c in range(w_ref.shape[1] // width):
            cols = slice(c * width, (c + 1) * width)
            w_bf[:, cols] = w_ref[:, cols].astype(BF16)

    u = _rms_f32(x_ref[...], g_ref[...]).astype(BF16)
    for c, o_ref in enumerate((q_ref, k_ref, v_ref, xl_ref, gl_ref)):
        p = jnp.dot(u, w_bf[:, c * width:(c + 1) * width], preferred_element_type=F32)
        if c == 0:
            p = p * q_scale
        o_ref[...] = p.astype(o_ref.dtype)


def _in_proj(x2, g, w, width, tm):
    n, d = x2.shape
    row = lambda i: (i, 0)
    out_bf = jax.ShapeDtypeStruct((n, width), BF16)
    out_f = jax.ShapeDtypeStruct((n, width), F32)
    return pl.pallas_call(
        functools.partial(_in_proj_kernel, width=width, q_scale=1.0 / math.sqrt(HEAD_DIM)),
        grid=(n // tm,),
        in_specs=[pl.BlockSpec((tm, d), row),
                  pl.BlockSpec((1, d), lambda i: (0, 0)),
                  pl.BlockSpec(w.shape, lambda i: (0, 0), pipeline_mode=pl.Buffered(1))],
        out_specs=[pl.BlockSpec((tm, width), row)] * 5,
        out_shape=[out_bf, out_bf, out_bf, out_f, out_f],
        scratch_shapes=[pltpu.VMEM(w.shape, BF16)],
        compiler_params=_cparams(("arbitrary",)),
        name="in_proj",
    )(x2, g, w)


def _attn_kernel(q_ref, k_ref, v_ref, o_ref, tri_ref, z_ref, arg_ref, ctot_ref, acc_ref, carry_ref,
                 *, tq, nwin):
    seq = q_ref.shape[0]
    win = nwin * tq
    lane = lax.broadcasted_iota(jnp.int32, (1, LANES), 1)
    r_id = lax.broadcasted_iota(jnp.int32, (tq, tq), 0)
    c_id = lax.broadcasted_iota(jnp.int32, (tq, tq), 1)
    causal = c_id < r_id
    rel = c_id - r_id

    k_r = lax.broadcasted_iota(jnp.int32, (tq, 2 * tq), 0)
    k_c = lax.broadcasted_iota(jnp.int32, (tq, 2 * tq), 1)
    tri_ref[...] = jnp.where(k_c >= tq, 1.0, jnp.where(k_r > k_c, 1.0, 0.0)).astype(BF16)

    def softplus2(z):
        return jnp.maximum(z, 0.0) + jnp.log2(1.0 + jnp.exp2(-jnp.abs(z)))

    def scores(qh, keys):
        z = LOG2_E * lax.dot_general(qh, keys, (((1,), (1,)), ((), ())),
                                     preferred_element_type=F32)
        nlog_nb = softplus2(z)
        return nlog_nb, z - nlog_nb

    def suffix(nlog_nb):
        r = jnp.dot(nlog_nb.astype(BF16), tri_ref[...], preferred_element_type=F32)
        return r[:, :tq], r[:, tq:]

    def window(qh, wstart, masks):
        nlog_nb, log_b = scores(qh, k_ref[pl.ds(wstart, win), :])
        carry = None
        parts = [None] * nwin
        for b in reversed(range(nwin)):
            nl = nlog_nb[:, b * tq:(b + 1) * tq]
            if masks[b] is not None:
                nl = jnp.where(masks[b], nl, 0.0)
            excl, tot = suffix(nl)
            arg = log_b[:, b * tq:(b + 1) * tq] - excl
            a = jnp.exp2(arg if carry is None else arg - carry)
            if masks[b] is not None:
                a = jnp.where(masks[b], a, 0.0)
            parts[b] = a.astype(BF16)
            carry = tot if carry is None else carry + tot
        out = jnp.dot(jnp.concatenate(parts, axis=1), v_ref[pl.ds(wstart, win), :],
                      preferred_element_type=F32)
        return out, carry

    def head_queries(i):
        q = q_ref[pl.ds(i * tq, tq), :]
        return [jnp.where((lane >= h * HEAD_DIM) & (lane < (h + 1) * HEAD_DIM), q, jnp.zeros_like(q))
                for h in range(HEADS_PER_BLOCK)]

    def store(i, outs):
        o_ref[pl.ds(i * tq, tq), :] = jnp.where(lane < HEAD_DIM, outs[0], outs[1])

    for i in range(nwin - 1):
        masks = [(rel + b * tq) < i * tq for b in range(nwin)]
        store(i, [window(qh, 0, masks)[0] for qh in head_queries(i)])

    def window_start(i):
        return pl.multiple_of((i - (nwin - 1)) * tq, tq)

    def stage_scores(i, p):
        keys = k_ref[pl.ds(window_start(i), win), :]
        for h, qh in enumerate(head_queries(i)):
            z_ref[p, h] = LOG2_E * lax.dot_general(qh, keys, (((1,), (1,)), ((), ())),
                                                   preferred_element_type=F32)

    def stage_exponents(p):
        carries = []
        for h in range(HEADS_PER_BLOCK):
            z = z_ref[p, h]
            softplus = softplus2(z)
            carry = None
            for b in reversed(range(nwin)):
                diag = b == nwin - 1
                cols = slice(b * tq, (b + 1) * tq)
                nl = softplus[:, cols]
                if diag:
                    nl = jnp.where(causal, nl, 0.0)
                excl, tot = suffix(nl)
                arg = z[:, cols] - softplus[:, cols] - excl
                if carry is not None:
                    arg = arg - carry
                if diag:
                    arg = jnp.where(causal, arg, NEG_BIG)
                arg_ref[p, h, :, cols] = arg
                carry = tot if carry is None else carry + tot
            ctot_ref[p, h] = carry
            carries.append(carry)
        return jnp.min(jnp.minimum(carries[0], carries[1]))

    def stage_output(i, p, s):
        vals = v_ref[pl.ds(window_start(i), win), :]
        for h in range(HEADS_PER_BLOCK):
            a = jnp.exp2(arg_ref[p, h]).astype(BF16)
            acc_ref[s, h] = jnp.dot(a, vals, preferred_element_type=F32)
            carry_ref[s, h] = ctot_ref[p, h]

    def finish(i, s, cmin):
        def cond(state):
            j, cmin = state
            return (j >= 0) & (cmin <= ATTN_STOP)

        def older(state):
            j, _ = state
            start = pl.multiple_of(j * tq, tq)
            keys = k_ref[pl.ds(start, tq), :]
            vals = v_ref[pl.ds(start, tq), :]
            cs = []
            for h, qh in enumerate(head_queries(i)):
                nlog_nb, log_b = scores(qh, keys)
                excl, tot = suffix(nlog_nb)
                carry = carry_ref[s, h]
                a = jnp.exp2(log_b - excl - carry)
                acc_ref[s, h] += jnp.dot(a.astype(BF16), vals, preferred_element_type=F32)
                carry_ref[s, h] = carry + tot
                cs.append(carry + tot)
            return j - 1, jnp.min(jnp.minimum(cs[0], cs[1]))

        lax.while_loop(cond, older, (i - nwin, cmin))
        store(i, [acc_ref[s, 0], acc_ref[s, 1]])

    first = nwin - 1
    n_blocks = seq // tq
    unroll = acc_ref.shape[0]
    slot = lambda i: (first + i) % 2
    stage_scores(first, slot(0))
    stage_scores(first + 1, slot(1))
    cmin = stage_exponents(slot(0))

    def steady(m, cmin):
        t = first + 2 + unroll * m
        cmins = [cmin]
        for u in range(unroll):
            stage_output(t - 2 + u, slot(u), u)
            cmins.append(stage_exponents(slot(u + 1)))
            stage_scores(t + u, slot(u))
        for u in range(unroll):
            finish(t - 2 + u, u, cmins[u])
        return cmins[unroll]

    cmin = lax.fori_loop(0, (n_blocks - first - 2) // unroll, steady, cmin)
    stage_output(n_blocks - 2, slot(0), 0)
    cmin_last = stage_exponents(slot(1))
    finish(n_blocks - 2, 0, cmin)
    stage_output(n_blocks - 1, slot(1), 1)
    finish(n_blocks - 1, 1, cmin_last)


def _attention(q, k, v, batch, seq, tq, nwin, unroll):
    n, width = q.shape
    assert tq == LANES and seq >= (nwin + 1) * tq
    assert unroll % 2 == 0 and (seq // tq - (nwin + 1)) % unroll == 0
    blk = pl.BlockSpec((seq, LANES), lambda b, hp: (b, hp))
    stage_buf = pltpu.VMEM((2, HEADS_PER_BLOCK, tq, nwin * tq), F32)
    carry_buf = pltpu.VMEM((2, HEADS_PER_BLOCK, tq, tq), F32)
    row_buf = pltpu.VMEM((unroll, HEADS_PER_BLOCK, tq, LANES), F32)
    return pl.pallas_call(
        functools.partial(_attn_kernel, tq=tq, nwin=nwin),
        grid=(batch, width // LANES),
        in_specs=[blk, blk, blk],
        out_specs=blk,
        out_shape=jax.ShapeDtypeStruct((n, width), F32),
        scratch_shapes=[pltpu.VMEM((tq, 2 * tq), BF16),
                        stage_buf, stage_buf, carry_buf, row_buf, row_buf],
        compiler_params=_cparams(("arbitrary", "arbitrary")),
        name="attn",
    )(q, k, v)


def _gelu_tanh(x):
    c = math.sqrt(2.0 / math.pi)
    half_x = 0.5 * x
    return half_x + half_x * jnp.tanh(x * (c + (c * 0.044715) * (x * x)))


def _sigmoid(x):
    return 0.5 + 0.5 * jnp.tanh(0.5 * x)


def _lru_kernel(*refs, ts, n_slab):
    xl_refs, gl_refs = refs[0:n_slab], refs[n_slab:2 * n_slab]
    cw_ref, cb_ref, wr_ref, br_ref, wi_ref, bi_ref, lam_ref = refs[2 * n_slab:2 * n_slab + 7]
    o_refs = refs[2 * n_slab + 7:3 * n_slab + 7]
    tail_ref, a7_ref, u7_ref, hp_ref, pa_ref, pu_ref, h_ref = refs[3 * n_slab + 7:]
    t = pl.program_id(1)
    groups = ts // SUBLANES

    @pl.when(t == 0)
    def _():
        tail_ref[...] = jnp.zeros_like(tail_ref)
        h_ref[...] = jnp.zeros_like(h_ref)

    first_group = lax.broadcasted_iota(jnp.int32, (groups, LANES), 0) == 0
    for c in range(n_slab):
        lanes = slice(c * LANES, (c + 1) * LANES)
        x = [xl_refs[c][pl.ds(s, groups, stride=SUBLANES), :] for s in range(SUBLANES)]
        shifted = {}
        for s in range(SUBLANES - (CONV_W - 1), SUBLANES):
            shifted[s] = jnp.where(first_group, tail_ref[c, s:s + 1, :], pltpu.roll(x[s], 1, axis=0))
            tail_ref[c, s:s + 1, :] = x[s][groups - 1:groups, :]
        conv = []
        for s in range(SUBLANES):
            y = cb_ref[:, lanes]
            for w in range(CONV_W):
                j = s - (CONV_W - 1) + w
                y = y + (x[j] if j >= 0 else shifted[j + SUBLANES]) * cw_ref[w:w + 1, lanes]
            conv.append(y)
        xc = jnp.concatenate(conv, axis=0)

        xcb = xc.astype(BF16)
        r = _sigmoid(jnp.dot(xcb, wr_ref[c], preferred_element_type=F32) + br_ref[:, lanes])
        ig = _sigmoid(jnp.dot(xcb, wi_ref[c], preferred_element_type=F32) + bi_ref[:, lanes])
        lam = lam_ref[:, lanes]
        log_sig_lam = -(jnp.maximum(-lam, 0.0) + jnp.log1p(jnp.exp(-jnp.abs(lam))))
        log_a = RG_C * r * log_sig_lam
        a = jnp.exp(log_a)
        th = jnp.tanh(log_a)
        one_m_a2 = -2.0 * th / (1.0 - th)
        root = jnp.where(one_m_a2 > 0.0, one_m_a2 * lax.rsqrt(one_m_a2), 0.0)
        u = root * (ig * xc)

        a_run = u_run = None
        for s in range(SUBLANES):
            rows = slice(s * groups, (s + 1) * groups)
            if s == 0:
                a_run, u_run = a[rows], u[rows]
            else:
                u_run = a[rows] * u_run + u[rows]
                a_run = a_run * a[rows]
            pa_ref[c, rows, :] = a_run
            pu_ref[c, rows, :] = u_run
        a7_ref[c] = a_run
        u7_ref[c] = u_run

    def group(g, hs):
        nxt = []
        for c in range(n_slab):
            hp_ref[c, pl.ds(g, 1), :] = hs[c]
            nxt.append(a7_ref[c, pl.ds(g, 1), :] * hs[c] + u7_ref[c, pl.ds(g, 1), :])
        return tuple(nxt)

    hs = lax.fori_loop(0, groups, group, tuple(h_ref[c] for c in range(n_slab)), unroll=8)
    for c in range(n_slab):
        h_ref[c] = hs[c]

    for c in range(n_slab):
        h_in = hp_ref[c]
        for s in range(SUBLANES):
            rows = slice(s * groups, (s + 1) * groups)
            hseq = pu_ref[c, rows, :] + pa_ref[c, rows, :] * h_in
            gate = _gelu_tanh(gl_refs[c][pl.ds(s, groups, stride=SUBLANES), :])
            o_refs[c][pl.ds(s, groups, stride=SUBLANES), :] = hseq * gate


def _lru(xl, gl, conv_w, conv_b, wr_bd, br, wi_bd, bi, lam, batch, seq, ts):
    n, width = xl.shape
    nt = seq // ts
    n_slab = width // LANES
    assert wr_bd.shape == (n_slab, LANES, LANES) and ts % (SUBLANES * SUBLANES) == 0
    groups = ts // SUBLANES
    slab = [pl.BlockSpec((ts, LANES), functools.partial(lambda b, t, c: (b * nt + t, c), c=c))
            for c in range(n_slab)]
    const2 = lambda b, t: (0, 0)
    const3 = lambda b, t: (0, 0, 0)
    vec = pl.BlockSpec((1, width), const2)
    per_group = pltpu.VMEM((n_slab, groups, LANES), F32)
    per_step = pltpu.VMEM((n_slab, ts, LANES), F32)
    return pl.pallas_call(
        functools.partial(_lru_kernel, ts=ts, n_slab=n_slab),
        grid=(batch, nt),
        in_specs=slab + slab + [pl.BlockSpec((CONV_W, width), const2), vec,
                                pl.BlockSpec(wr_bd.shape, const3), vec,
                                pl.BlockSpec(wi_bd.shape, const3), vec, vec],
        out_specs=[pl.BlockSpec((ts, LANES), lambda b, t: (b * nt + t, 0))] * n_slab,
        out_shape=[jax.ShapeDtypeStruct((n, LANES), F32)] * n_slab,
        scratch_shapes=[pltpu.VMEM((n_slab, SUBLANES, LANES), F32),
                        per_group, per_group, per_group, per_step, per_step,
                        pltpu.VMEM((n_slab, 1, LANES), F32)],
        compiler_params=_cparams(("arbitrary", "arbitrary")),
        name="lru",
    )(*([xl] * n_slab), *([gl] * n_slab), conv_w, conv_b, wr_bd, br, wi_bd, bi, lam)


def _mix_route_kernel(sb_ref, *refs, tm, n_slab):
    lru_refs = refs[:n_slab]
    (x_ref, sbg_ref, lrug_ref, wo_ref, n2g_ref, wrs_ref, rb_ref,
     h_ref, u2_ref, rr_ref, rwt_ref, tc_ref, before_ref, u2s_ref, wo_bf) = refs[n_slab:]

    @pl.when(pl.program_id(0) == 0)
    def _():
        wo_bf[...] = wo_ref[...].astype(BF16)

    _mix_route_body(sb_ref, lru_refs, x_ref, sbg_ref, lrug_ref, wo_bf, n2g_ref, wrs_ref, rb_ref,
                    h_ref, u2_ref, rr_ref, rwt_ref, tc_ref, before_ref, u2s_ref, tm)


def _mix_route_body(sb_ref, lru_refs, x_ref, sbg_ref, lrug_ref, wo_ref, n2g_ref, wrs_ref, rb_ref,
                    h_ref, u2_ref, rr_ref, rwt_ref, tc_ref, before_ref, u2s_ref, tm):
    step = pl.program_id(0)
    half = sb_ref.shape[1]

    @pl.when(step == 0)
    def _():
        r_id = lax.broadcasted_iota(jnp.int32, (tm, tm), 0)
        c_id = lax.broadcasted_iota(jnp.int32, (tm, tm), 1)
        before_ref[...] = (r_id < c_id).astype(BF16)
        tc_ref[...] = jnp.zeros_like(tc_ref)
        u2s_ref[...] = jnp.zeros_like(u2s_ref)

    u2 = u2s_ref[...]
    u_hi = u2.astype(BF16)
    u_lo = (u2 - u_hi.astype(F32)).astype(BF16)
    nt_dims = (((1,), (1,)), ((), ()))
    n_rows = rb_ref.shape[0]
    both = lax.dot_general(wrs_ref[...], u_hi, nt_dims, preferred_element_type=F32)
    lt = (both[0:n_rows] + both[n_rows:2 * n_rows]
          + lax.dot_general(wrs_ref[0:n_rows, :], u_lo, nt_dims, preferred_element_type=F32)
          + rb_ref[...])

    sub = lax.broadcasted_iota(jnp.int32, (SUBLANES, tm), 0)

    def top1(x):
        m = jnp.max(x, axis=0, keepdims=True)
        idx = jnp.min(jnp.where(x == m, sub, SUBLANES), axis=0, keepdims=True)
        return m, idx

    grp = lt[0:SUBLANES, :]
    g_max, g_idx = top1(grp)
    g_p = 1.0 / jnp.sum(jnp.exp(grp - g_max), axis=0, keepdims=True)
    fine = lt[SUBLANES:2 * SUBLANES, :]
    for g in range(1, N_GROUPS):
        fine = jnp.where(g_idx == g, lt[(g + 1) * SUBLANES:(g + 2) * SUBLANES, :], fine)
    m1, i1 = top1(fine)
    m2, i2 = top1(jnp.where(sub == i1, -jnp.inf, fine))
    e2 = jnp.exp(m2 - m1)
    p1 = 1.0 / (1.0 + e2)
    w1 = g_p * p1
    w2 = g_p * (e2 * p1)
    x1 = g_idx * EXPERTS_PER_GROUP + i1
    x2 = g_idx * EXPERTS_PER_GROUP + i2

    eid = lax.broadcasted_iota(jnp.int32, (N_EXPERTS, tm), 0)
    oh1 = jnp.where(eid == x1, 1.0, 0.0)
    oh2 = jnp.where(eid == x2, 1.0, 0.0)
    pre1 = jnp.dot(oh1.astype(BF16), before_ref[...], preferred_element_type=F32)
    pre2 = jnp.dot(oh2.astype(BF16), before_ref[...], preferred_element_type=F32)
    cnt1 = jnp.sum(oh1, axis=1, keepdims=True)
    cnt2 = jnp.sum(oh2, axis=1, keepdims=True)
    seg8 = jnp.floor((cnt1 + cnt2 + (SUBLANES - 1.0)) * (1.0 / SUBLANES))
    e_r = lax.broadcasted_iota(jnp.int32, (N_EXPERTS, N_EXPERTS), 0)
    e_c = lax.broadcasted_iota(jnp.int32, (N_EXPERTS, N_EXPERTS), 1)
    lower = jnp.where(e_c < e_r, 1.0, 0.0).astype(BF16)
    seg8_b = jnp.broadcast_to(seg8, (N_EXPERTS, LANES)).astype(BF16)
    seg_off = SUBLANES * jnp.dot(lower, seg8_b, preferred_element_type=F32)[:, 0:1]
    pos1 = jnp.sum(oh1 * (pre1 + seg_off), axis=0, keepdims=True)
    pos2 = jnp.sum(oh2 * (pre2 + (seg_off + cnt1)), axis=0, keepdims=True)

    lane = lax.broadcasted_iota(jnp.int32, tc_ref.shape, 1)
    seg_rows = jnp.broadcast_to(seg8 * SUBLANES, tc_ref.shape).astype(jnp.int32)
    tc_ref[...] = jnp.where(lane == step - 1, seg_rows, tc_ref[...])

    zrow = jnp.zeros((SUBLANES - 4, tm), jnp.int32)
    rr_ref[...] = jnp.concatenate(
        [pos1.astype(jnp.int32), pos2.astype(jnp.int32), x1, x2, zrow], axis=0)
    wt = jnp.concatenate([w1, w2, pos1, pos2, jnp.zeros((LANES - 4, tm), F32)], axis=0)
    rwt_ref[...] = wt.T

    m_sb = _rms_f32(sb_ref[...], sbg_ref[...]).astype(BF16)
    lru = jnp.concatenate([r[...] for r in lru_refs], axis=-1)
    m_lru = _rms_f32(lru, lrug_ref[...]).astype(BF16)
    h = (x_ref[...]
         + jnp.dot(m_sb, wo_ref[0:half, :], preferred_element_type=F32)
         + jnp.dot(m_lru, wo_ref[half:2 * half, :], preferred_element_type=F32))
    h_ref[...] = h
    u2_next = _rms_f32(h, n2g_ref[...])
    u2_ref[...] = u2_next.astype(BF16)
    u2s_ref[...] = u2_next


def _mix_route(sb, lru, x2, sbg, lrug, w_out, n2g, wr_stack, rbias, tm):
    n, d = x2.shape
    half = sb.shape[1]
    n_tiles = n // tm
    row = lambda i: (jnp.minimum(i, n_tiles - 1), 0)
    routed = lambda i: (jnp.maximum(i - 1, 0), 0)
    const = lambda i: (0, 0)
    return pl.pallas_call(
        functools.partial(_mix_route_kernel, tm=tm, n_slab=len(lru)),
        grid=(n_tiles + 1,),
        in_specs=[pl.BlockSpec((tm, half), row)] + [pl.BlockSpec((tm, LANES), row)] * len(lru)
                 + [pl.BlockSpec((tm, d), row),
                  pl.BlockSpec((1, half), const), pl.BlockSpec((1, half), const),
                  pl.BlockSpec(w_out.shape, const, pipeline_mode=pl.Buffered(1)),
                  pl.BlockSpec((1, d), const),
                  pl.BlockSpec(wr_stack.shape, const), pl.BlockSpec(rbias.shape, const)],
        out_specs=[pl.BlockSpec((tm, d), row), pl.BlockSpec((tm, d), row),
                   pl.BlockSpec((SUBLANES, tm), lambda i: (0, jnp.maximum(i - 1, 0))),
                   pl.BlockSpec((tm, LANES), routed),
                   pl.BlockSpec((N_EXPERTS, LANES), const)],
        out_shape=[jax.ShapeDtypeStruct((n, d), F32), jax.ShapeDtypeStruct((n, d), BF16),
                   jax.ShapeDtypeStruct((SUBLANES, n), jnp.int32),
                   jax.ShapeDtypeStruct((n, LANES), F32),
                   jax.ShapeDtypeStruct((N_EXPERTS, LANES), jnp.int32)],
        scratch_shapes=[pltpu.VMEM((tm, tm), BF16), pltpu.VMEM((tm, d), F32),
                        pltpu.VMEM(w_out.shape, BF16)],
        compiler_params=_cparams(("arbitrary",)),
        name="mix_route",
    )(sb, *lru, x2, sbg, lrug, w_out, n2g, wr_stack, rbias)


def _pack_halves(x):
    half = x.shape[1] // 2
    lo = lax.shift_right_logical(lax.bitcast_convert_type(x[:, :half], jnp.int32), 16)
    hi = lax.bitcast_convert_type(x[:, half:], jnp.int32) & HIGH_HALF
    return hi | lo


def _unpack_halves(p):
    lo = lax.bitcast_convert_type(lax.shift_left(p, 16), F32)
    hi = lax.bitcast_convert_type(p & HIGH_HALF, F32)
    return lo.astype(BF16), hi.astype(BF16)


def _segment_copies(tile, c8_ref, loff_ref, goff_ref, make):
    for e in range(N_EXPERTS):
        idx = tile * N_EXPERTS + e
        rows = pl.multiple_of(c8_ref[idx], SUBLANES)

        @pl.when(rows > 0)
        def _(idx=idx, rows=rows):
            lo = pl.multiple_of(loff_ref[idx], SUBLANES)
            go = pl.multiple_of(goff_ref[idx], SUBLANES)
            make(pl.ds(lo, rows), pl.ds(go, rows)).start()


def _dispatch_kernel(c8_ref, loff_ref, goff_ref, tot_ref, used_ref,
                     rr_ref, u2_ref, xs_ref, lbuf, zbuf, sem, zsem, *, td, lrows):
    i = pl.program_id(0)
    slot = i % 2

    r_id = lax.broadcasted_iota(jnp.int32, (lrows, td), 0)
    perm = jnp.where(r_id == rr_ref[0:1, :], 1.0, jnp.where(r_id == rr_ref[1:2, :], 1.0, 0.0))
    sorted_rows = jnp.dot(perm.astype(BF16), u2_ref[...], preferred_element_type=F32)
    lbuf[slot] = _pack_halves(sorted_rows)

    @pl.when(i == 0)
    def _():
        zbuf[...] = jnp.zeros_like(zbuf)
        chunk = zbuf.shape[0]
        used = used_ref[0]
        spare = xs_ref.shape[0] - used
        n_fill = (spare + chunk - 1) // chunk

        def fill_copy(k):
            rows = pl.multiple_of(jnp.minimum(chunk, spare - k * chunk), SUBLANES)
            start = pl.multiple_of(used + k * chunk, SUBLANES)
            return pltpu.make_async_copy(zbuf.at[pl.ds(0, rows)], xs_ref.at[pl.ds(start, rows)],
                                         zsem)

        def fill_start(k, c):
            fill_copy(k).start()
            return c

        def fill_wait(k, c):
            fill_copy(k).wait()
            return c

        lax.fori_loop(0, n_fill, fill_start, 0)
        lax.fori_loop(0, n_fill, fill_wait, 0)

    def wait_tile(tile, s):
        rows = pl.multiple_of(tot_ref[tile], SUBLANES)
        pltpu.make_async_copy(lbuf.at[s, pl.ds(0, rows)], xs_ref.at[pl.ds(0, rows)], sem).wait()

    @pl.when(i > 0)
    def _():
        wait_tile(i - 1, 1 - slot)

    _segment_copies(i, c8_ref, loff_ref, goff_ref,
                    lambda loc, glob: pltpu.make_async_copy(lbuf.at[slot, loc], xs_ref.at[glob], sem))

    @pl.when(i == pl.num_programs(0) - 1)
    def _():
        wait_tile(i, slot)


def _dispatch(c8, loff, goff, tot, used, rr, u2, p_rows, td, lrows, tme):
    n, d = u2.shape
    pmap = lambda i, *_: (0, i)
    return pl.pallas_call(
        functools.partial(_dispatch_kernel, td=td, lrows=lrows),
        grid_spec=pltpu.PrefetchScalarGridSpec(
            num_scalar_prefetch=5,
            grid=(n // td,),
            in_specs=[pl.BlockSpec((SUBLANES, td), pmap),
                      pl.BlockSpec((td, d), lambda i, *_: (i, 0))],
            out_specs=pl.BlockSpec(memory_space=pl.ANY),
            scratch_shapes=[pltpu.VMEM((2, lrows, d // 2), jnp.int32),
                            pltpu.VMEM((tme, d // 2), jnp.int32),
                            pltpu.SemaphoreType.DMA(()), pltpu.SemaphoreType.DMA(())]),
        out_shape=jax.ShapeDtypeStruct((p_rows, d // 2), jnp.int32),
        compiler_params=_cparams(("arbitrary",)),
        name="dispatch",
    )(c8, loff, goff, tot, used, rr, u2)


def _experts_kernel(eoff_ref, erows_ref, xs_ref, wg_ref, wu_ref, wd_ref, ys_ref,
                    wg_bf, wu_bf, wd_bf, xbuf, ybuf, sem_in, sem_out, state, *, tme):
    e = pl.program_id(0)
    n_experts = pl.num_programs(0)
    rows = erows_ref[e]
    off = eoff_ref[e]
    n_tiles = (rows + tme - 1) // tme
    nxt = jnp.minimum(e + 1, n_experts - 1)
    prefetch = (rows > 0) & (e + 1 < n_experts) & (erows_ref[nxt] > 0)

    def tile_rows(total, k):
        return pl.multiple_of(jnp.minimum(tme, total - k * tme), SUBLANES)

    def in_copy(start, r, slot):
        start = pl.multiple_of(start, SUBLANES)
        return pltpu.make_async_copy(xs_ref.at[pl.ds(start, r)], xbuf.at[slot, pl.ds(0, r)],
                                     sem_in.at[slot])

    def out_copy(start, r, slot):
        start = pl.multiple_of(start, SUBLANES)
        return pltpu.make_async_copy(ybuf.at[slot, pl.ds(0, r)], ys_ref.at[pl.ds(start, r)],
                                     sem_out.at[slot])

    @pl.when(e == 0)
    def _():
        for s in range(4):
            state[s] = 0
        xbuf[...] = jnp.zeros_like(xbuf)

    @pl.when(rows > 0)
    def _():
        wg_bf[...] = wg_ref[0].astype(BF16)
        wu_bf[...] = wu_ref[0].astype(BF16)
        wd_bf[...] = wd_ref[0].astype(BF16)
        done = state[0]

        @pl.when(state[1] == 0)
        def _():
            in_copy(off, tile_rows(rows, 0), done % 2).start()

        def tile(k, c):
            slot = (done + k) % 2
            r = tile_rows(rows, k)

            @pl.when(k + 1 < n_tiles)
            def _():
                in_copy(off + (k + 1) * tme, tile_rows(rows, k + 1), 1 - slot).start()

            @pl.when((k + 1 == n_tiles) & prefetch)
            def _():
                in_copy(eoff_ref[nxt], tile_rows(erows_ref[nxt], 0), 1 - slot).start()

            in_copy(off, r, slot).wait()

            @pl.when(done + k >= 2)
            def _():
                out_copy(0, pl.multiple_of(state[2 + slot], SUBLANES), slot).wait()

            def mlp(start, n):
                x = jnp.concatenate(_unpack_halves(xbuf[slot, start:start + n]), axis=1)
                hg = jnp.dot(x, wg_bf[...], preferred_element_type=F32)
                hu = jnp.dot(x, wu_bf[...], preferred_element_type=F32)
                act = (hg * jax.nn.sigmoid(hg) * hu).astype(BF16)
                y = jnp.dot(act, wd_bf[...], preferred_element_type=F32)
                ybuf[slot, start:start + n] = _pack_halves(y.astype(BF16).astype(F32))

            q = tme // 4
            for lo, hi, pieces in ((3 * q, 4 * q, ((0, 4 * q),)),
                                   (2 * q, 3 * q, ((0, 2 * q), (2 * q, q))),
                                   (q, 2 * q, ((0, 2 * q),)),
                                   (0, q, ((0, q),))):
                @pl.when((r > lo) & (r <= hi))
                def _(pieces=pieces):
                    for start, n in pieces:
                        mlp(start, n)

            out_copy(off + k * tme, r, slot).start()
            state[2 + slot] = r
            return c

        lax.fori_loop(0, n_tiles, tile, 0)
        state[0] = done + n_tiles

    state[1] = prefetch.astype(jnp.int32)

    @pl.when(e == n_experts - 1)
    def _():
        total = eoff_ref[e] + rows
        for slot in range(2):
            @pl.when(state[0] > slot)
            def _(slot=slot):
                out_copy(0, pl.multiple_of(state[2 + slot], SUBLANES), slot).wait()
        ybuf[0] = jnp.zeros(ybuf.shape[1:], ybuf.dtype)
        spare = ys_ref.shape[0] - total
        n_fill = (spare + tme - 1) // tme

        def fill_copy(k):
            return out_copy(total + k * tme, tile_rows(spare, k), 0)

        def fill_start(k, c):
            fill_copy(k).start()
            return c

        def fill_wait(k, c):
            fill_copy(k).wait()
            return c

        lax.fori_loop(0, n_fill, fill_start, 0)
        lax.fori_loop(0, n_fill, fill_wait, 0)


def _experts(eoff, erows, xs, wg, wu, wd, tme):
    p = xs.shape[0]
    n_experts, d, de = wg.shape
    wmap = lambda e, *_: (e, 0, 0)
    return pl.pallas_call(
        functools.partial(_experts_kernel, tme=tme),
        grid_spec=pltpu.PrefetchScalarGridSpec(
            num_scalar_prefetch=2,
            grid=(n_experts,),
            in_specs=[pl.BlockSpec(memory_space=pl.ANY),
                      pl.BlockSpec((1, d, de), wmap), pl.BlockSpec((1, d, de), wmap),
                      pl.BlockSpec((1, de, d), wmap)],
            out_specs=pl.BlockSpec(memory_space=pl.ANY),
            scratch_shapes=[pltpu.VMEM((d, de), BF16), pltpu.VMEM((d, de), BF16),
                            pltpu.VMEM((de, d), BF16),
                            pltpu.VMEM((2, tme, d // 2), jnp.int32),
                            pltpu.VMEM((2, tme, d // 2), jnp.int32),
                            pltpu.SemaphoreType.DMA((2,)), pltpu.SemaphoreType.DMA((2,)),
                            pltpu.SMEM((4,), jnp.int32)]),
        out_shape=jax.ShapeDtypeStruct((p, d // 2), jnp.int32),
        compiler_params=_cparams(("arbitrary",)),
        name="experts",
    )(eoff, erows, xs, wg, wu, wd)


def _combine_kernel(c8_ref, loff_ref, goff_ref, tot_ref,
                    rwt_ref, h_ref, fg_ref, ys_ref, y_ref, ybuf, sems, *, tc, lrows):
    i = pl.program_id(0)
    slot = i % 2

    def gather_tile(tile, s):
        _segment_copies(tile, c8_ref, loff_ref, goff_ref,
                        lambda loc, glob: pltpu.make_async_copy(ys_ref.at[glob], ybuf.at[s, loc],
                                                                sems.at[s]))

    @pl.when(i == 0)
    def _():
        ybuf[...] = jnp.zeros_like(ybuf)
        gather_tile(0, 0)

    @pl.when(i + 1 < pl.num_programs(0))
    def _():
        gather_tile(i + 1, 1 - slot)

    rows = pl.multiple_of(tot_ref[i], SUBLANES)
    pltpu.make_async_copy(ys_ref.at[pl.ds(0, rows)], ybuf.at[slot, pl.ds(0, rows)],
                          sems.at[slot]).wait()

    w = rwt_ref[...]
    c_id = lax.broadcasted_iota(jnp.int32, (tc, lrows), 1)
    pos1 = w[:, 2:3].astype(jnp.int32)
    pos2 = w[:, 3:4].astype(jnp.int32)
    wmat = (jnp.where(c_id == pos1, w[:, 0:1], 0.0)
            + jnp.where(c_id == pos2, w[:, 1:2], 0.0)).astype(BF16)
    y_lo, y_hi = _unpack_halves(ybuf[slot])
    moe = jnp.concatenate([jnp.dot(wmat, y_lo, preferred_element_type=F32),
                           jnp.dot(wmat, y_hi, preferred_element_type=F32)], axis=-1)
    y_ref[...] = _rms_f32(h_ref[...] + moe, fg_ref[...])


def _combine(c8, loff, goff, tot, rwt, h, final_g, ys, tc, lrows):
    n, d = h.shape
    return pl.pallas_call(
        functools.partial(_combine_kernel, tc=tc, lrows=lrows),
        grid_spec=pltpu.PrefetchScalarGridSpec(
            num_scalar_prefetch=4,
            grid=(n // tc,),
            in_specs=[pl.BlockSpec((tc, LANES), lambda i, *_: (i, 0)),
                      pl.BlockSpec((tc, d), lambda i, *_: (i, 0)),
                      pl.BlockSpec((1, d), lambda i, *_: (0, 0)),
                      pl.BlockSpec(memory_space=pl.ANY)],
            out_specs=pl.BlockSpec((tc, d), lambda i, *_: (i, 0)),
            scratch_shapes=[pltpu.VMEM((2, lrows, d // 2), jnp.int32),
                            pltpu.SemaphoreType.DMA((2,))]),
        out_shape=jax.ShapeDtypeStruct((n, d), F32),
        compiler_params=_cparams(("arbitrary",)),
        name="combine",
    )(c8, loff, goff, tot, rwt, h, final_g, ys)


def _block_diag(w, per):
    nb, c, _ = w.shape
    eye = jnp.eye(per, dtype=w.dtype)
    wg = w.reshape(nb // per, per, c, c)
    return jnp.einsum("gpij,pq->gpiqj", wg, eye).reshape(nb // per, per * c, per * c)


def _router_tables(w_group, b_group, w_fine, b_fine):
    d = w_group.shape[0]
    w = jnp.zeros((ROUTER_ROWS, d), F32)
    w = w.at[0:N_GROUPS].set(w_group.T).at[SUBLANES:SUBLANES + N_EXPERTS].set(w_fine.T)
    b = jnp.full((ROUTER_ROWS,), NEG_BIG, F32)
    b = b.at[0:N_GROUPS].set(b_group).at[SUBLANES:SUBLANES + N_EXPERTS].set(b_fine)
    w_hi = w.astype(BF16)
    w_lo = (w - w_hi.astype(F32)).astype(BF16)
    return jnp.concatenate([w_hi, w_lo], axis=0), b.reshape(ROUTER_ROWS, 1)


def kernel(x, norm1_g, w_in, conv_w, conv_b, w_rgate, b_rgate, w_igate, b_igate, lam, sb_norm_g,
           lru_norm_g, w_out, norm2_g, w_group, b_group, w_fine, b_fine, w_e_gate, w_e_up,
           w_e_down, final_g):
    batch, seq, d = x.shape
    n = batch * seq
    width = w_in.shape[1] // 5
    tm = min(512, seq)
    tme = 1024
    gate_per = LANES // w_rgate.shape[1]

    x2 = x.reshape(n, d)
    vec = lambda a: a.reshape(1, -1)

    q, k, v, xl, gl = _in_proj(x2, vec(norm1_g), w_in, width, min(2 * tm, seq))
    n_pipe = seq // LANES - (ATTN_WINDOW_BLOCKS + 1)
    unroll = ATTN_UNROLL if n_pipe % ATTN_UNROLL == 0 else 2
    out_sb = _attention(q, k, v, batch, seq, LANES, ATTN_WINDOW_BLOCKS, unroll)
    out_lru = _lru(xl, gl, conv_w, vec(conv_b),
                   _block_diag(w_rgate, gate_per).astype(BF16), vec(b_rgate),
                   _block_diag(w_igate, gate_per).astype(BF16), vec(b_igate),
                   vec(lam), batch, seq, min(1024, seq))

    wr_stack, rbias = _router_tables(w_group, b_group, w_fine, b_fine)
    h, u2, rr, rwt, tcnt = _mix_route(out_sb, out_lru, x2, vec(sb_norm_g), vec(lru_norm_g),
                                      w_out, vec(norm2_g), wr_stack, rbias, tm)

    n_tiles = n // tm
    assert n_tiles <= LANES, "one lane of the per-tile count table per token tile"
    c8 = tcnt[:, :n_tiles].T
    erows = jnp.sum(c8, axis=0)
    eoff = jnp.cumsum(erows) - erows
    goff = eoff[None, :] + jnp.cumsum(c8, axis=0) - c8
    loff = jnp.cumsum(c8, axis=1) - c8
    tot = jnp.sum(c8, axis=1)
    lrows = 2 * tm + N_EXPERTS * SUBLANES
    p_rows = 2 * n + n_tiles * N_EXPERTS * (SUBLANES - 1)
    p_rows = -(-p_rows // SUBLANES) * SUBLANES
    i32 = lambda a: a.reshape(-1).astype(jnp.int32)
    c8, loff, goff, tot, eoff, erows = (i32(a) for a in (c8, loff, goff, tot, eoff, erows))

    xs = _dispatch(c8, loff, goff, tot, jnp.sum(erows, keepdims=True), rr, u2, p_rows, tm, lrows, tme)
    ys = _experts(eoff, erows, xs, w_e_gate, w_e_up, w_e_down, tme)
    y = _combine(c8, loff, goff, tot, rwt, h, vec(final_g), ys, tm, lrows)
    return y.reshape(batch, seq, d)
```

```python
import functools
import math

import jax
import jax.numpy as jnp
from jax import lax
from jax.experimental import pallas as pl
from jax.experimental.pallas import tpu as pltpu

F32 = jnp.float32
BF16 = jnp.bfloat16

EPS = 1e-6
HEAD_DIM = 64
HEADS_PER_BLOCK = 2
LANES = 128
SUBLANES = 8
CONV_W = 4
RG_C = 8.0
N_GROUPS = 4
EXPERTS_PER_GROUP = 8
N_EXPERTS = N_GROUPS * EXPERTS_PER_GROUP
ROUTER_ROWS = 48
NEG_BIG = -1e30
LOG2_E = math.log2(math.e)
ATTN_STOP = 104.0 * LOG2_E
ATTN_WINDOW_BLOCKS = 3
ATTN_UNROLL = 12
HIGH_HALF = -65536
EXPERT_TILE_ROWS = 1536
EXPERT_ROW_STEP = 256

VMEM_LIMIT = 56 * 1024 * 1024


def _cparams(sem):
    return pltpu.CompilerParams(dimension_semantics=sem, vmem_limit_bytes=VMEM_LIMIT)


def _rms_f32(x, g):
    return x * lax.rsqrt(jnp.mean(x * x, axis=-1, keepdims=True) + EPS) * g


def _in_proj_kernel(x_ref, g_ref, w_ref, q_ref, k_ref, v_ref, xl_ref, gl_ref, w_bf, *,
                    width, q_scale):
    @pl.when(pl.program_id(0) == 0)
    def _():
        for c in range(w_ref.shape[1] // width):
            cols = slice(c * width, (c + 1) * width)
            w_bf[:, cols] = w_ref[:, cols].astype(BF16)

    u = _rms_f32(x_ref[...], g_ref[...]).astype(BF16)
    for c, o_ref in enumerate((q_ref, k_ref, v_ref, xl_ref, gl_ref)):
        p = jnp.dot(u, w_bf[:, c * width:(c + 1) * width], preferred_element_type=F32)
        if c == 0:
            p = p * q_scale
        o_ref[...] = p.astype(o_ref.dtype)


def _in_proj(x2, g, w, width, tm):
    n, d = x2.shape
    row = lambda i: (i, 0)
    out_bf = jax.ShapeDtypeStruct((n, width), BF16)
    out_f = jax.ShapeDtypeStruct((n, width), F32)
    return pl.pallas_call(
        functools.partial(_in_proj_kernel, width=width, q_scale=1.0 / math.sqrt(HEAD_DIM)),
        grid=(n // tm,),
        in_specs=[pl.BlockSpec((tm, d), row),
                  pl.BlockSpec((1, d), lambda i: (0, 0)),
                  pl.BlockSpec(w.shape, lambda i: (0, 0), pipeline_mode=pl.Buffered(1))],
        out_specs=[pl.BlockSpec((tm, width), row)] * 5,
        out_shape=[out_bf, out_bf, out_bf, out_f, out_f],
        scratch_shapes=[pltpu.VMEM(w.shape, BF16)],
        compiler_params=_cparams(("arbitrary",)),
        name="in_proj",
    )(x2, g, w)


def _attn_kernel(q_ref, k_ref, v_ref, o_ref, tri_ref, z_ref, arg_ref, ctot_ref, acc_ref, carry_ref,
                 *, tq, nwin):
    seq = q_ref.shape[0]
    win = nwin * tq
    lane = lax.broadcasted_iota(jnp.int32, (1, LANES), 1)
    r_id = lax.broadcasted_iota(jnp.int32, (tq, tq), 0)
    c_id = lax.broadcasted_iota(jnp.int32, (tq, tq), 1)
    causal = c_id < r_id
    rel = c_id - r_id

    k_r = lax.broadcasted_iota(jnp.int32, (tq, 2 * tq), 0)
    k_c = lax.broadcasted_iota(jnp.int32, (tq, 2 * tq), 1)
    tri_ref[...] = jnp.where(k_c >= tq, 1.0, jnp.where(k_r > k_c, 1.0, 0.0)).astype(BF16)

    def softplus2(z):
        return jnp.maximum(z, 0.0) + jnp.log2(1.0 + jnp.exp2(-jnp.abs(z)))

    def scores(qh, keys):
        z = LOG2_E * lax.dot_general(qh, keys, (((1,), (1,)), ((), ())),
                                     preferred_element_type=F32)
        nlog_nb = softplus2(z)
        return nlog_nb, z - nlog_nb

    def suffix(nlog_nb):
        r = jnp.dot(nlog_nb.astype(BF16), tri_ref[...], preferred_element_type=F32)
        return r[:, :tq], r[:, tq:]

    def window(qh, wstart, masks):
        nlog_nb, log_b = scores(qh, k_ref[pl.ds(wstart, win), :])
        carry = None
        parts = [None] * nwin
        for b in reversed(range(nwin)):
            nl = nlog_nb[:, b * tq:(b + 1) * tq]
            if masks[b] is not None:
                nl = jnp.where(masks[b], nl, 0.0)
            excl, tot = suffix(nl)
            arg = log_b[:, b * tq:(b + 1) * tq] - excl
            a = jnp.exp2(arg if carry is None else arg - carry)
            if masks[b] is not None:
                a = jnp.where(masks[b], a, 0.0)
            parts[b] = a.astype(BF16)
            carry = tot if carry is None else carry + tot
        out = jnp.dot(jnp.concatenate(parts, axis=1), v_ref[pl.ds(wstart, win), :],
                      preferred_element_type=F32)
        return out, carry

    def head_queries(i):
        q = q_ref[pl.ds(i * tq, tq), :]
        return [jnp.where((lane >= h * HEAD_DIM) & (lane < (h + 1) * HEAD_DIM), q, jnp.zeros_like(q))
                for h in range(HEADS_PER_BLOCK)]

    def store(i, outs):
        o_ref[pl.ds(i * tq, tq), :] = jnp.where(lane < HEAD_DIM, outs[0], outs[1])

    for i in range(nwin - 1):
        masks = [(rel + b * tq) < i * tq for b in range(nwin)]
        store(i, [window(qh, 0, masks)[0] for qh in head_queries(i)])

    def window_start(i):
        return pl.multiple_of((i - (nwin - 1)) * tq, tq)

    def stage_scores(i, p):
        keys = k_ref[pl.ds(window_start(i), win), :]
        for h, qh in enumerate(head_queries(i)):
            z_ref[p, h] = LOG2_E * lax.dot_general(qh, keys, (((1,), (1,)), ((), ())),
                                                   preferred_element_type=F32)

    def stage_exponents(p):
        carries = []
        for h in range(HEADS_PER_BLOCK):
            z = z_ref[p, h]
            softplus = softplus2(z)
            carry = None
            for b in reversed(range(nwin)):
                diag = b == nwin - 1
                cols = slice(b * tq, (b + 1) * tq)
                nl = softplus[:, cols]
                if diag:
                    nl = jnp.where(causal, nl, 0.0)
                excl, tot = suffix(nl)
                arg = z[:, cols] - softplus[:, cols] - excl
                if carry is not None:
                    arg = arg - carry
                if diag:
                    arg = jnp.where(causal, arg, NEG_BIG)
                arg_ref[p, h, :, cols] = arg
                carry = tot if carry is None else carry + tot
            ctot_ref[p, h] = carry
            carries.append(carry)
        return jnp.min(jnp.minimum(carries[0], carries[1]))

    def stage_output(i, p, s):
        vals = v_ref[pl.ds(window_start(i), win), :]
        for h in range(HEADS_PER_BLOCK):
            a = jnp.exp2(arg_ref[p, h]).astype(BF16)
            acc_ref[s, h] = jnp.dot(a, vals, preferred_element_type=F32)
            carry_ref[s, h] = ctot_ref[p, h]

    def finish(i, s, cmin):
        def cond(state):
            j, cmin = state
            return (j >= 0) & (cmin <= ATTN_STOP)

        def older(state):
            j, _ = state
            start = pl.multiple_of(j * tq, tq)
            keys = k_ref[pl.ds(start, tq), :]
            vals = v_ref[pl.ds(start, tq), :]
            cs = []
            for h, qh in enumerate(head_queries(i)):
                nlog_nb, log_b = scores(qh, keys)
                excl, tot = suffix(nlog_nb)
                carry = carry_ref[s, h]
                a = jnp.exp2(log_b - excl - carry)
                acc_ref[s, h] += jnp.dot(a.astype(BF16), vals, preferred_element_type=F32)
                carry_ref[s, h] = carry + tot
                cs.append(carry + tot)
            return j - 1, jnp.min(jnp.minimum(cs[0], cs[1]))

        lax.while_loop(cond, older, (i - nwin, cmin))
        store(i, [acc_ref[s, 0], acc_ref[s, 1]])

    first = nwin - 1
    n_blocks = seq // tq
    unroll = acc_ref.shape[0]
    slot = lambda i: (first + i) % 2
    stage_scores(first, slot(0))
    stage_scores(first + 1, slot(1))
    cmin = stage_exponents(slot(0))

    def steady(m, cmin):
        t = first + 2 + unroll * m
        cmins = [cmin]
        for u in range(unroll):
            stage_output(t - 2 + u, slot(u), u)
            cmins.append(stage_exponents(slot(u + 1)))
            stage_scores(t + u, slot(u))
        for u in range(unroll):
            finish(t - 2 + u, u, cmins[u])
        return cmins[unroll]

    cmin = lax.fori_loop(0, (n_blocks - first - 2) // unroll, steady, cmin)
    stage_output(n_blocks - 2, slot(0), 0)
    cmin_last = stage_exponents(slot(1))
    finish(n_blocks - 2, 0, cmin)
    stage_output(n_blocks - 1, slot(1), 1)
    finish(n_blocks - 1, 1, cmin_last)


def _attention(q, k, v, batch, seq, tq, nwin, unroll):
    n, width = q.shape
    assert tq == LANES and seq >= (nwin + 1) * tq
    assert unroll % 2 == 0 and (seq // tq - (nwin + 1)) % unroll == 0
    blk = pl.BlockSpec((seq, LANES), lambda b, hp: (b, hp))
    stage_buf = pltpu.VMEM((2, HEADS_PER_BLOCK, tq, nwin * tq), F32)
    carry_buf = pltpu.VMEM((2, HEADS_PER_BLOCK, tq, tq), F32)
    row_buf = pltpu.VMEM((unroll, HEADS_PER_BLOCK, tq, LANES), F32)
    return pl.pallas_call(
        functools.partial(_attn_kernel, tq=tq, nwin=nwin),
        grid=(batch, width // LANES),
        in_specs=[blk, blk, blk],
        out_specs=blk,
        out_shape=jax.ShapeDtypeStruct((n, width), F32),
        scratch_shapes=[pltpu.VMEM((tq, 2 * tq), BF16),
                        stage_buf, stage_buf, carry_buf, row_buf, row_buf],
        compiler_params=_cparams(("arbitrary", "arbitrary")),
        name="attn",
    )(q, k, v)


def _gelu_tanh(x):
    c = math.sqrt(2.0 / math.pi)
    half_x = 0.5 * x
    return half_x + half_x * jnp.tanh(x * (c + (c * 0.044715) * (x * x)))


def _sigmoid(x):
    return 0.5 + 0.5 * jnp.tanh(0.5 * x)


def _lru_kernel(*refs, ts, n_slab):
    xl_refs, gl_refs = refs[0:n_slab], refs[n_slab:2 * n_slab]
    cw_ref, cb_ref, wr_ref, br_ref, wi_ref, bi_ref, lam_ref = refs[2 * n_slab:2 * n_slab + 7]
    o_refs = refs[2 * n_slab + 7:3 * n_slab + 7]
    tail_ref, a7_ref, u7_ref, hp_ref, pa_ref, pu_ref, h_ref = refs[3 * n_slab + 7:]
    t = pl.program_id(1)
    groups = ts // SUBLANES

    @pl.when(t == 0)
    def _():
        tail_ref[...] = jnp.zeros_like(tail_ref)
        h_ref[...] = jnp.zeros_like(h_ref)

    first_group = lax.broadcasted_iota(jnp.int32, (groups, LANES), 0) == 0
    for c in range(n_slab):
        lanes = slice(c * LANES, (c + 1) * LANES)
        x = [xl_refs[c][pl.ds(s, groups, stride=SUBLANES), :] for s in range(SUBLANES)]
        shifted = {}
        for s in range(SUBLANES - (CONV_W - 1), SUBLANES):
            shifted[s] = jnp.where(first_group, tail_ref[c, s:s + 1, :], pltpu.roll(x[s], 1, axis=0))
            tail_ref[c, s:s + 1, :] = x[s][groups - 1:groups, :]
        conv = []
        for s in range(SUBLANES):
            y = cb_ref[:, lanes]
            for w in range(CONV_W):
                j = s - (CONV_W - 1) + w
                y = y + (x[j] if j >= 0 else shifted[j + SUBLANES]) * cw_ref[w:w + 1, lanes]
            conv.append(y)
        xc = jnp.concatenate(conv, axis=0)

        xcb = xc.astype(BF16)
        r = _sigmoid(jnp.dot(xcb, wr_ref[c], preferred_element_type=F32) + br_ref[:, lanes])
        ig = _sigmoid(jnp.dot(xcb, wi_ref[c], preferred_element_type=F32) + bi_ref[:, lanes])
        lam = lam_ref[:, lanes]
        log_sig_lam = -(jnp.maximum(-lam, 0.0) + jnp.log1p(jnp.exp(-jnp.abs(lam))))
        log_a = RG_C * r * log_sig_lam
        a = jnp.exp(log_a)
        th = jnp.tanh(log_a)
        one_m_a2 = -2.0 * th / (1.0 - th)
        root = jnp.where(one_m_a2 > 0.0, one_m_a2 * lax.rsqrt(one_m_a2), 0.0)
        u = root * (ig * xc)

        a_run = u_run = None
        for s in range(SUBLANES):
            rows = slice(s * groups, (s + 1) * groups)
            if s == 0:
                a_run, u_run = a[rows], u[rows]
            else:
                u_run = a[rows] * u_run + u[rows]
                a_run = a_run * a[rows]
            pa_ref[c, rows, :] = a_run
            pu_ref[c, rows, :] = u_run
        a7_ref[c] = a_run
        u7_ref[c] = u_run

    def group(g, hs):
        nxt = []
        for c in range(n_slab):
            hp_ref[c, pl.ds(g, 1), :] = hs[c]
            nxt.append(a7_ref[c, pl.ds(g, 1), :] * hs[c] + u7_ref[c, pl.ds(g, 1), :])
        return tuple(nxt)

    hs = lax.fori_loop(0, groups, group, tuple(h_ref[c] for c in range(n_slab)), unroll=8)
    for c in range(n_slab):
        h_ref[c] = hs[c]

    for c in range(n_slab):
        h_in = hp_ref[c]
        for s in range(SUBLANES):
            rows = slice(s * groups, (s + 1) * groups)
            hseq = pu_ref[c, rows, :] + pa_ref[c, rows, :] * h_in
            gate = _gelu_tanh(gl_refs[c][pl.ds(s, groups, stride=SUBLANES), :])
            o_refs[c][pl.ds(s, groups, stride=SUBLANES), :] = hseq * gate


def _lru(xl, gl, conv_w, conv_b, wr_bd, br, wi_bd, bi, lam, batch, seq, ts):
    n, width = xl.shape
    nt = seq // ts
    n_slab = width // LANES
    assert wr_bd.shape == (n_slab, LANES, LANES) and ts % (SUBLANES * SUBLANES) == 0
    groups = ts // SUBLANES
    slab = [pl.BlockSpec((ts, LANES), functools.partial(lambda b, t, c: (b * nt + t, c), c=c))
            for c in range(n_slab)]
    const2 = lambda b, t: (0, 0)
    const3 = lambda b, t: (0, 0, 0)
    vec = pl.BlockSpec((1, width), const2)
    per_group = pltpu.VMEM((n_slab, groups, LANES), F32)
    per_step = pltpu.VMEM((n_slab, ts, LANES), F32)
    return pl.pallas_call(
        functools.partial(_lru_kernel, ts=ts, n_slab=n_slab),
        grid=(batch, nt),
        in_specs=slab + slab + [pl.BlockSpec((CONV_W, width), const2), vec,
                                pl.BlockSpec(wr_bd.shape, const3), vec,
                                pl.BlockSpec(wi_bd.shape, const3), vec, vec],
        out_specs=[pl.BlockSpec((ts, LANES), lambda b, t: (b * nt + t, 0))] * n_slab,
        out_shape=[jax.ShapeDtypeStruct((n, LANES), F32)] * n_slab,
        scratch_shapes=[pltpu.VMEM((n_slab, SUBLANES, LANES), F32),
                        per_group, per_group, per_group, per_step, per_step,
                        pltpu.VMEM((n_slab, 1, LANES), F32)],
        compiler_params=_cparams(("arbitrary", "arbitrary")),
        name="lru",
    )(*([xl] * n_slab), *([gl] * n_slab), conv_w, conv_b, wr_bd, br, wi_bd, bi, lam)


def _mix_route_kernel(sb_ref, *refs, tm, n_slab):
    lru_refs = refs[:n_slab]
    (x_ref, sbg_ref, lrug_ref, wo_ref, n2g_ref, wrs_ref, rb_ref,
     h_ref, u2_ref, rr_ref, rwt_ref, tc_ref, before_ref, u2s_ref, wo_bf) = refs[n_slab:]

    @pl.when(pl.program_id(0) == 0)
    def _():
        wo_bf[...] = wo_ref[...].astype(BF16)

    _mix_route_body(sb_ref, lru_refs, x_ref, sbg_ref, lrug_ref, wo_bf, n2g_ref, wrs_ref, rb_ref,
                    h_ref, u2_ref, rr_ref, rwt_ref, tc_ref, before_ref, u2s_ref, tm)


def _mix_route_body(sb_ref, lru_refs, x_ref, sbg_ref, lrug_ref, wo_ref, n2g_ref, wrs_ref, rb_ref,
                    h_ref, u2_ref, rr_ref, rwt_ref, tc_ref, before_ref, u2s_ref, tm):
    step = pl.program_id(0)
    half = sb_ref.shape[1]
    n_sub = x_ref.shape[0] // tm

    @pl.when(step == 0)
    def _():
        r_id = lax.broadcasted_iota(jnp.int32, (tm, tm), 0)
        c_id = lax.broadcasted_iota(jnp.int32, (tm, tm), 1)
        before_ref[...] = (r_id < c_id).astype(BF16)
        tc_ref[...] = jnp.zeros_like(tc_ref)
        u2s_ref[...] = jnp.zeros_like(u2s_ref)

    def route(j):
        rows = slice(j * tm, (j + 1) * tm)
        u2 = u2s_ref[rows, :]
        u_hi = u2.astype(BF16)
        u_lo = (u2 - u_hi.astype(F32)).astype(BF16)
        nt_dims = (((1,), (1,)), ((), ()))
        n_rows = rb_ref.shape[0]
        both = lax.dot_general(wrs_ref[...], u_hi, nt_dims, preferred_element_type=F32)
        lt = (both[0:n_rows] + both[n_rows:2 * n_rows]
              + lax.dot_general(wrs_ref[0:n_rows, :], u_lo, nt_dims, preferred_element_type=F32)
              + rb_ref[...])

        sub = lax.broadcasted_iota(jnp.int32, (SUBLANES, tm), 0)

        def top1(x):
            m = jnp.max(x, axis=0, keepdims=True)
            idx = jnp.min(jnp.where(x == m, sub, SUBLANES), axis=0, keepdims=True)
            return m, idx

        grp = lt[0:SUBLANES, :]
        g_max, g_idx = top1(grp)
        g_p = 1.0 / jnp.sum(jnp.exp(grp - g_max), axis=0, keepdims=True)
        fine = lt[SUBLANES:2 * SUBLANES, :]
        for g in range(1, N_GROUPS):
            fine = jnp.where(g_idx == g, lt[(g + 1) * SUBLANES:(g + 2) * SUBLANES, :], fine)
        m1, i1 = top1(fine)
        m2, i2 = top1(jnp.where(sub == i1, -jnp.inf, fine))
        e2 = jnp.exp(m2 - m1)
        p1 = 1.0 / (1.0 + e2)
        w1 = g_p * p1
        w2 = g_p * (e2 * p1)
        x1 = g_idx * EXPERTS_PER_GROUP + i1
        x2 = g_idx * EXPERTS_PER_GROUP + i2

        eid = lax.broadcasted_iota(jnp.int32, (N_EXPERTS, tm), 0)
        oh1 = jnp.where(eid == x1, 1.0, 0.0)
        oh2 = jnp.where(eid == x2, 1.0, 0.0)
        pre1 = jnp.dot(oh1.astype(BF16), before_ref[...], preferred_element_type=F32)
        pre2 = jnp.dot(oh2.astype(BF16), before_ref[...], preferred_element_type=F32)
        cnt1 = jnp.sum(oh1, axis=1, keepdims=True)
        cnt2 = jnp.sum(oh2, axis=1, keepdims=True)
        seg8 = jnp.floor((cnt1 + cnt2 + (SUBLANES - 1.0)) * (1.0 / SUBLANES))
        e_r = lax.broadcasted_iota(jnp.int32, (N_EXPERTS, N_EXPERTS), 0)
        e_c = lax.broadcasted_iota(jnp.int32, (N_EXPERTS, N_EXPERTS), 1)
        lower = jnp.where(e_c < e_r, 1.0, 0.0).astype(BF16)
        seg8_b = jnp.broadcast_to(seg8, (N_EXPERTS, LANES)).astype(BF16)
        seg_off = SUBLANES * jnp.dot(lower, seg8_b, preferred_element_type=F32)[:, 0:1]
        pos1 = jnp.sum(oh1 * (pre1 + seg_off), axis=0, keepdims=True)
        pos2 = jnp.sum(oh2 * (pre2 + (seg_off + cnt1)), axis=0, keepdims=True)

        lane = lax.broadcasted_iota(jnp.int32, tc_ref.shape, 1)
        seg_rows = jnp.broadcast_to(seg8 * SUBLANES, tc_ref.shape).astype(jnp.int32)
        tile = (step - 1) * n_sub + j
        tc_ref[...] = jnp.where(lane == tile, seg_rows, tc_ref[...])

        zrow = jnp.zeros((SUBLANES - 4, tm), jnp.int32)
        rr_ref[:, rows] = jnp.concatenate(
            [pos1.astype(jnp.int32), pos2.astype(jnp.int32), x1, x2, zrow], axis=0)
        wt = jnp.concatenate([w1, w2, pos1, pos2, jnp.zeros((LANES - 4, tm), F32)], axis=0)
        rwt_ref[rows, :] = wt.T

    for j in range(n_sub):
        route(j)

    m_sb = _rms_f32(sb_ref[...], sbg_ref[...]).astype(BF16)
    lru = jnp.concatenate([r[...] for r in lru_refs], axis=-1)
    m_lru = _rms_f32(lru, lrug_ref[...]).astype(BF16)
    h = (x_ref[...]
         + jnp.dot(m_sb, wo_ref[0:half, :], preferred_element_type=F32)
         + jnp.dot(m_lru, wo_ref[half:2 * half, :], preferred_element_type=F32))
    h_ref[...] = h
    u2_next = _rms_f32(h, n2g_ref[...])
    u2_ref[...] = u2_next.astype(BF16)
    u2s_ref[...] = u2_next


def _mix_route(sb, lru, x2, sbg, lrug, w_out, n2g, wr_stack, rbias, tm, tp):
    n, d = x2.shape
    half = sb.shape[1]
    n_tiles = n // tp
    row = lambda i: (jnp.minimum(i, n_tiles - 1), 0)
    routed = lambda i: (jnp.maximum(i - 1, 0), 0)
    const = lambda i: (0, 0)
    return pl.pallas_call(
        functools.partial(_mix_route_kernel, tm=tm, n_slab=len(lru)),
        grid=(n_tiles + 1,),
        in_specs=[pl.BlockSpec((tp, half), row)] + [pl.BlockSpec((tp, LANES), row)] * len(lru)
                 + [pl.BlockSpec((tp, d), row),
                  pl.BlockSpec((1, half), const), pl.BlockSpec((1, half), const),
                  pl.BlockSpec(w_out.shape, const, pipeline_mode=pl.Buffered(1)),
                  pl.BlockSpec((1, d), const),
                  pl.BlockSpec(wr_stack.shape, const), pl.BlockSpec(rbias.shape, const)],
        out_specs=[pl.BlockSpec((tp, d), row), pl.BlockSpec((tp, d), row),
                   pl.BlockSpec((SUBLANES, tp), lambda i: (0, jnp.maximum(i - 1, 0))),
                   pl.BlockSpec((tp, LANES), routed),
                   pl.BlockSpec((N_EXPERTS, LANES), const)],
        out_shape=[jax.ShapeDtypeStruct((n, d), F32), jax.ShapeDtypeStruct((n, d), BF16),
                   jax.ShapeDtypeStruct((SUBLANES, n), jnp.int32),
                   jax.ShapeDtypeStruct((n, LANES), F32),
                   jax.ShapeDtypeStruct((N_EXPERTS, LANES), jnp.int32)],
        scratch_shapes=[pltpu.VMEM((tm, tm), BF16), pltpu.VMEM((tp, d), F32),
                        pltpu.VMEM(w_out.shape, BF16)],
        compiler_params=_cparams(("arbitrary",)),
        name="mix_route",
    )(sb, *lru, x2, sbg, lrug, w_out, n2g, wr_stack, rbias)


def _pack_halves(x):
    half = x.shape[1] // 2
    lo = lax.shift_right_logical(lax.bitcast_convert_type(x[:, :half], jnp.int32), 16)
    hi = lax.bitcast_convert_type(x[:, half:], jnp.int32) & HIGH_HALF
    return hi | lo


def _unpack_halves(p):
    lo = lax.bitcast_convert_type(lax.shift_left(p, 16), F32)
    hi = lax.bitcast_convert_type(p & HIGH_HALF, F32)
    return lo.astype(BF16), hi.astype(BF16)


def _segment_copies(tile, c8_ref, loff_ref, goff_ref, make):
    for e in range(N_EXPERTS):
        idx = tile * N_EXPERTS + e
        rows = pl.multiple_of(c8_ref[idx], SUBLANES)

        @pl.when(rows > 0)
        def _(idx=idx, rows=rows):
            lo = pl.multiple_of(loff_ref[idx], SUBLANES)
            go = pl.multiple_of(goff_ref[idx], SUBLANES)
            make(pl.ds(lo, rows), pl.ds(go, rows)).start()


def _dispatch_kernel(c8_ref, loff_ref, goff_ref, tot_ref, used_ref,
                     rr_ref, u2_ref, xs_ref, lbuf, zbuf, sem, zsem, *, td, lrows):
    i = pl.program_id(0)
    slot = i % 2

    r_id = lax.broadcasted_iota(jnp.int32, (lrows, td), 0)
    perm = jnp.where(r_id == rr_ref[0:1, :], 1.0, jnp.where(r_id == rr_ref[1:2, :], 1.0, 0.0))
    sorted_rows = jnp.dot(perm.astype(BF16), u2_ref[...], preferred_element_type=F32)
    lbuf[slot] = _pack_halves(sorted_rows)

    @pl.when(i == 0)
    def _():
        zbuf[...] = jnp.zeros_like(zbuf)
        chunk = zbuf.shape[0]
        used = used_ref[0]
        spare = xs_ref.shape[0] - used
        n_fill = (spare + chunk - 1) // chunk

        def fill_copy(k):
            rows = pl.multiple_of(jnp.minimum(chunk, spare - k * chunk), SUBLANES)
            start = pl.multiple_of(used + k * chunk, SUBLANES)
            return pltpu.make_async_copy(zbuf.at[pl.ds(0, rows)], xs_ref.at[pl.ds(start, rows)],
                                         zsem)

        def fill_start(k, c):
            fill_copy(k).start()
            return c

        def fill_wait(k, c):
            fill_copy(k).wait()
            return c

        lax.fori_loop(0, n_fill, fill_start, 0)
        lax.fori_loop(0, n_fill, fill_wait, 0)

    def wait_tile(tile, s):
        rows = pl.multiple_of(tot_ref[tile], SUBLANES)
        pltpu.make_async_copy(lbuf.at[s, pl.ds(0, rows)], xs_ref.at[pl.ds(0, rows)], sem).wait()

    @pl.when(i > 0)
    def _():
        wait_tile(i - 1, 1 - slot)

    _segment_copies(i, c8_ref, loff_ref, goff_ref,
                    lambda loc, glob: pltpu.make_async_copy(lbuf.at[slot, loc], xs_ref.at[glob], sem))

    @pl.when(i == pl.num_programs(0) - 1)
    def _():
        wait_tile(i, slot)


def _dispatch(c8, loff, goff, tot, used, rr, u2, p_rows, td, lrows, tme):
    n, d = u2.shape
    pmap = lambda i, *_: (0, i)
    return pl.pallas_call(
        functools.partial(_dispatch_kernel, td=td, lrows=lrows),
        grid_spec=pltpu.PrefetchScalarGridSpec(
            num_scalar_prefetch=5,
            grid=(n // td,),
            in_specs=[pl.BlockSpec((SUBLANES, td), pmap),
                      pl.BlockSpec((td, d), lambda i, *_: (i, 0))],
            out_specs=pl.BlockSpec(memory_space=pl.ANY),
            scratch_shapes=[pltpu.VMEM((2, lrows, d // 2), jnp.int32),
                            pltpu.VMEM((tme, d // 2), jnp.int32),
                            pltpu.SemaphoreType.DMA(()), pltpu.SemaphoreType.DMA(())]),
        out_shape=jax.ShapeDtypeStruct((p_rows, d // 2), jnp.int32),
        compiler_params=_cparams(("arbitrary",)),
        name="dispatch",
    )(c8, loff, goff, tot, used, rr, u2)


def _experts_kernel(eoff_ref, erows_ref, xs_ref, wg_ref, wu_ref, wd_ref, ys_ref,
                    wg_bf, wu_bf, wd_bf, xbuf, ybuf, sem_in, sem_out, state, *, tme):
    e = pl.program_id(0)
    n_experts = pl.num_programs(0)
    rows = erows_ref[e]
    off = eoff_ref[e]
    n_tiles = (rows + tme - 1) // tme
    nxt = jnp.minimum(e + 1, n_experts - 1)
    prefetch = (rows > 0) & (e + 1 < n_experts) & (erows_ref[nxt] > 0)

    def tile_rows(total, k):
        return pl.multiple_of(jnp.minimum(tme, total - k * tme), SUBLANES)

    def in_copy(start, r, slot):
        start = pl.multiple_of(start, SUBLANES)
        return pltpu.make_async_copy(xs_ref.at[pl.ds(start, r)], xbuf.at[slot, pl.ds(0, r)],
                                     sem_in.at[slot])

    def out_copy(start, r, slot):
        start = pl.multiple_of(start, SUBLANES)
        return pltpu.make_async_copy(ybuf.at[slot, pl.ds(0, r)], ys_ref.at[pl.ds(start, r)],
                                     sem_out.at[slot])

    @pl.when(e == 0)
    def _():
        for s in range(4):
            state[s] = 0
        xbuf[...] = jnp.zeros_like(xbuf)

    @pl.when(rows > 0)
    def _():
        wg_bf[...] = wg_ref[0].astype(BF16)
        wu_bf[...] = wu_ref[0].astype(BF16)
        wd_bf[...] = wd_ref[0].astype(BF16)
        done = state[0]

        @pl.when(state[1] == 0)
        def _():
            in_copy(off, tile_rows(rows, 0), done % 2).start()

        def tile(k, c):
            slot = (done + k) % 2
            r = tile_rows(rows, k)

            @pl.when(k + 1 < n_tiles)
            def _():
                in_copy(off + (k + 1) * tme, tile_rows(rows, k + 1), 1 - slot).start()

            @pl.when((k + 1 == n_tiles) & prefetch)
            def _():
                in_copy(eoff_ref[nxt], tile_rows(erows_ref[nxt], 0), 1 - slot).start()

            in_copy(off, r, slot).wait()

            @pl.when(done + k >= 2)
            def _():
                out_copy(0, pl.multiple_of(state[2 + slot], SUBLANES), slot).wait()

            def mlp(start, n):
                x = jnp.concatenate(_unpack_halves(xbuf[slot, start:start + n]), axis=1)
                hg = jnp.dot(x, wg_bf[...], preferred_element_type=F32)
                hu = jnp.dot(x, wu_bf[...], preferred_element_type=F32)
                act = (hg * jax.nn.sigmoid(hg) * hu).astype(BF16)
                y = jnp.dot(act, wd_bf[...], preferred_element_type=F32)
                ybuf[slot, start:start + n] = _pack_halves(y.astype(BF16).astype(F32))

            for n in range(EXPERT_ROW_STEP, tme + 1, EXPERT_ROW_STEP):
                @pl.when((r > n - EXPERT_ROW_STEP) & (r <= n))
                def _(n=n):
                    mlp(0, n)

            out_copy(off + k * tme, r, slot).start()
            state[2 + slot] = r
            return c

        lax.fori_loop(0, n_tiles, tile, 0)
        state[0] = done + n_tiles

    state[1] = prefetch.astype(jnp.int32)

    @pl.when(e == n_experts - 1)
    def _():
        total = eoff_ref[e] + rows
        for slot in range(2):
            @pl.when(state[0] > slot)
            def _(slot=slot):
                out_copy(0, pl.multiple_of(state[2 + slot], SUBLANES), slot).wait()
        ybuf[0] = jnp.zeros(ybuf.shape[1:], ybuf.dtype)
        spare = ys_ref.shape[0] - total
        n_fill = (spare + tme - 1) // tme

        def fill_copy(k):
            return out_copy(total + k * tme, tile_rows(spare, k), 0)

        def fill_start(k, c):
            fill_copy(k).start()
            return c

        def fill_wait(k, c):
            fill_copy(k).wait()
            return c

        lax.fori_loop(0, n_fill, fill_start, 0)
        lax.fori_loop(0, n_fill, fill_wait, 0)


def _experts(eoff, erows, xs, wg, wu, wd, tme):
    p = xs.shape[0]
    n_experts, d, de = wg.shape
    wmap = lambda e, *_: (e, 0, 0)
    return pl.pallas_call(
        functools.partial(_experts_kernel, tme=tme),
        grid_spec=pltpu.PrefetchScalarGridSpec(
            num_scalar_prefetch=2,
            grid=(n_experts,),
            in_specs=[pl.BlockSpec(memory_space=pl.ANY),
                      pl.BlockSpec((1, d, de), wmap), pl.BlockSpec((1, d, de), wmap),
                      pl.BlockSpec((1, de, d), wmap)],
            out_specs=pl.BlockSpec(memory_space=pl.ANY),
            scratch_shapes=[pltpu.VMEM((d, de), BF16), pltpu.VMEM((d, de), BF16),
                            pltpu.VMEM((de, d), BF16),
                            pltpu.VMEM((2, tme, d // 2), jnp.int32),
                            pltpu.VMEM((2, tme, d // 2), jnp.int32),
                            pltpu.SemaphoreType.DMA((2,)), pltpu.SemaphoreType.DMA((2,)),
                            pltpu.SMEM((4,), jnp.int32)]),
        out_shape=jax.ShapeDtypeStruct((p, d // 2), jnp.int32),
        compiler_params=_cparams(("arbitrary",)),
        name="experts",
    )(eoff, erows, xs, wg, wu, wd)


def _combine_kernel(c8_ref, loff_ref, goff_ref, tot_ref,
                    rwt_ref, h_ref, fg_ref, ys_ref, y_ref, ybuf, sems, *, tc, lrows):
    i = pl.program_id(0)
    slot = i % 2

    def gather_tile(tile, s):
        _segment_copies(tile, c8_ref, loff_ref, goff_ref,
                        lambda loc, glob: pltpu.make_async_copy(ys_ref.at[glob], ybuf.at[s, loc],
                                                                sems.at[s]))

    @pl.when(i == 0)
    def _():
        ybuf[...] = jnp.zeros_like(ybuf)
        gather_tile(0, 0)

    @pl.when(i + 1 < pl.num_programs(0))
    def _():
        gather_tile(i + 1, 1 - slot)

    rows = pl.multiple_of(tot_ref[i], SUBLANES)
    pltpu.make_async_copy(ys_ref.at[pl.ds(0, rows)], ybuf.at[slot, pl.ds(0, rows)],
                          sems.at[slot]).wait()

    w = rwt_ref[...]
    c_id = lax.broadcasted_iota(jnp.int32, (tc, lrows), 1)
    pos1 = w[:, 2:3].astype(jnp.int32)
    pos2 = w[:, 3:4].astype(jnp.int32)
    wmat = (jnp.where(c_id == pos1, w[:, 0:1], 0.0)
            + jnp.where(c_id == pos2, w[:, 1:2], 0.0)).astype(BF16)
    y_lo, y_hi = _unpack_halves(ybuf[slot])
    moe = jnp.concatenate([jnp.dot(wmat, y_lo, preferred_element_type=F32),
                           jnp.dot(wmat, y_hi, preferred_element_type=F32)], axis=-1)
    y_ref[...] = _rms_f32(h_ref[...] + moe, fg_ref[...])


def _combine(c8, loff, goff, tot, rwt, h, final_g, ys, tc, lrows):
    n, d = h.shape
    return pl.pallas_call(
        functools.partial(_combine_kernel, tc=tc, lrows=lrows),
        grid_spec=pltpu.PrefetchScalarGridSpec(
            num_scalar_prefetch=4,
            grid=(n // tc,),
            in_specs=[pl.BlockSpec((tc, LANES), lambda i, *_: (i, 0)),
                      pl.BlockSpec((tc, d), lambda i, *_: (i, 0)),
                      pl.BlockSpec((1, d), lambda i, *_: (0, 0)),
                      pl.BlockSpec(memory_space=pl.ANY)],
            out_specs=pl.BlockSpec((tc, d), lambda i, *_: (i, 0)),
            scratch_shapes=[pltpu.VMEM((2, lrows, d // 2), jnp.int32),
                            pltpu.SemaphoreType.DMA((2,))]),
        out_shape=jax.ShapeDtypeStruct((n, d), F32),
        compiler_params=_cparams(("arbitrary",)),
        name="combine",
    )(c8, loff, goff, tot, rwt, h, final_g, ys)


def _block_diag(w, per):
    nb, c, _ = w.shape
    eye = jnp.eye(per, dtype=w.dtype)
    wg = w.reshape(nb // per, per, c, c)
    return jnp.einsum("gpij,pq->gpiqj", wg, eye).reshape(nb // per, per * c, per * c)


def _router_tables(w_group, b_group, w_fine, b_fine):
    d = w_group.shape[0]
    w = jnp.zeros((ROUTER_ROWS, d), F32)
    w = w.at[0:N_GROUPS].set(w_group.T).at[SUBLANES:SUBLANES + N_EXPERTS].set(w_fine.T)
    b = jnp.full((ROUTER_ROWS,), NEG_BIG, F32)
    b = b.at[0:N_GROUPS].set(b_group).at[SUBLANES:SUBLANES + N_EXPERTS].set(b_fine)
    w_hi = w.astype(BF16)
    w_lo = (w - w_hi.astype(F32)).astype(BF16)
    return jnp.concatenate([w_hi, w_lo], axis=0), b.reshape(ROUTER_ROWS, 1)


def kernel(x, norm1_g, w_in, conv_w, conv_b, w_rgate, b_rgate, w_igate, b_igate, lam, sb_norm_g,
           lru_norm_g, w_out, norm2_g, w_group, b_group, w_fine, b_fine, w_e_gate, w_e_up,
           w_e_down, final_g):
    batch, seq, d = x.shape
    n = batch * seq
    width = w_in.shape[1] // 5
    tm = min(512, seq)
    tme = EXPERT_TILE_ROWS
    gate_per = LANES // w_rgate.shape[1]

    x2 = x.reshape(n, d)
    vec = lambda a: a.reshape(1, -1)

    q, k, v, xl, gl = _in_proj(x2, vec(norm1_g), w_in, width, min(2 * tm, seq))
    n_pipe = seq // LANES - (ATTN_WINDOW_BLOCKS + 1)
    unroll = ATTN_UNROLL if n_pipe % ATTN_UNROLL == 0 else 2
    out_sb = _attention(q, k, v, batch, seq, LANES, ATTN_WINDOW_BLOCKS, unroll)
    out_lru = _lru(xl, gl, conv_w, vec(conv_b),
                   _block_diag(w_rgate, gate_per).astype(BF16), vec(b_rgate),
                   _block_diag(w_igate, gate_per).astype(BF16), vec(b_igate),
                   vec(lam), batch, seq, min(1024, seq))

    wr_stack, rbias = _router_tables(w_group, b_group, w_fine, b_fine)
    h, u2, rr, rwt, tcnt = _mix_route(out_sb, out_lru, x2, vec(sb_norm_g), vec(lru_norm_g),
                                      w_out, vec(norm2_g), wr_stack, rbias, tm, min(2 * tm, seq))

    n_tiles = n // tm
    assert n_tiles <= LANES, "one lane of the per-tile count table per token tile"
    c8 = tcnt[:, :n_tiles].T
    erows = jnp.sum(c8, axis=0)
    eoff = jnp.cumsum(erows) - erows
    goff = eoff[None, :] + jnp.cumsum(c8, axis=0) - c8
    loff = jnp.cumsum(c8, axis=1) - c8
    tot = jnp.sum(c8, axis=1)
    lrows = 2 * tm + N_EXPERTS * SUBLANES
    p_rows = 2 * n + n_tiles * N_EXPERTS * (SUBLANES - 1)
    p_rows = -(-p_rows // SUBLANES) * SUBLANES
    i32 = lambda a: a.reshape(-1).astype(jnp.int32)
    c8, loff, goff, tot, eoff, erows = (i32(a) for a in (c8, loff, goff, tot, eoff, erows))

    xs = _dispatch(c8, loff, goff, tot, jnp.sum(erows, keepdims=True), rr, u2, p_rows, tm, lrows, tme)
    ys = _experts(eoff, erows, xs, w_e_gate, w_e_up, w_e_down, tme)
    y = _combine(c8, loff, goff, tot, rwt, h, vec(final_g), ys, tm, lrows)
    return y.reshape(batch, seq, d)
```

```python
import functools
import math

import jax
import jax.numpy as jnp
from jax import lax
from jax.experimental import pallas as pl
from jax.experimental.pallas import tpu as pltpu

F32 = jnp.float32
BF16 = jnp.bfloat16

EPS = 1e-6
HEAD_DIM = 64
HEADS_PER_BLOCK = 2
LANES = 128
SUBLANES = 8
CONV_W = 4
RG_C = 8.0
N_GROUPS = 4
EXPERTS_PER_GROUP = 8
N_EXPERTS = N_GROUPS * EXPERTS_PER_GROUP
ROUTER_ROWS = 48
NEG_BIG = -1e30
LOG2_E = math.log2(math.e)
ATTN_STOP = 104.0 * LOG2_E
ATTN_QUERY_ROWS = 64
ATTN_WINDOW_BLOCKS = 2
ATTN_UNROLL = 12
ATTN_STAGE_LAG = 2
HIGH_HALF = -65536
EXPERT_TILE_ROWS = 1536
EXPERT_ROW_STEP = 256

VMEM_LIMIT = 56 * 1024 * 1024


def _cparams(sem):
    return pltpu.CompilerParams(dimension_semantics=sem, vmem_limit_bytes=VMEM_LIMIT)


def _rms_f32(x, g):
    return x * lax.rsqrt(jnp.mean(x * x, axis=-1, keepdims=True) + EPS) * g


def _in_proj_kernel(x_ref, g_ref, w_ref, q_ref, k_ref, v_ref, xl_ref, gl_ref, w_bf, *,
                    width, q_scale):
    @pl.when(pl.program_id(0) == 0)
    def _():
        for c in range(w_ref.shape[1] // width):
            cols = slice(c * width, (c + 1) * width)
            w_bf[:, cols] = w_ref[:, cols].astype(BF16)

    u = _rms_f32(x_ref[...], g_ref[...]).astype(BF16)
    for c, o_ref in enumerate((q_ref, k_ref, v_ref, xl_ref, gl_ref)):
        p = jnp.dot(u, w_bf[:, c * width:(c + 1) * width], preferred_element_type=F32)
        if c == 0:
            p = p * q_scale
        o_ref[...] = p.astype(o_ref.dtype)


def _in_proj(x2, g, w, width, tm):
    n, d = x2.shape
    row = lambda i: (i, 0)
    out_bf = jax.ShapeDtypeStruct((n, width), BF16)
    out_f = jax.ShapeDtypeStruct((n, width), F32)
    return pl.pallas_call(
        functools.partial(_in_proj_kernel, width=width, q_scale=1.0 / math.sqrt(HEAD_DIM)),
        grid=(n // tm,),
        in_specs=[pl.BlockSpec((tm, d), row),
                  pl.BlockSpec((1, d), lambda i: (0, 0)),
                  pl.BlockSpec(w.shape, lambda i: (0, 0), pipeline_mode=pl.Buffered(1))],
        out_specs=[pl.BlockSpec((tm, width), row)] * 5,
        out_shape=[out_bf, out_bf, out_bf, out_f, out_f],
        scratch_shapes=[pltpu.VMEM(w.shape, BF16)],
        compiler_params=_cparams(("arbitrary",)),
        name="in_proj",
    )(x2, g, w)


def _attn_kernel(q_ref, k_ref, v_ref, o_ref, tri_ref, z_ref, arg_ref, ctot_ref, acc_ref, carry_ref,
                 *, tq, kb, nsub, first):
    seq = q_ref.shape[0]
    win = nsub * kb
    lookback = win - tq
    lane = lax.broadcasted_iota(jnp.int32, (1, LANES), 1)
    rel = (lax.broadcasted_iota(jnp.int32, (tq, kb), 1)
           - lax.broadcasted_iota(jnp.int32, (tq, kb), 0))
    rel = jnp.concatenate([rel] * HEADS_PER_BLOCK, axis=0)

    k_r = lax.broadcasted_iota(jnp.int32, (kb, 2 * kb), 0)
    k_c = lax.broadcasted_iota(jnp.int32, (kb, 2 * kb), 1)
    tri_ref[...] = jnp.where(k_c >= kb, 1.0, jnp.where(k_r > k_c, 1.0, 0.0)).astype(BF16)

    def softplus2(z):
        return jnp.maximum(z, 0.0) + jnp.log2(1.0 + jnp.exp2(-jnp.abs(z)))

    def scores(qh, keys):
        z = LOG2_E * lax.dot_general(qh, keys, (((1,), (1,)), ((), ())),
                                     preferred_element_type=F32)
        nlog_nb = softplus2(z)
        return nlog_nb, z - nlog_nb

    def suffix(nlog_nb):
        r = jnp.dot(nlog_nb.astype(BF16), tri_ref[...], preferred_element_type=F32)
        return r[:, :kb], r[:, kb:]

    def stacked_queries(i):
        q = q_ref[pl.ds(i * tq, tq), :]
        return jnp.concatenate(
            [jnp.where((lane >= h * HEAD_DIM) & (lane < (h + 1) * HEAD_DIM), q, jnp.zeros_like(q))
             for h in range(HEADS_PER_BLOCK)], axis=0)

    def store(i, out):
        o_ref[pl.ds(i * tq, tq), :] = jnp.where(lane < HEAD_DIM, out[0:tq], out[tq:2 * tq])

    def window_start(i):
        if isinstance(i, int):
            return max(i * tq - lookback, 0)
        return pl.multiple_of(i * tq - lookback, tq)

    def stage_scores(i, p):
        keys = k_ref[pl.ds(window_start(i), win), :]
        z_ref[p] = LOG2_E * lax.dot_general(stacked_queries(i), keys, (((1,), (1,)), ((), ())),
                                            preferred_element_type=F32)

    def stage_exponents(p, delta):
        z = z_ref[p]
        softplus = softplus2(z)
        carry = None
        for b in reversed(range(nsub)):
            cols = slice(b * kb, (b + 1) * kb)
            masked = (b + 1) * kb > delta
            valid = (rel + b * kb) < delta
            nl = softplus[:, cols]
            if masked:
                nl = jnp.where(valid, nl, 0.0)
            excl, tot = suffix(nl)
            arg = z[:, cols] - softplus[:, cols] - excl
            if carry is not None:
                arg = arg - carry
            if masked:
                arg = jnp.where(valid, arg, NEG_BIG)
            arg_ref[p, :, cols] = arg
            carry = tot if carry is None else carry + tot
        ctot_ref[p] = carry
        return jnp.min(carry)

    def stage_output(i, p, s):
        vals = v_ref[pl.ds(window_start(i), win), :]
        acc_ref[s] = jnp.dot(jnp.exp2(arg_ref[p]).astype(BF16), vals, preferred_element_type=F32)
        carry_ref[s] = ctot_ref[p]

    def finish(i, s, cmin):
        def cond(state):
            pos, cmin = state
            return (pos > -kb) & (cmin <= ATTN_STOP)

        def older(state):
            pos, _ = state
            start = pl.multiple_of(jnp.maximum(pos, 0), tq)
            keys = k_ref[pl.ds(start, kb), :]
            vals = v_ref[pl.ds(start, kb), :]
            fresh = lax.broadcasted_iota(jnp.int32, (1, kb), 1) < pos + kb - start
            nlog_nb, log_b = scores(stacked_queries(i), keys)
            nlog_nb = jnp.where(fresh, nlog_nb, 0.0)
            excl, tot = suffix(nlog_nb)
            carry = carry_ref[s]
            a = jnp.where(fresh, jnp.exp2(log_b - excl - carry), 0.0)
            acc_ref[s] += jnp.dot(a.astype(BF16), vals, preferred_element_type=F32)
            carry_ref[s] = carry + tot
            return pos - kb, jnp.min(carry + tot)

        lax.while_loop(cond, older, (jnp.asarray(window_start(i) - kb, jnp.int32), cmin))
        store(i, acc_ref[s])

    for i in range(first):
        stage_scores(i, 0)
        cmin = stage_exponents(0, i * tq - window_start(i))
        stage_output(i, 0, 0)
        finish(i, 0, cmin)

    n_pipe = seq // tq - first
    unroll = acc_ref.shape[0]
    lag = z_ref.shape[0] - 1
    n_slots = lag + 1
    pending = []
    for tau in range(2 * lag):
        stage_scores(first + tau, tau % n_slots)
        if tau >= lag:
            pending.append(stage_exponents((tau - lag) % n_slots, lookback))

    def steady(m, pending):
        pending = list(pending)
        tau0 = 2 * lag + unroll * m
        done = []
        for u in range(unroll):
            stage_output(first + tau0 + u - 2 * lag, u % n_slots, u)
            done.append((first + tau0 + u - 2 * lag, u, pending.pop(0)))
            pending.append(stage_exponents((lag + u) % n_slots, lookback))
            stage_scores(first + tau0 + u, (2 * lag + u) % n_slots)
        for block, s, cmin in done:
            finish(block, s, cmin)
        return tuple(pending)

    pending = list(lax.fori_loop(0, (n_pipe - 2 * lag) // unroll, steady, tuple(pending)))
    for tau in range(n_pipe, n_pipe + 2 * lag):
        j = tau - 2 * lag
        stage_output(first + j, j % n_slots, 0)
        cmin = pending.pop(0)
        if tau - lag < n_pipe:
            pending.append(stage_exponents((tau - lag) % n_slots, lookback))
        finish(first + j, 0, cmin)


def _attention(q, k, v, batch, seq):
    n, width = q.shape
    tq, kb, nsub = ATTN_QUERY_ROWS, LANES, ATTN_WINDOW_BLOCKS
    n_blocks = seq // tq
    lag, n_slots = ATTN_STAGE_LAG, ATTN_STAGE_LAG + 1
    clipped = -(-(nsub * kb - tq) // tq)
    unroll = first = None
    for u in range(ATTN_UNROLL - ATTN_UNROLL % n_slots, 0, -n_slots):
        for f in range(clipped, clipped + u):
            rest = n_blocks - f - 2 * lag
            if rest >= 0 and rest % u == 0 and first is None:
                unroll, first = u, f
    assert first is not None and seq % tq == 0 and seq >= nsub * kb
    blk = pl.BlockSpec((seq, LANES), lambda b, hp: (b, hp))
    rows = HEADS_PER_BLOCK * tq
    stage_buf = pltpu.VMEM((n_slots, rows, nsub * kb), F32)
    carry_buf = pltpu.VMEM((n_slots, rows, kb), F32)
    row_buf = pltpu.VMEM((unroll, rows, LANES), F32)
    return pl.pallas_call(
        functools.partial(_attn_kernel, tq=tq, kb=kb, nsub=nsub, first=first),
        grid=(batch, width // LANES),
        in_specs=[blk, blk, blk],
        out_specs=blk,
        out_shape=jax.ShapeDtypeStruct((n, width), F32),
        scratch_shapes=[pltpu.VMEM((kb, 2 * kb), BF16),
                        stage_buf, stage_buf, carry_buf, row_buf, row_buf],
        compiler_params=_cparams(("arbitrary", "arbitrary")),
        name="attn",
    )(q, k, v)


def _gelu_tanh(x):
    c = math.sqrt(2.0 / math.pi)
    half_x = 0.5 * x
    return half_x + half_x * jnp.tanh(x * (c + (c * 0.044715) * (x * x)))


def _sigmoid(x):
    return 0.5 + 0.5 * jnp.tanh(0.5 * x)


def _lru_kernel(*refs, ts, n_slab):
    xl_refs, gl_refs = refs[0:n_slab], refs[n_slab:2 * n_slab]
    cw_ref, cb_ref, wr_ref, br_ref, wi_ref, bi_ref, lam_ref = refs[2 * n_slab:2 * n_slab + 7]
    o_refs = refs[2 * n_slab + 7:3 * n_slab + 7]
    tail_ref, a7_ref, u7_ref, hp_ref, pa_ref, pu_ref, h_ref = refs[3 * n_slab + 7:]
    t = pl.program_id(1)
    groups = ts // SUBLANES

    @pl.when(t == 0)
    def _():
        tail_ref[...] = jnp.zeros_like(tail_ref)
        h_ref[...] = jnp.zeros_like(h_ref)

    first_group = lax.broadcasted_iota(jnp.int32, (groups, LANES), 0) == 0
    for c in range(n_slab):
        lanes = slice(c * LANES, (c + 1) * LANES)
        x = [xl_refs[c][pl.ds(s, groups, stride=SUBLANES), :] for s in range(SUBLANES)]
        shifted = {}
        for s in range(SUBLANES - (CONV_W - 1), SUBLANES):
            shifted[s] = jnp.where(first_group, tail_ref[c, s:s + 1, :], pltpu.roll(x[s], 1, axis=0))
            tail_ref[c, s:s + 1, :] = x[s][groups - 1:groups, :]
        conv = []
        for s in range(SUBLANES):
            y = cb_ref[:, lanes]
            for w in range(CONV_W):
                j = s - (CONV_W - 1) + w
                y = y + (x[j] if j >= 0 else shifted[j + SUBLANES]) * cw_ref[w:w + 1, lanes]
            conv.append(y)
        xc = jnp.concatenate(conv, axis=0)

        xcb = xc.astype(BF16)
        r = _sigmoid(jnp.dot(xcb, wr_ref[c], preferred_element_type=F32) + br_ref[:, lanes])
        ig = _sigmoid(jnp.dot(xcb, wi_ref[c], preferred_element_type=F32) + bi_ref[:, lanes])
        lam = lam_ref[:, lanes]
        log_sig_lam = -(jnp.maximum(-lam, 0.0) + jnp.log1p(jnp.exp(-jnp.abs(lam))))
        log_a = RG_C * r * log_sig_lam
        a = jnp.exp(log_a)
        th = jnp.tanh(log_a)
        one_m_a2 = -2.0 * th / (1.0 - th)
        root = jnp.where(one_m_a2 > 0.0, one_m_a2 * lax.rsqrt(one_m_a2), 0.0)
        u = root * (ig * xc)

        a_run = u_run = None
        for s in range(SUBLANES):
            rows = slice(s * groups, (s + 1) * groups)
            if s == 0:
                a_run, u_run = a[rows], u[rows]
            else:
                u_run = a[rows] * u_run + u[rows]
                a_run = a_run * a[rows]
            pa_ref[c, rows, :] = a_run
            pu_ref[c, rows, :] = u_run
        a7_ref[c] = a_run
        u7_ref[c] = u_run

    def group(g, hs):
        nxt = []
        for c in range(n_slab):
            hp_ref[c, pl.ds(g, 1), :] = hs[c]
            nxt.append(a7_ref[c, pl.ds(g, 1), :] * hs[c] + u7_ref[c, pl.ds(g, 1), :])
        return tuple(nxt)

    hs = lax.fori_loop(0, groups, group, tuple(h_ref[c] for c in range(n_slab)), unroll=8)
    for c in range(n_slab):
        h_ref[c] = hs[c]

    for c in range(n_slab):
        h_in = hp_ref[c]
        for s in range(SUBLANES):
            rows = slice(s * groups, (s + 1) * groups)
            hseq = pu_ref[c, rows, :] + pa_ref[c, rows, :] * h_in
            gate = _gelu_tanh(gl_refs[c][pl.ds(s, groups, stride=SUBLANES), :])
            o_refs[c][pl.ds(s, groups, stride=SUBLANES), :] = hseq * gate


def _lru(xl, gl, conv_w, conv_b, wr_bd, br, wi_bd, bi, lam, batch, seq, ts):
    n, width = xl.shape
    nt = seq // ts
    n_slab = width // LANES
    assert wr_bd.shape == (n_slab, LANES, LANES) and ts % (SUBLANES * SUBLANES) == 0
    groups = ts // SUBLANES
    slab = [pl.BlockSpec((ts, LANES), functools.partial(lambda b, t, c: (b * nt + t, c), c=c))
            for c in range(n_slab)]
    const2 = lambda b, t: (0, 0)
    const3 = lambda b, t: (0, 0, 0)
    vec = pl.BlockSpec((1, width), const2)
    per_group = pltpu.VMEM((n_slab, groups, LANES), F32)
    per_step = pltpu.VMEM((n_slab, ts, LANES), F32)
    return pl.pallas_call(
        functools.partial(_lru_kernel, ts=ts, n_slab=n_slab),
        grid=(batch, nt),
        in_specs=slab + slab + [pl.BlockSpec((CONV_W, width), const2), vec,
                                pl.BlockSpec(wr_bd.shape, const3), vec,
                                pl.BlockSpec(wi_bd.shape, const3), vec, vec],
        out_specs=[pl.BlockSpec((ts, LANES), lambda b, t: (b * nt + t, 0))] * n_slab,
        out_shape=[jax.ShapeDtypeStruct((n, LANES), F32)] * n_slab,
        scratch_shapes=[pltpu.VMEM((n_slab, SUBLANES, LANES), F32),
                        per_group, per_group, per_group, per_step, per_step,
                        pltpu.VMEM((n_slab, 1, LANES), F32)],
        compiler_params=_cparams(("arbitrary", "arbitrary")),
        name="lru",
    )(*([xl] * n_slab), *([gl] * n_slab), conv_w, conv_b, wr_bd, br, wi_bd, bi, lam)


def _mix_route_kernel(sb_ref, *refs, tm, n_slab):
    lru_refs = refs[:n_slab]
    (x_ref, sbg_ref, lrug_ref, wo_ref, n2g_ref, wrs_ref, rb_ref,
     h_ref, u2_ref, rr_ref, rwt_ref, tc_ref, before_ref, u2s_ref, wo_bf) = refs[n_slab:]

    @pl.when(pl.program_id(0) == 0)
    def _():
        wo_bf[...] = wo_ref[...].astype(BF16)

    _mix_route_body(sb_ref, lru_refs, x_ref, sbg_ref, lrug_ref, wo_bf, n2g_ref, wrs_ref, rb_ref,
                    h_ref, u2_ref, rr_ref, rwt_ref, tc_ref, before_ref, u2s_ref, tm)


def _mix_route_body(sb_ref, lru_refs, x_ref, sbg_ref, lrug_ref, wo_ref, n2g_ref, wrs_ref, rb_ref,
                    h_ref, u2_ref, rr_ref, rwt_ref, tc_ref, before_ref, u2s_ref, tm):
    step = pl.program_id(0)
    half = sb_ref.shape[1]
    n_sub = x_ref.shape[0] // tm

    @pl.when(step == 0)
    def _():
        r_id = lax.broadcasted_iota(jnp.int32, (tm, tm), 0)
        c_id = lax.broadcasted_iota(jnp.int32, (tm, tm), 1)
        before_ref[...] = (r_id < c_id).astype(BF16)
        tc_ref[...] = jnp.zeros_like(tc_ref)
        u2s_ref[...] = jnp.zeros_like(u2s_ref)

    def route(j):
        rows = slice(j * tm, (j + 1) * tm)
        u2 = u2s_ref[rows, :]
        u_hi = u2.astype(BF16)
        u_lo = (u2 - u_hi.astype(F32)).astype(BF16)
        nt_dims = (((1,), (1,)), ((), ()))
        n_rows = rb_ref.shape[0]
        both = lax.dot_general(wrs_ref[...], u_hi, nt_dims, preferred_element_type=F32)
        lt = (both[0:n_rows] + both[n_rows:2 * n_rows]
              + lax.dot_general(wrs_ref[0:n_rows, :], u_lo, nt_dims, preferred_element_type=F32)
              + rb_ref[...])

        sub = lax.broadcasted_iota(jnp.int32, (SUBLANES, tm), 0)

        def top1(x):
            m = jnp.max(x, axis=0, keepdims=True)
            idx = jnp.min(jnp.where(x == m, sub, SUBLANES), axis=0, keepdims=True)
            return m, idx

        grp = lt[0:SUBLANES, :]
        g_max, g_idx = top1(grp)
        g_p = 1.0 / jnp.sum(jnp.exp(grp - g_max), axis=0, keepdims=True)
        fine = lt[SUBLANES:2 * SUBLANES, :]
        for g in range(1, N_GROUPS):
            fine = jnp.where(g_idx == g, lt[(g + 1) * SUBLANES:(g + 2) * SUBLANES, :], fine)
        m1, i1 = top1(fine)
        m2, i2 = top1(jnp.where(sub == i1, -jnp.inf, fine))
        e2 = jnp.exp(m2 - m1)
        p1 = 1.0 / (1.0 + e2)
        w1 = g_p * p1
        w2 = g_p * (e2 * p1)
        x1 = g_idx * EXPERTS_PER_GROUP + i1
        x2 = g_idx * EXPERTS_PER_GROUP + i2

        eid = lax.broadcasted_iota(jnp.int32, (N_EXPERTS, tm), 0)
        oh1 = jnp.where(eid == x1, 1.0, 0.0)
        oh2 = jnp.where(eid == x2, 1.0, 0.0)
        pre1 = jnp.dot(oh1.astype(BF16), before_ref[...], preferred_element_type=F32)
        pre2 = jnp.dot(oh2.astype(BF16), before_ref[...], preferred_element_type=F32)
        cnt1 = jnp.sum(oh1, axis=1, keepdims=True)
        cnt2 = jnp.sum(oh2, axis=1, keepdims=True)
        seg8 = jnp.floor((cnt1 + cnt2 + (SUBLANES - 1.0)) * (1.0 / SUBLANES))
        e_r = lax.broadcasted_iota(jnp.int32, (N_EXPERTS, N_EXPERTS), 0)
        e_c = lax.broadcasted_iota(jnp.int32, (N_EXPERTS, N_EXPERTS), 1)
        lower = jnp.where(e_c < e_r, 1.0, 0.0).astype(BF16)
        seg8_b = jnp.broadcast_to(seg8, (N_EXPERTS, LANES)).astype(BF16)
        seg_off = SUBLANES * jnp.dot(lower, seg8_b, preferred_element_type=F32)[:, 0:1]
        pos1 = jnp.sum(oh1 * (pre1 + seg_off), axis=0, keepdims=True)
        pos2 = jnp.sum(oh2 * (pre2 + (seg_off + cnt1)), axis=0, keepdims=True)

        lane = lax.broadcasted_iota(jnp.int32, tc_ref.shape, 1)
        seg_rows = jnp.broadcast_to(seg8 * SUBLANES, tc_ref.shape).astype(jnp.int32)
        tile = (step - 1) * n_sub + j
        tc_ref[...] = jnp.where(lane == tile, seg_rows, tc_ref[...])

        zrow = jnp.zeros((SUBLANES - 4, tm), jnp.int32)
        rr_ref[:, rows] = jnp.concatenate(
            [pos1.astype(jnp.int32), pos2.astype(jnp.int32), x1, x2, zrow], axis=0)
        wt = jnp.concatenate([w1, w2, pos1, pos2, jnp.zeros((LANES - 4, tm), F32)], axis=0)
        rwt_ref[rows, :] = wt.T

    for j in range(n_sub):
        route(j)

    m_sb = _rms_f32(sb_ref[...], sbg_ref[...]).astype(BF16)
    lru = jnp.concatenate([r[...] for r in lru_refs], axis=-1)
    m_lru = _rms_f32(lru, lrug_ref[...]).astype(BF16)
    h = (x_ref[...]
         + jnp.dot(m_sb, wo_ref[0:half, :], preferred_element_type=F32)
         + jnp.dot(m_lru, wo_ref[half:2 * half, :], preferred_element_type=F32))
    h_ref[...] = h
    u2_next = _rms_f32(h, n2g_ref[...])
    u2_ref[...] = u2_next.astype(BF16)
    u2s_ref[...] = u2_next


def _mix_route(sb, lru, x2, sbg, lrug, w_out, n2g, wr_stack, rbias, tm, tp):
    n, d = x2.shape
    half = sb.shape[1]
    n_tiles = n // tp
    row = lambda i: (jnp.minimum(i, n_tiles - 1), 0)
    routed = lambda i: (jnp.maximum(i - 1, 0), 0)
    const = lambda i: (0, 0)
    return pl.pallas_call(
        functools.partial(_mix_route_kernel, tm=tm, n_slab=len(lru)),
        grid=(n_tiles + 1,),
        in_specs=[pl.BlockSpec((tp, half), row)] + [pl.BlockSpec((tp, LANES), row)] * len(lru)
                 + [pl.BlockSpec((tp, d), row),
                  pl.BlockSpec((1, half), const), pl.BlockSpec((1, half), const),
                  pl.BlockSpec(w_out.shape, const, pipeline_mode=pl.Buffered(1)),
                  pl.BlockSpec((1, d), const),
                  pl.BlockSpec(wr_stack.shape, const), pl.BlockSpec(rbias.shape, const)],
        out_specs=[pl.BlockSpec((tp, d), row), pl.BlockSpec((tp, d), row),
                   pl.BlockSpec((SUBLANES, tp), lambda i: (0, jnp.maximum(i - 1, 0))),
                   pl.BlockSpec((tp, LANES), routed),
                   pl.BlockSpec((N_EXPERTS, LANES), const)],
        out_shape=[jax.ShapeDtypeStruct((n, d), F32), jax.ShapeDtypeStruct((n, d), BF16),
                   jax.ShapeDtypeStruct((SUBLANES, n), jnp.int32),
                   jax.ShapeDtypeStruct((n, LANES), F32),
                   jax.ShapeDtypeStruct((N_EXPERTS, LANES), jnp.int32)],
        scratch_shapes=[pltpu.VMEM((tm, tm), BF16), pltpu.VMEM((tp, d), F32),
                        pltpu.VMEM(w_out.shape, BF16)],
        compiler_params=_cparams(("arbitrary",)),
        name="mix_route",
    )(sb, *lru, x2, sbg, lrug, w_out, n2g, wr_stack, rbias)


def _pack_halves(x):
    half = x.shape[1] // 2
    lo = lax.shift_right_logical(lax.bitcast_convert_type(x[:, :half], jnp.int32), 16)
    hi = lax.bitcast_convert_type(x[:, half:], jnp.int32) & HIGH_HALF
    return hi | lo


def _unpack_halves(p):
    lo = lax.bitcast_convert_type(lax.shift_left(p, 16), F32)
    hi = lax.bitcast_convert_type(p & HIGH_HALF, F32)
    return lo.astype(BF16), hi.astype(BF16)


def _segment_copies(tile, c8_ref, loff_ref, goff_ref, make):
    for e in range(N_EXPERTS):
        idx = tile * N_EXPERTS + e
        rows = pl.multiple_of(c8_ref[idx], SUBLANES)

        @pl.when(rows > 0)
        def _(idx=idx, rows=rows):
            lo = pl.multiple_of(loff_ref[idx], SUBLANES)
            go = pl.multiple_of(goff_ref[idx], SUBLANES)
            make(pl.ds(lo, rows), pl.ds(go, rows)).start()


def _dispatch_kernel(c8_ref, loff_ref, goff_ref, tot_ref, used_ref,
                     rr_ref, u2_ref, xs_ref, lbuf, zbuf, sem, zsem, *, td, lrows):
    i = pl.program_id(0)
    slot = i % 2

    r_id = lax.broadcasted_iota(jnp.int32, (lrows, td), 0)
    perm = jnp.where(r_id == rr_ref[0:1, :], 1.0, jnp.where(r_id == rr_ref[1:2, :], 1.0, 0.0))
    sorted_rows = jnp.dot(perm.astype(BF16), u2_ref[...], preferred_element_type=F32)
    lbuf[slot] = _pack_halves(sorted_rows)

    @pl.when(i == 0)
    def _():
        zbuf[...] = jnp.zeros_like(zbuf)
        chunk = zbuf.shape[0]
        used = used_ref[0]
        spare = xs_ref.shape[0] - used
        n_fill = (spare + chunk - 1) // chunk

        def fill_copy(k):
            rows = pl.multiple_of(jnp.minimum(chunk, spare - k * chunk), SUBLANES)
            start = pl.multiple_of(used + k * chunk, SUBLANES)
            return pltpu.make_async_copy(zbuf.at[pl.ds(0, rows)], xs_ref.at[pl.ds(start, rows)],
                                         zsem)

        def fill_start(k, c):
            fill_copy(k).start()
            return c

        def fill_wait(k, c):
            fill_copy(k).wait()
            return c

        lax.fori_loop(0, n_fill, fill_start, 0)
        lax.fori_loop(0, n_fill, fill_wait, 0)

    def wait_tile(tile, s):
        rows = pl.multiple_of(tot_ref[tile], SUBLANES)
        pltpu.make_async_copy(lbuf.at[s, pl.ds(0, rows)], xs_ref.at[pl.ds(0, rows)], sem).wait()

    @pl.when(i > 0)
    def _():
        wait_tile(i - 1, 1 - slot)

    _segment_copies(i, c8_ref, loff_ref, goff_ref,
                    lambda loc, glob: pltpu.make_async_copy(lbuf.at[slot, loc], xs_ref.at[glob], sem))

    @pl.when(i == pl.num_programs(0) - 1)
    def _():
        wait_tile(i, slot)


def _dispatch(c8, loff, goff, tot, used, rr, u2, p_rows, td, lrows, tme):
    n, d = u2.shape
    pmap = lambda i, *_: (0, i)
    return pl.pallas_call(
        functools.partial(_dispatch_kernel, td=td, lrows=lrows),
        grid_spec=pltpu.PrefetchScalarGridSpec(
            num_scalar_prefetch=5,
            grid=(n // td,),
            in_specs=[pl.BlockSpec((SUBLANES, td), pmap),
                      pl.BlockSpec((td, d), lambda i, *_: (i, 0))],
            out_specs=pl.BlockSpec(memory_space=pl.ANY),
            scratch_shapes=[pltpu.VMEM((2, lrows, d // 2), jnp.int32),
                            pltpu.VMEM((tme, d // 2), jnp.int32),
                            pltpu.SemaphoreType.DMA(()), pltpu.SemaphoreType.DMA(())]),
        out_shape=jax.ShapeDtypeStruct((p_rows, d // 2), jnp.int32),
        compiler_params=_cparams(("arbitrary",)),
        name="dispatch",
    )(c8, loff, goff, tot, used, rr, u2)


def _experts_kernel(eoff_ref, erows_ref, xs_ref, wg_ref, wu_ref, wd_ref, ys_ref,
                    wg_bf, wu_bf, wd_bf, xbuf, ybuf, sem_in, sem_out, state, *, tme):
    e = pl.program_id(0)
    n_experts = pl.num_programs(0)
    rows = erows_ref[e]
    off = eoff_ref[e]
    n_tiles = (rows + tme - 1) // tme
    nxt = jnp.minimum(e + 1, n_experts - 1)
    prefetch = (rows > 0) & (e + 1 < n_experts) & (erows_ref[nxt] > 0)

    def tile_rows(total, k):
        return pl.multiple_of(jnp.minimum(tme, total - k * tme), SUBLANES)

    def in_copy(start, r, slot):
        start = pl.multiple_of(start, SUBLANES)
        return pltpu.make_async_copy(xs_ref.at[pl.ds(start, r)], xbuf.at[slot, pl.ds(0, r)],
                                     sem_in.at[slot])

    def out_copy(start, r, slot):
        start = pl.multiple_of(start, SUBLANES)
        return pltpu.make_async_copy(ybuf.at[slot, pl.ds(0, r)], ys_ref.at[pl.ds(start, r)],
                                     sem_out.at[slot])

    @pl.when(e == 0)
    def _():
        for s in range(4):
            state[s] = 0
        xbuf[...] = jnp.zeros_like(xbuf)

    @pl.when(rows > 0)
    def _():
        wg_bf[...] = wg_ref[0].astype(BF16)
        wu_bf[...] = wu_ref[0].astype(BF16)
        wd_bf[...] = wd_ref[0].astype(BF16)
        done = state[0]

        @pl.when(state[1] == 0)
        def _():
            in_copy(off, tile_rows(rows, 0), done % 2).start()

        def tile(k, c):
            slot = (done + k) % 2
            r = tile_rows(rows, k)

            @pl.when(k + 1 < n_tiles)
            def _():
                in_copy(off + (k + 1) * tme, tile_rows(rows, k + 1), 1 - slot).start()

            @pl.when((k + 1 == n_tiles) & prefetch)
            def _():
                in_copy(eoff_ref[nxt], tile_rows(erows_ref[nxt], 0), 1 - slot).start()

            in_copy(off, r, slot).wait()

            @pl.when(done + k >= 2)
            def _():
                out_copy(0, pl.multiple_of(state[2 + slot], SUBLANES), slot).wait()

            def mlp(start, n):
                x = jnp.concatenate(_unpack_halves(xbuf[slot, start:start + n]), axis=1)
                hg = jnp.dot(x, wg_bf[...], preferred_element_type=F32)
                hu = jnp.dot(x, wu_bf[...], preferred_element_type=F32)
                act = (hg * jax.nn.sigmoid(hg) * hu).astype(BF16)
                y = jnp.dot(act, wd_bf[...], preferred_element_type=F32)
                ybuf[slot, start:start + n] = _pack_halves(y.astype(BF16).astype(F32))

            for n in range(EXPERT_ROW_STEP, tme + 1, EXPERT_ROW_STEP):
                @pl.when((r > n - EXPERT_ROW_STEP) & (r <= n))
                def _(n=n):
                    mlp(0, n)

            out_copy(off + k * tme, r, slot).start()
            state[2 + slot] = r
            return c

        lax.fori_loop(0, n_tiles, tile, 0)
        state[0] = done + n_tiles

    state[1] = prefetch.astype(jnp.int32)

    @pl.when(e == n_experts - 1)
    def _():
        total = eoff_ref[e] + rows
        for slot in range(2):
            @pl.when(state[0] > slot)
            def _(slot=slot):
                out_copy(0, pl.multiple_of(state[2 + slot], SUBLANES), slot).wait()
        ybuf[0] = jnp.zeros(ybuf.shape[1:], ybuf.dtype)
        spare = ys_ref.shape[0] - total
        n_fill = (spare + tme - 1) // tme

        def fill_copy(k):
            return out_copy(total + k * tme, tile_rows(spare, k), 0)

        def fill_start(k, c):
            fill_copy(k).start()
            return c

        def fill_wait(k, c):
            fill_copy(k).wait()
            return c

        lax.fori_loop(0, n_fill, fill_start, 0)
        lax.fori_loop(0, n_fill, fill_wait, 0)


def _experts(eoff, erows, xs, wg, wu, wd, tme):
    p = xs.shape[0]
    n_experts, d, de = wg.shape
    wmap = lambda e, *_: (e, 0, 0)
    return pl.pallas_call(
        functools.partial(_experts_kernel, tme=tme),
        grid_spec=pltpu.PrefetchScalarGridSpec(
            num_scalar_prefetch=2,
            grid=(n_experts,),
            in_specs=[pl.BlockSpec(memory_space=pl.ANY),
                      pl.BlockSpec((1, d, de), wmap), pl.BlockSpec((1, d, de), wmap),
                      pl.BlockSpec((1, de, d), wmap)],
            out_specs=pl.BlockSpec(memory_space=pl.ANY),
            scratch_shapes=[pltpu.VMEM((d, de), BF16), pltpu.VMEM((d, de), BF16),
                            pltpu.VMEM((de, d), BF16),
                            pltpu.VMEM((2, tme, d // 2), jnp.int32),
                            pltpu.VMEM((2, tme, d // 2), jnp.int32),
                            pltpu.SemaphoreType.DMA((2,)), pltpu.SemaphoreType.DMA((2,)),
                            pltpu.SMEM((4,), jnp.int32)]),
        out_shape=jax.ShapeDtypeStruct((p, d // 2), jnp.int32),
        compiler_params=_cparams(("arbitrary",)),
        name="experts",
    )(eoff, erows, xs, wg, wu, wd)


def _combine_kernel(c8_ref, loff_ref, goff_ref, tot_ref,
                    rwt_ref, h_ref, fg_ref, ys_ref, y_ref, ybuf, sems, *, tc, lrows):
    i = pl.program_id(0)
    slot = i % 2

    def gather_tile(tile, s):
        _segment_copies(tile, c8_ref, loff_ref, goff_ref,
                        lambda loc, glob: pltpu.make_async_copy(ys_ref.at[glob], ybuf.at[s, loc],
                                                                sems.at[s]))

    @pl.when(i == 0)
    def _():
        ybuf[...] = jnp.zeros_like(ybuf)
        gather_tile(0, 0)

    @pl.when(i + 1 < pl.num_programs(0))
    def _():
        gather_tile(i + 1, 1 - slot)

    rows = pl.multiple_of(tot_ref[i], SUBLANES)
    pltpu.make_async_copy(ys_ref.at[pl.ds(0, rows)], ybuf.at[slot, pl.ds(0, rows)],
                          sems.at[slot]).wait()

    w = rwt_ref[...]
    c_id = lax.broadcasted_iota(jnp.int32, (tc, lrows), 1)
    pos1 = w[:, 2:3].astype(jnp.int32)
    pos2 = w[:, 3:4].astype(jnp.int32)
    wmat = (jnp.where(c_id == pos1, w[:, 0:1], 0.0)
            + jnp.where(c_id == pos2, w[:, 1:2], 0.0)).astype(BF16)
    y_lo, y_hi = _unpack_halves(ybuf[slot])
    moe = jnp.concatenate([jnp.dot(wmat, y_lo, preferred_element_type=F32),
                           jnp.dot(wmat, y_hi, preferred_element_type=F32)], axis=-1)
    y_ref[...] = _rms_f32(h_ref[...] + moe, fg_ref[...])


def _combine(c8, loff, goff, tot, rwt, h, final_g, ys, tc, lrows):
    n, d = h.shape
    return pl.pallas_call(
        functools.partial(_combine_kernel, tc=tc, lrows=lrows),
        grid_spec=pltpu.PrefetchScalarGridSpec(
            num_scalar_prefetch=4,
            grid=(n // tc,),
            in_specs=[pl.BlockSpec((tc, LANES), lambda i, *_: (i, 0)),
                      pl.BlockSpec((tc, d), lambda i, *_: (i, 0)),
                      pl.BlockSpec((1, d), lambda i, *_: (0, 0)),
                      pl.BlockSpec(memory_space=pl.ANY)],
            out_specs=pl.BlockSpec((tc, d), lambda i, *_: (i, 0)),
            scratch_shapes=[pltpu.VMEM((2, lrows, d // 2), jnp.int32),
                            pltpu.SemaphoreType.DMA((2,))]),
        out_shape=jax.ShapeDtypeStruct((n, d), F32),
        compiler_params=_cparams(("arbitrary",)),
        name="combine",
    )(c8, loff, goff, tot, rwt, h, final_g, ys)


def _block_diag(w, per):
    nb, c, _ = w.shape
    eye = jnp.eye(per, dtype=w.dtype)
    wg = w.reshape(nb // per, per, c, c)
    return jnp.einsum("gpij,pq->gpiqj", wg, eye).reshape(nb // per, per * c, per * c)


def _router_tables(w_group, b_group, w_fine, b_fine):
    d = w_group.shape[0]
    w = jnp.zeros((ROUTER_ROWS, d), F32)
    w = w.at[0:N_GROUPS].set(w_group.T).at[SUBLANES:SUBLANES + N_EXPERTS].set(w_fine.T)
    b = jnp.full((ROUTER_ROWS,), NEG_BIG, F32)
    b = b.at[0:N_GROUPS].set(b_group).at[SUBLANES:SUBLANES + N_EXPERTS].set(b_fine)
    w_hi = w.astype(BF16)
    w_lo = (w - w_hi.astype(F32)).astype(BF16)
    return jnp.concatenate([w_hi, w_lo], axis=0), b.reshape(ROUTER_ROWS, 1)


def kernel(x, norm1_g, w_in, conv_w, conv_b, w_rgate, b_rgate, w_igate, b_igate, lam, sb_norm_g,
           lru_norm_g, w_out, norm2_g, w_group, b_group, w_fine, b_fine, w_e_gate, w_e_up,
           w_e_down, final_g):
    batch, seq, d = x.shape
    n = batch * seq
    width = w_in.shape[1] // 5
    tm = min(512, seq)
    tme = EXPERT_TILE_ROWS
    gate_per = LANES // w_rgate.shape[1]

    x2 = x.reshape(n, d)
    vec = lambda a: a.reshape(1, -1)

    q, k, v, xl, gl = _in_proj(x2, vec(norm1_g), w_in, width, min(2 * tm, seq))
    out_sb = _attention(q, k, v, batch, seq)
    out_lru = _lru(xl, gl, conv_w, vec(conv_b),
                   _block_diag(w_rgate, gate_per).astype(BF16), vec(b_rgate),
                   _block_diag(w_igate, gate_per).astype(BF16), vec(b_igate),
                   vec(lam), batch, seq, min(1024, seq))

    wr_stack, rbias = _router_tables(w_group, b_group, w_fine, b_fine)
    h, u2, rr, rwt, tcnt = _mix_route(out_sb, out_lru, x2, vec(sb_norm_g), vec(lru_norm_g),
                                      w_out, vec(norm2_g), wr_stack, rbias, tm, min(2 * tm, seq))

    n_tiles = n // tm
    assert n_tiles <= LANES, "one lane of the per-tile count table per token tile"
    c8 = tcnt[:, :n_tiles].T
    erows = jnp.sum(c8, axis=0)
    eoff = jnp.cumsum(erows) - erows
    goff = eoff[None, :] + jnp.cumsum(c8, axis=0) - c8
    loff = jnp.cumsum(c8, axis=1) - c8
    tot = jnp.sum(c8, axis=1)
    lrows = 2 * tm + N_EXPERTS * SUBLANES
    p_rows = 2 * n + n_tiles * N_EXPERTS * (SUBLANES - 1)
    p_rows = -(-p_rows // SUBLANES) * SUBLANES
    i32 = lambda a: a.reshape(-1).astype(jnp.int32)
    c8, loff, goff, tot, eoff, erows = (i32(a) for a in (c8, loff, goff, tot, eoff, erows))

    xs = _dispatch(c8, loff, goff, tot, jnp.sum(erows, keepdims=True), rr, u2, p_rows, tm, lrows, tme)
    ys = _experts(eoff, erows, xs, w_e_gate, w_e_up, w_e_down, tme)
    y = _combine(c8, loff, goff, tot, rwt, h, vec(final_g), ys, tm, lrows)
    return y.reshape(batch, seq, d)
```

```python
import functools
import math

import jax
import jax.numpy as jnp
from jax import lax
from jax.experimental import pallas as pl
from jax.experimental.pallas import tpu as pltpu

F32 = jnp.float32
BF16 = jnp.bfloat16

EPS = 1e-6
HEAD_DIM = 64
HEADS_PER_BLOCK = 2
LANES = 128
SUBLANES = 8
CONV_W = 4
RG_C = 8.0
N_GROUPS = 4
EXPERTS_PER_GROUP = 8
N_EXPERTS = N_GROUPS * EXPERTS_PER_GROUP
ROUTER_ROWS = 48
NEG_BIG = -1e30
LOG2_E = math.log2(math.e)
ATTN_STOP = 104.0 * LOG2_E
ATTN_QUERY_ROWS = 64
ATTN_WINDOW_BLOCKS = 2
ATTN_UNROLL = 24
ATTN_STAGE_LAG = 2
HIGH_HALF = -65536
EXPERT_TILE_ROWS = 1536
EXPERT_ROW_STEP = 256
TOKEN_TILE = 512
ROW_TILE = 1024

VMEM_LIMIT = 56 * 1024 * 1024


def _cparams(sem):
    return pltpu.CompilerParams(dimension_semantics=sem, vmem_limit_bytes=VMEM_LIMIT)


def _rms_f32(x, g):
    return x * lax.rsqrt(jnp.mean(x * x, axis=-1, keepdims=True) + EPS) * g


def _in_proj_kernel(x_ref, g_ref, w_ref, q_ref, k_ref, v_ref, xl_ref, gl_ref, w_bf, *,
                    width, q_scale):
    @pl.when(pl.program_id(0) == 0)
    def _():
        for c in range(w_ref.shape[1] // width):
            cols = slice(c * width, (c + 1) * width)
            w_bf[:, cols] = w_ref[:, cols].astype(BF16)

    u = _rms_f32(x_ref[...], g_ref[...]).astype(BF16)
    for c, o_ref in enumerate((q_ref, k_ref, v_ref, xl_ref, gl_ref)):
        p = jnp.dot(u, w_bf[:, c * width:(c + 1) * width], preferred_element_type=F32)
        if c == 0:
            p = p * q_scale
        o_ref[...] = p.astype(o_ref.dtype)


def _in_proj(x2, g, w, width, tm):
    n, d = x2.shape
    row = lambda i: (i, 0)
    out_bf = jax.ShapeDtypeStruct((n, width), BF16)
    out_f = jax.ShapeDtypeStruct((n, width), F32)
    return pl.pallas_call(
        functools.partial(_in_proj_kernel, width=width, q_scale=1.0 / math.sqrt(HEAD_DIM)),
        grid=(n // tm,),
        in_specs=[pl.BlockSpec((tm, d), row),
                  pl.BlockSpec((1, d), lambda i: (0, 0)),
                  pl.BlockSpec(w.shape, lambda i: (0, 0), pipeline_mode=pl.Buffered(1))],
        out_specs=[pl.BlockSpec((tm, width), row)] * 5,
        out_shape=[out_bf, out_bf, out_bf, out_f, out_f],
        scratch_shapes=[pltpu.VMEM(w.shape, BF16)],
        compiler_params=_cparams(("arbitrary",)),
        name="in_proj",
    )(x2, g, w)


def _attn_kernel(q_ref, k_ref, v_ref, o_ref, tri_ref, z_ref, arg_ref, ctot_ref, acc_ref, carry_ref,
                 *, tq, kb, nsub, first):
    seq = q_ref.shape[0]
    win = nsub * kb
    lookback = win - tq
    lane = lax.broadcasted_iota(jnp.int32, (1, LANES), 1)
    rel = (lax.broadcasted_iota(jnp.int32, (tq, kb), 1)
           - lax.broadcasted_iota(jnp.int32, (tq, kb), 0))
    rel = jnp.concatenate([rel] * HEADS_PER_BLOCK, axis=0)

    k_r = lax.broadcasted_iota(jnp.int32, (kb, 2 * kb), 0)
    k_c = lax.broadcasted_iota(jnp.int32, (kb, 2 * kb), 1)
    tri_ref[...] = jnp.where(k_c >= kb, 1.0, jnp.where(k_r > k_c, 1.0, 0.0)).astype(BF16)

    def softplus2(z):
        return jnp.maximum(z, 0.0) + jnp.log2(1.0 + jnp.exp2(-jnp.abs(z)))

    def scores(qh, keys):
        z = LOG2_E * lax.dot_general(qh, keys, (((1,), (1,)), ((), ())),
                                     preferred_element_type=F32)
        nlog_nb = softplus2(z)
        return nlog_nb, z - nlog_nb

    def suffix(nlog_nb):
        r = jnp.dot(nlog_nb.astype(BF16), tri_ref[...], preferred_element_type=F32)
        return r[:, :kb], r[:, kb:]

    def stacked_queries(i):
        q = q_ref[pl.ds(i * tq, tq), :]
        return jnp.concatenate(
            [jnp.where((lane >= h * HEAD_DIM) & (lane < (h + 1) * HEAD_DIM), q, jnp.zeros_like(q))
             for h in range(HEADS_PER_BLOCK)], axis=0)

    def store(i, out):
        o_ref[pl.ds(i * tq, tq), :] = jnp.where(lane < HEAD_DIM, out[0:tq], out[tq:2 * tq])

    def window_start(i):
        if isinstance(i, int):
            return max(i * tq - lookback, 0)
        return pl.multiple_of(i * tq - lookback, tq)

    def stage_scores(i, p):
        keys = k_ref[pl.ds(window_start(i), win), :]
        z_ref[p] = LOG2_E * lax.dot_general(stacked_queries(i), keys, (((1,), (1,)), ((), ())),
                                            preferred_element_type=F32)

    def stage_exponents(p, delta):
        z = z_ref[p]
        softplus = softplus2(z)
        carry = None
        for b in reversed(range(nsub)):
            cols = slice(b * kb, (b + 1) * kb)
            masked = (b + 1) * kb > delta
            valid = (rel + b * kb) < delta
            nl = softplus[:, cols]
            if masked:
                nl = jnp.where(valid, nl, 0.0)
            excl, tot = suffix(nl)
            arg = z[:, cols] - softplus[:, cols] - excl
            if carry is not None:
                arg = arg - carry
            if masked:
                arg = jnp.where(valid, arg, NEG_BIG)
            arg_ref[p, :, cols] = arg
            carry = tot if carry is None else carry + tot
        ctot_ref[p] = carry
        return jnp.min(carry)

    def stage_output(i, p, s):
        vals = v_ref[pl.ds(window_start(i), win), :]
        acc_ref[s] = jnp.dot(jnp.exp2(arg_ref[p]).astype(BF16), vals, preferred_element_type=F32)
        carry_ref[s] = ctot_ref[p]

    def finish(i, s, cmin):
        def cond(state):
            pos, cmin = state
            return (pos > -kb) & (cmin <= ATTN_STOP)

        def older(state):
            pos, _ = state
            start = pl.multiple_of(jnp.maximum(pos, 0), tq)
            keys = k_ref[pl.ds(start, kb), :]
            vals = v_ref[pl.ds(start, kb), :]
            fresh = lax.broadcasted_iota(jnp.int32, (1, kb), 1) < pos + kb - start
            nlog_nb, log_b = scores(stacked_queries(i), keys)
            nlog_nb = jnp.where(fresh, nlog_nb, 0.0)
            excl, tot = suffix(nlog_nb)
            carry = carry_ref[s]
            a = jnp.where(fresh, jnp.exp2(log_b - excl - carry), 0.0)
            acc_ref[s] += jnp.dot(a.astype(BF16), vals, preferred_element_type=F32)
            carry_ref[s] = carry + tot
            return pos - kb, jnp.min(carry + tot)

        lax.while_loop(cond, older, (jnp.asarray(window_start(i) - kb, jnp.int32), cmin))
        store(i, acc_ref[s])

    for i in range(first):
        stage_scores(i, 0)
        cmin = stage_exponents(0, i * tq - window_start(i))
        stage_output(i, 0, 0)
        finish(i, 0, cmin)

    n_pipe = seq // tq - first
    unroll = acc_ref.shape[0]
    lag = z_ref.shape[0] - 1
    n_slots = lag + 1
    pending = []
    for tau in range(2 * lag):
        stage_scores(first + tau, tau % n_slots)
        if tau >= lag:
            pending.append(stage_exponents((tau - lag) % n_slots, lookback))

    def steady(m, pending):
        pending = list(pending)
        tau0 = 2 * lag + unroll * m
        done = []
        for u in range(unroll):
            stage_output(first + tau0 + u - 2 * lag, u % n_slots, u)
            done.append((first + tau0 + u - 2 * lag, u, pending.pop(0)))
            pending.append(stage_exponents((lag + u) % n_slots, lookback))
            stage_scores(first + tau0 + u, (2 * lag + u) % n_slots)
        for block, s, cmin in done:
            finish(block, s, cmin)
        return tuple(pending)

    pending = list(lax.fori_loop(0, (n_pipe - 2 * lag) // unroll, steady, tuple(pending)))
    for tau in range(n_pipe, n_pipe + 2 * lag):
        j = tau - 2 * lag
        stage_output(first + j, j % n_slots, 0)
        cmin = pending.pop(0)
        if tau - lag < n_pipe:
            pending.append(stage_exponents((tau - lag) % n_slots, lookback))
        finish(first + j, 0, cmin)


def _attention(q, k, v, batch, seq):
    n, width = q.shape
    tq, kb, nsub = ATTN_QUERY_ROWS, LANES, ATTN_WINDOW_BLOCKS
    n_blocks = seq // tq
    lag, n_slots = ATTN_STAGE_LAG, ATTN_STAGE_LAG + 1
    clipped = -(-(nsub * kb - tq) // tq)
    unroll = first = None
    for u in range(ATTN_UNROLL - ATTN_UNROLL % n_slots, 0, -n_slots):
        for f in range(clipped, clipped + u):
            rest = n_blocks - f - 2 * lag
            if rest >= 0 and rest % u == 0 and first is None:
                unroll, first = u, f
    assert first is not None and seq % tq == 0 and seq >= nsub * kb
    blk = pl.BlockSpec((seq, LANES), lambda b, hp: (b, hp))
    rows = HEADS_PER_BLOCK * tq
    stage_buf = pltpu.VMEM((n_slots, rows, nsub * kb), F32)
    carry_buf = pltpu.VMEM((n_slots, rows, kb), F32)
    row_buf = pltpu.VMEM((unroll, rows, LANES), F32)
    return pl.pallas_call(
        functools.partial(_attn_kernel, tq=tq, kb=kb, nsub=nsub, first=first),
        grid=(batch, width // LANES),
        in_specs=[blk, blk, blk],
        out_specs=blk,
        out_shape=jax.ShapeDtypeStruct((n, width), F32),
        scratch_shapes=[pltpu.VMEM((kb, 2 * kb), BF16),
                        stage_buf, stage_buf, carry_buf, row_buf, row_buf],
        compiler_params=_cparams(("arbitrary", "arbitrary")),
        name="attn",
    )(q, k, v)


def _gelu_tanh(x):
    c = math.sqrt(2.0 / math.pi)
    half_x = 0.5 * x
    return half_x + half_x * jnp.tanh(x * (c + (c * 0.044715) * (x * x)))


def _sigmoid(x):
    return 0.5 + 0.5 * jnp.tanh(0.5 * x)


def _lru_kernel(*refs, ts, n_slab):
    xl_refs, gl_refs = refs[0:n_slab], refs[n_slab:2 * n_slab]
    cw_ref, cb_ref, wr_ref, br_ref, wi_ref, bi_ref, lam_ref = refs[2 * n_slab:2 * n_slab + 7]
    o_refs = refs[2 * n_slab + 7:3 * n_slab + 7]
    tail_ref, a7_ref, u7_ref, hp_ref, pa_ref, pu_ref, h_ref = refs[3 * n_slab + 7:]
    t = pl.program_id(1)
    groups = ts // SUBLANES

    @pl.when(t == 0)
    def _():
        tail_ref[...] = jnp.zeros_like(tail_ref)
        h_ref[...] = jnp.zeros_like(h_ref)

    first_group = lax.broadcasted_iota(jnp.int32, (groups, LANES), 0) == 0
    for c in range(n_slab):
        lanes = slice(c * LANES, (c + 1) * LANES)
        x = [xl_refs[c][pl.ds(s, groups, stride=SUBLANES), :] for s in range(SUBLANES)]
        shifted = {}
        for s in range(SUBLANES - (CONV_W - 1), SUBLANES):
            shifted[s] = jnp.where(first_group, tail_ref[c, s:s + 1, :], pltpu.roll(x[s], 1, axis=0))
            tail_ref[c, s:s + 1, :] = x[s][groups - 1:groups, :]
        conv = []
        for s in range(SUBLANES):
            y = cb_ref[:, lanes]
            for w in range(CONV_W):
                j = s - (CONV_W - 1) + w
                y = y + (x[j] if j >= 0 else shifted[j + SUBLANES]) * cw_ref[w:w + 1, lanes]
            conv.append(y)
        xc = jnp.concatenate(conv, axis=0)

        xcb = xc.astype(BF16)
        r = _sigmoid(jnp.dot(xcb, wr_ref[c], preferred_element_type=F32) + br_ref[:, lanes])
        ig = _sigmoid(jnp.dot(xcb, wi_ref[c], preferred_element_type=F32) + bi_ref[:, lanes])
        lam = lam_ref[:, lanes]
        log_sig_lam = -(jnp.maximum(-lam, 0.0) + jnp.log1p(jnp.exp(-jnp.abs(lam))))
        log_a = RG_C * r * log_sig_lam
        a = jnp.exp(log_a)
        th = jnp.tanh(log_a)
        one_m_a2 = -2.0 * th / (1.0 - th)
        root = jnp.where(one_m_a2 > 0.0, one_m_a2 * lax.rsqrt(one_m_a2), 0.0)
        u = root * (ig * xc)

        a_run = u_run = None
        for s in range(SUBLANES):
            rows = slice(s * groups, (s + 1) * groups)
            if s == 0:
                a_run, u_run = a[rows], u[rows]
            else:
                u_run = a[rows] * u_run + u[rows]
                a_run = a_run * a[rows]
            pa_ref[c, rows, :] = a_run
            pu_ref[c, rows, :] = u_run
        a7_ref[c] = a_run
        u7_ref[c] = u_run

    def group(g, hs):
        nxt = []
        for c in range(n_slab):
            hp_ref[c, pl.ds(g, 1), :] = hs[c]
            nxt.append(a7_ref[c, pl.ds(g, 1), :] * hs[c] + u7_ref[c, pl.ds(g, 1), :])
        return tuple(nxt)

    hs = lax.fori_loop(0, groups, group, tuple(h_ref[c] for c in range(n_slab)), unroll=8)
    for c in range(n_slab):
        h_ref[c] = hs[c]

    for c in range(n_slab):
        h_in = hp_ref[c]
        for s in range(SUBLANES):
            rows = slice(s * groups, (s + 1) * groups)
            hseq = pu_ref[c, rows, :] + pa_ref[c, rows, :] * h_in
            gate = _gelu_tanh(gl_refs[c][pl.ds(s, groups, stride=SUBLANES), :])
            o_refs[c][pl.ds(s, groups, stride=SUBLANES), :] = hseq * gate


def _lru(xl, gl, conv_w, conv_b, wr_bd, br, wi_bd, bi, lam, batch, seq, ts):
    n, width = xl.shape
    nt = seq // ts
    n_slab = width // LANES
    assert wr_bd.shape == (n_slab, LANES, LANES) and ts % (SUBLANES * SUBLANES) == 0
    groups = ts // SUBLANES
    slab = [pl.BlockSpec((ts, LANES), functools.partial(lambda b, t, c: (b * nt + t, c), c=c))
            for c in range(n_slab)]
    const2 = lambda b, t: (0, 0)
    const3 = lambda b, t: (0, 0, 0)
    vec = pl.BlockSpec((1, width), const2)
    per_group = pltpu.VMEM((n_slab, groups, LANES), F32)
    per_step = pltpu.VMEM((n_slab, ts, LANES), F32)
    return pl.pallas_call(
        functools.partial(_lru_kernel, ts=ts, n_slab=n_slab),
        grid=(batch, nt),
        in_specs=slab + slab + [pl.BlockSpec((CONV_W, width), const2), vec,
                                pl.BlockSpec(wr_bd.shape, const3), vec,
                                pl.BlockSpec(wi_bd.shape, const3), vec, vec],
        out_specs=[pl.BlockSpec((ts, LANES), lambda b, t: (b * nt + t, 0))] * n_slab,
        out_shape=[jax.ShapeDtypeStruct((n, LANES), F32)] * n_slab,
        scratch_shapes=[pltpu.VMEM((n_slab, SUBLANES, LANES), F32),
                        per_group, per_group, per_group, per_step, per_step,
                        pltpu.VMEM((n_slab, 1, LANES), F32)],
        compiler_params=_cparams(("arbitrary", "arbitrary")),
        name="lru",
    )(*([xl] * n_slab), *([gl] * n_slab), conv_w, conv_b, wr_bd, br, wi_bd, bi, lam)


def _mix_route_kernel(sb_ref, *refs, tm, n_slab):
    lru_refs = refs[:n_slab]
    (x_ref, sbg_ref, lrug_ref, wo_ref, n2g_ref, wrs_ref, rb_ref,
     h_ref, u2_ref, rr_ref, rwt_ref, tc_ref, before_ref, u2s_ref, wo_bf) = refs[n_slab:]

    @pl.when(pl.program_id(0) == 0)
    def _():
        wo_bf[...] = wo_ref[...].astype(BF16)

    _mix_route_body(sb_ref, lru_refs, x_ref, sbg_ref, lrug_ref, wo_bf, n2g_ref, wrs_ref, rb_ref,
                    h_ref, u2_ref, rr_ref, rwt_ref, tc_ref, before_ref, u2s_ref, tm)


def _mix_route_body(sb_ref, lru_refs, x_ref, sbg_ref, lrug_ref, wo_ref, n2g_ref, wrs_ref, rb_ref,
                    h_ref, u2_ref, rr_ref, rwt_ref, tc_ref, before_ref, u2s_ref, tm):
    step = pl.program_id(0)
    half = sb_ref.shape[1]
    n_sub = x_ref.shape[0] // tm

    @pl.when(step == 0)
    def _():
        r_id = lax.broadcasted_iota(jnp.int32, (tm, tm), 0)
        c_id = lax.broadcasted_iota(jnp.int32, (tm, tm), 1)
        before_ref[...] = (r_id < c_id).astype(BF16)
        tc_ref[...] = jnp.zeros_like(tc_ref)
        u2s_ref[...] = jnp.zeros_like(u2s_ref)

    def route(j):
        rows = slice(j * tm, (j + 1) * tm)
        u2 = u2s_ref[rows, :]
        u_hi = u2.astype(BF16)
        u_lo = (u2 - u_hi.astype(F32)).astype(BF16)
        nt_dims = (((1,), (1,)), ((), ()))
        n_rows = rb_ref.shape[0]
        both = lax.dot_general(wrs_ref[...], u_hi, nt_dims, preferred_element_type=F32)
        lt = (both[0:n_rows] + both[n_rows:2 * n_rows]
              + lax.dot_general(wrs_ref[0:n_rows, :], u_lo, nt_dims, preferred_element_type=F32)
              + rb_ref[...])

        sub = lax.broadcasted_iota(jnp.int32, (SUBLANES, tm), 0)

        def top1(x):
            m = jnp.max(x, axis=0, keepdims=True)
            idx = jnp.min(jnp.where(x == m, sub, SUBLANES), axis=0, keepdims=True)
            return m, idx

        grp = lt[0:SUBLANES, :]
        g_max, g_idx = top1(grp)
        g_p = 1.0 / jnp.sum(jnp.exp(grp - g_max), axis=0, keepdims=True)
        fine = lt[SUBLANES:2 * SUBLANES, :]
        for g in range(1, N_GROUPS):
            fine = jnp.where(g_idx == g, lt[(g + 1) * SUBLANES:(g + 2) * SUBLANES, :], fine)
        m1, i1 = top1(fine)
        m2, i2 = top1(jnp.where(sub == i1, -jnp.inf, fine))
        e2 = jnp.exp(m2 - m1)
        p1 = 1.0 / (1.0 + e2)
        w1 = g_p * p1
        w2 = g_p * (e2 * p1)
        x1 = g_idx * EXPERTS_PER_GROUP + i1
        x2 = g_idx * EXPERTS_PER_GROUP + i2

        eid = lax.broadcasted_iota(jnp.int32, (N_EXPERTS, tm), 0)
        oh1 = jnp.where(eid == x1, 1.0, 0.0)
        oh2 = jnp.where(eid == x2, 1.0, 0.0)
        pre = jnp.dot(jnp.concatenate([oh1, oh2], axis=0).astype(BF16), before_ref[...],
                      preferred_element_type=F32)
        pre1, pre2 = pre[0:N_EXPERTS], pre[N_EXPERTS:2 * N_EXPERTS]
        cnt1 = jnp.sum(oh1, axis=1, keepdims=True)
        cnt2 = jnp.sum(oh2, axis=1, keepdims=True)
        seg8 = jnp.floor((cnt1 + cnt2 + (SUBLANES - 1.0)) * (1.0 / SUBLANES))
        e_r = lax.broadcasted_iota(jnp.int32, (N_EXPERTS, N_EXPERTS), 0)
        e_c = lax.broadcasted_iota(jnp.int32, (N_EXPERTS, N_EXPERTS), 1)
        lower = jnp.where(e_c < e_r, 1.0, 0.0).astype(BF16)
        seg8_b = jnp.broadcast_to(seg8, (N_EXPERTS, LANES)).astype(BF16)
        seg_off = SUBLANES * jnp.dot(lower, seg8_b, preferred_element_type=F32)[:, 0:1]
        pos1 = jnp.sum(oh1 * (pre1 + seg_off), axis=0, keepdims=True)
        pos2 = jnp.sum(oh2 * (pre2 + (seg_off + cnt1)), axis=0, keepdims=True)

        lane = lax.broadcasted_iota(jnp.int32, tc_ref.shape, 1)
        seg_rows = jnp.broadcast_to(seg8 * SUBLANES, tc_ref.shape).astype(jnp.int32)
        tile = (step - 1) * n_sub + j
        tc_ref[...] = jnp.where(lane == tile, seg_rows, tc_ref[...])

        zrow = jnp.zeros((SUBLANES - 4, tm), jnp.int32)
        rr_ref[:, rows] = jnp.concatenate(
            [pos1.astype(jnp.int32), pos2.astype(jnp.int32), x1, x2, zrow], axis=0)
        wt = jnp.concatenate([w1, w2, pos1, pos2, jnp.zeros((LANES - 4, tm), F32)], axis=0)
        rwt_ref[rows, :] = wt.T

    for j in range(n_sub):
        route(j)

    m_sb = _rms_f32(sb_ref[...], sbg_ref[...]).astype(BF16)
    lru = jnp.concatenate([r[...] for r in lru_refs], axis=-1)
    m_lru = _rms_f32(lru, lrug_ref[...]).astype(BF16)
    h = (x_ref[...]
         + jnp.dot(m_sb, wo_ref[0:half, :], preferred_element_type=F32)
         + jnp.dot(m_lru, wo_ref[half:2 * half, :], preferred_element_type=F32))
    h_ref[...] = h
    u2_next = _rms_f32(h, n2g_ref[...])
    u2_ref[...] = u2_next.astype(BF16)
    u2s_ref[...] = u2_next


def _mix_route(sb, lru, x2, sbg, lrug, w_out, n2g, wr_stack, rbias, tm, tp):
    n, d = x2.shape
    half = sb.shape[1]
    n_tiles = n // tp
    row = lambda i: (jnp.minimum(i, n_tiles - 1), 0)
    routed = lambda i: (jnp.maximum(i - 1, 0), 0)
    const = lambda i: (0, 0)
    return pl.pallas_call(
        functools.partial(_mix_route_kernel, tm=tm, n_slab=len(lru)),
        grid=(n_tiles + 1,),
        in_specs=[pl.BlockSpec((tp, half), row)] + [pl.BlockSpec((tp, LANES), row)] * len(lru)
                 + [pl.BlockSpec((tp, d), row),
                  pl.BlockSpec((1, half), const), pl.BlockSpec((1, half), const),
                  pl.BlockSpec(w_out.shape, const, pipeline_mode=pl.Buffered(1)),
                  pl.BlockSpec((1, d), const),
                  pl.BlockSpec(wr_stack.shape, const), pl.BlockSpec(rbias.shape, const)],
        out_specs=[pl.BlockSpec((tp, d), row), pl.BlockSpec((tp, d), row),
                   pl.BlockSpec((SUBLANES, tp), lambda i: (0, jnp.maximum(i - 1, 0))),
                   pl.BlockSpec((tp, LANES), routed),
                   pl.BlockSpec((N_EXPERTS, LANES), const)],
        out_shape=[jax.ShapeDtypeStruct((n, d), F32), jax.ShapeDtypeStruct((n, d), BF16),
                   jax.ShapeDtypeStruct((SUBLANES, n), jnp.int32),
                   jax.ShapeDtypeStruct((n, LANES), F32),
                   jax.ShapeDtypeStruct((N_EXPERTS, LANES), jnp.int32)],
        scratch_shapes=[pltpu.VMEM((tm, tm), BF16), pltpu.VMEM((tp, d), F32),
                        pltpu.VMEM(w_out.shape, BF16)],
        compiler_params=_cparams(("arbitrary",)),
        name="mix_route",
    )(sb, *lru, x2, sbg, lrug, w_out, n2g, wr_stack, rbias)


def _pack_halves(x):
    half = x.shape[1] // 2
    lo = lax.shift_right_logical(lax.bitcast_convert_type(x[:, :half], jnp.int32), 16)
    hi = lax.bitcast_convert_type(x[:, half:], jnp.int32) & HIGH_HALF
    return hi | lo


def _unpack_halves(p):
    lo = lax.bitcast_convert_type(lax.shift_left(p, 16), F32)
    hi = lax.bitcast_convert_type(p & HIGH_HALF, F32)
    return lo.astype(BF16), hi.astype(BF16)


def _segment_copies(tile, c8_ref, loff_ref, goff_ref, make):
    for e in range(N_EXPERTS):
        idx = tile * N_EXPERTS + e
        rows = pl.multiple_of(c8_ref[idx], SUBLANES)

        @pl.when(rows > 0)
        def _(idx=idx, rows=rows):
            lo = pl.multiple_of(loff_ref[idx], SUBLANES)
            go = pl.multiple_of(goff_ref[idx], SUBLANES)
            make(pl.ds(lo, rows), pl.ds(go, rows)).start()


def _dispatch_kernel(c8_ref, loff_ref, goff_ref, tot_ref, used_ref,
                     rr_ref, u2_ref, xs_ref, lbuf, zbuf, sem, zsem, *, td, lrows):
    i = pl.program_id(0)
    slot = i % 2

    r_id = lax.broadcasted_iota(jnp.int32, (lrows, td), 0)
    perm = jnp.where(r_id == rr_ref[0:1, :], 1.0, jnp.where(r_id == rr_ref[1:2, :], 1.0, 0.0))
    sorted_rows = jnp.dot(perm.astype(BF16), u2_ref[...], preferred_element_type=F32)
    lbuf[slot] = _pack_halves(sorted_rows)

    @pl.when(i == 0)
    def _():
        zbuf[...] = jnp.zeros_like(zbuf)
        chunk = zbuf.shape[0]
        used = used_ref[0]
        spare = xs_ref.shape[0] - used
        n_fill = (spare + chunk - 1) // chunk

        def fill_copy(k):
            rows = pl.multiple_of(jnp.minimum(chunk, spare - k * chunk), SUBLANES)
            start = pl.multiple_of(used + k * chunk, SUBLANES)
            return pltpu.make_async_copy(zbuf.at[pl.ds(0, rows)], xs_ref.at[pl.ds(start, rows)],
                                         zsem)

        def fill_start(k, c):
            fill_copy(k).start()
            return c

        def fill_wait(k, c):
            fill_copy(k).wait()
            return c

        lax.fori_loop(0, n_fill, fill_start, 0)
        lax.fori_loop(0, n_fill, fill_wait, 0)

    def wait_tile(tile, s):
        rows = pl.multiple_of(tot_ref[tile], SUBLANES)
        pltpu.make_async_copy(lbuf.at[s, pl.ds(0, rows)], xs_ref.at[pl.ds(0, rows)], sem).wait()

    @pl.when(i > 0)
    def _():
        wait_tile(i - 1, 1 - slot)

    _segment_copies(i, c8_ref, loff_ref, goff_ref,
                    lambda loc, glob: pltpu.make_async_copy(lbuf.at[slot, loc], xs_ref.at[glob], sem))

    @pl.when(i == pl.num_programs(0) - 1)
    def _():
        wait_tile(i, slot)


def _dispatch(c8, loff, goff, tot, used, rr, u2, p_rows, td, lrows, tme):
    n, d = u2.shape
    pmap = lambda i, *_: (0, i)
    return pl.pallas_call(
        functools.partial(_dispatch_kernel, td=td, lrows=lrows),
        grid_spec=pltpu.PrefetchScalarGridSpec(
            num_scalar_prefetch=5,
            grid=(n // td,),
            in_specs=[pl.BlockSpec((SUBLANES, td), pmap),
                      pl.BlockSpec((td, d), lambda i, *_: (i, 0))],
            out_specs=pl.BlockSpec(memory_space=pl.ANY),
            scratch_shapes=[pltpu.VMEM((2, lrows, d // 2), jnp.int32),
                            pltpu.VMEM((tme, d // 2), jnp.int32),
                            pltpu.SemaphoreType.DMA(()), pltpu.SemaphoreType.DMA(())]),
        out_shape=jax.ShapeDtypeStruct((p_rows, d // 2), jnp.int32),
        compiler_params=_cparams(("arbitrary",)),
        name="dispatch",
    )(c8, loff, goff, tot, used, rr, u2)


def _experts_kernel(eoff_ref, erows_ref, xs_ref, wg_ref, wu_ref, wd_ref, ys_ref,
                    wg_bf, wu_bf, wd_bf, xbuf, ybuf, sem_in, sem_out, state, *, tme):
    e = pl.program_id(0)
    n_experts = pl.num_programs(0)
    rows = erows_ref[e]
    off = eoff_ref[e]
    n_tiles = (rows + tme - 1) // tme
    nxt = jnp.minimum(e + 1, n_experts - 1)
    prefetch = (rows > 0) & (e + 1 < n_experts) & (erows_ref[nxt] > 0)

    def tile_rows(total, k):
        return pl.multiple_of(jnp.minimum(tme, total - k * tme), SUBLANES)

    def in_copy(start, r, slot):
        start = pl.multiple_of(start, SUBLANES)
        return pltpu.make_async_copy(xs_ref.at[pl.ds(start, r)], xbuf.at[slot, pl.ds(0, r)],
                                     sem_in.at[slot])

    def out_copy(start, r, slot):
        start = pl.multiple_of(start, SUBLANES)
        return pltpu.make_async_copy(ybuf.at[slot, pl.ds(0, r)], ys_ref.at[pl.ds(start, r)],
                                     sem_out.at[slot])

    @pl.when(e == 0)
    def _():
        for s in range(4):
            state[s] = 0
        xbuf[...] = jnp.zeros_like(xbuf)

    @pl.when(rows > 0)
    def _():
        wg_bf[...] = wg_ref[0].astype(BF16)
        wu_bf[...] = wu_ref[0].astype(BF16)
        wd_bf[...] = wd_ref[0].astype(BF16)
        done = state[0]

        @pl.when(state[1] == 0)
        def _():
            in_copy(off, tile_rows(rows, 0), done % 2).start()

        def tile(k, c):
            slot = (done + k) % 2
            r = tile_rows(rows, k)

            @pl.when(k + 1 < n_tiles)
            def _():
                in_copy(off + (k + 1) * tme, tile_rows(rows, k + 1), 1 - slot).start()

            @pl.when((k + 1 == n_tiles) & prefetch)
            def _():
                in_copy(eoff_ref[nxt], tile_rows(erows_ref[nxt], 0), 1 - slot).start()

            in_copy(off, r, slot).wait()

            @pl.when(done + k >= 2)
            def _():
                out_copy(0, pl.multiple_of(state[2 + slot], SUBLANES), slot).wait()

            def mlp(start, n):
                x = jnp.concatenate(_unpack_halves(xbuf[slot, start:start + n]), axis=1)
                hg = jnp.dot(x, wg_bf[...], preferred_element_type=F32)
                hu = jnp.dot(x, wu_bf[...], preferred_element_type=F32)
                act = (hg * jax.nn.sigmoid(hg) * hu).astype(BF16)
                y = jnp.dot(act, wd_bf[...], preferred_element_type=F32)
                ybuf[slot, start:start + n] = _pack_halves(y.astype(BF16).astype(F32))

            for n in range(EXPERT_ROW_STEP, tme + 1, EXPERT_ROW_STEP):
                @pl.when((r > n - EXPERT_ROW_STEP) & (r <= n))
                def _(n=n):
                    mlp(0, n)

            out_copy(off + k * tme, r, slot).start()
            state[2 + slot] = r
            return c

        lax.fori_loop(0, n_tiles, tile, 0)
        state[0] = done + n_tiles

    state[1] = prefetch.astype(jnp.int32)

    @pl.when(e == n_experts - 1)
    def _():
        total = eoff_ref[e] + rows
        for slot in range(2):
            @pl.when(state[0] > slot)
            def _(slot=slot):
                out_copy(0, pl.multiple_of(state[2 + slot], SUBLANES), slot).wait()
        ybuf[0] = jnp.zeros(ybuf.shape[1:], ybuf.dtype)
        spare = ys_ref.shape[0] - total
        n_fill = (spare + tme - 1) // tme

        def fill_copy(k):
            return out_copy(total + k * tme, tile_rows(spare, k), 0)

        def fill_start(k, c):
            fill_copy(k).start()
            return c

        def fill_wait(k, c):
            fill_copy(k).wait()
            return c

        lax.fori_loop(0, n_fill, fill_start, 0)
        lax.fori_loop(0, n_fill, fill_wait, 0)


def _experts(eoff, erows, xs, wg, wu, wd, tme):
    p = xs.shape[0]
    n_experts, d, de = wg.shape
    wmap = lambda e, *_: (e, 0, 0)
    return pl.pallas_call(
        functools.partial(_experts_kernel, tme=tme),
        grid_spec=pltpu.PrefetchScalarGridSpec(
            num_scalar_prefetch=2,
            grid=(n_experts,),
            in_specs=[pl.BlockSpec(memory_space=pl.ANY),
                      pl.BlockSpec((1, d, de), wmap), pl.BlockSpec((1, d, de), wmap),
                      pl.BlockSpec((1, de, d), wmap)],
            out_specs=pl.BlockSpec(memory_space=pl.ANY),
            scratch_shapes=[pltpu.VMEM((d, de), BF16), pltpu.VMEM((d, de), BF16),
                            pltpu.VMEM((de, d), BF16),
                            pltpu.VMEM((2, tme, d // 2), jnp.int32),
                            pltpu.VMEM((2, tme, d // 2), jnp.int32),
                            pltpu.SemaphoreType.DMA((2,)), pltpu.SemaphoreType.DMA((2,)),
                            pltpu.SMEM((4,), jnp.int32)]),
        out_shape=jax.ShapeDtypeStruct((p, d // 2), jnp.int32),
        compiler_params=_cparams(("arbitrary",)),
        name="experts",
    )(eoff, erows, xs, wg, wu, wd)


def _combine_kernel(c8_ref, loff_ref, goff_ref, tot_ref,
                    rwt_ref, h_ref, fg_ref, ys_ref, y_ref, ybuf, sems, *, tc, lrows):
    i = pl.program_id(0)
    slot = i % 2

    def gather_tile(tile, s):
        _segment_copies(tile, c8_ref, loff_ref, goff_ref,
                        lambda loc, glob: pltpu.make_async_copy(ys_ref.at[glob], ybuf.at[s, loc],
                                                                sems.at[s]))

    @pl.when(i == 0)
    def _():
        ybuf[...] = jnp.zeros_like(ybuf)
        gather_tile(0, 0)

    @pl.when(i + 1 < pl.num_programs(0))
    def _():
        gather_tile(i + 1, 1 - slot)

    rows = pl.multiple_of(tot_ref[i], SUBLANES)
    pltpu.make_async_copy(ys_ref.at[pl.ds(0, rows)], ybuf.at[slot, pl.ds(0, rows)],
                          sems.at[slot]).wait()

    w = rwt_ref[...]
    c_id = lax.broadcasted_iota(jnp.int32, (tc, lrows), 1)
    pos1 = w[:, 2:3].astype(jnp.int32)
    pos2 = w[:, 3:4].astype(jnp.int32)
    wmat = jnp.where(c_id == pos1, w[:, 0:1], jnp.where(c_id == pos2, w[:, 1:2], 0.0)).astype(BF16)
    half = h_ref.shape[1] // 2
    outs, sumsq = [], 0.0
    for part, cols in zip(_unpack_halves(ybuf[slot]), (slice(0, half), slice(half, 2 * half))):
        out = h_ref[:, cols] + jnp.dot(wmat, part, preferred_element_type=F32)
        sumsq = sumsq + jnp.sum(out * out, axis=-1, keepdims=True)
        outs.append(out)
    scale = lax.rsqrt(sumsq * (1.0 / (2 * half)) + EPS)
    for out, cols in zip(outs, (slice(0, half), slice(half, 2 * half))):
        y_ref[:, cols] = out * scale * fg_ref[:, cols]


def _combine(c8, loff, goff, tot, rwt, h, final_g, ys, tc, lrows):
    n, d = h.shape
    return pl.pallas_call(
        functools.partial(_combine_kernel, tc=tc, lrows=lrows),
        grid_spec=pltpu.PrefetchScalarGridSpec(
            num_scalar_prefetch=4,
            grid=(n // tc,),
            in_specs=[pl.BlockSpec((tc, LANES), lambda i, *_: (i, 0)),
                      pl.BlockSpec((tc, d), lambda i, *_: (i, 0)),
                      pl.BlockSpec((1, d), lambda i, *_: (0, 0)),
                      pl.BlockSpec(memory_space=pl.ANY)],
            out_specs=pl.BlockSpec((tc, d), lambda i, *_: (i, 0)),
            scratch_shapes=[pltpu.VMEM((2, lrows, d // 2), jnp.int32),
                            pltpu.SemaphoreType.DMA((2,))]),
        out_shape=jax.ShapeDtypeStruct((n, d), F32),
        compiler_params=_cparams(("arbitrary",)),
        name="combine",
    )(c8, loff, goff, tot, rwt, h, final_g, ys)


def _block_diag(w, per):
    nb, c, _ = w.shape
    eye = jnp.eye(per, dtype=w.dtype)
    wg = w.reshape(nb // per, per, c, c)
    return jnp.einsum("gpij,pq->gpiqj", wg, eye).reshape(nb // per, per * c, per * c)


def _router_tables(w_group, b_group, w_fine, b_fine):
    d = w_group.shape[0]
    w = jnp.zeros((ROUTER_ROWS, d), F32)
    w = w.at[0:N_GROUPS].set(w_group.T).at[SUBLANES:SUBLANES + N_EXPERTS].set(w_fine.T)
    b = jnp.full((ROUTER_ROWS,), NEG_BIG, F32)
    b = b.at[0:N_GROUPS].set(b_group).at[SUBLANES:SUBLANES + N_EXPERTS].set(b_fine)
    w_hi = w.astype(BF16)
    w_lo = (w - w_hi.astype(F32)).astype(BF16)
    return jnp.concatenate([w_hi, w_lo], axis=0), b.reshape(ROUTER_ROWS, 1)


def kernel(x, norm1_g, w_in, conv_w, conv_b, w_rgate, b_rgate, w_igate, b_igate, lam, sb_norm_g,
           lru_norm_g, w_out, norm2_g, w_group, b_group, w_fine, b_fine, w_e_gate, w_e_up,
           w_e_down, final_g):
    batch, seq, d = x.shape
    n = batch * seq
    width = w_in.shape[1] // 5
    tm = min(TOKEN_TILE, seq)
    tp = min(ROW_TILE, seq)
    tme = EXPERT_TILE_ROWS
    gate_per = LANES // w_rgate.shape[1]

    x2 = x.reshape(n, d)
    vec = lambda a: a.reshape(1, -1)

    q, k, v, xl, gl = _in_proj(x2, vec(norm1_g), w_in, width, tp)
    out_sb = _attention(q, k, v, batch, seq)
    out_lru = _lru(xl, gl, conv_w, vec(conv_b),
                   _block_diag(w_rgate, gate_per).astype(BF16), vec(b_rgate),
                   _block_diag(w_igate, gate_per).astype(BF16), vec(b_igate),
                   vec(lam), batch, seq, tp)

    wr_stack, rbias = _router_tables(w_group, b_group, w_fine, b_fine)
    h, u2, rr, rwt, tcnt = _mix_route(out_sb, out_lru, x2, vec(sb_norm_g), vec(lru_norm_g),
                                      w_out, vec(norm2_g), wr_stack, rbias, tm, tp)

    n_tiles = n // tm
    assert n_tiles <= LANES, "one lane of the per-tile count table per token tile"
    c8 = tcnt[:, :n_tiles].T
    erows = jnp.sum(c8, axis=0)
    eoff = jnp.cumsum(erows) - erows
    goff = eoff[None, :] + jnp.cumsum(c8, axis=0) - c8
    loff = jnp.cumsum(c8, axis=1) - c8
    tot = jnp.sum(c8, axis=1)
    lrows = 2 * tm + N_EXPERTS * SUBLANES
    p_rows = 2 * n + n_tiles * N_EXPERTS * (SUBLANES - 1)
    p_rows = -(-p_rows // SUBLANES) * SUBLANES
    i32 = lambda a: a.reshape(-1).astype(jnp.int32)
    c8, loff, goff, tot, eoff, erows = (i32(a) for a in (c8, loff, goff, tot, eoff, erows))

    xs = _dispatch(c8, loff, goff, tot, jnp.sum(erows, keepdims=True), rr, u2, p_rows, tm, lrows, tme)
    ys = _experts(eoff, erows, xs, w_e_gate, w_e_up, w_e_down, tme)
    y = _combine(c8, loff, goff, tot, rwt, h, vec(final_g), ys, tm, lrows)
    return y.reshape(batch, seq, d)
```

```python
import functools
import math

import jax
import jax.numpy as jnp
from jax import lax
from jax.experimental import pallas as pl
from jax.experimental.pallas import tpu as pltpu

F32 = jnp.float32
BF16 = jnp.bfloat16

EPS = 1e-6
HEAD_DIM = 64
HEADS_PER_BLOCK = 2
LANES = 128
SUBLANES = 8
CONV_W = 4
RG_C = 8.0
N_GROUPS = 4
EXPERTS_PER_GROUP = 8
N_EXPERTS = N_GROUPS * EXPERTS_PER_GROUP
ROUTER_ROWS = 48
NEG_BIG = -1e30
LOG2_E = math.log2(math.e)
ATTN_STOP = 104.0 * LOG2_E
ATTN_QUERY_ROWS = 64
ATTN_WINDOW_BLOCKS = 2
ATTN_UNROLL = 24
ATTN_STAGE_LAG = 2
HIGH_HALF = -65536
EXPERT_TILE_ROWS = 1536
EXPERT_TILE_SIZES = (256, 512, 768, 1024, 1152, 1280, 1408, 1536)
TOKEN_TILE = 512
ROW_TILE = 1024

VMEM_LIMIT = 56 * 1024 * 1024


def _cparams(sem):
    return pltpu.CompilerParams(dimension_semantics=sem, vmem_limit_bytes=VMEM_LIMIT)


def _rms_f32(x, g):
    return x * lax.rsqrt(jnp.mean(x * x, axis=-1, keepdims=True) + EPS) * g


def _in_proj_kernel(x_ref, g_ref, w_ref, q_ref, k_ref, v_ref, xl_ref, gl_ref, w_bf, *,
                    width, q_scale):
    @pl.when(pl.program_id(0) == 0)
    def _():
        for c in range(w_ref.shape[1] // width):
            cols = slice(c * width, (c + 1) * width)
            w_bf[:, cols] = w_ref[:, cols].astype(BF16)

    u = _rms_f32(x_ref[...], g_ref[...]).astype(BF16)
    for c, o_ref in enumerate((q_ref, k_ref, v_ref, xl_ref, gl_ref)):
        p = jnp.dot(u, w_bf[:, c * width:(c + 1) * width], preferred_element_type=F32)
        if c == 0:
            p = p * q_scale
        o_ref[...] = p.astype(o_ref.dtype)


def _in_proj(x2, g, w, width, tm):
    n, d = x2.shape
    row = lambda i: (i, 0)
    out_bf = jax.ShapeDtypeStruct((n, width), BF16)
    out_f = jax.ShapeDtypeStruct((n, width), F32)
    return pl.pallas_call(
        functools.partial(_in_proj_kernel, width=width, q_scale=1.0 / math.sqrt(HEAD_DIM)),
        grid=(n // tm,),
        in_specs=[pl.BlockSpec((tm, d), row),
                  pl.BlockSpec((1, d), lambda i: (0, 0)),
                  pl.BlockSpec(w.shape, lambda i: (0, 0), pipeline_mode=pl.Buffered(1))],
        out_specs=[pl.BlockSpec((tm, width), row)] * 5,
        out_shape=[out_bf, out_bf, out_bf, out_f, out_f],
        scratch_shapes=[pltpu.VMEM(w.shape, BF16)],
        compiler_params=_cparams(("arbitrary",)),
        name="in_proj",
    )(x2, g, w)


def _attn_kernel(q_ref, k_ref, v_ref, o_ref, tri_ref, z_ref, arg_ref, ctot_ref, acc_ref, carry_ref,
                 *, tq, kb, nsub, first):
    seq = q_ref.shape[0]
    win = nsub * kb
    lookback = win - tq
    lane = lax.broadcasted_iota(jnp.int32, (1, LANES), 1)
    rel = (lax.broadcasted_iota(jnp.int32, (tq, kb), 1)
           - lax.broadcasted_iota(jnp.int32, (tq, kb), 0))
    rel = jnp.concatenate([rel] * HEADS_PER_BLOCK, axis=0)

    k_r = lax.broadcasted_iota(jnp.int32, (kb, 2 * kb), 0)
    k_c = lax.broadcasted_iota(jnp.int32, (kb, 2 * kb), 1)
    tri_ref[...] = jnp.where(k_c >= kb, 1.0, jnp.where(k_r > k_c, 1.0, 0.0)).astype(BF16)

    def softplus2(z):
        return jnp.maximum(z, 0.0) + jnp.log2(1.0 + jnp.exp2(-jnp.abs(z)))

    def scores(qh, keys):
        z = LOG2_E * lax.dot_general(qh, keys, (((1,), (1,)), ((), ())),
                                     preferred_element_type=F32)
        nlog_nb = softplus2(z)
        return nlog_nb, z - nlog_nb

    def suffix(nlog_nb):
        r = jnp.dot(nlog_nb.astype(BF16), tri_ref[...], preferred_element_type=F32)
        return r[:, :kb], r[:, kb:]

    def stacked_queries(i):
        q = q_ref[pl.ds(i * tq, tq), :]
        return jnp.concatenate(
            [jnp.where((lane >= h * HEAD_DIM) & (lane < (h + 1) * HEAD_DIM), q, jnp.zeros_like(q))
             for h in range(HEADS_PER_BLOCK)], axis=0)

    def store(i, out):
        o_ref[pl.ds(i * tq, tq), :] = jnp.where(lane < HEAD_DIM, out[0:tq], out[tq:2 * tq])

    def window_start(i):
        if isinstance(i, int):
            return max(i * tq - lookback, 0)
        return pl.multiple_of(i * tq - lookback, tq)

    def stage_scores(i, p):
        keys = k_ref[pl.ds(window_start(i), win), :]
        z_ref[p] = LOG2_E * lax.dot_general(stacked_queries(i), keys, (((1,), (1,)), ((), ())),
                                            preferred_element_type=F32)

    def stage_exponents(p, delta):
        z = z_ref[p]
        softplus = softplus2(z)
        carry = None
        for b in reversed(range(nsub)):
            cols = slice(b * kb, (b + 1) * kb)
            masked = (b + 1) * kb > delta
            valid = (rel + b * kb) < delta
            nl = softplus[:, cols]
            if masked:
                nl = jnp.where(valid, nl, 0.0)
            excl, tot = suffix(nl)
            arg = z[:, cols] - softplus[:, cols] - excl
            if carry is not None:
                arg = arg - carry
            if masked:
                arg = jnp.where(valid, arg, NEG_BIG)
            arg_ref[p, :, cols] = arg
            carry = tot if carry is None else carry + tot
        ctot_ref[p] = carry
        return jnp.min(carry)

    def stage_output(i, p, s):
        vals = v_ref[pl.ds(window_start(i), win), :]
        acc_ref[s] = jnp.dot(jnp.exp2(arg_ref[p]).astype(BF16), vals, preferred_element_type=F32)
        carry_ref[s] = ctot_ref[p]

    def finish(i, s, cmin):
        def cond(state):
            pos, cmin = state
            return (pos > -kb) & (cmin <= ATTN_STOP)

        def older(state):
            pos, _ = state
            start = pl.multiple_of(jnp.maximum(pos, 0), tq)
            keys = k_ref[pl.ds(start, kb), :]
            vals = v_ref[pl.ds(start, kb), :]
            fresh = lax.broadcasted_iota(jnp.int32, (1, kb), 1) < pos + kb - start
            nlog_nb, log_b = scores(stacked_queries(i), keys)
            nlog_nb = jnp.where(fresh, nlog_nb, 0.0)
            excl, tot = suffix(nlog_nb)
            carry = carry_ref[s]
            a = jnp.where(fresh, jnp.exp2(log_b - excl - carry), 0.0)
            acc_ref[s] += jnp.dot(a.astype(BF16), vals, preferred_element_type=F32)
            carry_ref[s] = carry + tot
            return pos - kb, jnp.min(carry + tot)

        lax.while_loop(cond, older, (jnp.asarray(window_start(i) - kb, jnp.int32), cmin))
        store(i, acc_ref[s])

    for i in range(first):
        stage_scores(i, 0)
        cmin = stage_exponents(0, i * tq - window_start(i))
        stage_output(i, 0, 0)
        finish(i, 0, cmin)

    n_pipe = seq // tq - first
    unroll = acc_ref.shape[0]
    lag = z_ref.shape[0] - 1
    n_slots = lag + 1
    pending = []
    for tau in range(2 * lag):
        stage_scores(first + tau, tau % n_slots)
        if tau >= lag:
            pending.append(stage_exponents((tau - lag) % n_slots, lookback))

    def steady(m, pending):
        pending = list(pending)
        tau0 = 2 * lag + unroll * m
        done = []
        for u in range(unroll):
            stage_output(first + tau0 + u - 2 * lag, u % n_slots, u)
            done.append((first + tau0 + u - 2 * lag, u, pending.pop(0)))
            pending.append(stage_exponents((lag + u) % n_slots, lookback))
            stage_scores(first + tau0 + u, (2 * lag + u) % n_slots)
        for block, s, cmin in done:
            finish(block, s, cmin)
        return tuple(pending)

    pending = list(lax.fori_loop(0, (n_pipe - 2 * lag) // unroll, steady, tuple(pending)))
    for tau in range(n_pipe, n_pipe + 2 * lag):
        j = tau - 2 * lag
        stage_output(first + j, j % n_slots, 0)
        cmin = pending.pop(0)
        if tau - lag < n_pipe:
            pending.append(stage_exponents((tau - lag) % n_slots, lookback))
        finish(first + j, 0, cmin)


def _attention(q, k, v, batch, seq):
    n, width = q.shape
    tq, kb, nsub = ATTN_QUERY_ROWS, LANES, ATTN_WINDOW_BLOCKS
    n_blocks = seq // tq
    lag, n_slots = ATTN_STAGE_LAG, ATTN_STAGE_LAG + 1
    clipped = -(-(nsub * kb - tq) // tq)
    unroll = first = None
    for u in range(ATTN_UNROLL - ATTN_UNROLL % n_slots, 0, -n_slots):
        for f in range(clipped, clipped + u):
            rest = n_blocks - f - 2 * lag
            if rest >= 0 and rest % u == 0 and first is None:
                unroll, first = u, f
    assert first is not None and seq % tq == 0 and seq >= nsub * kb
    blk = pl.BlockSpec((seq, LANES), lambda b, hp: (b, hp))
    rows = HEADS_PER_BLOCK * tq
    stage_buf = pltpu.VMEM((n_slots, rows, nsub * kb), F32)
    carry_buf = pltpu.VMEM((n_slots, rows, kb), F32)
    row_buf = pltpu.VMEM((unroll, rows, LANES), F32)
    return pl.pallas_call(
        functools.partial(_attn_kernel, tq=tq, kb=kb, nsub=nsub, first=first),
        grid=(batch, width // LANES),
        in_specs=[blk, blk, blk],
        out_specs=blk,
        out_shape=jax.ShapeDtypeStruct((n, width), F32),
        scratch_shapes=[pltpu.VMEM((kb, 2 * kb), BF16),
                        stage_buf, stage_buf, carry_buf, row_buf, row_buf],
        compiler_params=_cparams(("arbitrary", "arbitrary")),
        name="attn",
    )(q, k, v)


def _gelu_tanh(x):
    c = math.sqrt(2.0 / math.pi)
    half_x = 0.5 * x
    return half_x + half_x * jnp.tanh(x * (c + (c * 0.044715) * (x * x)))


def _sigmoid(x):
    return 0.5 + 0.5 * jnp.tanh(0.5 * x)


def _lru_kernel(*refs, ts, n_slab):
    xl_refs, gl_refs = refs[0:n_slab], refs[n_slab:2 * n_slab]
    cw_ref, cb_ref, wr_ref, br_ref, wi_ref, bi_ref, lam_ref = refs[2 * n_slab:2 * n_slab + 7]
    o_refs = refs[2 * n_slab + 7:3 * n_slab + 7]
    tail_ref, a7_ref, u7_ref, hp_ref, pa_ref, pu_ref, h_ref = refs[3 * n_slab + 7:]
    t = pl.program_id(1)
    groups = ts // SUBLANES

    @pl.when(t == 0)
    def _():
        tail_ref[...] = jnp.zeros_like(tail_ref)
        h_ref[...] = jnp.zeros_like(h_ref)

    first_group = lax.broadcasted_iota(jnp.int32, (groups, LANES), 0) == 0
    for c in range(n_slab):
        lanes = slice(c * LANES, (c + 1) * LANES)
        x = [xl_refs[c][pl.ds(s, groups, stride=SUBLANES), :] for s in range(SUBLANES)]
        shifted = {}
        for s in range(SUBLANES - (CONV_W - 1), SUBLANES):
            shifted[s] = jnp.where(first_group, tail_ref[c, s:s + 1, :], pltpu.roll(x[s], 1, axis=0))
            tail_ref[c, s:s + 1, :] = x[s][groups - 1:groups, :]
        conv = []
        for s in range(SUBLANES):
            y = cb_ref[:, lanes]
            for w in range(CONV_W):
                j = s - (CONV_W - 1) + w
                y = y + (x[j] if j >= 0 else shifted[j + SUBLANES]) * cw_ref[w:w + 1, lanes]
            conv.append(y)
        xc = jnp.concatenate(conv, axis=0)

        xcb = xc.astype(BF16)
        r = _sigmoid(jnp.dot(xcb, wr_ref[c], preferred_element_type=F32) + br_ref[:, lanes])
        ig = _sigmoid(jnp.dot(xcb, wi_ref[c], preferred_element_type=F32) + bi_ref[:, lanes])
        lam = lam_ref[:, lanes]
        log_sig_lam = -(jnp.maximum(-lam, 0.0) + jnp.log1p(jnp.exp(-jnp.abs(lam))))
        log_a = RG_C * r * log_sig_lam
        a = jnp.exp(log_a)
        th = jnp.tanh(log_a)
        one_m_a2 = -2.0 * th / (1.0 - th)
        root = jnp.where(one_m_a2 > 0.0, one_m_a2 * lax.rsqrt(one_m_a2), 0.0)
        u = root * (ig * xc)

        a_run = u_run = None
        for s in range(SUBLANES):
            rows = slice(s * groups, (s + 1) * groups)
            if s == 0:
                a_run, u_run = a[rows], u[rows]
            else:
                u_run = a[rows] * u_run + u[rows]
                a_run = a_run * a[rows]
            pa_ref[c, rows, :] = a_run
            pu_ref[c, rows, :] = u_run
        a7_ref[c] = a_run
        u7_ref[c] = u_run

    def group(g, hs):
        nxt = []
        for c in range(n_slab):
            hp_ref[c, pl.ds(g, 1), :] = hs[c]
            nxt.append(a7_ref[c, pl.ds(g, 1), :] * hs[c] + u7_ref[c, pl.ds(g, 1), :])
        return tuple(nxt)

    hs = lax.fori_loop(0, groups, group, tuple(h_ref[c] for c in range(n_slab)), unroll=8)
    for c in range(n_slab):
        h_ref[c] = hs[c]

    for c in range(n_slab):
        h_in = hp_ref[c]
        for s in range(SUBLANES):
            rows = slice(s * groups, (s + 1) * groups)
            hseq = pu_ref[c, rows, :] + pa_ref[c, rows, :] * h_in
            gate = _gelu_tanh(gl_refs[c][pl.ds(s, groups, stride=SUBLANES), :])
            o_refs[c][pl.ds(s, groups, stride=SUBLANES), :] = hseq * gate


def _lru(xl, gl, conv_w, conv_b, wr_bd, br, wi_bd, bi, lam, batch, seq, ts):
    n, width = xl.shape
    nt = seq // ts
    n_slab = width // LANES
    assert wr_bd.shape == (n_slab, LANES, LANES) and ts % (SUBLANES * SUBLANES) == 0
    groups = ts // SUBLANES
    slab = [pl.BlockSpec((ts, LANES), functools.partial(lambda b, t, c: (b * nt + t, c), c=c))
            for c in range(n_slab)]
    const2 = lambda b, t: (0, 0)
    const3 = lambda b, t: (0, 0, 0)
    vec = pl.BlockSpec((1, width), const2)
    per_group = pltpu.VMEM((n_slab, groups, LANES), F32)
    per_step = pltpu.VMEM((n_slab, ts, LANES), F32)
    return pl.pallas_call(
        functools.partial(_lru_kernel, ts=ts, n_slab=n_slab),
        grid=(batch, nt),
        in_specs=slab + slab + [pl.BlockSpec((CONV_W, width), const2), vec,
                                pl.BlockSpec(wr_bd.shape, const3), vec,
                                pl.BlockSpec(wi_bd.shape, const3), vec, vec],
        out_specs=[pl.BlockSpec((ts, LANES), lambda b, t: (b * nt + t, 0))] * n_slab,
        out_shape=[jax.ShapeDtypeStruct((n, LANES), F32)] * n_slab,
        scratch_shapes=[pltpu.VMEM((n_slab, SUBLANES, LANES), F32),
                        per_group, per_group, per_group, per_step, per_step,
                        pltpu.VMEM((n_slab, 1, LANES), F32)],
        compiler_params=_cparams(("arbitrary", "arbitrary")),
        name="lru",
    )(*([xl] * n_slab), *([gl] * n_slab), conv_w, conv_b, wr_bd, br, wi_bd, bi, lam)


def _mix_route_kernel(sb_ref, *refs, tm, n_slab):
    lru_refs = refs[:n_slab]
    (x_ref, sbg_ref, lrug_ref, wo_ref, n2g_ref, wrs_ref, rb_ref,
     h_ref, u2_ref, rr_ref, rwt_ref, tc_ref, before_ref, u2s_ref, wo_bf) = refs[n_slab:]

    @pl.when(pl.program_id(0) == 0)
    def _():
        wo_bf[...] = wo_ref[...].astype(BF16)

    _mix_route_body(sb_ref, lru_refs, x_ref, sbg_ref, lrug_ref, wo_bf, n2g_ref, wrs_ref, rb_ref,
                    h_ref, u2_ref, rr_ref, rwt_ref, tc_ref, before_ref, u2s_ref, tm)


def _mix_route_body(sb_ref, lru_refs, x_ref, sbg_ref, lrug_ref, wo_ref, n2g_ref, wrs_ref, rb_ref,
                    h_ref, u2_ref, rr_ref, rwt_ref, tc_ref, before_ref, u2s_ref, tm):
    step = pl.program_id(0)
    half = sb_ref.shape[1]
    n_sub = x_ref.shape[0] // tm

    @pl.when(step == 0)
    def _():
        r_id = lax.broadcasted_iota(jnp.int32, (tm, tm), 0)
        c_id = lax.broadcasted_iota(jnp.int32, (tm, tm), 1)
        before_ref[...] = (r_id < c_id).astype(BF16)
        tc_ref[...] = jnp.zeros_like(tc_ref)
        u2s_ref[...] = jnp.zeros_like(u2s_ref)

    def route(j):
        rows = slice(j * tm, (j + 1) * tm)
        u2 = u2s_ref[rows, :]
        u_hi = u2.astype(BF16)
        u_lo = (u2 - u_hi.astype(F32)).astype(BF16)
        nt_dims = (((1,), (1,)), ((), ()))
        n_rows = rb_ref.shape[0]
        both = lax.dot_general(wrs_ref[...], u_hi, nt_dims, preferred_element_type=F32)
        lt = (both[0:n_rows] + both[n_rows:2 * n_rows]
              + lax.dot_general(wrs_ref[0:n_rows, :], u_lo, nt_dims, preferred_element_type=F32)
              + rb_ref[...])

        sub = lax.broadcasted_iota(jnp.int32, (SUBLANES, tm), 0)

        def top1(x):
            m = jnp.max(x, axis=0, keepdims=True)
            idx = jnp.min(jnp.where(x == m, sub, SUBLANES), axis=0, keepdims=True)
            return m, idx

        grp = lt[0:SUBLANES, :]
        g_max, g_idx = top1(grp)
        g_p = 1.0 / jnp.sum(jnp.exp(grp - g_max), axis=0, keepdims=True)
        fine = lt[SUBLANES:2 * SUBLANES, :]
        for g in range(1, N_GROUPS):
            fine = jnp.where(g_idx == g, lt[(g + 1) * SUBLANES:(g + 2) * SUBLANES, :], fine)
        m1, i1 = top1(fine)
        m2, i2 = top1(jnp.where(sub == i1, -jnp.inf, fine))
        e2 = jnp.exp(m2 - m1)
        p1 = 1.0 / (1.0 + e2)
        w1 = g_p * p1
        w2 = g_p * (e2 * p1)
        x1 = g_idx * EXPERTS_PER_GROUP + i1
        x2 = g_idx * EXPERTS_PER_GROUP + i2

        eid = lax.broadcasted_iota(jnp.int32, (N_EXPERTS, tm), 0)
        oh1 = jnp.where(eid == x1, 1.0, 0.0)
        oh2 = jnp.where(eid == x2, 1.0, 0.0)
        pre = jnp.dot(jnp.concatenate([oh1, oh2], axis=0).astype(BF16), before_ref[...],
                      preferred_element_type=F32)
        pre1, pre2 = pre[0:N_EXPERTS], pre[N_EXPERTS:2 * N_EXPERTS]
        cnt1 = jnp.sum(oh1, axis=1, keepdims=True)
        cnt2 = jnp.sum(oh2, axis=1, keepdims=True)
        seg8 = jnp.floor((cnt1 + cnt2 + (SUBLANES - 1.0)) * (1.0 / SUBLANES))
        e_r = lax.broadcasted_iota(jnp.int32, (N_EXPERTS, N_EXPERTS), 0)
        e_c = lax.broadcasted_iota(jnp.int32, (N_EXPERTS, N_EXPERTS), 1)
        lower = jnp.where(e_c < e_r, 1.0, 0.0).astype(BF16)
        seg8_b = jnp.broadcast_to(seg8, (N_EXPERTS, LANES)).astype(BF16)
        seg_off = SUBLANES * jnp.dot(lower, seg8_b, preferred_element_type=F32)[:, 0:1]
        pos1 = jnp.sum(oh1 * (pre1 + seg_off), axis=0, keepdims=True)
        pos2 = jnp.sum(oh2 * (pre2 + (seg_off + cnt1)), axis=0, keepdims=True)

        lane = lax.broadcasted_iota(jnp.int32, tc_ref.shape, 1)
        seg_rows = jnp.broadcast_to(seg8 * SUBLANES, tc_ref.shape).astype(jnp.int32)
        tile = (step - 1) * n_sub + j
        tc_ref[...] = jnp.where(lane == tile, seg_rows, tc_ref[...])

        zrow = jnp.zeros((SUBLANES - 4, tm), jnp.int32)
        rr_ref[:, rows] = jnp.concatenate(
            [pos1.astype(jnp.int32), pos2.astype(jnp.int32), x1, x2, zrow], axis=0)
        wt = jnp.concatenate([w1, w2, pos1, pos2, jnp.zeros((LANES - 4, tm), F32)], axis=0)
        rwt_ref[rows, :] = wt.T

    m_sb = _rms_f32(sb_ref[...], sbg_ref[...]).astype(BF16)
    lru = jnp.concatenate([r[...] for r in lru_refs], axis=-1)
    m_lru = _rms_f32(lru, lrug_ref[...]).astype(BF16)
    h_ref[...] = (x_ref[...]
                  + jnp.dot(m_sb, wo_ref[0:half, :], preferred_element_type=F32)
                  + jnp.dot(m_lru, wo_ref[half:2 * half, :], preferred_element_type=F32))

    for j in range(n_sub):
        route(j)

    u2_next = _rms_f32(h_ref[...], n2g_ref[...])
    u2_ref[...] = u2_next.astype(BF16)
    u2s_ref[...] = u2_next


def _mix_route(sb, lru, x2, sbg, lrug, w_out, n2g, wr_stack, rbias, tm, tp):
    n, d = x2.shape
    half = sb.shape[1]
    n_tiles = n // tp
    row = lambda i: (jnp.minimum(i, n_tiles - 1), 0)
    routed = lambda i: (jnp.maximum(i - 1, 0), 0)
    const = lambda i: (0, 0)
    return pl.pallas_call(
        functools.partial(_mix_route_kernel, tm=tm, n_slab=len(lru)),
        grid=(n_tiles + 1,),
        in_specs=[pl.BlockSpec((tp, half), row)] + [pl.BlockSpec((tp, LANES), row)] * len(lru)
                 + [pl.BlockSpec((tp, d), row),
                  pl.BlockSpec((1, half), const), pl.BlockSpec((1, half), const),
                  pl.BlockSpec(w_out.shape, const, pipeline_mode=pl.Buffered(1)),
                  pl.BlockSpec((1, d), const),
                  pl.BlockSpec(wr_stack.shape, const), pl.BlockSpec(rbias.shape, const)],
        out_specs=[pl.BlockSpec((tp, d), row), pl.BlockSpec((tp, d), row),
                   pl.BlockSpec((SUBLANES, tp), lambda i: (0, jnp.maximum(i - 1, 0))),
                   pl.BlockSpec((tp, LANES), routed),
                   pl.BlockSpec((N_EXPERTS, LANES), const)],
        out_shape=[jax.ShapeDtypeStruct((n, d), F32), jax.ShapeDtypeStruct((n, d), BF16),
                   jax.ShapeDtypeStruct((SUBLANES, n), jnp.int32),
                   jax.ShapeDtypeStruct((n, LANES), F32),
                   jax.ShapeDtypeStruct((N_EXPERTS, LANES), jnp.int32)],
        scratch_shapes=[pltpu.VMEM((tm, tm), BF16), pltpu.VMEM((tp, d), F32),
                        pltpu.VMEM(w_out.shape, BF16)],
        compiler_params=_cparams(("arbitrary",)),
        name="mix_route",
    )(sb, *lru, x2, sbg, lrug, w_out, n2g, wr_stack, rbias)


def _pack_halves(x):
    half = x.shape[1] // 2
    lo = lax.shift_right_logical(lax.bitcast_convert_type(x[:, :half], jnp.int32), 16)
    hi = lax.bitcast_convert_type(x[:, half:], jnp.int32) & HIGH_HALF
    return hi | lo


def _unpack_halves(p):
    lo = lax.bitcast_convert_type(lax.shift_left(p, 16), F32)
    hi = lax.bitcast_convert_type(p & HIGH_HALF, F32)
    return lo.astype(BF16), hi.astype(BF16)


def _segment_copies(tile, c8_ref, loff_ref, goff_ref, make):
    for e in range(N_EXPERTS):
        idx = tile * N_EXPERTS + e
        rows = pl.multiple_of(c8_ref[idx], SUBLANES)

        @pl.when(rows > 0)
        def _(idx=idx, rows=rows):
            lo = pl.multiple_of(loff_ref[idx], SUBLANES)
            go = pl.multiple_of(goff_ref[idx], SUBLANES)
            make(pl.ds(lo, rows), pl.ds(go, rows)).start()


def _dispatch_kernel(c8_ref, loff_ref, goff_ref, tot_ref, used_ref,
                     rr_ref, u2_ref, xs_ref, lbuf, zbuf, sem, zsem, *, td, lrows):
    i = pl.program_id(0)
    slot = i % 2

    r_id = lax.broadcasted_iota(jnp.int32, (lrows, td), 0)
    perm = jnp.where(r_id == rr_ref[0:1, :], 1.0, jnp.where(r_id == rr_ref[1:2, :], 1.0, 0.0))
    sorted_rows = jnp.dot(perm.astype(BF16), u2_ref[...], preferred_element_type=F32)
    lbuf[slot] = _pack_halves(sorted_rows)

    @pl.when(i == 0)
    def _():
        zbuf[...] = jnp.zeros_like(zbuf)
        chunk = zbuf.shape[0]
        used = used_ref[0]
        spare = xs_ref.shape[0] - used
        n_fill = (spare + chunk - 1) // chunk

        def fill_copy(k):
            rows = pl.multiple_of(jnp.minimum(chunk, spare - k * chunk), SUBLANES)
            start = pl.multiple_of(used + k * chunk, SUBLANES)
            return pltpu.make_async_copy(zbuf.at[pl.ds(0, rows)], xs_ref.at[pl.ds(start, rows)],
                                         zsem)

        def fill_start(k, c):
            fill_copy(k).start()
            return c

        def fill_wait(k, c):
            fill_copy(k).wait()
            return c

        lax.fori_loop(0, n_fill, fill_start, 0)
        lax.fori_loop(0, n_fill, fill_wait, 0)

    def wait_tile(tile, s):
        rows = pl.multiple_of(tot_ref[tile], SUBLANES)
        pltpu.make_async_copy(lbuf.at[s, pl.ds(0, rows)], xs_ref.at[pl.ds(0, rows)], sem).wait()

    @pl.when(i > 0)
    def _():
        wait_tile(i - 1, 1 - slot)

    _segment_copies(i, c8_ref, loff_ref, goff_ref,
                    lambda loc, glob: pltpu.make_async_copy(lbuf.at[slot, loc], xs_ref.at[glob], sem))

    @pl.when(i == pl.num_programs(0) - 1)
    def _():
        wait_tile(i, slot)


def _dispatch(c8, loff, goff, tot, used, rr, u2, p_rows, td, lrows, tme):
    n, d = u2.shape
    pmap = lambda i, *_: (0, i)
    return pl.pallas_call(
        functools.partial(_dispatch_kernel, td=td, lrows=lrows),
        grid_spec=pltpu.PrefetchScalarGridSpec(
            num_scalar_prefetch=5,
            grid=(n // td,),
            in_specs=[pl.BlockSpec((SUBLANES, td), pmap),
                      pl.BlockSpec((td, d), lambda i, *_: (i, 0))],
            out_specs=pl.BlockSpec(memory_space=pl.ANY),
            scratch_shapes=[pltpu.VMEM((2, lrows, d // 2), jnp.int32),
                            pltpu.VMEM((tme, d // 2), jnp.int32),
                            pltpu.SemaphoreType.DMA(()), pltpu.SemaphoreType.DMA(())]),
        out_shape=jax.ShapeDtypeStruct((p_rows, d // 2), jnp.int32),
        compiler_params=_cparams(("arbitrary",)),
        name="dispatch",
    )(c8, loff, goff, tot, used, rr, u2)


def _experts_kernel(eoff_ref, erows_ref, xs_ref, wg_ref, wu_ref, wd_ref, ys_ref,
                    wg_bf, wu_bf, wd_bf, xbuf, ybuf, sem_in, sem_out, state, *, tme):
    e = pl.program_id(0)
    n_experts = pl.num_programs(0)
    rows = erows_ref[e]
    off = eoff_ref[e]
    n_tiles = (rows + tme - 1) // tme
    nxt = jnp.minimum(e + 1, n_experts - 1)
    prefetch = (rows > 0) & (e + 1 < n_experts) & (erows_ref[nxt] > 0)

    def tile_rows(total, k):
        return pl.multiple_of(jnp.minimum(tme, total - k * tme), SUBLANES)

    def in_copy(start, r, slot):
        start = pl.multiple_of(start, SUBLANES)
        return pltpu.make_async_copy(xs_ref.at[pl.ds(start, r)], xbuf.at[slot, pl.ds(0, r)],
                                     sem_in.at[slot])

    def out_copy(start, r, slot):
        start = pl.multiple_of(start, SUBLANES)
        return pltpu.make_async_copy(ybuf.at[slot, pl.ds(0, r)], ys_ref.at[pl.ds(start, r)],
                                     sem_out.at[slot])

    @pl.when(e == 0)
    def _():
        for s in range(4):
            state[s] = 0
        xbuf[...] = jnp.zeros_like(xbuf)

    @pl.when(rows > 0)
    def _():
        wg_bf[...] = wg_ref[0].astype(BF16)
        wu_bf[...] = wu_ref[0].astype(BF16)
        wd_bf[...] = wd_ref[0].astype(BF16)
        done = state[0]

        @pl.when(state[1] == 0)
        def _():
            in_copy(off, tile_rows(rows, 0), done % 2).start()

        def tile(k, c):
            slot = (done + k) % 2
            r = tile_rows(rows, k)

            @pl.when(k + 1 < n_tiles)
            def _():
                in_copy(off + (k + 1) * tme, tile_rows(rows, k + 1), 1 - slot).start()

            @pl.when((k + 1 == n_tiles) & prefetch)
            def _():
                in_copy(eoff_ref[nxt], tile_rows(erows_ref[nxt], 0), 1 - slot).start()

            in_copy(off, r, slot).wait()

            @pl.when(done + k >= 2)
            def _():
                out_copy(0, pl.multiple_of(state[2 + slot], SUBLANES), slot).wait()

            def mlp(start, n):
                x = jnp.concatenate(_unpack_halves(xbuf[slot, start:start + n]), axis=1)
                hg = jnp.dot(x, wg_bf[...], preferred_element_type=F32)
                hu = jnp.dot(x, wu_bf[...], preferred_element_type=F32)
                act = (hg * jax.nn.sigmoid(hg) * hu).astype(BF16)
                y = jnp.dot(act, wd_bf[...], preferred_element_type=F32)
                ybuf[slot, start:start + n] = _pack_halves(y.astype(BF16).astype(F32))

            for below, n in zip((0,) + EXPERT_TILE_SIZES, EXPERT_TILE_SIZES):
                @pl.when((r > below) & (r <= n))
                def _(n=n):
                    mlp(0, n)

            out_copy(off + k * tme, r, slot).start()
            state[2 + slot] = r
            return c

        lax.fori_loop(0, n_tiles, tile, 0)
        state[0] = done + n_tiles

    state[1] = prefetch.astype(jnp.int32)

    @pl.when(e == n_experts - 1)
    def _():
        total = eoff_ref[e] + rows
        for slot in range(2):
            @pl.when(state[0] > slot)
            def _(slot=slot):
                out_copy(0, pl.multiple_of(state[2 + slot], SUBLANES), slot).wait()
        ybuf[0] = jnp.zeros(ybuf.shape[1:], ybuf.dtype)
        spare = ys_ref.shape[0] - total
        n_fill = (spare + tme - 1) // tme

        def fill_copy(k):
            return out_copy(total + k * tme, tile_rows(spare, k), 0)

        def fill_start(k, c):
            fill_copy(k).start()
            return c

        def fill_wait(k, c):
            fill_copy(k).wait()
            return c

        lax.fori_loop(0, n_fill, fill_start, 0)
        lax.fori_loop(0, n_fill, fill_wait, 0)


def _experts(eoff, erows, xs, wg, wu, wd, tme):
    p = xs.shape[0]
    n_experts, d, de = wg.shape
    wmap = lambda e, *_: (e, 0, 0)
    return pl.pallas_call(
        functools.partial(_experts_kernel, tme=tme),
        grid_spec=pltpu.PrefetchScalarGridSpec(
            num_scalar_prefetch=2,
            grid=(n_experts,),
            in_specs=[pl.BlockSpec(memory_space=pl.ANY),
                      pl.BlockSpec((1, d, de), wmap), pl.BlockSpec((1, d, de), wmap),
                      pl.BlockSpec((1, de, d), wmap)],
            out_specs=pl.BlockSpec(memory_space=pl.ANY),
            scratch_shapes=[pltpu.VMEM((d, de), BF16), pltpu.VMEM((d, de), BF16),
                            pltpu.VMEM((de, d), BF16),
                            pltpu.VMEM((2, tme, d // 2), jnp.int32),
                            pltpu.VMEM((2, tme, d // 2), jnp.int32),
                            pltpu.SemaphoreType.DMA((2,)), pltpu.SemaphoreType.DMA((2,)),
                            pltpu.SMEM((4,), jnp.int32)]),
        out_shape=jax.ShapeDtypeStruct((p, d // 2), jnp.int32),
        compiler_params=_cparams(("arbitrary",)),
        name="experts",
    )(eoff, erows, xs, wg, wu, wd)


def _combine_kernel(c8_ref, loff_ref, goff_ref, tot_ref,
                    rwt_ref, h_ref, fg_ref, ys_ref, y_ref, ybuf, sems, *, tc, lrows):
    i = pl.program_id(0)
    slot = i % 2

    def gather_tile(tile, s):
        _segment_copies(tile, c8_ref, loff_ref, goff_ref,
                        lambda loc, glob: pltpu.make_async_copy(ys_ref.at[glob], ybuf.at[s, loc],
                                                                sems.at[s]))

    @pl.when(i == 0)
    def _():
        ybuf[...] = jnp.zeros_like(ybuf)
        gather_tile(0, 0)

    @pl.when(i + 1 < pl.num_programs(0))
    def _():
        gather_tile(i + 1, 1 - slot)

    rows = pl.multiple_of(tot_ref[i], SUBLANES)
    pltpu.make_async_copy(ys_ref.at[pl.ds(0, rows)], ybuf.at[slot, pl.ds(0, rows)],
                          sems.at[slot]).wait()

    w = rwt_ref[...]
    c_id = lax.broadcasted_iota(jnp.int32, (tc, lrows), 1)
    pos1 = w[:, 2:3].astype(jnp.int32)
    pos2 = w[:, 3:4].astype(jnp.int32)
    wmat = jnp.where(c_id == pos1, w[:, 0:1], jnp.where(c_id == pos2, w[:, 1:2], 0.0)).astype(BF16)
    half = h_ref.shape[1] // 2
    outs, sumsq = [], 0.0
    for part, cols in zip(_unpack_halves(ybuf[slot]), (slice(0, half), slice(half, 2 * half))):
        out = h_ref[:, cols] + jnp.dot(wmat, part, preferred_element_type=F32)
        sumsq = sumsq + jnp.sum(out * out, axis=-1, keepdims=True)
        outs.append(out)
    scale = lax.rsqrt(sumsq * (1.0 / (2 * half)) + EPS)
    for out, cols in zip(outs, (slice(0, half), slice(half, 2 * half))):
        y_ref[:, cols] = out * scale * fg_ref[:, cols]


def _combine(c8, loff, goff, tot, rwt, h, final_g, ys, tc, lrows):
    n, d = h.shape
    return pl.pallas_call(
        functools.partial(_combine_kernel, tc=tc, lrows=lrows),
        grid_spec=pltpu.PrefetchScalarGridSpec(
            num_scalar_prefetch=4,
            grid=(n // tc,),
            in_specs=[pl.BlockSpec((tc, LANES), lambda i, *_: (i, 0)),
                      pl.BlockSpec((tc, d), lambda i, *_: (i, 0)),
                      pl.BlockSpec((1, d), lambda i, *_: (0, 0)),
                      pl.BlockSpec(memory_space=pl.ANY)],
            out_specs=pl.BlockSpec((tc, d), lambda i, *_: (i, 0)),
            scratch_shapes=[pltpu.VMEM((2, lrows, d // 2), jnp.int32),
                            pltpu.SemaphoreType.DMA((2,))]),
        out_shape=jax.ShapeDtypeStruct((n, d), F32),
        compiler_params=_cparams(("arbitrary",)),
        name="combine",
    )(c8, loff, goff, tot, rwt, h, final_g, ys)


def _block_diag(w, per):
    nb, c, _ = w.shape
    eye = jnp.eye(per, dtype=w.dtype)
    wg = w.reshape(nb // per, per, c, c)
    return jnp.einsum("gpij,pq->gpiqj", wg, eye).reshape(nb // per, per * c, per * c)


def _router_tables(w_group, b_group, w_fine, b_fine):
    d = w_group.shape[0]
    w = jnp.zeros((ROUTER_ROWS, d), F32)
    w = w.at[0:N_GROUPS].set(w_group.T).at[SUBLANES:SUBLANES + N_EXPERTS].set(w_fine.T)
    b = jnp.full((ROUTER_ROWS,), NEG_BIG, F32)
    b = b.at[0:N_GROUPS].set(b_group).at[SUBLANES:SUBLANES + N_EXPERTS].set(b_fine)
    w_hi = w.astype(BF16)
    w_lo = (w - w_hi.astype(F32)).astype(BF16)
    return jnp.concatenate([w_hi, w_lo], axis=0), b.reshape(ROUTER_ROWS, 1)


def kernel(x, norm1_g, w_in, conv_w, conv_b, w_rgate, b_rgate, w_igate, b_igate, lam, sb_norm_g,
           lru_norm_g, w_out, norm2_g, w_group, b_group, w_fine, b_fine, w_e_gate, w_e_up,
           w_e_down, final_g):
    batch, seq, d = x.shape
    n = batch * seq
    width = w_in.shape[1] // 5
    tm = min(TOKEN_TILE, seq)
    tp = min(ROW_TILE, seq)
    tme = EXPERT_TILE_ROWS
    gate_per = LANES // w_rgate.shape[1]

    x2 = x.reshape(n, d)
    vec = lambda a: a.reshape(1, -1)

    q, k, v, xl, gl = _in_proj(x2, vec(norm1_g), w_in, width, tp)
    out_sb = _attention(q, k, v, batch, seq)
    out_lru = _lru(xl, gl, conv_w, vec(conv_b),
                   _block_diag(w_rgate, gate_per).astype(BF16), vec(b_rgate),
                   _block_diag(w_igate, gate_per).astype(BF16), vec(b_igate),
                   vec(lam), batch, seq, tp)

    wr_stack, rbias = _router_tables(w_group, b_group, w_fine, b_fine)
    h, u2, rr, rwt, tcnt = _mix_route(out_sb, out_lru, x2, vec(sb_norm_g), vec(lru_norm_g),
                                      w_out, vec(norm2_g), wr_stack, rbias, tm, tp)

    n_tiles = n // tm
    assert n_tiles <= LANES, "one lane of the per-tile count table per token tile"
    c8 = tcnt[:, :n_tiles].T
    erows = jnp.sum(c8, axis=0)
    eoff = jnp.cumsum(erows) - erows
    goff = eoff[None, :] + jnp.cumsum(c8, axis=0) - c8
    loff = jnp.cumsum(c8, axis=1) - c8
    tot = jnp.sum(c8, axis=1)
    lrows = 2 * tm + N_EXPERTS * SUBLANES
    p_rows = 2 * n + n_tiles * N_EXPERTS * (SUBLANES - 1)
    p_rows = -(-p_rows // SUBLANES) * SUBLANES
    i32 = lambda a: a.reshape(-1).astype(jnp.int32)
    c8, loff, goff, tot, eoff, erows = (i32(a) for a in (c8, loff, goff, tot, eoff, erows))

    xs = _dispatch(c8, loff, goff, tot, jnp.sum(erows, keepdims=True), rr, u2, p_rows, tm, lrows, tme)
    ys = _experts(eoff, erows, xs, w_e_gate, w_e_up, w_e_down, tme)
    y = _combine(c8, loff, goff, tot, rwt, h, vec(final_g), ys, tm, lrows)
    return y.reshape(batch, seq, d)
```

```python
import functools
import math

import jax
import jax.numpy as jnp
from jax import lax
from jax.experimental import pallas as pl
from jax.experimental.pallas import tpu as pltpu

F32 = jnp.float32
BF16 = jnp.bfloat16

EPS = 1e-6
HEAD_DIM = 64
HEADS_PER_BLOCK = 2
LANES = 128
SUBLANES = 8
CONV_W = 4
RG_C = 8.0
N_GROUPS = 4
EXPERTS_PER_GROUP = 8
N_EXPERTS = N_GROUPS * EXPERTS_PER_GROUP
ROUTER_ROWS = 48
NEG_BIG = -1e30
LOG2_E = math.log2(math.e)
ATTN_STOP = 104.0 * LOG2_E
ATTN_QUERY_ROWS = 64
ATTN_WINDOW_BLOCKS = 2
ATTN_UNROLL = 24
ATTN_STAGE_LAG = 2
HIGH_HALF = -65536
EXPERT_TILE_ROWS = 1536
EXPERT_TILE_SIZES = (256, 512, 768, 1024, 1280, 1536)
TOKEN_TILE = 512
ROW_TILE = 1024

VMEM_LIMIT = 56 * 1024 * 1024


def _cparams(sem):
    return pltpu.CompilerParams(dimension_semantics=sem, vmem_limit_bytes=VMEM_LIMIT)


def _rms_f32(x, g):
    return x * lax.rsqrt(jnp.mean(x * x, axis=-1, keepdims=True) + EPS) * g


def _in_proj_kernel(x_ref, g_ref, w_ref, q_ref, k_ref, v_ref, xl_ref, gl_ref, w_bf, *,
                    width, q_scale):
    @pl.when(pl.program_id(0) == 0)
    def _():
        for c in range(w_ref.shape[1] // width):
            cols = slice(c * width, (c + 1) * width)
            w_bf[:, cols] = w_ref[:, cols].astype(BF16)

    u = _rms_f32(x_ref[...], g_ref[...]).astype(BF16)
    for c, o_ref in enumerate((q_ref, k_ref, v_ref, xl_ref, gl_ref)):
        p = jnp.dot(u, w_bf[:, c * width:(c + 1) * width], preferred_element_type=F32)
        if c == 0:
            p = p * q_scale
        o_ref[...] = p.astype(o_ref.dtype)


def _in_proj(x2, g, w, width, tm):
    n, d = x2.shape
    row = lambda i: (i, 0)
    out_bf = jax.ShapeDtypeStruct((n, width), BF16)
    out_f = jax.ShapeDtypeStruct((n, width), F32)
    return pl.pallas_call(
        functools.partial(_in_proj_kernel, width=width, q_scale=1.0 / math.sqrt(HEAD_DIM)),
        grid=(n // tm,),
        in_specs=[pl.BlockSpec((tm, d), row),
                  pl.BlockSpec((1, d), lambda i: (0, 0)),
                  pl.BlockSpec(w.shape, lambda i: (0, 0), pipeline_mode=pl.Buffered(1))],
        out_specs=[pl.BlockSpec((tm, width), row)] * 5,
        out_shape=[out_bf, out_bf, out_bf, out_f, out_f],
        scratch_shapes=[pltpu.VMEM(w.shape, BF16)],
        compiler_params=_cparams(("arbitrary",)),
        name="in_proj",
    )(x2, g, w)


def _attn_kernel(q_ref, k_ref, v_ref, o_ref, tri_ref, z_ref, arg_ref, ctot_ref, acc_ref, carry_ref,
                 *, tq, kb, nsub, first):
    seq = q_ref.shape[0]
    win = nsub * kb
    lookback = win - tq
    lane = lax.broadcasted_iota(jnp.int32, (1, LANES), 1)
    rel = (lax.broadcasted_iota(jnp.int32, (tq, kb), 1)
           - lax.broadcasted_iota(jnp.int32, (tq, kb), 0))
    rel = jnp.concatenate([rel] * HEADS_PER_BLOCK, axis=0)

    k_r = lax.broadcasted_iota(jnp.int32, (kb, 2 * kb), 0)
    k_c = lax.broadcasted_iota(jnp.int32, (kb, 2 * kb), 1)
    tri_ref[...] = jnp.where(k_c >= kb, 1.0, jnp.where(k_r > k_c, 1.0, 0.0)).astype(BF16)

    def softplus2(z):
        return jnp.maximum(z, 0.0) + jnp.log2(1.0 + jnp.exp2(-jnp.abs(z)))

    def scores(qh, keys):
        z = LOG2_E * lax.dot_general(qh, keys, (((1,), (1,)), ((), ())),
                                     preferred_element_type=F32)
        nlog_nb = softplus2(z)
        return nlog_nb, z - nlog_nb

    def suffix(nlog_nb):
        r = jnp.dot(nlog_nb.astype(BF16), tri_ref[...], preferred_element_type=F32)
        return r[:, :kb], r[:, kb:]

    def stacked_queries(i):
        q = q_ref[pl.ds(i * tq, tq), :]
        return jnp.concatenate(
            [jnp.where((lane >= h * HEAD_DIM) & (lane < (h + 1) * HEAD_DIM), q, jnp.zeros_like(q))
             for h in range(HEADS_PER_BLOCK)], axis=0)

    def store(i, out):
        o_ref[pl.ds(i * tq, tq), :] = jnp.where(lane < HEAD_DIM, out[0:tq], out[tq:2 * tq])

    def window_start(i):
        if isinstance(i, int):
            return max(i * tq - lookback, 0)
        return pl.multiple_of(i * tq - lookback, tq)

    def stage_scores(i, p):
        keys = k_ref[pl.ds(window_start(i), win), :]
        z_ref[p] = LOG2_E * lax.dot_general(stacked_queries(i), keys, (((1,), (1,)), ((), ())),
                                            preferred_element_type=F32)

    def stage_exponents(p, delta):
        z = z_ref[p]
        softplus = softplus2(z)
        carry = None
        for b in reversed(range(nsub)):
            cols = slice(b * kb, (b + 1) * kb)
            masked = (b + 1) * kb > delta
            valid = (rel + b * kb) < delta
            nl = softplus[:, cols]
            if masked:
                nl = jnp.where(valid, nl, 0.0)
            excl, tot = suffix(nl)
            arg = z[:, cols] - softplus[:, cols] - excl
            if carry is not None:
                arg = arg - carry
            if masked:
                arg = jnp.where(valid, arg, NEG_BIG)
            arg_ref[p, :, cols] = arg
            carry = tot if carry is None else carry + tot
        ctot_ref[p] = carry
        return jnp.min(carry)

    def stage_output(i, p, s):
        vals = v_ref[pl.ds(window_start(i), win), :]
        acc_ref[s] = jnp.dot(jnp.exp2(arg_ref[p]).astype(BF16), vals, preferred_element_type=F32)
        carry_ref[s] = ctot_ref[p]

    def finish(i, s, cmin):
        def cond(state):
            pos, cmin = state
            return (pos > -kb) & (cmin <= ATTN_STOP)

        def older(state):
            pos, _ = state
            start = pl.multiple_of(jnp.maximum(pos, 0), tq)
            keys = k_ref[pl.ds(start, kb), :]
            vals = v_ref[pl.ds(start, kb), :]
            fresh = lax.broadcasted_iota(jnp.int32, (1, kb), 1) < pos + kb - start
            nlog_nb, log_b = scores(stacked_queries(i), keys)
            nlog_nb = jnp.where(fresh, nlog_nb, 0.0)
            excl, tot = suffix(nlog_nb)
            carry = carry_ref[s]
            a = jnp.where(fresh, jnp.exp2(log_b - excl - carry), 0.0)
            acc_ref[s] += jnp.dot(a.astype(BF16), vals, preferred_element_type=F32)
            carry_ref[s] = carry + tot
            return pos - kb, jnp.min(carry + tot)

        lax.while_loop(cond, older, (jnp.asarray(window_start(i) - kb, jnp.int32), cmin))
        store(i, acc_ref[s])

    for i in range(first):
        stage_scores(i, 0)
        cmin = stage_exponents(0, i * tq - window_start(i))
        stage_output(i, 0, 0)
        finish(i, 0, cmin)

    n_pipe = seq // tq - first
    unroll = acc_ref.shape[0]
    lag = z_ref.shape[0] - 1
    n_slots = lag + 1
    pending = []
    for tau in range(2 * lag):
        stage_scores(first + tau, tau % n_slots)
        if tau >= lag:
            pending.append(stage_exponents((tau - lag) % n_slots, lookback))

    def steady(m, pending):
        pending = list(pending)
        tau0 = 2 * lag + unroll * m
        done = []
        for u in range(unroll):
            stage_output(first + tau0 + u - 2 * lag, u % n_slots, u)
            done.append((first + tau0 + u - 2 * lag, u, pending.pop(0)))
            pending.append(stage_exponents((lag + u) % n_slots, lookback))
            stage_scores(first + tau0 + u, (2 * lag + u) % n_slots)
        for block, s, cmin in done:
            finish(block, s, cmin)
        return tuple(pending)

    pending = list(lax.fori_loop(0, (n_pipe - 2 * lag) // unroll, steady, tuple(pending)))
    for tau in range(n_pipe, n_pipe + 2 * lag):
        j = tau - 2 * lag
        stage_output(first + j, j % n_slots, 0)
        cmin = pending.pop(0)
        if tau - lag < n_pipe:
            pending.append(stage_exponents((tau - lag) % n_slots, lookback))
        finish(first + j, 0, cmin)


def _attention(q, k, v, batch, seq):
    n, width = q.shape
    tq, kb, nsub = ATTN_QUERY_ROWS, LANES, ATTN_WINDOW_BLOCKS
    n_blocks = seq // tq
    lag, n_slots = ATTN_STAGE_LAG, ATTN_STAGE_LAG + 1
    clipped = -(-(nsub * kb - tq) // tq)
    unroll = first = None
    for u in range(ATTN_UNROLL - ATTN_UNROLL % n_slots, 0, -n_slots):
        for f in range(clipped, clipped + u):
            rest = n_blocks - f - 2 * lag
            if rest >= 0 and rest % u == 0 and first is None:
                unroll, first = u, f
    assert first is not None and seq % tq == 0 and seq >= nsub * kb
    blk = pl.BlockSpec((seq, LANES), lambda b, hp: (b, hp))
    rows = HEADS_PER_BLOCK * tq
    stage_buf = pltpu.VMEM((n_slots, rows, nsub * kb), F32)
    carry_buf = pltpu.VMEM((n_slots, rows, kb), F32)
    row_buf = pltpu.VMEM((unroll, rows, LANES), F32)
    return pl.pallas_call(
        functools.partial(_attn_kernel, tq=tq, kb=kb, nsub=nsub, first=first),
        grid=(batch, width // LANES),
        in_specs=[blk, blk, blk],
        out_specs=blk,
        out_shape=jax.ShapeDtypeStruct((n, width), F32),
        scratch_shapes=[pltpu.VMEM((kb, 2 * kb), BF16),
                        stage_buf, stage_buf, carry_buf, row_buf, row_buf],
        compiler_params=_cparams(("arbitrary", "arbitrary")),
        name="attn",
    )(q, k, v)


def _gelu_tanh(x):
    c = math.sqrt(2.0 / math.pi)
    half_x = 0.5 * x
    return half_x + half_x * jnp.tanh(x * (c + (c * 0.044715) * (x * x)))


def _sigmoid(x):
    return 0.5 + 0.5 * jnp.tanh(0.5 * x)


def _lru_kernel(*refs, ts, n_slab):
    xl_refs, gl_refs = refs[0:n_slab], refs[n_slab:2 * n_slab]
    cw_ref, cb_ref, wr_ref, br_ref, wi_ref, bi_ref, lam_ref = refs[2 * n_slab:2 * n_slab + 7]
    o_refs = refs[2 * n_slab + 7:3 * n_slab + 7]
    (tail_ref, a7_ref, u7_ref, hp_ref, pa_ref, pu_ref, h_ref,
     wr_bd, wi_bd) = refs[3 * n_slab + 7:]
    t = pl.program_id(1)
    groups = ts // SUBLANES

    @pl.when(t == 0)
    def _():
        tail_ref[...] = jnp.zeros_like(tail_ref)
        h_ref[...] = jnp.zeros_like(h_ref)
        per = LANES // wr_ref.shape[1]
        for src, dst in ((wr_ref, wr_bd), (wi_ref, wi_bd)):
            for c in range(n_slab):
                rows = []
                for p in range(per):
                    blk = src[c * per + p]
                    rows.append(jnp.concatenate(
                        [blk if q == p else jnp.zeros_like(blk) for q in range(per)], axis=1))
                dst[c] = jnp.concatenate(rows, axis=0).astype(BF16)

    first_group = lax.broadcasted_iota(jnp.int32, (groups, LANES), 0) == 0
    for c in range(n_slab):
        lanes = slice(c * LANES, (c + 1) * LANES)
        x = [xl_refs[c][pl.ds(s, groups, stride=SUBLANES), :] for s in range(SUBLANES)]
        shifted = {}
        for s in range(SUBLANES - (CONV_W - 1), SUBLANES):
            shifted[s] = jnp.where(first_group, tail_ref[c, s:s + 1, :], pltpu.roll(x[s], 1, axis=0))
            tail_ref[c, s:s + 1, :] = x[s][groups - 1:groups, :]
        conv = []
        for s in range(SUBLANES):
            y = cb_ref[:, lanes]
            for w in range(CONV_W):
                j = s - (CONV_W - 1) + w
                y = y + (x[j] if j >= 0 else shifted[j + SUBLANES]) * cw_ref[w:w + 1, lanes]
            conv.append(y)
        xc = jnp.concatenate(conv, axis=0)

        xcb = xc.astype(BF16)
        r = _sigmoid(jnp.dot(xcb, wr_bd[c], preferred_element_type=F32) + br_ref[:, lanes])
        ig = _sigmoid(jnp.dot(xcb, wi_bd[c], preferred_element_type=F32) + bi_ref[:, lanes])
        lam = lam_ref[:, lanes]
        log_sig_lam = -(jnp.maximum(-lam, 0.0) + jnp.log1p(jnp.exp(-jnp.abs(lam))))
        log_a = r * (RG_C * log_sig_lam)
        a = jnp.exp(log_a)
        th = jnp.tanh(log_a)
        one_m_a2 = -2.0 * th / (1.0 - th)
        root = jnp.where(one_m_a2 > 0.0, one_m_a2 * lax.rsqrt(one_m_a2), 0.0)
        u = root * (ig * xc)

        a_run = u_run = None
        for s in range(SUBLANES):
            rows = slice(s * groups, (s + 1) * groups)
            if s == 0:
                a_run, u_run = a[rows], u[rows]
            else:
                u_run = a[rows] * u_run + u[rows]
                a_run = a_run * a[rows]
            pa_ref[c, rows, :] = a_run
            pu_ref[c, rows, :] = u_run
        a7_ref[c] = a_run
        u7_ref[c] = u_run

    def group(g, hs):
        nxt = []
        for c in range(n_slab):
            hp_ref[c, pl.ds(g, 1), :] = hs[c]
            nxt.append(a7_ref[c, pl.ds(g, 1), :] * hs[c] + u7_ref[c, pl.ds(g, 1), :])
        return tuple(nxt)

    hs = lax.fori_loop(0, groups, group, tuple(h_ref[c] for c in range(n_slab)), unroll=8)
    for c in range(n_slab):
        h_ref[c] = hs[c]

    for c in range(n_slab):
        h_in = hp_ref[c]
        for s in range(SUBLANES):
            rows = slice(s * groups, (s + 1) * groups)
            hseq = pu_ref[c, rows, :] + pa_ref[c, rows, :] * h_in
            gate = _gelu_tanh(gl_refs[c][pl.ds(s, groups, stride=SUBLANES), :])
            o_refs[c][pl.ds(s, groups, stride=SUBLANES), :] = hseq * gate


def _lru(xl, gl, conv_w, conv_b, w_r, br, w_i, bi, lam, batch, seq, ts):
    n, width = xl.shape
    nt = seq // ts
    n_slab = width // LANES
    blocks, block_w, _ = w_r.shape
    assert blocks * block_w == width and LANES % block_w == 0 and ts % (SUBLANES * SUBLANES) == 0
    groups = ts // SUBLANES
    gate_bd = pltpu.VMEM((n_slab, LANES, LANES), BF16)
    slab = [pl.BlockSpec((ts, LANES), functools.partial(lambda b, t, c: (b * nt + t, c), c=c))
            for c in range(n_slab)]
    const2 = lambda b, t: (0, 0)
    const3 = lambda b, t: (0, 0, 0)
    vec = pl.BlockSpec((1, width), const2)
    per_group = pltpu.VMEM((n_slab, groups, LANES), F32)
    per_step = pltpu.VMEM((n_slab, ts, LANES), F32)
    return pl.pallas_call(
        functools.partial(_lru_kernel, ts=ts, n_slab=n_slab),
        grid=(batch, nt),
        in_specs=slab + slab + [pl.BlockSpec((CONV_W, width), const2), vec,
                                pl.BlockSpec(w_r.shape, const3), vec,
                                pl.BlockSpec(w_i.shape, const3), vec, vec],
        out_specs=[pl.BlockSpec((ts, LANES), lambda b, t: (b * nt + t, 0))] * n_slab,
        out_shape=[jax.ShapeDtypeStruct((n, LANES), F32)] * n_slab,
        scratch_shapes=[pltpu.VMEM((n_slab, SUBLANES, LANES), F32),
                        per_group, per_group, per_group, per_step, per_step,
                        pltpu.VMEM((n_slab, 1, LANES), F32), gate_bd, gate_bd],
        compiler_params=_cparams(("arbitrary", "arbitrary")),
        name="lru",
    )(*([xl] * n_slab), *([gl] * n_slab), conv_w, conv_b, w_r, br, w_i, bi, lam)


def _mix_route_kernel(sb_ref, *refs, tm, n_slab):
    lru_refs = refs[:n_slab]
    (x_ref, sbg_ref, lrug_ref, wo_ref, n2g_ref, wrs_ref, rb_ref,
     h_ref, u2_ref, rr_ref, rwt_ref, tc_ref, before_ref, u2s_ref, wo_bf) = refs[n_slab:]

    @pl.when(pl.program_id(0) == 0)
    def _():
        wo_bf[...] = wo_ref[...].astype(BF16)

    _mix_route_body(sb_ref, lru_refs, x_ref, sbg_ref, lrug_ref, wo_bf, n2g_ref, wrs_ref, rb_ref,
                    h_ref, u2_ref, rr_ref, rwt_ref, tc_ref, before_ref, u2s_ref, tm)


def _mix_route_body(sb_ref, lru_refs, x_ref, sbg_ref, lrug_ref, wo_ref, n2g_ref, wrs_ref, rb_ref,
                    h_ref, u2_ref, rr_ref, rwt_ref, tc_ref, before_ref, u2s_ref, tm):
    step = pl.program_id(0)
    half = sb_ref.shape[1]
    n_sub = x_ref.shape[0] // tm

    @pl.when(step == 0)
    def _():
        r_id = lax.broadcasted_iota(jnp.int32, (tm, tm), 0)
        c_id = lax.broadcasted_iota(jnp.int32, (tm, tm), 1)
        before_ref[...] = (r_id < c_id).astype(BF16)
        tc_ref[...] = jnp.zeros_like(tc_ref)
        u2s_ref[...] = jnp.zeros_like(u2s_ref)

    def route(j):
        rows = slice(j * tm, (j + 1) * tm)
        u2 = u2s_ref[rows, :]
        u_hi = u2.astype(BF16)
        u_lo = (u2 - u_hi.astype(F32)).astype(BF16)
        nt_dims = (((1,), (1,)), ((), ()))
        n_rows = rb_ref.shape[0]
        both = lax.dot_general(wrs_ref[...], u_hi, nt_dims, preferred_element_type=F32)
        lt = (both[0:n_rows] + both[n_rows:2 * n_rows]
              + lax.dot_general(wrs_ref[0:n_rows, :], u_lo, nt_dims, preferred_element_type=F32)
              + rb_ref[...])

        sub = lax.broadcasted_iota(jnp.int32, (SUBLANES, tm), 0)

        def top1(x):
            m = jnp.max(x, axis=0, keepdims=True)
            idx = jnp.min(jnp.where(x == m, sub, SUBLANES), axis=0, keepdims=True)
            return m, idx

        grp = lt[0:SUBLANES, :]
        g_max, g_idx = top1(grp)
        g_p = 1.0 / jnp.sum(jnp.exp(grp - g_max), axis=0, keepdims=True)
        fine = lt[SUBLANES:2 * SUBLANES, :]
        for g in range(1, N_GROUPS):
            fine = jnp.where(g_idx == g, lt[(g + 1) * SUBLANES:(g + 2) * SUBLANES, :], fine)
        m1, i1 = top1(fine)
        m2, i2 = top1(jnp.where(sub == i1, -jnp.inf, fine))
        e2 = jnp.exp(m2 - m1)
        p1 = 1.0 / (1.0 + e2)
        w1 = g_p * p1
        w2 = g_p * (e2 * p1)
        x1 = g_idx * EXPERTS_PER_GROUP + i1
        x2 = g_idx * EXPERTS_PER_GROUP + i2

        eid = lax.broadcasted_iota(jnp.int32, (N_EXPERTS, tm), 0)
        oh1 = jnp.where(eid == x1, 1.0, 0.0)
        oh2 = jnp.where(eid == x2, 1.0, 0.0)
        pre = jnp.dot(jnp.concatenate([oh1, oh2], axis=0).astype(BF16), before_ref[...],
                      preferred_element_type=F32)
        pre1, pre2 = pre[0:N_EXPERTS], pre[N_EXPERTS:2 * N_EXPERTS]
        cnt1 = jnp.sum(oh1, axis=1, keepdims=True)
        cnt2 = jnp.sum(oh2, axis=1, keepdims=True)
        seg8 = jnp.floor((cnt1 + cnt2 + (SUBLANES - 1.0)) * (1.0 / SUBLANES))
        e_r = lax.broadcasted_iota(jnp.int32, (N_EXPERTS, N_EXPERTS), 0)
        e_c = lax.broadcasted_iota(jnp.int32, (N_EXPERTS, N_EXPERTS), 1)
        lower = jnp.where(e_c < e_r, 1.0, 0.0).astype(BF16)
        seg8_b = jnp.broadcast_to(seg8, (N_EXPERTS, LANES)).astype(BF16)
        seg_off = SUBLANES * jnp.dot(lower, seg8_b, preferred_element_type=F32)[:, 0:1]
        pos1 = jnp.sum(oh1 * (pre1 + seg_off), axis=0, keepdims=True)
        pos2 = jnp.sum(oh2 * (pre2 + (seg_off + cnt1)), axis=0, keepdims=True)

        lane = lax.broadcasted_iota(jnp.int32, tc_ref.shape, 1)
        seg_rows = jnp.broadcast_to(seg8 * SUBLANES, tc_ref.shape).astype(jnp.int32)
        tile = (step - 1) * n_sub + j
        tc_ref[...] = jnp.where(lane == tile, seg_rows, tc_ref[...])

        zrow = jnp.zeros((SUBLANES - 4, tm), jnp.int32)
        rr_ref[:, rows] = jnp.concatenate(
            [pos1.astype(jnp.int32), pos2.astype(jnp.int32), x1, x2, zrow], axis=0)
        wt = jnp.concatenate([w1, w2, pos1, pos2, jnp.zeros((LANES - 4, tm), F32)], axis=0)
        rwt_ref[rows, :] = wt.T

    m_sb = _rms_f32(sb_ref[...], sbg_ref[...]).astype(BF16)
    lru = jnp.concatenate([r[...] for r in lru_refs], axis=-1)
    m_lru = _rms_f32(lru, lrug_ref[...]).astype(BF16)
    h_ref[...] = (x_ref[...]
                  + jnp.dot(m_sb, wo_ref[0:half, :], preferred_element_type=F32)
                  + jnp.dot(m_lru, wo_ref[half:2 * half, :], preferred_element_type=F32))

    for j in range(n_sub):
        route(j)

    u2_next = _rms_f32(h_ref[...], n2g_ref[...])
    u2_ref[...] = u2_next.astype(BF16)
    u2s_ref[...] = u2_next


def _mix_route(sb, lru, x2, sbg, lrug, w_out, n2g, wr_stack, rbias, tm, tp):
    n, d = x2.shape
    half = sb.shape[1]
    n_tiles = n // tp
    row = lambda i: (jnp.minimum(i, n_tiles - 1), 0)
    routed = lambda i: (jnp.maximum(i - 1, 0), 0)
    const = lambda i: (0, 0)
    return pl.pallas_call(
        functools.partial(_mix_route_kernel, tm=tm, n_slab=len(lru)),
        grid=(n_tiles + 1,),
        in_specs=[pl.BlockSpec((tp, half), row)] + [pl.BlockSpec((tp, LANES), row)] * len(lru)
                 + [pl.BlockSpec((tp, d), row),
                  pl.BlockSpec((1, half), const), pl.BlockSpec((1, half), const),
                  pl.BlockSpec(w_out.shape, const, pipeline_mode=pl.Buffered(1)),
                  pl.BlockSpec((1, d), const),
                  pl.BlockSpec(wr_stack.shape, const), pl.BlockSpec(rbias.shape, const)],
        out_specs=[pl.BlockSpec((tp, d), row), pl.BlockSpec((tp, d), row),
                   pl.BlockSpec((SUBLANES, tp), lambda i: (0, jnp.maximum(i - 1, 0))),
                   pl.BlockSpec((tp, LANES), routed),
                   pl.BlockSpec((N_EXPERTS, LANES), const)],
        out_shape=[jax.ShapeDtypeStruct((n, d), F32), jax.ShapeDtypeStruct((n, d), BF16),
                   jax.ShapeDtypeStruct((SUBLANES, n), jnp.int32),
                   jax.ShapeDtypeStruct((n, LANES), F32),
                   jax.ShapeDtypeStruct((N_EXPERTS, LANES), jnp.int32)],
        scratch_shapes=[pltpu.VMEM((tm, tm), BF16), pltpu.VMEM((tp, d), F32),
                        pltpu.VMEM(w_out.shape, BF16)],
        compiler_params=_cparams(("arbitrary",)),
        name="mix_route",
    )(sb, *lru, x2, sbg, lrug, w_out, n2g, wr_stack, rbias)


def _pack_halves(x):
    half = x.shape[1] // 2
    lo = lax.shift_right_logical(lax.bitcast_convert_type(x[:, :half], jnp.int32), 16)
    hi = lax.bitcast_convert_type(x[:, half:], jnp.int32) & HIGH_HALF
    return hi | lo


def _unpack_halves(p):
    lo = lax.bitcast_convert_type(lax.shift_left(p, 16), F32)
    hi = lax.bitcast_convert_type(p & HIGH_HALF, F32)
    return lo.astype(BF16), hi.astype(BF16)


def _segment_copies(tile, c8_ref, loff_ref, goff_ref, make):
    for e in range(N_EXPERTS):
        idx = tile * N_EXPERTS + e
        rows = pl.multiple_of(c8_ref[idx], SUBLANES)

        @pl.when(rows > 0)
        def _(idx=idx, rows=rows):
            lo = pl.multiple_of(loff_ref[idx], SUBLANES)
            go = pl.multiple_of(goff_ref[idx], SUBLANES)
            make(pl.ds(lo, rows), pl.ds(go, rows)).start()


def _dispatch_kernel(c8_ref, loff_ref, goff_ref, tot_ref, used_ref,
                     rr_ref, u2_ref, xs_ref, lbuf, zbuf, sem, zsem, *, td, lrows):
    i = pl.program_id(0)
    slot = i % 2

    r_id = lax.broadcasted_iota(jnp.int32, (lrows, td), 0)
    perm = jnp.where(r_id == rr_ref[0:1, :], 1.0, jnp.where(r_id == rr_ref[1:2, :], 1.0, 0.0))
    sorted_rows = jnp.dot(perm.astype(BF16), u2_ref[...], preferred_element_type=F32)
    lbuf[slot] = _pack_halves(sorted_rows)

    @pl.when(i == 0)
    def _():
        zbuf[...] = jnp.zeros_like(zbuf)
        chunk = zbuf.shape[0]
        used = used_ref[0]
        spare = xs_ref.shape[0] - used
        n_fill = (spare + chunk - 1) // chunk

        def fill_copy(k):
            rows = pl.multiple_of(jnp.minimum(chunk, spare - k * chunk), SUBLANES)
            start = pl.multiple_of(used + k * chunk, SUBLANES)
            return pltpu.make_async_copy(zbuf.at[pl.ds(0, rows)], xs_ref.at[pl.ds(start, rows)],
                                         zsem)

        def fill_start(k, c):
            fill_copy(k).start()
            return c

        def fill_wait(k, c):
            fill_copy(k).wait()
            return c

        lax.fori_loop(0, n_fill, fill_start, 0)
        lax.fori_loop(0, n_fill, fill_wait, 0)

    def wait_tile(tile, s):
        rows = pl.multiple_of(tot_ref[tile], SUBLANES)
        pltpu.make_async_copy(lbuf.at[s, pl.ds(0, rows)], xs_ref.at[pl.ds(0, rows)], sem).wait()

    @pl.when(i > 0)
    def _():
        wait_tile(i - 1, 1 - slot)

    _segment_copies(i, c8_ref, loff_ref, goff_ref,
                    lambda loc, glob: pltpu.make_async_copy(lbuf.at[slot, loc], xs_ref.at[glob], sem))

    @pl.when(i == pl.num_programs(0) - 1)
    def _():
        wait_tile(i, slot)


def _dispatch(c8, loff, goff, tot, used, rr, u2, p_rows, td, lrows, tme):
    n, d = u2.shape
    pmap = lambda i, *_: (0, i)
    return pl.pallas_call(
        functools.partial(_dispatch_kernel, td=td, lrows=lrows),
        grid_spec=pltpu.PrefetchScalarGridSpec(
            num_scalar_prefetch=5,
            grid=(n // td,),
            in_specs=[pl.BlockSpec((SUBLANES, td), pmap),
                      pl.BlockSpec((td, d), lambda i, *_: (i, 0))],
            out_specs=pl.BlockSpec(memory_space=pl.ANY),
            scratch_shapes=[pltpu.VMEM((2, lrows, d // 2), jnp.int32),
                            pltpu.VMEM((tme, d // 2), jnp.int32),
                            pltpu.SemaphoreType.DMA(()), pltpu.SemaphoreType.DMA(())]),
        out_shape=jax.ShapeDtypeStruct((p_rows, d // 2), jnp.int32),
        compiler_params=_cparams(("arbitrary",)),
        name="dispatch",
    )(c8, loff, goff, tot, used, rr, u2)


def _experts_kernel(eoff_ref, erows_ref, xs_ref, wg_ref, wu_ref, wd_ref, ys_ref,
                    wg_bf, wu_bf, wd_bf, xbuf, ybuf, sem_in, sem_out, state, *, tme):
    e = pl.program_id(0)
    n_experts = pl.num_programs(0)
    rows = erows_ref[e]
    off = eoff_ref[e]
    n_tiles = (rows + tme - 1) // tme
    nxt = jnp.minimum(e + 1, n_experts - 1)
    prefetch = (rows > 0) & (e + 1 < n_experts) & (erows_ref[nxt] > 0)

    def tile_rows(total, k):
        return pl.multiple_of(jnp.minimum(tme, total - k * tme), SUBLANES)

    def in_copy(start, r, slot):
        start = pl.multiple_of(start, SUBLANES)
        return pltpu.make_async_copy(xs_ref.at[pl.ds(start, r)], xbuf.at[slot, pl.ds(0, r)],
                                     sem_in.at[slot])

    def out_copy(start, r, slot):
        start = pl.multiple_of(start, SUBLANES)
        return pltpu.make_async_copy(ybuf.at[slot, pl.ds(0, r)], ys_ref.at[pl.ds(start, r)],
                                     sem_out.at[slot])

    @pl.when(e == 0)
    def _():
        for s in range(4):
            state[s] = 0
        xbuf[...] = jnp.zeros_like(xbuf)

    @pl.when(rows > 0)
    def _():
        wg_bf[...] = wg_ref[0].astype(BF16)
        wu_bf[...] = wu_ref[0].astype(BF16)
        wd_bf[...] = wd_ref[0].astype(BF16)
        done = state[0]

        @pl.when(state[1] == 0)
        def _():
            in_copy(off, tile_rows(rows, 0), done % 2).start()

        def tile(k, c):
            slot = (done + k) % 2
            r = tile_rows(rows, k)

            @pl.when(k + 1 < n_tiles)
            def _():
                in_copy(off + (k + 1) * tme, tile_rows(rows, k + 1), 1 - slot).start()

            @pl.when((k + 1 == n_tiles) & prefetch)
            def _():
                in_copy(eoff_ref[nxt], tile_rows(erows_ref[nxt], 0), 1 - slot).start()

            in_copy(off, r, slot).wait()

            @pl.when(done + k >= 2)
            def _():
                out_copy(0, pl.multiple_of(state[2 + slot], SUBLANES), slot).wait()

            def mlp(start, n):
                x = jnp.concatenate(_unpack_halves(xbuf[slot, start:start + n]), axis=1)
                hg = jnp.dot(x, wg_bf[...], preferred_element_type=F32)
                hu = jnp.dot(x, wu_bf[...], preferred_element_type=F32)
                act = (hg * jax.nn.sigmoid(hg) * hu).astype(BF16)
                y = jnp.dot(act, wd_bf[...], preferred_element_type=F32)
                ybuf[slot, start:start + n] = _pack_halves(y.astype(BF16).astype(F32))

            for below, n in zip((0,) + EXPERT_TILE_SIZES, EXPERT_TILE_SIZES):
                @pl.when((r > below) & (r <= n))
                def _(n=n):
                    mlp(0, n)

            out_copy(off + k * tme, r, slot).start()
            state[2 + slot] = r
            return c

        lax.fori_loop(0, n_tiles, tile, 0)
        state[0] = done + n_tiles

    state[1] = prefetch.astype(jnp.int32)

    @pl.when(e == n_experts - 1)
    def _():
        total = eoff_ref[e] + rows
        for slot in range(2):
            @pl.when(state[0] > slot)
            def _(slot=slot):
                out_copy(0, pl.multiple_of(state[2 + slot], SUBLANES), slot).wait()
        ybuf[0] = jnp.zeros(ybuf.shape[1:], ybuf.dtype)
        spare = ys_ref.shape[0] - total
        n_fill = (spare + tme - 1) // tme

        def fill_copy(k):
            return out_copy(total + k * tme, tile_rows(spare, k), 0)

        def fill_start(k, c):
            fill_copy(k).start()
            return c

        def fill_wait(k, c):
            fill_copy(k).wait()
            return c

        lax.fori_loop(0, n_fill, fill_start, 0)
        lax.fori_loop(0, n_fill, fill_wait, 0)


def _experts(eoff, erows, xs, wg, wu, wd, tme):
    p = xs.shape[0]
    n_experts, d, de = wg.shape
    wmap = lambda e, *_: (e, 0, 0)
    return pl.pallas_call(
        functools.partial(_experts_kernel, tme=tme),
        grid_spec=pltpu.PrefetchScalarGridSpec(
            num_scalar_prefetch=2,
            grid=(n_experts,),
            in_specs=[pl.BlockSpec(memory_space=pl.ANY),
                      pl.BlockSpec((1, d, de), wmap), pl.BlockSpec((1, d, de), wmap),
                      pl.BlockSpec((1, de, d), wmap)],
            out_specs=pl.BlockSpec(memory_space=pl.ANY),
            scratch_shapes=[pltpu.VMEM((d, de), BF16), pltpu.VMEM((d, de), BF16),
                            pltpu.VMEM((de, d), BF16),
                            pltpu.VMEM((2, tme, d // 2), jnp.int32),
                            pltpu.VMEM((2, tme, d // 2), jnp.int32),
                            pltpu.SemaphoreType.DMA((2,)), pltpu.SemaphoreType.DMA((2,)),
                            pltpu.SMEM((4,), jnp.int32)]),
        out_shape=jax.ShapeDtypeStruct((p, d // 2), jnp.int32),
        compiler_params=_cparams(("arbitrary",)),
        name="experts",
    )(eoff, erows, xs, wg, wu, wd)


def _combine_kernel(c8_ref, loff_ref, goff_ref, tot_ref,
                    rwt_ref, h_ref, fg_ref, ys_ref, y_ref, ybuf, sems, *, tc, lrows):
    i = pl.program_id(0)
    slot = i % 2

    def gather_tile(tile, s):
        _segment_copies(tile, c8_ref, loff_ref, goff_ref,
                        lambda loc, glob: pltpu.make_async_copy(ys_ref.at[glob], ybuf.at[s, loc],
                                                                sems.at[s]))

    @pl.when(i == 0)
    def _():
        ybuf[...] = jnp.zeros_like(ybuf)
        gather_tile(0, 0)

    @pl.when(i + 1 < pl.num_programs(0))
    def _():
        gather_tile(i + 1, 1 - slot)

    rows = pl.multiple_of(tot_ref[i], SUBLANES)
    pltpu.make_async_copy(ys_ref.at[pl.ds(0, rows)], ybuf.at[slot, pl.ds(0, rows)],
                          sems.at[slot]).wait()

    w = rwt_ref[...]
    c_id = lax.broadcasted_iota(jnp.int32, (tc, lrows), 1)
    pos1 = w[:, 2:3].astype(jnp.int32)
    pos2 = w[:, 3:4].astype(jnp.int32)
    wmat = jnp.where(c_id == pos1, w[:, 0:1], jnp.where(c_id == pos2, w[:, 1:2], 0.0)).astype(BF16)
    half = h_ref.shape[1] // 2
    outs, sumsq = [], 0.0
    for part, cols in zip(_unpack_halves(ybuf[slot]), (slice(0, half), slice(half, 2 * half))):
        out = h_ref[:, cols] + jnp.dot(wmat, part, preferred_element_type=F32)
        sumsq = sumsq + jnp.sum(out * out, axis=-1, keepdims=True)
        outs.append(out)
    scale = lax.rsqrt(sumsq * (1.0 / (2 * half)) + EPS)
    for out, cols in zip(outs, (slice(0, half), slice(half, 2 * half))):
        y_ref[:, cols] = out * scale * fg_ref[:, cols]


def _combine(c8, loff, goff, tot, rwt, h, final_g, ys, tc, lrows):
    n, d = h.shape
    return pl.pallas_call(
        functools.partial(_combine_kernel, tc=tc, lrows=lrows),
        grid_spec=pltpu.PrefetchScalarGridSpec(
            num_scalar_prefetch=4,
            grid=(n // tc,),
            in_specs=[pl.BlockSpec((tc, LANES), lambda i, *_: (i, 0)),
                      pl.BlockSpec((tc, d), lambda i, *_: (i, 0)),
                      pl.BlockSpec((1, d), lambda i, *_: (0, 0)),
                      pl.BlockSpec(memory_space=pl.ANY)],
            out_specs=pl.BlockSpec((tc, d), lambda i, *_: (i, 0)),
            scratch_shapes=[pltpu.VMEM((2, lrows, d // 2), jnp.int32),
                            pltpu.SemaphoreType.DMA((2,))]),
        out_shape=jax.ShapeDtypeStruct((n, d), F32),
        compiler_params=_cparams(("arbitrary",)),
        name="combine",
    )(c8, loff, goff, tot, rwt, h, final_g, ys)


def _router_tables(w_group, b_group, w_fine, b_fine):
    d = w_group.shape[0]
    pad_g, pad_f = SUBLANES - N_GROUPS, ROUTER_ROWS - SUBLANES - N_EXPERTS
    w = jnp.concatenate([w_group.T, jnp.zeros((pad_g, d), F32), w_fine.T,
                         jnp.zeros((pad_f, d), F32)], axis=0)
    b = jnp.concatenate([b_group, jnp.full((pad_g,), NEG_BIG, F32), b_fine,
                         jnp.full((pad_f,), NEG_BIG, F32)])
    w_hi = w.astype(BF16)
    w_lo = (w - w_hi.astype(F32)).astype(BF16)
    return jnp.concatenate([w_hi, w_lo], axis=0), b.reshape(ROUTER_ROWS, 1)


def kernel(x, norm1_g, w_in, conv_w, conv_b, w_rgate, b_rgate, w_igate, b_igate, lam, sb_norm_g,
           lru_norm_g, w_out, norm2_g, w_group, b_group, w_fine, b_fine, w_e_gate, w_e_up,
           w_e_down, final_g):
    batch, seq, d = x.shape
    n = batch * seq
    width = w_in.shape[1] // 5
    tm = min(TOKEN_TILE, seq)
    tp = min(ROW_TILE, seq)
    tme = EXPERT_TILE_ROWS

    x2 = x.reshape(n, d)
    vec = lambda a: a.reshape(1, -1)

    q, k, v, xl, gl = _in_proj(x2, vec(norm1_g), w_in, width, tp)
    out_sb = _attention(q, k, v, batch, seq)
    out_lru = _lru(xl, gl, conv_w, vec(conv_b), w_rgate, vec(b_rgate), w_igate, vec(b_igate),
                   vec(lam), batch, seq, tp)

    wr_stack, rbias = _router_tables(w_group, b_group, w_fine, b_fine)
    h, u2, rr, rwt, tcnt = _mix_route(out_sb, out_lru, x2, vec(sb_norm_g), vec(lru_norm_g),
                                      w_out, vec(norm2_g), wr_stack, rbias, tm, tp)

    n_tiles = n // tm
    assert n_tiles <= LANES, "one lane of the per-tile count table per token tile"
    c8 = tcnt[:, :n_tiles].T
    erows = jnp.sum(c8, axis=0)
    eoff = jnp.cumsum(erows) - erows
    goff = eoff[None, :] + jnp.cumsum(c8, axis=0) - c8
    loff = jnp.cumsum(c8, axis=1) - c8
    tot = jnp.sum(c8, axis=1)
    lrows = 2 * tm + N_EXPERTS * SUBLANES
    p_rows = 2 * n + n_tiles * N_EXPERTS * (SUBLANES - 1)
    p_rows = -(-p_rows // SUBLANES) * SUBLANES
    i32 = lambda a: a.reshape(-1).astype(jnp.int32)
    c8, loff, goff, tot, eoff, erows = (i32(a) for a in (c8, loff, goff, tot, eoff, erows))

    xs = _dispatch(c8, loff, goff, tot, jnp.sum(erows, keepdims=True), rr, u2, p_rows, tm, lrows, tme)
    ys = _experts(eoff, erows, xs, w_e_gate, w_e_up, w_e_down, tme)
    y = _combine(c8, loff, goff, tot, rwt, h, vec(final_g), ys, tm, lrows)
    return y.reshape(batch, seq, d)
```

```python
import functools
import math

import jax
import jax.numpy as jnp
from jax import lax
from jax.experimental import pallas as pl
from jax.experimental.pallas import tpu as pltpu

F32 = jnp.float32
BF16 = jnp.bfloat16

EPS = 1e-6
HEAD_DIM = 64
HEADS_PER_BLOCK = 2
LANES = 128
SUBLANES = 8
CONV_W = 4
RG_C = 8.0
N_GROUPS = 4
EXPERTS_PER_GROUP = 8
N_EXPERTS = N_GROUPS * EXPERTS_PER_GROUP
ROUTER_ROWS = 48
NEG_BIG = -1e30
LOG2_E = math.log2(math.e)
ATTN_STOP = 104.0 * LOG2_E
ATTN_QUERY_ROWS = 64
ATTN_WINDOW_BLOCKS = 2
ATTN_UNROLL = 24
ATTN_STAGE_LAG = 2
HIGH_HALF = -65536
EXPERT_TILE_ROWS = 1536
EXPERT_TILE_SIZES = (256, 512, 768, 1024, 1280, 1536)
TOKEN_TILE = 512
ROW_TILE = 1024

VMEM_LIMIT = 56 * 1024 * 1024


def _cparams(sem):
    return pltpu.CompilerParams(dimension_semantics=sem, vmem_limit_bytes=VMEM_LIMIT)


def _rms_f32(x, g):
    return x * lax.rsqrt(jnp.mean(x * x, axis=-1, keepdims=True) + EPS) * g


def _in_proj_kernel(x_ref, g_ref, w_ref, q_ref, k_ref, v_ref, xl_ref, gl_ref, w_bf, *,
                    width, q_scale):
    @pl.when(pl.program_id(0) == 0)
    def _():
        for c in range(w_ref.shape[1] // width):
            cols = slice(c * width, (c + 1) * width)
            w_bf[:, cols] = w_ref[:, cols].astype(BF16)

    u = _rms_f32(x_ref[...], g_ref[...]).astype(BF16)
    for c, o_ref in enumerate((q_ref, k_ref, v_ref, xl_ref, gl_ref)):
        p = jnp.dot(u, w_bf[:, c * width:(c + 1) * width], preferred_element_type=F32)
        if c == 0:
            p = p * q_scale
        o_ref[...] = p.astype(o_ref.dtype)


def _in_proj(x2, g, w, width, tm):
    n, d = x2.shape
    row = lambda i: (i, 0)
    out_bf = jax.ShapeDtypeStruct((n, width), BF16)
    out_f = jax.ShapeDtypeStruct((n, width), F32)
    return pl.pallas_call(
        functools.partial(_in_proj_kernel, width=width, q_scale=1.0 / math.sqrt(HEAD_DIM)),
        grid=(n // tm,),
        in_specs=[pl.BlockSpec((tm, d), row),
                  pl.BlockSpec((1, d), lambda i: (0, 0)),
                  pl.BlockSpec(w.shape, lambda i: (0, 0), pipeline_mode=pl.Buffered(1))],
        out_specs=[pl.BlockSpec((tm, width), row)] * 5,
        out_shape=[out_bf, out_bf, out_bf, out_f, out_f],
        scratch_shapes=[pltpu.VMEM(w.shape, BF16)],
        compiler_params=_cparams(("arbitrary",)),
        name="in_proj",
    )(x2, g, w)


def _attn_kernel(q_ref, k_ref, v_ref, o_ref, tri_ref, z_ref, arg_ref, ctot_ref, acc_ref, carry_ref,
                 *, tq, kb, nsub, first):
    seq = q_ref.shape[0]
    win = nsub * kb
    lookback = win - tq
    lane = lax.broadcasted_iota(jnp.int32, (1, LANES), 1)
    rel = (lax.broadcasted_iota(jnp.int32, (tq, kb), 1)
           - lax.broadcasted_iota(jnp.int32, (tq, kb), 0))
    rel = jnp.concatenate([rel] * HEADS_PER_BLOCK, axis=0)

    k_r = lax.broadcasted_iota(jnp.int32, (kb, 2 * kb), 0)
    k_c = lax.broadcasted_iota(jnp.int32, (kb, 2 * kb), 1)
    tri_ref[...] = jnp.where(k_c >= kb, 1.0, jnp.where(k_r > k_c, 1.0, 0.0)).astype(BF16)

    def softplus2(z):
        return jnp.maximum(z, 0.0) + jnp.log2(1.0 + jnp.exp2(-jnp.abs(z)))

    def scores(qh, keys):
        z = LOG2_E * lax.dot_general(qh, keys, (((1,), (1,)), ((), ())),
                                     preferred_element_type=F32)
        nlog_nb = softplus2(z)
        return nlog_nb, z - nlog_nb

    def suffix(nlog_nb):
        r = jnp.dot(nlog_nb.astype(BF16), tri_ref[...], preferred_element_type=F32)
        return r[:, :kb], r[:, kb:]

    def stacked_queries(i):
        q = q_ref[pl.ds(i * tq, tq), :]
        return jnp.concatenate(
            [jnp.where((lane >= h * HEAD_DIM) & (lane < (h + 1) * HEAD_DIM), q, jnp.zeros_like(q))
             for h in range(HEADS_PER_BLOCK)], axis=0)

    def store(i, out):
        o_ref[pl.ds(i * tq, tq), :] = jnp.where(lane < HEAD_DIM, out[0:tq], out[tq:2 * tq])

    def window_start(i):
        if isinstance(i, int):
            return max(i * tq - lookback, 0)
        return pl.multiple_of(i * tq - lookback, tq)

    def stage_scores(i, p):
        keys = k_ref[pl.ds(window_start(i), win), :]
        z_ref[p] = LOG2_E * lax.dot_general(stacked_queries(i), keys, (((1,), (1,)), ((), ())),
                                            preferred_element_type=F32)

    def stage_exponents(p, delta):
        z = z_ref[p]
        softplus = softplus2(z)
        carry = None
        for b in reversed(range(nsub)):
            cols = slice(b * kb, (b + 1) * kb)
            masked = (b + 1) * kb > delta
            valid = (rel + b * kb) < delta
            nl = softplus[:, cols]
            if masked:
                nl = jnp.where(valid, nl, 0.0)
            excl, tot = suffix(nl)
            arg = z[:, cols] - softplus[:, cols] - excl
            if carry is not None:
                arg = arg - carry
            if masked:
                arg = jnp.where(valid, arg, NEG_BIG)
            arg_ref[p, :, cols] = arg
            carry = tot if carry is None else carry + tot
        ctot_ref[p] = carry
        return jnp.min(carry)

    def stage_output(i, p, s):
        vals = v_ref[pl.ds(window_start(i), win), :]
        acc_ref[s] = jnp.dot(jnp.exp2(arg_ref[p]).astype(BF16), vals, preferred_element_type=F32)
        carry_ref[s] = ctot_ref[p]

    def finish(i, s, cmin):
        def cond(state):
            pos, cmin = state
            return (pos > -kb) & (cmin <= ATTN_STOP)

        def older(state):
            pos, _ = state
            start = pl.multiple_of(jnp.maximum(pos, 0), tq)
            keys = k_ref[pl.ds(start, kb), :]
            vals = v_ref[pl.ds(start, kb), :]
            fresh = lax.broadcasted_iota(jnp.int32, (1, kb), 1) < pos + kb - start
            nlog_nb, log_b = scores(stacked_queries(i), keys)
            nlog_nb = jnp.where(fresh, nlog_nb, 0.0)
            excl, tot = suffix(nlog_nb)
            carry = carry_ref[s]
            a = jnp.where(fresh, jnp.exp2(log_b - excl - carry), 0.0)
            acc_ref[s] += jnp.dot(a.astype(BF16), vals, preferred_element_type=F32)
            carry_ref[s] = carry + tot
            return pos - kb, jnp.min(carry + tot)

        lax.while_loop(cond, older, (jnp.asarray(window_start(i) - kb, jnp.int32), cmin))
        store(i, acc_ref[s])

    for i in range(first):
        stage_scores(i, 0)
        cmin = stage_exponents(0, i * tq - window_start(i))
        stage_output(i, 0, 0)
        finish(i, 0, cmin)

    n_pipe = seq // tq - first
    unroll = acc_ref.shape[0]
    lag = z_ref.shape[0] - 1
    n_slots = lag + 1
    pending = []
    for tau in range(2 * lag):
        stage_scores(first + tau, tau % n_slots)
        if tau >= lag:
            pending.append(stage_exponents((tau - lag) % n_slots, lookback))

    def steady(m, pending):
        pending = list(pending)
        tau0 = 2 * lag + unroll * m
        done = []
        for u in range(unroll):
            stage_output(first + tau0 + u - 2 * lag, u % n_slots, u)
            done.append((first + tau0 + u - 2 * lag, u, pending.pop(0)))
            pending.append(stage_exponents((lag + u) % n_slots, lookback))
            stage_scores(first + tau0 + u, (2 * lag + u) % n_slots)
        for block, s, cmin in done:
            finish(block, s, cmin)
        return tuple(pending)

    pending = list(lax.fori_loop(0, (n_pipe - 2 * lag) // unroll, steady, tuple(pending)))
    for tau in range(n_pipe, n_pipe + 2 * lag):
        j = tau - 2 * lag
        stage_output(first + j, j % n_slots, 0)
        cmin = pending.pop(0)
        if tau - lag < n_pipe:
            pending.append(stage_exponents((tau - lag) % n_slots, lookback))
        finish(first + j, 0, cmin)


def _attention(q, k, v, batch, seq):
    n, width = q.shape
    tq, kb, nsub = ATTN_QUERY_ROWS, LANES, ATTN_WINDOW_BLOCKS
    n_blocks = seq // tq
    lag, n_slots = ATTN_STAGE_LAG, ATTN_STAGE_LAG + 1
    clipped = -(-(nsub * kb - tq) // tq)
    unroll = first = None
    for u in range(ATTN_UNROLL - ATTN_UNROLL % n_slots, 0, -n_slots):
        for f in range(clipped, clipped + u):
            rest = n_blocks - f - 2 * lag
            if rest >= 0 and rest % u == 0 and first is None:
                unroll, first = u, f
    assert first is not None and seq % tq == 0 and seq >= nsub * kb
    blk = pl.BlockSpec((seq, LANES), lambda b, hp: (b, hp))
    rows = HEADS_PER_BLOCK * tq
    stage_buf = pltpu.VMEM((n_slots, rows, nsub * kb), F32)
    carry_buf = pltpu.VMEM((n_slots, rows, kb), F32)
    row_buf = pltpu.VMEM((unroll, rows, LANES), F32)
    return pl.pallas_call(
        functools.partial(_attn_kernel, tq=tq, kb=kb, nsub=nsub, first=first),
        grid=(batch, width // LANES),
        in_specs=[blk, blk, blk],
        out_specs=blk,
        out_shape=jax.ShapeDtypeStruct((n, width), F32),
        scratch_shapes=[pltpu.VMEM((kb, 2 * kb), BF16),
                        stage_buf, stage_buf, carry_buf, row_buf, row_buf],
        compiler_params=_cparams(("arbitrary", "arbitrary")),
        name="attn",
    )(q, k, v)


def _gelu_tanh(x):
    c = math.sqrt(2.0 / math.pi)
    half_x = 0.5 * x
    return half_x + half_x * jnp.tanh(x * (c + (c * 0.044715) * (x * x)))


def _sigmoid(x):
    return 0.5 + 0.5 * jnp.tanh(0.5 * x)


def _lru_kernel(*refs, ts, n_slab):
    xl_refs, gl_refs = refs[0:n_slab], refs[n_slab:2 * n_slab]
    cw_ref, cb_ref, wr_ref, br_ref, wi_ref, bi_ref, lam_ref = refs[2 * n_slab:2 * n_slab + 7]
    o_refs = refs[2 * n_slab + 7:3 * n_slab + 7]
    (tail_ref, a7_ref, u7_ref, hp_ref, pa_ref, pu_ref, h_ref,
     wr_bd, wi_bd) = refs[3 * n_slab + 7:]
    t = pl.program_id(1)
    groups = ts // SUBLANES

    @pl.when(t == 0)
    def _():
        tail_ref[...] = jnp.zeros_like(tail_ref)
        h_ref[...] = jnp.zeros_like(h_ref)
        per = LANES // wr_ref.shape[1]
        for src, dst in ((wr_ref, wr_bd), (wi_ref, wi_bd)):
            for c in range(n_slab):
                rows = []
                for p in range(per):
                    blk = src[c * per + p]
                    rows.append(jnp.concatenate(
                        [blk if q == p else jnp.zeros_like(blk) for q in range(per)], axis=1))
                dst[c] = jnp.concatenate(rows, axis=0).astype(BF16)

    first_group = lax.broadcasted_iota(jnp.int32, (groups, LANES), 0) == 0
    for c in range(n_slab):
        lanes = slice(c * LANES, (c + 1) * LANES)
        x = [xl_refs[c][pl.ds(s, groups, stride=SUBLANES), :] for s in range(SUBLANES)]
        shifted = {}
        for s in range(SUBLANES - (CONV_W - 1), SUBLANES):
            shifted[s] = jnp.where(first_group, tail_ref[c, s:s + 1, :], pltpu.roll(x[s], 1, axis=0))
            tail_ref[c, s:s + 1, :] = x[s][groups - 1:groups, :]
        conv = []
        for s in range(SUBLANES):
            y = cb_ref[:, lanes]
            for w in range(CONV_W):
                j = s - (CONV_W - 1) + w
                y = y + (x[j] if j >= 0 else shifted[j + SUBLANES]) * cw_ref[w:w + 1, lanes]
            conv.append(y)
        xc = jnp.concatenate(conv, axis=0)

        xcb = xc.astype(BF16)
        r = _sigmoid(jnp.dot(xcb, wr_bd[c], preferred_element_type=F32) + br_ref[:, lanes])
        ig = _sigmoid(jnp.dot(xcb, wi_bd[c], preferred_element_type=F32) + bi_ref[:, lanes])
        lam = lam_ref[:, lanes]
        log_sig_lam = -(jnp.maximum(-lam, 0.0) + jnp.log1p(jnp.exp(-jnp.abs(lam))))
        log_a = r * (RG_C * log_sig_lam)
        a = jnp.exp(log_a)
        th = jnp.tanh(log_a)
        one_m_a2 = -2.0 * th / (1.0 - th)
        root = jnp.where(one_m_a2 > 0.0, one_m_a2 * lax.rsqrt(one_m_a2), 0.0)
        u = root * (ig * xc)

        a_run = u_run = None
        for s in range(SUBLANES):
            rows = slice(s * groups, (s + 1) * groups)
            if s == 0:
                a_run, u_run = a[rows], u[rows]
            else:
                u_run = a[rows] * u_run + u[rows]
                a_run = a_run * a[rows]
            pa_ref[c, rows, :] = a_run
            pu_ref[c, rows, :] = u_run
        a7_ref[c] = a_run
        u7_ref[c] = u_run

    def group(g, hs):
        nxt = []
        for c in range(n_slab):
            hp_ref[c, pl.ds(g, 1), :] = hs[c]
            nxt.append(a7_ref[c, pl.ds(g, 1), :] * hs[c] + u7_ref[c, pl.ds(g, 1), :])
        return tuple(nxt)

    hs = lax.fori_loop(0, groups, group, tuple(h_ref[c] for c in range(n_slab)), unroll=8)
    for c in range(n_slab):
        h_ref[c] = hs[c]

    for c in range(n_slab):
        h_in = hp_ref[c]
        for s in range(SUBLANES):
            rows = slice(s * groups, (s + 1) * groups)
            hseq = pu_ref[c, rows, :] + pa_ref[c, rows, :] * h_in
            gate = _gelu_tanh(gl_refs[c][pl.ds(s, groups, stride=SUBLANES), :])
            o_refs[c][pl.ds(s, groups, stride=SUBLANES), :] = hseq * gate


def _lru(xl, gl, conv_w, conv_b, w_r, br, w_i, bi, lam, batch, seq, ts):
    n, width = xl.shape
    nt = seq // ts
    n_slab = width // LANES
    blocks, block_w, _ = w_r.shape
    assert blocks * block_w == width and LANES % block_w == 0 and ts % (SUBLANES * SUBLANES) == 0
    groups = ts // SUBLANES
    gate_bd = pltpu.VMEM((n_slab, LANES, LANES), BF16)
    slab = [pl.BlockSpec((ts, LANES), functools.partial(lambda b, t, c: (b * nt + t, c), c=c))
            for c in range(n_slab)]
    const2 = lambda b, t: (0, 0)
    const3 = lambda b, t: (0, 0, 0)
    vec = pl.BlockSpec((1, width), const2)
    per_group = pltpu.VMEM((n_slab, groups, LANES), F32)
    per_step = pltpu.VMEM((n_slab, ts, LANES), F32)
    return pl.pallas_call(
        functools.partial(_lru_kernel, ts=ts, n_slab=n_slab),
        grid=(batch, nt),
        in_specs=slab + slab + [pl.BlockSpec((CONV_W, width), const2), vec,
                                pl.BlockSpec(w_r.shape, const3), vec,
                                pl.BlockSpec(w_i.shape, const3), vec, vec],
        out_specs=[pl.BlockSpec((ts, LANES), lambda b, t: (b * nt + t, 0))] * n_slab,
        out_shape=[jax.ShapeDtypeStruct((n, LANES), F32)] * n_slab,
        scratch_shapes=[pltpu.VMEM((n_slab, SUBLANES, LANES), F32),
                        per_group, per_group, per_group, per_step, per_step,
                        pltpu.VMEM((n_slab, 1, LANES), F32), gate_bd, gate_bd],
        compiler_params=_cparams(("arbitrary", "arbitrary")),
        name="lru",
    )(*([xl] * n_slab), *([gl] * n_slab), conv_w, conv_b, w_r, br, w_i, bi, lam)


def _mix_route_kernel(sb_ref, *refs, tm, n_slab):
    lru_refs = refs[:n_slab]
    (x_ref, sbg_ref, lrug_ref, wo_ref, n2g_ref, wrs_ref, rb_ref,
     h_ref, u2_ref, rr_ref, rwt_ref, tc_ref, before_ref, u2s_ref, wo_bf) = refs[n_slab:]

    @pl.when(pl.program_id(0) == 0)
    def _():
        wo_bf[...] = wo_ref[...].astype(BF16)

    _mix_route_body(sb_ref, lru_refs, x_ref, sbg_ref, lrug_ref, wo_bf, n2g_ref, wrs_ref, rb_ref,
                    h_ref, u2_ref, rr_ref, rwt_ref, tc_ref, before_ref, u2s_ref, tm)


def _mix_route_body(sb_ref, lru_refs, x_ref, sbg_ref, lrug_ref, wo_ref, n2g_ref, wrs_ref, rb_ref,
                    h_ref, u2_ref, rr_ref, rwt_ref, tc_ref, before_ref, u2s_ref, tm):
    step = pl.program_id(0)
    half = sb_ref.shape[1]
    n_sub = x_ref.shape[0] // tm

    @pl.when(step == 0)
    def _():
        r_id = lax.broadcasted_iota(jnp.int32, (tm, tm), 0)
        c_id = lax.broadcasted_iota(jnp.int32, (tm, tm), 1)
        before_ref[...] = (r_id < c_id).astype(BF16)
        tc_ref[...] = jnp.zeros_like(tc_ref)
        u2s_ref[...] = jnp.zeros_like(u2s_ref)

    def route(j):
        rows = slice(j * tm, (j + 1) * tm)
        u2 = u2s_ref[rows, :]
        u_hi = u2.astype(BF16)
        u_lo = (u2 - u_hi.astype(F32)).astype(BF16)
        nt_dims = (((1,), (1,)), ((), ()))
        n_rows = rb_ref.shape[0]
        both = lax.dot_general(wrs_ref[...], u_hi, nt_dims, preferred_element_type=F32)
        lt = (both[0:n_rows] + both[n_rows:2 * n_rows]
              + lax.dot_general(wrs_ref[0:n_rows, :], u_lo, nt_dims, preferred_element_type=F32)
              + rb_ref[...])

        sub = lax.broadcasted_iota(jnp.int32, (SUBLANES, tm), 0)

        def top1(x):
            m = jnp.max(x, axis=0, keepdims=True)
            idx = jnp.min(jnp.where(x == m, sub, SUBLANES), axis=0, keepdims=True)
            return m, idx

        grp = lt[0:SUBLANES, :]
        g_max, g_idx = top1(grp)
        g_p = 1.0 / jnp.sum(jnp.exp(grp - g_max), axis=0, keepdims=True)
        fine = lt[SUBLANES:2 * SUBLANES, :]
        for g in range(1, N_GROUPS):
            fine = jnp.where(g_idx == g, lt[(g + 1) * SUBLANES:(g + 2) * SUBLANES, :], fine)
        m1, i1 = top1(fine)
        m2, i2 = top1(jnp.where(sub == i1, -jnp.inf, fine))
        e2 = jnp.exp(m2 - m1)
        p1 = 1.0 / (1.0 + e2)
        w1 = g_p * p1
        w2 = g_p * (e2 * p1)
        x1 = g_idx * EXPERTS_PER_GROUP + i1
        x2 = g_idx * EXPERTS_PER_GROUP + i2

        eid = lax.broadcasted_iota(jnp.int32, (N_EXPERTS, tm), 0)
        oh1 = jnp.where(eid == x1, 1.0, 0.0)
        oh2 = jnp.where(eid == x2, 1.0, 0.0)
        pre = jnp.dot(jnp.concatenate([oh1, oh2], axis=0).astype(BF16), before_ref[...],
                      preferred_element_type=F32)
        pre1, pre2 = pre[0:N_EXPERTS], pre[N_EXPERTS:2 * N_EXPERTS]
        cnt1 = jnp.sum(oh1, axis=1, keepdims=True)
        cnt2 = jnp.sum(oh2, axis=1, keepdims=True)
        seg8 = jnp.maximum(jnp.floor((cnt1 + cnt2 + (SUBLANES - 1.0)) * (1.0 / SUBLANES)), 1.0)
        e_r = lax.broadcasted_iota(jnp.int32, (N_EXPERTS, N_EXPERTS), 0)
        e_c = lax.broadcasted_iota(jnp.int32, (N_EXPERTS, N_EXPERTS), 1)
        lower = jnp.where(e_c < e_r, 1.0, 0.0).astype(BF16)
        seg8_b = jnp.broadcast_to(seg8, (N_EXPERTS, LANES)).astype(BF16)
        seg_off = SUBLANES * jnp.dot(lower, seg8_b, preferred_element_type=F32)[:, 0:1]
        pos1 = jnp.sum(oh1 * (pre1 + seg_off), axis=0, keepdims=True)
        pos2 = jnp.sum(oh2 * (pre2 + (seg_off + cnt1)), axis=0, keepdims=True)

        lane = lax.broadcasted_iota(jnp.int32, tc_ref.shape, 1)
        seg_rows = jnp.broadcast_to(seg8 * SUBLANES, tc_ref.shape).astype(jnp.int32)
        tile = (step - 1) * n_sub + j
        tc_ref[...] = jnp.where(lane == tile, seg_rows, tc_ref[...])

        zrow = jnp.zeros((SUBLANES - 4, tm), jnp.int32)
        rr_ref[:, rows] = jnp.concatenate(
            [pos1.astype(jnp.int32), pos2.astype(jnp.int32), x1, x2, zrow], axis=0)
        wt = jnp.concatenate([w1, w2, pos1, pos2, jnp.zeros((LANES - 4, tm), F32)], axis=0)
        rwt_ref[rows, :] = wt.T

    m_sb = _rms_f32(sb_ref[...], sbg_ref[...]).astype(BF16)
    lru = jnp.concatenate([r[...] for r in lru_refs], axis=-1)
    m_lru = _rms_f32(lru, lrug_ref[...]).astype(BF16)
    h_ref[...] = (x_ref[...]
                  + jnp.dot(m_sb, wo_ref[0:half, :], preferred_element_type=F32)
                  + jnp.dot(m_lru, wo_ref[half:2 * half, :], preferred_element_type=F32))

    for j in range(n_sub):
        route(j)

    u2_next = _rms_f32(h_ref[...], n2g_ref[...])
    u2_ref[...] = u2_next.astype(BF16)
    u2s_ref[...] = u2_next


def _mix_route(sb, lru, x2, sbg, lrug, w_out, n2g, wr_stack, rbias, tm, tp):
    n, d = x2.shape
    half = sb.shape[1]
    n_tiles = n // tp
    row = lambda i: (jnp.minimum(i, n_tiles - 1), 0)
    routed = lambda i: (jnp.maximum(i - 1, 0), 0)
    const = lambda i: (0, 0)
    return pl.pallas_call(
        functools.partial(_mix_route_kernel, tm=tm, n_slab=len(lru)),
        grid=(n_tiles + 1,),
        in_specs=[pl.BlockSpec((tp, half), row)] + [pl.BlockSpec((tp, LANES), row)] * len(lru)
                 + [pl.BlockSpec((tp, d), row),
                  pl.BlockSpec((1, half), const), pl.BlockSpec((1, half), const),
                  pl.BlockSpec(w_out.shape, const, pipeline_mode=pl.Buffered(1)),
                  pl.BlockSpec((1, d), const),
                  pl.BlockSpec(wr_stack.shape, const), pl.BlockSpec(rbias.shape, const)],
        out_specs=[pl.BlockSpec((tp, d), row), pl.BlockSpec((tp, d), row),
                   pl.BlockSpec((SUBLANES, tp), lambda i: (0, jnp.maximum(i - 1, 0))),
                   pl.BlockSpec((tp, LANES), routed),
                   pl.BlockSpec((N_EXPERTS, LANES), const)],
        out_shape=[jax.ShapeDtypeStruct((n, d), F32), jax.ShapeDtypeStruct((n, d), BF16),
                   jax.ShapeDtypeStruct((SUBLANES, n), jnp.int32),
                   jax.ShapeDtypeStruct((n, LANES), F32),
                   jax.ShapeDtypeStruct((N_EXPERTS, LANES), jnp.int32)],
        scratch_shapes=[pltpu.VMEM((tm, tm), BF16), pltpu.VMEM((tp, d), F32),
                        pltpu.VMEM(w_out.shape, BF16)],
        compiler_params=_cparams(("arbitrary",)),
        name="mix_route",
    )(sb, *lru, x2, sbg, lrug, w_out, n2g, wr_stack, rbias)


def _pack_halves(x):
    half = x.shape[1] // 2
    lo = lax.shift_right_logical(lax.bitcast_convert_type(x[:, :half], jnp.int32), 16)
    hi = lax.bitcast_convert_type(x[:, half:], jnp.int32) & HIGH_HALF
    return hi | lo


def _unpack_halves(p):
    lo = lax.bitcast_convert_type(lax.shift_left(p, 16), F32)
    hi = lax.bitcast_convert_type(p & HIGH_HALF, F32)
    return lo.astype(BF16), hi.astype(BF16)


def _segment_copies(tile, c8_ref, loff_ref, goff_ref, make):
    for e in range(N_EXPERTS):
        idx = tile * N_EXPERTS + e
        rows = pl.multiple_of(c8_ref[idx], SUBLANES)
        lo = pl.multiple_of(loff_ref[idx], SUBLANES)
        go = pl.multiple_of(goff_ref[idx], SUBLANES)
        make(pl.ds(lo, rows), pl.ds(go, rows)).start()


def _dispatch_kernel(c8_ref, loff_ref, goff_ref, tot_ref, used_ref,
                     rr_ref, u2_ref, xs_ref, lbuf, zbuf, sem, zsem, *, td, lrows):
    i = pl.program_id(0)
    slot = i % 2

    r_id = lax.broadcasted_iota(jnp.int32, (lrows, td), 0)
    perm = jnp.where(r_id == rr_ref[0:1, :], 1.0, jnp.where(r_id == rr_ref[1:2, :], 1.0, 0.0))
    sorted_rows = jnp.dot(perm.astype(BF16), u2_ref[...], preferred_element_type=F32)
    lbuf[slot] = _pack_halves(sorted_rows)

    @pl.when(i == 0)
    def _():
        zbuf[...] = jnp.zeros_like(zbuf)
        chunk = zbuf.shape[0]
        used = used_ref[0]
        spare = xs_ref.shape[0] - used
        n_fill = (spare + chunk - 1) // chunk

        def fill_copy(k):
            rows = pl.multiple_of(jnp.minimum(chunk, spare - k * chunk), SUBLANES)
            start = pl.multiple_of(used + k * chunk, SUBLANES)
            return pltpu.make_async_copy(zbuf.at[pl.ds(0, rows)], xs_ref.at[pl.ds(start, rows)],
                                         zsem)

        def fill_start(k, c):
            fill_copy(k).start()
            return c

        def fill_wait(k, c):
            fill_copy(k).wait()
            return c

        lax.fori_loop(0, n_fill, fill_start, 0)
        lax.fori_loop(0, n_fill, fill_wait, 0)

    def wait_tile(tile, s):
        rows = pl.multiple_of(tot_ref[tile], SUBLANES)
        pltpu.make_async_copy(lbuf.at[s, pl.ds(0, rows)], xs_ref.at[pl.ds(0, rows)], sem).wait()

    @pl.when(i > 0)
    def _():
        wait_tile(i - 1, 1 - slot)

    _segment_copies(i, c8_ref, loff_ref, goff_ref,
                    lambda loc, glob: pltpu.make_async_copy(lbuf.at[slot, loc], xs_ref.at[glob], sem))

    @pl.when(i == pl.num_programs(0) - 1)
    def _():
        wait_tile(i, slot)


def _dispatch(c8, loff, goff, tot, used, rr, u2, p_rows, td, lrows, tme):
    n, d = u2.shape
    pmap = lambda i, *_: (0, i)
    return pl.pallas_call(
        functools.partial(_dispatch_kernel, td=td, lrows=lrows),
        grid_spec=pltpu.PrefetchScalarGridSpec(
            num_scalar_prefetch=5,
            grid=(n // td,),
            in_specs=[pl.BlockSpec((SUBLANES, td), pmap),
                      pl.BlockSpec((td, d), lambda i, *_: (i, 0))],
            out_specs=pl.BlockSpec(memory_space=pl.ANY),
            scratch_shapes=[pltpu.VMEM((2, lrows, d // 2), jnp.int32),
                            pltpu.VMEM((tme, d // 2), jnp.int32),
                            pltpu.SemaphoreType.DMA(()), pltpu.SemaphoreType.DMA(())]),
        out_shape=jax.ShapeDtypeStruct((p_rows, d // 2), jnp.int32),
        compiler_params=_cparams(("arbitrary",)),
        name="dispatch",
    )(c8, loff, goff, tot, used, rr, u2)


def _experts_kernel(eoff_ref, erows_ref, xs_ref, wg_ref, wu_ref, wd_ref, ys_ref,
                    wg_bf, wu_bf, wd_bf, xbuf, ybuf, sem_in, sem_out, state, *, tme):
    e = pl.program_id(0)
    n_experts = pl.num_programs(0)
    rows = erows_ref[e]
    off = eoff_ref[e]
    n_tiles = (rows + tme - 1) // tme
    nxt = jnp.minimum(e + 1, n_experts - 1)
    prefetch = (rows > 0) & (e + 1 < n_experts) & (erows_ref[nxt] > 0)

    def tile_rows(total, k):
        return pl.multiple_of(jnp.minimum(tme, total - k * tme), SUBLANES)

    def in_copy(start, r, slot):
        start = pl.multiple_of(start, SUBLANES)
        return pltpu.make_async_copy(xs_ref.at[pl.ds(start, r)], xbuf.at[slot, pl.ds(0, r)],
                                     sem_in.at[slot])

    def out_copy(start, r, slot):
        start = pl.multiple_of(start, SUBLANES)
        return pltpu.make_async_copy(ybuf.at[slot, pl.ds(0, r)], ys_ref.at[pl.ds(start, r)],
                                     sem_out.at[slot])

    @pl.when(e == 0)
    def _():
        for s in range(4):
            state[s] = 0
        xbuf[...] = jnp.zeros_like(xbuf)

    @pl.when(rows > 0)
    def _():
        wg_bf[...] = wg_ref[0].astype(BF16)
        wu_bf[...] = wu_ref[0].astype(BF16)
        wd_bf[...] = wd_ref[0].astype(BF16)
        done = state[0]

        @pl.when(state[1] == 0)
        def _():
            in_copy(off, tile_rows(rows, 0), done % 2).start()

        def tile(k, c):
            slot = (done + k) % 2
            r = tile_rows(rows, k)

            @pl.when(k + 1 < n_tiles)
            def _():
                in_copy(off + (k + 1) * tme, tile_rows(rows, k + 1), 1 - slot).start()

            @pl.when((k + 1 == n_tiles) & prefetch)
            def _():
                in_copy(eoff_ref[nxt], tile_rows(erows_ref[nxt], 0), 1 - slot).start()

            in_copy(off, r, slot).wait()

            @pl.when(done + k >= 2)
            def _():
                out_copy(0, pl.multiple_of(state[2 + slot], SUBLANES), slot).wait()

            def mlp(start, n):
                x = jnp.concatenate(_unpack_halves(xbuf[slot, start:start + n]), axis=1)
                hg = jnp.dot(x, wg_bf[...], preferred_element_type=F32)
                hu = jnp.dot(x, wu_bf[...], preferred_element_type=F32)
                act = (hg * jax.nn.sigmoid(hg) * hu).astype(BF16)
                y = jnp.dot(act, wd_bf[...], preferred_element_type=F32)
                ybuf[slot, start:start + n] = _pack_halves(y.astype(BF16).astype(F32))

            for below, n in zip((0,) + EXPERT_TILE_SIZES, EXPERT_TILE_SIZES):
                @pl.when((r > below) & (r <= n))
                def _(n=n):
                    mlp(0, n)

            out_copy(off + k * tme, r, slot).start()
            state[2 + slot] = r
            return c

        lax.fori_loop(0, n_tiles, tile, 0)
        state[0] = done + n_tiles

    state[1] = prefetch.astype(jnp.int32)

    @pl.when(e == n_experts - 1)
    def _():
        total = eoff_ref[e] + rows
        for slot in range(2):
            @pl.when(state[0] > slot)
            def _(slot=slot):
                out_copy(0, pl.multiple_of(state[2 + slot], SUBLANES), slot).wait()
        ybuf[0] = jnp.zeros(ybuf.shape[1:], ybuf.dtype)
        spare = ys_ref.shape[0] - total
        n_fill = (spare + tme - 1) // tme

        def fill_copy(k):
            return out_copy(total + k * tme, tile_rows(spare, k), 0)

        def fill_start(k, c):
            fill_copy(k).start()
            return c

        def fill_wait(k, c):
            fill_copy(k).wait()
            return c

        lax.fori_loop(0, n_fill, fill_start, 0)
        lax.fori_loop(0, n_fill, fill_wait, 0)


def _experts(eoff, erows, xs, wg, wu, wd, tme):
    p = xs.shape[0]
    n_experts, d, de = wg.shape
    wmap = lambda e, *_: (e, 0, 0)
    return pl.pallas_call(
        functools.partial(_experts_kernel, tme=tme),
        grid_spec=pltpu.PrefetchScalarGridSpec(
            num_scalar_prefetch=2,
            grid=(n_experts,),
            in_specs=[pl.BlockSpec(memory_space=pl.ANY),
                      pl.BlockSpec((1, d, de), wmap), pl.BlockSpec((1, d, de), wmap),
                      pl.BlockSpec((1, de, d), wmap)],
            out_specs=pl.BlockSpec(memory_space=pl.ANY),
            scratch_shapes=[pltpu.VMEM((d, de), BF16), pltpu.VMEM((d, de), BF16),
                            pltpu.VMEM((de, d), BF16),
                            pltpu.VMEM((2, tme, d // 2), jnp.int32),
                            pltpu.VMEM((2, tme, d // 2), jnp.int32),
                            pltpu.SemaphoreType.DMA((2,)), pltpu.SemaphoreType.DMA((2,)),
                            pltpu.SMEM((4,), jnp.int32)]),
        out_shape=jax.ShapeDtypeStruct((p, d // 2), jnp.int32),
        compiler_params=_cparams(("arbitrary",)),
        name="experts",
    )(eoff, erows, xs, wg, wu, wd)


def _combine_kernel(c8_ref, loff_ref, goff_ref, tot_ref,
                    rwt_ref, h_ref, fg_ref, ys_ref, y_ref, ybuf, sems, *, tc, lrows):
    i = pl.program_id(0)
    slot = i % 2

    def gather_tile(tile, s):
        _segment_copies(tile, c8_ref, loff_ref, goff_ref,
                        lambda loc, glob: pltpu.make_async_copy(ys_ref.at[glob], ybuf.at[s, loc],
                                                                sems.at[s]))

    @pl.when(i == 0)
    def _():
        ybuf[...] = jnp.zeros_like(ybuf)
        gather_tile(0, 0)

    @pl.when(i + 1 < pl.num_programs(0))
    def _():
        gather_tile(i + 1, 1 - slot)

    rows = pl.multiple_of(tot_ref[i], SUBLANES)
    pltpu.make_async_copy(ys_ref.at[pl.ds(0, rows)], ybuf.at[slot, pl.ds(0, rows)],
                          sems.at[slot]).wait()

    w = rwt_ref[...]
    c_id = lax.broadcasted_iota(jnp.int32, (tc, lrows), 1)
    pos1 = w[:, 2:3].astype(jnp.int32)
    pos2 = w[:, 3:4].astype(jnp.int32)
    wmat = jnp.where(c_id == pos1, w[:, 0:1], jnp.where(c_id == pos2, w[:, 1:2], 0.0)).astype(BF16)
    half = h_ref.shape[1] // 2
    outs, sumsq = [], 0.0
    for part, cols in zip(_unpack_halves(ybuf[slot]), (slice(0, half), slice(half, 2 * half))):
        out = h_ref[:, cols] + jnp.dot(wmat, part, preferred_element_type=F32)
        sumsq = sumsq + jnp.sum(out * out, axis=-1, keepdims=True)
        outs.append(out)
    scale = lax.rsqrt(sumsq * (1.0 / (2 * half)) + EPS)
    for out, cols in zip(outs, (slice(0, half), slice(half, 2 * half))):
        y_ref[:, cols] = out * scale * fg_ref[:, cols]


def _combine(c8, loff, goff, tot, rwt, h, final_g, ys, tc, lrows):
    n, d = h.shape
    return pl.pallas_call(
        functools.partial(_combine_kernel, tc=tc, lrows=lrows),
        grid_spec=pltpu.PrefetchScalarGridSpec(
            num_scalar_prefetch=4,
            grid=(n // tc,),
            in_specs=[pl.BlockSpec((tc, LANES), lambda i, *_: (i, 0)),
                      pl.BlockSpec((tc, d), lambda i, *_: (i, 0)),
                      pl.BlockSpec((1, d), lambda i, *_: (0, 0)),
                      pl.BlockSpec(memory_space=pl.ANY)],
            out_specs=pl.BlockSpec((tc, d), lambda i, *_: (i, 0)),
            scratch_shapes=[pltpu.VMEM((2, lrows, d // 2), jnp.int32),
                            pltpu.SemaphoreType.DMA((2,))]),
        out_shape=jax.ShapeDtypeStruct((n, d), F32),
        compiler_params=_cparams(("arbitrary",)),
        name="combine",
    )(c8, loff, goff, tot, rwt, h, final_g, ys)


def _router_tables(w_group, b_group, w_fine, b_fine):
    d = w_group.shape[0]
    pad_g, pad_f = SUBLANES - N_GROUPS, ROUTER_ROWS - SUBLANES - N_EXPERTS
    w = jnp.concatenate([w_group.T, jnp.zeros((pad_g, d), F32), w_fine.T,
                         jnp.zeros((pad_f, d), F32)], axis=0)
    b = jnp.concatenate([b_group, jnp.full((pad_g,), NEG_BIG, F32), b_fine,
                         jnp.full((pad_f,), NEG_BIG, F32)])
    w_hi = w.astype(BF16)
    w_lo = (w - w_hi.astype(F32)).astype(BF16)
    return jnp.concatenate([w_hi, w_lo], axis=0), b.reshape(ROUTER_ROWS, 1)


def kernel(x, norm1_g, w_in, conv_w, conv_b, w_rgate, b_rgate, w_igate, b_igate, lam, sb_norm_g,
           lru_norm_g, w_out, norm2_g, w_group, b_group, w_fine, b_fine, w_e_gate, w_e_up,
           w_e_down, final_g):
    batch, seq, d = x.shape
    n = batch * seq
    width = w_in.shape[1] // 5
    tm = min(TOKEN_TILE, seq)
    tp = min(ROW_TILE, seq)
    tme = EXPERT_TILE_ROWS

    x2 = x.reshape(n, d)
    vec = lambda a: a.reshape(1, -1)

    q, k, v, xl, gl = _in_proj(x2, vec(norm1_g), w_in, width, tp)
    out_sb = _attention(q, k, v, batch, seq)
    out_lru = _lru(xl, gl, conv_w, vec(conv_b), w_rgate, vec(b_rgate), w_igate, vec(b_igate),
                   vec(lam), batch, seq, tp)

    wr_stack, rbias = _router_tables(w_group, b_group, w_fine, b_fine)
    h, u2, rr, rwt, tcnt = _mix_route(out_sb, out_lru, x2, vec(sb_norm_g), vec(lru_norm_g),
                                      w_out, vec(norm2_g), wr_stack, rbias, tm, tp)

    n_tiles = n // tm
    assert n_tiles <= LANES, "one lane of the per-tile count table per token tile"
    c8 = tcnt[:, :n_tiles].T
    erows = jnp.sum(c8, axis=0)
    eoff = jnp.cumsum(erows) - erows
    goff = eoff[None, :] + jnp.cumsum(c8, axis=0) - c8
    loff = jnp.cumsum(c8, axis=1) - c8
    tot = jnp.sum(c8, axis=1)
    lrows = 2 * tm + N_EXPERTS * SUBLANES
    p_rows = 2 * n + n_tiles * N_EXPERTS * SUBLANES
    i32 = lambda a: a.reshape(-1).astype(jnp.int32)
    c8, loff, goff, tot, eoff, erows = (i32(a) for a in (c8, loff, goff, tot, eoff, erows))

    xs = _dispatch(c8, loff, goff, tot, jnp.sum(erows, keepdims=True), rr, u2, p_rows, tm, lrows, tme)
    ys = _experts(eoff, erows, xs, w_e_gate, w_e_up, w_e_down, tme)
    y = _combine(c8, loff, goff, tot, rwt, h, vec(final_g), ys, tm, lrows)
    return y.reshape(batch, seq, d)
```

```python
import functools
import math

import jax
import jax.numpy as jnp
from jax import lax
from jax.experimental import pallas as pl
from jax.experimental.pallas import tpu as pltpu

F32 = jnp.float32
BF16 = jnp.bfloat16

EPS = 1e-6
HEAD_DIM = 64
HEADS_PER_BLOCK = 2
LANES = 128
SUBLANES = 8
CONV_W = 4
RG_C = 8.0
N_GROUPS = 4
EXPERTS_PER_GROUP = 8
N_EXPERTS = N_GROUPS * EXPERTS_PER_GROUP
ROUTER_ROWS = 48
NEG_BIG = -1e30
LOG2_E = math.log2(math.e)
ATTN_STOP = 104.0 * LOG2_E
ATTN_QUERY_ROWS = 64
ATTN_WINDOW_BLOCKS = 2
ATTN_UNROLL = 24
ATTN_STAGE_LAG = 2
HIGH_HALF = -65536
EXPERT_TILE_ROWS = 1536
EXPERT_TILE_SIZES = (256, 512, 768, 1024, 1280, 1536)
TOKEN_TILE = 512
ROW_TILE = 1024

VMEM_LIMIT = 56 * 1024 * 1024


def _cparams(sem):
    return pltpu.CompilerParams(dimension_semantics=sem, vmem_limit_bytes=VMEM_LIMIT)


def _rms_f32(x, g):
    return x * lax.rsqrt(jnp.mean(x * x, axis=-1, keepdims=True) + EPS) * g


def _in_proj_kernel(x_ref, g_ref, w_ref, q_ref, k_ref, v_ref, xl_ref, gl_ref, w_bf, *,
                    width, q_scale):
    @pl.when(pl.program_id(0) == 0)
    def _():
        for c in range(w_ref.shape[1] // width):
            cols = slice(c * width, (c + 1) * width)
            w_bf[:, cols] = w_ref[:, cols].astype(BF16)

    u = _rms_f32(x_ref[...], g_ref[...]).astype(BF16)
    for c, o_ref in enumerate((q_ref, k_ref, v_ref, xl_ref, gl_ref)):
        p = jnp.dot(u, w_bf[:, c * width:(c + 1) * width], preferred_element_type=F32)
        if c == 0:
            p = p * q_scale
        o_ref[...] = p.astype(o_ref.dtype)


def _in_proj(x2, g, w, width, tm):
    n, d = x2.shape
    row = lambda i: (i, 0)
    out_bf = jax.ShapeDtypeStruct((n, width), BF16)
    out_f = jax.ShapeDtypeStruct((n, width), F32)
    return pl.pallas_call(
        functools.partial(_in_proj_kernel, width=width, q_scale=1.0 / math.sqrt(HEAD_DIM)),
        grid=(n // tm,),
        in_specs=[pl.BlockSpec((tm, d), row),
                  pl.BlockSpec((1, d), lambda i: (0, 0)),
                  pl.BlockSpec(w.shape, lambda i: (0, 0), pipeline_mode=pl.Buffered(1))],
        out_specs=[pl.BlockSpec((tm, width), row)] * 5,
        out_shape=[out_bf, out_bf, out_bf, out_f, out_f],
        scratch_shapes=[pltpu.VMEM(w.shape, BF16)],
        compiler_params=_cparams(("arbitrary",)),
        name="in_proj",
    )(x2, g, w)


def _attn_kernel(q_ref, k_ref, v_ref, o_ref, tri_ref, z_ref, arg_ref, ctot_ref, acc_ref, carry_ref,
                 *, tq, kb, nsub, fill):
    seq = q_ref.shape[0]
    win = nsub * kb
    lookback = win - tq
    lane = lax.broadcasted_iota(jnp.int32, (1, LANES), 1)
    rel = (lax.broadcasted_iota(jnp.int32, (tq, kb), 1)
           - lax.broadcasted_iota(jnp.int32, (tq, kb), 0))
    rel = jnp.concatenate([rel] * HEADS_PER_BLOCK, axis=0)

    k_r = lax.broadcasted_iota(jnp.int32, (kb, 2 * kb), 0)
    k_c = lax.broadcasted_iota(jnp.int32, (kb, 2 * kb), 1)
    tri_ref[...] = jnp.where(k_c >= kb, 1.0, jnp.where(k_r > k_c, 1.0, 0.0)).astype(BF16)

    def softplus2(z):
        return jnp.maximum(z, 0.0) + jnp.log2(1.0 + jnp.exp2(-jnp.abs(z)))

    def scores(qh, keys):
        z = LOG2_E * lax.dot_general(qh, keys, (((1,), (1,)), ((), ())),
                                     preferred_element_type=F32)
        nlog_nb = softplus2(z)
        return nlog_nb, z - nlog_nb

    def suffix(nlog_nb):
        r = jnp.dot(nlog_nb.astype(BF16), tri_ref[...], preferred_element_type=F32)
        return r[:, :kb], r[:, kb:]

    def stacked_queries(i):
        q = q_ref[pl.ds(i * tq, tq), :]
        return jnp.concatenate(
            [jnp.where((lane >= h * HEAD_DIM) & (lane < (h + 1) * HEAD_DIM), q, jnp.zeros_like(q))
             for h in range(HEADS_PER_BLOCK)], axis=0)

    def store(i, out):
        o_ref[pl.ds(i * tq, tq), :] = jnp.where(lane < HEAD_DIM, out[0:tq], out[tq:2 * tq])

    def window_start(i):
        if isinstance(i, int):
            return max(i * tq - lookback, 0)
        return pl.multiple_of(i * tq - lookback, tq)

    def stage_scores(i, p):
        keys = k_ref[pl.ds(window_start(i), win), :]
        z_ref[p] = LOG2_E * lax.dot_general(stacked_queries(i), keys, (((1,), (1,)), ((), ())),
                                            preferred_element_type=F32)

    def stage_exponents(p, delta):
        z = z_ref[p]
        softplus = softplus2(z)
        carry = None
        for b in reversed(range(nsub)):
            cols = slice(b * kb, (b + 1) * kb)
            masked = (b + 1) * kb > delta
            valid = (rel + b * kb) < delta
            nl = softplus[:, cols]
            if masked:
                nl = jnp.where(valid, nl, 0.0)
            excl, tot = suffix(nl)
            arg = z[:, cols] - softplus[:, cols] - excl
            if carry is not None:
                arg = arg - carry
            if masked:
                arg = jnp.where(valid, arg, NEG_BIG)
            arg_ref[p, :, cols] = arg
            carry = tot if carry is None else carry + tot
        ctot_ref[p] = carry
        return jnp.min(carry)

    def stage_output(i, p, s):
        vals = v_ref[pl.ds(window_start(i), win), :]
        acc_ref[s] = jnp.dot(jnp.exp2(arg_ref[p]).astype(BF16), vals, preferred_element_type=F32)
        carry_ref[s] = ctot_ref[p]

    def finish(i, s, cmin):
        def cond(state):
            pos, cmin = state
            return (pos > -kb) & (cmin <= ATTN_STOP)

        def older(state):
            pos, _ = state
            start = pl.multiple_of(jnp.maximum(pos, 0), tq)
            keys = k_ref[pl.ds(start, kb), :]
            vals = v_ref[pl.ds(start, kb), :]
            fresh = lax.broadcasted_iota(jnp.int32, (1, kb), 1) < pos + kb - start
            nlog_nb, log_b = scores(stacked_queries(i), keys)
            nlog_nb = jnp.where(fresh, nlog_nb, 0.0)
            excl, tot = suffix(nlog_nb)
            carry = carry_ref[s]
            a = jnp.where(fresh, jnp.exp2(log_b - excl - carry), 0.0)
            acc_ref[s] += jnp.dot(a.astype(BF16), vals, preferred_element_type=F32)
            carry_ref[s] = carry + tot
            return pos - kb, jnp.min(carry + tot)

        lax.while_loop(cond, older, (jnp.asarray(window_start(i) - kb, jnp.int32), cmin))
        store(i, acc_ref[s])

    n_blocks = seq // tq
    unroll = acc_ref.shape[0]
    lag = z_ref.shape[0] - 1
    n_slots = lag + 1
    pending = []
    for tau in range(fill):
        stage_scores(tau, tau % n_slots)
        if tau >= lag:
            j = tau - lag
            pending.append(stage_exponents(j % n_slots, j * tq - window_start(j)))
        if tau >= 2 * lag:
            j = tau - 2 * lag
            stage_output(j, j % n_slots, 0)
            finish(j, 0, pending.pop(0))

    def steady(m, pending):
        pending = list(pending)
        tau0 = fill + unroll * m
        done = []
        for u in range(unroll):
            stage_output(tau0 + u - 2 * lag, (fill - 2 * lag + u) % n_slots, u)
            done.append((tau0 + u - 2 * lag, u, pending.pop(0)))
            pending.append(stage_exponents((fill - lag + u) % n_slots, lookback))
            stage_scores(tau0 + u, (fill + u) % n_slots)
        for block, s, cmin in done:
            finish(block, s, cmin)
        return tuple(pending)

    pending = list(lax.fori_loop(0, (n_blocks - fill) // unroll, steady, tuple(pending)))
    for tau in range(n_blocks, n_blocks + 2 * lag):
        j = tau - 2 * lag
        stage_output(j, j % n_slots, 0)
        cmin = pending.pop(0)
        if tau - lag < n_blocks:
            pending.append(stage_exponents((tau - lag) % n_slots, lookback))
        finish(j, 0, cmin)


def _attention(q, k, v, batch, seq):
    n, width = q.shape
    tq, kb, nsub = ATTN_QUERY_ROWS, LANES, ATTN_WINDOW_BLOCKS
    n_blocks = seq // tq
    lag, n_slots = ATTN_STAGE_LAG, ATTN_STAGE_LAG + 1
    clipped = -(-(nsub * kb - tq) // tq)
    min_fill = clipped + 2 * lag
    assert seq % tq == 0 and seq >= nsub * kb and n_blocks >= min_fill
    options = [(f + 2 * (n_blocks - f) // u, u, f)
               for u in range(ATTN_UNROLL - ATTN_UNROLL % n_slots, 0, -n_slots)
               for f in range(min_fill, n_blocks + 1) if (n_blocks - f) % u == 0]
    _, unroll, fill = min(options)
    blk = pl.BlockSpec((seq, LANES), lambda b, hp: (b, hp))
    rows = HEADS_PER_BLOCK * tq
    stage_buf = pltpu.VMEM((n_slots, rows, nsub * kb), F32)
    carry_buf = pltpu.VMEM((n_slots, rows, kb), F32)
    row_buf = pltpu.VMEM((unroll, rows, LANES), F32)
    return pl.pallas_call(
        functools.partial(_attn_kernel, tq=tq, kb=kb, nsub=nsub, fill=fill),
        grid=(batch, width // LANES),
        in_specs=[blk, blk, blk],
        out_specs=blk,
        out_shape=jax.ShapeDtypeStruct((n, width), F32),
        scratch_shapes=[pltpu.VMEM((kb, 2 * kb), BF16),
                        stage_buf, stage_buf, carry_buf, row_buf, row_buf],
        compiler_params=_cparams(("arbitrary", "arbitrary")),
        name="attn",
    )(q, k, v)


def _gelu_tanh(x):
    c = math.sqrt(2.0 / math.pi)
    half_x = 0.5 * x
    return half_x + half_x * jnp.tanh(x * (c + (c * 0.044715) * (x * x)))


def _sigmoid(x):
    return 0.5 + 0.5 * jnp.tanh(0.5 * x)


def _lru_kernel(*refs, ts, n_slab):
    xl_refs, gl_refs = refs[0:n_slab], refs[n_slab:2 * n_slab]
    cw_ref, cb_ref, wr_ref, br_ref, wi_ref, bi_ref, lam_ref = refs[2 * n_slab:2 * n_slab + 7]
    o_refs = refs[2 * n_slab + 7:3 * n_slab + 7]
    (tail_ref, a7_ref, u7_ref, hp_ref, pa_ref, pu_ref, h_ref,
     wr_bd, wi_bd) = refs[3 * n_slab + 7:]
    t = pl.program_id(1)
    groups = ts // SUBLANES

    @pl.when(t == 0)
    def _():
        tail_ref[...] = jnp.zeros_like(tail_ref)
        h_ref[...] = jnp.zeros_like(h_ref)
        per = LANES // wr_ref.shape[1]
        for src, dst in ((wr_ref, wr_bd), (wi_ref, wi_bd)):
            for c in range(n_slab):
                rows = []
                for p in range(per):
                    blk = src[c * per + p]
                    rows.append(jnp.concatenate(
                        [blk if q == p else jnp.zeros_like(blk) for q in range(per)], axis=1))
                dst[c] = jnp.concatenate(rows, axis=0).astype(BF16)

    first_group = lax.broadcasted_iota(jnp.int32, (groups, LANES), 0) == 0
    for c in range(n_slab):
        lanes = slice(c * LANES, (c + 1) * LANES)
        x = [xl_refs[c][pl.ds(s, groups, stride=SUBLANES), :] for s in range(SUBLANES)]
        shifted = {}
        for s in range(SUBLANES - (CONV_W - 1), SUBLANES):
            shifted[s] = jnp.where(first_group, tail_ref[c, s:s + 1, :], pltpu.roll(x[s], 1, axis=0))
            tail_ref[c, s:s + 1, :] = x[s][groups - 1:groups, :]
        conv = []
        for s in range(SUBLANES):
            y = cb_ref[:, lanes]
            for w in range(CONV_W):
                j = s - (CONV_W - 1) + w
                y = y + (x[j] if j >= 0 else shifted[j + SUBLANES]) * cw_ref[w:w + 1, lanes]
            conv.append(y)
        xc = jnp.concatenate(conv, axis=0)

        xcb = xc.astype(BF16)
        r = _sigmoid(jnp.dot(xcb, wr_bd[c], preferred_element_type=F32) + br_ref[:, lanes])
        ig = _sigmoid(jnp.dot(xcb, wi_bd[c], preferred_element_type=F32) + bi_ref[:, lanes])
        lam = lam_ref[:, lanes]
        log_sig_lam = -(jnp.maximum(-lam, 0.0) + jnp.log1p(jnp.exp(-jnp.abs(lam))))
        log_a = r * (RG_C * log_sig_lam)
        a = jnp.exp(log_a)
        th = jnp.tanh(log_a)
        one_m_a2 = -2.0 * th / (1.0 - th)
        root = jnp.where(one_m_a2 > 0.0, one_m_a2 * lax.rsqrt(one_m_a2), 0.0)
        u = root * (ig * xc)

        a_run = u_run = None
        for s in range(SUBLANES):
            rows = slice(s * groups, (s + 1) * groups)
            if s == 0:
                a_run, u_run = a[rows], u[rows]
            else:
                u_run = a[rows] * u_run + u[rows]
                a_run = a_run * a[rows]
            pa_ref[c, rows, :] = a_run
            pu_ref[c, rows, :] = u_run
        a7_ref[c] = a_run
        u7_ref[c] = u_run

    def group(g, hs):
        nxt = []
        for c in range(n_slab):
            hp_ref[c, pl.ds(g, 1), :] = hs[c]
            nxt.append(a7_ref[c, pl.ds(g, 1), :] * hs[c] + u7_ref[c, pl.ds(g, 1), :])
        return tuple(nxt)

    hs = lax.fori_loop(0, groups, group, tuple(h_ref[c] for c in range(n_slab)), unroll=8)
    for c in range(n_slab):
        h_ref[c] = hs[c]

    for c in range(n_slab):
        h_in = hp_ref[c]
        for s in range(SUBLANES):
            rows = slice(s * groups, (s + 1) * groups)
            hseq = pu_ref[c, rows, :] + pa_ref[c, rows, :] * h_in
            gate = _gelu_tanh(gl_refs[c][pl.ds(s, groups, stride=SUBLANES), :])
            o_refs[c][pl.ds(s, groups, stride=SUBLANES), :] = hseq * gate


def _lru(xl, gl, conv_w, conv_b, w_r, br, w_i, bi, lam, batch, seq, ts):
    n, width = xl.shape
    nt = seq // ts
    n_slab = width // LANES
    blocks, block_w, _ = w_r.shape
    assert blocks * block_w == width and LANES % block_w == 0 and ts % (SUBLANES * SUBLANES) == 0
    groups = ts // SUBLANES
    gate_bd = pltpu.VMEM((n_slab, LANES, LANES), BF16)
    slab = [pl.BlockSpec((ts, LANES), functools.partial(lambda b, t, c: (b * nt + t, c), c=c))
            for c in range(n_slab)]
    const2 = lambda b, t: (0, 0)
    const3 = lambda b, t: (0, 0, 0)
    vec = pl.BlockSpec((1, width), const2)
    per_group = pltpu.VMEM((n_slab, groups, LANES), F32)
    per_step = pltpu.VMEM((n_slab, ts, LANES), F32)
    return pl.pallas_call(
        functools.partial(_lru_kernel, ts=ts, n_slab=n_slab),
        grid=(batch, nt),
        in_specs=slab + slab + [pl.BlockSpec((CONV_W, width), const2), vec,
                                pl.BlockSpec(w_r.shape, const3), vec,
                                pl.BlockSpec(w_i.shape, const3), vec, vec],
        out_specs=[pl.BlockSpec((ts, LANES), lambda b, t: (b * nt + t, 0))] * n_slab,
        out_shape=[jax.ShapeDtypeStruct((n, LANES), F32)] * n_slab,
        scratch_shapes=[pltpu.VMEM((n_slab, SUBLANES, LANES), F32),
                        per_group, per_group, per_group, per_step, per_step,
                        pltpu.VMEM((n_slab, 1, LANES), F32), gate_bd, gate_bd],
        compiler_params=_cparams(("arbitrary", "arbitrary")),
        name="lru",
    )(*([xl] * n_slab), *([gl] * n_slab), conv_w, conv_b, w_r, br, w_i, bi, lam)


def _mix_route_kernel(sb_ref, *refs, tm, n_slab):
    lru_refs = refs[:n_slab]
    (x_ref, sbg_ref, lrug_ref, wo_ref, n2g_ref, wrs_ref, rb_ref,
     h_ref, u2_ref, rr_ref, rwt_ref, tc_ref, before_ref, u2s_ref, wo_bf) = refs[n_slab:]

    @pl.when(pl.program_id(0) == 0)
    def _():
        wo_bf[...] = wo_ref[...].astype(BF16)

    _mix_route_body(sb_ref, lru_refs, x_ref, sbg_ref, lrug_ref, wo_bf, n2g_ref, wrs_ref, rb_ref,
                    h_ref, u2_ref, rr_ref, rwt_ref, tc_ref, before_ref, u2s_ref, tm)


def _mix_route_body(sb_ref, lru_refs, x_ref, sbg_ref, lrug_ref, wo_ref, n2g_ref, wrs_ref, rb_ref,
                    h_ref, u2_ref, rr_ref, rwt_ref, tc_ref, before_ref, u2s_ref, tm):
    step = pl.program_id(0)
    half = sb_ref.shape[1]
    n_sub = x_ref.shape[0] // tm

    @pl.when(step == 0)
    def _():
        r_id = lax.broadcasted_iota(jnp.int32, (tm, tm), 0)
        c_id = lax.broadcasted_iota(jnp.int32, (tm, tm), 1)
        before_ref[...] = (r_id < c_id).astype(BF16)
        tc_ref[...] = jnp.zeros_like(tc_ref)
        u2s_ref[...] = jnp.zeros_like(u2s_ref)

    def route(j):
        rows = slice(j * tm, (j + 1) * tm)
        u2 = u2s_ref[rows, :]
        u_hi = u2.astype(BF16)
        u_lo = (u2 - u_hi.astype(F32)).astype(BF16)
        nt_dims = (((1,), (1,)), ((), ()))
        n_rows = rb_ref.shape[0]
        both = lax.dot_general(wrs_ref[...], u_hi, nt_dims, preferred_element_type=F32)
        lt = (both[0:n_rows] + both[n_rows:2 * n_rows]
              + lax.dot_general(wrs_ref[0:n_rows, :], u_lo, nt_dims, preferred_element_type=F32)
              + rb_ref[...])

        sub = lax.broadcasted_iota(jnp.int32, (SUBLANES, tm), 0)

        def top1(x):
            m = jnp.max(x, axis=0, keepdims=True)
            idx = jnp.min(jnp.where(x == m, sub, SUBLANES), axis=0, keepdims=True)
            return m, idx

        grp = lt[0:SUBLANES, :]
        g_max, g_idx = top1(grp)
        g_p = 1.0 / jnp.sum(jnp.exp(grp - g_max), axis=0, keepdims=True)
        fine = lt[SUBLANES:2 * SUBLANES, :]
        for g in range(1, N_GROUPS):
            fine = jnp.where(g_idx == g, lt[(g + 1) * SUBLANES:(g + 2) * SUBLANES, :], fine)
        m1, i1 = top1(fine)
        m2, i2 = top1(jnp.where(sub == i1, -jnp.inf, fine))
        e2 = jnp.exp(m2 - m1)
        p1 = 1.0 / (1.0 + e2)
        w1 = g_p * p1
        w2 = g_p * (e2 * p1)
        x1 = g_idx * EXPERTS_PER_GROUP + i1
        x2 = g_idx * EXPERTS_PER_GROUP + i2

        eid = lax.broadcasted_iota(jnp.int32, (N_EXPERTS, tm), 0)
        oh1 = jnp.where(eid == x1, 1.0, 0.0)
        oh2 = jnp.where(eid == x2, 1.0, 0.0)
        pre = jnp.dot(jnp.concatenate([oh1, oh2], axis=0).astype(BF16), before_ref[...],
                      preferred_element_type=F32)
        pre1, pre2 = pre[0:N_EXPERTS], pre[N_EXPERTS:2 * N_EXPERTS]
        cnt1 = jnp.sum(oh1, axis=1, keepdims=True)
        cnt2 = jnp.sum(oh2, axis=1, keepdims=True)
        seg8 = jnp.floor((cnt1 + cnt2 + (SUBLANES - 1.0)) * (1.0 / SUBLANES))
        e_r = lax.broadcasted_iota(jnp.int32, (N_EXPERTS, N_EXPERTS), 0)
        e_c = lax.broadcasted_iota(jnp.int32, (N_EXPERTS, N_EXPERTS), 1)
        lower = jnp.where(e_c < e_r, 1.0, 0.0).astype(BF16)
        seg8_b = jnp.broadcast_to(seg8, (N_EXPERTS, LANES)).astype(BF16)
        seg_off = SUBLANES * jnp.dot(lower, seg8_b, preferred_element_type=F32)[:, 0:1]
        pos1 = jnp.sum(oh1 * (pre1 + seg_off), axis=0, keepdims=True)
        pos2 = jnp.sum(oh2 * (pre2 + (seg_off + cnt1)), axis=0, keepdims=True)

        lane = lax.broadcasted_iota(jnp.int32, tc_ref.shape, 1)
        seg_rows = jnp.broadcast_to(seg8 * SUBLANES, tc_ref.shape).astype(jnp.int32)
        tile = (step - 1) * n_sub + j
        tc_ref[...] = jnp.where(lane == tile, seg_rows, tc_ref[...])

        zrow = jnp.zeros((SUBLANES - 4, tm), jnp.int32)
        rr_ref[:, rows] = jnp.concatenate(
            [pos1.astype(jnp.int32), pos2.astype(jnp.int32), x1, x2, zrow], axis=0)
        wt = jnp.concatenate([w1, w2, pos1, pos2, jnp.zeros((LANES - 4, tm), F32)], axis=0)
        rwt_ref[rows, :] = wt.T

    m_sb = _rms_f32(sb_ref[...], sbg_ref[...]).astype(BF16)
    lru = jnp.concatenate([r[...] for r in lru_refs], axis=-1)
    m_lru = _rms_f32(lru, lrug_ref[...]).astype(BF16)
    h_ref[...] = (x_ref[...]
                  + jnp.dot(m_sb, wo_ref[0:half, :], preferred_element_type=F32)
                  + jnp.dot(m_lru, wo_ref[half:2 * half, :], preferred_element_type=F32))

    for j in range(n_sub):
        route(j)

    u2_next = _rms_f32(h_ref[...], n2g_ref[...])
    u2_ref[...] = u2_next.astype(BF16)
    u2s_ref[...] = u2_next


def _mix_route(sb, lru, x2, sbg, lrug, w_out, n2g, wr_stack, rbias, tm, tp):
    n, d = x2.shape
    half = sb.shape[1]
    n_tiles = n // tp
    row = lambda i: (jnp.minimum(i, n_tiles - 1), 0)
    routed = lambda i: (jnp.maximum(i - 1, 0), 0)
    const = lambda i: (0, 0)
    return pl.pallas_call(
        functools.partial(_mix_route_kernel, tm=tm, n_slab=len(lru)),
        grid=(n_tiles + 1,),
        in_specs=[pl.BlockSpec((tp, half), row)] + [pl.BlockSpec((tp, LANES), row)] * len(lru)
                 + [pl.BlockSpec((tp, d), row),
                  pl.BlockSpec((1, half), const), pl.BlockSpec((1, half), const),
                  pl.BlockSpec(w_out.shape, const, pipeline_mode=pl.Buffered(1)),
                  pl.BlockSpec((1, d), const),
                  pl.BlockSpec(wr_stack.shape, const), pl.BlockSpec(rbias.shape, const)],
        out_specs=[pl.BlockSpec((tp, d), row), pl.BlockSpec((tp, d), row),
                   pl.BlockSpec((SUBLANES, tp), lambda i: (0, jnp.maximum(i - 1, 0))),
                   pl.BlockSpec((tp, LANES), routed),
                   pl.BlockSpec((N_EXPERTS, LANES), const)],
        out_shape=[jax.ShapeDtypeStruct((n, d), F32), jax.ShapeDtypeStruct((n, d), BF16),
                   jax.ShapeDtypeStruct((SUBLANES, n), jnp.int32),
                   jax.ShapeDtypeStruct((n, LANES), F32),
                   jax.ShapeDtypeStruct((N_EXPERTS, LANES), jnp.int32)],
        scratch_shapes=[pltpu.VMEM((tm, tm), BF16), pltpu.VMEM((tp, d), F32),
                        pltpu.VMEM(w_out.shape, BF16)],
        compiler_params=_cparams(("arbitrary",)),
        name="mix_route",
    )(sb, *lru, x2, sbg, lrug, w_out, n2g, wr_stack, rbias)


def _pack_halves(x):
    half = x.shape[1] // 2
    lo = lax.shift_right_logical(lax.bitcast_convert_type(x[:, :half], jnp.int32), 16)
    hi = lax.bitcast_convert_type(x[:, half:], jnp.int32) & HIGH_HALF
    return hi | lo


def _unpack_halves(p):
    lo = lax.bitcast_convert_type(lax.shift_left(p, 16), F32)
    hi = lax.bitcast_convert_type(p & HIGH_HALF, F32)
    return lo.astype(BF16), hi.astype(BF16)


def _segment_copies(tile, c8_ref, loff_ref, goff_ref, make):
    for e in range(N_EXPERTS):
        idx = tile * N_EXPERTS + e
        rows = pl.multiple_of(c8_ref[idx], SUBLANES)

        @pl.when(rows > 0)
        def _(idx=idx, rows=rows):
            lo = pl.multiple_of(loff_ref[idx], SUBLANES)
            go = pl.multiple_of(goff_ref[idx], SUBLANES)
            make(pl.ds(lo, rows), pl.ds(go, rows)).start()


def _dispatch_kernel(c8_ref, loff_ref, goff_ref, tot_ref, used_ref,
                     rr_ref, u2_ref, xs_ref, lbuf, zbuf, sem, zsem, *, td, lrows):
    i = pl.program_id(0)
    slot = i % 2

    r_id = lax.broadcasted_iota(jnp.int32, (lrows, td), 0)
    perm = jnp.where(r_id == rr_ref[0:1, :], 1.0, jnp.where(r_id == rr_ref[1:2, :], 1.0, 0.0))
    sorted_rows = jnp.dot(perm.astype(BF16), u2_ref[...], preferred_element_type=F32)
    lbuf[slot] = _pack_halves(sorted_rows)

    @pl.when(i == 0)
    def _():
        zbuf[...] = jnp.zeros_like(zbuf)
        chunk = zbuf.shape[0]
        used = used_ref[0]
        spare = xs_ref.shape[0] - used
        n_fill = (spare + chunk - 1) // chunk

        def fill_copy(k):
            rows = pl.multiple_of(jnp.minimum(chunk, spare - k * chunk), SUBLANES)
            start = pl.multiple_of(used + k * chunk, SUBLANES)
            return pltpu.make_async_copy(zbuf.at[pl.ds(0, rows)], xs_ref.at[pl.ds(start, rows)],
                                         zsem)

        def fill_start(k, c):
            fill_copy(k).start()
            return c

        def fill_wait(k, c):
            fill_copy(k).wait()
            return c

        lax.fori_loop(0, n_fill, fill_start, 0)
        lax.fori_loop(0, n_fill, fill_wait, 0)

    def wait_tile(tile, s):
        rows = pl.multiple_of(tot_ref[tile], SUBLANES)
        pltpu.make_async_copy(lbuf.at[s, pl.ds(0, rows)], xs_ref.at[pl.ds(0, rows)], sem).wait()

    @pl.when(i > 0)
    def _():
        wait_tile(i - 1, 1 - slot)

    _segment_copies(i, c8_ref, loff_ref, goff_ref,
                    lambda loc, glob: pltpu.make_async_copy(lbuf.at[slot, loc], xs_ref.at[glob], sem))

    @pl.when(i == pl.num_programs(0) - 1)
    def _():
        wait_tile(i, slot)


def _dispatch(c8, loff, goff, tot, used, rr, u2, p_rows, td, lrows, tme):
    n, d = u2.shape
    pmap = lambda i, *_: (0, i)
    return pl.pallas_call(
        functools.partial(_dispatch_kernel, td=td, lrows=lrows),
        grid_spec=pltpu.PrefetchScalarGridSpec(
            num_scalar_prefetch=5,
            grid=(n // td,),
            in_specs=[pl.BlockSpec((SUBLANES, td), pmap),
                      pl.BlockSpec((td, d), lambda i, *_: (i, 0))],
            out_specs=pl.BlockSpec(memory_space=pl.ANY),
            scratch_shapes=[pltpu.VMEM((2, lrows, d // 2), jnp.int32),
                            pltpu.VMEM((tme, d // 2), jnp.int32),
                            pltpu.SemaphoreType.DMA(()), pltpu.SemaphoreType.DMA(())]),
        out_shape=jax.ShapeDtypeStruct((p_rows, d // 2), jnp.int32),
        compiler_params=_cparams(("arbitrary",)),
        name="dispatch",
    )(c8, loff, goff, tot, used, rr, u2)


def _experts_kernel(eoff_ref, erows_ref, xs_ref, wg_ref, wu_ref, wd_ref, ys_ref,
                    wg_bf, wu_bf, wd_bf, xbuf, ybuf, sem_in, sem_out, state, *, tme):
    e = pl.program_id(0)
    n_experts = pl.num_programs(0)
    rows = erows_ref[e]
    off = eoff_ref[e]
    n_tiles = (rows + tme - 1) // tme
    nxt = jnp.minimum(e + 1, n_experts - 1)
    prefetch = (rows > 0) & (e + 1 < n_experts) & (erows_ref[nxt] > 0)

    def tile_rows(total, k):
        return pl.multiple_of(jnp.minimum(tme, total - k * tme), SUBLANES)

    def in_copy(start, r, slot):
        start = pl.multiple_of(start, SUBLANES)
        return pltpu.make_async_copy(xs_ref.at[pl.ds(start, r)], xbuf.at[slot, pl.ds(0, r)],
                                     sem_in.at[slot])

    def out_copy(start, r, slot):
        start = pl.multiple_of(start, SUBLANES)
        return pltpu.make_async_copy(ybuf.at[slot, pl.ds(0, r)], ys_ref.at[pl.ds(start, r)],
                                     sem_out.at[slot])

    @pl.when(e == 0)
    def _():
        for s in range(4):
            state[s] = 0
        xbuf[...] = jnp.zeros_like(xbuf)

    @pl.when(rows > 0)
    def _():
        wg_bf[...] = wg_ref[0].astype(BF16)
        wu_bf[...] = wu_ref[0].astype(BF16)
        wd_bf[...] = wd_ref[0].astype(BF16)
        done = state[0]

        @pl.when(state[1] == 0)
        def _():
            in_copy(off, tile_rows(rows, 0), done % 2).start()

        def tile(k, c):
            slot = (done + k) % 2
            r = tile_rows(rows, k)

            @pl.when(k + 1 < n_tiles)
            def _():
                in_copy(off + (k + 1) * tme, tile_rows(rows, k + 1), 1 - slot).start()

            @pl.when((k + 1 == n_tiles) & prefetch)
            def _():
                in_copy(eoff_ref[nxt], tile_rows(erows_ref[nxt], 0), 1 - slot).start()

            in_copy(off, r, slot).wait()

            @pl.when(done + k >= 2)
            def _():
                out_copy(0, pl.multiple_of(state[2 + slot], SUBLANES), slot).wait()

            def mlp(start, n):
                x = jnp.concatenate(_unpack_halves(xbuf[slot, start:start + n]), axis=1)
                hg = jnp.dot(x, wg_bf[...], preferred_element_type=F32)
                hu = jnp.dot(x, wu_bf[...], preferred_element_type=F32)
                act = (hg * jax.nn.sigmoid(hg) * hu).astype(BF16)
                y = jnp.dot(act, wd_bf[...], preferred_element_type=F32)
                ybuf[slot, start:start + n] = _pack_halves(y.astype(BF16).astype(F32))

            for below, n in zip((0,) + EXPERT_TILE_SIZES, EXPERT_TILE_SIZES):
                @pl.when((r > below) & (r <= n))
                def _(n=n):
                    mlp(0, n)

            out_copy(off + k * tme, r, slot).start()
            state[2 + slot] = r
            return c

        lax.fori_loop(0, n_tiles, tile, 0)
        state[0] = done + n_tiles

    state[1] = prefetch.astype(jnp.int32)

    @pl.when(e == n_experts - 1)
    def _():
        total = eoff_ref[e] + rows
        for slot in range(2):
            @pl.when(state[0] > slot)
            def _(slot=slot):
                out_copy(0, pl.multiple_of(state[2 + slot], SUBLANES), slot).wait()
        ybuf[0] = jnp.zeros(ybuf.shape[1:], ybuf.dtype)
        spare = ys_ref.shape[0] - total
        n_fill = (spare + tme - 1) // tme

        def fill_copy(k):
            return out_copy(total + k * tme, tile_rows(spare, k), 0)

        def fill_start(k, c):
            fill_copy(k).start()
            return c

        def fill_wait(k, c):
            fill_copy(k).wait()
            return c

        lax.fori_loop(0, n_fill, fill_start, 0)
        lax.fori_loop(0, n_fill, fill_wait, 0)


def _experts(eoff, erows, xs, wg, wu, wd, tme):
    p = xs.shape[0]
    n_experts, d, de = wg.shape
    wmap = lambda e, *_: (e, 0, 0)
    return pl.pallas_call(
        functools.partial(_experts_kernel, tme=tme),
        grid_spec=pltpu.PrefetchScalarGridSpec(
            num_scalar_prefetch=2,
            grid=(n_experts,),
            in_specs=[pl.BlockSpec(memory_space=pl.ANY),
                      pl.BlockSpec((1, d, de), wmap), pl.BlockSpec((1, d, de), wmap),
                      pl.BlockSpec((1, de, d), wmap)],
            out_specs=pl.BlockSpec(memory_space=pl.ANY),
            scratch_shapes=[pltpu.VMEM((d, de), BF16), pltpu.VMEM((d, de), BF16),
                            pltpu.VMEM((de, d), BF16),
                            pltpu.VMEM((2, tme, d // 2), jnp.int32),
                            pltpu.VMEM((2, tme, d // 2), jnp.int32),
                            pltpu.SemaphoreType.DMA((2,)), pltpu.SemaphoreType.DMA((2,)),
                            pltpu.SMEM((4,), jnp.int32)]),
        out_shape=jax.ShapeDtypeStruct((p, d // 2), jnp.int32),
        compiler_params=_cparams(("arbitrary",)),
        name="experts",
    )(eoff, erows, xs, wg, wu, wd)


def _combine_kernel(c8_ref, loff_ref, goff_ref, tot_ref,
                    rwt_ref, h_ref, fg_ref, ys_ref, y_ref, ybuf, sems, *, tc, lrows):
    i = pl.program_id(0)
    slot = i % 2

    def gather_tile(tile, s):
        _segment_copies(tile, c8_ref, loff_ref, goff_ref,
                        lambda loc, glob: pltpu.make_async_copy(ys_ref.at[glob], ybuf.at[s, loc],
                                                                sems.at[s]))

    @pl.when(i == 0)
    def _():
        ybuf[...] = jnp.zeros_like(ybuf)
        gather_tile(0, 0)

    @pl.when(i + 1 < pl.num_programs(0))
    def _():
        gather_tile(i + 1, 1 - slot)

    rows = pl.multiple_of(tot_ref[i], SUBLANES)
    pltpu.make_async_copy(ys_ref.at[pl.ds(0, rows)], ybuf.at[slot, pl.ds(0, rows)],
                          sems.at[slot]).wait()

    w = rwt_ref[...]
    c_id = lax.broadcasted_iota(jnp.int32, (tc, lrows), 1)
    pos1 = w[:, 2:3].astype(jnp.int32)
    pos2 = w[:, 3:4].astype(jnp.int32)
    wmat = jnp.where(c_id == pos1, w[:, 0:1], jnp.where(c_id == pos2, w[:, 1:2], 0.0)).astype(BF16)
    half = h_ref.shape[1] // 2
    outs, sumsq = [], 0.0
    for part, cols in zip(_unpack_halves(ybuf[slot]), (slice(0, half), slice(half, 2 * half))):
        out = h_ref[:, cols] + jnp.dot(wmat, part, preferred_element_type=F32)
        sumsq = sumsq + jnp.sum(out * out, axis=-1, keepdims=True)
        outs.append(out)
    scale = lax.rsqrt(sumsq * (1.0 / (2 * half)) + EPS)
    for out, cols in zip(outs, (slice(0, half), slice(half, 2 * half))):
        y_ref[:, cols] = out * scale * fg_ref[:, cols]


def _combine(c8, loff, goff, tot, rwt, h, final_g, ys, tc, lrows):
    n, d = h.shape
    return pl.pallas_call(
        functools.partial(_combine_kernel, tc=tc, lrows=lrows),
        grid_spec=pltpu.PrefetchScalarGridSpec(
            num_scalar_prefetch=4,
            grid=(n // tc,),
            in_specs=[pl.BlockSpec((tc, LANES), lambda i, *_: (i, 0)),
                      pl.BlockSpec((tc, d), lambda i, *_: (i, 0)),
                      pl.BlockSpec((1, d), lambda i, *_: (0, 0)),
                      pl.BlockSpec(memory_space=pl.ANY)],
            out_specs=pl.BlockSpec((tc, d), lambda i, *_: (i, 0)),
            scratch_shapes=[pltpu.VMEM((2, lrows, d // 2), jnp.int32),
                            pltpu.SemaphoreType.DMA((2,))]),
        out_shape=jax.ShapeDtypeStruct((n, d), F32),
        compiler_params=_cparams(("arbitrary",)),
        name="combine",
    )(c8, loff, goff, tot, rwt, h, final_g, ys)


def _router_tables(w_group, b_group, w_fine, b_fine):
    d = w_group.shape[0]
    pad_g, pad_f = SUBLANES - N_GROUPS, ROUTER_ROWS - SUBLANES - N_EXPERTS
    w = jnp.concatenate([w_group.T, jnp.zeros((pad_g, d), F32), w_fine.T,
                         jnp.zeros((pad_f, d), F32)], axis=0)
    b = jnp.concatenate([b_group, jnp.full((pad_g,), NEG_BIG, F32), b_fine,
                         jnp.full((pad_f,), NEG_BIG, F32)])
    w_hi = w.astype(BF16)
    w_lo = (w - w_hi.astype(F32)).astype(BF16)
    return jnp.concatenate([w_hi, w_lo], axis=0), b.reshape(ROUTER_ROWS, 1)


def kernel(x, norm1_g, w_in, conv_w, conv_b, w_rgate, b_rgate, w_igate, b_igate, lam, sb_norm_g,
           lru_norm_g, w_out, norm2_g, w_group, b_group, w_fine, b_fine, w_e_gate, w_e_up,
           w_e_down, final_g):
    batch, seq, d = x.shape
    n = batch * seq
    width = w_in.shape[1] // 5
    tm = min(TOKEN_TILE, seq)
    tp = min(ROW_TILE, seq)
    tme = EXPERT_TILE_ROWS

    x2 = x.reshape(n, d)
    vec = lambda a: a.reshape(1, -1)

    q, k, v, xl, gl = _in_proj(x2, vec(norm1_g), w_in, width, tp)
    out_sb = _attention(q, k, v, batch, seq)
    out_lru = _lru(xl, gl, conv_w, vec(conv_b), w_rgate, vec(b_rgate), w_igate, vec(b_igate),
                   vec(lam), batch, seq, tp)

    wr_stack, rbias = _router_tables(w_group, b_group, w_fine, b_fine)
    h, u2, rr, rwt, tcnt = _mix_route(out_sb, out_lru, x2, vec(sb_norm_g), vec(lru_norm_g),
                                      w_out, vec(norm2_g), wr_stack, rbias, tm, tp)

    n_tiles = n // tm
    assert n_tiles <= LANES, "one lane of the per-tile count table per token tile"
    c8 = tcnt[:, :n_tiles].T
    erows = jnp.sum(c8, axis=0)
    eoff = jnp.cumsum(erows) - erows
    goff = eoff[None, :] + jnp.cumsum(c8, axis=0) - c8
    loff = jnp.cumsum(c8, axis=1) - c8
    tot = jnp.sum(c8, axis=1)
    lrows = 2 * tm + N_EXPERTS * SUBLANES
    p_rows = 2 * n + n_tiles * N_EXPERTS * (SUBLANES - 1)
    p_rows = -(-p_rows // SUBLANES) * SUBLANES
    i32 = lambda a: a.reshape(-1).astype(jnp.int32)
    c8, loff, goff, tot, eoff, erows = (i32(a) for a in (c8, loff, goff, tot, eoff, erows))

    xs = _dispatch(c8, loff, goff, tot, jnp.sum(erows, keepdims=True), rr, u2, p_rows, tm, lrows, tme)
    ys = _experts(eoff, erows, xs, w_e_gate, w_e_up, w_e_down, tme)
    y = _combine(c8, loff, goff, tot, rwt, h, vec(final_g), ys, tm, lrows)
    return y.reshape(batch, seq, d)
```

```python
import functools
import math

import jax
import jax.numpy as jnp
from jax import lax
from jax.experimental import pallas as pl
from jax.experimental.pallas import tpu as pltpu

F32 = jnp.float32
BF16 = jnp.bfloat16

EPS = 1e-6
HEAD_DIM = 64
HEADS_PER_BLOCK = 2
LANES = 128
SUBLANES = 8
CONV_W = 4
RG_C = 8.0
N_GROUPS = 4
EXPERTS_PER_GROUP = 8
N_EXPERTS = N_GROUPS * EXPERTS_PER_GROUP
ROUTER_ROWS = 48
NEG_BIG = -1e30
LOG2_E = math.log2(math.e)
ATTN_STOP = 104.0 * LOG2_E
ATTN_QUERY_ROWS = 64
ATTN_WINDOW_BLOCKS = 2
ATTN_UNROLL = 24
ATTN_STAGE_LAG = 2
HIGH_HALF = -65536
EXPERT_TILE_ROWS = 1536
EXPERT_TILE_SIZES = (256, 512, 1024, 1152, 1280, 1536)
TOKEN_TILE = 512
ROW_TILE = 1024

VMEM_LIMIT = 56 * 1024 * 1024


def _cparams(sem):
    return pltpu.CompilerParams(dimension_semantics=sem, vmem_limit_bytes=VMEM_LIMIT)


def _rms_f32(x, g):
    return x * lax.rsqrt(jnp.mean(x * x, axis=-1, keepdims=True) + EPS) * g


def _in_proj_kernel(x_ref, g_ref, w_ref, q_ref, k_ref, v_ref, xl_ref, gl_ref, w_bf, *,
                    width, q_scale):
    @pl.when(pl.program_id(0) == 0)
    def _():
        for c in range(w_ref.shape[1] // width):
            cols = slice(c * width, (c + 1) * width)
            w_bf[:, cols] = w_ref[:, cols].astype(BF16)

    u = _rms_f32(x_ref[...], g_ref[...]).astype(BF16)
    for c, o_ref in enumerate((q_ref, k_ref, v_ref, xl_ref, gl_ref)):
        p = jnp.dot(u, w_bf[:, c * width:(c + 1) * width], preferred_element_type=F32)
        if c == 0:
            p = p * q_scale
        o_ref[...] = p.astype(o_ref.dtype)


def _in_proj(x2, g, w, width, tm):
    n, d = x2.shape
    row = lambda i: (i, 0)
    out_bf = jax.ShapeDtypeStruct((n, width), BF16)
    out_f = jax.ShapeDtypeStruct((n, width), F32)
    return pl.pallas_call(
        functools.partial(_in_proj_kernel, width=width, q_scale=1.0 / math.sqrt(HEAD_DIM)),
        grid=(n // tm,),
        in_specs=[pl.BlockSpec((tm, d), row),
                  pl.BlockSpec((1, d), lambda i: (0, 0)),
                  pl.BlockSpec(w.shape, lambda i: (0, 0), pipeline_mode=pl.Buffered(1))],
        out_specs=[pl.BlockSpec((tm, width), row)] * 5,
        out_shape=[out_bf, out_bf, out_bf, out_f, out_f],
        scratch_shapes=[pltpu.VMEM(w.shape, BF16)],
        compiler_params=_cparams(("arbitrary",)),
        name="in_proj",
    )(x2, g, w)


def _attn_kernel(q_ref, k_ref, v_ref, o_ref, tri_ref, z_ref, arg_ref, ctot_ref, acc_ref, carry_ref,
                 *, tq, kb, nsub, fill):
    seq = q_ref.shape[0]
    win = nsub * kb
    lookback = win - tq
    lane = lax.broadcasted_iota(jnp.int32, (1, LANES), 1)
    rel = (lax.broadcasted_iota(jnp.int32, (tq, kb), 1)
           - lax.broadcasted_iota(jnp.int32, (tq, kb), 0))
    rel = jnp.concatenate([rel] * HEADS_PER_BLOCK, axis=0)

    k_r = lax.broadcasted_iota(jnp.int32, (kb, 2 * kb), 0)
    k_c = lax.broadcasted_iota(jnp.int32, (kb, 2 * kb), 1)
    tri_ref[...] = jnp.where(k_c >= kb, 1.0, jnp.where(k_r > k_c, 1.0, 0.0)).astype(BF16)

    def softplus2(z):
        return jnp.maximum(z, 0.0) + jnp.log2(1.0 + jnp.exp2(-jnp.abs(z)))

    def scores(qh, keys):
        z = LOG2_E * lax.dot_general(qh, keys, (((1,), (1,)), ((), ())),
                                     preferred_element_type=F32)
        nlog_nb = softplus2(z)
        return nlog_nb, z - nlog_nb

    def suffix(nlog_nb):
        r = jnp.dot(nlog_nb.astype(BF16), tri_ref[...], preferred_element_type=F32)
        return r[:, :kb], r[:, kb:]

    def stacked_queries(i):
        q = q_ref[pl.ds(i * tq, tq), :]
        return jnp.concatenate(
            [jnp.where((lane >= h * HEAD_DIM) & (lane < (h + 1) * HEAD_DIM), q, jnp.zeros_like(q))
             for h in range(HEADS_PER_BLOCK)], axis=0)

    def store(i, out):
        o_ref[pl.ds(i * tq, tq), :] = jnp.where(lane < HEAD_DIM, out[0:tq], out[tq:2 * tq])

    def window_start(i):
        if isinstance(i, int):
            return max(i * tq - lookback, 0)
        return pl.multiple_of(i * tq - lookback, tq)

    def stage_scores(i, p):
        keys = k_ref[pl.ds(window_start(i), win), :]
        z_ref[p] = LOG2_E * lax.dot_general(stacked_queries(i), keys, (((1,), (1,)), ((), ())),
                                            preferred_element_type=F32)

    def stage_exponents(p, delta):
        z = z_ref[p]
        softplus = softplus2(z)
        carry = None
        for b in reversed(range(nsub)):
            cols = slice(b * kb, (b + 1) * kb)
            masked = (b + 1) * kb > delta
            valid = (rel + b * kb) < delta
            nl = softplus[:, cols]
            if masked:
                nl = jnp.where(valid, nl, 0.0)
            excl, tot = suffix(nl)
            arg = z[:, cols] - softplus[:, cols] - excl
            if carry is not None:
                arg = arg - carry
            if masked:
                arg = jnp.where(valid, arg, NEG_BIG)
            arg_ref[p, :, cols] = arg
            carry = tot if carry is None else carry + tot
        ctot_ref[p] = carry
        return jnp.min(carry)

    def stage_output(i, p, s):
        vals = v_ref[pl.ds(window_start(i), win), :]
        acc_ref[s] = jnp.dot(jnp.exp2(arg_ref[p]).astype(BF16), vals, preferred_element_type=F32)
        carry_ref[s] = ctot_ref[p]

    def finish(i, s, cmin):
        def cond(state):
            pos, cmin = state
            return (pos > -kb) & (cmin <= ATTN_STOP)

        def older(state):
            pos, _ = state
            start = pl.multiple_of(jnp.maximum(pos, 0), tq)
            keys = k_ref[pl.ds(start, kb), :]
            vals = v_ref[pl.ds(start, kb), :]
            fresh = lax.broadcasted_iota(jnp.int32, (1, kb), 1) < pos + kb - start
            nlog_nb, log_b = scores(stacked_queries(i), keys)
            nlog_nb = jnp.where(fresh, nlog_nb, 0.0)
            excl, tot = suffix(nlog_nb)
            carry = carry_ref[s]
            a = jnp.where(fresh, jnp.exp2(log_b - excl - carry), 0.0)
            acc_ref[s] += jnp.dot(a.astype(BF16), vals, preferred_element_type=F32)
            carry_ref[s] = carry + tot
            return pos - kb, jnp.min(carry + tot)

        lax.while_loop(cond, older, (jnp.asarray(window_start(i) - kb, jnp.int32), cmin))
        store(i, acc_ref[s])

    n_blocks = seq // tq
    unroll = acc_ref.shape[0]
    lag = z_ref.shape[0] - 1
    n_slots = lag + 1
    pending = []
    for tau in range(fill):
        stage_scores(tau, tau % n_slots)
        if tau >= lag:
            j = tau - lag
            pending.append(stage_exponents(j % n_slots, j * tq - window_start(j)))
        if tau >= 2 * lag:
            j = tau - 2 * lag
            stage_output(j, j % n_slots, 0)
            finish(j, 0, pending.pop(0))

    def steady(m, pending):
        pending = list(pending)
        tau0 = fill + unroll * m
        done = []
        for u in range(unroll):
            stage_output(tau0 + u - 2 * lag, (fill - 2 * lag + u) % n_slots, u)
            done.append((tau0 + u - 2 * lag, u, pending.pop(0)))
            pending.append(stage_exponents((fill - lag + u) % n_slots, lookback))
            stage_scores(tau0 + u, (fill + u) % n_slots)
        for block, s, cmin in done:
            finish(block, s, cmin)
        return tuple(pending)

    pending = list(lax.fori_loop(0, (n_blocks - fill) // unroll, steady, tuple(pending)))
    for tau in range(n_blocks, n_blocks + 2 * lag):
        j = tau - 2 * lag
        stage_output(j, j % n_slots, 0)
        cmin = pending.pop(0)
        if tau - lag < n_blocks:
            pending.append(stage_exponents((tau - lag) % n_slots, lookback))
        finish(j, 0, cmin)


def _attention(q, k, v, batch, seq):
    n, width = q.shape
    tq, kb, nsub = ATTN_QUERY_ROWS, LANES, ATTN_WINDOW_BLOCKS
    n_blocks = seq // tq
    lag, n_slots = ATTN_STAGE_LAG, ATTN_STAGE_LAG + 1
    clipped = -(-(nsub * kb - tq) // tq)
    min_fill = clipped + 2 * lag
    assert seq % tq == 0 and seq >= nsub * kb and n_blocks >= min_fill
    options = [(f + 2 * (n_blocks - f) // u, u, f)
               for u in range(ATTN_UNROLL - ATTN_UNROLL % n_slots, 0, -n_slots)
               for f in range(min_fill, n_blocks + 1) if (n_blocks - f) % u == 0]
    _, unroll, fill = min(options)
    blk = pl.BlockSpec((seq, LANES), lambda b, hp: (b, hp))
    rows = HEADS_PER_BLOCK * tq
    stage_buf = pltpu.VMEM((n_slots, rows, nsub * kb), F32)
    carry_buf = pltpu.VMEM((n_slots, rows, kb), F32)
    row_buf = pltpu.VMEM((unroll, rows, LANES), F32)
    return pl.pallas_call(
        functools.partial(_attn_kernel, tq=tq, kb=kb, nsub=nsub, fill=fill),
        grid=(batch, width // LANES),
        in_specs=[blk, blk, blk],
        out_specs=blk,
        out_shape=jax.ShapeDtypeStruct((n, width), F32),
        scratch_shapes=[pltpu.VMEM((kb, 2 * kb), BF16),
                        stage_buf, stage_buf, carry_buf, row_buf, row_buf],
        compiler_params=_cparams(("arbitrary", "arbitrary")),
        name="attn",
    )(q, k, v)


def _gelu_tanh(x):
    c = math.sqrt(2.0 / math.pi)
    half_x = 0.5 * x
    return half_x + half_x * jnp.tanh(x * (c + (c * 0.044715) * (x * x)))


def _sigmoid(x):
    return 0.5 + 0.5 * jnp.tanh(0.5 * x)


def _lru_kernel(*refs, ts, n_slab):
    xl_refs, gl_refs = refs[0:n_slab], refs[n_slab:2 * n_slab]
    cw_ref, cb_ref, wr_ref, br_ref, wi_ref, bi_ref, lam_ref = refs[2 * n_slab:2 * n_slab + 7]
    o_refs = refs[2 * n_slab + 7:3 * n_slab + 7]
    (tail_ref, a7_ref, u7_ref, hp_ref, pa_ref, pu_ref, h_ref,
     wr_bd, wi_bd) = refs[3 * n_slab + 7:]
    t = pl.program_id(1)
    groups = ts // SUBLANES

    @pl.when(t == 0)
    def _():
        tail_ref[...] = jnp.zeros_like(tail_ref)
        h_ref[...] = jnp.zeros_like(h_ref)
        per = LANES // wr_ref.shape[1]
        for src, dst in ((wr_ref, wr_bd), (wi_ref, wi_bd)):
            for c in range(n_slab):
                rows = []
                for p in range(per):
                    blk = src[c * per + p]
                    rows.append(jnp.concatenate(
                        [blk if q == p else jnp.zeros_like(blk) for q in range(per)], axis=1))
                dst[c] = jnp.concatenate(rows, axis=0).astype(BF16)

    first_group = lax.broadcasted_iota(jnp.int32, (groups, LANES), 0) == 0
    for c in range(n_slab):
        lanes = slice(c * LANES, (c + 1) * LANES)
        x = [xl_refs[c][pl.ds(s, groups, stride=SUBLANES), :] for s in range(SUBLANES)]
        shifted = {}
        for s in range(SUBLANES - (CONV_W - 1), SUBLANES):
            shifted[s] = jnp.where(first_group, tail_ref[c, s:s + 1, :], pltpu.roll(x[s], 1, axis=0))
            tail_ref[c, s:s + 1, :] = x[s][groups - 1:groups, :]
        conv = []
        for s in range(SUBLANES):
            y = cb_ref[:, lanes]
            for w in range(CONV_W):
                j = s - (CONV_W - 1) + w
                y = y + (x[j] if j >= 0 else shifted[j + SUBLANES]) * cw_ref[w:w + 1, lanes]
            conv.append(y)
        xc = jnp.concatenate(conv, axis=0)

        xcb = xc.astype(BF16)
        r = _sigmoid(jnp.dot(xcb, wr_bd[c], preferred_element_type=F32) + br_ref[:, lanes])
        ig = _sigmoid(jnp.dot(xcb, wi_bd[c], preferred_element_type=F32) + bi_ref[:, lanes])
        lam = lam_ref[:, lanes]
        log_sig_lam = -(jnp.maximum(-lam, 0.0) + jnp.log1p(jnp.exp(-jnp.abs(lam))))
        log_a = r * (RG_C * log_sig_lam)
        a = jnp.exp(log_a)
        th = jnp.tanh(log_a)
        one_m_a2 = -2.0 * th / (1.0 - th)
        root = jnp.where(one_m_a2 > 0.0, one_m_a2 * lax.rsqrt(one_m_a2), 0.0)
        u = root * (ig * xc)

        a_run = u_run = None
        for s in range(SUBLANES):
            rows = slice(s * groups, (s + 1) * groups)
            if s == 0:
                a_run, u_run = a[rows], u[rows]
            else:
                u_run = a[rows] * u_run + u[rows]
                a_run = a_run * a[rows]
            pa_ref[c, rows, :] = a_run
            pu_ref[c, rows, :] = u_run
        a7_ref[c] = a_run
        u7_ref[c] = u_run

    def group(g, hs):
        nxt = []
        for c in range(n_slab):
            hp_ref[c, pl.ds(g, 1), :] = hs[c]
            nxt.append(a7_ref[c, pl.ds(g, 1), :] * hs[c] + u7_ref[c, pl.ds(g, 1), :])
        return tuple(nxt)

    hs = lax.fori_loop(0, groups, group, tuple(h_ref[c] for c in range(n_slab)), unroll=8)
    for c in range(n_slab):
        h_ref[c] = hs[c]

    for c in range(n_slab):
        h_in = hp_ref[c]
        for s in range(SUBLANES):
            rows = slice(s * groups, (s + 1) * groups)
            hseq = pu_ref[c, rows, :] + pa_ref[c, rows, :] * h_in
            gate = _gelu_tanh(gl_refs[c][pl.ds(s, groups, stride=SUBLANES), :])
            o_refs[c][pl.ds(s, groups, stride=SUBLANES), :] = hseq * gate


def _lru(xl, gl, conv_w, conv_b, w_r, br, w_i, bi, lam, batch, seq, ts):
    n, width = xl.shape
    nt = seq // ts
    n_slab = width // LANES
    blocks, block_w, _ = w_r.shape
    assert blocks * block_w == width and LANES % block_w == 0 and ts % (SUBLANES * SUBLANES) == 0
    groups = ts // SUBLANES
    gate_bd = pltpu.VMEM((n_slab, LANES, LANES), BF16)
    slab = [pl.BlockSpec((ts, LANES), functools.partial(lambda b, t, c: (b * nt + t, c), c=c))
            for c in range(n_slab)]
    const2 = lambda b, t: (0, 0)
    const3 = lambda b, t: (0, 0, 0)
    vec = pl.BlockSpec((1, width), const2)
    per_group = pltpu.VMEM((n_slab, groups, LANES), F32)
    per_step = pltpu.VMEM((n_slab, ts, LANES), F32)
    return pl.pallas_call(
        functools.partial(_lru_kernel, ts=ts, n_slab=n_slab),
        grid=(batch, nt),
        in_specs=slab + slab + [pl.BlockSpec((CONV_W, width), const2), vec,
                                pl.BlockSpec(w_r.shape, const3), vec,
                                pl.BlockSpec(w_i.shape, const3), vec, vec],
        out_specs=[pl.BlockSpec((ts, LANES), lambda b, t: (b * nt + t, 0))] * n_slab,
        out_shape=[jax.ShapeDtypeStruct((n, LANES), F32)] * n_slab,
        scratch_shapes=[pltpu.VMEM((n_slab, SUBLANES, LANES), F32),
                        per_group, per_group, per_group, per_step, per_step,
                        pltpu.VMEM((n_slab, 1, LANES), F32), gate_bd, gate_bd],
        compiler_params=_cparams(("arbitrary", "arbitrary")),
        name="lru",
    )(*([xl] * n_slab), *([gl] * n_slab), conv_w, conv_b, w_r, br, w_i, bi, lam)


def _mix_route_kernel(sb_ref, *refs, tm, n_slab):
    lru_refs = refs[:n_slab]
    (x_ref, sbg_ref, lrug_ref, wo_ref, n2g_ref, wrs_ref, rb_ref,
     h_ref, u2_ref, rr_ref, rwt_ref, tc_ref, before_ref, u2s_ref, wo_bf) = refs[n_slab:]

    @pl.when(pl.program_id(0) == 0)
    def _():
        wo_bf[...] = wo_ref[...].astype(BF16)

    _mix_route_body(sb_ref, lru_refs, x_ref, sbg_ref, lrug_ref, wo_bf, n2g_ref, wrs_ref, rb_ref,
                    h_ref, u2_ref, rr_ref, rwt_ref, tc_ref, before_ref, u2s_ref, tm)


def _mix_route_body(sb_ref, lru_refs, x_ref, sbg_ref, lrug_ref, wo_ref, n2g_ref, wrs_ref, rb_ref,
                    h_ref, u2_ref, rr_ref, rwt_ref, tc_ref, before_ref, u2s_ref, tm):
    step = pl.program_id(0)
    half = sb_ref.shape[1]
    n_sub = x_ref.shape[0] // tm

    @pl.when(step == 0)
    def _():
        r_id = lax.broadcasted_iota(jnp.int32, (tm, tm), 0)
        c_id = lax.broadcasted_iota(jnp.int32, (tm, tm), 1)
        before_ref[...] = (r_id < c_id).astype(BF16)
        tc_ref[...] = jnp.zeros_like(tc_ref)
        u2s_ref[...] = jnp.zeros_like(u2s_ref)

    def route(j):
        rows = slice(j * tm, (j + 1) * tm)
        u2 = u2s_ref[rows, :]
        u_hi = u2.astype(BF16)
        u_lo = (u2 - u_hi.astype(F32)).astype(BF16)
        nt_dims = (((1,), (1,)), ((), ()))
        n_rows = rb_ref.shape[0]
        both = lax.dot_general(wrs_ref[...], u_hi, nt_dims, preferred_element_type=F32)
        lt = (both[0:n_rows] + both[n_rows:2 * n_rows]
              + lax.dot_general(wrs_ref[0:n_rows, :], u_lo, nt_dims, preferred_element_type=F32)
              + rb_ref[...])

        sub = lax.broadcasted_iota(jnp.int32, (SUBLANES, tm), 0)

        def top1(x):
            m = jnp.max(x, axis=0, keepdims=True)
            idx = jnp.min(jnp.where(x == m, sub, SUBLANES), axis=0, keepdims=True)
            return m, idx

        grp = lt[0:SUBLANES, :]
        g_max, g_idx = top1(grp)
        g_p = 1.0 / jnp.sum(jnp.exp(grp - g_max), axis=0, keepdims=True)
        fine = lt[SUBLANES:2 * SUBLANES, :]
        for g in range(1, N_GROUPS):
            fine = jnp.where(g_idx == g, lt[(g + 1) * SUBLANES:(g + 2) * SUBLANES, :], fine)
        m1, i1 = top1(fine)
        m2, i2 = top1(jnp.where(sub == i1, -jnp.inf, fine))
        e2 = jnp.exp(m2 - m1)
        p1 = 1.0 / (1.0 + e2)
        w1 = g_p * p1
        w2 = g_p * (e2 * p1)
        x1 = g_idx * EXPERTS_PER_GROUP + i1
        x2 = g_idx * EXPERTS_PER_GROUP + i2

        eid = lax.broadcasted_iota(jnp.int32, (N_EXPERTS, tm), 0)
        oh1 = jnp.where(eid == x1, 1.0, 0.0)
        oh2 = jnp.where(eid == x2, 1.0, 0.0)
        pre = jnp.dot(jnp.concatenate([oh1, oh2], axis=0).astype(BF16), before_ref[...],
                      preferred_element_type=F32)
        pre1, pre2 = pre[0:N_EXPERTS], pre[N_EXPERTS:2 * N_EXPERTS]
        cnt1 = jnp.sum(oh1, axis=1, keepdims=True)
        cnt2 = jnp.sum(oh2, axis=1, keepdims=True)
        seg8 = jnp.floor((cnt1 + cnt2 + (SUBLANES - 1.0)) * (1.0 / SUBLANES))
        e_r = lax.broadcasted_iota(jnp.int32, (N_EXPERTS, N_EXPERTS), 0)
        e_c = lax.broadcasted_iota(jnp.int32, (N_EXPERTS, N_EXPERTS), 1)
        lower = jnp.where(e_c < e_r, 1.0, 0.0).astype(BF16)
        seg8_b = jnp.broadcast_to(seg8, (N_EXPERTS, LANES)).astype(BF16)
        seg_off = SUBLANES * jnp.dot(lower, seg8_b, preferred_element_type=F32)[:, 0:1]
        pos1 = jnp.sum(oh1 * (pre1 + seg_off), axis=0, keepdims=True)
        pos2 = jnp.sum(oh2 * (pre2 + (seg_off + cnt1)), axis=0, keepdims=True)

        lane = lax.broadcasted_iota(jnp.int32, tc_ref.shape, 1)
        seg_rows = jnp.broadcast_to(seg8 * SUBLANES, tc_ref.shape).astype(jnp.int32)
        tile = (step - 1) * n_sub + j
        tc_ref[...] = jnp.where(lane == tile, seg_rows, tc_ref[...])

        zrow = jnp.zeros((SUBLANES - 4, tm), jnp.int32)
        rr_ref[:, rows] = jnp.concatenate(
            [pos1.astype(jnp.int32), pos2.astype(jnp.int32), x1, x2, zrow], axis=0)
        wt = jnp.concatenate([w1, w2, pos1, pos2, jnp.zeros((LANES - 4, tm), F32)], axis=0)
        rwt_ref[rows, :] = wt.T

    m_sb = _rms_f32(sb_ref[...], sbg_ref[...]).astype(BF16)
    lru = jnp.concatenate([r[...] for r in lru_refs], axis=-1)
    m_lru = _rms_f32(lru, lrug_ref[...]).astype(BF16)
    h_ref[...] = (x_ref[...]
                  + jnp.dot(m_sb, wo_ref[0:half, :], preferred_element_type=F32)
                  + jnp.dot(m_lru, wo_ref[half:2 * half, :], preferred_element_type=F32))

    for j in range(n_sub):
        route(j)

    u2_next = _rms_f32(h_ref[...], n2g_ref[...])
    u2_ref[...] = u2_next.astype(BF16)
    u2s_ref[...] = u2_next


def _mix_route(sb, lru, x2, sbg, lrug, w_out, n2g, wr_stack, rbias, tm, tp):
    n, d = x2.shape
    half = sb.shape[1]
    n_tiles = n // tp
    row = lambda i: (jnp.minimum(i, n_tiles - 1), 0)
    routed = lambda i: (jnp.maximum(i - 1, 0), 0)
    const = lambda i: (0, 0)
    return pl.pallas_call(
        functools.partial(_mix_route_kernel, tm=tm, n_slab=len(lru)),
        grid=(n_tiles + 1,),
        in_specs=[pl.BlockSpec((tp, half), row)] + [pl.BlockSpec((tp, LANES), row)] * len(lru)
                 + [pl.BlockSpec((tp, d), row),
                  pl.BlockSpec((1, half), const), pl.BlockSpec((1, half), const),
                  pl.BlockSpec(w_out.shape, const, pipeline_mode=pl.Buffered(1)),
                  pl.BlockSpec((1, d), const),
                  pl.BlockSpec(wr_stack.shape, const), pl.BlockSpec(rbias.shape, const)],
        out_specs=[pl.BlockSpec((tp, d), row), pl.BlockSpec((tp, d), row),
                   pl.BlockSpec((SUBLANES, tp), lambda i: (0, jnp.maximum(i - 1, 0))),
                   pl.BlockSpec((tp, LANES), routed),
                   pl.BlockSpec((N_EXPERTS, LANES), const)],
        out_shape=[jax.ShapeDtypeStruct((n, d), F32), jax.ShapeDtypeStruct((n, d), BF16),
                   jax.ShapeDtypeStruct((SUBLANES, n), jnp.int32),
                   jax.ShapeDtypeStruct((n, LANES), F32),
                   jax.ShapeDtypeStruct((N_EXPERTS, LANES), jnp.int32)],
        scratch_shapes=[pltpu.VMEM((tm, tm), BF16), pltpu.VMEM((tp, d), F32),
                        pltpu.VMEM(w_out.shape, BF16)],
        compiler_params=_cparams(("arbitrary",)),
        name="mix_route",
    )(sb, *lru, x2, sbg, lrug, w_out, n2g, wr_stack, rbias)


def _pack_halves(x):
    half = x.shape[1] // 2
    lo = lax.shift_right_logical(lax.bitcast_convert_type(x[:, :half], jnp.int32), 16)
    hi = lax.bitcast_convert_type(x[:, half:], jnp.int32) & HIGH_HALF
    return hi | lo


def _unpack_halves(p):
    lo = lax.bitcast_convert_type(lax.shift_left(p, 16), F32)
    hi = lax.bitcast_convert_type(p & HIGH_HALF, F32)
    return lo.astype(BF16), hi.astype(BF16)


def _segment_copies(tile, c8_ref, loff_ref, goff_ref, make):
    for e in range(N_EXPERTS):
        idx = tile * N_EXPERTS + e
        rows = pl.multiple_of(c8_ref[idx], SUBLANES)

        @pl.when(rows > 0)
        def _(idx=idx, rows=rows):
            lo = pl.multiple_of(loff_ref[idx], SUBLANES)
            go = pl.multiple_of(goff_ref[idx], SUBLANES)
            make(pl.ds(lo, rows), pl.ds(go, rows)).start()


def _dispatch_kernel(c8_ref, loff_ref, goff_ref, tot_ref, used_ref,
                     rr_ref, u2_ref, xs_ref, lbuf, zbuf, sem, zsem, *, td, lrows):
    i = pl.program_id(0)
    slot = i % 2

    r_id = lax.broadcasted_iota(jnp.int32, (lrows, td), 0)
    perm = jnp.where(r_id == rr_ref[0:1, :], 1.0, jnp.where(r_id == rr_ref[1:2, :], 1.0, 0.0))
    sorted_rows = jnp.dot(perm.astype(BF16), u2_ref[...], preferred_element_type=F32)
    lbuf[slot] = _pack_halves(sorted_rows)

    @pl.when(i == 0)
    def _():
        zbuf[...] = jnp.zeros_like(zbuf)
        chunk = zbuf.shape[0]
        used = used_ref[0]
        spare = xs_ref.shape[0] - used
        n_fill = (spare + chunk - 1) // chunk

        def fill_copy(k):
            rows = pl.multiple_of(jnp.minimum(chunk, spare - k * chunk), SUBLANES)
            start = pl.multiple_of(used + k * chunk, SUBLANES)
            return pltpu.make_async_copy(zbuf.at[pl.ds(0, rows)], xs_ref.at[pl.ds(start, rows)],
                                         zsem)

        def fill_start(k, c):
            fill_copy(k).start()
            return c

        def fill_wait(k, c):
            fill_copy(k).wait()
            return c

        lax.fori_loop(0, n_fill, fill_start, 0)
        lax.fori_loop(0, n_fill, fill_wait, 0)

    def wait_tile(tile, s):
        rows = pl.multiple_of(tot_ref[tile], SUBLANES)
        pltpu.make_async_copy(lbuf.at[s, pl.ds(0, rows)], xs_ref.at[pl.ds(0, rows)], sem).wait()

    @pl.when(i > 0)
    def _():
        wait_tile(i - 1, 1 - slot)

    _segment_copies(i, c8_ref, loff_ref, goff_ref,
                    lambda loc, glob: pltpu.make_async_copy(lbuf.at[slot, loc], xs_ref.at[glob], sem))

    @pl.when(i == pl.num_programs(0) - 1)
    def _():
        wait_tile(i, slot)


def _dispatch(c8, loff, goff, tot, used, rr, u2, p_rows, td, lrows, tme):
    n, d = u2.shape
    pmap = lambda i, *_: (0, i)
    return pl.pallas_call(
        functools.partial(_dispatch_kernel, td=td, lrows=lrows),
        grid_spec=pltpu.PrefetchScalarGridSpec(
            num_scalar_prefetch=5,
            grid=(n // td,),
            in_specs=[pl.BlockSpec((SUBLANES, td), pmap),
                      pl.BlockSpec((td, d), lambda i, *_: (i, 0))],
            out_specs=pl.BlockSpec(memory_space=pl.ANY),
            scratch_shapes=[pltpu.VMEM((2, lrows, d // 2), jnp.int32),
                            pltpu.VMEM((tme, d // 2), jnp.int32),
                            pltpu.SemaphoreType.DMA(()), pltpu.SemaphoreType.DMA(())]),
        out_shape=jax.ShapeDtypeStruct((p_rows, d // 2), jnp.int32),
        compiler_params=_cparams(("arbitrary",)),
        name="dispatch",
    )(c8, loff, goff, tot, used, rr, u2)


def _experts_kernel(eoff_ref, erows_ref, xs_ref, wg_ref, wu_ref, wd_ref, ys_ref,
                    wg_bf, wu_bf, wd_bf, xbuf, ybuf, sem_in, sem_out, state, *, tme):
    e = pl.program_id(0)
    n_experts = pl.num_programs(0)
    rows = erows_ref[e]
    off = eoff_ref[e]
    n_tiles = (rows + tme - 1) // tme
    nxt = jnp.minimum(e + 1, n_experts - 1)
    prefetch = (rows > 0) & (e + 1 < n_experts) & (erows_ref[nxt] > 0)

    def tile_rows(total, k):
        return pl.multiple_of(jnp.minimum(tme, total - k * tme), SUBLANES)

    def in_copy(start, r, slot):
        start = pl.multiple_of(start, SUBLANES)
        return pltpu.make_async_copy(xs_ref.at[pl.ds(start, r)], xbuf.at[slot, pl.ds(0, r)],
                                     sem_in.at[slot])

    def out_copy(start, r, slot):
        start = pl.multiple_of(start, SUBLANES)
        return pltpu.make_async_copy(ybuf.at[slot, pl.ds(0, r)], ys_ref.at[pl.ds(start, r)],
                                     sem_out.at[slot])

    @pl.when(e == 0)
    def _():
        for s in range(4):
            state[s] = 0
        xbuf[...] = jnp.zeros_like(xbuf)

    @pl.when(rows > 0)
    def _():
        wg_bf[...] = wg_ref[0].astype(BF16)
        wu_bf[...] = wu_ref[0].astype(BF16)
        wd_bf[...] = wd_ref[0].astype(BF16)
        done = state[0]

        @pl.when(state[1] == 0)
        def _():
            in_copy(off, tile_rows(rows, 0), done % 2).start()

        def tile(k, c):
            slot = (done + k) % 2
            r = tile_rows(rows, k)

            @pl.when(k + 1 < n_tiles)
            def _():
                in_copy(off + (k + 1) * tme, tile_rows(rows, k + 1), 1 - slot).start()

            @pl.when((k + 1 == n_tiles) & prefetch)
            def _():
                in_copy(eoff_ref[nxt], tile_rows(erows_ref[nxt], 0), 1 - slot).start()

            in_copy(off, r, slot).wait()

            @pl.when(done + k >= 2)
            def _():
                out_copy(0, pl.multiple_of(state[2 + slot], SUBLANES), slot).wait()

            def mlp(start, n):
                x = jnp.concatenate(_unpack_halves(xbuf[slot, start:start + n]), axis=1)
                hg = jnp.dot(x, wg_bf[...], preferred_element_type=F32)
                hu = jnp.dot(x, wu_bf[...], preferred_element_type=F32)
                act = (hg * jax.nn.sigmoid(hg) * hu).astype(BF16)
                y = jnp.dot(act, wd_bf[...], preferred_element_type=F32)
                ybuf[slot, start:start + n] = _pack_halves(y.astype(BF16).astype(F32))

            for below, n in zip((0,) + EXPERT_TILE_SIZES, EXPERT_TILE_SIZES):
                @pl.when((r > below) & (r <= n))
                def _(n=n):
                    mlp(0, n)

            out_copy(off + k * tme, r, slot).start()
            state[2 + slot] = r
            return c

        lax.fori_loop(0, n_tiles, tile, 0)
        state[0] = done + n_tiles

    state[1] = prefetch.astype(jnp.int32)

    @pl.when(e == n_experts - 1)
    def _():
        total = eoff_ref[e] + rows
        for slot in range(2):
            @pl.when(state[0] > slot)
            def _(slot=slot):
                out_copy(0, pl.multiple_of(state[2 + slot], SUBLANES), slot).wait()
        ybuf[0] = jnp.zeros(ybuf.shape[1:], ybuf.dtype)
        spare = ys_ref.shape[0] - total
        n_fill = (spare + tme - 1) // tme

        def fill_copy(k):
            return out_copy(total + k * tme, tile_rows(spare, k), 0)

        def fill_start(k, c):
            fill_copy(k).start()
            return c

        def fill_wait(k, c):
            fill_copy(k).wait()
            return c

        lax.fori_loop(0, n_fill, fill_start, 0)
        lax.fori_loop(0, n_fill, fill_wait, 0)


def _experts(eoff, erows, xs, wg, wu, wd, tme):
    p = xs.shape[0]
    n_experts, d, de = wg.shape
    wmap = lambda e, *_: (e, 0, 0)
    return pl.pallas_call(
        functools.partial(_experts_kernel, tme=tme),
        grid_spec=pltpu.PrefetchScalarGridSpec(
            num_scalar_prefetch=2,
            grid=(n_experts,),
            in_specs=[pl.BlockSpec(memory_space=pl.ANY),
                      pl.BlockSpec((1, d, de), wmap), pl.BlockSpec((1, d, de), wmap),
                      pl.BlockSpec((1, de, d), wmap)],
            out_specs=pl.BlockSpec(memory_space=pl.ANY),
            scratch_shapes=[pltpu.VMEM((d, de), BF16), pltpu.VMEM((d, de), BF16),
                            pltpu.VMEM((de, d), BF16),
                            pltpu.VMEM((2, tme, d // 2), jnp.int32),
                            pltpu.VMEM((2, tme, d // 2), jnp.int32),
                            pltpu.SemaphoreType.DMA((2,)), pltpu.SemaphoreType.DMA((2,)),
                            pltpu.SMEM((4,), jnp.int32)]),
        out_shape=jax.ShapeDtypeStruct((p, d // 2), jnp.int32),
        compiler_params=_cparams(("arbitrary",)),
        name="experts",
    )(eoff, erows, xs, wg, wu, wd)


def _combine_kernel(c8_ref, loff_ref, goff_ref, tot_ref,
                    rwt_ref, h_ref, fg_ref, ys_ref, y_ref, ybuf, sems, *, tc, lrows):
    i = pl.program_id(0)
    slot = i % 2

    def gather_tile(tile, s):
        _segment_copies(tile, c8_ref, loff_ref, goff_ref,
                        lambda loc, glob: pltpu.make_async_copy(ys_ref.at[glob], ybuf.at[s, loc],
                                                                sems.at[s]))

    @pl.when(i == 0)
    def _():
        ybuf[...] = jnp.zeros_like(ybuf)
        gather_tile(0, 0)

    @pl.when(i + 1 < pl.num_programs(0))
    def _():
        gather_tile(i + 1, 1 - slot)

    rows = pl.multiple_of(tot_ref[i], SUBLANES)
    pltpu.make_async_copy(ys_ref.at[pl.ds(0, rows)], ybuf.at[slot, pl.ds(0, rows)],
                          sems.at[slot]).wait()

    w = rwt_ref[...]
    c_id = lax.broadcasted_iota(jnp.int32, (tc, lrows), 1)
    pos1 = w[:, 2:3].astype(jnp.int32)
    pos2 = w[:, 3:4].astype(jnp.int32)
    wmat = jnp.where(c_id == pos1, w[:, 0:1], jnp.where(c_id == pos2, w[:, 1:2], 0.0)).astype(BF16)
    half = h_ref.shape[1] // 2
    outs, sumsq = [], 0.0
    for part, cols in zip(_unpack_halves(ybuf[slot]), (slice(0, half), slice(half, 2 * half))):
        out = h_ref[:, cols] + jnp.dot(wmat, part, preferred_element_type=F32)
        sumsq = sumsq + jnp.sum(out * out, axis=-1, keepdims=True)
        outs.append(out)
    scale = lax.rsqrt(sumsq * (1.0 / (2 * half)) + EPS)
    for out, cols in zip(outs, (slice(0, half), slice(half, 2 * half))):
        y_ref[:, cols] = out * scale * fg_ref[:, cols]


def _combine(c8, loff, goff, tot, rwt, h, final_g, ys, tc, lrows):
    n, d = h.shape
    return pl.pallas_call(
        functools.partial(_combine_kernel, tc=tc, lrows=lrows),
        grid_spec=pltpu.PrefetchScalarGridSpec(
            num_scalar_prefetch=4,
            grid=(n // tc,),
            in_specs=[pl.BlockSpec((tc, LANES), lambda i, *_: (i, 0)),
                      pl.BlockSpec((tc, d), lambda i, *_: (i, 0)),
                      pl.BlockSpec((1, d), lambda i, *_: (0, 0)),
                      pl.BlockSpec(memory_space=pl.ANY)],
            out_specs=pl.BlockSpec((tc, d), lambda i, *_: (i, 0)),
            scratch_shapes=[pltpu.VMEM((2, lrows, d // 2), jnp.int32),
                            pltpu.SemaphoreType.DMA((2,))]),
        out_shape=jax.ShapeDtypeStruct((n, d), F32),
        compiler_params=_cparams(("arbitrary",)),
        name="combine",
    )(c8, loff, goff, tot, rwt, h, final_g, ys)


def _router_tables(w_group, b_group, w_fine, b_fine):
    d = w_group.shape[0]
    pad_g, pad_f = SUBLANES - N_GROUPS, ROUTER_ROWS - SUBLANES - N_EXPERTS
    w = jnp.concatenate([w_group.T, jnp.zeros((pad_g, d), F32), w_fine.T,
                         jnp.zeros((pad_f, d), F32)], axis=0)
    b = jnp.concatenate([b_group, jnp.full((pad_g,), NEG_BIG, F32), b_fine,
                         jnp.full((pad_f,), NEG_BIG, F32)])
    w_hi = w.astype(BF16)
    w_lo = (w - w_hi.astype(F32)).astype(BF16)
    return jnp.concatenate([w_hi, w_lo], axis=0), b.reshape(ROUTER_ROWS, 1)


def kernel(x, norm1_g, w_in, conv_w, conv_b, w_rgate, b_rgate, w_igate, b_igate, lam, sb_norm_g,
           lru_norm_g, w_out, norm2_g, w_group, b_group, w_fine, b_fine, w_e_gate, w_e_up,
           w_e_down, final_g):
    batch, seq, d = x.shape
    n = batch * seq
    width = w_in.shape[1] // 5
    tm = min(TOKEN_TILE, seq)
    tp = min(ROW_TILE, seq)
    tme = EXPERT_TILE_ROWS

    x2 = x.reshape(n, d)
    vec = lambda a: a.reshape(1, -1)

    q, k, v, xl, gl = _in_proj(x2, vec(norm1_g), w_in, width, tp)
    out_sb = _attention(q, k, v, batch, seq)
    out_lru = _lru(xl, gl, conv_w, vec(conv_b), w_rgate, vec(b_rgate), w_igate, vec(b_igate),
                   vec(lam), batch, seq, tp)

    wr_stack, rbias = _router_tables(w_group, b_group, w_fine, b_fine)
    h, u2, rr, rwt, tcnt = _mix_route(out_sb, out_lru, x2, vec(sb_norm_g), vec(lru_norm_g),
                                      w_out, vec(norm2_g), wr_stack, rbias, tm, tp)

    n_tiles = n // tm
    assert n_tiles <= LANES, "one lane of the per-tile count table per token tile"
    c8 = tcnt[:, :n_tiles].T
    erows = jnp.sum(c8, axis=0)
    eoff = jnp.cumsum(erows) - erows
    goff = eoff[None, :] + jnp.cumsum(c8, axis=0) - c8
    loff = jnp.cumsum(c8, axis=1) - c8
    tot = jnp.sum(c8, axis=1)
    lrows = 2 * tm + N_EXPERTS * SUBLANES
    p_rows = 2 * n + n_tiles * N_EXPERTS * (SUBLANES - 1)
    p_rows = -(-p_rows // SUBLANES) * SUBLANES
    i32 = lambda a: a.reshape(-1).astype(jnp.int32)
    c8, loff, goff, tot, eoff, erows = (i32(a) for a in (c8, loff, goff, tot, eoff, erows))

    xs = _dispatch(c8, loff, goff, tot, jnp.sum(erows, keepdims=True), rr, u2, p_rows, tm, lrows, tme)
    ys = _experts(eoff, erows, xs, w_e_gate, w_e_up, w_e_down, tme)
    y = _combine(c8, loff, goff, tot, rwt, h, vec(final_g), ys, tm, lrows)
    return y.reshape(batch, seq, d)
```

```python
import functools
import math

import jax
import jax.numpy as jnp
from jax import lax
from jax.experimental import pallas as pl
from jax.experimental.pallas import tpu as pltpu

F32 = jnp.float32
BF16 = jnp.bfloat16

EPS = 1e-6
HEAD_DIM = 64
HEADS_PER_BLOCK = 2
LANES = 128
SUBLANES = 8
CONV_W = 4
RG_C = 8.0
N_GROUPS = 4
EXPERTS_PER_GROUP = 8
N_EXPERTS = N_GROUPS * EXPERTS_PER_GROUP
ROUTER_ROWS = 48
NEG_BIG = -1e30
LOG2_E = math.log2(math.e)
ATTN_STOP = 104.0 * LOG2_E
ATTN_QUERY_ROWS = 64
ATTN_WINDOW_BLOCKS = 2
ATTN_UNROLL = 24
ATTN_STAGE_LAG = 2
HIGH_HALF = -65536
EXPERT_TILE_ROWS = 1536
EXPERT_TILE_SIZES = (256, 512, 1024, 1152, 1280, 1536)
TOKEN_TILE = 512
ROW_TILE = 1024

VMEM_LIMIT = 56 * 1024 * 1024


def _cparams(sem):
    return pltpu.CompilerParams(dimension_semantics=sem, vmem_limit_bytes=VMEM_LIMIT)


def _rms_f32(x, g):
    return x * lax.rsqrt(jnp.mean(x * x, axis=-1, keepdims=True) + EPS) * g


def _in_proj_kernel(x_ref, g_ref, w_ref, q_ref, k_ref, v_ref, xl_ref, gl_ref, w_bf, *,
                    width, q_scale):
    @pl.when(pl.program_id(0) == 0)
    def _():
        for c in range(w_ref.shape[1] // width):
            cols = slice(c * width, (c + 1) * width)
            w_bf[:, cols] = w_ref[:, cols].astype(BF16)

    u = _rms_f32(x_ref[...], g_ref[...]).astype(BF16)
    for c, o_ref in enumerate((q_ref, k_ref, v_ref, xl_ref, gl_ref)):
        p = jnp.dot(u, w_bf[:, c * width:(c + 1) * width], preferred_element_type=F32)
        if c == 0:
            p = p * q_scale
        o_ref[...] = p.astype(o_ref.dtype)


def _in_proj(x2, g, w, width, tm):
    n, d = x2.shape
    row = lambda i: (i, 0)
    out_bf = jax.ShapeDtypeStruct((n, width), BF16)
    out_f = jax.ShapeDtypeStruct((n, width), F32)
    return pl.pallas_call(
        functools.partial(_in_proj_kernel, width=width, q_scale=1.0 / math.sqrt(HEAD_DIM)),
        grid=(n // tm,),
        in_specs=[pl.BlockSpec((tm, d), row),
                  pl.BlockSpec((1, d), lambda i: (0, 0)),
                  pl.BlockSpec(w.shape, lambda i: (0, 0), pipeline_mode=pl.Buffered(1))],
        out_specs=[pl.BlockSpec((tm, width), row)] * 5,
        out_shape=[out_bf, out_bf, out_bf, out_f, out_f],
        scratch_shapes=[pltpu.VMEM(w.shape, BF16)],
        compiler_params=_cparams(("arbitrary",)),
        name="in_proj",
    )(x2, g, w)


def _attn_kernel(q_ref, k_ref, v_ref, o_ref, tri_ref, z_ref, arg_ref, ctot_ref, acc_ref, carry_ref,
                 *, tq, kb, nsub, fill):
    seq = q_ref.shape[0]
    win = nsub * kb
    lookback = win - tq
    lane = lax.broadcasted_iota(jnp.int32, (1, LANES), 1)
    rel = (lax.broadcasted_iota(jnp.int32, (tq, kb), 1)
           - lax.broadcasted_iota(jnp.int32, (tq, kb), 0))
    rel = jnp.concatenate([rel] * HEADS_PER_BLOCK, axis=0)

    k_r = lax.broadcasted_iota(jnp.int32, (kb, 2 * kb), 0)
    k_c = lax.broadcasted_iota(jnp.int32, (kb, 2 * kb), 1)
    tri_ref[...] = jnp.where(k_c >= kb, 1.0, jnp.where(k_r > k_c, 1.0, 0.0)).astype(BF16)

    def softplus2(z):
        return jnp.maximum(z, 0.0) + jnp.log2(1.0 + jnp.exp2(-jnp.abs(z)))

    def scores(qh, keys):
        z = LOG2_E * lax.dot_general(qh, keys, (((1,), (1,)), ((), ())),
                                     preferred_element_type=F32)
        nlog_nb = softplus2(z)
        return nlog_nb, z - nlog_nb

    def suffix(nlog_nb):
        r = jnp.dot(nlog_nb.astype(BF16), tri_ref[...], preferred_element_type=F32)
        return r[:, :kb], r[:, kb:]

    def stacked_queries(i):
        q = q_ref[pl.ds(i * tq, tq), :]
        return jnp.concatenate(
            [jnp.where((lane >= h * HEAD_DIM) & (lane < (h + 1) * HEAD_DIM), q, jnp.zeros_like(q))
             for h in range(HEADS_PER_BLOCK)], axis=0)

    def store(i, out):
        o_ref[pl.ds(i * tq, tq), :] = jnp.where(lane < HEAD_DIM, out[0:tq], out[tq:2 * tq])

    def window_start(i):
        if isinstance(i, int):
            return max(i * tq - lookback, 0)
        return pl.multiple_of(i * tq - lookback, tq)

    def stage_scores(i, p):
        keys = k_ref[pl.ds(window_start(i), win), :]
        z_ref[p] = LOG2_E * lax.dot_general(stacked_queries(i), keys, (((1,), (1,)), ((), ())),
                                            preferred_element_type=F32)

    def stage_exponents(p, delta):
        z = z_ref[p]
        softplus = softplus2(z)
        carry = None
        for b in reversed(range(nsub)):
            cols = slice(b * kb, (b + 1) * kb)
            masked = (b + 1) * kb > delta
            valid = (rel + b * kb) < delta
            nl = softplus[:, cols]
            if masked:
                nl = jnp.where(valid, nl, 0.0)
            excl, tot = suffix(nl)
            arg = z[:, cols] - softplus[:, cols] - excl
            if carry is not None:
                arg = arg - carry
            if masked:
                arg = jnp.where(valid, arg, NEG_BIG)
            arg_ref[p, :, cols] = arg
            carry = tot if carry is None else carry + tot
        ctot_ref[p] = carry
        return jnp.min(carry)

    def stage_output(i, p, s):
        vals = v_ref[pl.ds(window_start(i), win), :]
        acc_ref[s] = jnp.dot(jnp.exp2(arg_ref[p]).astype(BF16), vals, preferred_element_type=F32)
        carry_ref[s] = ctot_ref[p]

    def finish(i, s, cmin):
        def cond(state):
            pos, cmin = state
            return (pos > -kb) & (cmin <= ATTN_STOP)

        def older(state):
            pos, _ = state
            start = pl.multiple_of(jnp.maximum(pos, 0), tq)
            keys = k_ref[pl.ds(start, kb), :]
            vals = v_ref[pl.ds(start, kb), :]
            fresh = lax.broadcasted_iota(jnp.int32, (1, kb), 1) < pos + kb - start
            nlog_nb, log_b = scores(stacked_queries(i), keys)
            nlog_nb = jnp.where(fresh, nlog_nb, 0.0)
            excl, tot = suffix(nlog_nb)
            carry = carry_ref[s]
            a = jnp.where(fresh, jnp.exp2(log_b - excl - carry), 0.0)
            acc_ref[s] += jnp.dot(a.astype(BF16), vals, preferred_element_type=F32)
            carry_ref[s] = carry + tot
            return pos - kb, jnp.min(carry + tot)

        lax.while_loop(cond, older, (jnp.asarray(window_start(i) - kb, jnp.int32), cmin))
        store(i, acc_ref[s])

    n_blocks = seq // tq
    unroll = acc_ref.shape[0]
    lag = z_ref.shape[0] - 1
    n_slots = lag + 1
    pending = []
    for tau in range(fill):
        stage_scores(tau, tau % n_slots)
        if tau >= lag:
            j = tau - lag
            pending.append(stage_exponents(j % n_slots, j * tq - window_start(j)))
        if tau >= 2 * lag:
            j = tau - 2 * lag
            stage_output(j, j % n_slots, 0)
            finish(j, 0, pending.pop(0))

    def steady(m, pending):
        pending = list(pending)
        tau0 = fill + unroll * m
        done = []
        for u in range(unroll):
            stage_output(tau0 + u - 2 * lag, (fill - 2 * lag + u) % n_slots, u)
            done.append((tau0 + u - 2 * lag, u, pending.pop(0)))
            pending.append(stage_exponents((fill - lag + u) % n_slots, lookback))
            stage_scores(tau0 + u, (fill + u) % n_slots)
        for block, s, cmin in done:
            finish(block, s, cmin)
        return tuple(pending)

    pending = list(lax.fori_loop(0, (n_blocks - fill) // unroll, steady, tuple(pending)))
    for tau in range(n_blocks, n_blocks + 2 * lag):
        j = tau - 2 * lag
        stage_output(j, j % n_slots, 0)
        cmin = pending.pop(0)
        if tau - lag < n_blocks:
            pending.append(stage_exponents((tau - lag) % n_slots, lookback))
        finish(j, 0, cmin)


def _attention(q, k, v, batch, seq):
    n, width = q.shape
    tq, kb, nsub = ATTN_QUERY_ROWS, LANES, ATTN_WINDOW_BLOCKS
    n_blocks = seq // tq
    lag, n_slots = ATTN_STAGE_LAG, ATTN_STAGE_LAG + 1
    clipped = -(-(nsub * kb - tq) // tq)
    min_fill = clipped + 2 * lag
    assert seq % tq == 0 and seq >= nsub * kb and n_blocks >= min_fill
    options = [(f + 2 * (n_blocks - f) // u, u, f)
               for u in range(ATTN_UNROLL - ATTN_UNROLL % n_slots, 0, -n_slots)
               for f in range(min_fill, n_blocks + 1) if (n_blocks - f) % u == 0]
    _, unroll, fill = min(options)
    blk = pl.BlockSpec((seq, LANES), lambda b, hp: (b, hp))
    rows = HEADS_PER_BLOCK * tq
    stage_buf = pltpu.VMEM((n_slots, rows, nsub * kb), F32)
    carry_buf = pltpu.VMEM((n_slots, rows, kb), F32)
    row_buf = pltpu.VMEM((unroll, rows, LANES), F32)
    return pl.pallas_call(
        functools.partial(_attn_kernel, tq=tq, kb=kb, nsub=nsub, fill=fill),
        grid=(batch, width // LANES),
        in_specs=[blk, blk, blk],
        out_specs=blk,
        out_shape=jax.ShapeDtypeStruct((n, width), F32),
        scratch_shapes=[pltpu.VMEM((kb, 2 * kb), BF16),
                        stage_buf, stage_buf, carry_buf, row_buf, row_buf],
        compiler_params=_cparams(("arbitrary", "arbitrary")),
        name="attn",
    )(q, k, v)


def _gelu_tanh(x):
    c = math.sqrt(2.0 / math.pi)
    half_x = 0.5 * x
    return half_x + half_x * jnp.tanh(x * (c + (c * 0.044715) * (x * x)))


def _sigmoid(x):
    return 0.5 + 0.5 * jnp.tanh(0.5 * x)


def _lru_kernel(*refs, ts, n_slab):
    xl_refs, gl_refs = refs[0:n_slab], refs[n_slab:2 * n_slab]
    cw_ref, cb_ref, wr_ref, br_ref, wi_ref, bi_ref, lam_ref = refs[2 * n_slab:2 * n_slab + 7]
    o_refs = refs[2 * n_slab + 7:3 * n_slab + 7]
    (tail_ref, a7_ref, u7_ref, hp_ref, pa_ref, pu_ref, h_ref,
     wr_bd, wi_bd) = refs[3 * n_slab + 7:]
    t = pl.program_id(1)
    groups = ts // SUBLANES

    @pl.when(t == 0)
    def _():
        tail_ref[...] = jnp.zeros_like(tail_ref)
        h_ref[...] = jnp.zeros_like(h_ref)
        per = LANES // wr_ref.shape[1]
        for src, dst in ((wr_ref, wr_bd), (wi_ref, wi_bd)):
            for c in range(n_slab):
                rows = []
                for p in range(per):
                    blk = src[c * per + p]
                    rows.append(jnp.concatenate(
                        [blk if q == p else jnp.zeros_like(blk) for q in range(per)], axis=1))
                dst[c] = jnp.concatenate(rows, axis=0).astype(BF16)

    first_group = lax.broadcasted_iota(jnp.int32, (groups, LANES), 0) == 0
    for c in range(n_slab):
        lanes = slice(c * LANES, (c + 1) * LANES)
        x = [xl_refs[c][pl.ds(s, groups, stride=SUBLANES), :] for s in range(SUBLANES)]
        shifted = {}
        for s in range(SUBLANES - (CONV_W - 1), SUBLANES):
            shifted[s] = jnp.where(first_group, tail_ref[c, s:s + 1, :], pltpu.roll(x[s], 1, axis=0))
            tail_ref[c, s:s + 1, :] = x[s][groups - 1:groups, :]
        conv = []
        for s in range(SUBLANES):
            y = cb_ref[:, lanes]
            for w in range(CONV_W):
                j = s - (CONV_W - 1) + w
                y = y + (x[j] if j >= 0 else shifted[j + SUBLANES]) * cw_ref[w:w + 1, lanes]
            conv.append(y)
        xc = jnp.concatenate(conv, axis=0)

        xcb = xc.astype(BF16)
        r = _sigmoid(jnp.dot(xcb, wr_bd[c], preferred_element_type=F32) + br_ref[:, lanes])
        ig = _sigmoid(jnp.dot(xcb, wi_bd[c], preferred_element_type=F32) + bi_ref[:, lanes])
        lam = lam_ref[:, lanes]
        log_sig_lam = -(jnp.maximum(-lam, 0.0) + jnp.log1p(jnp.exp(-jnp.abs(lam))))
        log_a = r * (RG_C * log_sig_lam)
        a = jnp.exp(log_a)
        th = jnp.tanh(log_a)
        one_m_a2 = -2.0 * th / (1.0 - th)
        root = jnp.where(one_m_a2 > 0.0, one_m_a2 * lax.rsqrt(one_m_a2), 0.0)
        u = root * (ig * xc)

        a_run = u_run = None
        for s in range(SUBLANES):
            rows = slice(s * groups, (s + 1) * groups)
            if s == 0:
                a_run, u_run = a[rows], u[rows]
            else:
                u_run = a[rows] * u_run + u[rows]
                a_run = a_run * a[rows]
            pa_ref[c, rows, :] = a_run
            pu_ref[c, rows, :] = u_run
        a7_ref[c] = a_run
        u7_ref[c] = u_run

    def group(g, hs):
        nxt = []
        for c in range(n_slab):
            hp_ref[c, pl.ds(g, 1), :] = hs[c]
            nxt.append(a7_ref[c, pl.ds(g, 1), :] * hs[c] + u7_ref[c, pl.ds(g, 1), :])
        return tuple(nxt)

    hs = lax.fori_loop(0, groups, group, tuple(h_ref[c] for c in range(n_slab)), unroll=8)
    for c in range(n_slab):
        h_ref[c] = hs[c]

    for c in range(n_slab):
        h_in = hp_ref[c]
        for s in range(SUBLANES):
            rows = slice(s * groups, (s + 1) * groups)
            hseq = pu_ref[c, rows, :] + pa_ref[c, rows, :] * h_in
            gate = _gelu_tanh(gl_refs[c][pl.ds(s, groups, stride=SUBLANES), :])
            o_refs[c][pl.ds(s, groups, stride=SUBLANES), :] = hseq * gate


def _lru(xl, gl, conv_w, conv_b, w_r, br, w_i, bi, lam, batch, seq, ts):
    n, width = xl.shape
    nt = seq // ts
    n_slab = width // LANES
    blocks, block_w, _ = w_r.shape
    assert blocks * block_w == width and LANES % block_w == 0 and ts % (SUBLANES * SUBLANES) == 0
    groups = ts // SUBLANES
    gate_bd = pltpu.VMEM((n_slab, LANES, LANES), BF16)
    slab = [pl.BlockSpec((ts, LANES), functools.partial(lambda b, t, c: (b * nt + t, c), c=c))
            for c in range(n_slab)]
    const2 = lambda b, t: (0, 0)
    const3 = lambda b, t: (0, 0, 0)
    vec = pl.BlockSpec((1, width), const2)
    per_group = pltpu.VMEM((n_slab, groups, LANES), F32)
    per_step = pltpu.VMEM((n_slab, ts, LANES), F32)
    return pl.pallas_call(
        functools.partial(_lru_kernel, ts=ts, n_slab=n_slab),
        grid=(batch, nt),
        in_specs=slab + slab + [pl.BlockSpec((CONV_W, width), const2), vec,
                                pl.BlockSpec(w_r.shape, const3), vec,
                                pl.BlockSpec(w_i.shape, const3), vec, vec],
        out_specs=[pl.BlockSpec((ts, LANES), lambda b, t: (b * nt + t, 0))] * n_slab,
        out_shape=[jax.ShapeDtypeStruct((n, LANES), F32)] * n_slab,
        scratch_shapes=[pltpu.VMEM((n_slab, SUBLANES, LANES), F32),
                        per_group, per_group, per_group, per_step, per_step,
                        pltpu.VMEM((n_slab, 1, LANES), F32), gate_bd, gate_bd],
        compiler_params=_cparams(("arbitrary", "arbitrary")),
        name="lru",
    )(*([xl] * n_slab), *([gl] * n_slab), conv_w, conv_b, w_r, br, w_i, bi, lam)


def _mix_route_kernel(sb_ref, *refs, tm, n_slab):
    lru_refs = refs[:n_slab]
    (x_ref, sbg_ref, lrug_ref, wo_ref, n2g_ref, wrs_ref, rb_ref,
     h_ref, u2_ref, rr_ref, rwt_ref, tc_ref, before_ref, u2s_ref, wo_bf) = refs[n_slab:]

    @pl.when(pl.program_id(0) == 0)
    def _():
        wo_bf[...] = wo_ref[...].astype(BF16)

    _mix_route_body(sb_ref, lru_refs, x_ref, sbg_ref, lrug_ref, wo_bf, n2g_ref, wrs_ref, rb_ref,
                    h_ref, u2_ref, rr_ref, rwt_ref, tc_ref, before_ref, u2s_ref, tm)


def _mix_route_body(sb_ref, lru_refs, x_ref, sbg_ref, lrug_ref, wo_ref, n2g_ref, wrs_ref, rb_ref,
                    h_ref, u2_ref, rr_ref, rwt_ref, tc_ref, before_ref, u2s_ref, tm):
    step = pl.program_id(0)
    half = sb_ref.shape[1]
    n_sub = x_ref.shape[0] // tm

    @pl.when(step == 0)
    def _():
        r_id = lax.broadcasted_iota(jnp.int32, (tm, tm), 0)
        c_id = lax.broadcasted_iota(jnp.int32, (tm, tm), 1)
        before_ref[...] = (r_id < c_id).astype(BF16)
        tc_ref[...] = jnp.zeros_like(tc_ref)
        u2s_ref[...] = jnp.zeros_like(u2s_ref)

    def route(j):
        rows = slice(j * tm, (j + 1) * tm)
        u2 = u2s_ref[rows, :]
        u_hi = u2.astype(BF16)
        u_lo = (u2 - u_hi.astype(F32)).astype(BF16)
        nt_dims = (((1,), (1,)), ((), ()))
        n_rows = rb_ref.shape[0]
        both = lax.dot_general(wrs_ref[...], u_hi, nt_dims, preferred_element_type=F32)
        lt = (both[0:n_rows] + both[n_rows:2 * n_rows]
              + lax.dot_general(wrs_ref[0:n_rows, :], u_lo, nt_dims, preferred_element_type=F32)
              + rb_ref[...])

        sub = lax.broadcasted_iota(jnp.int32, (SUBLANES, tm), 0)

        def top1(x):
            m = jnp.max(x, axis=0, keepdims=True)
            idx = jnp.min(jnp.where(x == m, sub, SUBLANES), axis=0, keepdims=True)
            return m, idx

        grp = lt[0:SUBLANES, :]
        g_max, g_idx = top1(grp)
        g_p = 1.0 / jnp.sum(jnp.exp(grp - g_max), axis=0, keepdims=True)
        fine = lt[SUBLANES:2 * SUBLANES, :]
        for g in range(1, N_GROUPS):
            fine = jnp.where(g_idx == g, lt[(g + 1) * SUBLANES:(g + 2) * SUBLANES, :], fine)
        m1, i1 = top1(fine)
        m2, i2 = top1(jnp.where(sub == i1, -jnp.inf, fine))
        e2 = jnp.exp(m2 - m1)
        p1 = 1.0 / (1.0 + e2)
        w1 = g_p * p1
        w2 = g_p * (e2 * p1)
        x1 = g_idx * EXPERTS_PER_GROUP + i1
        x2 = g_idx * EXPERTS_PER_GROUP + i2

        eid = lax.broadcasted_iota(jnp.int32, (N_EXPERTS, tm), 0)
        oh1 = jnp.where(eid == x1, 1.0, 0.0)
        oh2 = jnp.where(eid == x2, 1.0, 0.0)
        pre = jnp.dot(jnp.concatenate([oh1, oh2], axis=0).astype(BF16), before_ref[...],
                      preferred_element_type=F32)
        pre1, pre2 = pre[0:N_EXPERTS], pre[N_EXPERTS:2 * N_EXPERTS]
        cnt1 = jnp.sum(oh1, axis=1, keepdims=True)
        cnt2 = jnp.sum(oh2, axis=1, keepdims=True)
        seg8 = jnp.floor((cnt1 + cnt2 + (SUBLANES - 1.0)) * (1.0 / SUBLANES))
        e_r = lax.broadcasted_iota(jnp.int32, (N_EXPERTS, N_EXPERTS), 0)
        e_c = lax.broadcasted_iota(jnp.int32, (N_EXPERTS, N_EXPERTS), 1)
        lower = jnp.where(e_c < e_r, 1.0, 0.0).astype(BF16)
        seg8_b = jnp.broadcast_to(seg8, (N_EXPERTS, LANES)).astype(BF16)
        seg_off = SUBLANES * jnp.dot(lower, seg8_b, preferred_element_type=F32)[:, 0:1]
        pos1 = jnp.sum(oh1 * (pre1 + seg_off), axis=0, keepdims=True)
        pos2 = jnp.sum(oh2 * (pre2 + (seg_off + cnt1)), axis=0, keepdims=True)

        lane = lax.broadcasted_iota(jnp.int32, tc_ref.shape, 1)
        seg_rows = jnp.broadcast_to(seg8 * SUBLANES, tc_ref.shape).astype(jnp.int32)
        tile = (step - 1) * n_sub + j
        tc_ref[...] = jnp.where(lane == tile, seg_rows, tc_ref[...])

        zrow = jnp.zeros((SUBLANES - 4, tm), jnp.int32)
        rr_ref[:, rows] = jnp.concatenate(
            [pos1.astype(jnp.int32), pos2.astype(jnp.int32), x1, x2, zrow], axis=0)
        wt = jnp.concatenate([w1, w2, pos1, pos2, jnp.zeros((LANES - 4, tm), F32)], axis=0)
        rwt_ref[rows, :] = wt.T

    m_sb = _rms_f32(sb_ref[...], sbg_ref[...]).astype(BF16)
    lru = jnp.concatenate([r[...] for r in lru_refs], axis=-1)
    m_lru = _rms_f32(lru, lrug_ref[...]).astype(BF16)
    h_ref[...] = (x_ref[...]
                  + jnp.dot(m_sb, wo_ref[0:half, :], preferred_element_type=F32)
                  + jnp.dot(m_lru, wo_ref[half:2 * half, :], preferred_element_type=F32))

    for j in range(n_sub):
        route(j)

    u2_next = _rms_f32(h_ref[...], n2g_ref[...])
    u2_ref[...] = u2_next.astype(BF16)
    u2s_ref[...] = u2_next


def _mix_route(sb, lru, x2, sbg, lrug, w_out, n2g, wr_stack, rbias, tm, tp):
    n, d = x2.shape
    half = sb.shape[1]
    n_tiles = n // tp
    row = lambda i: (jnp.minimum(i, n_tiles - 1), 0)
    routed = lambda i: (jnp.maximum(i - 1, 0), 0)
    const = lambda i: (0, 0)
    return pl.pallas_call(
        functools.partial(_mix_route_kernel, tm=tm, n_slab=len(lru)),
        grid=(n_tiles + 1,),
        in_specs=[pl.BlockSpec((tp, half), row)] + [pl.BlockSpec((tp, LANES), row)] * len(lru)
                 + [pl.BlockSpec((tp, d), row),
                  pl.BlockSpec((1, half), const), pl.BlockSpec((1, half), const),
                  pl.BlockSpec(w_out.shape, const, pipeline_mode=pl.Buffered(1)),
                  pl.BlockSpec((1, d), const),
                  pl.BlockSpec(wr_stack.shape, const), pl.BlockSpec(rbias.shape, const)],
        out_specs=[pl.BlockSpec((tp, d), row), pl.BlockSpec((tp, d), row),
                   pl.BlockSpec((SUBLANES, tp), lambda i: (0, jnp.maximum(i - 1, 0))),
                   pl.BlockSpec((tp, LANES), routed),
                   pl.BlockSpec((N_EXPERTS, LANES), const)],
        out_shape=[jax.ShapeDtypeStruct((n, d), F32), jax.ShapeDtypeStruct((n, d), BF16),
                   jax.ShapeDtypeStruct((SUBLANES, n), jnp.int32),
                   jax.ShapeDtypeStruct((n, LANES), F32),
                   jax.ShapeDtypeStruct((N_EXPERTS, LANES), jnp.int32)],
        scratch_shapes=[pltpu.VMEM((tm, tm), BF16), pltpu.VMEM((tp, d), F32),
                        pltpu.VMEM(w_out.shape, BF16)],
        compiler_params=_cparams(("arbitrary",)),
        name="mix_route",
    )(sb, *lru, x2, sbg, lrug, w_out, n2g, wr_stack, rbias)


def _pack_halves(x):
    half = x.shape[1] // 2
    lo = lax.shift_right_logical(lax.bitcast_convert_type(x[:, :half], jnp.int32), 16)
    hi = lax.bitcast_convert_type(x[:, half:], jnp.int32) & HIGH_HALF
    return hi | lo


def _unpack_halves(p):
    lo = lax.bitcast_convert_type(lax.shift_left(p, 16), F32)
    hi = lax.bitcast_convert_type(p & HIGH_HALF, F32)
    return lo.astype(BF16), hi.astype(BF16)


def _segment_copies(tile, c8_ref, loff_ref, goff_ref, make):
    for e in range(N_EXPERTS):
        idx = tile * N_EXPERTS + e
        rows = pl.multiple_of(c8_ref[idx], SUBLANES)

        @pl.when(rows > 0)
        def _(idx=idx, rows=rows):
            lo = pl.multiple_of(loff_ref[idx], SUBLANES)
            go = pl.multiple_of(goff_ref[idx], SUBLANES)
            make(pl.ds(lo, rows), pl.ds(go, rows)).start()


def _dispatch_kernel(c8_ref, loff_ref, goff_ref, tot_ref, used_ref,
                     rr_ref, u2_ref, xs_ref, lbuf, zbuf, sem, zsem, *, td, lrows):
    i = pl.program_id(0)
    slot = i % 2

    r_id = lax.broadcasted_iota(jnp.int32, (lrows, td), 0)
    perm = jnp.where(r_id == rr_ref[0:1, :], 1.0, jnp.where(r_id == rr_ref[1:2, :], 1.0, 0.0))
    sorted_rows = jnp.dot(perm.astype(BF16), u2_ref[...], preferred_element_type=F32)
    lbuf[slot] = _pack_halves(sorted_rows)

    @pl.when(i == 0)
    def _():
        zbuf[...] = jnp.zeros_like(zbuf)
        chunk = zbuf.shape[0]
        used = used_ref[0]
        spare = xs_ref.shape[0] - used
        n_fill = (spare + chunk - 1) // chunk

        def fill_copy(k):
            rows = pl.multiple_of(jnp.minimum(chunk, spare - k * chunk), SUBLANES)
            start = pl.multiple_of(used + k * chunk, SUBLANES)
            return pltpu.make_async_copy(zbuf.at[pl.ds(0, rows)], xs_ref.at[pl.ds(start, rows)],
                                         zsem)

        def fill_start(k, c):
            fill_copy(k).start()
            return c

        def fill_wait(k, c):
            fill_copy(k).wait()
            return c

        lax.fori_loop(0, n_fill, fill_start, 0)
        lax.fori_loop(0, n_fill, fill_wait, 0)

    def wait_tile(tile, s):
        rows = pl.multiple_of(tot_ref[tile], SUBLANES)
        pltpu.make_async_copy(lbuf.at[s, pl.ds(0, rows)], xs_ref.at[pl.ds(0, rows)], sem).wait()

    @pl.when(i > 0)
    def _():
        wait_tile(i - 1, 1 - slot)

    _segment_copies(i, c8_ref, loff_ref, goff_ref,
                    lambda loc, glob: pltpu.make_async_copy(lbuf.at[slot, loc], xs_ref.at[glob], sem))

    @pl.when(i == pl.num_programs(0) - 1)
    def _():
        wait_tile(i, slot)


def _dispatch(c8, loff, goff, tot, used, rr, u2, p_rows, td, lrows, tme):
    n, d = u2.shape
    pmap = lambda i, *_: (0, i)
    return pl.pallas_call(
        functools.partial(_dispatch_kernel, td=td, lrows=lrows),
        grid_spec=pltpu.PrefetchScalarGridSpec(
            num_scalar_prefetch=5,
            grid=(n // td,),
            in_specs=[pl.BlockSpec((SUBLANES, td), pmap),
                      pl.BlockSpec((td, d), lambda i, *_: (i, 0))],
            out_specs=pl.BlockSpec(memory_space=pl.ANY),
            scratch_shapes=[pltpu.VMEM((2, lrows, d // 2), jnp.int32),
                            pltpu.VMEM((tme, d // 2), jnp.int32),
                            pltpu.SemaphoreType.DMA(()), pltpu.SemaphoreType.DMA(())]),
        out_shape=jax.ShapeDtypeStruct((p_rows, d // 2), jnp.int32),
        compiler_params=_cparams(("arbitrary",)),
        name="dispatch",
    )(c8, loff, goff, tot, used, rr, u2)


def _experts_kernel(eoff_ref, erows_ref, xs_ref, wg_ref, wu_ref, wd_ref, ys_ref,
                    wg_bf, wu_bf, wd_bf, xbuf, ybuf, sem_in, sem_out, state, *, tme):
    e = pl.program_id(0)
    n_experts = pl.num_programs(0)
    rows = erows_ref[e]
    off = eoff_ref[e]
    n_tiles = (rows + tme - 1) // tme
    nxt = jnp.minimum(e + 1, n_experts - 1)
    prefetch = (rows > 0) & (e + 1 < n_experts) & (erows_ref[nxt] > 0)

    def tile_rows(total, k):
        return pl.multiple_of(jnp.minimum(tme, total - k * tme), SUBLANES)

    def in_copy(start, r, slot):
        start = pl.multiple_of(start, SUBLANES)
        return pltpu.make_async_copy(xs_ref.at[pl.ds(start, r)], xbuf.at[slot, pl.ds(0, r)],
                                     sem_in.at[slot])

    def out_copy(start, r, slot):
        start = pl.multiple_of(start, SUBLANES)
        return pltpu.make_async_copy(ybuf.at[slot, pl.ds(0, r)], ys_ref.at[pl.ds(start, r)],
                                     sem_out.at[slot])

    @pl.when(e == 0)
    def _():
        for s in range(4):
            state[s] = 0
        xbuf[...] = jnp.zeros_like(xbuf)

    @pl.when(rows > 0)
    def _():
        wg_bf[...] = wg_ref[0].astype(BF16)
        wu_bf[...] = wu_ref[0].astype(BF16)
        wd_bf[...] = wd_ref[0].astype(BF16)
        done = state[0]

        @pl.when(state[1] == 0)
        def _():
            in_copy(off, tile_rows(rows, 0), done % 2).start()

        def tile(k, c):
            slot = (done + k) % 2
            r = tile_rows(rows, k)

            @pl.when(k + 1 < n_tiles)
            def _():
                in_copy(off + (k + 1) * tme, tile_rows(rows, k + 1), 1 - slot).start()

            @pl.when((k + 1 == n_tiles) & prefetch)
            def _():
                in_copy(eoff_ref[nxt], tile_rows(erows_ref[nxt], 0), 1 - slot).start()

            in_copy(off, r, slot).wait()

            @pl.when(done + k >= 2)
            def _():
                out_copy(0, pl.multiple_of(state[2 + slot], SUBLANES), slot).wait()

            def mlp(start, n):
                x = jnp.concatenate(_unpack_halves(xbuf[slot, start:start + n]), axis=1)
                hg = jnp.dot(x, wg_bf[...], preferred_element_type=F32)
                hu = jnp.dot(x, wu_bf[...], preferred_element_type=F32)
                act = (hg * jax.nn.sigmoid(hg) * hu).astype(BF16)
                y = jnp.dot(act, wd_bf[...], preferred_element_type=F32)
                ybuf[slot, start:start + n] = _pack_halves(y.astype(BF16).astype(F32))

            for below, n in zip((0,) + EXPERT_TILE_SIZES, EXPERT_TILE_SIZES):
                @pl.when((r > below) & (r <= n))
                def _(n=n):
                    mlp(0, n)

            out_copy(off + k * tme, r, slot).start()
            state[2 + slot] = r
            return c

        lax.fori_loop(0, n_tiles, tile, 0)
        state[0] = done + n_tiles

    state[1] = prefetch.astype(jnp.int32)

    @pl.when(e == n_experts - 1)
    def _():
        total = eoff_ref[e] + rows
        for slot in range(2):
            @pl.when(state[0] > slot)
            def _(slot=slot):
                out_copy(0, pl.multiple_of(state[2 + slot], SUBLANES), slot).wait()
        ybuf[0] = jnp.zeros(ybuf.shape[1:], ybuf.dtype)
        spare = ys_ref.shape[0] - total
        n_fill = (spare + tme - 1) // tme

        def fill_copy(k):
            return out_copy(total + k * tme, tile_rows(spare, k), 0)

        def fill_start(k, c):
            fill_copy(k).start()
            return c

        def fill_wait(k, c):
            fill_copy(k).wait()
            return c

        lax.fori_loop(0, n_fill, fill_start, 0)
        lax.fori_loop(0, n_fill, fill_wait, 0)


def _experts(eoff, erows, xs, wg, wu, wd, tme):
    p = xs.shape[0]
    n_experts, d, de = wg.shape
    wmap = lambda e, *_: (e, 0, 0)
    return pl.pallas_call(
        functools.partial(_experts_kernel, tme=tme),
        grid_spec=pltpu.PrefetchScalarGridSpec(
            num_scalar_prefetch=2,
            grid=(n_experts,),
            in_specs=[pl.BlockSpec(memory_space=pl.ANY),
                      pl.BlockSpec((1, d, de), wmap), pl.BlockSpec((1, d, de), wmap),
                      pl.BlockSpec((1, de, d), wmap)],
            out_specs=pl.BlockSpec(memory_space=pl.ANY),
            scratch_shapes=[pltpu.VMEM((d, de), BF16), pltpu.VMEM((d, de), BF16),
                            pltpu.VMEM((de, d), BF16),
                            pltpu.VMEM((2, tme, d // 2), jnp.int32),
                            pltpu.VMEM((2, tme, d // 2), jnp.int32),
                            pltpu.SemaphoreType.DMA((2,)), pltpu.SemaphoreType.DMA((2,)),
                            pltpu.SMEM((4,), jnp.int32)]),
        out_shape=jax.ShapeDtypeStruct((p, d // 2), jnp.int32),
        compiler_params=_cparams(("arbitrary",)),
        name="experts",
    )(eoff, erows, xs, wg, wu, wd)


def _combine_kernel(c8_ref, loff_ref, goff_ref, tot_ref,
                    rwt_ref, nxt_ref, h_ref, fg_ref, ys_ref, y_ref, ybuf, sems, spread_ref,
                    *, tc, lrows):
    i = pl.program_id(0)
    slot = i % 2

    def gather_tile(tile, s):
        _segment_copies(tile, c8_ref, loff_ref, goff_ref,
                        lambda loc, glob: pltpu.make_async_copy(ys_ref.at[glob], ybuf.at[s, loc],
                                                                sems.at[s]))

    def spread(ref):
        w = ref[...]
        for c in range(4):
            spread_ref[c] = jnp.broadcast_to(w[:, c:c + 1], (tc, LANES))

    @pl.when(i == 0)
    def _():
        ybuf[...] = jnp.zeros_like(ybuf)
        gather_tile(0, 0)
        spread(rwt_ref)

    @pl.when(i + 1 < pl.num_programs(0))
    def _():
        gather_tile(i + 1, 1 - slot)

    rows = pl.multiple_of(tot_ref[i], SUBLANES)
    pltpu.make_async_copy(ys_ref.at[pl.ds(0, rows)], ybuf.at[slot, pl.ds(0, rows)],
                          sems.at[slot]).wait()

    across = lambda c: jnp.tile(spread_ref[c], (1, lrows // LANES))
    c_id = lax.broadcasted_iota(jnp.int32, (tc, lrows), 1)
    pos1 = across(2).astype(jnp.int32)
    pos2 = across(3).astype(jnp.int32)
    wmat = jnp.where(c_id == pos1, across(0), jnp.where(c_id == pos2, across(1), 0.0)).astype(BF16)
    half = h_ref.shape[1] // 2
    outs, sumsq = [], 0.0
    for part, cols in zip(_unpack_halves(ybuf[slot]), (slice(0, half), slice(half, 2 * half))):
        out = h_ref[:, cols] + jnp.dot(wmat, part, preferred_element_type=F32)
        if not outs:
            spread(nxt_ref)
        sumsq = sumsq + jnp.sum(out * out, axis=-1, keepdims=True)
        outs.append(out)
    scale = lax.rsqrt(sumsq * (1.0 / (2 * half)) + EPS)
    for out, cols in zip(outs, (slice(0, half), slice(half, 2 * half))):
        y_ref[:, cols] = out * scale * fg_ref[:, cols]


def _combine(c8, loff, goff, tot, rwt, h, final_g, ys, tc, lrows):
    n, d = h.shape
    n_tiles = n // tc
    return pl.pallas_call(
        functools.partial(_combine_kernel, tc=tc, lrows=lrows),
        grid_spec=pltpu.PrefetchScalarGridSpec(
            num_scalar_prefetch=4,
            grid=(n_tiles,),
            in_specs=[pl.BlockSpec((tc, LANES), lambda i, *_: (i, 0)),
                      pl.BlockSpec((tc, LANES), lambda i, *_: (jnp.minimum(i + 1, n_tiles - 1), 0)),
                      pl.BlockSpec((tc, d), lambda i, *_: (i, 0)),
                      pl.BlockSpec((1, d), lambda i, *_: (0, 0)),
                      pl.BlockSpec(memory_space=pl.ANY)],
            out_specs=pl.BlockSpec((tc, d), lambda i, *_: (i, 0)),
            scratch_shapes=[pltpu.VMEM((2, lrows, d // 2), jnp.int32),
                            pltpu.SemaphoreType.DMA((2,)),
                            pltpu.VMEM((4, tc, LANES), F32)]),
        out_shape=jax.ShapeDtypeStruct((n, d), F32),
        compiler_params=_cparams(("arbitrary",)),
        name="combine",
    )(c8, loff, goff, tot, rwt, rwt, h, final_g, ys)


def _router_tables(w_group, b_group, w_fine, b_fine):
    d = w_group.shape[0]
    pad_g, pad_f = SUBLANES - N_GROUPS, ROUTER_ROWS - SUBLANES - N_EXPERTS
    w = jnp.concatenate([w_group.T, jnp.zeros((pad_g, d), F32), w_fine.T,
                         jnp.zeros((pad_f, d), F32)], axis=0)
    b = jnp.concatenate([b_group, jnp.full((pad_g,), NEG_BIG, F32), b_fine,
                         jnp.full((pad_f,), NEG_BIG, F32)])
    w_hi = w.astype(BF16)
    w_lo = (w - w_hi.astype(F32)).astype(BF16)
    return jnp.concatenate([w_hi, w_lo], axis=0), b.reshape(ROUTER_ROWS, 1)


def kernel(x, norm1_g, w_in, conv_w, conv_b, w_rgate, b_rgate, w_igate, b_igate, lam, sb_norm_g,
           lru_norm_g, w_out, norm2_g, w_group, b_group, w_fine, b_fine, w_e_gate, w_e_up,
           w_e_down, final_g):
    batch, seq, d = x.shape
    n = batch * seq
    width = w_in.shape[1] // 5
    tm = min(TOKEN_TILE, seq)
    tp = min(ROW_TILE, seq)
    tme = EXPERT_TILE_ROWS

    x2 = x.reshape(n, d)
    vec = lambda a: a.reshape(1, -1)

    q, k, v, xl, gl = _in_proj(x2, vec(norm1_g), w_in, width, tp)
    out_sb = _attention(q, k, v, batch, seq)
    out_lru = _lru(xl, gl, conv_w, vec(conv_b), w_rgate, vec(b_rgate), w_igate, vec(b_igate),
                   vec(lam), batch, seq, tp)

    wr_stack, rbias = _router_tables(w_group, b_group, w_fine, b_fine)
    h, u2, rr, rwt, tcnt = _mix_route(out_sb, out_lru, x2, vec(sb_norm_g), vec(lru_norm_g),
                                      w_out, vec(norm2_g), wr_stack, rbias, tm, tp)

    n_tiles = n // tm
    assert n_tiles <= LANES, "one lane of the per-tile count table per token tile"
    c8 = tcnt[:, :n_tiles].T
    erows = jnp.sum(c8, axis=0)
    eoff = jnp.cumsum(erows) - erows
    goff = eoff[None, :] + jnp.cumsum(c8, axis=0) - c8
    loff = jnp.cumsum(c8, axis=1) - c8
    tot = jnp.sum(c8, axis=1)
    lrows = 2 * tm + N_EXPERTS * SUBLANES
    p_rows = 2 * n + n_tiles * N_EXPERTS * (SUBLANES - 1)
    p_rows = -(-p_rows // SUBLANES) * SUBLANES
    i32 = lambda a: a.reshape(-1).astype(jnp.int32)
    c8, loff, goff, tot, eoff, erows = (i32(a) for a in (c8, loff, goff, tot, eoff, erows))

    xs = _dispatch(c8, loff, goff, tot, jnp.sum(erows, keepdims=True), rr, u2, p_rows, tm, lrows, tme)
    ys = _experts(eoff, erows, xs, w_e_gate, w_e_up, w_e_down, tme)
    y = _combine(c8, loff, goff, tot, rwt, h, vec(final_g), ys, tm, lrows)
    return y.reshape(batch, seq, d)
```

```python
import functools
import math

import jax
import jax.numpy as jnp
from jax import lax
from jax.experimental import pallas as pl
from jax.experimental.pallas import tpu as pltpu

F32 = jnp.float32
BF16 = jnp.bfloat16

EPS = 1e-6
HEAD_DIM = 64
HEADS_PER_BLOCK = 2
LANES = 128
SUBLANES = 8
CONV_W = 4
RG_C = 8.0
N_GROUPS = 4
EXPERTS_PER_GROUP = 8
N_EXPERTS = N_GROUPS * EXPERTS_PER_GROUP
ROUTER_ROWS = 48
NEG_BIG = -1e30
LOG2_E = math.log2(math.e)
ATTN_STOP = 104.0 * LOG2_E
ATTN_QUERY_ROWS = 64
ATTN_WINDOW_BLOCKS = 2
ATTN_UNROLL = 24
ATTN_STAGE_LAG = 2
HIGH_HALF = -65536
EXPERT_TILE_ROWS = 1536
EXPERT_TILE_SIZES = (256, 512, 1024, 1152, 1280, 1536)
TOKEN_TILE = 512
ROW_TILE = 1024

VMEM_LIMIT = 56 * 1024 * 1024


def _cparams(sem):
    return pltpu.CompilerParams(dimension_semantics=sem, vmem_limit_bytes=VMEM_LIMIT)


def _rms_f32(x, g):
    return x * lax.rsqrt(jnp.mean(x * x, axis=-1, keepdims=True) + EPS) * g


def _in_proj_kernel(x_ref, g_ref, w_ref, q_ref, k_ref, v_ref, xl_ref, gl_ref, w_bf, *,
                    width, q_scale):
    @pl.when(pl.program_id(0) == 0)
    def _():
        for c in range(w_ref.shape[1] // width):
            cols = slice(c * width, (c + 1) * width)
            w_bf[:, cols] = w_ref[:, cols].astype(BF16)

    u = _rms_f32(x_ref[...], g_ref[...]).astype(BF16)
    for c, o_ref in enumerate((q_ref, k_ref, v_ref, xl_ref, gl_ref)):
        p = jnp.dot(u, w_bf[:, c * width:(c + 1) * width], preferred_element_type=F32)
        if c == 0:
            p = p * q_scale
        o_ref[...] = p.astype(o_ref.dtype)


def _in_proj(x2, g, w, width, tm):
    n, d = x2.shape
    row = lambda i: (i, 0)
    out_bf = jax.ShapeDtypeStruct((n, width), BF16)
    out_f = jax.ShapeDtypeStruct((n, width), F32)
    return pl.pallas_call(
        functools.partial(_in_proj_kernel, width=width, q_scale=1.0 / math.sqrt(HEAD_DIM)),
        grid=(n // tm,),
        in_specs=[pl.BlockSpec((tm, d), row),
                  pl.BlockSpec((1, d), lambda i: (0, 0)),
                  pl.BlockSpec(w.shape, lambda i: (0, 0), pipeline_mode=pl.Buffered(1))],
        out_specs=[pl.BlockSpec((tm, width), row)] * 5,
        out_shape=[out_bf, out_bf, out_bf, out_f, out_f],
        scratch_shapes=[pltpu.VMEM(w.shape, BF16)],
        compiler_params=_cparams(("arbitrary",)),
        name="in_proj",
    )(x2, g, w)


def _attn_kernel(q_ref, k_ref, v_ref, o_ref, tri_ref, z_ref, arg_ref, ctot_ref, acc_ref, carry_ref,
                 *, tq, kb, nsub, fill):
    seq = q_ref.shape[0]
    win = nsub * kb
    lookback = win - tq
    lane = lax.broadcasted_iota(jnp.int32, (1, LANES), 1)
    rel = (lax.broadcasted_iota(jnp.int32, (tq, kb), 1)
           - lax.broadcasted_iota(jnp.int32, (tq, kb), 0))
    rel = jnp.concatenate([rel] * HEADS_PER_BLOCK, axis=0)

    k_r = lax.broadcasted_iota(jnp.int32, (kb, 2 * kb), 0)
    k_c = lax.broadcasted_iota(jnp.int32, (kb, 2 * kb), 1)
    tri_ref[...] = jnp.where(k_c >= kb, 1.0, jnp.where(k_r > k_c, 1.0, 0.0)).astype(BF16)

    def softplus2(z):
        return jnp.maximum(z, 0.0) + jnp.log2(1.0 + jnp.exp2(-jnp.abs(z)))

    def scores(qh, keys):
        z = LOG2_E * lax.dot_general(qh, keys, (((1,), (1,)), ((), ())),
                                     preferred_element_type=F32)
        nlog_nb = softplus2(z)
        return nlog_nb, z - nlog_nb

    def suffix(nlog_nb):
        r = jnp.dot(nlog_nb.astype(BF16), tri_ref[...], preferred_element_type=F32)
        return r[:, :kb], r[:, kb:]

    def stacked_queries(i):
        q = q_ref[pl.ds(i * tq, tq), :]
        return jnp.concatenate(
            [jnp.where((lane >= h * HEAD_DIM) & (lane < (h + 1) * HEAD_DIM), q, jnp.zeros_like(q))
             for h in range(HEADS_PER_BLOCK)], axis=0)

    def store(i, out):
        o_ref[pl.ds(i * tq, tq), :] = jnp.where(lane < HEAD_DIM, out[0:tq], out[tq:2 * tq])

    def window_start(i):
        if isinstance(i, int):
            return max(i * tq - lookback, 0)
        return pl.multiple_of(i * tq - lookback, tq)

    def stage_scores(i, p):
        keys = k_ref[pl.ds(window_start(i), win), :]
        z_ref[p] = LOG2_E * lax.dot_general(stacked_queries(i), keys, (((1,), (1,)), ((), ())),
                                            preferred_element_type=F32)

    def stage_exponents(p, delta):
        z = z_ref[p]
        softplus = softplus2(z)
        carry = None
        for b in reversed(range(nsub)):
            cols = slice(b * kb, (b + 1) * kb)
            masked = (b + 1) * kb > delta
            valid = (rel + b * kb) < delta
            nl = softplus[:, cols]
            if masked:
                nl = jnp.where(valid, nl, 0.0)
            excl, tot = suffix(nl)
            arg = z[:, cols] - softplus[:, cols] - excl
            if carry is not None:
                arg = arg - carry
            if masked:
                arg = jnp.where(valid, arg, NEG_BIG)
            arg_ref[p, :, cols] = arg
            carry = tot if carry is None else carry + tot
        ctot_ref[p] = carry
        return jnp.min(carry)

    def stage_output(i, p, s):
        vals = v_ref[pl.ds(window_start(i), win), :]
        acc_ref[s] = jnp.dot(jnp.exp2(arg_ref[p]).astype(BF16), vals, preferred_element_type=F32)
        carry_ref[s] = ctot_ref[p]

    def finish(i, s, cmin):
        def cond(state):
            pos, cmin = state
            return (pos > -kb) & (cmin <= ATTN_STOP)

        def older(state):
            pos, _ = state
            start = pl.multiple_of(jnp.maximum(pos, 0), tq)
            keys = k_ref[pl.ds(start, kb), :]
            vals = v_ref[pl.ds(start, kb), :]
            fresh = lax.broadcasted_iota(jnp.int32, (1, kb), 1) < pos + kb - start
            nlog_nb, log_b = scores(stacked_queries(i), keys)
            nlog_nb = jnp.where(fresh, nlog_nb, 0.0)
            excl, tot = suffix(nlog_nb)
            carry = carry_ref[s]
            a = jnp.where(fresh, jnp.exp2(log_b - excl - carry), 0.0)
            acc_ref[s] += jnp.dot(a.astype(BF16), vals, preferred_element_type=F32)
            carry_ref[s] = carry + tot
            return pos - kb, jnp.min(carry + tot)

        lax.while_loop(cond, older, (jnp.asarray(window_start(i) - kb, jnp.int32), cmin))
        store(i, acc_ref[s])

    n_blocks = seq // tq
    unroll = acc_ref.shape[0]
    lag = z_ref.shape[0] - 1
    n_slots = lag + 1
    pending = []
    for tau in range(fill):
        stage_scores(tau, tau % n_slots)
        if tau >= lag:
            j = tau - lag
            pending.append(stage_exponents(j % n_slots, j * tq - window_start(j)))
        if tau >= 2 * lag:
            j = tau - 2 * lag
            stage_output(j, j % n_slots, 0)
            finish(j, 0, pending.pop(0))

    def steady(m, pending):
        pending = list(pending)
        tau0 = fill + unroll * m
        done = []
        for u in range(unroll):
            stage_output(tau0 + u - 2 * lag, (fill - 2 * lag + u) % n_slots, u)
            done.append((tau0 + u - 2 * lag, u, pending.pop(0)))
            pending.append(stage_exponents((fill - lag + u) % n_slots, lookback))
            stage_scores(tau0 + u, (fill + u) % n_slots)
        for block, s, cmin in done:
            finish(block, s, cmin)
        return tuple(pending)

    pending = list(lax.fori_loop(0, (n_blocks - fill) // unroll, steady, tuple(pending)))
    for tau in range(n_blocks, n_blocks + 2 * lag):
        j = tau - 2 * lag
        stage_output(j, j % n_slots, 0)
        cmin = pending.pop(0)
        if tau - lag < n_blocks:
            pending.append(stage_exponents((tau - lag) % n_slots, lookback))
        finish(j, 0, cmin)


def _attention(q, k, v, batch, seq):
    n, width = q.shape
    tq, kb, nsub = ATTN_QUERY_ROWS, LANES, ATTN_WINDOW_BLOCKS
    n_blocks = seq // tq
    lag, n_slots = ATTN_STAGE_LAG, ATTN_STAGE_LAG + 1
    clipped = -(-(nsub * kb - tq) // tq)
    min_fill = clipped + 2 * lag
    assert seq % tq == 0 and seq >= nsub * kb and n_blocks >= min_fill
    options = [(f + 2 * (n_blocks - f) // u, u, f)
               for u in range(ATTN_UNROLL - ATTN_UNROLL % n_slots, 0, -n_slots)
               for f in range(min_fill, n_blocks + 1) if (n_blocks - f) % u == 0]
    _, unroll, fill = min(options)
    blk = pl.BlockSpec((seq, LANES), lambda b, hp: (b, hp))
    rows = HEADS_PER_BLOCK * tq
    stage_buf = pltpu.VMEM((n_slots, rows, nsub * kb), F32)
    carry_buf = pltpu.VMEM((n_slots, rows, kb), F32)
    row_buf = pltpu.VMEM((unroll, rows, LANES), F32)
    return pl.pallas_call(
        functools.partial(_attn_kernel, tq=tq, kb=kb, nsub=nsub, fill=fill),
        grid=(batch, width // LANES),
        in_specs=[blk, blk, blk],
        out_specs=blk,
        out_shape=jax.ShapeDtypeStruct((n, width), F32),
        scratch_shapes=[pltpu.VMEM((kb, 2 * kb), BF16),
                        stage_buf, stage_buf, carry_buf, row_buf, row_buf],
        compiler_params=_cparams(("arbitrary", "arbitrary")),
        name="attn",
    )(q, k, v)


def _gelu_tanh(x):
    c = math.sqrt(2.0 / math.pi)
    half_x = 0.5 * x
    return half_x + half_x * jnp.tanh(x * (c + (c * 0.044715) * (x * x)))


def _sigmoid(x):
    return 0.5 + 0.5 * jnp.tanh(0.5 * x)


def _lru_kernel(*refs, ts, n_slab):
    xl_refs, gl_refs = refs[0:n_slab], refs[n_slab:2 * n_slab]
    cw_ref, cb_ref, wr_ref, br_ref, wi_ref, bi_ref, lam_ref = refs[2 * n_slab:2 * n_slab + 7]
    o_refs = refs[2 * n_slab + 7:3 * n_slab + 7]
    (tail_ref, a7_ref, u7_ref, hp_ref, pa_ref, pu_ref, h_ref,
     wr_bd, wi_bd) = refs[3 * n_slab + 7:]
    t = pl.program_id(1)
    groups = ts // SUBLANES

    @pl.when(t == 0)
    def _():
        tail_ref[...] = jnp.zeros_like(tail_ref)
        h_ref[...] = jnp.zeros_like(h_ref)
        per = LANES // wr_ref.shape[1]
        for src, dst in ((wr_ref, wr_bd), (wi_ref, wi_bd)):
            for c in range(n_slab):
                rows = []
                for p in range(per):
                    blk = src[c * per + p]
                    rows.append(jnp.concatenate(
                        [blk if q == p else jnp.zeros_like(blk) for q in range(per)], axis=1))
                dst[c] = jnp.concatenate(rows, axis=0).astype(BF16)

    first_group = lax.broadcasted_iota(jnp.int32, (groups, LANES), 0) == 0
    for c in range(n_slab):
        lanes = slice(c * LANES, (c + 1) * LANES)
        x = [xl_refs[c][pl.ds(s, groups, stride=SUBLANES), :] for s in range(SUBLANES)]
        shifted = {}
        for s in range(SUBLANES - (CONV_W - 1), SUBLANES):
            shifted[s] = jnp.where(first_group, tail_ref[c, s:s + 1, :], pltpu.roll(x[s], 1, axis=0))
            tail_ref[c, s:s + 1, :] = x[s][groups - 1:groups, :]
        conv = []
        for s in range(SUBLANES):
            y = cb_ref[:, lanes]
            for w in range(CONV_W):
                j = s - (CONV_W - 1) + w
                y = y + (x[j] if j >= 0 else shifted[j + SUBLANES]) * cw_ref[w:w + 1, lanes]
            conv.append(y)
        xc = jnp.concatenate(conv, axis=0)

        xcb = xc.astype(BF16)
        r = _sigmoid(jnp.dot(xcb, wr_bd[c], preferred_element_type=F32) + br_ref[:, lanes])
        ig = _sigmoid(jnp.dot(xcb, wi_bd[c], preferred_element_type=F32) + bi_ref[:, lanes])
        lam = lam_ref[:, lanes]
        log_sig_lam = -(jnp.maximum(-lam, 0.0) + jnp.log1p(jnp.exp(-jnp.abs(lam))))
        log_a = r * (RG_C * log_sig_lam)
        a = jnp.exp(log_a)
        th = jnp.tanh(log_a)
        one_m_a2 = -2.0 * th / (1.0 - th)
        root = jnp.where(one_m_a2 > 0.0, one_m_a2 * lax.rsqrt(one_m_a2), 0.0)
        u = root * (ig * xc)

        a_run = u_run = None
        for s in range(SUBLANES):
            rows = slice(s * groups, (s + 1) * groups)
            if s == 0:
                a_run, u_run = a[rows], u[rows]
            else:
                u_run = a[rows] * u_run + u[rows]
                a_run = a_run * a[rows]
            pa_ref[c, rows, :] = a_run
            pu_ref[c, rows, :] = u_run
        a7_ref[c] = a_run
        u7_ref[c] = u_run

    def group(g, hs):
        nxt = []
        for c in range(n_slab):
            hp_ref[c, pl.ds(g, 1), :] = hs[c]
            nxt.append(a7_ref[c, pl.ds(g, 1), :] * hs[c] + u7_ref[c, pl.ds(g, 1), :])
        return tuple(nxt)

    hs = lax.fori_loop(0, groups, group, tuple(h_ref[c] for c in range(n_slab)), unroll=8)
    for c in range(n_slab):
        h_ref[c] = hs[c]

    for c in range(n_slab):
        h_in = hp_ref[c]
        for s in range(SUBLANES):
            rows = slice(s * groups, (s + 1) * groups)
            hseq = pu_ref[c, rows, :] + pa_ref[c, rows, :] * h_in
            gate = _gelu_tanh(gl_refs[c][pl.ds(s, groups, stride=SUBLANES), :])
            o_refs[c][pl.ds(s, groups, stride=SUBLANES), :] = hseq * gate


def _lru(xl, gl, conv_w, conv_b, w_r, br, w_i, bi, lam, batch, seq, ts):
    n, width = xl.shape
    nt = seq // ts
    n_slab = width // LANES
    blocks, block_w, _ = w_r.shape
    assert blocks * block_w == width and LANES % block_w == 0 and ts % (SUBLANES * SUBLANES) == 0
    groups = ts // SUBLANES
    gate_bd = pltpu.VMEM((n_slab, LANES, LANES), BF16)
    slab = [pl.BlockSpec((ts, LANES), functools.partial(lambda b, t, c: (b * nt + t, c), c=c))
            for c in range(n_slab)]
    const2 = lambda b, t: (0, 0)
    const3 = lambda b, t: (0, 0, 0)
    vec = pl.BlockSpec((1, width), const2)
    per_group = pltpu.VMEM((n_slab, groups, LANES), F32)
    per_step = pltpu.VMEM((n_slab, ts, LANES), F32)
    return pl.pallas_call(
        functools.partial(_lru_kernel, ts=ts, n_slab=n_slab),
        grid=(batch, nt),
        in_specs=slab + slab + [pl.BlockSpec((CONV_W, width), const2), vec,
                                pl.BlockSpec(w_r.shape, const3), vec,
                                pl.BlockSpec(w_i.shape, const3), vec, vec],
        out_specs=[pl.BlockSpec((ts, LANES), lambda b, t: (b * nt + t, 0))] * n_slab,
        out_shape=[jax.ShapeDtypeStruct((n, LANES), F32)] * n_slab,
        scratch_shapes=[pltpu.VMEM((n_slab, SUBLANES, LANES), F32),
                        per_group, per_group, per_group, per_step, per_step,
                        pltpu.VMEM((n_slab, 1, LANES), F32), gate_bd, gate_bd],
        compiler_params=_cparams(("arbitrary", "arbitrary")),
        name="lru",
    )(*([xl] * n_slab), *([gl] * n_slab), conv_w, conv_b, w_r, br, w_i, bi, lam)


def _mix_route_kernel(sb_ref, *refs, tm, n_slab):
    lru_refs = refs[:n_slab]
    (x_ref, sbg_ref, lrug_ref, wo_ref, n2g_ref, wrs_ref, rb_ref,
     h_ref, u2_ref, rr_ref, rwt_ref, tc_ref, before_ref, u2s_ref, wo_bf) = refs[n_slab:]

    @pl.when(pl.program_id(0) == 0)
    def _():
        wo_bf[...] = wo_ref[...].astype(BF16)

    _mix_route_body(sb_ref, lru_refs, x_ref, sbg_ref, lrug_ref, wo_bf, n2g_ref, wrs_ref, rb_ref,
                    h_ref, u2_ref, rr_ref, rwt_ref, tc_ref, before_ref, u2s_ref, tm)


def _mix_route_body(sb_ref, lru_refs, x_ref, sbg_ref, lrug_ref, wo_ref, n2g_ref, wrs_ref, rb_ref,
                    h_ref, u2_ref, rr_ref, rwt_ref, tc_ref, before_ref, u2s_ref, tm):
    step = pl.program_id(0)
    half = sb_ref.shape[1]
    n_sub = x_ref.shape[0] // tm

    @pl.when(step == 0)
    def _():
        r_id = lax.broadcasted_iota(jnp.int32, (tm, tm), 0)
        c_id = lax.broadcasted_iota(jnp.int32, (tm, tm), 1)
        before_ref[...] = (r_id < c_id).astype(BF16)
        tc_ref[...] = jnp.zeros_like(tc_ref)
        u2s_ref[...] = jnp.zeros_like(u2s_ref)

    def route(j):
        rows = slice(j * tm, (j + 1) * tm)
        u2 = u2s_ref[rows, :]
        u_hi = u2.astype(BF16)
        u_lo = (u2 - u_hi.astype(F32)).astype(BF16)
        nt_dims = (((1,), (1,)), ((), ()))
        n_rows = rb_ref.shape[0]
        both = lax.dot_general(wrs_ref[...], u_hi, nt_dims, preferred_element_type=F32)
        lt = (both[0:n_rows] + both[n_rows:2 * n_rows]
              + lax.dot_general(wrs_ref[0:n_rows, :], u_lo, nt_dims, preferred_element_type=F32)
              + rb_ref[...])

        sub = lax.broadcasted_iota(jnp.int32, (SUBLANES, tm), 0)

        def top1(x):
            m = jnp.max(x, axis=0, keepdims=True)
            idx = jnp.min(jnp.where(x == m, sub, SUBLANES), axis=0, keepdims=True)
            return m, idx

        grp = lt[0:SUBLANES, :]
        g_max, g_idx = top1(grp)
        g_p = 1.0 / jnp.sum(jnp.exp(grp - g_max), axis=0, keepdims=True)
        fine = lt[SUBLANES:2 * SUBLANES, :]
        for g in range(1, N_GROUPS):
            fine = jnp.where(g_idx == g, lt[(g + 1) * SUBLANES:(g + 2) * SUBLANES, :], fine)
        m1, i1 = top1(fine)
        m2, i2 = top1(jnp.where(sub == i1, -jnp.inf, fine))
        e2 = jnp.exp(m2 - m1)
        p1 = 1.0 / (1.0 + e2)
        w1 = g_p * p1
        w2 = g_p * (e2 * p1)
        x1 = g_idx * EXPERTS_PER_GROUP + i1
        x2 = g_idx * EXPERTS_PER_GROUP + i2

        eid = lax.broadcasted_iota(jnp.int32, (N_EXPERTS, tm), 0)
        oh1 = jnp.where(eid == x1, 1.0, 0.0)
        oh2 = jnp.where(eid == x2, 1.0, 0.0)
        pre = jnp.dot(jnp.concatenate([oh1, oh2], axis=0).astype(BF16), before_ref[...],
                      preferred_element_type=F32)
        pre1, pre2 = pre[0:N_EXPERTS], pre[N_EXPERTS:2 * N_EXPERTS]
        cnt1 = jnp.sum(oh1, axis=1, keepdims=True)
        cnt2 = jnp.sum(oh2, axis=1, keepdims=True)
        seg8 = jnp.floor((cnt1 + cnt2 + (SUBLANES - 1.0)) * (1.0 / SUBLANES))
        e_r = lax.broadcasted_iota(jnp.int32, (N_EXPERTS, N_EXPERTS), 0)
        e_c = lax.broadcasted_iota(jnp.int32, (N_EXPERTS, N_EXPERTS), 1)
        lower = jnp.where(e_c < e_r, 1.0, 0.0).astype(BF16)
        seg8_b = jnp.broadcast_to(seg8, (N_EXPERTS, LANES)).astype(BF16)
        seg_off = SUBLANES * jnp.dot(lower, seg8_b, preferred_element_type=F32)[:, 0:1]
        pos1 = jnp.sum(oh1 * (pre1 + seg_off), axis=0, keepdims=True)
        pos2 = jnp.sum(oh2 * (pre2 + (seg_off + cnt1)), axis=0, keepdims=True)

        lane = lax.broadcasted_iota(jnp.int32, tc_ref.shape, 1)
        seg_rows = jnp.broadcast_to(seg8 * SUBLANES, tc_ref.shape).astype(jnp.int32)
        tile = (step - 1) * n_sub + j
        tc_ref[...] = jnp.where(lane == tile, seg_rows, tc_ref[...])

        zrow = jnp.zeros((SUBLANES - 4, tm), jnp.int32)
        rr_ref[:, rows] = jnp.concatenate(
            [pos1.astype(jnp.int32), pos2.astype(jnp.int32), x1, x2, zrow], axis=0)
        wt = jnp.concatenate([w1, w2, pos1, pos2, jnp.zeros((LANES - 4, tm), F32)], axis=0)
        rwt_ref[rows, :] = wt.T

    m_sb = _rms_f32(sb_ref[...], sbg_ref[...]).astype(BF16)
    lru = jnp.concatenate([r[...] for r in lru_refs], axis=-1)
    m_lru = _rms_f32(lru, lrug_ref[...]).astype(BF16)
    h_ref[...] = (x_ref[...]
                  + jnp.dot(m_sb, wo_ref[0:half, :], preferred_element_type=F32)
                  + jnp.dot(m_lru, wo_ref[half:2 * half, :], preferred_element_type=F32))

    for j in range(n_sub):
        route(j)

    u2_next = _rms_f32(h_ref[...], n2g_ref[...])
    u2_ref[...] = u2_next.astype(BF16)
    u2s_ref[...] = u2_next


def _mix_route(sb, lru, x2, sbg, lrug, w_out, n2g, wr_stack, rbias, tm, tp):
    n, d = x2.shape
    half = sb.shape[1]
    n_tiles = n // tp
    row = lambda i: (jnp.minimum(i, n_tiles - 1), 0)
    routed = lambda i: (jnp.maximum(i - 1, 0), 0)
    const = lambda i: (0, 0)
    return pl.pallas_call(
        functools.partial(_mix_route_kernel, tm=tm, n_slab=len(lru)),
        grid=(n_tiles + 1,),
        in_specs=[pl.BlockSpec((tp, half), row)] + [pl.BlockSpec((tp, LANES), row)] * len(lru)
                 + [pl.BlockSpec((tp, d), row),
                  pl.BlockSpec((1, half), const), pl.BlockSpec((1, half), const),
                  pl.BlockSpec(w_out.shape, const, pipeline_mode=pl.Buffered(1)),
                  pl.BlockSpec((1, d), const),
                  pl.BlockSpec(wr_stack.shape, const), pl.BlockSpec(rbias.shape, const)],
        out_specs=[pl.BlockSpec((tp, d), row), pl.BlockSpec((tp, d), row),
                   pl.BlockSpec((SUBLANES, tp), lambda i: (0, jnp.maximum(i - 1, 0))),
                   pl.BlockSpec((tp, LANES), routed),
                   pl.BlockSpec((N_EXPERTS, LANES), const)],
        out_shape=[jax.ShapeDtypeStruct((n, d), F32), jax.ShapeDtypeStruct((n, d), BF16),
                   jax.ShapeDtypeStruct((SUBLANES, n), jnp.int32),
                   jax.ShapeDtypeStruct((n, LANES), F32),
                   jax.ShapeDtypeStruct((N_EXPERTS, LANES), jnp.int32)],
        scratch_shapes=[pltpu.VMEM((tm, tm), BF16), pltpu.VMEM((tp, d), F32),
                        pltpu.VMEM(w_out.shape, BF16)],
        compiler_params=_cparams(("arbitrary",)),
        name="mix_route",
    )(sb, *lru, x2, sbg, lrug, w_out, n2g, wr_stack, rbias)


def _pack_halves(x):
    half = x.shape[1] // 2
    lo = lax.shift_right_logical(lax.bitcast_convert_type(x[:, :half], jnp.int32), 16)
    hi = lax.bitcast_convert_type(x[:, half:], jnp.int32) & HIGH_HALF
    return hi | lo


def _unpack_halves(p):
    lo = lax.bitcast_convert_type(lax.shift_left(p, 16), F32)
    hi = lax.bitcast_convert_type(p & HIGH_HALF, F32)
    return lo.astype(BF16), hi.astype(BF16)


def _segment_copies(tile, c8_ref, loff_ref, goff_ref, make):
    for e in range(N_EXPERTS):
        idx = tile * N_EXPERTS + e
        rows = pl.multiple_of(c8_ref[idx], SUBLANES)

        @pl.when(rows > 0)
        def _(idx=idx, rows=rows):
            lo = pl.multiple_of(loff_ref[idx], SUBLANES)
            go = pl.multiple_of(goff_ref[idx], SUBLANES)
            make(pl.ds(lo, rows), pl.ds(go, rows)).start()


def _dispatch_kernel(c8_ref, loff_ref, goff_ref, tot_ref, used_ref,
                     rr_ref, u2_ref, xs_ref, lbuf, zbuf, sem, zsem, *, td, lrows):
    i = pl.program_id(0)
    slot = i % 2

    r_id = lax.broadcasted_iota(jnp.int32, (lrows, td), 0)
    perm = jnp.where(r_id == rr_ref[0:1, :], 1.0, jnp.where(r_id == rr_ref[1:2, :], 1.0, 0.0))
    sorted_rows = jnp.dot(perm.astype(BF16), u2_ref[...], preferred_element_type=F32)
    lbuf[slot] = _pack_halves(sorted_rows)

    @pl.when(i == 0)
    def _():
        zbuf[...] = jnp.zeros_like(zbuf)
        chunk = zbuf.shape[0]
        used = used_ref[0]
        spare = xs_ref.shape[0] - used
        n_fill = (spare + chunk - 1) // chunk

        def fill_copy(k):
            rows = pl.multiple_of(jnp.minimum(chunk, spare - k * chunk), SUBLANES)
            start = pl.multiple_of(used + k * chunk, SUBLANES)
            return pltpu.make_async_copy(zbuf.at[pl.ds(0, rows)], xs_ref.at[pl.ds(start, rows)],
                                         zsem)

        def fill_start(k, c):
            fill_copy(k).start()
            return c

        def fill_wait(k, c):
            fill_copy(k).wait()
            return c

        lax.fori_loop(0, n_fill, fill_start, 0)
        lax.fori_loop(0, n_fill, fill_wait, 0)

    def wait_tile(tile, s):
        rows = pl.multiple_of(tot_ref[tile], SUBLANES)
        pltpu.make_async_copy(lbuf.at[s, pl.ds(0, rows)], xs_ref.at[pl.ds(0, rows)], sem).wait()

    @pl.when(i > 0)
    def _():
        wait_tile(i - 1, 1 - slot)

    _segment_copies(i, c8_ref, loff_ref, goff_ref,
                    lambda loc, glob: pltpu.make_async_copy(lbuf.at[slot, loc], xs_ref.at[glob], sem))

    @pl.when(i == pl.num_programs(0) - 1)
    def _():
        wait_tile(i, slot)


def _dispatch(c8, loff, goff, tot, used, rr, u2, p_rows, td, lrows, tme):
    n, d = u2.shape
    pmap = lambda i, *_: (0, i)
    return pl.pallas_call(
        functools.partial(_dispatch_kernel, td=td, lrows=lrows),
        grid_spec=pltpu.PrefetchScalarGridSpec(
            num_scalar_prefetch=5,
            grid=(n // td,),
            in_specs=[pl.BlockSpec((SUBLANES, td), pmap),
                      pl.BlockSpec((td, d), lambda i, *_: (i, 0))],
            out_specs=pl.BlockSpec(memory_space=pl.ANY),
            scratch_shapes=[pltpu.VMEM((2, lrows, d // 2), jnp.int32),
                            pltpu.VMEM((tme, d // 2), jnp.int32),
                            pltpu.SemaphoreType.DMA(()), pltpu.SemaphoreType.DMA(())]),
        out_shape=jax.ShapeDtypeStruct((p_rows, d // 2), jnp.int32),
        compiler_params=_cparams(("arbitrary",)),
        name="dispatch",
    )(c8, loff, goff, tot, used, rr, u2)


def _experts_kernel(eoff_ref, erows_ref, xs_ref, wg_ref, wu_ref, wd_ref, ys_ref,
                    wg_bf, wu_bf, wd_bf, xbuf, ybuf, sem_in, sem_out, state, *, tme):
    e = pl.program_id(0)
    n_experts = pl.num_programs(0)
    rows = erows_ref[e]
    off = eoff_ref[e]
    n_tiles = (rows + tme - 1) // tme
    nxt = jnp.minimum(e + 1, n_experts - 1)
    prefetch = (rows > 0) & (e + 1 < n_experts) & (erows_ref[nxt] > 0)

    def tile_rows(total, k):
        return pl.multiple_of(jnp.minimum(tme, total - k * tme), SUBLANES)

    def in_copy(start, r, slot):
        start = pl.multiple_of(start, SUBLANES)
        return pltpu.make_async_copy(xs_ref.at[pl.ds(start, r)], xbuf.at[slot, pl.ds(0, r)],
                                     sem_in.at[slot])

    def out_copy(start, r, slot):
        start = pl.multiple_of(start, SUBLANES)
        return pltpu.make_async_copy(ybuf.at[slot, pl.ds(0, r)], ys_ref.at[pl.ds(start, r)],
                                     sem_out.at[slot])

    @pl.when(e == 0)
    def _():
        for s in range(4):
            state[s] = 0
        xbuf[...] = jnp.zeros_like(xbuf)

    @pl.when(rows > 0)
    def _():
        wg_bf[...] = wg_ref[0].astype(BF16)
        wu_bf[...] = wu_ref[0].astype(BF16)
        wd_bf[...] = wd_ref[0].astype(BF16)
        done = state[0]

        @pl.when(state[1] == 0)
        def _():
            in_copy(off, tile_rows(rows, 0), done % 2).start()

        def tile(k, c):
            slot = (done + k) % 2
            r = tile_rows(rows, k)

            @pl.when(k + 1 < n_tiles)
            def _():
                in_copy(off + (k + 1) * tme, tile_rows(rows, k + 1), 1 - slot).start()

            @pl.when((k + 1 == n_tiles) & prefetch)
            def _():
                in_copy(eoff_ref[nxt], tile_rows(erows_ref[nxt], 0), 1 - slot).start()

            in_copy(off, r, slot).wait()

            @pl.when(done + k >= 2)
            def _():
                out_copy(0, pl.multiple_of(state[2 + slot], SUBLANES), slot).wait()

            def mlp(start, n):
                x = jnp.concatenate(_unpack_halves(xbuf[slot, start:start + n]), axis=1)
                hg = jnp.dot(x, wg_bf[...], preferred_element_type=F32)
                hu = jnp.dot(x, wu_bf[...], preferred_element_type=F32)
                act = (hg * jax.nn.sigmoid(hg) * hu).astype(BF16)
                y = jnp.dot(act, wd_bf[...], preferred_element_type=F32)
                ybuf[slot, start:start + n] = _pack_halves(y.astype(BF16).astype(F32))

            for below, n in zip((0,) + EXPERT_TILE_SIZES, EXPERT_TILE_SIZES):
                @pl.when((r > below) & (r <= n))
                def _(n=n):
                    mlp(0, n)

            out_copy(off + k * tme, r, slot).start()
            state[2 + slot] = r
            return c

        lax.fori_loop(0, n_tiles, tile, 0)
        state[0] = done + n_tiles

    state[1] = prefetch.astype(jnp.int32)

    @pl.when(e == n_experts - 1)
    def _():
        total = eoff_ref[e] + rows
        for slot in range(2):
            @pl.when(state[0] > slot)
            def _(slot=slot):
                out_copy(0, pl.multiple_of(state[2 + slot], SUBLANES), slot).wait()
        ybuf[0] = jnp.zeros(ybuf.shape[1:], ybuf.dtype)
        spare = ys_ref.shape[0] - total
        n_fill = (spare + tme - 1) // tme

        def fill_copy(k):
            return out_copy(total + k * tme, tile_rows(spare, k), 0)

        def fill_start(k, c):
            fill_copy(k).start()
            return c

        def fill_wait(k, c):
            fill_copy(k).wait()
            return c

        lax.fori_loop(0, n_fill, fill_start, 0)
        lax.fori_loop(0, n_fill, fill_wait, 0)


def _experts(eoff, erows, xs, wg, wu, wd, tme):
    p = xs.shape[0]
    n_experts, d, de = wg.shape
    wmap = lambda e, *_: (e, 0, 0)
    return pl.pallas_call(
        functools.partial(_experts_kernel, tme=tme),
        grid_spec=pltpu.PrefetchScalarGridSpec(
            num_scalar_prefetch=2,
            grid=(n_experts,),
            in_specs=[pl.BlockSpec(memory_space=pl.ANY),
                      pl.BlockSpec((1, d, de), wmap), pl.BlockSpec((1, d, de), wmap),
                      pl.BlockSpec((1, de, d), wmap)],
            out_specs=pl.BlockSpec(memory_space=pl.ANY),
            scratch_shapes=[pltpu.VMEM((d, de), BF16), pltpu.VMEM((d, de), BF16),
                            pltpu.VMEM((de, d), BF16),
                            pltpu.VMEM((2, tme, d // 2), jnp.int32),
                            pltpu.VMEM((2, tme, d // 2), jnp.int32),
                            pltpu.SemaphoreType.DMA((2,)), pltpu.SemaphoreType.DMA((2,)),
                            pltpu.SMEM((4,), jnp.int32)]),
        out_shape=jax.ShapeDtypeStruct((p, d // 2), jnp.int32),
        compiler_params=_cparams(("arbitrary",)),
        name="experts",
    )(eoff, erows, xs, wg, wu, wd)


def _combine_kernel(c8_ref, loff_ref, goff_ref, tot_ref,
                    rwt_ref, nxt_ref, h_ref, fg_ref, ys_ref, y_ref, ybuf, sems, spread_ref,
                    *, tc, lrows):
    i = pl.program_id(0)
    slot = i % 2

    def gather_tile(tile, s):
        _segment_copies(tile, c8_ref, loff_ref, goff_ref,
                        lambda loc, glob: pltpu.make_async_copy(ys_ref.at[glob], ybuf.at[s, loc],
                                                                sems.at[s]))

    def spread(ref):
        w = ref[...]
        for c in range(4):
            spread_ref[c] = jnp.broadcast_to(w[:, c:c + 1], (tc, LANES))

    @pl.when(i == 0)
    def _():
        ybuf[...] = jnp.zeros_like(ybuf)
        gather_tile(0, 0)
        spread(rwt_ref)

    @pl.when(i + 1 < pl.num_programs(0))
    def _():
        gather_tile(i + 1, 1 - slot)

    rows = pl.multiple_of(tot_ref[i], SUBLANES)
    pltpu.make_async_copy(ys_ref.at[pl.ds(0, rows)], ybuf.at[slot, pl.ds(0, rows)],
                          sems.at[slot]).wait()

    across = lambda c: jnp.tile(spread_ref[c], (1, lrows // LANES))
    c_id = lax.broadcasted_iota(jnp.int32, (tc, lrows), 1)
    pos1 = across(2).astype(jnp.int32)
    pos2 = across(3).astype(jnp.int32)
    wmat = jnp.where(c_id == pos1, across(0), jnp.where(c_id == pos2, across(1), 0.0)).astype(BF16)
    half = h_ref.shape[1] // 2
    outs, sumsq = [], 0.0
    for part, cols in zip(_unpack_halves(ybuf[slot]), (slice(0, half), slice(half, 2 * half))):
        out = h_ref[:, cols] + jnp.dot(wmat, part, preferred_element_type=F32)
        if not outs:
            spread(nxt_ref)
        sumsq = sumsq + jnp.sum(out * out, axis=-1, keepdims=True)
        outs.append(out)
    scale = lax.rsqrt(sumsq * (1.0 / (2 * half)) + EPS)
    for out, cols in zip(outs, (slice(0, half), slice(half, 2 * half))):
        y_ref[:, cols] = out * scale * fg_ref[:, cols]


def _combine(c8, loff, goff, tot, rwt, h, final_g, ys, tc, lrows):
    n, d = h.shape
    n_tiles = n // tc
    return pl.pallas_call(
        functools.partial(_combine_kernel, tc=tc, lrows=lrows),
        grid_spec=pltpu.PrefetchScalarGridSpec(
            num_scalar_prefetch=4,
            grid=(n_tiles,),
            in_specs=[pl.BlockSpec((tc, LANES), lambda i, *_: (0, 0)),
                      pl.BlockSpec((tc, LANES), lambda i, *_: (jnp.minimum(i + 1, n_tiles - 1), 0)),
                      pl.BlockSpec((tc, d), lambda i, *_: (i, 0)),
                      pl.BlockSpec((1, d), lambda i, *_: (0, 0)),
                      pl.BlockSpec(memory_space=pl.ANY)],
            out_specs=pl.BlockSpec((tc, d), lambda i, *_: (i, 0)),
            scratch_shapes=[pltpu.VMEM((2, lrows, d // 2), jnp.int32),
                            pltpu.SemaphoreType.DMA((2,)),
                            pltpu.VMEM((4, tc, LANES), F32)]),
        out_shape=jax.ShapeDtypeStruct((n, d), F32),
        compiler_params=_cparams(("arbitrary",)),
        name="combine",
    )(c8, loff, goff, tot, rwt, rwt, h, final_g, ys)


def _router_tables(w_group, b_group, w_fine, b_fine):
    d = w_group.shape[0]
    pad_g, pad_f = SUBLANES - N_GROUPS, ROUTER_ROWS - SUBLANES - N_EXPERTS
    w = jnp.concatenate([w_group.T, jnp.zeros((pad_g, d), F32), w_fine.T,
                         jnp.zeros((pad_f, d), F32)], axis=0)
    b = jnp.concatenate([b_group, jnp.full((pad_g,), NEG_BIG, F32), b_fine,
                         jnp.full((pad_f,), NEG_BIG, F32)])
    w_hi = w.astype(BF16)
    w_lo = (w - w_hi.astype(F32)).astype(BF16)
    return jnp.concatenate([w_hi, w_lo], axis=0), b.reshape(ROUTER_ROWS, 1)


def kernel(x, norm1_g, w_in, conv_w, conv_b, w_rgate, b_rgate, w_igate, b_igate, lam, sb_norm_g,
           lru_norm_g, w_out, norm2_g, w_group, b_group, w_fine, b_fine, w_e_gate, w_e_up,
           w_e_down, final_g):
    batch, seq, d = x.shape
    n = batch * seq
    width = w_in.shape[1] // 5
    tm = min(TOKEN_TILE, seq)
    tp = min(ROW_TILE, seq)
    tme = EXPERT_TILE_ROWS

    x2 = x.reshape(n, d)
    vec = lambda a: a.reshape(1, -1)

    q, k, v, xl, gl = _in_proj(x2, vec(norm1_g), w_in, width, tp)
    out_sb = _attention(q, k, v, batch, seq)
    out_lru = _lru(xl, gl, conv_w, vec(conv_b), w_rgate, vec(b_rgate), w_igate, vec(b_igate),
                   vec(lam), batch, seq, tp)

    wr_stack, rbias = _router_tables(w_group, b_group, w_fine, b_fine)
    h, u2, rr, rwt, tcnt = _mix_route(out_sb, out_lru, x2, vec(sb_norm_g), vec(lru_norm_g),
                                      w_out, vec(norm2_g), wr_stack, rbias, tm, tp)

    n_tiles = n // tm
    assert n_tiles <= LANES, "one lane of the per-tile count table per token tile"
    c8 = tcnt[:, :n_tiles].T
    erows = jnp.sum(c8, axis=0)
    eoff = jnp.cumsum(erows) - erows
    goff = eoff[None, :] + jnp.cumsum(c8, axis=0) - c8
    loff = jnp.cumsum(c8, axis=1) - c8
    tot = jnp.sum(c8, axis=1)
    lrows = 2 * tm + N_EXPERTS * SUBLANES
    p_rows = 2 * n + n_tiles * N_EXPERTS * (SUBLANES - 1)
    p_rows = -(-p_rows // SUBLANES) * SUBLANES
    i32 = lambda a: a.reshape(-1).astype(jnp.int32)
    c8, loff, goff, tot, eoff, erows = (i32(a) for a in (c8, loff, goff, tot, eoff, erows))

    xs = _dispatch(c8, loff, goff, tot, jnp.sum(erows, keepdims=True), rr, u2, p_rows, tm, lrows, tme)
    ys = _experts(eoff, erows, xs, w_e_gate, w_e_up, w_e_down, tme)
    y = _combine(c8, loff, goff, tot, rwt, h, vec(final_g), ys, tm, lrows)
    return y.reshape(batch, seq, d)
```

```python
import functools
import math

import jax
import jax.numpy as jnp
from jax import lax
from jax.experimental import pallas as pl
from jax.experimental.pallas import tpu as pltpu

F32 = jnp.float32
BF16 = jnp.bfloat16

EPS = 1e-6
HEAD_DIM = 64
HEADS_PER_BLOCK = 2
LANES = 128
SUBLANES = 8
CONV_W = 4
RG_C = 8.0
N_GROUPS = 4
EXPERTS_PER_GROUP = 8
N_EXPERTS = N_GROUPS * EXPERTS_PER_GROUP
ROUTER_ROWS = 48
NEG_BIG = -1e30
LOG2_E = math.log2(math.e)
ATTN_STOP = 104.0 * LOG2_E
ATTN_QUERY_ROWS = 64
ATTN_WINDOW_BLOCKS = 2
ATTN_UNROLL = 24
ATTN_STAGE_LAG = 2
HIGH_HALF = -65536
COMBINE_RING = 3
EXPERT_TILE_ROWS = 1536
EXPERT_TILE_SIZES = (256, 512, 1024, 1152, 1280, 1536)
TOKEN_TILE = 512
ROW_TILE = 1024

VMEM_LIMIT = 56 * 1024 * 1024


def _cparams(sem):
    return pltpu.CompilerParams(dimension_semantics=sem, vmem_limit_bytes=VMEM_LIMIT)


def _rms_f32(x, g):
    return x * lax.rsqrt(jnp.mean(x * x, axis=-1, keepdims=True) + EPS) * g


def _in_proj_kernel(x_ref, g_ref, w_ref, q_ref, k_ref, v_ref, xl_ref, gl_ref, w_bf, *,
                    width, q_scale):
    @pl.when(pl.program_id(0) == 0)
    def _():
        for c in range(w_ref.shape[1] // width):
            cols = slice(c * width, (c + 1) * width)
            w_bf[:, cols] = w_ref[:, cols].astype(BF16)

    u = _rms_f32(x_ref[...], g_ref[...]).astype(BF16)
    for c, o_ref in enumerate((q_ref, k_ref, v_ref, xl_ref, gl_ref)):
        p = jnp.dot(u, w_bf[:, c * width:(c + 1) * width], preferred_element_type=F32)
        if c == 0:
            p = p * q_scale
        o_ref[...] = p.astype(o_ref.dtype)


def _in_proj(x2, g, w, width, tm):
    n, d = x2.shape
    row = lambda i: (i, 0)
    out_bf = jax.ShapeDtypeStruct((n, width), BF16)
    out_f = jax.ShapeDtypeStruct((n, width), F32)
    return pl.pallas_call(
        functools.partial(_in_proj_kernel, width=width, q_scale=1.0 / math.sqrt(HEAD_DIM)),
        grid=(n // tm,),
        in_specs=[pl.BlockSpec((tm, d), row),
                  pl.BlockSpec((1, d), lambda i: (0, 0)),
                  pl.BlockSpec(w.shape, lambda i: (0, 0), pipeline_mode=pl.Buffered(1))],
        out_specs=[pl.BlockSpec((tm, width), row)] * 5,
        out_shape=[out_bf, out_bf, out_bf, out_f, out_f],
        scratch_shapes=[pltpu.VMEM(w.shape, BF16)],
        compiler_params=_cparams(("arbitrary",)),
        name="in_proj",
    )(x2, g, w)


def _attn_kernel(q_ref, k_ref, v_ref, o_ref, tri_ref, z_ref, arg_ref, ctot_ref, acc_ref, carry_ref,
                 *, tq, kb, nsub, fill):
    seq = q_ref.shape[0]
    win = nsub * kb
    lookback = win - tq
    lane = lax.broadcasted_iota(jnp.int32, (1, LANES), 1)
    rel = (lax.broadcasted_iota(jnp.int32, (tq, kb), 1)
           - lax.broadcasted_iota(jnp.int32, (tq, kb), 0))
    rel = jnp.concatenate([rel] * HEADS_PER_BLOCK, axis=0)

    k_r = lax.broadcasted_iota(jnp.int32, (kb, 2 * kb), 0)
    k_c = lax.broadcasted_iota(jnp.int32, (kb, 2 * kb), 1)
    tri_ref[...] = jnp.where(k_c >= kb, 1.0, jnp.where(k_r > k_c, 1.0, 0.0)).astype(BF16)

    def softplus2(z):
        return jnp.maximum(z, 0.0) + jnp.log2(1.0 + jnp.exp2(-jnp.abs(z)))

    def scores(qh, keys):
        z = LOG2_E * lax.dot_general(qh, keys, (((1,), (1,)), ((), ())),
                                     preferred_element_type=F32)
        nlog_nb = softplus2(z)
        return nlog_nb, z - nlog_nb

    def suffix(nlog_nb):
        r = jnp.dot(nlog_nb.astype(BF16), tri_ref[...], preferred_element_type=F32)
        return r[:, :kb], r[:, kb:]

    def stacked_queries(i):
        q = q_ref[pl.ds(i * tq, tq), :]
        return jnp.concatenate(
            [jnp.where((lane >= h * HEAD_DIM) & (lane < (h + 1) * HEAD_DIM), q, jnp.zeros_like(q))
             for h in range(HEADS_PER_BLOCK)], axis=0)

    def store(i, out):
        o_ref[pl.ds(i * tq, tq), :] = jnp.where(lane < HEAD_DIM, out[0:tq], out[tq:2 * tq])

    def window_start(i):
        if isinstance(i, int):
            return max(i * tq - lookback, 0)
        return pl.multiple_of(i * tq - lookback, tq)

    def stage_scores(i, p):
        keys = k_ref[pl.ds(window_start(i), win), :]
        z_ref[p] = LOG2_E * lax.dot_general(stacked_queries(i), keys, (((1,), (1,)), ((), ())),
                                            preferred_element_type=F32)

    def stage_exponents(p, delta):
        z = z_ref[p]
        softplus = softplus2(z)
        carry = None
        for b in reversed(range(nsub)):
            cols = slice(b * kb, (b + 1) * kb)
            masked = (b + 1) * kb > delta
            valid = (rel + b * kb) < delta
            nl = softplus[:, cols]
            if masked:
                nl = jnp.where(valid, nl, 0.0)
            excl, tot = suffix(nl)
            arg = z[:, cols] - softplus[:, cols] - excl
            if carry is not None:
                arg = arg - carry
            if masked:
                arg = jnp.where(valid, arg, NEG_BIG)
            arg_ref[p, :, cols] = arg
            carry = tot if carry is None else carry + tot
        ctot_ref[p] = carry
        return jnp.min(carry)

    def stage_output(i, p, s):
        vals = v_ref[pl.ds(window_start(i), win), :]
        acc_ref[s] = jnp.dot(jnp.exp2(arg_ref[p]).astype(BF16), vals, preferred_element_type=F32)
        carry_ref[s] = ctot_ref[p]

    def finish(i, s, cmin):
        def cond(state):
            pos, cmin = state
            return (pos > -kb) & (cmin <= ATTN_STOP)

        def older(state):
            pos, _ = state
            start = pl.multiple_of(jnp.maximum(pos, 0), tq)
            keys = k_ref[pl.ds(start, kb), :]
            vals = v_ref[pl.ds(start, kb), :]
            fresh = lax.broadcasted_iota(jnp.int32, (1, kb), 1) < pos + kb - start
            nlog_nb, log_b = scores(stacked_queries(i), keys)
            nlog_nb = jnp.where(fresh, nlog_nb, 0.0)
            excl, tot = suffix(nlog_nb)
            carry = carry_ref[s]
            a = jnp.where(fresh, jnp.exp2(log_b - excl - carry), 0.0)
            acc_ref[s] += jnp.dot(a.astype(BF16), vals, preferred_element_type=F32)
            carry_ref[s] = carry + tot
            return pos - kb, jnp.min(carry + tot)

        lax.while_loop(cond, older, (jnp.asarray(window_start(i) - kb, jnp.int32), cmin))
        store(i, acc_ref[s])

    n_blocks = seq // tq
    unroll = acc_ref.shape[0]
    lag = z_ref.shape[0] - 1
    n_slots = lag + 1
    pending = []
    for tau in range(fill):
        stage_scores(tau, tau % n_slots)
        if tau >= lag:
            j = tau - lag
            pending.append(stage_exponents(j % n_slots, j * tq - window_start(j)))
        if tau >= 2 * lag:
            j = tau - 2 * lag
            stage_output(j, j % n_slots, 0)
            finish(j, 0, pending.pop(0))

    def steady(m, pending):
        pending = list(pending)
        tau0 = fill + unroll * m
        done = []
        for u in range(unroll):
            stage_output(tau0 + u - 2 * lag, (fill - 2 * lag + u) % n_slots, u)
            done.append((tau0 + u - 2 * lag, u, pending.pop(0)))
            pending.append(stage_exponents((fill - lag + u) % n_slots, lookback))
            stage_scores(tau0 + u, (fill + u) % n_slots)
        for block, s, cmin in done:
            finish(block, s, cmin)
        return tuple(pending)

    pending = list(lax.fori_loop(0, (n_blocks - fill) // unroll, steady, tuple(pending)))
    for tau in range(n_blocks, n_blocks + 2 * lag):
        j = tau - 2 * lag
        stage_output(j, j % n_slots, 0)
        cmin = pending.pop(0)
        if tau - lag < n_blocks:
            pending.append(stage_exponents((tau - lag) % n_slots, lookback))
        finish(j, 0, cmin)


def _attention(q, k, v, batch, seq):
    n, width = q.shape
    tq, kb, nsub = ATTN_QUERY_ROWS, LANES, ATTN_WINDOW_BLOCKS
    n_blocks = seq // tq
    lag, n_slots = ATTN_STAGE_LAG, ATTN_STAGE_LAG + 1
    clipped = -(-(nsub * kb - tq) // tq)
    min_fill = clipped + 2 * lag
    assert seq % tq == 0 and seq >= nsub * kb and n_blocks >= min_fill
    options = [(f + 2 * (n_blocks - f) // u, u, f)
               for u in range(ATTN_UNROLL - ATTN_UNROLL % n_slots, 0, -n_slots)
               for f in range(min_fill, n_blocks + 1) if (n_blocks - f) % u == 0]
    _, unroll, fill = min(options)
    blk = pl.BlockSpec((seq, LANES), lambda b, hp: (b, hp))
    rows = HEADS_PER_BLOCK * tq
    stage_buf = pltpu.VMEM((n_slots, rows, nsub * kb), F32)
    carry_buf = pltpu.VMEM((n_slots, rows, kb), F32)
    row_buf = pltpu.VMEM((unroll, rows, LANES), F32)
    return pl.pallas_call(
        functools.partial(_attn_kernel, tq=tq, kb=kb, nsub=nsub, fill=fill),
        grid=(batch, width // LANES),
        in_specs=[blk, blk, blk],
        out_specs=blk,
        out_shape=jax.ShapeDtypeStruct((n, width), F32),
        scratch_shapes=[pltpu.VMEM((kb, 2 * kb), BF16),
                        stage_buf, stage_buf, carry_buf, row_buf, row_buf],
        compiler_params=_cparams(("arbitrary", "arbitrary")),
        name="attn",
    )(q, k, v)


def _gelu_tanh(x):
    c = math.sqrt(2.0 / math.pi)
    half_x = 0.5 * x
    return half_x + half_x * jnp.tanh(x * (c + (c * 0.044715) * (x * x)))


def _sigmoid(x):
    return 0.5 + 0.5 * jnp.tanh(0.5 * x)


def _lru_kernel(*refs, ts, n_slab):
    xl_refs, gl_refs = refs[0:n_slab], refs[n_slab:2 * n_slab]
    cw_ref, cb_ref, wr_ref, br_ref, wi_ref, bi_ref, lam_ref = refs[2 * n_slab:2 * n_slab + 7]
    o_refs = refs[2 * n_slab + 7:3 * n_slab + 7]
    (tail_ref, a7_ref, u7_ref, hp_ref, pa_ref, pu_ref, h_ref,
     wr_bd, wi_bd) = refs[3 * n_slab + 7:]
    t = pl.program_id(1)
    groups = ts // SUBLANES

    @pl.when(t == 0)
    def _():
        tail_ref[...] = jnp.zeros_like(tail_ref)
        h_ref[...] = jnp.zeros_like(h_ref)
        per = LANES // wr_ref.shape[1]
        for src, dst in ((wr_ref, wr_bd), (wi_ref, wi_bd)):
            for c in range(n_slab):
                rows = []
                for p in range(per):
                    blk = src[c * per + p]
                    rows.append(jnp.concatenate(
                        [blk if q == p else jnp.zeros_like(blk) for q in range(per)], axis=1))
                dst[c] = jnp.concatenate(rows, axis=0).astype(BF16)

    first_group = lax.broadcasted_iota(jnp.int32, (groups, LANES), 0) == 0
    for c in range(n_slab):
        lanes = slice(c * LANES, (c + 1) * LANES)
        x = [xl_refs[c][pl.ds(s, groups, stride=SUBLANES), :] for s in range(SUBLANES)]
        shifted = {}
        for s in range(SUBLANES - (CONV_W - 1), SUBLANES):
            shifted[s] = jnp.where(first_group, tail_ref[c, s:s + 1, :], pltpu.roll(x[s], 1, axis=0))
            tail_ref[c, s:s + 1, :] = x[s][groups - 1:groups, :]
        conv = []
        for s in range(SUBLANES):
            y = cb_ref[:, lanes]
            for w in range(CONV_W):
                j = s - (CONV_W - 1) + w
                y = y + (x[j] if j >= 0 else shifted[j + SUBLANES]) * cw_ref[w:w + 1, lanes]
            conv.append(y)
        xc = jnp.concatenate(conv, axis=0)

        xcb = xc.astype(BF16)
        r = _sigmoid(jnp.dot(xcb, wr_bd[c], preferred_element_type=F32) + br_ref[:, lanes])
        ig = _sigmoid(jnp.dot(xcb, wi_bd[c], preferred_element_type=F32) + bi_ref[:, lanes])
        lam = lam_ref[:, lanes]
        log_sig_lam = -(jnp.maximum(-lam, 0.0) + jnp.log1p(jnp.exp(-jnp.abs(lam))))
        log_a = r * (RG_C * log_sig_lam)
        a = jnp.exp(log_a)
        th = jnp.tanh(log_a)
        one_m_a2 = -2.0 * th / (1.0 - th)
        root = jnp.where(one_m_a2 > 0.0, one_m_a2 * lax.rsqrt(one_m_a2), 0.0)
        u = root * (ig * xc)

        a_run = u_run = None
        for s in range(SUBLANES):
            rows = slice(s * groups, (s + 1) * groups)
            if s == 0:
                a_run, u_run = a[rows], u[rows]
            else:
                u_run = a[rows] * u_run + u[rows]
                a_run = a_run * a[rows]
            pa_ref[c, rows, :] = a_run
            pu_ref[c, rows, :] = u_run
        a7_ref[c] = a_run
        u7_ref[c] = u_run

    def group(g, hs):
        nxt = []
        for c in range(n_slab):
            hp_ref[c, pl.ds(g, 1), :] = hs[c]
            nxt.append(a7_ref[c, pl.ds(g, 1), :] * hs[c] + u7_ref[c, pl.ds(g, 1), :])
        return tuple(nxt)

    hs = lax.fori_loop(0, groups, group, tuple(h_ref[c] for c in range(n_slab)), unroll=8)
    for c in range(n_slab):
        h_ref[c] = hs[c]

    for c in range(n_slab):
        h_in = hp_ref[c]
        for s in range(SUBLANES):
            rows = slice(s * groups, (s + 1) * groups)
            hseq = pu_ref[c, rows, :] + pa_ref[c, rows, :] * h_in
            gate = _gelu_tanh(gl_refs[c][pl.ds(s, groups, stride=SUBLANES), :])
            o_refs[c][pl.ds(s, groups, stride=SUBLANES), :] = hseq * gate


def _lru(xl, gl, conv_w, conv_b, w_r, br, w_i, bi, lam, batch, seq, ts):
    n, width = xl.shape
    nt = seq // ts
    n_slab = width // LANES
    blocks, block_w, _ = w_r.shape
    assert blocks * block_w == width and LANES % block_w == 0 and ts % (SUBLANES * SUBLANES) == 0
    groups = ts // SUBLANES
    gate_bd = pltpu.VMEM((n_slab, LANES, LANES), BF16)
    slab = [pl.BlockSpec((ts, LANES), functools.partial(lambda b, t, c: (b * nt + t, c), c=c))
            for c in range(n_slab)]
    const2 = lambda b, t: (0, 0)
    const3 = lambda b, t: (0, 0, 0)
    vec = pl.BlockSpec((1, width), const2)
    per_group = pltpu.VMEM((n_slab, groups, LANES), F32)
    per_step = pltpu.VMEM((n_slab, ts, LANES), F32)
    return pl.pallas_call(
        functools.partial(_lru_kernel, ts=ts, n_slab=n_slab),
        grid=(batch, nt),
        in_specs=slab + slab + [pl.BlockSpec((CONV_W, width), const2), vec,
                                pl.BlockSpec(w_r.shape, const3), vec,
                                pl.BlockSpec(w_i.shape, const3), vec, vec],
        out_specs=[pl.BlockSpec((ts, LANES), lambda b, t: (b * nt + t, 0))] * n_slab,
        out_shape=[jax.ShapeDtypeStruct((n, LANES), F32)] * n_slab,
        scratch_shapes=[pltpu.VMEM((n_slab, SUBLANES, LANES), F32),
                        per_group, per_group, per_group, per_step, per_step,
                        pltpu.VMEM((n_slab, 1, LANES), F32), gate_bd, gate_bd],
        compiler_params=_cparams(("arbitrary", "arbitrary")),
        name="lru",
    )(*([xl] * n_slab), *([gl] * n_slab), conv_w, conv_b, w_r, br, w_i, bi, lam)


def _mix_route_kernel(sb_ref, *refs, tm, n_slab):
    lru_refs = refs[:n_slab]
    (x_ref, sbg_ref, lrug_ref, wo_ref, n2g_ref, wrs_ref, rb_ref,
     h_ref, u2_ref, rr_ref, rwt_ref, tc_ref, before_ref, u2s_ref, wo_bf) = refs[n_slab:]

    @pl.when(pl.program_id(0) == 0)
    def _():
        wo_bf[...] = wo_ref[...].astype(BF16)

    _mix_route_body(sb_ref, lru_refs, x_ref, sbg_ref, lrug_ref, wo_bf, n2g_ref, wrs_ref, rb_ref,
                    h_ref, u2_ref, rr_ref, rwt_ref, tc_ref, before_ref, u2s_ref, tm)


def _mix_route_body(sb_ref, lru_refs, x_ref, sbg_ref, lrug_ref, wo_ref, n2g_ref, wrs_ref, rb_ref,
                    h_ref, u2_ref, rr_ref, rwt_ref, tc_ref, before_ref, u2s_ref, tm):
    step = pl.program_id(0)
    half = sb_ref.shape[1]
    n_sub = x_ref.shape[0] // tm

    @pl.when(step == 0)
    def _():
        r_id = lax.broadcasted_iota(jnp.int32, (tm, tm), 0)
        c_id = lax.broadcasted_iota(jnp.int32, (tm, tm), 1)
        before_ref[...] = (r_id < c_id).astype(BF16)
        tc_ref[...] = jnp.zeros_like(tc_ref)
        u2s_ref[...] = jnp.zeros_like(u2s_ref)

    def route(j):
        rows = slice(j * tm, (j + 1) * tm)
        u2 = u2s_ref[rows, :]
        u_hi = u2.astype(BF16)
        u_lo = (u2 - u_hi.astype(F32)).astype(BF16)
        nt_dims = (((1,), (1,)), ((), ()))
        n_rows = rb_ref.shape[0]
        both = lax.dot_general(wrs_ref[...], u_hi, nt_dims, preferred_element_type=F32)
        lt = (both[0:n_rows] + both[n_rows:2 * n_rows]
              + lax.dot_general(wrs_ref[0:n_rows, :], u_lo, nt_dims, preferred_element_type=F32)
              + rb_ref[...])

        sub = lax.broadcasted_iota(jnp.int32, (SUBLANES, tm), 0)

        def top1(x):
            m = jnp.max(x, axis=0, keepdims=True)
            idx = jnp.min(jnp.where(x == m, sub, SUBLANES), axis=0, keepdims=True)
            return m, idx

        grp = lt[0:SUBLANES, :]
        g_max, g_idx = top1(grp)
        g_p = 1.0 / jnp.sum(jnp.exp(grp - g_max), axis=0, keepdims=True)
        fine = lt[SUBLANES:2 * SUBLANES, :]
        for g in range(1, N_GROUPS):
            fine = jnp.where(g_idx == g, lt[(g + 1) * SUBLANES:(g + 2) * SUBLANES, :], fine)
        m1, i1 = top1(fine)
        m2, i2 = top1(jnp.where(sub == i1, -jnp.inf, fine))
        e2 = jnp.exp(m2 - m1)
        p1 = 1.0 / (1.0 + e2)
        w1 = g_p * p1
        w2 = g_p * (e2 * p1)
        x1 = g_idx * EXPERTS_PER_GROUP + i1
        x2 = g_idx * EXPERTS_PER_GROUP + i2

        eid = lax.broadcasted_iota(jnp.int32, (N_EXPERTS, tm), 0)
        oh1 = jnp.where(eid == x1, 1.0, 0.0)
        oh2 = jnp.where(eid == x2, 1.0, 0.0)
        pre = jnp.dot(jnp.concatenate([oh1, oh2], axis=0).astype(BF16), before_ref[...],
                      preferred_element_type=F32)
        pre1, pre2 = pre[0:N_EXPERTS], pre[N_EXPERTS:2 * N_EXPERTS]
        cnt1 = jnp.sum(oh1, axis=1, keepdims=True)
        cnt2 = jnp.sum(oh2, axis=1, keepdims=True)
        seg8 = jnp.floor((cnt1 + cnt2 + (SUBLANES - 1.0)) * (1.0 / SUBLANES))
        e_r = lax.broadcasted_iota(jnp.int32, (N_EXPERTS, N_EXPERTS), 0)
        e_c = lax.broadcasted_iota(jnp.int32, (N_EXPERTS, N_EXPERTS), 1)
        lower = jnp.where(e_c < e_r, 1.0, 0.0).astype(BF16)
        seg8_b = jnp.broadcast_to(seg8, (N_EXPERTS, LANES)).astype(BF16)
        seg_off = SUBLANES * jnp.dot(lower, seg8_b, preferred_element_type=F32)[:, 0:1]
        pos1 = jnp.sum(oh1 * (pre1 + seg_off), axis=0, keepdims=True)
        pos2 = jnp.sum(oh2 * (pre2 + (seg_off + cnt1)), axis=0, keepdims=True)

        lane = lax.broadcasted_iota(jnp.int32, tc_ref.shape, 1)
        seg_rows = jnp.broadcast_to(seg8 * SUBLANES, tc_ref.shape).astype(jnp.int32)
        tile = (step - 1) * n_sub + j
        tc_ref[...] = jnp.where(lane == tile, seg_rows, tc_ref[...])

        zrow = jnp.zeros((SUBLANES - 4, tm), jnp.int32)
        rr_ref[:, rows] = jnp.concatenate(
            [pos1.astype(jnp.int32), pos2.astype(jnp.int32), x1, x2, zrow], axis=0)
        wt = jnp.concatenate([w1, w2, pos1, pos2, jnp.zeros((LANES - 4, tm), F32)], axis=0)
        rwt_ref[rows, :] = wt.T

    m_sb = _rms_f32(sb_ref[...], sbg_ref[...]).astype(BF16)
    lru = jnp.concatenate([r[...] for r in lru_refs], axis=-1)
    m_lru = _rms_f32(lru, lrug_ref[...]).astype(BF16)
    h_ref[...] = (x_ref[...]
                  + jnp.dot(m_sb, wo_ref[0:half, :], preferred_element_type=F32)
                  + jnp.dot(m_lru, wo_ref[half:2 * half, :], preferred_element_type=F32))

    for j in range(n_sub):
        route(j)

    u2_next = _rms_f32(h_ref[...], n2g_ref[...])
    u2_ref[...] = u2_next.astype(BF16)
    u2s_ref[...] = u2_next


def _mix_route(sb, lru, x2, sbg, lrug, w_out, n2g, wr_stack, rbias, tm, tp):
    n, d = x2.shape
    half = sb.shape[1]
    n_tiles = n // tp
    row = lambda i: (jnp.minimum(i, n_tiles - 1), 0)
    routed = lambda i: (jnp.maximum(i - 1, 0), 0)
    const = lambda i: (0, 0)
    return pl.pallas_call(
        functools.partial(_mix_route_kernel, tm=tm, n_slab=len(lru)),
        grid=(n_tiles + 1,),
        in_specs=[pl.BlockSpec((tp, half), row)] + [pl.BlockSpec((tp, LANES), row)] * len(lru)
                 + [pl.BlockSpec((tp, d), row),
                  pl.BlockSpec((1, half), const), pl.BlockSpec((1, half), const),
                  pl.BlockSpec(w_out.shape, const, pipeline_mode=pl.Buffered(1)),
                  pl.BlockSpec((1, d), const),
                  pl.BlockSpec(wr_stack.shape, const), pl.BlockSpec(rbias.shape, const)],
        out_specs=[pl.BlockSpec((tp, d), row), pl.BlockSpec((tp, d), row),
                   pl.BlockSpec((SUBLANES, tp), lambda i: (0, jnp.maximum(i - 1, 0))),
                   pl.BlockSpec((tp, LANES), routed),
                   pl.BlockSpec((N_EXPERTS, LANES), const)],
        out_shape=[jax.ShapeDtypeStruct((n, d), F32), jax.ShapeDtypeStruct((n, d), BF16),
                   jax.ShapeDtypeStruct((SUBLANES, n), jnp.int32),
                   jax.ShapeDtypeStruct((n, LANES), F32),
                   jax.ShapeDtypeStruct((N_EXPERTS, LANES), jnp.int32)],
        scratch_shapes=[pltpu.VMEM((tm, tm), BF16), pltpu.VMEM((tp, d), F32),
                        pltpu.VMEM(w_out.shape, BF16)],
        compiler_params=_cparams(("arbitrary",)),
        name="mix_route",
    )(sb, *lru, x2, sbg, lrug, w_out, n2g, wr_stack, rbias)


def _pack_halves(x):
    half = x.shape[1] // 2
    lo = lax.shift_right_logical(lax.bitcast_convert_type(x[:, :half], jnp.int32), 16)
    hi = lax.bitcast_convert_type(x[:, half:], jnp.int32) & HIGH_HALF
    return hi | lo


def _unpack_halves(p):
    lo = lax.bitcast_convert_type(lax.shift_left(p, 16), F32)
    hi = lax.bitcast_convert_type(p & HIGH_HALF, F32)
    return lo.astype(BF16), hi.astype(BF16)


def _segment_copies(tile, c8_ref, loff_ref, goff_ref, make):
    for e in range(N_EXPERTS):
        idx = tile * N_EXPERTS + e
        rows = pl.multiple_of(c8_ref[idx], SUBLANES)

        @pl.when(rows > 0)
        def _(idx=idx, rows=rows):
            lo = pl.multiple_of(loff_ref[idx], SUBLANES)
            go = pl.multiple_of(goff_ref[idx], SUBLANES)
            make(pl.ds(lo, rows), pl.ds(go, rows)).start()


def _dispatch_kernel(c8_ref, loff_ref, goff_ref, tot_ref, used_ref,
                     rr_ref, u2_ref, xs_ref, lbuf, zbuf, sem, zsem, *, td, lrows):
    i = pl.program_id(0)
    slot = i % 2

    r_id = lax.broadcasted_iota(jnp.int32, (lrows, td), 0)
    perm = jnp.where(r_id == rr_ref[0:1, :], 1.0, jnp.where(r_id == rr_ref[1:2, :], 1.0, 0.0))
    sorted_rows = jnp.dot(perm.astype(BF16), u2_ref[...], preferred_element_type=F32)
    lbuf[slot] = _pack_halves(sorted_rows)

    @pl.when(i == 0)
    def _():
        zbuf[...] = jnp.zeros_like(zbuf)
        chunk = zbuf.shape[0]
        used = used_ref[0]
        spare = xs_ref.shape[0] - used
        n_fill = (spare + chunk - 1) // chunk

        def fill_copy(k):
            rows = pl.multiple_of(jnp.minimum(chunk, spare - k * chunk), SUBLANES)
            start = pl.multiple_of(used + k * chunk, SUBLANES)
            return pltpu.make_async_copy(zbuf.at[pl.ds(0, rows)], xs_ref.at[pl.ds(start, rows)],
                                         zsem)

        def fill_start(k, c):
            fill_copy(k).start()
            return c

        def fill_wait(k, c):
            fill_copy(k).wait()
            return c

        lax.fori_loop(0, n_fill, fill_start, 0)
        lax.fori_loop(0, n_fill, fill_wait, 0)

    def wait_tile(tile, s):
        rows = pl.multiple_of(tot_ref[tile], SUBLANES)
        pltpu.make_async_copy(lbuf.at[s, pl.ds(0, rows)], xs_ref.at[pl.ds(0, rows)], sem).wait()

    @pl.when(i > 0)
    def _():
        wait_tile(i - 1, 1 - slot)

    _segment_copies(i, c8_ref, loff_ref, goff_ref,
                    lambda loc, glob: pltpu.make_async_copy(lbuf.at[slot, loc], xs_ref.at[glob], sem))

    @pl.when(i == pl.num_programs(0) - 1)
    def _():
        wait_tile(i, slot)


def _dispatch(c8, loff, goff, tot, used, rr, u2, p_rows, td, lrows, tme):
    n, d = u2.shape
    pmap = lambda i, *_: (0, i)
    return pl.pallas_call(
        functools.partial(_dispatch_kernel, td=td, lrows=lrows),
        grid_spec=pltpu.PrefetchScalarGridSpec(
            num_scalar_prefetch=5,
            grid=(n // td,),
            in_specs=[pl.BlockSpec((SUBLANES, td), pmap),
                      pl.BlockSpec((td, d), lambda i, *_: (i, 0))],
            out_specs=pl.BlockSpec(memory_space=pl.ANY),
            scratch_shapes=[pltpu.VMEM((2, lrows, d // 2), jnp.int32),
                            pltpu.VMEM((tme, d // 2), jnp.int32),
                            pltpu.SemaphoreType.DMA(()), pltpu.SemaphoreType.DMA(())]),
        out_shape=jax.ShapeDtypeStruct((p_rows, d // 2), jnp.int32),
        compiler_params=_cparams(("arbitrary",)),
        name="dispatch",
    )(c8, loff, goff, tot, used, rr, u2)


def _experts_kernel(eoff_ref, erows_ref, xs_ref, wg_ref, wu_ref, wd_ref, ys_ref,
                    wg_bf, wu_bf, wd_bf, xbuf, ybuf, sem_in, sem_out, state, *, tme):
    e = pl.program_id(0)
    n_experts = pl.num_programs(0)
    rows = erows_ref[e]
    off = eoff_ref[e]
    n_tiles = (rows + tme - 1) // tme
    nxt = jnp.minimum(e + 1, n_experts - 1)
    prefetch = (rows > 0) & (e + 1 < n_experts) & (erows_ref[nxt] > 0)

    def tile_rows(total, k):
        return pl.multiple_of(jnp.minimum(tme, total - k * tme), SUBLANES)

    def in_copy(start, r, slot):
        start = pl.multiple_of(start, SUBLANES)
        return pltpu.make_async_copy(xs_ref.at[pl.ds(start, r)], xbuf.at[slot, pl.ds(0, r)],
                                     sem_in.at[slot])

    def out_copy(start, r, slot):
        start = pl.multiple_of(start, SUBLANES)
        return pltpu.make_async_copy(ybuf.at[slot, pl.ds(0, r)], ys_ref.at[pl.ds(start, r)],
                                     sem_out.at[slot])

    @pl.when(e == 0)
    def _():
        for s in range(4):
            state[s] = 0
        xbuf[...] = jnp.zeros_like(xbuf)

    @pl.when(rows > 0)
    def _():
        wg_bf[...] = wg_ref[0].astype(BF16)
        wu_bf[...] = wu_ref[0].astype(BF16)
        wd_bf[...] = wd_ref[0].astype(BF16)
        done = state[0]

        @pl.when(state[1] == 0)
        def _():
            in_copy(off, tile_rows(rows, 0), done % 2).start()

        def tile(k, c):
            slot = (done + k) % 2
            r = tile_rows(rows, k)

            @pl.when(k + 1 < n_tiles)
            def _():
                in_copy(off + (k + 1) * tme, tile_rows(rows, k + 1), 1 - slot).start()

            @pl.when((k + 1 == n_tiles) & prefetch)
            def _():
                in_copy(eoff_ref[nxt], tile_rows(erows_ref[nxt], 0), 1 - slot).start()

            in_copy(off, r, slot).wait()

            @pl.when(done + k >= 2)
            def _():
                out_copy(0, pl.multiple_of(state[2 + slot], SUBLANES), slot).wait()

            def mlp(start, n):
                x = jnp.concatenate(_unpack_halves(xbuf[slot, start:start + n]), axis=1)
                hg = jnp.dot(x, wg_bf[...], preferred_element_type=F32)
                hu = jnp.dot(x, wu_bf[...], preferred_element_type=F32)
                act = (hg * jax.nn.sigmoid(hg) * hu).astype(BF16)
                y = jnp.dot(act, wd_bf[...], preferred_element_type=F32)
                ybuf[slot, start:start + n] = _pack_halves(y.astype(BF16).astype(F32))

            for below, n in zip((0,) + EXPERT_TILE_SIZES, EXPERT_TILE_SIZES):
                @pl.when((r > below) & (r <= n))
                def _(n=n):
                    mlp(0, n)

            out_copy(off + k * tme, r, slot).start()
            state[2 + slot] = r
            return c

        lax.fori_loop(0, n_tiles, tile, 0)
        state[0] = done + n_tiles

    state[1] = prefetch.astype(jnp.int32)

    @pl.when(e == n_experts - 1)
    def _():
        total = eoff_ref[e] + rows
        for slot in range(2):
            @pl.when(state[0] > slot)
            def _(slot=slot):
                out_copy(0, pl.multiple_of(state[2 + slot], SUBLANES), slot).wait()
        ybuf[0] = jnp.zeros(ybuf.shape[1:], ybuf.dtype)
        spare = ys_ref.shape[0] - total
        n_fill = (spare + tme - 1) // tme

        def fill_copy(k):
            return out_copy(total + k * tme, tile_rows(spare, k), 0)

        def fill_start(k, c):
            fill_copy(k).start()
            return c

        def fill_wait(k, c):
            fill_copy(k).wait()
            return c

        lax.fori_loop(0, n_fill, fill_start, 0)
        lax.fori_loop(0, n_fill, fill_wait, 0)


def _experts(eoff, erows, xs, wg, wu, wd, tme):
    p = xs.shape[0]
    n_experts, d, de = wg.shape
    wmap = lambda e, *_: (e, 0, 0)
    return pl.pallas_call(
        functools.partial(_experts_kernel, tme=tme),
        grid_spec=pltpu.PrefetchScalarGridSpec(
            num_scalar_prefetch=2,
            grid=(n_experts,),
            in_specs=[pl.BlockSpec(memory_space=pl.ANY),
                      pl.BlockSpec((1, d, de), wmap), pl.BlockSpec((1, d, de), wmap),
                      pl.BlockSpec((1, de, d), wmap)],
            out_specs=pl.BlockSpec(memory_space=pl.ANY),
            scratch_shapes=[pltpu.VMEM((d, de), BF16), pltpu.VMEM((d, de), BF16),
                            pltpu.VMEM((de, d), BF16),
                            pltpu.VMEM((2, tme, d // 2), jnp.int32),
                            pltpu.VMEM((2, tme, d // 2), jnp.int32),
                            pltpu.SemaphoreType.DMA((2,)), pltpu.SemaphoreType.DMA((2,)),
                            pltpu.SMEM((4,), jnp.int32)]),
        out_shape=jax.ShapeDtypeStruct((p, d // 2), jnp.int32),
        compiler_params=_cparams(("arbitrary",)),
        name="experts",
    )(eoff, erows, xs, wg, wu, wd)


def _combine_kernel(c8_ref, loff_ref, goff_ref, tot_ref,
                    rwt_ref, nxt_ref, h_ref, fg_ref, ys_ref, y_ref, ybuf, sems, spread_ref,
                    hbuf, hsems, *, tc, lrows, n_tiles):
    i = pl.program_id(0)
    slot = i % 2
    h_slot = i % COMBINE_RING

    def residual_copy(tile, s):
        start = pl.multiple_of(tile * tc, tc)
        return pltpu.make_async_copy(h_ref.at[pl.ds(start, tc)], hbuf.at[s], hsems.at[s])

    def gather_tile(tile, s):
        _segment_copies(tile, c8_ref, loff_ref, goff_ref,
                        lambda loc, glob: pltpu.make_async_copy(ys_ref.at[glob], ybuf.at[s, loc],
                                                                sems.at[s]))

    def spread(ref):
        w = ref[...]
        for c in range(4):
            spread_ref[c] = jnp.broadcast_to(w[:, c:c + 1], (tc, LANES))

    @pl.when(i == 0)
    def _():
        ybuf[...] = jnp.zeros_like(ybuf)
        for t in range(min(COMBINE_RING - 1, n_tiles)):
            residual_copy(t, t).start()
        gather_tile(0, 0)
        spread(rwt_ref)

    ahead = i + (COMBINE_RING - 1)

    @pl.when(ahead < n_tiles)
    def _():
        residual_copy(ahead, ahead % COMBINE_RING).start()

    @pl.when(i + 1 < n_tiles)
    def _():
        gather_tile(i + 1, 1 - slot)

    rows = pl.multiple_of(tot_ref[i], SUBLANES)
    pltpu.make_async_copy(ys_ref.at[pl.ds(0, rows)], ybuf.at[slot, pl.ds(0, rows)],
                          sems.at[slot]).wait()
    residual_copy(i, h_slot).wait()

    across = lambda c: jnp.tile(spread_ref[c], (1, lrows // LANES))
    c_id = lax.broadcasted_iota(jnp.int32, (tc, lrows), 1)
    pos1 = across(2).astype(jnp.int32)
    pos2 = across(3).astype(jnp.int32)
    wmat = jnp.where(c_id == pos1, across(0), jnp.where(c_id == pos2, across(1), 0.0)).astype(BF16)
    half = h_ref.shape[1] // 2
    outs, sumsq = [], 0.0
    for part, cols in zip(_unpack_halves(ybuf[slot]), (slice(0, half), slice(half, 2 * half))):
        out = hbuf[h_slot, :, cols] + jnp.dot(wmat, part, preferred_element_type=F32)
        if not outs:
            spread(nxt_ref)
        sumsq = sumsq + jnp.sum(out * out, axis=-1, keepdims=True)
        outs.append(out)
    scale = lax.rsqrt(sumsq * (1.0 / (2 * half)) + EPS)
    for out, cols in zip(outs, (slice(0, half), slice(half, 2 * half))):
        y_ref[:, cols] = out * scale * fg_ref[:, cols]


def _combine(c8, loff, goff, tot, rwt, h, final_g, ys, tc, lrows):
    n, d = h.shape
    n_tiles = n // tc
    return pl.pallas_call(
        functools.partial(_combine_kernel, tc=tc, lrows=lrows, n_tiles=n_tiles),
        grid_spec=pltpu.PrefetchScalarGridSpec(
            num_scalar_prefetch=4,
            grid=(n_tiles,),
            in_specs=[pl.BlockSpec((tc, LANES), lambda i, *_: (0, 0)),
                      pl.BlockSpec((tc, LANES), lambda i, *_: (jnp.minimum(i + 1, n_tiles - 1), 0)),
                      pl.BlockSpec(memory_space=pl.ANY),
                      pl.BlockSpec((1, d), lambda i, *_: (0, 0)),
                      pl.BlockSpec(memory_space=pl.ANY)],
            out_specs=pl.BlockSpec((tc, d), lambda i, *_: (i, 0)),
            scratch_shapes=[pltpu.VMEM((2, lrows, d // 2), jnp.int32),
                            pltpu.SemaphoreType.DMA((2,)),
                            pltpu.VMEM((4, tc, LANES), F32),
                            pltpu.VMEM((COMBINE_RING, tc, d), F32),
                            pltpu.SemaphoreType.DMA((COMBINE_RING,))]),
        out_shape=jax.ShapeDtypeStruct((n, d), F32),
        compiler_params=_cparams(("arbitrary",)),
        name="combine",
    )(c8, loff, goff, tot, rwt, rwt, h, final_g, ys)


def _router_tables(w_group, b_group, w_fine, b_fine):
    d = w_group.shape[0]
    pad_g, pad_f = SUBLANES - N_GROUPS, ROUTER_ROWS - SUBLANES - N_EXPERTS
    w = jnp.concatenate([w_group.T, jnp.zeros((pad_g, d), F32), w_fine.T,
                         jnp.zeros((pad_f, d), F32)], axis=0)
    b = jnp.concatenate([b_group, jnp.full((pad_g,), NEG_BIG, F32), b_fine,
                         jnp.full((pad_f,), NEG_BIG, F32)])
    w_hi = w.astype(BF16)
    w_lo = (w - w_hi.astype(F32)).astype(BF16)
    return jnp.concatenate([w_hi, w_lo], axis=0), b.reshape(ROUTER_ROWS, 1)


def kernel(x, norm1_g, w_in, conv_w, conv_b, w_rgate, b_rgate, w_igate, b_igate, lam, sb_norm_g,
           lru_norm_g, w_out, norm2_g, w_group, b_group, w_fine, b_fine, w_e_gate, w_e_up,
           w_e_down, final_g):
    batch, seq, d = x.shape
    n = batch * seq
    width = w_in.shape[1] // 5
    tm = min(TOKEN_TILE, seq)
    tp = min(ROW_TILE, seq)
    tme = EXPERT_TILE_ROWS

    x2 = x.reshape(n, d)
    vec = lambda a: a.reshape(1, -1)

    q, k, v, xl, gl = _in_proj(x2, vec(norm1_g), w_in, width, tp)
    out_sb = _attention(q, k, v, batch, seq)
    out_lru = _lru(xl, gl, conv_w, vec(conv_b), w_rgate, vec(b_rgate), w_igate, vec(b_igate),
                   vec(lam), batch, seq, tp)

    wr_stack, rbias = _router_tables(w_group, b_group, w_fine, b_fine)
    h, u2, rr, rwt, tcnt = _mix_route(out_sb, out_lru, x2, vec(sb_norm_g), vec(lru_norm_g),
                                      w_out, vec(norm2_g), wr_stack, rbias, tm, tp)

    n_tiles = n // tm
    assert n_tiles <= LANES, "one lane of the per-tile count table per token tile"
    c8 = tcnt[:, :n_tiles].T
    erows = jnp.sum(c8, axis=0)
    eoff = jnp.cumsum(erows) - erows
    goff = eoff[None, :] + jnp.cumsum(c8, axis=0) - c8
    loff = jnp.cumsum(c8, axis=1) - c8
    tot = jnp.sum(c8, axis=1)
    lrows = 2 * tm + N_EXPERTS * SUBLANES
    p_rows = 2 * n + n_tiles * N_EXPERTS * (SUBLANES - 1)
    p_rows = -(-p_rows // SUBLANES) * SUBLANES
    i32 = lambda a: a.reshape(-1).astype(jnp.int32)
    c8, loff, goff, tot, eoff, erows = (i32(a) for a in (c8, loff, goff, tot, eoff, erows))

    xs = _dispatch(c8, loff, goff, tot, jnp.sum(erows, keepdims=True), rr, u2, p_rows, tm, lrows, tme)
    ys = _experts(eoff, erows, xs, w_e_gate, w_e_up, w_e_down, tme)
    y = _combine(c8, loff, goff, tot, rwt, h, vec(final_g), ys, tm, lrows)
    return y.reshape(batch, seq, d)
```

```python
import functools
import math

import jax
import jax.numpy as jnp
from jax import lax
from jax.experimental import pallas as pl
from jax.experimental.pallas import tpu as pltpu

F32 = jnp.float32
BF16 = jnp.bfloat16

EPS = 1e-6
HEAD_DIM = 64
HEADS_PER_BLOCK = 2
LANES = 128
SUBLANES = 8
CONV_W = 4
RG_C = 8.0
N_GROUPS = 4
EXPERTS_PER_GROUP = 8
N_EXPERTS = N_GROUPS * EXPERTS_PER_GROUP
ROUTER_ROWS = 48
NEG_BIG = -1e30
LOG2_E = math.log2(math.e)
ATTN_STOP = 104.0 * LOG2_E
ATTN_QUERY_ROWS = 64
ATTN_WINDOW_BLOCKS = 2
ATTN_UNROLL = 24
ATTN_STAGE_LAG = 2
HIGH_HALF = -65536
EXPERT_TILE_ROWS = 1536
EXPERT_TILE_SIZES = (256, 512, 1024, 1152, 1280, 1536)
TOKEN_TILE = 512
ROW_TILE = 1024

VMEM_LIMIT = 56 * 1024 * 1024


def _cparams(sem):
    return pltpu.CompilerParams(dimension_semantics=sem, vmem_limit_bytes=VMEM_LIMIT)


def _rms_f32(x, g):
    return x * lax.rsqrt(jnp.mean(x * x, axis=-1, keepdims=True) + EPS) * g


def _in_proj_kernel(x_ref, g_ref, w_ref, q_ref, k_ref, v_ref, xl_ref, gl_ref, w_bf, *,
                    width, q_scale):
    @pl.when(pl.program_id(0) == 0)
    def _():
        for c in range(w_ref.shape[1] // width):
            cols = slice(c * width, (c + 1) * width)
            w_bf[:, cols] = w_ref[:, cols].astype(BF16)

    u = _rms_f32(x_ref[...], g_ref[...]).astype(BF16)
    for c, o_ref in enumerate((q_ref, k_ref, v_ref, xl_ref, gl_ref)):
        p = jnp.dot(u, w_bf[:, c * width:(c + 1) * width], preferred_element_type=F32)
        if c == 0:
            p = p * q_scale
        o_ref[...] = p.astype(o_ref.dtype)


def _in_proj(x2, g, w, width, tm):
    n, d = x2.shape
    row = lambda i: (i, 0)
    out_bf = jax.ShapeDtypeStruct((n, width), BF16)
    out_f = jax.ShapeDtypeStruct((n, width), F32)
    return pl.pallas_call(
        functools.partial(_in_proj_kernel, width=width, q_scale=1.0 / math.sqrt(HEAD_DIM)),
        grid=(n // tm,),
        in_specs=[pl.BlockSpec((tm, d), row),
                  pl.BlockSpec((1, d), lambda i: (0, 0)),
                  pl.BlockSpec(w.shape, lambda i: (0, 0), pipeline_mode=pl.Buffered(1))],
        out_specs=[pl.BlockSpec((tm, width), row)] * 5,
        out_shape=[out_bf, out_bf, out_bf, out_f, out_f],
        scratch_shapes=[pltpu.VMEM(w.shape, BF16)],
        compiler_params=_cparams(("arbitrary",)),
        name="in_proj",
    )(x2, g, w)


def _attn_kernel(q_ref, k_ref, v_ref, o_ref, tri_ref, z_ref, arg_ref, ctot_ref, acc_ref, carry_ref,
                 *, tq, kb, nsub, fill):
    seq = q_ref.shape[0]
    win = nsub * kb
    lookback = win - tq
    lane = lax.broadcasted_iota(jnp.int32, (1, LANES), 1)
    rel = (lax.broadcasted_iota(jnp.int32, (tq, kb), 1)
           - lax.broadcasted_iota(jnp.int32, (tq, kb), 0))
    rel = jnp.concatenate([rel] * HEADS_PER_BLOCK, axis=0)

    k_r = lax.broadcasted_iota(jnp.int32, (kb, 2 * kb), 0)
    k_c = lax.broadcasted_iota(jnp.int32, (kb, 2 * kb), 1)
    tri_ref[...] = jnp.where(k_c >= kb, 1.0, jnp.where(k_r > k_c, 1.0, 0.0)).astype(BF16)

    def softplus2(z):
        return jnp.maximum(z, 0.0) + jnp.log2(1.0 + jnp.exp2(-jnp.abs(z)))

    def scores(qh, keys):
        z = LOG2_E * lax.dot_general(qh, keys, (((1,), (1,)), ((), ())),
                                     preferred_element_type=F32)
        nlog_nb = softplus2(z)
        return nlog_nb, z - nlog_nb

    def suffix(nlog_nb):
        r = jnp.dot(nlog_nb.astype(BF16), tri_ref[...], preferred_element_type=F32)
        return r[:, :kb], r[:, kb:]

    def stacked_queries(i):
        q = q_ref[pl.ds(i * tq, tq), :]
        return jnp.concatenate(
            [jnp.where((lane >= h * HEAD_DIM) & (lane < (h + 1) * HEAD_DIM), q, jnp.zeros_like(q))
             for h in range(HEADS_PER_BLOCK)], axis=0)

    def store(i, out):
        o_ref[pl.ds(i * tq, tq), :] = jnp.where(lane < HEAD_DIM, out[0:tq], out[tq:2 * tq])

    def window_start(i):
        if isinstance(i, int):
            return max(i * tq - lookback, 0)
        return pl.multiple_of(i * tq - lookback, tq)

    def stage_scores(i, p):
        keys = k_ref[pl.ds(window_start(i), win), :]
        z_ref[p] = LOG2_E * lax.dot_general(stacked_queries(i), keys, (((1,), (1,)), ((), ())),
                                            preferred_element_type=F32)

    def stage_exponents(p, delta):
        z = z_ref[p]
        softplus = softplus2(z)
        carry = None
        for b in reversed(range(nsub)):
            cols = slice(b * kb, (b + 1) * kb)
            masked = (b + 1) * kb > delta
            valid = (rel + b * kb) < delta
            nl = softplus[:, cols]
            if masked:
                nl = jnp.where(valid, nl, 0.0)
            excl, tot = suffix(nl)
            arg = z[:, cols] - softplus[:, cols] - excl
            if carry is not None:
                arg = arg - carry
            if masked:
                arg = jnp.where(valid, arg, NEG_BIG)
            arg_ref[p, :, cols] = arg
            carry = tot if carry is None else carry + tot
        ctot_ref[p] = carry
        return jnp.min(carry)

    def stage_output(i, p, s):
        vals = v_ref[pl.ds(window_start(i), win), :]
        acc_ref[s] = jnp.dot(jnp.exp2(arg_ref[p]).astype(BF16), vals, preferred_element_type=F32)
        carry_ref[s] = ctot_ref[p]

    def finish(i, s, cmin):
        def cond(state):
            pos, cmin = state
            return (pos > -kb) & (cmin <= ATTN_STOP)

        def older(state):
            pos, _ = state
            start = pl.multiple_of(jnp.maximum(pos, 0), tq)
            keys = k_ref[pl.ds(start, kb), :]
            vals = v_ref[pl.ds(start, kb), :]
            fresh = lax.broadcasted_iota(jnp.int32, (1, kb), 1) < pos + kb - start
            nlog_nb, log_b = scores(stacked_queries(i), keys)
            nlog_nb = jnp.where(fresh, nlog_nb, 0.0)
            excl, tot = suffix(nlog_nb)
            carry = carry_ref[s]
            a = jnp.where(fresh, jnp.exp2(log_b - excl - carry), 0.0)
            acc_ref[s] += jnp.dot(a.astype(BF16), vals, preferred_element_type=F32)
            carry_ref[s] = carry + tot
            return pos - kb, jnp.min(carry + tot)

        lax.while_loop(cond, older, (jnp.asarray(window_start(i) - kb, jnp.int32), cmin))
        store(i, acc_ref[s])

    n_blocks = seq // tq
    unroll = acc_ref.shape[0]
    lag = z_ref.shape[0] - 1
    n_slots = lag + 1
    pending = []
    for tau in range(fill):
        stage_scores(tau, tau % n_slots)
        if tau >= lag:
            j = tau - lag
            pending.append(stage_exponents(j % n_slots, j * tq - window_start(j)))
        if tau >= 2 * lag:
            j = tau - 2 * lag
            stage_output(j, j % n_slots, 0)
            finish(j, 0, pending.pop(0))

    def steady(m, pending):
        pending = list(pending)
        tau0 = fill + unroll * m
        done = []
        for u in range(unroll):
            stage_output(tau0 + u - 2 * lag, (fill - 2 * lag + u) % n_slots, u)
            done.append((tau0 + u - 2 * lag, u, pending.pop(0)))
            pending.append(stage_exponents((fill - lag + u) % n_slots, lookback))
            stage_scores(tau0 + u, (fill + u) % n_slots)
        for block, s, cmin in done:
            finish(block, s, cmin)
        return tuple(pending)

    pending = list(lax.fori_loop(0, (n_blocks - fill) // unroll, steady, tuple(pending)))
    for tau in range(n_blocks, n_blocks + 2 * lag):
        j = tau - 2 * lag
        stage_output(j, j % n_slots, 0)
        cmin = pending.pop(0)
        if tau - lag < n_blocks:
            pending.append(stage_exponents((tau - lag) % n_slots, lookback))
        finish(j, 0, cmin)


def _attention(q, k, v, batch, seq):
    n, width = q.shape
    tq, kb, nsub = ATTN_QUERY_ROWS, LANES, ATTN_WINDOW_BLOCKS
    n_blocks = seq // tq
    lag, n_slots = ATTN_STAGE_LAG, ATTN_STAGE_LAG + 1
    clipped = -(-(nsub * kb - tq) // tq)
    min_fill = clipped + 2 * lag
    assert seq % tq == 0 and seq >= nsub * kb and n_blocks >= min_fill
    options = [(f + 2 * (n_blocks - f) // u, u, f)
               for u in range(ATTN_UNROLL - ATTN_UNROLL % n_slots, 0, -n_slots)
               for f in range(min_fill, n_blocks + 1) if (n_blocks - f) % u == 0]
    _, unroll, fill = min(options)
    blk = pl.BlockSpec((seq, LANES), lambda b, hp: (b, hp))
    rows = HEADS_PER_BLOCK * tq
    stage_buf = pltpu.VMEM((n_slots, rows, nsub * kb), F32)
    carry_buf = pltpu.VMEM((n_slots, rows, kb), F32)
    row_buf = pltpu.VMEM((unroll, rows, LANES), F32)
    return pl.pallas_call(
        functools.partial(_attn_kernel, tq=tq, kb=kb, nsub=nsub, fill=fill),
        grid=(batch, width // LANES),
        in_specs=[blk, blk, blk],
        out_specs=blk,
        out_shape=jax.ShapeDtypeStruct((n, width), F32),
        scratch_shapes=[pltpu.VMEM((kb, 2 * kb), BF16),
                        stage_buf, stage_buf, carry_buf, row_buf, row_buf],
        compiler_params=_cparams(("arbitrary", "arbitrary")),
        name="attn",
    )(q, k, v)


def _gelu_tanh(x):
    c = math.sqrt(2.0 / math.pi)
    half_x = 0.5 * x
    return half_x + half_x * jnp.tanh(x * (c + (c * 0.044715) * (x * x)))


def _sigmoid(x):
    return 0.5 + 0.5 * jnp.tanh(0.5 * x)


def _lru_kernel(*refs, ts, n_slab):
    xl_refs, gl_refs = refs[0:n_slab], refs[n_slab:2 * n_slab]
    cw_ref, cb_ref, wr_ref, br_ref, wi_ref, bi_ref, lam_ref = refs[2 * n_slab:2 * n_slab + 7]
    o_refs = refs[2 * n_slab + 7:3 * n_slab + 7]
    (tail_ref, a7_ref, u7_ref, hp_ref, pa_ref, pu_ref, h_ref,
     wr_bd, wi_bd) = refs[3 * n_slab + 7:]
    t = pl.program_id(1)
    groups = ts // SUBLANES

    @pl.when(t == 0)
    def _():
        tail_ref[...] = jnp.zeros_like(tail_ref)
        h_ref[...] = jnp.zeros_like(h_ref)
        per = LANES // wr_ref.shape[1]
        for src, dst in ((wr_ref, wr_bd), (wi_ref, wi_bd)):
            for c in range(n_slab):
                rows = []
                for p in range(per):
                    blk = src[c * per + p]
                    rows.append(jnp.concatenate(
                        [blk if q == p else jnp.zeros_like(blk) for q in range(per)], axis=1))
                dst[c] = jnp.concatenate(rows, axis=0).astype(BF16)

    first_group = lax.broadcasted_iota(jnp.int32, (groups, LANES), 0) == 0
    for c in range(n_slab):
        lanes = slice(c * LANES, (c + 1) * LANES)
        x = [xl_refs[c][pl.ds(s, groups, stride=SUBLANES), :] for s in range(SUBLANES)]
        shifted = {}
        for s in range(SUBLANES - (CONV_W - 1), SUBLANES):
            shifted[s] = jnp.where(first_group, tail_ref[c, s:s + 1, :], pltpu.roll(x[s], 1, axis=0))
            tail_ref[c, s:s + 1, :] = x[s][groups - 1:groups, :]
        conv = []
        for s in range(SUBLANES):
            y = cb_ref[:, lanes]
            for w in range(CONV_W):
                j = s - (CONV_W - 1) + w
                y = y + (x[j] if j >= 0 else shifted[j + SUBLANES]) * cw_ref[w:w + 1, lanes]
            conv.append(y)
        xc = jnp.concatenate(conv, axis=0)

        xcb = xc.astype(BF16)
        r = _sigmoid(jnp.dot(xcb, wr_bd[c], preferred_element_type=F32) + br_ref[:, lanes])
        ig = _sigmoid(jnp.dot(xcb, wi_bd[c], preferred_element_type=F32) + bi_ref[:, lanes])
        lam = lam_ref[:, lanes]
        log_sig_lam = -(jnp.maximum(-lam, 0.0) + jnp.log1p(jnp.exp(-jnp.abs(lam))))
        log_a = r * (RG_C * log_sig_lam)
        a = jnp.exp(log_a)
        th = jnp.tanh(log_a)
        one_m_a2 = -2.0 * th / (1.0 - th)
        root = jnp.where(one_m_a2 > 0.0, one_m_a2 * lax.rsqrt(one_m_a2), 0.0)
        u = root * (ig * xc)

        a_run = u_run = None
        for s in range(SUBLANES):
            rows = slice(s * groups, (s + 1) * groups)
            if s == 0:
                a_run, u_run = a[rows], u[rows]
            else:
                u_run = a[rows] * u_run + u[rows]
                a_run = a_run * a[rows]
            pa_ref[c, rows, :] = a_run
            pu_ref[c, rows, :] = u_run
        a7_ref[c] = a_run
        u7_ref[c] = u_run

    def group(g, hs):
        nxt = []
        for c in range(n_slab):
            hp_ref[c, pl.ds(g, 1), :] = hs[c]
            nxt.append(a7_ref[c, pl.ds(g, 1), :] * hs[c] + u7_ref[c, pl.ds(g, 1), :])
        return tuple(nxt)

    hs = lax.fori_loop(0, groups, group, tuple(h_ref[c] for c in range(n_slab)), unroll=8)
    for c in range(n_slab):
        h_ref[c] = hs[c]

    for c in range(n_slab):
        h_in = hp_ref[c]
        for s in range(SUBLANES):
            rows = slice(s * groups, (s + 1) * groups)
            hseq = pu_ref[c, rows, :] + pa_ref[c, rows, :] * h_in
            gate = _gelu_tanh(gl_refs[c][pl.ds(s, groups, stride=SUBLANES), :])
            o_refs[c][pl.ds(s, groups, stride=SUBLANES), :] = hseq * gate


def _lru(xl, gl, conv_w, conv_b, w_r, br, w_i, bi, lam, batch, seq, ts):
    n, width = xl.shape
    nt = seq // ts
    n_slab = width // LANES
    blocks, block_w, _ = w_r.shape
    assert blocks * block_w == width and LANES % block_w == 0 and ts % (SUBLANES * SUBLANES) == 0
    groups = ts // SUBLANES
    gate_bd = pltpu.VMEM((n_slab, LANES, LANES), BF16)
    slab = [pl.BlockSpec((ts, LANES), functools.partial(lambda b, t, c: (b * nt + t, c), c=c))
            for c in range(n_slab)]
    const2 = lambda b, t: (0, 0)
    const3 = lambda b, t: (0, 0, 0)
    vec = pl.BlockSpec((1, width), const2)
    per_group = pltpu.VMEM((n_slab, groups, LANES), F32)
    per_step = pltpu.VMEM((n_slab, ts, LANES), F32)
    return pl.pallas_call(
        functools.partial(_lru_kernel, ts=ts, n_slab=n_slab),
        grid=(batch, nt),
        in_specs=slab + slab + [pl.BlockSpec((CONV_W, width), const2), vec,
                                pl.BlockSpec(w_r.shape, const3), vec,
                                pl.BlockSpec(w_i.shape, const3), vec, vec],
        out_specs=[pl.BlockSpec((ts, LANES), lambda b, t: (b * nt + t, 0))] * n_slab,
        out_shape=[jax.ShapeDtypeStruct((n, LANES), F32)] * n_slab,
        scratch_shapes=[pltpu.VMEM((n_slab, SUBLANES, LANES), F32),
                        per_group, per_group, per_group, per_step, per_step,
                        pltpu.VMEM((n_slab, 1, LANES), F32), gate_bd, gate_bd],
        compiler_params=_cparams(("arbitrary", "arbitrary")),
        name="lru",
    )(*([xl] * n_slab), *([gl] * n_slab), conv_w, conv_b, w_r, br, w_i, bi, lam)


def _mix_route_kernel(sb_ref, *refs, tm, n_slab):
    lru_refs = refs[:n_slab]
    (x_ref, sbg_ref, lrug_ref, wo_ref, n2g_ref, wrs_ref, rb_ref,
     h_ref, u2_ref, rr_ref, rwt_ref, tc_ref, before_ref, u2s_ref, wo_bf) = refs[n_slab:]

    @pl.when(pl.program_id(0) == 0)
    def _():
        wo_bf[...] = wo_ref[...].astype(BF16)

    _mix_route_body(sb_ref, lru_refs, x_ref, sbg_ref, lrug_ref, wo_bf, n2g_ref, wrs_ref, rb_ref,
                    h_ref, u2_ref, rr_ref, rwt_ref, tc_ref, before_ref, u2s_ref, tm)


def _mix_route_body(sb_ref, lru_refs, x_ref, sbg_ref, lrug_ref, wo_ref, n2g_ref, wrs_ref, rb_ref,
                    h_ref, u2_ref, rr_ref, rwt_ref, tc_ref, before_ref, u2s_ref, tm):
    step = pl.program_id(0)
    half = sb_ref.shape[1]
    n_sub = x_ref.shape[0] // tm

    @pl.when(step == 0)
    def _():
        r_id = lax.broadcasted_iota(jnp.int32, (tm, tm), 0)
        c_id = lax.broadcasted_iota(jnp.int32, (tm, tm), 1)
        before_ref[...] = (r_id < c_id).astype(BF16)
        tc_ref[...] = jnp.zeros_like(tc_ref)
        u2s_ref[...] = jnp.zeros_like(u2s_ref)

    def route(j):
        rows = slice(j * tm, (j + 1) * tm)
        u2 = u2s_ref[rows, :]
        u_hi = u2.astype(BF16)
        u_lo = (u2 - u_hi.astype(F32)).astype(BF16)
        nt_dims = (((1,), (1,)), ((), ()))
        n_rows = rb_ref.shape[0]
        both = lax.dot_general(wrs_ref[...], u_hi, nt_dims, preferred_element_type=F32)
        lt = (both[0:n_rows] + both[n_rows:2 * n_rows]
              + lax.dot_general(wrs_ref[0:n_rows, :], u_lo, nt_dims, preferred_element_type=F32)
              + rb_ref[...])

        sub = lax.broadcasted_iota(jnp.int32, (SUBLANES, tm), 0)

        def top1(x):
            m = jnp.max(x, axis=0, keepdims=True)
            idx = jnp.min(jnp.where(x == m, sub, SUBLANES), axis=0, keepdims=True)
            return m, idx

        grp = lt[0:SUBLANES, :]
        g_max, g_idx = top1(grp)
        g_p = 1.0 / jnp.sum(jnp.exp(grp - g_max), axis=0, keepdims=True)
        fine = lt[SUBLANES:2 * SUBLANES, :]
        for g in range(1, N_GROUPS):
            fine = jnp.where(g_idx == g, lt[(g + 1) * SUBLANES:(g + 2) * SUBLANES, :], fine)
        m1, i1 = top1(fine)
        m2, i2 = top1(jnp.where(sub == i1, -jnp.inf, fine))
        e2 = jnp.exp(m2 - m1)
        p1 = 1.0 / (1.0 + e2)
        w1 = g_p * p1
        w2 = g_p * (e2 * p1)
        x1 = g_idx * EXPERTS_PER_GROUP + i1
        x2 = g_idx * EXPERTS_PER_GROUP + i2

        eid = lax.broadcasted_iota(jnp.int32, (N_EXPERTS, tm), 0)
        oh1 = jnp.where(eid == x1, 1.0, 0.0)
        oh2 = jnp.where(eid == x2, 1.0, 0.0)
        pre = jnp.dot(jnp.concatenate([oh1, oh2], axis=0).astype(BF16), before_ref[...],
                      preferred_element_type=F32)
        pre1, pre2 = pre[0:N_EXPERTS], pre[N_EXPERTS:2 * N_EXPERTS]
        cnt1 = jnp.sum(oh1, axis=1, keepdims=True)
        cnt2 = jnp.sum(oh2, axis=1, keepdims=True)
        seg8 = jnp.floor((cnt1 + cnt2 + (SUBLANES - 1.0)) * (1.0 / SUBLANES))
        e_r = lax.broadcasted_iota(jnp.int32, (N_EXPERTS, N_EXPERTS), 0)
        e_c = lax.broadcasted_iota(jnp.int32, (N_EXPERTS, N_EXPERTS), 1)
        lower = jnp.where(e_c < e_r, 1.0, 0.0).astype(BF16)
        seg8_b = jnp.broadcast_to(seg8, (N_EXPERTS, LANES)).astype(BF16)
        seg_off = SUBLANES * jnp.dot(lower, seg8_b, preferred_element_type=F32)[:, 0:1]
        pos1 = jnp.sum(oh1 * (pre1 + seg_off), axis=0, keepdims=True)
        pos2 = jnp.sum(oh2 * (pre2 + (seg_off + cnt1)), axis=0, keepdims=True)

        lane = lax.broadcasted_iota(jnp.int32, tc_ref.shape, 1)
        seg_rows = jnp.broadcast_to(seg8 * SUBLANES, tc_ref.shape).astype(jnp.int32)
        tile = (step - 1) * n_sub + j
        tc_ref[...] = jnp.where(lane == tile, seg_rows, tc_ref[...])

        zrow = jnp.zeros((SUBLANES - 4, tm), jnp.int32)
        rr_ref[:, rows] = jnp.concatenate(
            [pos1.astype(jnp.int32), pos2.astype(jnp.int32), x1, x2, zrow], axis=0)
        wt = jnp.concatenate([w1, w2, pos1, pos2, jnp.zeros((LANES - 4, tm), F32)], axis=0)
        rwt_ref[rows, :] = wt.T

    m_sb = _rms_f32(sb_ref[...], sbg_ref[...]).astype(BF16)
    lru = jnp.concatenate([r[...] for r in lru_refs], axis=-1)
    m_lru = _rms_f32(lru, lrug_ref[...]).astype(BF16)
    h_ref[...] = (x_ref[...]
                  + jnp.dot(m_sb, wo_ref[0:half, :], preferred_element_type=F32)
                  + jnp.dot(m_lru, wo_ref[half:2 * half, :], preferred_element_type=F32))

    for j in range(n_sub):
        route(j)

    u2_next = _rms_f32(h_ref[...], n2g_ref[...])
    u2_ref[...] = u2_next.astype(BF16)
    u2s_ref[...] = u2_next


def _mix_route(sb, lru, x2, sbg, lrug, w_out, n2g, wr_stack, rbias, tm, tp):
    n, d = x2.shape
    half = sb.shape[1]
    n_tiles = n // tp
    row = lambda i: (jnp.minimum(i, n_tiles - 1), 0)
    routed = lambda i: (jnp.maximum(i - 1, 0), 0)
    const = lambda i: (0, 0)
    return pl.pallas_call(
        functools.partial(_mix_route_kernel, tm=tm, n_slab=len(lru)),
        grid=(n_tiles + 1,),
        in_specs=[pl.BlockSpec((tp, half), row)] + [pl.BlockSpec((tp, LANES), row)] * len(lru)
                 + [pl.BlockSpec((tp, d), row),
                  pl.BlockSpec((1, half), const), pl.BlockSpec((1, half), const),
                  pl.BlockSpec(w_out.shape, const, pipeline_mode=pl.Buffered(1)),
                  pl.BlockSpec((1, d), const),
                  pl.BlockSpec(wr_stack.shape, const), pl.BlockSpec(rbias.shape, const)],
        out_specs=[pl.BlockSpec((tp, d), row), pl.BlockSpec((tp, d), row),
                   pl.BlockSpec((SUBLANES, tp), lambda i: (0, jnp.maximum(i - 1, 0))),
                   pl.BlockSpec((tp, LANES), routed),
                   pl.BlockSpec((N_EXPERTS, LANES), const)],
        out_shape=[jax.ShapeDtypeStruct((n, d), F32), jax.ShapeDtypeStruct((n, d), BF16),
                   jax.ShapeDtypeStruct((SUBLANES, n), jnp.int32),
                   jax.ShapeDtypeStruct((n, LANES), F32),
                   jax.ShapeDtypeStruct((N_EXPERTS, LANES), jnp.int32)],
        scratch_shapes=[pltpu.VMEM((tm, tm), BF16), pltpu.VMEM((tp, d), F32),
                        pltpu.VMEM(w_out.shape, BF16)],
        compiler_params=_cparams(("arbitrary",)),
        name="mix_route",
    )(sb, *lru, x2, sbg, lrug, w_out, n2g, wr_stack, rbias)


def _pack_halves(x):
    half = x.shape[1] // 2
    lo = lax.shift_right_logical(lax.bitcast_convert_type(x[:, :half], jnp.int32), 16)
    hi = lax.bitcast_convert_type(x[:, half:], jnp.int32) & HIGH_HALF
    return hi | lo


def _unpack_halves(p):
    lo = lax.bitcast_convert_type(lax.shift_left(p, 16), F32)
    hi = lax.bitcast_convert_type(p & HIGH_HALF, F32)
    return lo.astype(BF16), hi.astype(BF16)


def _segment_copies(tile, c8_ref, loff_ref, goff_ref, make):
    for e in range(N_EXPERTS):
        idx = tile * N_EXPERTS + e
        rows = pl.multiple_of(c8_ref[idx], SUBLANES)

        @pl.when(rows > 0)
        def _(idx=idx, rows=rows, e=e):
            lo = pl.multiple_of(loff_ref[idx], SUBLANES)
            go = pl.multiple_of(goff_ref[idx], SUBLANES)
            make(pl.ds(lo, rows), pl.ds(go, rows)).start(priority=e % 2)


def _dispatch_kernel(c8_ref, loff_ref, goff_ref, tot_ref, used_ref,
                     rr_ref, u2_ref, xs_ref, lbuf, zbuf, sem, zsem, *, td, lrows):
    i = pl.program_id(0)
    slot = i % 2

    r_id = lax.broadcasted_iota(jnp.int32, (lrows, td), 0)
    perm = jnp.where(r_id == rr_ref[0:1, :], 1.0, jnp.where(r_id == rr_ref[1:2, :], 1.0, 0.0))
    sorted_rows = jnp.dot(perm.astype(BF16), u2_ref[...], preferred_element_type=F32)
    lbuf[slot] = _pack_halves(sorted_rows)

    @pl.when(i == 0)
    def _():
        zbuf[...] = jnp.zeros_like(zbuf)
        chunk = zbuf.shape[0]
        used = used_ref[0]
        spare = xs_ref.shape[0] - used
        n_fill = (spare + chunk - 1) // chunk

        def fill_copy(k):
            rows = pl.multiple_of(jnp.minimum(chunk, spare - k * chunk), SUBLANES)
            start = pl.multiple_of(used + k * chunk, SUBLANES)
            return pltpu.make_async_copy(zbuf.at[pl.ds(0, rows)], xs_ref.at[pl.ds(start, rows)],
                                         zsem)

        def fill_start(k, c):
            fill_copy(k).start()
            return c

        def fill_wait(k, c):
            fill_copy(k).wait()
            return c

        lax.fori_loop(0, n_fill, fill_start, 0)
        lax.fori_loop(0, n_fill, fill_wait, 0)

    def wait_tile(tile, s):
        rows = pl.multiple_of(tot_ref[tile], SUBLANES)
        pltpu.make_async_copy(lbuf.at[s, pl.ds(0, rows)], xs_ref.at[pl.ds(0, rows)], sem).wait()

    @pl.when(i > 0)
    def _():
        wait_tile(i - 1, 1 - slot)

    _segment_copies(i, c8_ref, loff_ref, goff_ref,
                    lambda loc, glob: pltpu.make_async_copy(lbuf.at[slot, loc], xs_ref.at[glob], sem))

    @pl.when(i == pl.num_programs(0) - 1)
    def _():
        wait_tile(i, slot)


def _dispatch(c8, loff, goff, tot, used, rr, u2, p_rows, td, lrows, tme):
    n, d = u2.shape
    pmap = lambda i, *_: (0, i)
    return pl.pallas_call(
        functools.partial(_dispatch_kernel, td=td, lrows=lrows),
        grid_spec=pltpu.PrefetchScalarGridSpec(
            num_scalar_prefetch=5,
            grid=(n // td,),
            in_specs=[pl.BlockSpec((SUBLANES, td), pmap),
                      pl.BlockSpec((td, d), lambda i, *_: (i, 0))],
            out_specs=pl.BlockSpec(memory_space=pl.ANY),
            scratch_shapes=[pltpu.VMEM((2, lrows, d // 2), jnp.int32),
                            pltpu.VMEM((tme, d // 2), jnp.int32),
                            pltpu.SemaphoreType.DMA(()), pltpu.SemaphoreType.DMA(())]),
        out_shape=jax.ShapeDtypeStruct((p_rows, d // 2), jnp.int32),
        compiler_params=_cparams(("arbitrary",)),
        name="dispatch",
    )(c8, loff, goff, tot, used, rr, u2)


def _experts_kernel(eoff_ref, erows_ref, xs_ref, wg_ref, wu_ref, wd_ref, ys_ref,
                    wg_bf, wu_bf, wd_bf, xbuf, ybuf, sem_in, sem_out, state, *, tme):
    e = pl.program_id(0)
    n_experts = pl.num_programs(0)
    rows = erows_ref[e]
    off = eoff_ref[e]
    n_tiles = (rows + tme - 1) // tme
    nxt = jnp.minimum(e + 1, n_experts - 1)
    prefetch = (rows > 0) & (e + 1 < n_experts) & (erows_ref[nxt] > 0)

    def tile_rows(total, k):
        return pl.multiple_of(jnp.minimum(tme, total - k * tme), SUBLANES)

    def in_copy(start, r, slot):
        start = pl.multiple_of(start, SUBLANES)
        return pltpu.make_async_copy(xs_ref.at[pl.ds(start, r)], xbuf.at[slot, pl.ds(0, r)],
                                     sem_in.at[slot])

    def out_copy(start, r, slot):
        start = pl.multiple_of(start, SUBLANES)
        return pltpu.make_async_copy(ybuf.at[slot, pl.ds(0, r)], ys_ref.at[pl.ds(start, r)],
                                     sem_out.at[slot])

    @pl.when(e == 0)
    def _():
        for s in range(4):
            state[s] = 0
        xbuf[...] = jnp.zeros_like(xbuf)

    @pl.when(rows > 0)
    def _():
        wg_bf[...] = wg_ref[0].astype(BF16)
        wu_bf[...] = wu_ref[0].astype(BF16)
        wd_bf[...] = wd_ref[0].astype(BF16)
        done = state[0]

        @pl.when(state[1] == 0)
        def _():
            in_copy(off, tile_rows(rows, 0), done % 2).start()

        def tile(k, c):
            slot = (done + k) % 2
            r = tile_rows(rows, k)

            @pl.when(k + 1 < n_tiles)
            def _():
                in_copy(off + (k + 1) * tme, tile_rows(rows, k + 1), 1 - slot).start()

            @pl.when((k + 1 == n_tiles) & prefetch)
            def _():
                in_copy(eoff_ref[nxt], tile_rows(erows_ref[nxt], 0), 1 - slot).start()

            in_copy(off, r, slot).wait()

            @pl.when(done + k >= 2)
            def _():
                out_copy(0, pl.multiple_of(state[2 + slot], SUBLANES), slot).wait()

            def mlp(start, n):
                x = jnp.concatenate(_unpack_halves(xbuf[slot, start:start + n]), axis=1)
                hg = jnp.dot(x, wg_bf[...], preferred_element_type=F32)
                hu = jnp.dot(x, wu_bf[...], preferred_element_type=F32)
                act = (hg * jax.nn.sigmoid(hg) * hu).astype(BF16)
                y = jnp.dot(act, wd_bf[...], preferred_element_type=F32)
                ybuf[slot, start:start + n] = _pack_halves(y.astype(BF16).astype(F32))

            for below, n in zip((0,) + EXPERT_TILE_SIZES, EXPERT_TILE_SIZES):
                @pl.when((r > below) & (r <= n))
                def _(n=n):
                    mlp(0, n)

            out_copy(off + k * tme, r, slot).start()
            state[2 + slot] = r
            return c

        lax.fori_loop(0, n_tiles, tile, 0)
        state[0] = done + n_tiles

    state[1] = prefetch.astype(jnp.int32)

    @pl.when(e == n_experts - 1)
    def _():
        total = eoff_ref[e] + rows
        for slot in range(2):
            @pl.when(state[0] > slot)
            def _(slot=slot):
                out_copy(0, pl.multiple_of(state[2 + slot], SUBLANES), slot).wait()
        ybuf[0] = jnp.zeros(ybuf.shape[1:], ybuf.dtype)
        spare = ys_ref.shape[0] - total
        n_fill = (spare + tme - 1) // tme

        def fill_copy(k):
            return out_copy(total + k * tme, tile_rows(spare, k), 0)

        def fill_start(k, c):
            fill_copy(k).start()
            return c

        def fill_wait(k, c):
            fill_copy(k).wait()
            return c

        lax.fori_loop(0, n_fill, fill_start, 0)
        lax.fori_loop(0, n_fill, fill_wait, 0)


def _experts(eoff, erows, xs, wg, wu, wd, tme):
    p = xs.shape[0]
    n_experts, d, de = wg.shape
    wmap = lambda e, *_: (e, 0, 0)
    return pl.pallas_call(
        functools.partial(_experts_kernel, tme=tme),
        grid_spec=pltpu.PrefetchScalarGridSpec(
            num_scalar_prefetch=2,
            grid=(n_experts,),
            in_specs=[pl.BlockSpec(memory_space=pl.ANY),
                      pl.BlockSpec((1, d, de), wmap), pl.BlockSpec((1, d, de), wmap),
                      pl.BlockSpec((1, de, d), wmap)],
            out_specs=pl.BlockSpec(memory_space=pl.ANY),
            scratch_shapes=[pltpu.VMEM((d, de), BF16), pltpu.VMEM((d, de), BF16),
                            pltpu.VMEM((de, d), BF16),
                            pltpu.VMEM((2, tme, d // 2), jnp.int32),
                            pltpu.VMEM((2, tme, d // 2), jnp.int32),
                            pltpu.SemaphoreType.DMA((2,)), pltpu.SemaphoreType.DMA((2,)),
                            pltpu.SMEM((4,), jnp.int32)]),
        out_shape=jax.ShapeDtypeStruct((p, d // 2), jnp.int32),
        compiler_params=_cparams(("arbitrary",)),
        name="experts",
    )(eoff, erows, xs, wg, wu, wd)


def _combine_kernel(c8_ref, loff_ref, goff_ref, tot_ref,
                    rwt_ref, nxt_ref, h_ref, fg_ref, ys_ref, y_ref, ybuf, sems, spread_ref,
                    *, tc, lrows):
    i = pl.program_id(0)
    slot = i % 2

    def gather_tile(tile, s):
        _segment_copies(tile, c8_ref, loff_ref, goff_ref,
                        lambda loc, glob: pltpu.make_async_copy(ys_ref.at[glob], ybuf.at[s, loc],
                                                                sems.at[s]))

    def spread(ref):
        w = ref[...]
        for c in range(4):
            spread_ref[c] = jnp.broadcast_to(w[:, c:c + 1], (tc, LANES))

    @pl.when(i == 0)
    def _():
        ybuf[...] = jnp.zeros_like(ybuf)
        gather_tile(0, 0)
        spread(rwt_ref)

    @pl.when(i + 1 < pl.num_programs(0))
    def _():
        gather_tile(i + 1, 1 - slot)

    rows = pl.multiple_of(tot_ref[i], SUBLANES)
    pltpu.make_async_copy(ys_ref.at[pl.ds(0, rows)], ybuf.at[slot, pl.ds(0, rows)],
                          sems.at[slot]).wait()

    across = lambda c: jnp.tile(spread_ref[c], (1, lrows // LANES))
    c_id = lax.broadcasted_iota(jnp.int32, (tc, lrows), 1)
    pos1 = across(2).astype(jnp.int32)
    pos2 = across(3).astype(jnp.int32)
    wmat = jnp.where(c_id == pos1, across(0), jnp.where(c_id == pos2, across(1), 0.0)).astype(BF16)
    half = h_ref.shape[1] // 2
    outs, sumsq = [], 0.0
    for part, cols in zip(_unpack_halves(ybuf[slot]), (slice(0, half), slice(half, 2 * half))):
        out = h_ref[:, cols] + jnp.dot(wmat, part, preferred_element_type=F32)
        if not outs:
            spread(nxt_ref)
        sumsq = sumsq + jnp.sum(out * out, axis=-1, keepdims=True)
        outs.append(out)
    scale = lax.rsqrt(sumsq * (1.0 / (2 * half)) + EPS)
    for out, cols in zip(outs, (slice(0, half), slice(half, 2 * half))):
        y_ref[:, cols] = out * scale * fg_ref[:, cols]


def _combine(c8, loff, goff, tot, rwt, h, final_g, ys, tc, lrows):
    n, d = h.shape
    n_tiles = n // tc
    return pl.pallas_call(
        functools.partial(_combine_kernel, tc=tc, lrows=lrows),
        grid_spec=pltpu.PrefetchScalarGridSpec(
            num_scalar_prefetch=4,
            grid=(n_tiles,),
            in_specs=[pl.BlockSpec((tc, LANES), lambda i, *_: (0, 0)),
                      pl.BlockSpec((tc, LANES), lambda i, *_: (jnp.minimum(i + 1, n_tiles - 1), 0)),
                      pl.BlockSpec((tc, d), lambda i, *_: (i, 0)),
                      pl.BlockSpec((1, d), lambda i, *_: (0, 0)),
                      pl.BlockSpec(memory_space=pl.ANY)],
            out_specs=pl.BlockSpec((tc, d), lambda i, *_: (i, 0)),
            scratch_shapes=[pltpu.VMEM((2, lrows, d // 2), jnp.int32),
                            pltpu.SemaphoreType.DMA((2,)),
                            pltpu.VMEM((4, tc, LANES), F32)]),
        out_shape=jax.ShapeDtypeStruct((n, d), F32),
        compiler_params=_cparams(("arbitrary",)),
        name="combine",
    )(c8, loff, goff, tot, rwt, rwt, h, final_g, ys)


def _router_tables(w_group, b_group, w_fine, b_fine):
    d = w_group.shape[0]
    pad_g, pad_f = SUBLANES - N_GROUPS, ROUTER_ROWS - SUBLANES - N_EXPERTS
    w = jnp.concatenate([w_group.T, jnp.zeros((pad_g, d), F32), w_fine.T,
                         jnp.zeros((pad_f, d), F32)], axis=0)
    b = jnp.concatenate([b_group, jnp.full((pad_g,), NEG_BIG, F32), b_fine,
                         jnp.full((pad_f,), NEG_BIG, F32)])
    w_hi = w.astype(BF16)
    w_lo = (w - w_hi.astype(F32)).astype(BF16)
    return jnp.concatenate([w_hi, w_lo], axis=0), b.reshape(ROUTER_ROWS, 1)


def kernel(x, norm1_g, w_in, conv_w, conv_b, w_rgate, b_rgate, w_igate, b_igate, lam, sb_norm_g,
           lru_norm_g, w_out, norm2_g, w_group, b_group, w_fine, b_fine, w_e_gate, w_e_up,
           w_e_down, final_g):
    batch, seq, d = x.shape
    n = batch * seq
    width = w_in.shape[1] // 5
    tm = min(TOKEN_TILE, seq)
    tp = min(ROW_TILE, seq)
    tme = EXPERT_TILE_ROWS

    x2 = x.reshape(n, d)
    vec = lambda a: a.reshape(1, -1)

    q, k, v, xl, gl = _in_proj(x2, vec(norm1_g), w_in, width, tp)
    out_sb = _attention(q, k, v, batch, seq)
    out_lru = _lru(xl, gl, conv_w, vec(conv_b), w_rgate, vec(b_rgate), w_igate, vec(b_igate),
                   vec(lam), batch, seq, tp)

    wr_stack, rbias = _router_tables(w_group, b_group, w_fine, b_fine)
    h, u2, rr, rwt, tcnt = _mix_route(out_sb, out_lru, x2, vec(sb_norm_g), vec(lru_norm_g),
                                      w_out, vec(norm2_g), wr_stack, rbias, tm, tp)

    n_tiles = n // tm
    assert n_tiles <= LANES, "one lane of the per-tile count table per token tile"
    c8 = tcnt[:, :n_tiles].T
    erows = jnp.sum(c8, axis=0)
    eoff = jnp.cumsum(erows) - erows
    goff = eoff[None, :] + jnp.cumsum(c8, axis=0) - c8
    loff = jnp.cumsum(c8, axis=1) - c8
    tot = jnp.sum(c8, axis=1)
    lrows = 2 * tm + N_EXPERTS * SUBLANES
    p_rows = 2 * n + n_tiles * N_EXPERTS * (SUBLANES - 1)
    p_rows = -(-p_rows // SUBLANES) * SUBLANES
    i32 = lambda a: a.reshape(-1).astype(jnp.int32)
    c8, loff, goff, tot, eoff, erows = (i32(a) for a in (c8, loff, goff, tot, eoff, erows))

    xs = _dispatch(c8, loff, goff, tot, jnp.sum(erows, keepdims=True), rr, u2, p_rows, tm, lrows, tme)
    ys = _experts(eoff, erows, xs, w_e_gate, w_e_up, w_e_down, tme)
    y = _combine(c8, loff, goff, tot, rwt, h, vec(final_g), ys, tm, lrows)
    return y.reshape(batch, seq, d)
```

```python
import functools
import math

import jax
import jax.numpy as jnp
from jax import lax
from jax.experimental import pallas as pl
from jax.experimental.pallas import tpu as pltpu

F32 = jnp.float32
BF16 = jnp.bfloat16

EPS = 1e-6
HEAD_DIM = 64
HEADS_PER_BLOCK = 2
LANES = 128
SUBLANES = 8
CONV_W = 4
RG_C = 8.0
N_GROUPS = 4
EXPERTS_PER_GROUP = 8
N_EXPERTS = N_GROUPS * EXPERTS_PER_GROUP
ROUTER_ROWS = 48
NEG_BIG = -1e30
LOG2_E = math.log2(math.e)
ATTN_STOP = 104.0 * LOG2_E
ATTN_QUERY_ROWS = 64
ATTN_WINDOW_BLOCKS = 2
ATTN_UNROLL = 24
ATTN_STAGE_LAG = 2
HIGH_HALF = -65536
EXPERT_TILE_ROWS = 1536
EXPERT_TILE_SIZES = (256, 512, 1024, 1152, 1280, 1536)
TOKEN_TILE = 512
ROW_TILE = 1024

VMEM_LIMIT = 56 * 1024 * 1024


def _cparams(sem):
    return pltpu.CompilerParams(dimension_semantics=sem, vmem_limit_bytes=VMEM_LIMIT)


def _rms_f32(x, g):
    return x * lax.rsqrt(jnp.mean(x * x, axis=-1, keepdims=True) + EPS) * g


def _in_proj_kernel(x_ref, g_ref, w_ref, q_ref, k_ref, v_ref, xl_ref, gl_ref, w_bf, *,
                    width, q_scale):
    @pl.when(pl.program_id(0) == 0)
    def _():
        for c in range(w_ref.shape[1] // width):
            cols = slice(c * width, (c + 1) * width)
            w_bf[:, cols] = w_ref[:, cols].astype(BF16)

    u = _rms_f32(x_ref[...], g_ref[...]).astype(BF16)
    for c, o_ref in enumerate((q_ref, k_ref, v_ref, xl_ref, gl_ref)):
        p = jnp.dot(u, w_bf[:, c * width:(c + 1) * width], preferred_element_type=F32)
        if c == 0:
            p = p * q_scale
        o_ref[...] = p.astype(o_ref.dtype)


def _in_proj(x2, g, w, width, tm):
    n, d = x2.shape
    row = lambda i: (i, 0)
    out_bf = jax.ShapeDtypeStruct((n, width), BF16)
    out_f = jax.ShapeDtypeStruct((n, width), F32)
    return pl.pallas_call(
        functools.partial(_in_proj_kernel, width=width, q_scale=1.0 / math.sqrt(HEAD_DIM)),
        grid=(n // tm,),
        in_specs=[pl.BlockSpec((tm, d), row),
                  pl.BlockSpec((1, d), lambda i: (0, 0)),
                  pl.BlockSpec(w.shape, lambda i: (0, 0), pipeline_mode=pl.Buffered(1))],
        out_specs=[pl.BlockSpec((tm, width), row)] * 5,
        out_shape=[out_bf, out_bf, out_bf, out_f, out_f],
        scratch_shapes=[pltpu.VMEM(w.shape, BF16)],
        compiler_params=_cparams(("arbitrary",)),
        name="in_proj",
    )(x2, g, w)


def _attn_kernel(q_ref, k_ref, v_ref, o_ref, tri_ref, z_ref, arg_ref, ctot_ref, acc_ref, carry_ref,
                 *, tq, kb, nsub, fill):
    seq = q_ref.shape[0]
    win = nsub * kb
    lookback = win - tq
    lane = lax.broadcasted_iota(jnp.int32, (1, LANES), 1)
    rel = (lax.broadcasted_iota(jnp.int32, (tq, kb), 1)
           - lax.broadcasted_iota(jnp.int32, (tq, kb), 0))
    rel = jnp.concatenate([rel] * HEADS_PER_BLOCK, axis=0)

    k_r = lax.broadcasted_iota(jnp.int32, (kb, 2 * kb), 0)
    k_c = lax.broadcasted_iota(jnp.int32, (kb, 2 * kb), 1)
    tri_ref[...] = jnp.where(k_c >= kb, 1.0, jnp.where(k_r > k_c, 1.0, 0.0)).astype(BF16)

    def softplus2(z):
        return jnp.maximum(z, 0.0) + jnp.log2(1.0 + jnp.exp2(-jnp.abs(z)))

    def scores(qh, keys):
        z = LOG2_E * lax.dot_general(qh, keys, (((1,), (1,)), ((), ())),
                                     preferred_element_type=F32)
        nlog_nb = softplus2(z)
        return nlog_nb, z - nlog_nb

    def suffix(nlog_nb):
        r = jnp.dot(nlog_nb.astype(BF16), tri_ref[...], preferred_element_type=F32)
        return r[:, :kb], r[:, kb:]

    def stacked_queries(i):
        q = q_ref[pl.ds(i * tq, tq), :]
        return jnp.concatenate(
            [jnp.where((lane >= h * HEAD_DIM) & (lane < (h + 1) * HEAD_DIM), q, jnp.zeros_like(q))
             for h in range(HEADS_PER_BLOCK)], axis=0)

    def store(i, out):
        o_ref[pl.ds(i * tq, tq), :] = jnp.where(lane < HEAD_DIM, out[0:tq], out[tq:2 * tq])

    def window_start(i):
        if isinstance(i, int):
            return max(i * tq - lookback, 0)
        return pl.multiple_of(i * tq - lookback, tq)

    def stage_scores(i, p):
        keys = k_ref[pl.ds(window_start(i), win), :]
        z_ref[p] = LOG2_E * lax.dot_general(stacked_queries(i), keys, (((1,), (1,)), ((), ())),
                                            preferred_element_type=F32)

    def stage_exponents(p, delta):
        z = z_ref[p]
        softplus = softplus2(z)
        carry = None
        for b in reversed(range(nsub)):
            cols = slice(b * kb, (b + 1) * kb)
            masked = (b + 1) * kb > delta
            valid = (rel + b * kb) < delta
            nl = softplus[:, cols]
            if masked:
                nl = jnp.where(valid, nl, 0.0)
            excl, tot = suffix(nl)
            arg = z[:, cols] - softplus[:, cols] - excl
            if carry is not None:
                arg = arg - carry
            if masked:
                arg = jnp.where(valid, arg, NEG_BIG)
            arg_ref[p, :, cols] = arg
            carry = tot if carry is None else carry + tot
        ctot_ref[p] = carry
        return jnp.min(carry)

    def stage_output(i, p, s):
        vals = v_ref[pl.ds(window_start(i), win), :]
        acc_ref[s] = jnp.dot(jnp.exp2(arg_ref[p]).astype(BF16), vals, preferred_element_type=F32)
        carry_ref[s] = ctot_ref[p]

    def finish(i, s, cmin):
        def cond(state):
            pos, cmin = state
            return (pos > -kb) & (cmin <= ATTN_STOP)

        def older(state):
            pos, _ = state
            start = pl.multiple_of(jnp.maximum(pos, 0), tq)
            keys = k_ref[pl.ds(start, kb), :]
            vals = v_ref[pl.ds(start, kb), :]
            fresh = lax.broadcasted_iota(jnp.int32, (1, kb), 1) < pos + kb - start
            nlog_nb, log_b = scores(stacked_queries(i), keys)
            nlog_nb = jnp.where(fresh, nlog_nb, 0.0)
            excl, tot = suffix(nlog_nb)
            carry = carry_ref[s]
            a = jnp.where(fresh, jnp.exp2(log_b - excl - carry), 0.0)
            acc_ref[s] += jnp.dot(a.astype(BF16), vals, preferred_element_type=F32)
            carry_ref[s] = carry + tot
            return pos - kb, jnp.min(carry + tot)

        lax.while_loop(cond, older, (jnp.asarray(window_start(i) - kb, jnp.int32), cmin))
        store(i, acc_ref[s])

    n_blocks = seq // tq
    unroll = acc_ref.shape[0]
    lag = z_ref.shape[0] - 1
    n_slots = lag + 1
    pending = []
    for tau in range(fill):
        stage_scores(tau, tau % n_slots)
        if tau >= lag:
            j = tau - lag
            pending.append(stage_exponents(j % n_slots, j * tq - window_start(j)))
        if tau >= 2 * lag:
            j = tau - 2 * lag
            stage_output(j, j % n_slots, 0)
            finish(j, 0, pending.pop(0))

    def steady(m, pending):
        pending = list(pending)
        tau0 = fill + unroll * m
        done = []
        for u in range(unroll):
            stage_output(tau0 + u - 2 * lag, (fill - 2 * lag + u) % n_slots, u)
            done.append((tau0 + u - 2 * lag, u, pending.pop(0)))
            pending.append(stage_exponents((fill - lag + u) % n_slots, lookback))
            stage_scores(tau0 + u, (fill + u) % n_slots)
        for block, s, cmin in done:
            finish(block, s, cmin)
        return tuple(pending)

    pending = list(lax.fori_loop(0, (n_blocks - fill) // unroll, steady, tuple(pending)))
    for tau in range(n_blocks, n_blocks + 2 * lag):
        j = tau - 2 * lag
        stage_output(j, j % n_slots, 0)
        cmin = pending.pop(0)
        if tau - lag < n_blocks:
            pending.append(stage_exponents((tau - lag) % n_slots, lookback))
        finish(j, 0, cmin)


def _attention(q, k, v, batch, seq):
    n, width = q.shape
    tq, kb, nsub = ATTN_QUERY_ROWS, LANES, ATTN_WINDOW_BLOCKS
    n_blocks = seq // tq
    lag, n_slots = ATTN_STAGE_LAG, ATTN_STAGE_LAG + 1
    clipped = -(-(nsub * kb - tq) // tq)
    min_fill = clipped + 2 * lag
    assert seq % tq == 0 and seq >= nsub * kb and n_blocks >= min_fill
    options = [(f + 2 * (n_blocks - f) // u, u, f)
               for u in range(ATTN_UNROLL - ATTN_UNROLL % n_slots, 0, -n_slots)
               for f in range(min_fill, n_blocks + 1) if (n_blocks - f) % u == 0]
    _, unroll, fill = min(options)
    blk = pl.BlockSpec((seq, LANES), lambda b, hp: (b, hp))
    rows = HEADS_PER_BLOCK * tq
    stage_buf = pltpu.VMEM((n_slots, rows, nsub * kb), F32)
    carry_buf = pltpu.VMEM((n_slots, rows, kb), F32)
    row_buf = pltpu.VMEM((unroll, rows, LANES), F32)
    return pl.pallas_call(
        functools.partial(_attn_kernel, tq=tq, kb=kb, nsub=nsub, fill=fill),
        grid=(batch, width // LANES),
        in_specs=[blk, blk, blk],
        out_specs=blk,
        out_shape=jax.ShapeDtypeStruct((n, width), F32),
        scratch_shapes=[pltpu.VMEM((kb, 2 * kb), BF16),
                        stage_buf, stage_buf, carry_buf, row_buf, row_buf],
        compiler_params=_cparams(("arbitrary", "arbitrary")),
        name="attn",
    )(q, k, v)


def _gelu_tanh(x):
    c = math.sqrt(2.0 / math.pi)
    half_x = 0.5 * x
    return half_x + half_x * jnp.tanh(x * (c + (c * 0.044715) * (x * x)))


def _sigmoid(x):
    return 0.5 + 0.5 * jnp.tanh(0.5 * x)


def _lru_kernel(*refs, ts, n_slab):
    xl_refs, gl_refs = refs[0:n_slab], refs[n_slab:2 * n_slab]
    cw_ref, cb_ref, wr_ref, br_ref, wi_ref, bi_ref, lam_ref = refs[2 * n_slab:2 * n_slab + 7]
    o_refs = refs[2 * n_slab + 7:3 * n_slab + 7]
    (tail_ref, a7_ref, u7_ref, hp_ref, pa_ref, pu_ref, h_ref,
     wr_bd, wi_bd) = refs[3 * n_slab + 7:]
    t = pl.program_id(1)
    groups = ts // SUBLANES

    @pl.when(t == 0)
    def _():
        tail_ref[...] = jnp.zeros_like(tail_ref)
        h_ref[...] = jnp.zeros_like(h_ref)
        per = LANES // wr_ref.shape[1]
        for src, dst in ((wr_ref, wr_bd), (wi_ref, wi_bd)):
            for c in range(n_slab):
                rows = []
                for p in range(per):
                    blk = src[c * per + p]
                    rows.append(jnp.concatenate(
                        [blk if q == p else jnp.zeros_like(blk) for q in range(per)], axis=1))
                dst[c] = jnp.concatenate(rows, axis=0).astype(BF16)

    first_group = lax.broadcasted_iota(jnp.int32, (groups, LANES), 0) == 0
    for c in range(n_slab):
        lanes = slice(c * LANES, (c + 1) * LANES)
        x = [xl_refs[c][pl.ds(s, groups, stride=SUBLANES), :] for s in range(SUBLANES)]
        shifted = {}
        for s in range(SUBLANES - (CONV_W - 1), SUBLANES):
            shifted[s] = jnp.where(first_group, tail_ref[c, s:s + 1, :], pltpu.roll(x[s], 1, axis=0))
            tail_ref[c, s:s + 1, :] = x[s][groups - 1:groups, :]
        conv = []
        for s in range(SUBLANES):
            y = cb_ref[:, lanes]
            for w in range(CONV_W):
                j = s - (CONV_W - 1) + w
                y = y + (x[j] if j >= 0 else shifted[j + SUBLANES]) * cw_ref[w:w + 1, lanes]
            conv.append(y)
        xc = jnp.concatenate(conv, axis=0)

        xcb = xc.astype(BF16)
        r = _sigmoid(jnp.dot(xcb, wr_bd[c], preferred_element_type=F32) + br_ref[:, lanes])
        ig = _sigmoid(jnp.dot(xcb, wi_bd[c], preferred_element_type=F32) + bi_ref[:, lanes])
        lam = lam_ref[:, lanes]
        log_sig_lam = -(jnp.maximum(-lam, 0.0) + jnp.log1p(jnp.exp(-jnp.abs(lam))))
        log_a = r * (RG_C * log_sig_lam)
        a = jnp.exp(log_a)
        th = jnp.tanh(log_a)
        one_m_a2 = -2.0 * th / (1.0 - th)
        root = jnp.where(one_m_a2 > 0.0, one_m_a2 * lax.rsqrt(one_m_a2), 0.0)
        u = root * (ig * xc)

        a_run = u_run = None
        for s in range(SUBLANES):
            rows = slice(s * groups, (s + 1) * groups)
            if s == 0:
                a_run, u_run = a[rows], u[rows]
            else:
                u_run = a[rows] * u_run + u[rows]
                a_run = a_run * a[rows]
            pa_ref[c, rows, :] = a_run
            pu_ref[c, rows, :] = u_run
        a7_ref[c] = a_run
        u7_ref[c] = u_run

    def group(g, hs):
        nxt = []
        for c in range(n_slab):
            hp_ref[c, pl.ds(g, 1), :] = hs[c]
            nxt.append(a7_ref[c, pl.ds(g, 1), :] * hs[c] + u7_ref[c, pl.ds(g, 1), :])
        return tuple(nxt)

    hs = lax.fori_loop(0, groups, group, tuple(h_ref[c] for c in range(n_slab)), unroll=8)
    for c in range(n_slab):
        h_ref[c] = hs[c]

    for c in range(n_slab):
        h_in = hp_ref[c]
        for s in range(SUBLANES):
            rows = slice(s * groups, (s + 1) * groups)
            hseq = pu_ref[c, rows, :] + pa_ref[c, rows, :] * h_in
            gate = _gelu_tanh(gl_refs[c][pl.ds(s, groups, stride=SUBLANES), :])
            o_refs[c][pl.ds(s, groups, stride=SUBLANES), :] = hseq * gate


def _lru(xl, gl, conv_w, conv_b, w_r, br, w_i, bi, lam, batch, seq, ts):
    n, width = xl.shape
    nt = seq // ts
    n_slab = width // LANES
    blocks, block_w, _ = w_r.shape
    assert blocks * block_w == width and LANES % block_w == 0 and ts % (SUBLANES * SUBLANES) == 0
    groups = ts // SUBLANES
    gate_bd = pltpu.VMEM((n_slab, LANES, LANES), BF16)
    slab = [pl.BlockSpec((ts, LANES), functools.partial(lambda b, t, c: (b * nt + t, c), c=c))
            for c in range(n_slab)]
    const2 = lambda b, t: (0, 0)
    const3 = lambda b, t: (0, 0, 0)
    vec = pl.BlockSpec((1, width), const2)
    per_group = pltpu.VMEM((n_slab, groups, LANES), F32)
    per_step = pltpu.VMEM((n_slab, ts, LANES), F32)
    return pl.pallas_call(
        functools.partial(_lru_kernel, ts=ts, n_slab=n_slab),
        grid=(batch, nt),
        in_specs=slab + slab + [pl.BlockSpec((CONV_W, width), const2), vec,
                                pl.BlockSpec(w_r.shape, const3), vec,
                                pl.BlockSpec(w_i.shape, const3), vec, vec],
        out_specs=[pl.BlockSpec((ts, LANES), lambda b, t: (b * nt + t, 0))] * n_slab,
        out_shape=[jax.ShapeDtypeStruct((n, LANES), F32)] * n_slab,
        scratch_shapes=[pltpu.VMEM((n_slab, SUBLANES, LANES), F32),
                        per_group, per_group, per_group, per_step, per_step,
                        pltpu.VMEM((n_slab, 1, LANES), F32), gate_bd, gate_bd],
        compiler_params=_cparams(("arbitrary", "arbitrary")),
        name="lru",
    )(*([xl] * n_slab), *([gl] * n_slab), conv_w, conv_b, w_r, br, w_i, bi, lam)


def _mix_route_kernel(sb_ref, *refs, tm, n_slab):
    lru_refs = refs[:n_slab]
    (x_ref, sbg_ref, lrug_ref, wo_ref, n2g_ref, wrs_ref, rb_ref,
     h_ref, u2_ref, rr_ref, rwt_ref, tc_ref, before_ref, u2s_ref, wo_bf) = refs[n_slab:]

    @pl.when(pl.program_id(0) == 0)
    def _():
        wo_bf[...] = wo_ref[...].astype(BF16)

    _mix_route_body(sb_ref, lru_refs, x_ref, sbg_ref, lrug_ref, wo_bf, n2g_ref, wrs_ref, rb_ref,
                    h_ref, u2_ref, rr_ref, rwt_ref, tc_ref, before_ref, u2s_ref, tm)


def _mix_route_body(sb_ref, lru_refs, x_ref, sbg_ref, lrug_ref, wo_ref, n2g_ref, wrs_ref, rb_ref,
                    h_ref, u2_ref, rr_ref, rwt_ref, tc_ref, before_ref, u2s_ref, tm):
    step = pl.program_id(0)
    half = sb_ref.shape[1]
    n_sub = x_ref.shape[0] // tm

    @pl.when(step == 0)
    def _():
        r_id = lax.broadcasted_iota(jnp.int32, (tm, tm), 0)
        c_id = lax.broadcasted_iota(jnp.int32, (tm, tm), 1)
        before_ref[...] = (r_id < c_id).astype(BF16)
        tc_ref[...] = jnp.zeros_like(tc_ref)
        u2s_ref[...] = jnp.zeros_like(u2s_ref)

    def route(j):
        rows = slice(j * tm, (j + 1) * tm)
        u2 = u2s_ref[rows, :]
        u_hi = u2.astype(BF16)
        u_lo = (u2 - u_hi.astype(F32)).astype(BF16)
        nt_dims = (((1,), (1,)), ((), ()))
        n_rows = rb_ref.shape[0]
        both = lax.dot_general(wrs_ref[...], u_hi, nt_dims, preferred_element_type=F32)
        lt = (both[0:n_rows] + both[n_rows:2 * n_rows]
              + lax.dot_general(wrs_ref[0:n_rows, :], u_lo, nt_dims, preferred_element_type=F32)
              + rb_ref[...])

        sub = lax.broadcasted_iota(jnp.int32, (SUBLANES, tm), 0)

        def top1(x):
            m = jnp.max(x, axis=0, keepdims=True)
            idx = jnp.min(jnp.where(x == m, sub, SUBLANES), axis=0, keepdims=True)
            return m, idx

        grp = lt[0:SUBLANES, :]
        g_max, g_idx = top1(grp)
        g_p = 1.0 / jnp.sum(jnp.exp(grp - g_max), axis=0, keepdims=True)
        fine = lt[SUBLANES:2 * SUBLANES, :]
        for g in range(1, N_GROUPS):
            fine = jnp.where(g_idx == g, lt[(g + 1) * SUBLANES:(g + 2) * SUBLANES, :], fine)
        m1, i1 = top1(fine)
        m2, i2 = top1(jnp.where(sub == i1, -jnp.inf, fine))
        e2 = jnp.exp(m2 - m1)
        p1 = 1.0 / (1.0 + e2)
        w1 = g_p * p1
        w2 = g_p * (e2 * p1)
        x1 = g_idx * EXPERTS_PER_GROUP + i1
        x2 = g_idx * EXPERTS_PER_GROUP + i2

        eid = lax.broadcasted_iota(jnp.int32, (N_EXPERTS, tm), 0)
        oh1 = jnp.where(eid == x1, 1.0, 0.0)
        oh2 = jnp.where(eid == x2, 1.0, 0.0)
        pre = jnp.dot(jnp.concatenate([oh1, oh2], axis=0).astype(BF16), before_ref[...],
                      preferred_element_type=F32)
        pre1, pre2 = pre[0:N_EXPERTS], pre[N_EXPERTS:2 * N_EXPERTS]
        cnt1 = jnp.sum(oh1, axis=1, keepdims=True)
        cnt2 = jnp.sum(oh2, axis=1, keepdims=True)
        seg8 = jnp.floor((cnt1 + cnt2 + (SUBLANES - 1.0)) * (1.0 / SUBLANES))
        e_r = lax.broadcasted_iota(jnp.int32, (N_EXPERTS, N_EXPERTS), 0)
        e_c = lax.broadcasted_iota(jnp.int32, (N_EXPERTS, N_EXPERTS), 1)
        lower = jnp.where(e_c < e_r, 1.0, 0.0).astype(BF16)
        seg8_b = jnp.broadcast_to(seg8, (N_EXPERTS, LANES)).astype(BF16)
        seg_off = SUBLANES * jnp.dot(lower, seg8_b, preferred_element_type=F32)[:, 0:1]
        pos1 = jnp.sum(oh1 * (pre1 + seg_off), axis=0, keepdims=True)
        pos2 = jnp.sum(oh2 * (pre2 + (seg_off + cnt1)), axis=0, keepdims=True)

        lane = lax.broadcasted_iota(jnp.int32, tc_ref.shape, 1)
        seg_rows = jnp.broadcast_to(seg8 * SUBLANES, tc_ref.shape).astype(jnp.int32)
        tile = (step - 1) * n_sub + j
        tc_ref[...] = jnp.where(lane == tile, seg_rows, tc_ref[...])

        zrow = jnp.zeros((SUBLANES - 4, tm), jnp.int32)
        rr_ref[:, rows] = jnp.concatenate(
            [pos1.astype(jnp.int32), pos2.astype(jnp.int32), x1, x2, zrow], axis=0)
        wt = jnp.concatenate([w1, w2, pos1, pos2, jnp.zeros((LANES - 4, tm), F32)], axis=0)
        rwt_ref[rows, :] = wt.T

    m_sb = _rms_f32(sb_ref[...], sbg_ref[...]).astype(BF16)
    lru = jnp.concatenate([r[...] for r in lru_refs], axis=-1)
    m_lru = _rms_f32(lru, lrug_ref[...]).astype(BF16)
    h_ref[...] = (x_ref[...]
                  + jnp.dot(m_sb, wo_ref[0:half, :], preferred_element_type=F32)
                  + jnp.dot(m_lru, wo_ref[half:2 * half, :], preferred_element_type=F32))

    for j in range(n_sub):
        route(j)

    u2_next = _rms_f32(h_ref[...], n2g_ref[...])
    u2_ref[...] = u2_next.astype(BF16)
    u2s_ref[...] = u2_next


def _mix_route(sb, lru, x2, sbg, lrug, w_out, n2g, wr_stack, rbias, tm, tp):
    n, d = x2.shape
    half = sb.shape[1]
    n_tiles = n // tp
    row = lambda i: (jnp.minimum(i, n_tiles - 1), 0)
    routed = lambda i: (jnp.maximum(i - 1, 0), 0)
    const = lambda i: (0, 0)
    return pl.pallas_call(
        functools.partial(_mix_route_kernel, tm=tm, n_slab=len(lru)),
        grid=(n_tiles + 1,),
        in_specs=[pl.BlockSpec((tp, half), row)] + [pl.BlockSpec((tp, LANES), row)] * len(lru)
                 + [pl.BlockSpec((tp, d), row),
                  pl.BlockSpec((1, half), const), pl.BlockSpec((1, half), const),
                  pl.BlockSpec(w_out.shape, const, pipeline_mode=pl.Buffered(1)),
                  pl.BlockSpec((1, d), const),
                  pl.BlockSpec(wr_stack.shape, const), pl.BlockSpec(rbias.shape, const)],
        out_specs=[pl.BlockSpec((tp, d), row), pl.BlockSpec((tp, d), row),
                   pl.BlockSpec((SUBLANES, tp), lambda i: (0, jnp.maximum(i - 1, 0))),
                   pl.BlockSpec((tp, LANES), routed),
                   pl.BlockSpec((N_EXPERTS, LANES), const)],
        out_shape=[jax.ShapeDtypeStruct((n, d), F32), jax.ShapeDtypeStruct((n, d), BF16),
                   jax.ShapeDtypeStruct((SUBLANES, n), jnp.int32),
                   jax.ShapeDtypeStruct((n, LANES), F32),
                   jax.ShapeDtypeStruct((N_EXPERTS, LANES), jnp.int32)],
        scratch_shapes=[pltpu.VMEM((tm, tm), BF16), pltpu.VMEM((tp, d), F32),
                        pltpu.VMEM(w_out.shape, BF16)],
        compiler_params=_cparams(("arbitrary",)),
        name="mix_route",
    )(sb, *lru, x2, sbg, lrug, w_out, n2g, wr_stack, rbias)


def _pack_halves(x):
    half = x.shape[1] // 2
    lo = lax.shift_right_logical(lax.bitcast_convert_type(x[:, :half], jnp.int32), 16)
    hi = lax.bitcast_convert_type(x[:, half:], jnp.int32) & HIGH_HALF
    return hi | lo


def _unpack_halves(p):
    lo = lax.bitcast_convert_type(lax.shift_left(p, 16), F32)
    hi = lax.bitcast_convert_type(p & HIGH_HALF, F32)
    return lo.astype(BF16), hi.astype(BF16)


def _segment_copies(tile, c8_ref, loff_ref, goff_ref, make):
    for e in range(N_EXPERTS):
        idx = tile * N_EXPERTS + e
        rows = pl.multiple_of(c8_ref[idx], SUBLANES)

        @pl.when(rows > 0)
        def _(idx=idx, rows=rows):
            lo = pl.multiple_of(loff_ref[idx], SUBLANES)
            go = pl.multiple_of(goff_ref[idx], SUBLANES)
            make(pl.ds(lo, rows), pl.ds(go, rows)).start()


def _dispatch_kernel(c8_ref, loff_ref, goff_ref, tot_ref, used_ref,
                     rr_ref, u2_ref, xs_ref, lbuf, zbuf, sem, zsem, *, td, lrows):
    i = pl.program_id(0)
    slot = i % 2

    part = lrows // 2
    for r0 in range(0, lrows, part):
        r_id = r0 + lax.broadcasted_iota(jnp.int32, (part, td), 0)
        perm = jnp.where(r_id == rr_ref[0:1, :], 1.0, jnp.where(r_id == rr_ref[1:2, :], 1.0, 0.0))
        sorted_rows = jnp.dot(perm.astype(BF16), u2_ref[...], preferred_element_type=F32)
        lbuf[slot, r0:r0 + part] = _pack_halves(sorted_rows)

    @pl.when(i == 0)
    def _():
        zbuf[...] = jnp.zeros_like(zbuf)
        chunk = zbuf.shape[0]
        used = used_ref[0]
        spare = xs_ref.shape[0] - used
        n_fill = (spare + chunk - 1) // chunk

        def fill_copy(k):
            rows = pl.multiple_of(jnp.minimum(chunk, spare - k * chunk), SUBLANES)
            start = pl.multiple_of(used + k * chunk, SUBLANES)
            return pltpu.make_async_copy(zbuf.at[pl.ds(0, rows)], xs_ref.at[pl.ds(start, rows)],
                                         zsem)

        def fill_start(k, c):
            fill_copy(k).start()
            return c

        def fill_wait(k, c):
            fill_copy(k).wait()
            return c

        lax.fori_loop(0, n_fill, fill_start, 0)
        lax.fori_loop(0, n_fill, fill_wait, 0)

    def wait_tile(tile, s):
        rows = pl.multiple_of(tot_ref[tile], SUBLANES)
        pltpu.make_async_copy(lbuf.at[s, pl.ds(0, rows)], xs_ref.at[pl.ds(0, rows)], sem).wait()

    @pl.when(i > 0)
    def _():
        wait_tile(i - 1, 1 - slot)

    _segment_copies(i, c8_ref, loff_ref, goff_ref,
                    lambda loc, glob: pltpu.make_async_copy(lbuf.at[slot, loc], xs_ref.at[glob], sem))

    @pl.when(i == pl.num_programs(0) - 1)
    def _():
        wait_tile(i, slot)


def _dispatch(c8, loff, goff, tot, used, rr, u2, p_rows, td, lrows, tme):
    n, d = u2.shape
    pmap = lambda i, *_: (0, i)
    return pl.pallas_call(
        functools.partial(_dispatch_kernel, td=td, lrows=lrows),
        grid_spec=pltpu.PrefetchScalarGridSpec(
            num_scalar_prefetch=5,
            grid=(n // td,),
            in_specs=[pl.BlockSpec((SUBLANES, td), pmap),
                      pl.BlockSpec((td, d), lambda i, *_: (i, 0))],
            out_specs=pl.BlockSpec(memory_space=pl.ANY),
            scratch_shapes=[pltpu.VMEM((2, lrows, d // 2), jnp.int32),
                            pltpu.VMEM((tme, d // 2), jnp.int32),
                            pltpu.SemaphoreType.DMA(()), pltpu.SemaphoreType.DMA(())]),
        out_shape=jax.ShapeDtypeStruct((p_rows, d // 2), jnp.int32),
        compiler_params=_cparams(("arbitrary",)),
        name="dispatch",
    )(c8, loff, goff, tot, used, rr, u2)


def _experts_kernel(eoff_ref, erows_ref, xs_ref, wg_ref, wu_ref, wd_ref, ys_ref,
                    wg_bf, wu_bf, wd_bf, xbuf, ybuf, sem_in, sem_out, state, *, tme):
    e = pl.program_id(0)
    n_experts = pl.num_programs(0)
    rows = erows_ref[e]
    off = eoff_ref[e]
    n_tiles = (rows + tme - 1) // tme
    nxt = jnp.minimum(e + 1, n_experts - 1)
    prefetch = (rows > 0) & (e + 1 < n_experts) & (erows_ref[nxt] > 0)

    def tile_rows(total, k):
        return pl.multiple_of(jnp.minimum(tme, total - k * tme), SUBLANES)

    def in_copy(start, r, slot):
        start = pl.multiple_of(start, SUBLANES)
        return pltpu.make_async_copy(xs_ref.at[pl.ds(start, r)], xbuf.at[slot, pl.ds(0, r)],
                                     sem_in.at[slot])

    def out_copy(start, r, slot):
        start = pl.multiple_of(start, SUBLANES)
        return pltpu.make_async_copy(ybuf.at[slot, pl.ds(0, r)], ys_ref.at[pl.ds(start, r)],
                                     sem_out.at[slot])

    @pl.when(e == 0)
    def _():
        for s in range(4):
            state[s] = 0
        xbuf[...] = jnp.zeros_like(xbuf)

    @pl.when(rows > 0)
    def _():
        wg_bf[...] = wg_ref[0].astype(BF16)
        wu_bf[...] = wu_ref[0].astype(BF16)
        wd_bf[...] = wd_ref[0].astype(BF16)
        done = state[0]

        @pl.when(state[1] == 0)
        def _():
            in_copy(off, tile_rows(rows, 0), done % 2).start()

        def tile(k, c):
            slot = (done + k) % 2
            r = tile_rows(rows, k)

            @pl.when(k + 1 < n_tiles)
            def _():
                in_copy(off + (k + 1) * tme, tile_rows(rows, k + 1), 1 - slot).start()

            @pl.when((k + 1 == n_tiles) & prefetch)
            def _():
                in_copy(eoff_ref[nxt], tile_rows(erows_ref[nxt], 0), 1 - slot).start()

            in_copy(off, r, slot).wait()

            @pl.when(done + k >= 2)
            def _():
                out_copy(0, pl.multiple_of(state[2 + slot], SUBLANES), slot).wait()

            def mlp(start, n):
                x = jnp.concatenate(_unpack_halves(xbuf[slot, start:start + n]), axis=1)
                hg = jnp.dot(x, wg_bf[...], preferred_element_type=F32)
                hu = jnp.dot(x, wu_bf[...], preferred_element_type=F32)
                act = (hg * jax.nn.sigmoid(hg) * hu).astype(BF16)
                y = jnp.dot(act, wd_bf[...], preferred_element_type=F32)
                ybuf[slot, start:start + n] = _pack_halves(y.astype(BF16).astype(F32))

            for below, n in zip((0,) + EXPERT_TILE_SIZES, EXPERT_TILE_SIZES):
                @pl.when((r > below) & (r <= n))
                def _(n=n):
                    mlp(0, n)

            out_copy(off + k * tme, r, slot).start()
            state[2 + slot] = r
            return c

        lax.fori_loop(0, n_tiles, tile, 0)
        state[0] = done + n_tiles

    state[1] = prefetch.astype(jnp.int32)

    @pl.when(e == n_experts - 1)
    def _():
        total = eoff_ref[e] + rows
        for slot in range(2):
            @pl.when(state[0] > slot)
            def _(slot=slot):
                out_copy(0, pl.multiple_of(state[2 + slot], SUBLANES), slot).wait()
        ybuf[0] = jnp.zeros(ybuf.shape[1:], ybuf.dtype)
        spare = ys_ref.shape[0] - total
        n_fill = (spare + tme - 1) // tme

        def fill_copy(k):
            return out_copy(total + k * tme, tile_rows(spare, k), 0)

        def fill_start(k, c):
            fill_copy(k).start()
            return c

        def fill_wait(k, c):
            fill_copy(k).wait()
            return c

        lax.fori_loop(0, n_fill, fill_start, 0)
        lax.fori_loop(0, n_fill, fill_wait, 0)


def _experts(eoff, erows, xs, wg, wu, wd, tme):
    p = xs.shape[0]
    n_experts, d, de = wg.shape
    wmap = lambda e, *_: (e, 0, 0)
    return pl.pallas_call(
        functools.partial(_experts_kernel, tme=tme),
        grid_spec=pltpu.PrefetchScalarGridSpec(
            num_scalar_prefetch=2,
            grid=(n_experts,),
            in_specs=[pl.BlockSpec(memory_space=pl.ANY),
                      pl.BlockSpec((1, d, de), wmap), pl.BlockSpec((1, d, de), wmap),
                      pl.BlockSpec((1, de, d), wmap)],
            out_specs=pl.BlockSpec(memory_space=pl.ANY),
            scratch_shapes=[pltpu.VMEM((d, de), BF16), pltpu.VMEM((d, de), BF16),
                            pltpu.VMEM((de, d), BF16),
                            pltpu.VMEM((2, tme, d // 2), jnp.int32),
                            pltpu.VMEM((2, tme, d // 2), jnp.int32),
                            pltpu.SemaphoreType.DMA((2,)), pltpu.SemaphoreType.DMA((2,)),
                            pltpu.SMEM((4,), jnp.int32)]),
        out_shape=jax.ShapeDtypeStruct((p, d // 2), jnp.int32),
        compiler_params=_cparams(("arbitrary",)),
        name="experts",
    )(eoff, erows, xs, wg, wu, wd)


def _combine_kernel(c8_ref, loff_ref, goff_ref, tot_ref,
                    rwt_ref, nxt_ref, h_ref, fg_ref, ys_ref, y_ref, ybuf, sems, spread_ref,
                    *, tc, lrows):
    i = pl.program_id(0)
    slot = i % 2

    def gather_tile(tile, s):
        _segment_copies(tile, c8_ref, loff_ref, goff_ref,
                        lambda loc, glob: pltpu.make_async_copy(ys_ref.at[glob], ybuf.at[s, loc],
                                                                sems.at[s]))

    def spread(ref):
        w = ref[...]
        for c in range(4):
            spread_ref[c] = jnp.broadcast_to(w[:, c:c + 1], (tc, LANES))

    @pl.when(i == 0)
    def _():
        ybuf[...] = jnp.zeros_like(ybuf)
        gather_tile(0, 0)
        spread(rwt_ref)

    @pl.when(i + 1 < pl.num_programs(0))
    def _():
        gather_tile(i + 1, 1 - slot)

    rows = pl.multiple_of(tot_ref[i], SUBLANES)
    pltpu.make_async_copy(ys_ref.at[pl.ds(0, rows)], ybuf.at[slot, pl.ds(0, rows)],
                          sems.at[slot]).wait()

    across = lambda c: jnp.tile(spread_ref[c], (1, lrows // LANES))
    c_id = lax.broadcasted_iota(jnp.int32, (tc, lrows), 1)
    pos1 = across(2).astype(jnp.int32)
    pos2 = across(3).astype(jnp.int32)
    wmat = jnp.where(c_id == pos1, across(0), jnp.where(c_id == pos2, across(1), 0.0)).astype(BF16)
    half = h_ref.shape[1] // 2
    outs, sumsq = [], 0.0
    for part, cols in zip(_unpack_halves(ybuf[slot]), (slice(0, half), slice(half, 2 * half))):
        out = h_ref[:, cols] + jnp.dot(wmat, part, preferred_element_type=F32)
        if not outs:
            spread(nxt_ref)
        sumsq = sumsq + jnp.sum(out * out, axis=-1, keepdims=True)
        outs.append(out)
    scale = lax.rsqrt(sumsq * (1.0 / (2 * half)) + EPS)
    for out, cols in zip(outs, (slice(0, half), slice(half, 2 * half))):
        y_ref[:, cols] = out * scale * fg_ref[:, cols]


def _combine(c8, loff, goff, tot, rwt, h, final_g, ys, tc, lrows):
    n, d = h.shape
    n_tiles = n // tc
    return pl.pallas_call(
        functools.partial(_combine_kernel, tc=tc, lrows=lrows),
        grid_spec=pltpu.PrefetchScalarGridSpec(
            num_scalar_prefetch=4,
            grid=(n_tiles,),
            in_specs=[pl.BlockSpec((tc, LANES), lambda i, *_: (0, 0)),
                      pl.BlockSpec((tc, LANES), lambda i, *_: (jnp.minimum(i + 1, n_tiles - 1), 0)),
                      pl.BlockSpec((tc, d), lambda i, *_: (i, 0)),
                      pl.BlockSpec((1, d), lambda i, *_: (0, 0)),
                      pl.BlockSpec(memory_space=pl.ANY)],
            out_specs=pl.BlockSpec((tc, d), lambda i, *_: (i, 0)),
            scratch_shapes=[pltpu.VMEM((2, lrows, d // 2), jnp.int32),
                            pltpu.SemaphoreType.DMA((2,)),
                            pltpu.VMEM((4, tc, LANES), F32)]),
        out_shape=jax.ShapeDtypeStruct((n, d), F32),
        compiler_params=_cparams(("arbitrary",)),
        name="combine",
    )(c8, loff, goff, tot, rwt, rwt, h, final_g, ys)


def _router_tables(w_group, b_group, w_fine, b_fine):
    d = w_group.shape[0]
    pad_g, pad_f = SUBLANES - N_GROUPS, ROUTER_ROWS - SUBLANES - N_EXPERTS
    w = jnp.concatenate([w_group.T, jnp.zeros((pad_g, d), F32), w_fine.T,
                         jnp.zeros((pad_f, d), F32)], axis=0)
    b = jnp.concatenate([b_group, jnp.full((pad_g,), NEG_BIG, F32), b_fine,
                         jnp.full((pad_f,), NEG_BIG, F32)])
    w_hi = w.astype(BF16)
    w_lo = (w - w_hi.astype(F32)).astype(BF16)
    return jnp.concatenate([w_hi, w_lo], axis=0), b.reshape(ROUTER_ROWS, 1)


def kernel(x, norm1_g, w_in, conv_w, conv_b, w_rgate, b_rgate, w_igate, b_igate, lam, sb_norm_g,
           lru_norm_g, w_out, norm2_g, w_group, b_group, w_fine, b_fine, w_e_gate, w_e_up,
           w_e_down, final_g):
    batch, seq, d = x.shape
    n = batch * seq
    width = w_in.shape[1] // 5
    tm = min(TOKEN_TILE, seq)
    tp = min(ROW_TILE, seq)
    tme = EXPERT_TILE_ROWS

    x2 = x.reshape(n, d)
    vec = lambda a: a.reshape(1, -1)

    q, k, v, xl, gl = _in_proj(x2, vec(norm1_g), w_in, width, tp)
    out_sb = _attention(q, k, v, batch, seq)
    out_lru = _lru(xl, gl, conv_w, vec(conv_b), w_rgate, vec(b_rgate), w_igate, vec(b_igate),
                   vec(lam), batch, seq, tp)

    wr_stack, rbias = _router_tables(w_group, b_group, w_fine, b_fine)
    h, u2, rr, rwt, tcnt = _mix_route(out_sb, out_lru, x2, vec(sb_norm_g), vec(lru_norm_g),
                                      w_out, vec(norm2_g), wr_stack, rbias, tm, tp)

    n_tiles = n // tm
    assert n_tiles <= LANES, "one lane of the per-tile count table per token tile"
    c8 = tcnt[:, :n_tiles].T
    erows = jnp.sum(c8, axis=0)
    eoff = jnp.cumsum(erows) - erows
    goff = eoff[None, :] + jnp.cumsum(c8, axis=0) - c8
    loff = jnp.cumsum(c8, axis=1) - c8
    tot = jnp.sum(c8, axis=1)
    lrows = 2 * tm + N_EXPERTS * SUBLANES
    p_rows = 2 * n + n_tiles * N_EXPERTS * (SUBLANES - 1)
    p_rows = -(-p_rows // SUBLANES) * SUBLANES
    i32 = lambda a: a.reshape(-1).astype(jnp.int32)
    c8, loff, goff, tot, eoff, erows = (i32(a) for a in (c8, loff, goff, tot, eoff, erows))

    xs = _dispatch(c8, loff, goff, tot, jnp.sum(erows, keepdims=True), rr, u2, p_rows, tm, lrows, tme)
    ys = _experts(eoff, erows, xs, w_e_gate, w_e_up, w_e_down, tme)
    y = _combine(c8, loff, goff, tot, rwt, h, vec(final_g), ys, tm, lrows)
    return y.reshape(batch, seq, d)
```
